```python
import jax, jax.numpy as jnp
from jax import lax
import numpy as np

D_MODEL = 1024
BATCH = 8
SEQ = 2048
DEPTH = 4

CHUNK = 64
N_MIXERS = 3
N_GLA = (DEPTH + 2) // 3
N_MLA = (DEPTH + 1) // 3
N_CONV = DEPTH // 3
ALPHA = (2 * DEPTH) ** 0.25
BETA = (8 * DEPTH) ** -0.25
LN_EPS = 1e-5
RMS_EPS = 1e-6
PLE_DIM = 256
D_FF = 4 * D_MODEL
MAX_OFFSET = 4096

GLA_HEADS = 4
GLA_DK = D_MODEL // 2 // GLA_HEADS
GLA_DV = D_MODEL // GLA_HEADS
GLA_GATE_RANK = 16
GLA_TAU = 16.0
GLA_HK = GLA_HEADS * GLA_DK
GLA_HV = GLA_HEADS * GLA_DV
GLA_SPLITS = [GLA_HK, 2 * GLA_HK, 2 * GLA_HK + GLA_HV, 2 * GLA_HK + GLA_HV + D_MODEL]
GLA_IN = 2 * GLA_HK + GLA_HV + D_MODEL + GLA_GATE_RANK

MLA_HEADS = 8
MLA_NOPE = 128
MLA_ROPE = 64
MLA_V = 128
MLA_Q_RANK = 256
MLA_KV_RANK = 256
MLA_IN = MLA_Q_RANK + MLA_KV_RANK + MLA_ROPE
ROPE_BASE = 10000.0
Q_BLOCK = 128

CONV_WIDTH = 3

kernel_name = 'hybrid_gla_mla_shortconv_deepnorm_trunk'


def layer_norm(x, g, b):
    xf = x.astype(jnp.float32)
    mu = jnp.mean(xf, -1, keepdims=True)
    var = jnp.mean(jnp.square(xf - mu), -1, keepdims=True)
    return ((xf - mu) * lax.rsqrt(var + LN_EPS) * g + b).astype(x.dtype)


def rms_norm(x, g):
    xf = x.astype(jnp.float32)
    return (xf * lax.rsqrt(jnp.mean(xf * xf, -1, keepdims=True) + RMS_EPS) * g).astype(x.dtype)


def rope(x, cos, sin):
    x1, x2 = jnp.split(x, 2, axis=-1)
    return jnp.concatenate([x1 * cos - x2 * sin, x2 * cos + x1 * sin], axis=-1)


def gla_mixer(x, w_in, w_gate_up, b_gate, norm_g, w_out):
    B_, S_, _ = x.shape
    nc = S_ // CHUNK
    q, k, v, r, g_lr = jnp.split(x @ w_in, GLA_SPLITS, axis=-1)
    log_a = jax.nn.log_sigmoid((g_lr @ w_gate_up + b_gate).astype(jnp.float32)) / GLA_TAU

    def to_chunks(t, d):
        return t.astype(jnp.float32).reshape(B_, nc, CHUNK, GLA_HEADS, d).transpose(1, 0, 3, 2, 4)

    qc = to_chunks(q, GLA_DK) * GLA_DK ** -0.5
    kc = to_chunks(k, GLA_DK)
    vc = to_chunks(v, GLA_DV)
    lc = to_chunks(log_a, GLA_DK)

    def step(state, inp):
        q_, k_, v_, la = inp
        L = jnp.cumsum(la, axis=2)
        decay = jnp.exp(-jnp.abs(L[:, :, :, None, :] - L[:, :, None, :, :]))
        scores = jnp.einsum('bhtd,bhsd,bhtsd->bhts', q_, k_, decay)
        o = scores @ v_ + (q_ * jnp.exp(L)) @ state
        L_end = L[:, :, -1:, :]
        state = (jnp.exp(L_end[:, :, 0, :, None]) * state
                 + jnp.einsum('bhsd,bhse->bhde', k_ * jnp.exp(L_end - L), v_))
        return state, o

    s0 = jnp.zeros((B_, GLA_HEADS, GLA_DK, GLA_DV), jnp.float32)
    _, o = lax.scan(step, s0, (qc, kc, vc, lc))
    o = o.transpose(1, 0, 3, 2, 4).reshape(B_, S_, GLA_HEADS, GLA_DV)
    o = rms_norm(o, norm_g).reshape(B_, S_, GLA_HV) * jax.nn.silu(r.astype(jnp.float32))
    return o.astype(x.dtype) @ w_out


def mla_mixer(x, cos, sin, w_in, q_norm, kv_norm, w_uq, w_ukv, w_out):
    B_, S_, _ = x.shape
    c_q, c_kv, k_rope = jnp.split(x @ w_in, [MLA_Q_RANK, MLA_Q_RANK + MLA_KV_RANK], axis=-1)
    q = (rms_norm(c_q, q_norm) @ w_uq).reshape(B_, S_, MLA_HEADS, MLA_NOPE + MLA_ROPE)
    kv = (rms_norm(c_kv, kv_norm) @ w_ukv).reshape(B_, S_, MLA_HEADS, MLA_NOPE + MLA_V)
    q_nope, q_rope = jnp.split(q, [MLA_NOPE], axis=-1)
    k_nope, v = jnp.split(kv, [MLA_NOPE], axis=-1)
    q_rope = rope(q_rope, cos[:, :, None, :], sin[:, :, None, :])
    k_rope = rope(k_rope, cos, sin)
    qf = jnp.concatenate([q_nope.astype(jnp.float32), q_rope.astype(jnp.float32)], axis=-1)
    qf = qf * (MLA_NOPE + MLA_ROPE) ** -0.5
    kf = jnp.concatenate([k_nope.astype(jnp.float32),
                          jnp.broadcast_to(k_rope.astype(jnp.float32)[:, :, None, :],
                                           (B_, S_, MLA_HEADS, MLA_ROPE))], axis=-1)
    n_qb = S_ // Q_BLOCK
    q_blocks = qf.reshape(B_, n_qb, Q_BLOCK, MLA_HEADS, MLA_NOPE + MLA_ROPE).transpose(1, 0, 2, 3, 4)
    key_chunk = jnp.arange(S_) // CHUNK

    def attend(args):
        qb, bi = args
        q_chunk = (bi * Q_BLOCK + jnp.arange(Q_BLOCK)) // CHUNK
        s = jnp.einsum('bqhd,bkhd->bhqk', qb, kf)
        s = jnp.where(key_chunk[None, :] <= q_chunk[:, None], s, -jnp.inf)
        pr = jax.nn.softmax(s, axis=-1)
        return jnp.einsum('bhqk,bkhd->bqhd', pr.astype(v.dtype), v)

    o = lax.map(attend, (q_blocks, jnp.arange(n_qb)))
    o = o.transpose(1, 0, 2, 3, 4).reshape(B_, S_, MLA_HEADS * MLA_V)
    return o.astype(x.dtype) @ w_out


def conv_mixer(x, w_in, conv_w, w_out):
    b, c, u = jnp.split(x @ w_in, 3, axis=-1)
    z = lax.conv_general_dilated(c * u, conv_w[:, None, :], window_strides=(1,),
                                 padding=[(CONV_WIDTH - 1, 0)],
                                 dimension_numbers=('NWC', 'WIO', 'NWC'),
                                 feature_group_count=D_MODEL)
    return (b * z) @ w_out


def sq_relu_mlp(x, w1, w2):
    return jnp.square(jax.nn.relu(x @ w1)) @ w2


def _fwd_setup_inputs(seed: int = 0) -> dict:
    key = jax.random.key(seed)
    ks = jax.random.split(key, 24)

    def nrm(i, shape, scale):
        return jax.random.normal(ks[i], shape, jnp.float32) * scale

    x = nrm(0, (BATCH, SEQ, D_MODEL), 1.0)
    p = nrm(1, (DEPTH, BATCH, SEQ, PLE_DIM), 1.0)
    offsets = jax.random.randint(ks[2], (BATCH, 1), 0, MAX_OFFSET, dtype=jnp.int32)
    positions = (offsets + jnp.arange(SEQ, dtype=jnp.int32)[None, :]).astype(jnp.int32)
    return {
        'x': x,
        'p': p,
        'positions': positions,
        'gla_w_in': nrm(3, (N_GLA, D_MODEL, GLA_IN), D_MODEL ** -0.5),
        'gla_w_gate_up': nrm(4, (N_GLA, GLA_GATE_RANK, GLA_HK), GLA_GATE_RANK ** -0.5),
        'gla_b_gate': nrm(5, (N_GLA, GLA_HK), 0.1),
        'gla_norm_g': 1.0 + nrm(6, (N_GLA, GLA_DV), 0.01),
        'gla_w_out': nrm(7, (N_GLA, GLA_HV, D_MODEL), GLA_HV ** -0.5 * BETA),
        'mla_w_in': nrm(8, (N_MLA, D_MODEL, MLA_IN), D_MODEL ** -0.5),
        'mla_q_norm': 1.0 + nrm(9, (N_MLA, MLA_Q_RANK), 0.01),
        'mla_kv_norm': 1.0 + nrm(10, (N_MLA, MLA_KV_RANK), 0.01),
        'mla_w_uq': nrm(11, (N_MLA, MLA_Q_RANK, MLA_HEADS * (MLA_NOPE + MLA_ROPE)), MLA_Q_RANK ** -0.5),
        'mla_w_ukv': nrm(12, (N_MLA, MLA_KV_RANK, MLA_HEADS * (MLA_NOPE + MLA_V)), MLA_KV_RANK ** -0.5),
        'mla_w_out': nrm(13, (N_MLA, MLA_HEADS * MLA_V, D_MODEL), (MLA_HEADS * MLA_V) ** -0.5 * BETA),
        'conv_w_in': nrm(14, (N_CONV, D_MODEL, 3 * D_MODEL), D_MODEL ** -0.5),
        'conv_w': nrm(15, (N_CONV, CONV_WIDTH, D_MODEL), CONV_WIDTH ** -0.5),
        'conv_w_out': nrm(16, (N_CONV, D_MODEL, D_MODEL), D_MODEL ** -0.5 * BETA),
        'ln_g': 1.0 + nrm(17, (DEPTH, 2, D_MODEL), 0.01),
        'ln_b': nrm(18, (DEPTH, 2, D_MODEL), 0.01),
        'mlp_w1': nrm(19, (DEPTH, D_MODEL, D_FF), D_MODEL ** -0.5),
        'mlp_w2': nrm(20, (DEPTH, D_FF, D_MODEL), D_FF ** -0.5 * BETA),
        'ple_w_gate': nrm(21, (DEPTH, D_MODEL, D_MODEL), D_MODEL ** -0.5),
        'ple_w_proj': nrm(22, (DEPTH, PLE_DIM, D_MODEL), PLE_DIM ** -0.5),
    }


def _fwd_reference(x, p, positions, gla_w_in, gla_w_gate_up, gla_b_gate, gla_norm_g, gla_w_out,
              mla_w_in, mla_q_norm, mla_kv_norm, mla_w_uq, mla_w_ukv, mla_w_out,
              conv_w_in, conv_w, conv_w_out, ln_g, ln_b, mlp_w1, mlp_w2,
              ple_w_gate, ple_w_proj):
    inv_freq = ROPE_BASE ** (-jnp.arange(0, MLA_ROPE // 2, dtype=jnp.float32) * (2.0 / MLA_ROPE))
    ang = positions.astype(jnp.float32)[..., None] * inv_freq
    cos, sin = jnp.cos(ang), jnp.sin(ang)
    for i in range(DEPTH):
        j = i // N_MIXERS
        kind = i % N_MIXERS
        if kind == 0:
            h = gla_mixer(x, gla_w_in[j], gla_w_gate_up[j], gla_b_gate[j], gla_norm_g[j], gla_w_out[j])
        elif kind == 1:
            h = mla_mixer(x, cos, sin, mla_w_in[j], mla_q_norm[j], mla_kv_norm[j],
                          mla_w_uq[j], mla_w_ukv[j], mla_w_out[j])
        else:
            h = conv_mixer(x, conv_w_in[j], conv_w[j], conv_w_out[j])
        x = layer_norm(ALPHA * x + h, ln_g[i, 0], ln_b[i, 0])
        x = layer_norm(ALPHA * x + sq_relu_mlp(x, mlp_w1[i], mlp_w2[i]), ln_g[i, 1], ln_b[i, 1])
        x = x + jax.nn.sigmoid(x @ ple_w_gate[i]) * (p[i] @ ple_w_proj[i])
    return x


import jax as _jax
import jax.numpy as _jnp

TWIN_FORMAT = 'train_step'
FWD_PARAMS = ['x', 'p', 'positions', 'gla_w_in', 'gla_w_gate_up', 'gla_b_gate', 'gla_norm_g', 'gla_w_out', 'mla_w_in', 'mla_q_norm', 'mla_kv_norm', 'mla_w_uq', 'mla_w_ukv', 'mla_w_out', 'conv_w_in', 'conv_w', 'conv_w_out', 'ln_g', 'ln_b', 'mlp_w1', 'mlp_w2', 'ple_w_gate', 'ple_w_proj']
TWIN_WEIGHTS = ['gla_w_in', 'gla_w_gate_up', 'gla_b_gate', 'gla_norm_g', 'gla_w_out', 'mla_w_in', 'mla_q_norm', 'mla_kv_norm', 'mla_w_uq', 'mla_w_ukv', 'mla_w_out', 'conv_w_in', 'conv_w', 'conv_w_out', 'ln_g', 'ln_b', 'mlp_w1', 'mlp_w2', 'ple_w_gate', 'ple_w_proj']
TWIN_DIFF_INPUT = 'x'
TWIN_INPUTS = ['x', 'p', 'positions', 'gla_w_in', 'gla_w_gate_up', 'gla_b_gate', 'gla_norm_g', 'gla_w_out', 'mla_w_in', 'mla_q_norm', 'mla_kv_norm', 'mla_w_uq', 'mla_w_ukv', 'mla_w_out', 'conv_w_in', 'conv_w', 'conv_w_out', 'ln_g', 'ln_b', 'mlp_w1', 'mlp_w2', 'ple_w_gate', 'ple_w_proj', 'loss_target', 'm_gla_w_in', 'm_gla_w_gate_up', 'm_gla_b_gate', 'm_gla_norm_g', 'm_gla_w_out', 'm_mla_w_in', 'm_mla_q_norm', 'm_mla_kv_norm', 'm_mla_w_uq', 'm_mla_w_ukv', 'm_mla_w_out', 'm_conv_w_in', 'm_conv_w', 'm_conv_w_out', 'm_ln_g', 'm_ln_b', 'm_mlp_w1', 'm_mlp_w2', 'm_ple_w_gate', 'm_ple_w_proj', 'v_gla_w_in', 'v_gla_w_gate_up', 'v_gla_b_gate', 'v_gla_norm_g', 'v_gla_w_out', 'v_mla_w_in', 'v_mla_q_norm', 'v_mla_kv_norm', 'v_mla_w_uq', 'v_mla_w_ukv', 'v_mla_w_out', 'v_conv_w_in', 'v_conv_w', 'v_conv_w_out', 'v_ln_g', 'v_ln_b', 'v_mlp_w1', 'v_mlp_w2', 'v_ple_w_gate', 'v_ple_w_proj']
TWIN_OUTPUTS = ['loss', 'grad_x', 'grad_gla_w_in', 'grad_gla_w_gate_up', 'grad_gla_b_gate', 'grad_gla_norm_g', 'grad_gla_w_out', 'grad_mla_w_in', 'grad_mla_q_norm', 'grad_mla_kv_norm', 'grad_mla_w_uq', 'grad_mla_w_ukv', 'grad_mla_w_out', 'grad_conv_w_in', 'grad_conv_w', 'grad_conv_w_out', 'grad_ln_g', 'grad_ln_b', 'grad_mlp_w1', 'grad_mlp_w2', 'grad_ple_w_gate', 'grad_ple_w_proj', 'delta_gla_w_in', 'delta_gla_w_gate_up', 'delta_gla_b_gate', 'delta_gla_norm_g', 'delta_gla_w_out', 'delta_mla_w_in', 'delta_mla_q_norm', 'delta_mla_kv_norm', 'delta_mla_w_uq', 'delta_mla_w_ukv', 'delta_mla_w_out', 'delta_conv_w_in', 'delta_conv_w', 'delta_conv_w_out', 'delta_ln_g', 'delta_ln_b', 'delta_mlp_w1', 'delta_mlp_w2', 'delta_ple_w_gate', 'delta_ple_w_proj', 'new_m_gla_w_in', 'new_m_gla_w_gate_up', 'new_m_gla_b_gate', 'new_m_gla_norm_g', 'new_m_gla_w_out', 'new_m_mla_w_in', 'new_m_mla_q_norm', 'new_m_mla_kv_norm', 'new_m_mla_w_uq', 'new_m_mla_w_ukv', 'new_m_mla_w_out', 'new_m_conv_w_in', 'new_m_conv_w', 'new_m_conv_w_out', 'new_m_ln_g', 'new_m_ln_b', 'new_m_mlp_w1', 'new_m_mlp_w2', 'new_m_ple_w_gate', 'new_m_ple_w_proj', 'new_v_gla_w_in', 'new_v_gla_w_gate_up', 'new_v_gla_b_gate', 'new_v_gla_norm_g', 'new_v_gla_w_out', 'new_v_mla_w_in', 'new_v_mla_q_norm', 'new_v_mla_kv_norm', 'new_v_mla_w_uq', 'new_v_mla_w_ukv', 'new_v_mla_w_out', 'new_v_conv_w_in', 'new_v_conv_w', 'new_v_conv_w_out', 'new_v_ln_g', 'new_v_ln_b', 'new_v_mlp_w1', 'new_v_mlp_w2', 'new_v_ple_w_gate', 'new_v_ple_w_proj']
TWIN_LEAF_KINDS = {'loss': 'loss', 'grad_x': 'grad_x', 'grad_gla_w_in': 'grad_w', 'grad_gla_w_gate_up': 'grad_w', 'grad_gla_b_gate': 'grad_w', 'grad_gla_norm_g': 'grad_w', 'grad_gla_w_out': 'grad_w', 'grad_mla_w_in': 'grad_w', 'grad_mla_q_norm': 'grad_w', 'grad_mla_kv_norm': 'grad_w', 'grad_mla_w_uq': 'grad_w', 'grad_mla_w_ukv': 'grad_w', 'grad_mla_w_out': 'grad_w', 'grad_conv_w_in': 'grad_w', 'grad_conv_w': 'grad_w', 'grad_conv_w_out': 'grad_w', 'grad_ln_g': 'grad_w', 'grad_ln_b': 'grad_w', 'grad_mlp_w1': 'grad_w', 'grad_mlp_w2': 'grad_w', 'grad_ple_w_gate': 'grad_w', 'grad_ple_w_proj': 'grad_w', 'delta_gla_w_in': 'delta_w', 'delta_gla_w_gate_up': 'delta_w', 'delta_gla_b_gate': 'delta_w', 'delta_gla_norm_g': 'delta_w', 'delta_gla_w_out': 'delta_w', 'delta_mla_w_in': 'delta_w', 'delta_mla_q_norm': 'delta_w', 'delta_mla_kv_norm': 'delta_w', 'delta_mla_w_uq': 'delta_w', 'delta_mla_w_ukv': 'delta_w', 'delta_mla_w_out': 'delta_w', 'delta_conv_w_in': 'delta_w', 'delta_conv_w': 'delta_w', 'delta_conv_w_out': 'delta_w', 'delta_ln_g': 'delta_w', 'delta_ln_b': 'delta_w', 'delta_mlp_w1': 'delta_w', 'delta_mlp_w2': 'delta_w', 'delta_ple_w_gate': 'delta_w', 'delta_ple_w_proj': 'delta_w', 'new_m_gla_w_in': 'new_m', 'new_m_gla_w_gate_up': 'new_m', 'new_m_gla_b_gate': 'new_m', 'new_m_gla_norm_g': 'new_m', 'new_m_gla_w_out': 'new_m', 'new_m_mla_w_in': 'new_m', 'new_m_mla_q_norm': 'new_m', 'new_m_mla_kv_norm': 'new_m', 'new_m_mla_w_uq': 'new_m', 'new_m_mla_w_ukv': 'new_m', 'new_m_mla_w_out': 'new_m', 'new_m_conv_w_in': 'new_m', 'new_m_conv_w': 'new_m', 'new_m_conv_w_out': 'new_m', 'new_m_ln_g': 'new_m', 'new_m_ln_b': 'new_m', 'new_m_mlp_w1': 'new_m', 'new_m_mlp_w2': 'new_m', 'new_m_ple_w_gate': 'new_m', 'new_m_ple_w_proj': 'new_m', 'new_v_gla_w_in': 'new_v', 'new_v_gla_w_gate_up': 'new_v', 'new_v_gla_b_gate': 'new_v', 'new_v_gla_norm_g': 'new_v', 'new_v_gla_w_out': 'new_v', 'new_v_mla_w_in': 'new_v', 'new_v_mla_q_norm': 'new_v', 'new_v_mla_kv_norm': 'new_v', 'new_v_mla_w_uq': 'new_v', 'new_v_mla_w_ukv': 'new_v', 'new_v_mla_w_out': 'new_v', 'new_v_conv_w_in': 'new_v', 'new_v_conv_w': 'new_v', 'new_v_conv_w_out': 'new_v', 'new_v_ln_g': 'new_v', 'new_v_ln_b': 'new_v', 'new_v_mlp_w1': 'new_v', 'new_v_mlp_w2': 'new_v', 'new_v_ple_w_gate': 'new_v', 'new_v_ple_w_proj': 'new_v'}


def _forward(args):
    return _fwd_reference(*[args[k] for k in FWD_PARAMS])


def _output_shape():
    out = _jax.eval_shape(lambda: _forward(_fwd_setup_inputs(0)))
    return out.shape, out.dtype

N_MICROBATCH = 1
ADAM_LR = 0.001
ADAM_B1 = 0.9
ADAM_B2 = 0.999
ADAM_EPS = 1e-08
ADAM_WD = 0.01
ADAM_STEP = 10
PER_EXAMPLE_BATCH_AXIS = {'x': 0, 'p': 1, 'positions': 0, 'loss_target': 0}
SHARED_INPUTS = []
_WEIGHT_DTYPES = {'gla_w_in': _jnp.float32, 'gla_w_gate_up': _jnp.float32, 'gla_b_gate': _jnp.float32, 'gla_norm_g': _jnp.float32, 'gla_w_out': _jnp.float32, 'mla_w_in': _jnp.float32, 'mla_q_norm': _jnp.float32, 'mla_kv_norm': _jnp.float32, 'mla_w_uq': _jnp.float32, 'mla_w_ukv': _jnp.float32, 'mla_w_out': _jnp.float32, 'conv_w_in': _jnp.float32, 'conv_w': _jnp.float32, 'conv_w_out': _jnp.float32, 'ln_g': _jnp.float32, 'ln_b': _jnp.float32, 'mlp_w1': _jnp.float32, 'mlp_w2': _jnp.float32, 'ple_w_gate': _jnp.float32, 'ple_w_proj': _jnp.float32}
MOMENT_SCALE = {'gla_w_in': 2.480642e-02, 'gla_w_gate_up': 3.785384e-03, 'gla_b_gate': 1.266188e-02, 'gla_norm_g': 4.251168e-02, 'gla_w_out': 5.030647e-02, 'mla_w_in': 2.532865e-02, 'mla_q_norm': 1.021765e-02, 'mla_kv_norm': 3.489880e-02, 'mla_w_uq': 3.987254e-03, 'mla_w_ukv': 1.374585e-02, 'mla_w_out': 4.751867e-02, 'conv_w_in': 4.405162e-02, 'conv_w': 4.362581e-02, 'conv_w_out': 1.046779e-01, 'ln_g': 5.828479e+00, 'ln_b': 1.660875e+00, 'mlp_w1': 2.621346e-02, 'mlp_w2': 3.368849e-01, 'ple_w_gate': 1.452534e-01, 'ple_w_proj': 1.350920e-01}


def _to_microbatches(a, axis):
    t = _jnp.moveaxis(a, axis, 0)
    t = t.reshape((N_MICROBATCH, t.shape[0] // N_MICROBATCH) + t.shape[1:])
    return _jnp.moveaxis(t, 1, axis + 1)


def setup_inputs(seed: int = 0) -> dict:
    inp = _fwd_setup_inputs(seed)
    key = _jax.random.fold_in(_jax.random.key(seed), 7919)
    shape, _ = _output_shape()
    out = dict(inp)
    out["loss_target"] = _jax.random.normal(_jax.random.fold_in(key, 0), shape, _jnp.float32)
    for i, name in enumerate(TWIN_WEIGHTS):
        w = inp[name].astype(_jnp.float32)
        if MOMENT_SCALE is None:
            s = _jnp.sqrt(_jnp.mean(_jnp.square(w)) + 1e-30)
        else:
            s = MOMENT_SCALE[name]
        km, kv = _jax.random.split(_jax.random.fold_in(key, i + 1))
        out[name] = w
        out["m_" + name] = s * _jax.random.normal(km, w.shape, _jnp.float32)
        out["v_" + name] = (s * s) * _jax.random.uniform(kv, w.shape, _jnp.float32, 0.5, 1.5)
    if N_MICROBATCH > 1:
        for name, axis in PER_EXAMPLE_BATCH_AXIS.items():
            out[name] = _to_microbatches(out[name], axis)
    return {'x': out['x'], 'p': out['p'], 'positions': out['positions'], 'gla_w_in': out['gla_w_in'], 'gla_w_gate_up': out['gla_w_gate_up'], 'gla_b_gate': out['gla_b_gate'], 'gla_norm_g': out['gla_norm_g'], 'gla_w_out': out['gla_w_out'], 'mla_w_in': out['mla_w_in'], 'mla_q_norm': out['mla_q_norm'], 'mla_kv_norm': out['mla_kv_norm'], 'mla_w_uq': out['mla_w_uq'], 'mla_w_ukv': out['mla_w_ukv'], 'mla_w_out': out['mla_w_out'], 'conv_w_in': out['conv_w_in'], 'conv_w': out['conv_w'], 'conv_w_out': out['conv_w_out'], 'ln_g': out['ln_g'], 'ln_b': out['ln_b'], 'mlp_w1': out['mlp_w1'], 'mlp_w2': out['mlp_w2'], 'ple_w_gate': out['ple_w_gate'], 'ple_w_proj': out['ple_w_proj'], 'loss_target': out['loss_target'], 'm_gla_w_in': out['m_gla_w_in'], 'm_gla_w_gate_up': out['m_gla_w_gate_up'], 'm_gla_b_gate': out['m_gla_b_gate'], 'm_gla_norm_g': out['m_gla_norm_g'], 'm_gla_w_out': out['m_gla_w_out'], 'm_mla_w_in': out['m_mla_w_in'], 'm_mla_q_norm': out['m_mla_q_norm'], 'm_mla_kv_norm': out['m_mla_kv_norm'], 'm_mla_w_uq': out['m_mla_w_uq'], 'm_mla_w_ukv': out['m_mla_w_ukv'], 'm_mla_w_out': out['m_mla_w_out'], 'm_conv_w_in': out['m_conv_w_in'], 'm_conv_w': out['m_conv_w'], 'm_conv_w_out': out['m_conv_w_out'], 'm_ln_g': out['m_ln_g'], 'm_ln_b': out['m_ln_b'], 'm_mlp_w1': out['m_mlp_w1'], 'm_mlp_w2': out['m_mlp_w2'], 'm_ple_w_gate': out['m_ple_w_gate'], 'm_ple_w_proj': out['m_ple_w_proj'], 'v_gla_w_in': out['v_gla_w_in'], 'v_gla_w_gate_up': out['v_gla_w_gate_up'], 'v_gla_b_gate': out['v_gla_b_gate'], 'v_gla_norm_g': out['v_gla_norm_g'], 'v_gla_w_out': out['v_gla_w_out'], 'v_mla_w_in': out['v_mla_w_in'], 'v_mla_q_norm': out['v_mla_q_norm'], 'v_mla_kv_norm': out['v_mla_kv_norm'], 'v_mla_w_uq': out['v_mla_w_uq'], 'v_mla_w_ukv': out['v_mla_w_ukv'], 'v_mla_w_out': out['v_mla_w_out'], 'v_conv_w_in': out['v_conv_w_in'], 'v_conv_w': out['v_conv_w'], 'v_conv_w_out': out['v_conv_w_out'], 'v_ln_g': out['v_ln_g'], 'v_ln_b': out['v_ln_b'], 'v_mlp_w1': out['v_mlp_w1'], 'v_mlp_w2': out['v_mlp_w2'], 'v_ple_w_gate': out['v_ple_w_gate'], 'v_ple_w_proj': out['v_ple_w_proj']}


def _loss(weights, diff, rest, loss_target):
    with _jax.named_scope("forward"):
        args = {**rest, TWIN_DIFF_INPUT: diff, **{k: w.astype(_WEIGHT_DTYPES[k]) for k, w in weights.items()}}
        y = _forward(args)
    with _jax.named_scope("loss_head"):
        err = _jnp.square(y.astype(_jnp.float32) - loss_target)
        return 0.5 * _jnp.sum(_jnp.mean(err, axis=-1)) if err.ndim else 0.5 * err


def _adamw(w, g, m, v):
    m = ADAM_B1 * m + (1.0 - ADAM_B1) * g
    v = ADAM_B2 * v + (1.0 - ADAM_B2) * _jnp.square(g)
    m_hat = m / (1.0 - ADAM_B1 ** ADAM_STEP)
    v_hat = v / (1.0 - ADAM_B2 ** ADAM_STEP)
    delta = -ADAM_LR * (m_hat / (_jnp.sqrt(v_hat) + ADAM_EPS) + ADAM_WD * w)
    return delta, m, v


def reference(x, p, positions, gla_w_in, gla_w_gate_up, gla_b_gate, gla_norm_g, gla_w_out, mla_w_in, mla_q_norm, mla_kv_norm, mla_w_uq, mla_w_ukv, mla_w_out, conv_w_in, conv_w, conv_w_out, ln_g, ln_b, mlp_w1, mlp_w2, ple_w_gate, ple_w_proj, loss_target, m_gla_w_in, m_gla_w_gate_up, m_gla_b_gate, m_gla_norm_g, m_gla_w_out, m_mla_w_in, m_mla_q_norm, m_mla_kv_norm, m_mla_w_uq, m_mla_w_ukv, m_mla_w_out, m_conv_w_in, m_conv_w, m_conv_w_out, m_ln_g, m_ln_b, m_mlp_w1, m_mlp_w2, m_ple_w_gate, m_ple_w_proj, v_gla_w_in, v_gla_w_gate_up, v_gla_b_gate, v_gla_norm_g, v_gla_w_out, v_mla_w_in, v_mla_q_norm, v_mla_kv_norm, v_mla_w_uq, v_mla_w_ukv, v_mla_w_out, v_conv_w_in, v_conv_w, v_conv_w_out, v_ln_g, v_ln_b, v_mlp_w1, v_mlp_w2, v_ple_w_gate, v_ple_w_proj):
    given = dict(x=x, p=p, positions=positions, gla_w_in=gla_w_in, gla_w_gate_up=gla_w_gate_up, gla_b_gate=gla_b_gate, gla_norm_g=gla_norm_g, gla_w_out=gla_w_out, mla_w_in=mla_w_in, mla_q_norm=mla_q_norm, mla_kv_norm=mla_kv_norm, mla_w_uq=mla_w_uq, mla_w_ukv=mla_w_ukv, mla_w_out=mla_w_out, conv_w_in=conv_w_in, conv_w=conv_w, conv_w_out=conv_w_out, ln_g=ln_g, ln_b=ln_b, mlp_w1=mlp_w1, mlp_w2=mlp_w2, ple_w_gate=ple_w_gate, ple_w_proj=ple_w_proj, loss_target=loss_target, m_gla_w_in=m_gla_w_in, m_gla_w_gate_up=m_gla_w_gate_up, m_gla_b_gate=m_gla_b_gate, m_gla_norm_g=m_gla_norm_g, m_gla_w_out=m_gla_w_out, m_mla_w_in=m_mla_w_in, m_mla_q_norm=m_mla_q_norm, m_mla_kv_norm=m_mla_kv_norm, m_mla_w_uq=m_mla_w_uq, m_mla_w_ukv=m_mla_w_ukv, m_mla_w_out=m_mla_w_out, m_conv_w_in=m_conv_w_in, m_conv_w=m_conv_w, m_conv_w_out=m_conv_w_out, m_ln_g=m_ln_g, m_ln_b=m_ln_b, m_mlp_w1=m_mlp_w1, m_mlp_w2=m_mlp_w2, m_ple_w_gate=m_ple_w_gate, m_ple_w_proj=m_ple_w_proj, v_gla_w_in=v_gla_w_in, v_gla_w_gate_up=v_gla_w_gate_up, v_gla_b_gate=v_gla_b_gate, v_gla_norm_g=v_gla_norm_g, v_gla_w_out=v_gla_w_out, v_mla_w_in=v_mla_w_in, v_mla_q_norm=v_mla_q_norm, v_mla_kv_norm=v_mla_kv_norm, v_mla_w_uq=v_mla_w_uq, v_mla_w_ukv=v_mla_w_ukv, v_mla_w_out=v_mla_w_out, v_conv_w_in=v_conv_w_in, v_conv_w=v_conv_w, v_conv_w_out=v_conv_w_out, v_ln_g=v_ln_g, v_ln_b=v_ln_b, v_mlp_w1=v_mlp_w1, v_mlp_w2=v_mlp_w2, v_ple_w_gate=v_ple_w_gate, v_ple_w_proj=v_ple_w_proj)
    weights = {n: given[n] for n in TWIN_WEIGHTS}
    shared = {n: given[n] for n in SHARED_INPUTS}
    per_example = {n: given[n] for n in ['x', 'p', 'positions']}
    grad_fn = _jax.value_and_grad(_loss, argnums=(0, 1))

    def one_microbatch(ex, loss_target):
        ex = dict(ex)
        diff = ex.pop(TWIN_DIFF_INPUT)
        return grad_fn(weights, diff, {**shared, **ex}, loss_target)

    if N_MICROBATCH == 1:
        loss, (grad_w, grad_x) = one_microbatch(per_example, given["loss_target"])
    else:
        def body(carry, xs):
            loss_sum, grad_sum = carry
            l_k, (gw_k, gx_k) = one_microbatch(xs[0], xs[1])
            with _jax.named_scope("update"):
                return (loss_sum + l_k, _jax.tree.map(_jnp.add, grad_sum, gw_k)), gx_k

        init = (_jnp.zeros((), _jnp.float32), _jax.tree.map(_jnp.zeros_like, weights))
        (loss, grad_w), grad_x = _jax.lax.scan(body, init, (per_example, given["loss_target"]))
    with _jax.named_scope("update"):
        delta_w, new_m, new_v = {}, {}, {}
        for n in TWIN_WEIGHTS:
            delta_w[n], new_m[n], new_v[n] = _adamw(weights[n], grad_w[n], given["m_" + n], given["v_" + n])
    return (loss, grad_x, *[grad_w[n] for n in TWIN_WEIGHTS], *[delta_w[n] for n in TWIN_WEIGHTS],
            *[new_m[n] for n in TWIN_WEIGHTS], *[new_v[n] for n in TWIN_WEIGHTS])
```

```python
import functools
import math

import numpy as np
import jax
import jax.numpy as jnp
from jax import lax
from jax.experimental import pallas as pl
from jax.experimental.pallas import tpu as pltpu

F32 = jnp.float32
BF16 = jnp.bfloat16
MESH = pl.DeviceIdType.MESH

D_MODEL = 1024
DEPTH = 4
CHUNK = 64
ALPHA = (2 * DEPTH) ** 0.25
LN_EPS = 1e-5
RMS_EPS = 1e-6
PLE_DIM = 256
D_FF = 4 * D_MODEL
GLA_HEADS = 4
GLA_DK = 128
GLA_DV = 256
GLA_RANK = 16
GLA_TAU = 16.0
GLA_HK = GLA_HEADS * GLA_DK
GLA_HV = GLA_HEADS * GLA_DV
GLA_IN = 2 * GLA_HK + GLA_HV + D_MODEL + GLA_RANK
GLA_IN_PAD = 2 * GLA_HK + GLA_HV + D_MODEL + 128
MLA_HEADS = 8
MLA_NOPE = 128
MLA_ROPE = 64
MLA_V = 128
MLA_QR = 256
MLA_KVR = 256
MLA_IN = MLA_QR + MLA_KVR + MLA_ROPE
MLA_IN_PAD = MLA_QR + MLA_KVR + 128
MLA_QH = 256
ROPE_BASE = 10000.0
ADAM_LR = 0.001
ADAM_B1 = 0.9
ADAM_B2 = 0.999
ADAM_EPS = 1e-08
ADAM_WD = 0.01
ADAM_STEP = 10

VMEM_LIMIT = 48 * 1024 * 1024
N_CHIPS = 4
LANES = 1024

WSPEC = {
    'gla_w_in': ((2, 1024, 772), 2), 'gla_w_gate_up': ((2, 16, 128), 2), 'gla_b_gate': ((2, 128), 1),
    'gla_norm_g': ((2, 64), 1), 'gla_w_out': ((2, 256, 1024), 1), 'mla_w_in': ((1, 256, 576), 1),
    'mla_q_norm': ((1, 256), None), 'mla_kv_norm': ((1, 256), None), 'mla_w_uq': ((1, 256, 384), 2),
    'mla_w_ukv': ((1, 256, 512), 2), 'mla_w_out': ((1, 256, 1024), 1), 'conv_w_in': ((1, 1024, 768), 2),
    'conv_w': ((1, 3, 256), 2), 'conv_w_out': ((1, 256, 1024), 1), 'ln_g': ((4, 2, 256), 2),
    'ln_b': ((4, 2, 256), 2), 'mlp_w1': ((4, 1024, 1024), 2), 'mlp_w2': ((4, 1024, 1024), 1),
    'ple_w_gate': ((4, 256, 1024), 1), 'ple_w_proj': ((4, 256, 256), 2),
}
WNAMES = list(WSPEC)
BIG = ['gla_w_in', 'gla_w_out', 'mla_w_in', 'mla_w_uq', 'mla_w_ukv', 'mla_w_out', 'conv_w_in', 'conv_w_out',
       'mlp_w1', 'mlp_w2', 'ple_w_gate', 'ple_w_proj']
SMALL_SHARDED = ['gla_w_gate_up', 'gla_b_gate', 'gla_norm_g', 'conv_w', 'ln_g', 'ln_b']
SMALL = SMALL_SHARDED + ['mla_q_norm', 'mla_kv_norm']


def _size(shape):
    return int(np.prod(shape))


BIG_ROWS = -(-sum(_size(WSPEC[n][0]) for n in BIG) // LANES // 1024) * 1024
SMALL_F32 = 16384
AG_ROWS = BIG_ROWS + 2 * SMALL_F32 // LANES


def _full_shape(name):
    shape, ax = WSPEC[name]
    if ax is None:
        return shape
    return tuple(s * N_CHIPS if i == ax else s for i, s in enumerate(shape))


def _cparams(sem=None):
    return pltpu.CompilerParams(dimension_semantics=sem, vmem_limit_bytes=VMEM_LIMIT)


def _mm(a, b, *, name, ta=False, tb=False, M=None, N=None, K=None, out_dtypes=(F32,), epilogue=None, extras=(),
        tm=1024, tn=512, tk=1024, a_off=(0, 0), b_off=(0, 0)):
    if M is None:
        M = a.shape[1] if ta else a.shape[0]
    if K is None:
        K = a.shape[0] if ta else a.shape[1]
    if N is None:
        N = b.shape[0] if tb else b.shape[1]
    tm, tn, tk = min(tm, M), min(tn, N), min(tk, K)
    assert M % tm == 0 and N % tn == 0 and K % tk == 0, (name, M, N, K, tm, tn, tk)
    nk = K // tk
    n_ex, n_out = len(extras), len(out_dtypes)

    def body(a_ref, b_ref, *rest):
        ex_refs, out_refs = rest[:n_ex], rest[n_ex:n_ex + n_out]
        part = lax.dot_general(a_ref[...].astype(BF16), b_ref[...].astype(BF16),
                               ((((0,) if ta else (1,)), ((1,) if tb else (0,))), ((), ())),
                               preferred_element_type=F32)

        def finish(acc):
            res = (acc,) if epilogue is None else epilogue(acc, *[r[...] for r in ex_refs])
            for r, v in zip(out_refs, res):
                r[...] = v.astype(r.dtype)

        if nk == 1:
            finish(part)
        else:
            acc_ref = rest[-1]
            k = pl.program_id(2)

            @pl.when(k == 0)
            def _():
                acc_ref[...] = part

            @pl.when(k > 0)
            def _():
                acc_ref[...] += part

            @pl.when(k == nk - 1)
            def _():
                finish(acc_ref[...])

    if ta:
        a_spec = pl.BlockSpec((tk, tm), lambda i, j, k: (k + a_off[0], i + a_off[1]))
    else:
        a_spec = pl.BlockSpec((tm, tk), lambda i, j, k: (i + a_off[0], k + a_off[1]))
    if tb:
        b_spec = pl.BlockSpec((tn, tk), lambda i, j, k: (j + b_off[0], k + b_off[1]))
    else:
        b_spec = pl.BlockSpec((tk, tn), lambda i, j, k: (k + b_off[0], j + b_off[1]))
    ex_specs = []
    for arr, kind in extras:
        if kind == 'mn':
            ex_specs.append(pl.BlockSpec((tm, tn), lambda i, j, k: (i, j)))
        else:
            ex_specs.append(pl.BlockSpec((1, tn), lambda i, j, k: (0, j)))
    outs = pl.pallas_call(
        body, name=name, grid=(M // tm, N // tn, nk),
        in_specs=[a_spec, b_spec] + ex_specs,
        out_specs=[pl.BlockSpec((tm, tn), lambda i, j, k: (i, j)) for _ in out_dtypes],
        out_shape=[jax.ShapeDtypeStruct((M, N), d) for d in out_dtypes],
        scratch_shapes=[pltpu.VMEM((tm, tn), F32)] if nk > 1 else [],
        compiler_params=_cparams(("parallel", "parallel", "arbitrary")),
    )(a, b, *[e[0] for e in extras])
    return outs[0] if n_out == 1 else tuple(outs)


def _rowwise(fn, *, name, rows, pars=(), outs=(), accs=(), tm=256):
    S = rows[0][0].shape[0]
    tm = min(tm, S)
    assert S % tm == 0
    n_r, n_p, n_o, n_a = len(rows), len(pars), len(outs), len(accs)

    def body(*refs):
        r_refs, p_refs = refs[:n_r], refs[n_r:n_r + n_p]
        o_refs, a_refs = refs[n_r + n_p:n_r + n_p + n_o], refs[n_r + n_p + n_o:]
        o_vals, a_vals = fn([r[...] for r in r_refs], [p[...] for p in p_refs])
        for r, v in zip(o_refs, o_vals):
            r[...] = v.astype(r.dtype)
        if n_a:
            i = pl.program_id(0)

            @pl.when(i == 0)
            def _():
                for r in a_refs:
                    r[...] = jnp.zeros(r.shape, r.dtype)

            for r, v in zip(a_refs, a_vals):
                r[...] += jnp.broadcast_to(v, r.shape)

    in_specs = [pl.BlockSpec((tm, w), functools.partial(lambda i, o: (i, o), o=off)) for _, w, off in rows]
    in_specs += [pl.BlockSpec(p.shape, functools.partial(lambda i, nd: (0,) * nd, nd=p.ndim)) for p in pars]
    out_specs = [pl.BlockSpec((tm, w), lambda i: (i, 0)) for w, _ in outs]
    out_specs += [pl.BlockSpec((8, w), lambda i: (0, 0)) for w in accs]
    out_shape = [jax.ShapeDtypeStruct((S, w), d) for w, d in outs]
    out_shape += [jax.ShapeDtypeStruct((8, w), F32) for w in accs]
    res = pl.pallas_call(
        body, name=name, grid=(S // tm,), in_specs=in_specs, out_specs=out_specs, out_shape=out_shape,
        compiler_params=_cparams(("arbitrary",)),
    )(*[r[0] for r in rows], *pars)
    return tuple(res)


def _colsum(v):
    return jnp.sum(v, axis=0, keepdims=True)


def _ln_stats(v):
    mu = jnp.mean(v, axis=-1, keepdims=True)
    d = v - mu
    var = jnp.mean(d * d, axis=-1, keepdims=True)
    rstd = lax.rsqrt(var + LN_EPS)
    return d * rstd, rstd


def _ln_fwd(x, h, g, b, name):
    def fn(r, p):
        xhat, _ = _ln_stats(ALPHA * r[0] + r[1])
        y = xhat * p[0] + p[1]
        return [y, y], []
    return _rowwise(fn, name=name, rows=[(x, D_MODEL, 0), (h, D_MODEL, 0)], pars=[g, b],
                    outs=[(D_MODEL, F32), (D_MODEL, BF16)])


def _ln_bwd(x, h, g, dy, name):
    def fn(r, p):
        xhat, rstd = _ln_stats(ALPHA * r[0] + r[1])
        dyv = r[2]
        dxh = dyv * p[0]
        m1 = jnp.mean(dxh, axis=-1, keepdims=True)
        m2 = jnp.mean(dxh * xhat, axis=-1, keepdims=True)
        dv = rstd * (dxh - m1 - xhat * m2)
        return [dv, dv], [_colsum(dyv * xhat), _colsum(dyv)]
    return _rowwise(fn, name=name, rows=[(x, D_MODEL, 0), (h, D_MODEL, 0), (dy, D_MODEL, 0)], pars=[g],
                    outs=[(D_MODEL, F32), (D_MODEL, BF16)], accs=[D_MODEL, D_MODEL])


def _loss_head(y, t):
    def fn(r, p):
        d = r[0] - r[1]
        return [d * (1.0 / D_MODEL)], [_colsum(d * d) * (0.5 / D_MODEL)]
    return _rowwise(fn, name="loss_head", rows=[(y, D_MODEL, 0), (t, D_MODEL, 0)], outs=[(D_MODEL, F32)],
                    accs=[D_MODEL])


def _ple_bwd_gate(dx3, z, pp):
    def fn(r, p):
        s = jax.nn.sigmoid(r[1])
        return [r[0] * s, r[0] * r[2] * s * (1.0 - s)], []
    return _rowwise(fn, name="ple_bwd_gate", rows=[(dx3, D_MODEL, 0), (z, D_MODEL, 0), (pp, D_MODEL, 0)],
                    outs=[(D_MODEL, BF16), (D_MODEL, BF16)])


N_LEVELS = 6


def _gla_consts():
    C = CHUNK
    A = np.zeros((N_LEVELS + 3, C, C), np.float32)
    masks = np.zeros((N_LEVELS + 1, C, C), np.float32)
    r = np.arange(C)[:, None]
    u = np.arange(C)[None, :]
    for l in range(N_LEVELS):
        half = C >> (l + 1)
        mid = (r // (2 * half)) * (2 * half) + half - 1
        A[l] = np.where(r > mid, (u > mid) & (u <= r), (u > r) & (u <= mid))
        masks[l] = ((r // (2 * half)) == (u // (2 * half))) & (((r // half) % 2) != ((u // half) % 2))
    masks[N_LEVELS] = (r == u)
    A[N_LEVELS] = (u <= r)
    A[N_LEVELS + 1] = (u > r)
    A[N_LEVELS + 2] = 1.0
    A = A.reshape(-1, C)
    return A, np.ascontiguousarray(A.T), masks


def _split3(v):
    hi = v.astype(BF16)
    r1 = v - hi.astype(F32)
    mid = r1.astype(BF16)
    lo = (r1 - mid.astype(F32)).astype(BF16)
    return hi, mid, lo


def _dot_exact01(a01, v):
    hi, mid, lo = _split3(v)
    f = lambda p: jnp.dot(a01, p, preferred_element_type=F32)
    return f(hi) + f(mid) + f(lo)


def _nt(a, b):
    return lax.dot_general(a, b, (((1,), (1,)), ((), ())), preferred_element_type=F32)


def _tn(a, b):
    return lax.dot_general(a, b, (((0,), (0,)), ((), ())), preferred_element_type=F32)


def _nn(a, b):
    return jnp.dot(a, b, preferred_element_type=F32)


def _gla_chunk_terms(q, k, la, a_ref, m_ref):
    C = CHUNK
    E = jnp.exp(_dot_exact01(a_ref[...], la))
    scores = m_ref[N_LEVELS] * _nt(q.astype(BF16), k.astype(BF16))
    qes, kes = [], []
    for l in range(N_LEVELS):
        El = E[l * C:(l + 1) * C]
        qe, ke = (q * El).astype(BF16), (k * El).astype(BF16)
        qes.append(qe)
        kes.append(ke)
        scores = scores + m_ref[l] * _nt(qe, ke)
    return E, qes, kes, scores


def _gla_fwd(pin, la):
    S = pin.shape[0]
    NC = S // CHUNK
    C = CHUNK
    A, _, masks = _gla_consts()

    def body(q_ref, k_ref, v_ref, la_ref, a_ref, m_ref, o_ref, st_ref, state):
        c = pl.program_id(1)

        @pl.when(c == 0)
        def _():
            state[...] = jnp.zeros(state.shape, F32)

        q = q_ref[...] * (GLA_DK ** -0.5)
        k, v, la_c = k_ref[...], v_ref[...], la_ref[...]
        E, _, _, scores = _gla_chunk_terms(q, k, la_c, a_ref, m_ref)
        Eq, Ek, Ee = E[6 * C:7 * C], E[7 * C:8 * C], E[8 * C:9 * C]
        st = state[...]
        st_ref[...] = st
        vb = v.astype(BF16)
        o_ref[...] = _nn(scores.astype(BF16), vb) + _nt((q * Eq).astype(BF16), st.astype(BF16))
        state[...] = st * jnp.concatenate([Ee] * (GLA_DV // C), axis=0) + _tn(vb, (k * Ek).astype(BF16))

    nkb = GLA_HK // GLA_DK
    return pl.pallas_call(
        body, name="gla_fwd", grid=(GLA_HEADS, NC),
        in_specs=[pl.BlockSpec((C, GLA_DK), lambda h, c: (c, h)),
                  pl.BlockSpec((C, GLA_DK), lambda h, c: (c, nkb + h)),
                  pl.BlockSpec((C, GLA_DV), lambda h, c: (c, 2 * GLA_HK // GLA_DV + h)),
                  pl.BlockSpec((C, GLA_DK), lambda h, c: (c, h)),
                  pl.BlockSpec(A.shape, lambda h, c: (0, 0)),
                  pl.BlockSpec(masks.shape, lambda h, c: (0, 0, 0))],
        out_specs=[pl.BlockSpec((C, GLA_DV), lambda h, c: (c, h)),
                   pl.BlockSpec((None, None, GLA_DV, GLA_DK), lambda h, c: (h, c, 0, 0))],
        out_shape=[jax.ShapeDtypeStruct((S, GLA_HV), F32),
                   jax.ShapeDtypeStruct((GLA_HEADS, NC, GLA_DV, GLA_DK), F32)],
        scratch_shapes=[pltpu.VMEM((GLA_DV, GLA_DK), F32)],
        compiler_params=_cparams(("parallel", "arbitrary")),
    )(pin, pin, pin, la, jnp.asarray(A, BF16), jnp.asarray(masks))


def _gla_bwd(pin, la, states, do):
    S = pin.shape[0]
    NC = S // CHUNK
    C = CHUNK
    A, AT, masks = _gla_consts()
    scale = GLA_DK ** -0.5

    def body(q_ref, k_ref, v_ref, la_ref, st_ref, do_ref, a_ref, at_ref, m_ref,
             dq_ref, dk_ref, dv_ref, dla_ref, dstate):
        c = pl.program_id(1)

        @pl.when(c == 0)
        def _():
            dstate[...] = jnp.zeros(dstate.shape, F32)

        q = q_ref[...] * scale
        k, v, la_c, st, dov = k_ref[...], v_ref[...], la_ref[...], st_ref[...], do_ref[...]
        E, qes, kes, scores = _gla_chunk_terms(q, k, la_c, a_ref, m_ref)
        Eq, Ek, Ee = E[6 * C:7 * C], E[7 * C:8 * C], E[8 * C:9 * C]
        dst = dstate[...]
        dob, vb, dstb = dov.astype(BF16), v.astype(BF16), dst.astype(BF16)
        qEq, kEk = (q * Eq).astype(BF16), (k * Ek).astype(BF16)
        dsc = _nt(dob, vb)
        dv_ref[...] = (_tn(scores.astype(BF16), dob) + _nt(kEk, dstb)).astype(dv_ref.dtype)
        dqEq = _nn(dob, st.astype(BF16))
        dkEk = _nn(vb, dstb)
        Gd = (m_ref[N_LEVELS] * dsc).astype(BF16)
        dq = _nn(Gd, k.astype(BF16)) + dqEq * Eq
        dk = _tn(Gd, q.astype(BF16)) + dkEk * Ek
        dX = []
        for l in range(N_LEVELS):
            El = E[l * C:(l + 1) * C]
            G = (m_ref[l] * dsc).astype(BF16)
            dqe, dke = _nn(G, kes[l]), _tn(G, qes[l])
            dq = dq + dqe * El
            dk = dk + dke * El
            dX.append((dqe * q + dke * k) * El)
        dX.append(dqEq * q * Eq)
        dX.append(dkEk * k * Ek)
        prod = dst * st
        dEe = prod[0:C]
        for i in range(1, GLA_DV // C):
            dEe = dEe + prod[i * C:(i + 1) * C]
        dX.append(dEe * Ee)
        dla_ref[...] = _dot_exact01(at_ref[...], jnp.concatenate(dX, axis=0))
        dq_ref[...] = (dq * scale).astype(dq_ref.dtype)
        dk_ref[...] = dk.astype(dk_ref.dtype)
        dstate[...] = dst * jnp.concatenate([Ee] * (GLA_DV // C), axis=0) + _tn(dob, qEq)

    nkb = GLA_HK // GLA_DK
    rc = lambda c: NC - 1 - c
    return pl.pallas_call(
        body, name="gla_bwd", grid=(GLA_HEADS, NC),
        in_specs=[pl.BlockSpec((C, GLA_DK), lambda h, c: (rc(c), h)),
                  pl.BlockSpec((C, GLA_DK), lambda h, c: (rc(c), nkb + h)),
                  pl.BlockSpec((C, GLA_DV), lambda h, c: (rc(c), 2 * GLA_HK // GLA_DV + h)),
                  pl.BlockSpec((C, GLA_DK), lambda h, c: (rc(c), h)),
                  pl.BlockSpec((None, None, GLA_DV, GLA_DK), lambda h, c: (h, rc(c), 0, 0)),
                  pl.BlockSpec((C, GLA_DV), lambda h, c: (rc(c), h)),
                  pl.BlockSpec(A.shape, lambda h, c: (0, 0)),
                  pl.BlockSpec(AT.shape, lambda h, c: (0, 0)),
                  pl.BlockSpec(masks.shape, lambda h, c: (0, 0, 0))],
        out_specs=[pl.BlockSpec((C, GLA_DK), lambda h, c: (rc(c), h)),
                   pl.BlockSpec((C, GLA_DK), lambda h, c: (rc(c), h)),
                   pl.BlockSpec((C, GLA_DV), lambda h, c: (rc(c), h)),
                   pl.BlockSpec((C, GLA_DK), lambda h, c: (rc(c), h))],
        out_shape=[jax.ShapeDtypeStruct((S, GLA_HK), BF16), jax.ShapeDtypeStruct((S, GLA_HK), BF16),
                   jax.ShapeDtypeStruct((S, GLA_HV), BF16), jax.ShapeDtypeStruct((S, GLA_HK), F32)],
        scratch_shapes=[pltpu.VMEM((GLA_DV, GLA_DK), F32)],
        compiler_params=_cparams(("parallel", "arbitrary")),
    )(pin, pin, pin, la, states, do, jnp.asarray(A, BF16), jnp.asarray(AT, BF16), jnp.asarray(masks))


def _gla_post_fwd(o, pin, g):
    def fn(r, p):
        ov, rv = r
        ys = []
        for h in range(GLA_HEADS):
            oh = ov[:, h * GLA_DV:(h + 1) * GLA_DV]
            rh = rv[:, h * GLA_DV:(h + 1) * GLA_DV]
            rs = lax.rsqrt(jnp.mean(oh * oh, axis=-1, keepdims=True) + RMS_EPS)
            ys.append(oh * rs * p[0] * (rh * jax.nn.sigmoid(rh)))
        return [jnp.concatenate(ys, axis=1)], []
    return _rowwise(fn, name="gla_post_fwd", rows=[(o, GLA_HV, 0), (pin, GLA_HV, (2 * GLA_HK + GLA_HV) // GLA_HV)],
                    pars=[g], outs=[(GLA_HV, BF16)])[0]


def _gla_post_bwd(dy, o, pin, g):
    def fn(r, p):
        dyv, ov, rv = r
        dos, drs, dg = [], [], 0.0
        for h in range(GLA_HEADS):
            sl = slice(h * GLA_DV, (h + 1) * GLA_DV)
            oh, rh, dyh = ov[:, sl], rv[:, sl], dyv[:, sl]
            rs = lax.rsqrt(jnp.mean(oh * oh, axis=-1, keepdims=True) + RMS_EPS)
            xh = oh * rs
            sg = jax.nn.sigmoid(rh)
            d_on = dyh * (rh * sg)
            drs.append(dyh * (xh * p[0]) * (sg * (1.0 + rh * (1.0 - sg))))
            dg = dg + _colsum(d_on * xh)
            dxh = d_on * p[0]
            dos.append(rs * (dxh - xh * jnp.mean(dxh * xh, axis=-1, keepdims=True)))
        return [jnp.concatenate(dos, axis=1), jnp.concatenate(drs, axis=1)], [dg]
    return _rowwise(fn, name="gla_post_bwd",
                    rows=[(dy, GLA_HV, 0), (o, GLA_HV, 0), (pin, GLA_HV, (2 * GLA_HK + GLA_HV) // GLA_HV)],
                    pars=[g], outs=[(GLA_HV, F32), (GLA_HV, BF16)], accs=[GLA_DV])


def _gla_gate_bwd(dla, la):
    def fn(r, p):
        dz = r[0] * (1.0 / GLA_TAU) * (1.0 - jnp.exp(GLA_TAU * r[1]))
        return [dz], [_colsum(dz)]
    return _rowwise(fn, name="gla_gate_bwd", rows=[(dla, GLA_HK, 0), (la, GLA_HK, 0)], outs=[(GLA_HK, BF16)],
                    accs=[GLA_HK])


def _log_sigmoid(z):
    return jnp.minimum(z, 0.0) - jnp.log(1.0 + jnp.exp(-jnp.abs(z)))


def _rot_half(v):
    lane = lax.broadcasted_iota(jnp.int32, v.shape, 1)
    return jnp.where(lane < 32, -pltpu.roll(v, 96, 1), jnp.where(lane < 64, pltpu.roll(v, 32, 1), 0.0))


def _rms(v):
    rs = lax.rsqrt(jnp.mean(v * v, axis=-1, keepdims=True) + RMS_EPS)
    return v * rs, rs


def _mla_norm_fwd(cin, gq, gkv, cosp, sinp):
    def fn(r, p):
        cv, cs, sn = r
        qn, _ = _rms(cv[:, :MLA_QR])
        kvn, _ = _rms(cv[:, MLA_QR:MLA_QR + MLA_KVR])
        kr = cv[:, MLA_QR + MLA_KVR:]
        return [qn * p[0], kvn * p[1], kr * cs + _rot_half(kr) * sn], []
    return _rowwise(fn, name="mla_norm_fwd", rows=[(cin, MLA_IN_PAD, 0), (cosp, 128, 0), (sinp, 128, 0)],
                    pars=[gq, gkv], outs=[(MLA_QR, BF16), (MLA_KVR, BF16), (128, BF16)])


def _mla_qrope_fwd(q, cosp, sinp):
    scale = (MLA_NOPE + MLA_ROPE) ** -0.5

    def fn(r, p):
        qv, cs, sn = r
        parts = []
        for h in range(MLA_HEADS):
            parts.append(qv[:, h * MLA_QH:h * MLA_QH + 128] * scale)
            rp = qv[:, h * MLA_QH + 128:(h + 1) * MLA_QH]
            parts.append((rp * cs + _rot_half(rp) * sn) * scale)
        return [jnp.concatenate(parts, axis=1)], []
    W = MLA_HEADS * MLA_QH
    return _rowwise(fn, name="mla_qrope_fwd", rows=[(q, W, 0), (cosp, 128, 0), (sinp, 128, 0)],
                    outs=[(W, BF16)])[0]


def _mla_qrope_bwd(dq, cosp, sinp):
    scale = (MLA_NOPE + MLA_ROPE) ** -0.5

    def fn(r, p):
        dv, cs, sn = r
        parts = []
        for h in range(MLA_HEADS):
            parts.append(dv[:, h * MLA_QH:h * MLA_QH + 128] * scale)
            rp = dv[:, h * MLA_QH + 128:(h + 1) * MLA_QH]
            parts.append((rp * cs - _rot_half(rp) * sn) * scale)
        return [jnp.concatenate(parts, axis=1)], []
    W = MLA_HEADS * MLA_QH
    return _rowwise(fn, name="mla_qrope_bwd", rows=[(dq, W, 0), (cosp, 128, 0), (sinp, 128, 0)],
                    outs=[(W, BF16)])[0]


def _mla_norm_bwd(cin, dqn, dkvn, dkr, gq, gkv, cosp, sinp):
    def fn(r, p):
        cv, dq_, dkv_, dkr_, cs, sn = r
        outs, accs = [], []
        for (lo, hi), dn, g in (((0, MLA_QR), dq_, p[0]), ((MLA_QR, MLA_QR + MLA_KVR), dkv_, p[1])):
            xh, rs = _rms(cv[:, lo:hi])
            dxh = dn * g
            outs.append(rs * (dxh - xh * jnp.mean(dxh * xh, axis=-1, keepdims=True)))
            accs.append(_colsum(dn * xh))
        dk = dkr_[:, 0:128]
        for h in range(1, MLA_HEADS):
            dk = dk + dkr_[:, h * 128:(h + 1) * 128]
        outs.append(dk * cs - _rot_half(dk) * sn)
        return [jnp.concatenate(outs, axis=1)], accs
    return _rowwise(fn, name="mla_norm_bwd",
                    rows=[(cin, MLA_IN_PAD, 0), (dqn, MLA_QR, 0), (dkvn, MLA_KVR, 0), (dkr, MLA_HEADS * 128, 0),
                          (cosp, 128, 0), (sinp, 128, 0)],
                    pars=[gq, gkv], outs=[(MLA_IN_PAD, BF16)], accs=[MLA_QR, MLA_KVR])


def _mla_probs(q, kn, kr, i, tq):
    s = _nt(q[:, :128], kn) + _nt(q[:, 128:], kr)
    row = (i * tq + lax.broadcasted_iota(jnp.int32, s.shape, 0)) // CHUNK
    col = lax.broadcasted_iota(jnp.int32, s.shape, 1) // CHUNK
    s = jnp.where(col <= row, s, -jnp.inf)
    e = jnp.exp(s - jnp.max(s, axis=-1, keepdims=True))
    return e / jnp.sum(e, axis=-1, keepdims=True)


def _mla_attn_fwd(qr, knv, kr, tq=256):
    S = qr.shape[0]
    tq = min(tq, S)

    def body(q_ref, kn_ref, v_ref, kr_ref, o_ref):
        pr = _mla_probs(q_ref[...], kn_ref[...], kr_ref[...], pl.program_id(1), tq)
        o_ref[...] = _nn(pr.astype(BF16), v_ref[...])

    return pl.pallas_call(
        body, name="mla_attn_fwd", grid=(MLA_HEADS, S // tq),
        in_specs=[pl.BlockSpec((tq, MLA_QH), lambda h, i: (i, h)),
                  pl.BlockSpec((S, 128), lambda h, i: (0, h)),
                  pl.BlockSpec((S, 128), lambda h, i: (0, MLA_HEADS + h)),
                  pl.BlockSpec((S, 128), lambda h, i: (0, 0))],
        out_specs=pl.BlockSpec((tq, 128), lambda h, i: (i, h)),
        out_shape=jax.ShapeDtypeStruct((S, MLA_HEADS * MLA_V), F32),
        compiler_params=_cparams(("parallel", "arbitrary")),
    )(qr, knv, knv, kr)


def _mla_attn_bwd(qr, knv, kr, o, do, tq=256):
    S = qr.shape[0]
    tq = min(tq, S)
    W = MLA_HEADS * 128

    def body(q_ref, kn_ref, v_ref, kr_ref, o_ref, do_ref, dq_ref, dkn_ref, dv_ref, dkr_ref, dkn_acc, dv_acc):
        i = pl.program_id(1)
        q, kn, v, krv = q_ref[...], kn_ref[...], v_ref[...], kr_ref[...]
        pr = _mla_probs(q, kn, krv, i, tq)
        dov = do_ref[...]
        delta = jnp.sum(dov * o_ref[...], axis=-1, keepdims=True)
        dob = dov.astype(BF16)
        ds = (pr * (_nt(dob, v) - delta)).astype(BF16)
        dq_ref[...] = jnp.concatenate([_nn(ds, kn), _nn(ds, krv)], axis=1)

        @pl.when(i == 0)
        def _():
            dkn_acc[...] = jnp.zeros(dkn_acc.shape, F32)
            dv_acc[...] = jnp.zeros(dv_acc.shape, F32)
            dkr_ref[...] = jnp.zeros(dkr_ref.shape, F32)

        dkn_acc[...] += _tn(ds, q[:, :128])
        dkr_ref[...] += _tn(ds, q[:, 128:])
        dv_acc[...] += _tn(pr.astype(BF16), dob)

        @pl.when(i == pl.num_programs(1) - 1)
        def _():
            dkn_ref[...] = dkn_acc[...].astype(dkn_ref.dtype)
            dv_ref[...] = dv_acc[...].astype(dv_ref.dtype)

    return pl.pallas_call(
        body, name="mla_attn_bwd", grid=(MLA_HEADS, S // tq),
        in_specs=[pl.BlockSpec((tq, MLA_QH), lambda h, i: (i, h)),
                  pl.BlockSpec((S, 128), lambda h, i: (0, h)),
                  pl.BlockSpec((S, 128), lambda h, i: (0, MLA_HEADS + h)),
                  pl.BlockSpec((S, 128), lambda h, i: (0, 0)),
                  pl.BlockSpec((tq, 128), lambda h, i: (i, h)),
                  pl.BlockSpec((tq, 128), lambda h, i: (i, h))],
        out_specs=[pl.BlockSpec((tq, MLA_QH), lambda h, i: (i, h)),
                   pl.BlockSpec((S, 128), lambda h, i: (0, h)),
                   pl.BlockSpec((S, 128), lambda h, i: (0, h)),
                   pl.BlockSpec((S, 128), lambda h, i: (0, h))],
        out_shape=[jax.ShapeDtypeStruct((S, MLA_HEADS * MLA_QH), F32), jax.ShapeDtypeStruct((S, W), BF16),
                   jax.ShapeDtypeStruct((S, W), BF16), jax.ShapeDtypeStruct((S, W), F32)],
        scratch_shapes=[pltpu.VMEM((S, 128), F32), pltpu.VMEM((S, 128), F32)],
        compiler_params=_cparams(("parallel", "arbitrary")),
    )(qr, knv, knv, kr, o, do)


CONV_TILE = 256


def _shift_down(v, n):
    row = lax.broadcasted_iota(jnp.int32, v.shape, 0)
    return jnp.where(row >= n, pltpu.roll(v, n, 0), 0.0)


def _shift_up(v, n):
    S = v.shape[0]
    row = lax.broadcasted_iota(jnp.int32, v.shape, 0)
    return jnp.where(row < S - n, pltpu.roll(v, S - n, 0), 0.0)


def _conv_specs(S, n_extra_cols):
    nt = D_MODEL // CONV_TILE
    specs = [pl.BlockSpec((S, CONV_TILE), functools.partial(lambda j, o: (0, o + j), o=part * nt))
             for part in range(3)]
    specs.append(pl.BlockSpec((8, CONV_TILE), lambda j: (0, j)))
    specs += [pl.BlockSpec((S, CONV_TILE), lambda j: (0, j)) for _ in range(n_extra_cols)]
    return specs


def _conv_fwd(bcu, w8):
    S = bcu.shape[0]

    def body(b_ref, c_ref, u_ref, w_ref, y_ref):
        cu = c_ref[...] * u_ref[...]
        z = w_ref[2:3, :] * cu + w_ref[1:2, :] * _shift_down(cu, 1) + w_ref[0:1, :] * _shift_down(cu, 2)
        y_ref[...] = (b_ref[...] * z).astype(y_ref.dtype)

    return pl.pallas_call(
        body, name="conv_fwd", grid=(D_MODEL // CONV_TILE,), in_specs=_conv_specs(S, 0),
        out_specs=pl.BlockSpec((S, CONV_TILE), lambda j: (0, j)),
        out_shape=jax.ShapeDtypeStruct((S, D_MODEL), BF16),
        compiler_params=_cparams(("parallel",)),
    )(bcu, bcu, bcu, w8)


def _conv_bwd(bcu, w8, dy):
    S = bcu.shape[0]

    def body(b_ref, c_ref, u_ref, w_ref, dy_ref, db_ref, dc_ref, du_ref, dw_ref):
        b, c, u, dyv = b_ref[...], c_ref[...], u_ref[...], dy_ref[...]
        w0, w1, w2 = w_ref[0:1, :], w_ref[1:2, :], w_ref[2:3, :]
        cu = c * u
        cu1, cu2 = _shift_down(cu, 1), _shift_down(cu, 2)
        z = w2 * cu + w1 * cu1 + w0 * cu2
        dz = dyv * b
        db_ref[...] = (dyv * z).astype(db_ref.dtype)
        dcu = w2 * dz + w1 * _shift_up(dz, 1) + w0 * _shift_up(dz, 2)
        dc_ref[...] = (dcu * u).astype(dc_ref.dtype)
        du_ref[...] = (dcu * c).astype(du_ref.dtype)
        dw_ref[...] = jnp.zeros(dw_ref.shape, F32)
        dw_ref[0:1, :] = _colsum(dz * cu2)
        dw_ref[1:2, :] = _colsum(dz * cu1)
        dw_ref[2:3, :] = _colsum(dz * cu)

    col = pl.BlockSpec((S, CONV_TILE), lambda j: (0, j))
    return pl.pallas_call(
        body, name="conv_bwd", grid=(D_MODEL // CONV_TILE,), in_specs=_conv_specs(S, 1),
        out_specs=[col, col, col, pl.BlockSpec((8, CONV_TILE), lambda j: (0, j))],
        out_shape=[jax.ShapeDtypeStruct((S, D_MODEL), BF16)] * 3 + [jax.ShapeDtypeStruct((8, D_MODEL), F32)],
        compiler_params=_cparams(("parallel",)),
    )(bcu, bcu, bcu, w8, dy)


def _adamw(w, g, m, v, name):
    R, Cn = w.shape
    tr = R if R <= 512 else 512
    assert R % tr == 0

    def body(w_ref, g_ref, m_ref, v_ref, d_ref, nm_ref, nv_ref):
        gv = g_ref[...]
        nm = ADAM_B1 * m_ref[...] + (1.0 - ADAM_B1) * gv
        nv = ADAM_B2 * v_ref[...] + (1.0 - ADAM_B2) * jnp.square(gv)
        m_hat = nm / (1.0 - ADAM_B1 ** ADAM_STEP)
        v_hat = nv / (1.0 - ADAM_B2 ** ADAM_STEP)
        d_ref[...] = -ADAM_LR * (m_hat / (jnp.sqrt(v_hat) + ADAM_EPS) + ADAM_WD * w_ref[...])
        nm_ref[...] = nm
        nv_ref[...] = nv

    spec = pl.BlockSpec((tr, Cn), lambda i: (i, 0))
    return pl.pallas_call(
        body, name=name, grid=(R // tr,), in_specs=[spec] * 4, out_specs=[spec] * 3,
        out_shape=[jax.ShapeDtypeStruct((R, Cn), F32)] * 3,
        compiler_params=_cparams(("parallel",)),
    )(w, g, m, v)


HBM_SPEC = pl.BlockSpec(memory_space=pltpu.HBM)


def _place():
    return lax.axis_index("x"), lax.axis_index("y"), lax.axis_index("c")


def _other_chips(x, y):
    return [(1 - x, y), (x, 1 - y), (1 - x, 1 - y)]


def _all_gather(flat):
    R = flat.shape[0]
    H = R // 2

    def body(in_ref, out_ref, send_sems, recv_sems, local_sem):
        x, y, c = _place()
        q = 2 * x + y
        chips = _other_chips(x, y)
        sibling = (x, y, 1 - c)

        def region(chip, half):
            return out_ref.at[chip, pl.ds(half * H, H), :]

        def copy(k, src, dst, to):
            return pltpu.make_async_remote_copy(src_ref=src, dst_ref=dst, send_sem=send_sems.at[k],
                                                recv_sem=recv_sems.at[k], device_id=to, device_id_type=MESH)

        mine = pltpu.make_async_copy(in_ref, out_ref.at[q], local_sem)
        mine.start()
        first = [copy(j, in_ref.at[pl.ds(c * H, H), :], region(q, c), (cx, cy, c))
                 for j, (cx, cy) in enumerate(chips)]
        for cp in first:
            cp.start()
        passed = []
        for j, (cx, cy) in enumerate(chips):
            land = region(2 * cx + cy, c)
            copy(j, land, land, (cx, cy, c)).wait_recv()
            cp = copy(3 + j, land, land, sibling)
            cp.start()
            passed.append(cp)
        for j, (cx, cy) in enumerate(chips):
            land = region(2 * cx + cy, 1 - c)
            copy(3 + j, land, land, sibling).wait_recv()
        for cp in first + passed:
            cp.wait_send()
        mine.wait()

    return pl.pallas_call(
        body, name="all_gather_weights", in_specs=[HBM_SPEC], out_specs=HBM_SPEC,
        out_shape=jax.ShapeDtypeStruct((N_CHIPS, R, LANES), flat.dtype),
        scratch_shapes=[pltpu.SemaphoreType.DMA((6,)), pltpu.SemaphoreType.DMA((6,)), pltpu.SemaphoreType.DMA],
    )(flat)


def _swap_halves(g):
    R = g.shape[1]
    H = R // 2

    def body(g_ref, out_ref, send_sem, recv_sem):
        x, y, c = _place()
        cp = pltpu.make_async_remote_copy(src_ref=g_ref.at[:, pl.ds((1 - c) * H, H), :], dst_ref=out_ref,
                                          send_sem=send_sem, recv_sem=recv_sem, device_id=(x, y, 1 - c),
                                          device_id_type=MESH)
        cp.start()
        cp.wait()

    return pl.pallas_call(
        body, name="rs_swap_halves", in_specs=[HBM_SPEC], out_specs=HBM_SPEC,
        out_shape=jax.ShapeDtypeStruct((N_CHIPS, H, LANES), g.dtype),
        scratch_shapes=[pltpu.SemaphoreType.DMA, pltpu.SemaphoreType.DMA],
    )(g)


def _pair_sum(g, t, tr=512):
    H = t.shape[1]
    assert H % tr == 0
    nt = H // tr

    def body(c_ref, g_ref, t_ref, o_ref):
        o_ref[...] = (g_ref[...].astype(F32) + t_ref[...].astype(F32)).astype(o_ref.dtype)

    grid_spec = pltpu.PrefetchScalarGridSpec(
        num_scalar_prefetch=1, grid=(N_CHIPS, nt),
        in_specs=[pl.BlockSpec((None, tr, LANES), lambda j, i, c_ref: (j, c_ref[0] * nt + i, 0)),
                  pl.BlockSpec((None, tr, LANES), lambda j, i, c_ref: (j, i, 0))],
        out_specs=pl.BlockSpec((None, tr, LANES), lambda j, i, c_ref: (j, i, 0)))
    c = lax.axis_index("c").astype(jnp.int32).reshape(1)
    return pl.pallas_call(
        body, name="rs_pair_sum", grid_spec=grid_spec, out_shape=jax.ShapeDtypeStruct(t.shape, BF16),
        compiler_params=_cparams(("parallel", "parallel")),
    )(c, g, t)


def _scatter_partials(p):
    def body(p_ref, out_ref, send_sems, recv_sems, local_sem):
        x, y, c = _place()
        q = 2 * x + y
        mine = pltpu.make_async_copy(p_ref.at[q], out_ref.at[q], local_sem)
        mine.start()
        cps = [pltpu.make_async_remote_copy(src_ref=p_ref.at[2 * cx + cy], dst_ref=out_ref.at[q],
                                            send_sem=send_sems.at[j], recv_sem=recv_sems.at[j],
                                            device_id=(cx, cy, c), device_id_type=MESH)
               for j, (cx, cy) in enumerate(_other_chips(x, y))]
        for cp in cps:
            cp.start()
        for cp in cps:
            cp.wait()
        mine.wait()

    return pl.pallas_call(
        body, name="rs_scatter_partials", in_specs=[HBM_SPEC], out_specs=HBM_SPEC,
        out_shape=jax.ShapeDtypeStruct(p.shape, p.dtype),
        scratch_shapes=[pltpu.SemaphoreType.DMA((3,)), pltpu.SemaphoreType.DMA((3,)), pltpu.SemaphoreType.DMA],
    )(p)


def _chip_sum(t, tr=512):
    H = t.shape[1]
    assert H % tr == 0

    def body(t_ref, o_ref):
        acc = t_ref[0].astype(F32)
        for j in range(1, N_CHIPS):
            acc = acc + t_ref[j].astype(F32)
        o_ref[...] = acc

    return pl.pallas_call(
        body, name="rs_chip_sum", grid=(H // tr,),
        in_specs=[pl.BlockSpec((N_CHIPS, tr, LANES), lambda i: (0, i, 0))],
        out_specs=pl.BlockSpec((tr, LANES), lambda i: (i, 0)),
        out_shape=jax.ShapeDtypeStruct((H, LANES), F32),
        compiler_params=_cparams(("parallel",)),
    )(t)


def _join_halves(f):
    H = f.shape[0]

    def body(f_ref, out_ref, send_sem, recv_sem, local_sem):
        x, y, c = _place()
        mine = pltpu.make_async_copy(f_ref, out_ref.at[pl.ds(c * H, H), :], local_sem)
        mine.start()
        cp = pltpu.make_async_remote_copy(src_ref=f_ref, dst_ref=out_ref.at[pl.ds(c * H, H), :],
                                          send_sem=send_sem, recv_sem=recv_sem, device_id=(x, y, 1 - c),
                                          device_id_type=MESH)
        cp.start()
        other = out_ref.at[pl.ds((1 - c) * H, H), :]
        pltpu.make_async_remote_copy(src_ref=other, dst_ref=other, send_sem=send_sem, recv_sem=recv_sem,
                                     device_id=(x, y, 1 - c), device_id_type=MESH).wait_recv()
        cp.wait_send()
        mine.wait()

    return pl.pallas_call(
        body, name="rs_join_halves", in_specs=[HBM_SPEC], out_specs=HBM_SPEC,
        out_shape=jax.ShapeDtypeStruct((2 * H, LANES), f.dtype),
        scratch_shapes=[pltpu.SemaphoreType.DMA, pltpu.SemaphoreType.DMA, pltpu.SemaphoreType.DMA],
    )(f)


def _all_reduce_small(v):
    n = v.shape[0]

    def body(v_ref, out_ref, buf, send_sems, recv_sems):
        x, y, c = _place()
        me = 4 * x + 2 * y + c
        buf[me] = v_ref[...]
        cps = []
        for k in range(1, 8):
            peer = (x ^ (k >> 2), y ^ ((k >> 1) & 1), c ^ (k & 1))
            cp = pltpu.make_async_remote_copy(src_ref=v_ref, dst_ref=buf.at[me], send_sem=send_sems.at[k - 1],
                                              recv_sem=recv_sems.at[k - 1], device_id=peer, device_id_type=MESH)
            cp.start()
            cps.append(cp)
        for k in range(1, 8):
            px, py, pc = x ^ (k >> 2), y ^ ((k >> 1) & 1), c ^ (k & 1)
            land = buf.at[4 * px + 2 * py + pc]
            pltpu.make_async_remote_copy(src_ref=land, dst_ref=land, send_sem=send_sems.at[k - 1],
                                         recv_sem=recv_sems.at[k - 1], device_id=(px, py, pc),
                                         device_id_type=MESH).wait_recv()
        for cp in cps:
            cp.wait_send()
        acc = buf[0]
        for d in range(1, 8):
            acc = acc + buf[d]
        out_ref[...] = acc

    vm = pl.BlockSpec(memory_space=pltpu.VMEM)
    return pl.pallas_call(
        body, name="all_reduce_small", in_specs=[vm], out_specs=vm,
        out_shape=jax.ShapeDtypeStruct((n, 128), F32),
        scratch_shapes=[pltpu.VMEM((8, n, 128), F32), pltpu.SemaphoreType.DMA((7,)), pltpu.SemaphoreType.DMA((7,))],
    )(v)


def _pack_for_gather(w):
    parts = [w[n].astype(BF16).reshape(-1) for n in BIG]
    big = jnp.concatenate(parts)
    big = jnp.pad(big, (0, BIG_ROWS * LANES - big.shape[0]))
    small = jnp.concatenate([w[n].reshape(-1) for n in SMALL_SHARDED])
    small = jnp.pad(small, (0, SMALL_F32 - small.shape[0]))
    small = lax.bitcast_convert_type(small, BF16).reshape(-1)
    return jnp.concatenate([big, small]).reshape(AG_ROWS, LANES)


def _join_shards(seg, name):
    shape, ax = WSPEC[name]
    return jnp.moveaxis(seg, 0, ax).reshape(_full_shape(name))


def _unpack_gathered(g):
    flat = g.reshape(N_CHIPS, -1)
    out, off = {}, 0
    for n in BIG:
        sz = _size(WSPEC[n][0])
        out[n] = _join_shards(flat[:, off:off + sz].reshape((N_CHIPS,) + WSPEC[n][0]), n)
        off += sz
    small = lax.bitcast_convert_type(flat[:, BIG_ROWS * LANES:].reshape(N_CHIPS, SMALL_F32, 2), F32)
    off = 0
    for n in SMALL_SHARDED:
        sz = _size(WSPEC[n][0])
        out[n] = _join_shards(small[:, off:off + sz].reshape((N_CHIPS,) + WSPEC[n][0]), n)
        off += sz
    return out


def _split_shards(full, name):
    shape, ax = WSPEC[name]
    v = full.reshape(shape[:ax] + (N_CHIPS, shape[ax]) + shape[ax + 1:])
    return jnp.moveaxis(v, ax, 0).reshape(N_CHIPS, -1)


def _pack_big_grads(grads):
    flat = jnp.concatenate([_split_shards(grads[n], n).astype(BF16) for n in BIG], axis=1)
    flat = jnp.pad(flat, ((0, 0), (0, BIG_ROWS * LANES - flat.shape[1])))
    return flat.reshape(N_CHIPS, BIG_ROWS, LANES)


SMALL_FULL = sum(_size(_full_shape(n)) for n in SMALL)
SMALL_FULL_ROWS = -(-SMALL_FULL // 128 // 8) * 8


def _pack_small(vals):
    flat = jnp.concatenate([vals[n].reshape(-1) for n in SMALL])
    return jnp.pad(flat, (0, SMALL_FULL_ROWS * 128 - flat.shape[0])).reshape(SMALL_FULL_ROWS, 128)


def _unpack_small(packed, q):
    flat = packed.reshape(-1)
    out, off = {}, 0
    for n in SMALL:
        shape, ax = WSPEC[n]
        full = flat[off:off + _size(_full_shape(n))].reshape(_full_shape(n))
        off += _size(_full_shape(n))
        out[n] = full if ax is None else lax.dynamic_slice_in_dim(full, q * shape[ax], shape[ax], axis=ax)
    return out


def _row(v):
    return v.reshape(1, -1)


def _pad_cols(w, n):
    return jnp.pad(w, ((0, 0), (0, n - w.shape[1])))


def _local_step(x, p, positions, target, W):
    S = x.shape[0]
    inv_freq = ROPE_BASE ** (-jnp.arange(0, MLA_ROPE // 2, dtype=F32) * (2.0 / MLA_ROPE))
    ang = positions.astype(F32)[:, None] * inv_freq
    zeros = jnp.zeros((S, 64), F32)
    cosp = jnp.concatenate([jnp.cos(ang), jnp.cos(ang), zeros], axis=1)
    sinp = jnp.concatenate([jnp.sin(ang), jnp.sin(ang), zeros], axis=1)

    grads = {n: [None] * _full_shape(n)[0] for n in WNAMES}
    saved = []
    xin = x
    xin_b = x.astype(BF16)
    for i in range(DEPTH):
        j, kind = i // 3, i % 3
        sv = {'xin': xin, 'xin_b': xin_b}
        if kind == 0:
            w_in = _pad_cols(W['gla_w_in'][j], GLA_IN_PAD)
            w_up = jnp.pad(W['gla_w_gate_up'][j].astype(BF16), ((0, 128 - GLA_RANK), (0, 0)))
            pin = _mm(xin_b, w_in, name="gla_in", tn=640)
            la = _mm(pin, w_up, name="gla_gate", K=128, tk=128, a_off=(0, (GLA_IN_PAD - 128) // 128), tn=512,
                     extras=[(_row(W['gla_b_gate'][j]), 'n')],
                     epilogue=lambda acc, b: (_log_sigmoid(acc + b) * (1.0 / GLA_TAU),))
            o, states = _gla_fwd(pin, la)
            yb = _gla_post_fwd(o, pin, _row(W['gla_norm_g'][j]))
            h = _mm(yb, W['gla_w_out'][j], name="mix_out")
            sv.update(w_in=w_in, w_up=w_up, pin=pin, la=la, o=o, states=states, yb=yb)
        elif kind == 1:
            w_in = _pad_cols(W['mla_w_in'][j], MLA_IN_PAD)
            w_uq = jnp.pad(W['mla_w_uq'][j].reshape(MLA_QR, MLA_HEADS, MLA_NOPE + MLA_ROPE),
                           ((0, 0), (0, 0), (0, MLA_QH - MLA_NOPE - MLA_ROPE))).reshape(MLA_QR, MLA_HEADS * MLA_QH)
            w_ukv = W['mla_w_ukv'][j].reshape(MLA_KVR, MLA_HEADS, 2, 128).transpose(0, 2, 1, 3).reshape(MLA_KVR, -1)
            gq, gkv = W['mla_q_norm'][j:j + 1], W['mla_kv_norm'][j:j + 1]
            cin = _mm(xin_b, w_in, name="mla_in", tn=640)
            qn, kvn, kr = _mla_norm_fwd(cin, gq, gkv, cosp, sinp)
            qr = _mla_qrope_fwd(_mm(qn, w_uq, name="mla_uq"), cosp, sinp)
            knv = _mm(kvn, w_ukv, name="mla_ukv", out_dtypes=(BF16,))
            o = _mla_attn_fwd(qr, knv, kr)
            ob = o.astype(BF16)
            h = _mm(ob, W['mla_w_out'][j], name="mix_out")
            sv.update(w_in=w_in, w_uq=w_uq, w_ukv=w_ukv, gq=gq, gkv=gkv, cin=cin, qn=qn, kvn=kvn, kr=kr, qr=qr,
                      knv=knv, o=o, ob=ob)
        else:
            w8 = jnp.pad(W['conv_w'][j], ((0, 5), (0, 0)))
            bcu = _mm(xin_b, W['conv_w_in'][j], name="conv_in", tn=768)
            yb = _conv_fwd(bcu, w8)
            h = _mm(yb, W['conv_w_out'][j], name="mix_out")
            sv.update(w8=w8, bcu=bcu, yb=yb)
        g0, b0 = _row(W['ln_g'][i, 0]), _row(W['ln_b'][i, 0])
        g1, b1 = _row(W['ln_g'][i, 1]), _row(W['ln_b'][i, 1])
        x1, x1b = _ln_fwd(xin, h, g0, b0, "ln_fwd")
        ub, ab = _mm(x1b, W['mlp_w1'][i], name="mlp_up", out_dtypes=(BF16, BF16),
                     epilogue=lambda acc: (acc, jnp.square(jnp.maximum(acc, 0.0))))
        m = _mm(ab, W['mlp_w2'][i], name="mlp_down")
        x2, x2b = _ln_fwd(x1, m, g1, b1, "ln_fwd")
        pp = _mm(p[i], W['ple_w_proj'][i], name="ple_proj")
        z, x3, x3b = _mm(x2b, W['ple_w_gate'][i], name="ple_gate", out_dtypes=(F32, F32, BF16),
                         extras=[(x2, 'mn'), (pp, 'mn')],
                         epilogue=lambda acc, xv, pv: (acc,) + (xv + jax.nn.sigmoid(acc) * pv,) * 2)
        sv.update(h=h, x1=x1, x1b=x1b, ub=ub, ab=ab, m=m, x2b=x2b, pp=pp, z=z, g0=g0, g1=g1)
        saved.append(sv)
        xin, xin_b = x3, x3b

    dx, loss_cols = _loss_head(xin, target)
    loss = jnp.sum(loss_cols[0])

    for i in reversed(range(DEPTH)):
        j, kind = i // 3, i % 3
        sv = saved[i]
        dpp_b, dz_b = _ple_bwd_gate(dx, sv['z'], sv['pp'])
        grads['ple_w_proj'][i] = _mm(p[i], dpp_b, ta=True, name="ple_proj_dw")
        grads['ple_w_gate'][i] = _mm(sv['x2b'], dz_b, ta=True, name="dw_dd")
        dx2 = _mm(dz_b, W['ple_w_gate'][i], tb=True, name="dx_dd_add", extras=[(dx, 'mn')],
                  epilogue=lambda acc, r: (acc + r,))
        dv1, dv1b, dg1, db1 = _ln_bwd(sv['x1'], sv['m'], sv['g1'], dx2, "ln_bwd")
        grads['mlp_w2'][i] = _mm(sv['ab'], dv1b, ta=True, name="mlp_down_dw")
        dub = _mm(dv1b, W['mlp_w2'][i], tb=True, name="mlp_down_dx", out_dtypes=(BF16,), extras=[(sv['ub'], 'mn')],
                  epilogue=lambda acc, u: (acc * (2.0 * jnp.maximum(u.astype(F32), 0.0)),))
        grads['mlp_w1'][i] = _mm(sv['x1b'], dub, ta=True, name="mlp_up_dw")
        dx1 = _mm(dub, W['mlp_w1'][i], tb=True, name="mlp_up_dx", extras=[(dv1, 'mn')],
                  epilogue=lambda acc, r: (acc + ALPHA * r,))
        dv0, dv0b, dg0, db0 = _ln_bwd(sv['xin'], sv['h'], sv['g0'], dx1, "ln_bwd")
        grads['ln_g'][i] = jnp.stack([dg0[0], dg1[0]])
        grads['ln_b'][i] = jnp.stack([db0[0], db1[0]])
        resid = dict(extras=[(dv0, 'mn')], epilogue=lambda acc, r: (acc + ALPHA * r,))
        if kind == 0:
            grads['gla_w_out'][j] = _mm(sv['yb'], dv0b, ta=True, name="dw_dd")
            dy = _mm(dv0b, W['gla_w_out'][j], tb=True, name="dx_dd")
            do, dr_b, dng = _gla_post_bwd(dy, sv['o'], sv['pin'], _row(W['gla_norm_g'][j]))
            dq_b, dk_b, dvv_b, dla = _gla_bwd(sv['pin'], sv['la'], sv['states'], do)
            dzg_b, dbg = _gla_gate_bwd(dla, sv['la'])
            dw_up = _mm(sv['pin'], dzg_b, ta=True, name="gla_gate_dw", M=128, tm=128,
                        a_off=(0, (GLA_IN_PAD - 128) // 128))
            dglr_b = _mm(dzg_b, sv['w_up'], tb=True, name="gla_gate_dx", out_dtypes=(BF16,))
            dpin_b = jnp.concatenate([dq_b, dk_b, dvv_b, dr_b, dglr_b], axis=1)
            dw_in = _mm(sv['xin_b'], dpin_b, ta=True, name="gla_in_dw", tn=640)
            dx = _mm(dpin_b, sv['w_in'], tb=True, name="gla_in_dx", tk=640, **resid)
            grads['gla_w_in'][j] = dw_in[:, :GLA_IN]
            grads['gla_w_gate_up'][j] = dw_up[:GLA_RANK]
            grads['gla_b_gate'][j] = dbg[0]
            grads['gla_norm_g'][j] = dng[0]
        elif kind == 1:
            grads['mla_w_out'][j] = _mm(sv['ob'], dv0b, ta=True, name="dw_dd")
            do = _mm(dv0b, W['mla_w_out'][j], tb=True, name="dx_dd")
            dqr, dkn_b, dvv_b, dkr = _mla_attn_bwd(sv['qr'], sv['knv'], sv['kr'], sv['o'], do)
            dq_b = _mla_qrope_bwd(dqr, cosp, sinp)
            dw_uq = _mm(sv['qn'], dq_b, ta=True, name="mla_uq_dw")
            dqn = _mm(dq_b, sv['w_uq'], tb=True, name="mla_uq_dx")
            dknv_b = jnp.concatenate([dkn_b, dvv_b], axis=1)
            dw_ukv = _mm(sv['kvn'], dknv_b, ta=True, name="mla_uq_dw")
            dkvn = _mm(dknv_b, sv['w_ukv'], tb=True, name="mla_uq_dx")
            dcin_b, dgq, dgkv = _mla_norm_bwd(sv['cin'], dqn, dkvn, dkr, sv['gq'], sv['gkv'], cosp, sinp)
            dw_in = _mm(sv['xin_b'], dcin_b, ta=True, name="mla_in_dw", tn=640)
            dx = _mm(dcin_b, sv['w_in'], tb=True, name="mla_in_dx", tk=640, **resid)
            grads['mla_w_in'][j] = dw_in[:, :MLA_IN]
            grads['mla_w_uq'][j] = dw_uq.reshape(MLA_QR, MLA_HEADS, MLA_QH)[:, :, :MLA_NOPE + MLA_ROPE].reshape(
                MLA_QR, -1)
            grads['mla_w_ukv'][j] = dw_ukv.reshape(MLA_KVR, 2, MLA_HEADS, 128).transpose(0, 2, 1, 3).reshape(
                MLA_KVR, -1)
            grads['mla_q_norm'][j] = dgq[0]
            grads['mla_kv_norm'][j] = dgkv[0]
        else:
            grads['conv_w_out'][j] = _mm(sv['yb'], dv0b, ta=True, name="dw_dd")
            dy = _mm(dv0b, W['conv_w_out'][j], tb=True, name="dx_dd")
            db_b, dc_b, du_b, dw8 = _conv_bwd(sv['bcu'], sv['w8'], dy)
            dbcu_b = jnp.concatenate([db_b, dc_b, du_b], axis=1)
            grads['conv_w_in'][j] = _mm(sv['xin_b'], dbcu_b, ta=True, name="conv_in_dw", tn=768)
            dx = _mm(dbcu_b, W['conv_w_in'][j], tb=True, name="conv_in_dx", tk=768, **resid)
            grads['conv_w'][j] = dw8[:3]

    return loss, dx, {n: jnp.stack(v) for n, v in grads.items()}


def _gather_weights(w):
    full = _unpack_gathered(_all_gather(_pack_for_gather(w)))
    full['mla_q_norm'], full['mla_kv_norm'] = w['mla_q_norm'], w['mla_kv_norm']
    return full


def _reduce_grads(gfull, q):
    gpack = _pack_big_grads(gfull)
    partial = _pair_sum(gpack, _swap_halves(gpack))
    gflat = _join_halves(_chip_sum(_scatter_partials(partial))).reshape(-1)
    g, off = {}, 0
    for n in BIG:
        sz = _size(WSPEC[n][0])
        g[n] = gflat[off:off + sz].reshape(WSPEC[n][0])
        off += sz
    g.update(_unpack_small(_all_reduce_small(_pack_small(gfull)), q))
    return g


def _adamw_all(w, g, m, v):
    delta, new_m, new_v = {}, {}, {}
    for n in BIG:
        shape = WSPEC[n][0]
        r2 = lambda a: a.reshape(-1, shape[-1])
        d, nm, nv = _adamw(r2(w[n]), r2(g[n]), r2(m[n]), r2(v[n]), name="adamw_" + n)
        delta[n], new_m[n], new_v[n] = d.reshape(shape), nm.reshape(shape), nv.reshape(shape)
    total = sum(_size(WSPEC[n][0]) for n in SMALL)
    rows = -(-total // 128 // 8) * 8

    def pack(dct):
        flat = jnp.concatenate([dct[n].reshape(-1) for n in SMALL])
        return jnp.pad(flat, (0, rows * 128 - total), constant_values=1.0).reshape(rows, 128)

    res = _adamw(pack(w), pack(g), pack(m), pack(v), name="adamw_small")
    for out, packed in zip((delta, new_m, new_v), res):
        flat, off = packed.reshape(-1), 0
        for n in SMALL:
            sz = _size(WSPEC[n][0])
            out[n] = flat[off:off + sz].reshape(WSPEC[n][0])
            off += sz
    return delta, new_m, new_v


def kernel(x, p, positions, gla_w_in, gla_w_gate_up, gla_b_gate, gla_norm_g, gla_w_out, mla_w_in, mla_q_norm, mla_kv_norm, mla_w_uq, mla_w_ukv, mla_w_out, conv_w_in, conv_w, conv_w_out, ln_g, ln_b, mlp_w1, mlp_w2, ple_w_gate, ple_w_proj, loss_target, m_gla_w_in, m_gla_w_gate_up, m_gla_b_gate, m_gla_norm_g, m_gla_w_out, m_mla_w_in, m_mla_q_norm, m_mla_kv_norm, m_mla_w_uq, m_mla_w_ukv, m_mla_w_out, m_conv_w_in, m_conv_w, m_conv_w_out, m_ln_g, m_ln_b, m_mlp_w1, m_mlp_w2, m_ple_w_gate, m_ple_w_proj, v_gla_w_in, v_gla_w_gate_up, v_gla_b_gate, v_gla_norm_g, v_gla_w_out, v_mla_w_in, v_mla_q_norm, v_mla_kv_norm, v_mla_w_uq, v_mla_w_ukv, v_mla_w_out, v_conv_w_in, v_conv_w, v_conv_w_out, v_ln_g, v_ln_b, v_mlp_w1, v_mlp_w2, v_ple_w_gate, v_ple_w_proj):
    args = locals()
    w = {n: args[n] for n in WNAMES}
    m = {n: args['m_' + n] for n in WNAMES}
    v = {n: args['v_' + n] for n in WNAMES}
    q = 2 * lax.axis_index("x") + lax.axis_index("y")

    full = _gather_weights(w)
    loss, grad_x, gfull = _local_step(x[0], p[:, 0], positions[0], loss_target[0], full)
    loss = lax.psum(loss, ("x", "y", "c"))
    g = _reduce_grads(gfull, q)
    delta, new_m, new_v = _adamw_all(w, g, m, v)
    return (loss, grad_x[None], *[g[n] for n in WNAMES], *[delta[n] for n in WNAMES],
            *[new_m[n] for n in WNAMES], *[new_v[n] for n in WNAMES])
```

```python
import functools

import numpy as np
import jax
import jax.numpy as jnp
from jax import lax
from jax.experimental import pallas as pl
from jax.experimental.pallas import tpu as pltpu

F32 = jnp.float32
BF16 = jnp.bfloat16
MESH = pl.DeviceIdType.MESH

D_MODEL = 1024
DEPTH = 4
CHUNK = 64
ALPHA = (2 * DEPTH) ** 0.25
LN_EPS = 1e-5
RMS_EPS = 1e-6
PLE_DIM = 256
D_FF = 4 * D_MODEL
GLA_HEADS = 4
GLA_DK = 128
GLA_DV = 256
GLA_RANK = 16
GLA_TAU = 16.0
GLA_HK = GLA_HEADS * GLA_DK
GLA_HV = GLA_HEADS * GLA_DV
GLA_IN = 2 * GLA_HK + GLA_HV + D_MODEL + GLA_RANK
GLA_IN_PAD = 2 * GLA_HK + GLA_HV + D_MODEL + 128
GLA_SHARD = GLA_IN // 4
GLA_WIN = 896
GLA_WIN_STEP = 768
MLA_HEADS = 8
MLA_NOPE = 128
MLA_ROPE = 64
MLA_V = 128
MLA_QR = 256
MLA_KVR = 256
MLA_IN = MLA_QR + MLA_KVR + MLA_ROPE
MLA_IN_PAD = MLA_QR + MLA_KVR + 128
MLA_QH = 256
ROPE_BASE = 10000.0
ADAM_LR = 0.001
ADAM_B1 = 0.9
ADAM_B2 = 0.999
ADAM_EPS = 1e-08
ADAM_WD = 0.01
ADAM_STEP = 10

VMEM_LIMIT = 48 * 1024 * 1024
N_CHIPS = 4

WSPEC = {
    'gla_w_in': ((2, 1024, 772), 2), 'gla_w_gate_up': ((2, 16, 128), 2), 'gla_b_gate': ((2, 128), 1),
    'gla_norm_g': ((2, 64), 1), 'gla_w_out': ((2, 256, 1024), 1), 'mla_w_in': ((1, 256, 576), 1),
    'mla_q_norm': ((1, 256), None), 'mla_kv_norm': ((1, 256), None), 'mla_w_uq': ((1, 256, 384), 2),
    'mla_w_ukv': ((1, 256, 512), 2), 'mla_w_out': ((1, 256, 1024), 1), 'conv_w_in': ((1, 1024, 768), 2),
    'conv_w': ((1, 3, 256), 2), 'conv_w_out': ((1, 256, 1024), 1), 'ln_g': ((4, 2, 256), 2),
    'ln_b': ((4, 2, 256), 2), 'mlp_w1': ((4, 1024, 1024), 2), 'mlp_w2': ((4, 1024, 1024), 1),
    'ple_w_gate': ((4, 256, 1024), 1), 'ple_w_proj': ((4, 256, 256), 2),
}
WNAMES = list(WSPEC)
BIG = ['gla_w_in', 'gla_w_out', 'mla_w_in', 'mla_w_uq', 'mla_w_ukv', 'mla_w_out', 'conv_w_in', 'conv_w_out',
       'mlp_w1', 'mlp_w2', 'ple_w_gate', 'ple_w_proj']
SMALL_SHARDED = ['gla_w_gate_up', 'gla_b_gate', 'gla_norm_g', 'conv_w', 'ln_g', 'ln_b']
SMALL = SMALL_SHARDED + ['mla_q_norm', 'mla_kv_norm']
MIXER = ['gla', 'mla', 'conv']
LAYER_BIG = {'gla': ['gla_w_in', 'gla_w_out'], 'mla': ['mla_w_in', 'mla_w_uq', 'mla_w_ukv', 'mla_w_out'],
             'conv': ['conv_w_in', 'conv_w_out']}
COMMON_BIG = ['mlp_w1', 'mlp_w2', 'ple_w_gate', 'ple_w_proj']


def _size(shape):
    return int(np.prod(shape))


def _full_shape(name):
    shape, ax = WSPEC[name]
    if ax is None:
        return shape
    return tuple(s * N_CHIPS if i == ax else s for i, s in enumerate(shape))


def _cparams(sem=None):
    return pltpu.CompilerParams(dimension_semantics=sem, vmem_limit_bytes=VMEM_LIMIT)


def _mm(a, b, *, name, ta=False, tb=False, M=None, N=None, K=None, out_dtypes=(F32,), epilogue=None, extras=(),
        tm=1024, tn=512, tk=1024, a_off=(0, 0), b_sh=False, out_sh=False):
    if M is None:
        M = a.shape[1] if ta else a.shape[0]
    if K is None:
        K = a.shape[0] if ta else a.shape[1]
    if b_sh:
        kw, nq = b.shape[1], b.shape[2]
        n_b, k_b = (kw, N_CHIPS * nq) if tb else (N_CHIPS * nq, kw)
        N = n_b if N is None else N
        assert K == k_b
    elif N is None:
        N = b.shape[0] if tb else b.shape[1]
    tm, tn, tk = min(tm, M), min(tn, N), min(tk, K)
    assert M % tm == 0 and N % tn == 0 and K % tk == 0, (name, M, N, K, tm, tn, tk)
    nk = K // tk
    n_ex, n_out = len(extras), len(out_dtypes)

    def body(a_ref, b_ref, *rest):
        ex_refs, out_refs = rest[:n_ex], rest[n_ex:n_ex + n_out]
        part = lax.dot_general(a_ref[...].astype(BF16), b_ref[...].astype(BF16),
                               ((((0,) if ta else (1,)), ((1,) if tb else (0,))), ((), ())),
                               preferred_element_type=F32)

        def finish(acc):
            res = (acc,) if epilogue is None else epilogue(acc, *[r[...] for r in ex_refs])
            for r, v in zip(out_refs, res):
                r[...] = v.astype(r.dtype)

        if nk == 1:
            finish(part)
        else:
            acc_ref = rest[-1]
            k = pl.program_id(2)

            @pl.when(k == 0)
            def _():
                acc_ref[...] = part

            @pl.when(k > 0)
            def _():
                acc_ref[...] += part

            @pl.when(k == nk - 1)
            def _():
                finish(acc_ref[...])

    if ta:
        a_spec = pl.BlockSpec((tk, tm), lambda i, j, k: (k + a_off[0], i + a_off[1]))
    else:
        a_spec = pl.BlockSpec((tm, tk), lambda i, j, k: (i + a_off[0], k + a_off[1]))
    if b_sh and tb:
        assert nq % tk == 0
        per = nq // tk
        b_spec = pl.BlockSpec((None, tn, tk), lambda i, j, k: (k // per, j, k % per))
    elif b_sh:
        assert nq % tn == 0
        per = nq // tn
        b_spec = pl.BlockSpec((None, tk, tn), lambda i, j, k: (j // per, k, j % per))
    elif tb:
        b_spec = pl.BlockSpec((tn, tk), lambda i, j, k: (j, k))
    else:
        b_spec = pl.BlockSpec((tk, tn), lambda i, j, k: (k, j))
    ex_specs = []
    for arr, kind in extras:
        if kind == 'mn':
            ex_specs.append(pl.BlockSpec((tm, tn), lambda i, j, k: (i, j)))
        else:
            ex_specs.append(pl.BlockSpec((1, tn), lambda i, j, k: (0, j)))
    if out_sh:
        assert (N // N_CHIPS) % tn == 0
        per_o = N // N_CHIPS // tn
        o_spec = pl.BlockSpec((None, tm, tn), lambda i, j, k: (j // per_o, i, j % per_o))
        o_shape = (N_CHIPS, M, N // N_CHIPS)
    else:
        o_spec = pl.BlockSpec((tm, tn), lambda i, j, k: (i, j))
        o_shape = (M, N)
    outs = pl.pallas_call(
        body, name=name, grid=(M // tm, N // tn, nk),
        in_specs=[a_spec, b_spec] + ex_specs,
        out_specs=[o_spec for _ in out_dtypes],
        out_shape=[jax.ShapeDtypeStruct(o_shape, d) for d in out_dtypes],
        scratch_shapes=[pltpu.VMEM((tm, tn), F32)] if nk > 1 else [],
        compiler_params=_cparams(("parallel", "parallel", "arbitrary")),
    )(a, b, *[e[0] for e in extras])
    return outs[0] if n_out == 1 else tuple(outs)


def _rowwise(fn, *, name, rows, pars=(), outs=(), accs=(), tm=256):
    S = rows[0][0].shape[0]
    tm = min(tm, S)
    assert S % tm == 0
    n_r, n_p, n_o, n_a = len(rows), len(pars), len(outs), len(accs)

    def body(*refs):
        r_refs, p_refs = refs[:n_r], refs[n_r:n_r + n_p]
        o_refs, a_refs = refs[n_r + n_p:n_r + n_p + n_o], refs[n_r + n_p + n_o:]
        o_vals, a_vals = fn([r[...] for r in r_refs], [p[...] for p in p_refs])
        for r, v in zip(o_refs, o_vals):
            r[...] = v.astype(r.dtype)
        if n_a:
            i = pl.program_id(0)

            @pl.when(i == 0)
            def _():
                for r in a_refs:
                    r[...] = jnp.zeros(r.shape, r.dtype)

            for r, v in zip(a_refs, a_vals):
                r[...] += jnp.broadcast_to(v, r.shape)

    in_specs = [pl.BlockSpec((tm, w), functools.partial(lambda i, o: (i, o), o=off)) for _, w, off in rows]
    in_specs += [pl.BlockSpec(p.shape, functools.partial(lambda i, nd: (0,) * nd, nd=p.ndim)) for p in pars]
    out_specs = [pl.BlockSpec((tm, w), lambda i: (i, 0)) for w, _ in outs]
    out_specs += [pl.BlockSpec((8, w), lambda i: (0, 0)) for w in accs]
    out_shape = [jax.ShapeDtypeStruct((S, w), d) for w, d in outs]
    out_shape += [jax.ShapeDtypeStruct((8, w), F32) for w in accs]
    res = pl.pallas_call(
        body, name=name, grid=(S // tm,), in_specs=in_specs, out_specs=out_specs, out_shape=out_shape,
        compiler_params=_cparams(("arbitrary",)),
    )(*[r[0] for r in rows], *pars)
    return tuple(res)


def _colsum(v):
    return jnp.sum(v, axis=0, keepdims=True)


def _ln_stats(v):
    mu = jnp.mean(v, axis=-1, keepdims=True)
    d = v - mu
    var = jnp.mean(d * d, axis=-1, keepdims=True)
    rstd = lax.rsqrt(var + LN_EPS)
    return d * rstd, rstd


def _ln_fwd(x, h, g, b, name):
    def fn(r, p):
        xhat, _ = _ln_stats(ALPHA * r[0] + r[1])
        y = xhat * p[0] + p[1]
        return [y, y], []
    return _rowwise(fn, name=name, rows=[(x, D_MODEL, 0), (h, D_MODEL, 0)], pars=[g, b],
                    outs=[(D_MODEL, F32), (D_MODEL, BF16)])


def _ln_bwd(x, h, g, dy, name):
    def fn(r, p):
        xhat, rstd = _ln_stats(ALPHA * r[0] + r[1])
        dyv = r[2]
        dxh = dyv * p[0]
        m1 = jnp.mean(dxh, axis=-1, keepdims=True)
        m2 = jnp.mean(dxh * xhat, axis=-1, keepdims=True)
        dv = rstd * (dxh - m1 - xhat * m2)
        return [dv, dv], [_colsum(dyv * xhat), _colsum(dyv)]
    return _rowwise(fn, name=name, rows=[(x, D_MODEL, 0), (h, D_MODEL, 0), (dy, D_MODEL, 0)], pars=[g],
                    outs=[(D_MODEL, F32), (D_MODEL, BF16)], accs=[D_MODEL, D_MODEL])


def _loss_head(y, t):
    def fn(r, p):
        d = r[0] - r[1]
        return [d * (1.0 / D_MODEL)], [_colsum(d * d) * (0.5 / D_MODEL)]
    return _rowwise(fn, name="loss_head", rows=[(y, D_MODEL, 0), (t, D_MODEL, 0)], outs=[(D_MODEL, F32)],
                    accs=[D_MODEL])


def _ple_bwd_gate(dx3, z, pp):
    def fn(r, p):
        s = jax.nn.sigmoid(r[1])
        return [r[0] * s, r[0] * r[2] * s * (1.0 - s)], []
    return _rowwise(fn, name="ple_bwd_gate", rows=[(dx3, D_MODEL, 0), (z, D_MODEL, 0), (pp, D_MODEL, 0)],
                    outs=[(D_MODEL, BF16), (D_MODEL, BF16)])


N_LEVELS = 6


def _gla_consts():
    C = CHUNK
    A = np.zeros((N_LEVELS + 3, C, C), np.float32)
    masks = np.zeros((N_LEVELS + 1, C, C), np.float32)
    r = np.arange(C)[:, None]
    u = np.arange(C)[None, :]
    for l in range(N_LEVELS):
        half = C >> (l + 1)
        mid = (r // (2 * half)) * (2 * half) + half - 1
        A[l] = np.where(r > mid, (u > mid) & (u <= r), (u > r) & (u <= mid))
        masks[l] = ((r // (2 * half)) == (u // (2 * half))) & (((r // half) % 2) != ((u // half) % 2))
    masks[N_LEVELS] = (r == u)
    A[N_LEVELS] = (u <= r)
    A[N_LEVELS + 1] = (u > r)
    A[N_LEVELS + 2] = 1.0
    A = A.reshape(-1, C)
    return A, np.ascontiguousarray(A.T), masks


def _split3(v):
    hi = v.astype(BF16)
    r1 = v - hi.astype(F32)
    mid = r1.astype(BF16)
    lo = (r1 - mid.astype(F32)).astype(BF16)
    return hi, mid, lo


def _dot_exact01(a01, v):
    hi, mid, lo = _split3(v)
    f = lambda p: jnp.dot(a01, p, preferred_element_type=F32)
    return f(hi) + f(mid) + f(lo)


def _nt(a, b):
    return lax.dot_general(a, b, (((1,), (1,)), ((), ())), preferred_element_type=F32)


def _tn(a, b):
    return lax.dot_general(a, b, (((0,), (0,)), ((), ())), preferred_element_type=F32)


def _nn(a, b):
    return jnp.dot(a, b, preferred_element_type=F32)


def _gla_chunk_terms(q, k, la, a_ref, m_ref):
    C = CHUNK
    E = jnp.exp(_dot_exact01(a_ref[...], la))
    scores = m_ref[N_LEVELS] * _nt(q.astype(BF16), k.astype(BF16))
    qes, kes = [], []
    for l in range(N_LEVELS):
        El = E[l * C:(l + 1) * C]
        qe, ke = (q * El).astype(BF16), (k * El).astype(BF16)
        qes.append(qe)
        kes.append(ke)
        scores = scores + m_ref[l] * _nt(qe, ke)
    return E, qes, kes, scores


def _gla_fwd(pin, la):
    S = pin.shape[0]
    NC = S // CHUNK
    C = CHUNK
    A, _, masks = _gla_consts()

    def body(q_ref, k_ref, v_ref, la_ref, a_ref, m_ref, o_ref, st_ref, state):
        c = pl.program_id(1)

        @pl.when(c == 0)
        def _():
            state[...] = jnp.zeros(state.shape, F32)

        q = q_ref[...] * (GLA_DK ** -0.5)
        k, v, la_c = k_ref[...], v_ref[...], la_ref[...]
        E, _, _, scores = _gla_chunk_terms(q, k, la_c, a_ref, m_ref)
        Eq, Ek, Ee = E[6 * C:7 * C], E[7 * C:8 * C], E[8 * C:9 * C]
        st = state[...]
        st_ref[...] = st
        vb = v.astype(BF16)
        o_ref[...] = _nn(scores.astype(BF16), vb) + _nt((q * Eq).astype(BF16), st.astype(BF16))
        state[...] = st * jnp.concatenate([Ee] * (GLA_DV // C), axis=0) + _tn(vb, (k * Ek).astype(BF16))

    nkb = GLA_HK // GLA_DK
    return pl.pallas_call(
        body, name="gla_fwd", grid=(GLA_HEADS, NC),
        in_specs=[pl.BlockSpec((C, GLA_DK), lambda h, c: (c, h)),
                  pl.BlockSpec((C, GLA_DK), lambda h, c: (c, nkb + h)),
                  pl.BlockSpec((C, GLA_DV), lambda h, c: (c, 2 * GLA_HK // GLA_DV + h)),
                  pl.BlockSpec((C, GLA_DK), lambda h, c: (c, h)),
                  pl.BlockSpec(A.shape, lambda h, c: (0, 0)),
                  pl.BlockSpec(masks.shape, lambda h, c: (0, 0, 0))],
        out_specs=[pl.BlockSpec((C, GLA_DV), lambda h, c: (c, h)),
                   pl.BlockSpec((None, None, GLA_DV, GLA_DK), lambda h, c: (h, c, 0, 0))],
        out_shape=[jax.ShapeDtypeStruct((S, GLA_HV), F32),
                   jax.ShapeDtypeStruct((GLA_HEADS, NC, GLA_DV, GLA_DK), F32)],
        scratch_shapes=[pltpu.VMEM((GLA_DV, GLA_DK), F32)],
        compiler_params=_cparams(("parallel", "arbitrary")),
    )(pin, pin, pin, la, jnp.asarray(A, BF16), jnp.asarray(masks))


def _gla_bwd(pin, la, states, do):
    S = pin.shape[0]
    NC = S // CHUNK
    C = CHUNK
    A, AT, masks = _gla_consts()
    scale = GLA_DK ** -0.5

    def body(q_ref, k_ref, v_ref, la_ref, st_ref, do_ref, a_ref, at_ref, m_ref,
             dq_ref, dk_ref, dv_ref, dla_ref, dstate):
        c = pl.program_id(1)

        @pl.when(c == 0)
        def _():
            dstate[...] = jnp.zeros(dstate.shape, F32)

        q = q_ref[...] * scale
        k, v, la_c, st, dov = k_ref[...], v_ref[...], la_ref[...], st_ref[...], do_ref[...]
        E, qes, kes, scores = _gla_chunk_terms(q, k, la_c, a_ref, m_ref)
        Eq, Ek, Ee = E[6 * C:7 * C], E[7 * C:8 * C], E[8 * C:9 * C]
        dst = dstate[...]
        dob, vb, dstb = dov.astype(BF16), v.astype(BF16), dst.astype(BF16)
        qEq, kEk = (q * Eq).astype(BF16), (k * Ek).astype(BF16)
        dsc = _nt(dob, vb)
        dv_ref[...] = (_tn(scores.astype(BF16), dob) + _nt(kEk, dstb)).astype(dv_ref.dtype)
        dqEq = _nn(dob, st.astype(BF16))
        dkEk = _nn(vb, dstb)
        Gd = (m_ref[N_LEVELS] * dsc).astype(BF16)
        dq = _nn(Gd, k.astype(BF16)) + dqEq * Eq
        dk = _tn(Gd, q.astype(BF16)) + dkEk * Ek
        dX = []
        for l in range(N_LEVELS):
            El = E[l * C:(l + 1) * C]
            G = (m_ref[l] * dsc).astype(BF16)
            dqe, dke = _nn(G, kes[l]), _tn(G, qes[l])
            dq = dq + dqe * El
            dk = dk + dke * El
            dX.append((dqe * q + dke * k) * El)
        dX.append(dqEq * q * Eq)
        dX.append(dkEk * k * Ek)
        prod = dst * st
        dEe = prod[0:C]
        for i in range(1, GLA_DV // C):
            dEe = dEe + prod[i * C:(i + 1) * C]
        dX.append(dEe * Ee)
        dla_ref[...] = _dot_exact01(at_ref[...], jnp.concatenate(dX, axis=0))
        dq_ref[...] = (dq * scale).astype(dq_ref.dtype)
        dk_ref[...] = dk.astype(dk_ref.dtype)
        dstate[...] = dst * jnp.concatenate([Ee] * (GLA_DV // C), axis=0) + _tn(dob, qEq)

    nkb = GLA_HK // GLA_DK
    rc = lambda c: NC - 1 - c
    return pl.pallas_call(
        body, name="gla_bwd", grid=(GLA_HEADS, NC),
        in_specs=[pl.BlockSpec((C, GLA_DK), lambda h, c: (rc(c), h)),
                  pl.BlockSpec((C, GLA_DK), lambda h, c: (rc(c), nkb + h)),
                  pl.BlockSpec((C, GLA_DV), lambda h, c: (rc(c), 2 * GLA_HK // GLA_DV + h)),
                  pl.BlockSpec((C, GLA_DK), lambda h, c: (rc(c), h)),
                  pl.BlockSpec((None, None, GLA_DV, GLA_DK), lambda h, c: (h, rc(c), 0, 0)),
                  pl.BlockSpec((C, GLA_DV), lambda h, c: (rc(c), h)),
                  pl.BlockSpec(A.shape, lambda h, c: (0, 0)),
                  pl.BlockSpec(AT.shape, lambda h, c: (0, 0)),
                  pl.BlockSpec(masks.shape, lambda h, c: (0, 0, 0))],
        out_specs=[pl.BlockSpec((C, GLA_DK), lambda h, c: (rc(c), h)),
                   pl.BlockSpec((C, GLA_DK), lambda h, c: (rc(c), h)),
                   pl.BlockSpec((C, GLA_DV), lambda h, c: (rc(c), h)),
                   pl.BlockSpec((C, GLA_DK), lambda h, c: (rc(c), h))],
        out_shape=[jax.ShapeDtypeStruct((S, GLA_HK), BF16), jax.ShapeDtypeStruct((S, GLA_HK), BF16),
                   jax.ShapeDtypeStruct((S, GLA_HV), BF16), jax.ShapeDtypeStruct((S, GLA_HK), F32)],
        scratch_shapes=[pltpu.VMEM((GLA_DV, GLA_DK), F32)],
        compiler_params=_cparams(("parallel", "arbitrary")),
    )(pin, pin, pin, la, states, do, jnp.asarray(A, BF16), jnp.asarray(AT, BF16), jnp.asarray(masks))


def _gla_post_fwd(o, pin, g):
    def fn(r, p):
        ov, rv = r
        ys = []
        for h in range(GLA_HEADS):
            oh = ov[:, h * GLA_DV:(h + 1) * GLA_DV]
            rh = rv[:, h * GLA_DV:(h + 1) * GLA_DV]
            rs = lax.rsqrt(jnp.mean(oh * oh, axis=-1, keepdims=True) + RMS_EPS)
            ys.append(oh * rs * p[0] * (rh * jax.nn.sigmoid(rh)))
        return [jnp.concatenate(ys, axis=1)], []
    return _rowwise(fn, name="gla_post_fwd", rows=[(o, GLA_HV, 0), (pin, GLA_HV, (2 * GLA_HK + GLA_HV) // GLA_HV)],
                    pars=[g], outs=[(GLA_HV, BF16)])[0]


def _gla_post_bwd(dy, o, pin, g):
    def fn(r, p):
        dyv, ov, rv = r
        dos, drs, dg = [], [], 0.0
        for h in range(GLA_HEADS):
            sl = slice(h * GLA_DV, (h + 1) * GLA_DV)
            oh, rh, dyh = ov[:, sl], rv[:, sl], dyv[:, sl]
            rs = lax.rsqrt(jnp.mean(oh * oh, axis=-1, keepdims=True) + RMS_EPS)
            xh = oh * rs
            sg = jax.nn.sigmoid(rh)
            d_on = dyh * (rh * sg)
            drs.append(dyh * (xh * p[0]) * (sg * (1.0 + rh * (1.0 - sg))))
            dg = dg + _colsum(d_on * xh)
            dxh = d_on * p[0]
            dos.append(rs * (dxh - xh * jnp.mean(dxh * xh, axis=-1, keepdims=True)))
        return [jnp.concatenate(dos, axis=1), jnp.concatenate(drs, axis=1)], [dg]
    return _rowwise(fn, name="gla_post_bwd",
                    rows=[(dy, GLA_HV, 0), (o, GLA_HV, 0), (pin, GLA_HV, (2 * GLA_HK + GLA_HV) // GLA_HV)],
                    pars=[g], outs=[(GLA_HV, F32), (GLA_HV, BF16)], accs=[GLA_DV])


def _gla_gate_bwd(dla, la):
    def fn(r, p):
        dz = r[0] * (1.0 / GLA_TAU) * (1.0 - jnp.exp(GLA_TAU * r[1]))
        return [dz], [_colsum(dz)]
    return _rowwise(fn, name="gla_gate_bwd", rows=[(dla, GLA_HK, 0), (la, GLA_HK, 0)], outs=[(GLA_HK, BF16)],
                    accs=[GLA_HK])


def _log_sigmoid(z):
    return jnp.minimum(z, 0.0) - jnp.log(1.0 + jnp.exp(-jnp.abs(z)))


def _rot_half(v):
    lane = lax.broadcasted_iota(jnp.int32, v.shape, 1)
    return jnp.where(lane < 32, -pltpu.roll(v, 96, 1), jnp.where(lane < 64, pltpu.roll(v, 32, 1), 0.0))


def _rms(v):
    rs = lax.rsqrt(jnp.mean(v * v, axis=-1, keepdims=True) + RMS_EPS)
    return v * rs, rs


def _mla_norm_fwd(cin, gq, gkv, cosp, sinp):
    def fn(r, p):
        cv, cs, sn = r
        qn, _ = _rms(cv[:, :MLA_QR])
        kvn, _ = _rms(cv[:, MLA_QR:MLA_QR + MLA_KVR])
        kr = cv[:, MLA_QR + MLA_KVR:]
        return [qn * p[0], kvn * p[1], kr * cs + _rot_half(kr) * sn], []
    return _rowwise(fn, name="mla_norm_fwd", rows=[(cin, MLA_IN_PAD, 0), (cosp, 128, 0), (sinp, 128, 0)],
                    pars=[gq, gkv], outs=[(MLA_QR, BF16), (MLA_KVR, BF16), (128, BF16)])


def _mla_qrope_fwd(q, cosp, sinp):
    scale = (MLA_NOPE + MLA_ROPE) ** -0.5

    def fn(r, p):
        qv, cs, sn = r
        parts = []
        for h in range(MLA_HEADS):
            parts.append(qv[:, h * MLA_QH:h * MLA_QH + 128] * scale)
            rp = qv[:, h * MLA_QH + 128:(h + 1) * MLA_QH]
            parts.append((rp * cs + _rot_half(rp) * sn) * scale)
        return [jnp.concatenate(parts, axis=1)], []
    W = MLA_HEADS * MLA_QH
    return _rowwise(fn, name="mla_qrope_fwd", rows=[(q, W, 0), (cosp, 128, 0), (sinp, 128, 0)],
                    outs=[(W, BF16)])[0]


def _mla_qrope_bwd(dq, cosp, sinp):
    scale = (MLA_NOPE + MLA_ROPE) ** -0.5

    def fn(r, p):
        dv, cs, sn = r
        parts = []
        for h in range(MLA_HEADS):
            parts.append(dv[:, h * MLA_QH:h * MLA_QH + 128] * scale)
            rp = dv[:, h * MLA_QH + 128:(h + 1) * MLA_QH]
            parts.append((rp * cs - _rot_half(rp) * sn) * scale)
        return [jnp.concatenate(parts, axis=1)], []
    W = MLA_HEADS * MLA_QH
    return _rowwise(fn, name="mla_qrope_bwd", rows=[(dq, W, 0), (cosp, 128, 0), (sinp, 128, 0)],
                    outs=[(W, BF16)])[0]


def _mla_norm_bwd(cin, dqn, dkvn, dkr, gq, gkv, cosp, sinp):
    def fn(r, p):
        cv, dq_, dkv_, dkr_, cs, sn = r
        outs, accs = [], []
        for (lo, hi), dn, g in (((0, MLA_QR), dq_, p[0]), ((MLA_QR, MLA_QR + MLA_KVR), dkv_, p[1])):
            xh, rs = _rms(cv[:, lo:hi])
            dxh = dn * g
            outs.append(rs * (dxh - xh * jnp.mean(dxh * xh, axis=-1, keepdims=True)))
            accs.append(_colsum(dn * xh))
        dk = dkr_[:, 0:128]
        for h in range(1, MLA_HEADS):
            dk = dk + dkr_[:, h * 128:(h + 1) * 128]
        outs.append(dk * cs - _rot_half(dk) * sn)
        return [jnp.concatenate(outs, axis=1)], accs
    return _rowwise(fn, name="mla_norm_bwd",
                    rows=[(cin, MLA_IN_PAD, 0), (dqn, MLA_QR, 0), (dkvn, MLA_KVR, 0), (dkr, MLA_HEADS * 128, 0),
                          (cosp, 128, 0), (sinp, 128, 0)],
                    pars=[gq, gkv], outs=[(MLA_IN_PAD, BF16)], accs=[MLA_QR, MLA_KVR])


def _mla_probs(q, kn, kr, i, tq):
    s = _nt(q[:, :128], kn) + _nt(q[:, 128:], kr)
    row = (i * tq + lax.broadcasted_iota(jnp.int32, s.shape, 0)) // CHUNK
    col = lax.broadcasted_iota(jnp.int32, s.shape, 1) // CHUNK
    s = jnp.where(col <= row, s, -jnp.inf)
    e = jnp.exp(s - jnp.max(s, axis=-1, keepdims=True))
    return e / jnp.sum(e, axis=-1, keepdims=True)


def _mla_attn_fwd(qr, knv, kr, tq=256):
    S = qr.shape[0]
    tq = min(tq, S)

    def body(q_ref, kn_ref, v_ref, kr_ref, o_ref):
        pr = _mla_probs(q_ref[...], kn_ref[...], kr_ref[...], pl.program_id(1), tq)
        o_ref[...] = _nn(pr.astype(BF16), v_ref[...])

    return pl.pallas_call(
        body, name="mla_attn_fwd", grid=(MLA_HEADS, S // tq),
        in_specs=[pl.BlockSpec((tq, MLA_QH), lambda h, i: (i, h)),
                  pl.BlockSpec((S, 128), lambda h, i: (0, h)),
                  pl.BlockSpec((S, 128), lambda h, i: (0, MLA_HEADS + h)),
                  pl.BlockSpec((S, 128), lambda h, i: (0, 0))],
        out_specs=pl.BlockSpec((tq, 128), lambda h, i: (i, h)),
        out_shape=jax.ShapeDtypeStruct((S, MLA_HEADS * MLA_V), F32),
        compiler_params=_cparams(("parallel", "arbitrary")),
    )(qr, knv, knv, kr)


def _mla_attn_bwd(qr, knv, kr, o, do, tq=256):
    S = qr.shape[0]
    tq = min(tq, S)
    W = MLA_HEADS * 128

    def body(q_ref, kn_ref, v_ref, kr_ref, o_ref, do_ref, dq_ref, dkn_ref, dv_ref, dkr_ref, dkn_acc, dv_acc):
        i = pl.program_id(1)
        q, kn, v, krv = q_ref[...], kn_ref[...], v_ref[...], kr_ref[...]
        pr = _mla_probs(q, kn, krv, i, tq)
        dov = do_ref[...]
        delta = jnp.sum(dov * o_ref[...], axis=-1, keepdims=True)
        dob = dov.astype(BF16)
        ds = (pr * (_nt(dob, v) - delta)).astype(BF16)
        dq_ref[...] = jnp.concatenate([_nn(ds, kn), _nn(ds, krv)], axis=1)

        @pl.when(i == 0)
        def _():
            dkn_acc[...] = jnp.zeros(dkn_acc.shape, F32)
            dv_acc[...] = jnp.zeros(dv_acc.shape, F32)
            dkr_ref[...] = jnp.zeros(dkr_ref.shape, F32)

        dkn_acc[...] += _tn(ds, q[:, :128])
        dkr_ref[...] += _tn(ds, q[:, 128:])
        dv_acc[...] += _tn(pr.astype(BF16), dob)

        @pl.when(i == pl.num_programs(1) - 1)
        def _():
            dkn_ref[...] = dkn_acc[...].astype(dkn_ref.dtype)
            dv_ref[...] = dv_acc[...].astype(dv_ref.dtype)

    return pl.pallas_call(
        body, name="mla_attn_bwd", grid=(MLA_HEADS, S // tq),
        in_specs=[pl.BlockSpec((tq, MLA_QH), lambda h, i: (i, h)),
                  pl.BlockSpec((S, 128), lambda h, i: (0, h)),
                  pl.BlockSpec((S, 128), lambda h, i: (0, MLA_HEADS + h)),
                  pl.BlockSpec((S, 128), lambda h, i: (0, 0)),
                  pl.BlockSpec((tq, 128), lambda h, i: (i, h)),
                  pl.BlockSpec((tq, 128), lambda h, i: (i, h))],
        out_specs=[pl.BlockSpec((tq, MLA_QH), lambda h, i: (i, h)),
                   pl.BlockSpec((S, 128), lambda h, i: (0, h)),
                   pl.BlockSpec((S, 128), lambda h, i: (0, h)),
                   pl.BlockSpec((S, 128), lambda h, i: (0, h))],
        out_shape=[jax.ShapeDtypeStruct((S, MLA_HEADS * MLA_QH), F32), jax.ShapeDtypeStruct((S, W), BF16),
                   jax.ShapeDtypeStruct((S, W), BF16), jax.ShapeDtypeStruct((S, W), F32)],
        scratch_shapes=[pltpu.VMEM((S, 128), F32), pltpu.VMEM((S, 128), F32)],
        compiler_params=_cparams(("parallel", "arbitrary")),
    )(qr, knv, knv, kr, o, do)


CONV_TILE = 256


def _shift_down(v, n):
    row = lax.broadcasted_iota(jnp.int32, v.shape, 0)
    return jnp.where(row >= n, pltpu.roll(v, n, 0), 0.0)


def _shift_up(v, n):
    S = v.shape[0]
    row = lax.broadcasted_iota(jnp.int32, v.shape, 0)
    return jnp.where(row < S - n, pltpu.roll(v, S - n, 0), 0.0)


def _conv_specs(S, n_extra_cols):
    nt = D_MODEL // CONV_TILE
    specs = [pl.BlockSpec((S, CONV_TILE), functools.partial(lambda j, o: (0, o + j), o=part * nt))
             for part in range(3)]
    specs.append(pl.BlockSpec((8, CONV_TILE), lambda j: (0, j)))
    specs += [pl.BlockSpec((S, CONV_TILE), lambda j: (0, j)) for _ in range(n_extra_cols)]
    return specs


def _conv_fwd(bcu, w8):
    S = bcu.shape[0]

    def body(b_ref, c_ref, u_ref, w_ref, y_ref):
        cu = c_ref[...] * u_ref[...]
        z = w_ref[2:3, :] * cu + w_ref[1:2, :] * _shift_down(cu, 1) + w_ref[0:1, :] * _shift_down(cu, 2)
        y_ref[...] = (b_ref[...] * z).astype(y_ref.dtype)

    return pl.pallas_call(
        body, name="conv_fwd", grid=(D_MODEL // CONV_TILE,), in_specs=_conv_specs(S, 0),
        out_specs=pl.BlockSpec((S, CONV_TILE), lambda j: (0, j)),
        out_shape=jax.ShapeDtypeStruct((S, D_MODEL), BF16),
        compiler_params=_cparams(("parallel",)),
    )(bcu, bcu, bcu, w8)


def _conv_bwd(bcu, w8, dy):
    S = bcu.shape[0]

    def body(b_ref, c_ref, u_ref, w_ref, dy_ref, db_ref, dc_ref, du_ref, dw_ref):
        b, c, u, dyv = b_ref[...], c_ref[...], u_ref[...], dy_ref[...]
        w0, w1, w2 = w_ref[0:1, :], w_ref[1:2, :], w_ref[2:3, :]
        cu = c * u
        cu1, cu2 = _shift_down(cu, 1), _shift_down(cu, 2)
        z = w2 * cu + w1 * cu1 + w0 * cu2
        dz = dyv * b
        db_ref[...] = (dyv * z).astype(db_ref.dtype)
        dcu = w2 * dz + w1 * _shift_up(dz, 1) + w0 * _shift_up(dz, 2)
        dc_ref[...] = (dcu * u).astype(dc_ref.dtype)
        du_ref[...] = (dcu * c).astype(du_ref.dtype)
        dw_ref[...] = jnp.zeros(dw_ref.shape, F32)
        dw_ref[0:1, :] = _colsum(dz * cu2)
        dw_ref[1:2, :] = _colsum(dz * cu1)
        dw_ref[2:3, :] = _colsum(dz * cu)

    col = pl.BlockSpec((S, CONV_TILE), lambda j: (0, j))
    return pl.pallas_call(
        body, name="conv_bwd", grid=(D_MODEL // CONV_TILE,), in_specs=_conv_specs(S, 1),
        out_specs=[col, col, col, pl.BlockSpec((8, CONV_TILE), lambda j: (0, j))],
        out_shape=[jax.ShapeDtypeStruct((S, D_MODEL), BF16)] * 3 + [jax.ShapeDtypeStruct((8, D_MODEL), F32)],
        compiler_params=_cparams(("parallel",)),
    )(bcu, bcu, bcu, w8, dy)


def _adamw(w, m, v, gs, name):
    L, R, Cn = w.shape
    assert len(gs) == L
    tr = R if R <= 512 else 512
    assert R % tr == 0

    def body(w_ref, m_ref, v_ref, *rest):
        g_refs, (go_ref, d_ref, nm_ref, nv_ref) = rest[:L], rest[L:]
        layer = pl.program_id(0)
        gv = g_refs[0][...]
        for k in range(1, L):
            gv = jnp.where(layer == k, g_refs[k][...], gv)
        nm = ADAM_B1 * m_ref[...] + (1.0 - ADAM_B1) * gv
        nv = ADAM_B2 * v_ref[...] + (1.0 - ADAM_B2) * jnp.square(gv)
        m_hat = nm / (1.0 - ADAM_B1 ** ADAM_STEP)
        v_hat = nv / (1.0 - ADAM_B2 ** ADAM_STEP)
        d_ref[...] = -ADAM_LR * (m_hat / (jnp.sqrt(v_hat) + ADAM_EPS) + ADAM_WD * w_ref[...])
        go_ref[...] = gv
        nm_ref[...] = nm
        nv_ref[...] = nv

    spec = pl.BlockSpec((None, tr, Cn), lambda l, i: (l, i, 0))
    g_specs = [pl.BlockSpec((tr, Cn), functools.partial(lambda l, i, k: (jnp.where(l == k, i, 0), 0), k=k))
               for k in range(L)]
    return pl.pallas_call(
        body, name=name, grid=(L, R // tr), in_specs=[spec] * 3 + g_specs, out_specs=[spec] * 4,
        out_shape=[jax.ShapeDtypeStruct((L, R, Cn), F32)] * 4,
        compiler_params=_cparams(("arbitrary", "arbitrary")),
    )(w, m, v, *gs)


HBM_SPEC = pl.BlockSpec(memory_space=pltpu.HBM)
BOUNCE_ROWS = 256


def _place():
    return lax.axis_index("x"), lax.axis_index("y"), lax.axis_index("c")


def _other_chips(x, y):
    return [(1 - x, y), (x, 1 - y), (1 - x, 1 - y)]


def _copy_via_vmem(src, dst, buf, sems, rows):
    ch = buf.shape[1]
    n = rows // ch
    cin = lambda i: pltpu.make_async_copy(src.at[pl.ds(i * ch, ch), :], buf.at[i % 2], sems.at[i % 2])
    cout = lambda i: pltpu.make_async_copy(buf.at[i % 2], dst.at[pl.ds(i * ch, ch), :], sems.at[2 + i % 2])
    cin(0).start()
    for i in range(n):
        cin(i).wait()
        cout(i).start()
        if i + 1 < n:
            if i >= 1:
                cout(i - 1).wait()
            cin(i + 1).start()
    if n >= 2:
        cout(n - 2).wait()
    cout(n - 1).wait()


def _all_gather(ops, name):
    n = len(ops)
    halves = [o.shape[0] // 2 for o in ops]
    chunk = [min(o.shape[0], BOUNCE_ROWS) for o in ops]

    def body(*refs):
        in_refs, out_refs = refs[:n], refs[n:2 * n]
        send_sems, recv_sems, local_sems = refs[2 * n:2 * n + 3]
        bufs = refs[2 * n + 3:]
        x, y, c = _place()
        q = 2 * x + y
        chips = _other_chips(x, y)
        sibling = (x, y, 1 - c)

        def region(t, chip, half):
            return out_refs[t].at[chip, pl.ds(half * halves[t], halves[t]), :]

        def copy(t, k, src, dst, to):
            return pltpu.make_async_remote_copy(src_ref=src, dst_ref=dst, send_sem=send_sems.at[6 * t + k],
                                                recv_sem=recv_sems.at[6 * t + k], device_id=to,
                                                device_id_type=MESH)

        first = []
        for t in range(n):
            for j, (cx, cy) in enumerate(chips):
                cp = copy(t, j, in_refs[t].at[pl.ds(c * halves[t], halves[t]), :], region(t, q, c), (cx, cy, c))
                cp.start()
                first.append(cp)
        for t in range(n):
            _copy_via_vmem(in_refs[t], out_refs[t].at[q], bufs[t], local_sems, ops[t].shape[0])
        passed = []
        for j, (cx, cy) in enumerate(chips):
            for t in range(n):
                land = region(t, 2 * cx + cy, c)
                copy(t, j, land, land, (cx, cy, c)).wait_recv()
                cp = copy(t, 3 + j, land, land, sibling)
                cp.start()
                passed.append(cp)
        for j, (cx, cy) in enumerate(chips):
            for t in range(n):
                land = region(t, 2 * cx + cy, 1 - c)
                copy(t, 3 + j, land, land, sibling).wait_recv()
        for cp in first + passed:
            cp.wait_send()

    return pl.pallas_call(
        body, name=name, in_specs=[HBM_SPEC] * n, out_specs=[HBM_SPEC] * n,
        out_shape=[jax.ShapeDtypeStruct((N_CHIPS,) + o.shape, o.dtype) for o in ops],
        scratch_shapes=[pltpu.SemaphoreType.DMA((6 * n,)), pltpu.SemaphoreType.DMA((6 * n,)),
                        pltpu.SemaphoreType.DMA((4,))]
        + [pltpu.VMEM((2, chunk[t], ops[t].shape[1]), ops[t].dtype) for t in range(n)],
        compiler_params=pltpu.CompilerParams(vmem_limit_bytes=VMEM_LIMIT),
    )(*ops)


def _swap_halves(ops, name):
    n = len(ops)

    def body(*refs):
        in_refs, out_refs, send_sems, recv_sems = refs[:n], refs[n:2 * n], refs[2 * n], refs[2 * n + 1]
        x, y, c = _place()
        cps = []
        for t in range(n):
            H = ops[t].shape[1] // 2
            cp = pltpu.make_async_remote_copy(src_ref=in_refs[t].at[:, pl.ds((1 - c) * H, H), :],
                                              dst_ref=out_refs[t], send_sem=send_sems.at[t],
                                              recv_sem=recv_sems.at[t], device_id=(x, y, 1 - c),
                                              device_id_type=MESH)
            cp.start()
            cps.append(cp)
        for cp in cps:
            cp.wait()

    return pl.pallas_call(
        body, name=name, in_specs=[HBM_SPEC] * n, out_specs=[HBM_SPEC] * n,
        out_shape=[jax.ShapeDtypeStruct((N_CHIPS, o.shape[1] // 2, o.shape[2]), o.dtype) for o in ops],
        scratch_shapes=[pltpu.SemaphoreType.DMA((n,)), pltpu.SemaphoreType.DMA((n,))],
    )(*ops)


def _sum_rows_tile(h):
    return h if h <= 512 else 512


def _pair_sum(g, t, cq, name):
    _, a, b = g.shape
    H = a // 2
    tr = _sum_rows_tile(H)

    def body(cq_ref, g_ref, t_ref, o_ref):
        o_ref[...] = (g_ref[...].astype(F32) + t_ref[...].astype(F32)).astype(o_ref.dtype)

    grid_spec = pltpu.PrefetchScalarGridSpec(
        num_scalar_prefetch=1, grid=(N_CHIPS, H // tr),
        in_specs=[pl.BlockSpec((None, None, tr, b), lambda j, i, cq_ref: (j, cq_ref[0], i, 0)),
                  pl.BlockSpec((None, tr, b), lambda j, i, cq_ref: (j, i, 0))],
        out_specs=pl.BlockSpec((None, tr, b), lambda j, i, cq_ref: (j, i, 0)))
    return pl.pallas_call(
        body, name=name, grid_spec=grid_spec, out_shape=jax.ShapeDtypeStruct(t.shape, BF16),
        compiler_params=_cparams(("parallel", "parallel")),
    )(cq, g.reshape(N_CHIPS, 2, H, b), t)


def _scatter_partials(ops, name):
    n = len(ops)

    def body(*refs):
        in_refs, out_refs, send_sems, recv_sems = refs[:n], refs[n:2 * n], refs[2 * n], refs[2 * n + 1]
        x, y, c = _place()
        cps = []
        for j, (cx, cy) in enumerate(_other_chips(x, y)):
            for t in range(n):
                cp = pltpu.make_async_remote_copy(src_ref=in_refs[t].at[2 * cx + cy], dst_ref=out_refs[t].at[j],
                                                  send_sem=send_sems.at[3 * t + j], recv_sem=recv_sems.at[3 * t + j],
                                                  device_id=(cx, cy, c), device_id_type=MESH)
                cp.start()
                cps.append(cp)
        for cp in cps:
            cp.wait()

    return pl.pallas_call(
        body, name=name, in_specs=[HBM_SPEC] * n, out_specs=[HBM_SPEC] * n,
        out_shape=[jax.ShapeDtypeStruct((3,) + o.shape[1:], o.dtype) for o in ops],
        scratch_shapes=[pltpu.SemaphoreType.DMA((3 * n,)), pltpu.SemaphoreType.DMA((3 * n,))],
    )(*ops)


def _chip_sum(p, t, cq, name):
    _, H, b = p.shape
    tr = _sum_rows_tile(H)

    def body(cq_ref, p_ref, t_ref, o_ref):
        acc = p_ref[...].astype(F32)
        for j in range(3):
            acc = acc + t_ref[j].astype(F32)
        o_ref[...] = acc

    grid_spec = pltpu.PrefetchScalarGridSpec(
        num_scalar_prefetch=1, grid=(H // tr,),
        in_specs=[pl.BlockSpec((None, tr, b), lambda i, cq_ref: (cq_ref[1], i, 0)),
                  pl.BlockSpec((3, tr, b), lambda i, cq_ref: (0, i, 0))],
        out_specs=pl.BlockSpec((None, tr, b), lambda i, cq_ref: (cq_ref[0], i, 0)))
    out = pl.pallas_call(
        body, name=name, grid_spec=grid_spec, out_shape=jax.ShapeDtypeStruct((2, H, b), F32),
        compiler_params=_cparams(("parallel",)),
    )(cq, p, t)
    return out.reshape(2 * H, b)


def _join_halves(ops, name):
    n = len(ops)

    def body(*refs):
        out_refs, send_sems, recv_sems = refs[n:2 * n], refs[2 * n], refs[2 * n + 1]
        x, y, c = _place()
        cps = []
        for t in range(n):
            H = ops[t].shape[0] // 2
            mine = out_refs[t].at[pl.ds(c * H, H), :]
            cp = pltpu.make_async_remote_copy(src_ref=mine, dst_ref=mine, send_sem=send_sems.at[t],
                                              recv_sem=recv_sems.at[t], device_id=(x, y, 1 - c),
                                              device_id_type=MESH)
            cp.start()
            cps.append(cp)
        for t in range(n):
            H = ops[t].shape[0] // 2
            other = out_refs[t].at[pl.ds((1 - c) * H, H), :]
            pltpu.make_async_remote_copy(src_ref=other, dst_ref=other, send_sem=send_sems.at[t],
                                         recv_sem=recv_sems.at[t], device_id=(x, y, 1 - c),
                                         device_id_type=MESH).wait_recv()
        for cp in cps:
            cp.wait_send()

    return pl.pallas_call(
        body, name=name, in_specs=[HBM_SPEC] * n, out_specs=[HBM_SPEC] * n,
        out_shape=[jax.ShapeDtypeStruct(o.shape, o.dtype) for o in ops],
        input_output_aliases={t: t for t in range(n)},
        scratch_shapes=[pltpu.SemaphoreType.DMA((n,)), pltpu.SemaphoreType.DMA((n,))],
    )(*ops)


def _reduce_scatter(gs, cq, tag):
    ts = _swap_halves(gs, "rs_swap_" + tag)
    ps = [_pair_sum(g, t, cq, "rs_pair_sum") for g, t in zip(gs, ts)]
    rs = _scatter_partials(ps, "rs_scatter_" + tag)
    fs = [_chip_sum(p, r, cq, "rs_chip_sum") for p, r in zip(ps, rs)]
    return _join_halves(fs, "rs_join_" + tag)


def _all_reduce_small(v):
    n = v.shape[0]

    def body(v_ref, out_ref, buf, send_sems, recv_sems):
        x, y, c = _place()
        me = 4 * x + 2 * y + c
        buf[me] = v_ref[...]
        cps = []
        for k in range(1, 8):
            peer = (x ^ (k >> 2), y ^ ((k >> 1) & 1), c ^ (k & 1))
            cp = pltpu.make_async_remote_copy(src_ref=v_ref, dst_ref=buf.at[me], send_sem=send_sems.at[k - 1],
                                              recv_sem=recv_sems.at[k - 1], device_id=peer, device_id_type=MESH)
            cp.start()
            cps.append(cp)
        for k in range(1, 8):
            px, py, pc = x ^ (k >> 2), y ^ ((k >> 1) & 1), c ^ (k & 1)
            land = buf.at[4 * px + 2 * py + pc]
            pltpu.make_async_remote_copy(src_ref=land, dst_ref=land, send_sem=send_sems.at[k - 1],
                                         recv_sem=recv_sems.at[k - 1], device_id=(px, py, pc),
                                         device_id_type=MESH).wait_recv()
        for cp in cps:
            cp.wait_send()
        acc = buf[0]
        for d in range(1, 8):
            acc = acc + buf[d]
        out_ref[...] = acc

    vm = pl.BlockSpec(memory_space=pltpu.VMEM)
    return pl.pallas_call(
        body, name="all_reduce_small", in_specs=[vm], out_specs=vm,
        out_shape=jax.ShapeDtypeStruct((n, 128), F32),
        scratch_shapes=[pltpu.VMEM((8, n, 128), F32), pltpu.SemaphoreType.DMA((7,)), pltpu.SemaphoreType.DMA((7,))],
    )(v)


SMALL_GATHER = (16, 1024)
SMALL_FULL = sum(_size(_full_shape(n)) for n in SMALL)
SMALL_FULL_ROWS = -(-SMALL_FULL // 128 // 8) * 8


def _layer_shards(w, i, q):
    kind, j = MIXER[i % 3], i // 3
    out = {n: w[n][i].astype(BF16) for n in COMMON_BIG}
    if kind == 'gla':
        win = jnp.zeros((D_MODEL, GLA_WIN), F32)
        win = lax.dynamic_update_slice(win, w['gla_w_in'][j], (0, (GLA_SHARD - GLA_WIN_STEP) * q))
        out['gla_w_in'] = win.astype(BF16)
        out['gla_w_out'] = w['gla_w_out'][j].astype(BF16)
    elif kind == 'mla':
        out['mla_w_in'] = jnp.pad(w['mla_w_in'][j], ((0, 0), (0, MLA_IN_PAD - MLA_IN))).astype(BF16)
        for n in ('mla_w_uq', 'mla_w_ukv', 'mla_w_out'):
            out[n] = w[n][j].astype(BF16)
    else:
        out['conv_w_in'] = w['conv_w_in'][j].astype(BF16)
        out['conv_w_out'] = w['conv_w_out'][j].astype(BF16)
    return out


def _rows_joined(g):
    return g.reshape(g.shape[0] * g.shape[1], g.shape[2])


def _cols_joined(g):
    return jnp.moveaxis(g, 0, 1).reshape(g.shape[1], -1)


def _layer_weights(g, i):
    kind = MIXER[i % 3]
    W = {'w1': g['mlp_w1'], 'w2': _rows_joined(g['mlp_w2']), 'gate': _rows_joined(g['ple_w_gate']),
         'proj': g['ple_w_proj']}
    if kind == 'gla':
        parts = []
        for qq in range(N_CHIPS):
            lo = g['gla_w_in'][qq][:, :128]
            if qq > 0:
                lo = lo + g['gla_w_in'][qq - 1][:, GLA_WIN_STEP:]
            parts += [lo, g['gla_w_in'][qq][:, 128:GLA_WIN_STEP]]
        parts.append(g['gla_w_in'][N_CHIPS - 1][:, GLA_WIN_STEP:])
        W['w_in'] = jnp.concatenate(parts, axis=1)
        W['w_out'] = _rows_joined(g['gla_w_out'])
    elif kind == 'mla':
        W['w_in'] = _rows_joined(g['mla_w_in'])
        uq = _cols_joined(g['mla_w_uq']).reshape(MLA_QR, MLA_HEADS, MLA_NOPE + MLA_ROPE)
        W['w_uq'] = jnp.pad(uq, ((0, 0), (0, 0), (0, MLA_QH - MLA_NOPE - MLA_ROPE))).reshape(MLA_QR, -1)
        ukv = _cols_joined(g['mla_w_ukv']).reshape(MLA_KVR, MLA_HEADS, 2, 128)
        W['w_ukv'] = ukv.transpose(0, 2, 1, 3).reshape(MLA_KVR, -1)
        W['w_out'] = _rows_joined(g['mla_w_out'])
    else:
        W['w_in'] = g['conv_w_in']
        W['w_out'] = _rows_joined(g['conv_w_out'])
    return W


def _pack_small_shards(w):
    flat = jnp.concatenate([w[n].reshape(-1) for n in SMALL_SHARDED])
    return jnp.pad(flat, (0, _size(SMALL_GATHER) - flat.shape[0])).reshape(SMALL_GATHER)


def _unpack_small_gathered(g):
    flat, out, off = g.reshape(N_CHIPS, -1), {}, 0
    for n in SMALL_SHARDED:
        shape, ax = WSPEC[n]
        seg = flat[:, off:off + _size(shape)].reshape((N_CHIPS,) + shape)
        out[n] = jnp.moveaxis(seg, 0, ax).reshape(_full_shape(n))
        off += _size(shape)
    return out


def _pack_small(vals):
    flat = jnp.concatenate([vals[n].reshape(-1) for n in SMALL])
    return jnp.pad(flat, (0, SMALL_FULL_ROWS * 128 - flat.shape[0])).reshape(SMALL_FULL_ROWS, 128)


def _unpack_small(packed, q):
    flat = packed.reshape(-1)
    out, off = {}, 0
    for n in SMALL:
        shape, ax = WSPEC[n]
        full = flat[off:off + _size(_full_shape(n))].reshape(_full_shape(n))
        off += _size(_full_shape(n))
        out[n] = full if ax is None else lax.dynamic_slice_in_dim(full, q * shape[ax], shape[ax], axis=ax)
    return out


def _row_shards(dw):
    return dw.reshape(N_CHIPS, dw.shape[0] // N_CHIPS, dw.shape[1])


def _col_shards(dw):
    return jnp.moveaxis(dw.reshape(dw.shape[0], N_CHIPS, -1), 1, 0)


def _row(v):
    return v.reshape(1, -1)


def _layer_fwd(i, xin, xin_b, p_i, W, sm, cosp, sinp):
    kind, j = MIXER[i % 3], i // 3
    sv = {'xin': xin, 'xin_b': xin_b}
    if kind == 'gla':
        w_up = jnp.pad(sm['gla_w_gate_up'][j].astype(BF16), ((0, 128 - GLA_RANK), (0, 0)))
        pin = _mm(xin_b, W['w_in'], name="gla_in", tn=640)
        la = _mm(pin, w_up, name="gla_gate", K=128, tk=128, a_off=(0, (GLA_IN_PAD - 128) // 128), tn=512,
                 extras=[(_row(sm['gla_b_gate'][j]), 'n')],
                 epilogue=lambda acc, b: (_log_sigmoid(acc + b) * (1.0 / GLA_TAU),))
        o, states = _gla_fwd(pin, la)
        yb = _gla_post_fwd(o, pin, _row(sm['gla_norm_g'][j]))
        h = _mm(yb, W['w_out'], name="mix_out")
        sv.update(w_up=w_up, pin=pin, la=la, o=o, states=states, yb=yb)
    elif kind == 'mla':
        gq, gkv = sm['mla_q_norm'][j:j + 1], sm['mla_kv_norm'][j:j + 1]
        cin = _mm(xin_b, W['w_in'], name="mla_in", tn=640)
        qn, kvn, kr = _mla_norm_fwd(cin, gq, gkv, cosp, sinp)
        qr = _mla_qrope_fwd(_mm(qn, W['w_uq'], name="mla_uq"), cosp, sinp)
        knv = _mm(kvn, W['w_ukv'], name="mla_ukv", out_dtypes=(BF16,))
        o = _mla_attn_fwd(qr, knv, kr)
        ob = o.astype(BF16)
        h = _mm(ob, W['w_out'], name="mix_out")
        sv.update(gq=gq, gkv=gkv, cin=cin, qn=qn, kvn=kvn, kr=kr, qr=qr, knv=knv, o=o, ob=ob)
    else:
        w8 = jnp.pad(sm['conv_w'][j], ((0, 5), (0, 0)))
        bcu = _mm(xin_b, W['w_in'], name="conv_in", tn=768, b_sh=True)
        yb = _conv_fwd(bcu, w8)
        h = _mm(yb, W['w_out'], name="mix_out")
        sv.update(w8=w8, bcu=bcu, yb=yb)
    g0, b0 = _row(sm['ln_g'][i, 0]), _row(sm['ln_b'][i, 0])
    g1, b1 = _row(sm['ln_g'][i, 1]), _row(sm['ln_b'][i, 1])
    x1, x1b = _ln_fwd(xin, h, g0, b0, "ln_fwd")
    ub, ab = _mm(x1b, W['w1'], name="mlp_up", out_dtypes=(BF16, BF16), b_sh=True,
                 epilogue=lambda acc: (acc, jnp.square(jnp.maximum(acc, 0.0))))
    m = _mm(ab, W['w2'], name="mlp_down")
    x2, x2b = _ln_fwd(x1, m, g1, b1, "ln_fwd")
    pp = _mm(p_i, W['proj'], name="ple_proj", tn=256, b_sh=True)
    z, x3, x3b = _mm(x2b, W['gate'], name="ple_gate", out_dtypes=(F32, F32, BF16),
                     extras=[(x2, 'mn'), (pp, 'mn')],
                     epilogue=lambda acc, xv, pv: (acc,) + (xv + jax.nn.sigmoid(acc) * pv,) * 2)
    sv.update(h=h, x1=x1, x1b=x1b, ub=ub, ab=ab, m=m, x2b=x2b, pp=pp, z=z, g0=g0, g1=g1)
    return x3, x3b, sv


def _layer_bwd(i, dx, p_i, W, sm, sv, cosp, sinp):
    kind, j = MIXER[i % 3], i // 3
    big, small = {}, {}
    dpp_b, dz_b = _ple_bwd_gate(dx, sv['z'], sv['pp'])
    big['ple_w_proj'] = _mm(p_i, dpp_b, ta=True, name="ple_proj_dw", tn=256, out_sh=True, out_dtypes=(BF16,))
    big['ple_w_gate'] = _row_shards(_mm(sv['x2b'], dz_b, ta=True, name="dw_dd", out_dtypes=(BF16,)))
    dx2 = _mm(dz_b, W['gate'], tb=True, name="dx_dd_add", extras=[(dx, 'mn')], epilogue=lambda acc, r: (acc + r,))
    dv1, dv1b, dg1, db1 = _ln_bwd(sv['x1'], sv['m'], sv['g1'], dx2, "ln_bwd")
    big['mlp_w2'] = _row_shards(_mm(sv['ab'], dv1b, ta=True, name="mlp_down_dw", out_dtypes=(BF16,)))
    dub = _mm(dv1b, W['w2'], tb=True, name="mlp_down_dx", out_dtypes=(BF16,), extras=[(sv['ub'], 'mn')],
              epilogue=lambda acc, u: (acc * (2.0 * jnp.maximum(u.astype(F32), 0.0)),))
    big['mlp_w1'] = _mm(sv['x1b'], dub, ta=True, name="mlp_up_dw", out_sh=True, out_dtypes=(BF16,))
    dx1 = _mm(dub, W['w1'], tb=True, name="mlp_up_dx", b_sh=True, extras=[(dv1, 'mn')],
              epilogue=lambda acc, r: (acc + ALPHA * r,))
    dv0, dv0b, dg0, db0 = _ln_bwd(sv['xin'], sv['h'], sv['g0'], dx1, "ln_bwd")
    small['ln_g'] = jnp.stack([dg0[0], dg1[0]])
    small['ln_b'] = jnp.stack([db0[0], db1[0]])
    resid = dict(extras=[(dv0, 'mn')], epilogue=lambda acc, r: (acc + ALPHA * r,))
    if kind == 'gla':
        big['gla_w_out'] = _row_shards(_mm(sv['yb'], dv0b, ta=True, name="dw_dd", out_dtypes=(BF16,)))
        dy = _mm(dv0b, W['w_out'], tb=True, name="dx_dd")
        do, dr_b, dng = _gla_post_bwd(dy, sv['o'], sv['pin'], _row(sm['gla_norm_g'][j]))
        dq_b, dk_b, dvv_b, dla = _gla_bwd(sv['pin'], sv['la'], sv['states'], do)
        dzg_b, dbg = _gla_gate_bwd(dla, sv['la'])
        dw_up = _mm(sv['pin'], dzg_b, ta=True, name="gla_gate_dw", M=128, tm=128,
                    a_off=(0, (GLA_IN_PAD - 128) // 128))
        dglr_b = _mm(dzg_b, sv['w_up'], tb=True, name="gla_gate_dx", out_dtypes=(BF16,))
        dpin_b = jnp.concatenate([dq_b, dk_b, dvv_b, dr_b, dglr_b], axis=1)
        dw_in = _mm(sv['xin_b'], dpin_b, ta=True, name="gla_in_dw", tn=640, out_dtypes=(BF16,))
        dxin = _mm(dpin_b, W['w_in'], tb=True, name="gla_in_dx", tk=640, **resid)
        big['gla_w_in'] = jnp.stack([dw_in[:, GLA_WIN_STEP * qq:GLA_WIN_STEP * qq + GLA_WIN]
                                     for qq in range(N_CHIPS)])
        small.update(gla_w_gate_up=dw_up[:GLA_RANK], gla_b_gate=dbg[0], gla_norm_g=dng[0])
    elif kind == 'mla':
        big['mla_w_out'] = _row_shards(_mm(sv['ob'], dv0b, ta=True, name="dw_dd", out_dtypes=(BF16,)))
        do = _mm(dv0b, W['w_out'], tb=True, name="dx_dd")
        dqr, dkn_b, dvv_b, dkr = _mla_attn_bwd(sv['qr'], sv['knv'], sv['kr'], sv['o'], do)
        dq_b = _mla_qrope_bwd(dqr, cosp, sinp)
        dw_uq = _mm(sv['qn'], dq_b, ta=True, name="mla_up_dw", out_dtypes=(BF16,))
        dqn = _mm(dq_b, W['w_uq'], tb=True, name="mla_up_dx")
        dknv_b = jnp.concatenate([dkn_b, dvv_b], axis=1)
        dw_ukv = _mm(sv['kvn'], dknv_b, ta=True, name="mla_up_dw", out_dtypes=(BF16,))
        dkvn = _mm(dknv_b, W['w_ukv'], tb=True, name="mla_up_dx")
        dcin_b, dgq, dgkv = _mla_norm_bwd(sv['cin'], dqn, dkvn, dkr, sv['gq'], sv['gkv'], cosp, sinp)
        big['mla_w_in'] = _row_shards(_mm(sv['xin_b'], dcin_b, ta=True, name="mla_in_dw", tn=640,
                                          out_dtypes=(BF16,)))
        dxin = _mm(dcin_b, W['w_in'], tb=True, name="mla_in_dx", tk=640, **resid)
        big['mla_w_uq'] = _col_shards(
            dw_uq.reshape(MLA_QR, MLA_HEADS, MLA_QH)[:, :, :MLA_NOPE + MLA_ROPE].reshape(MLA_QR, -1))
        big['mla_w_ukv'] = _col_shards(
            dw_ukv.reshape(MLA_KVR, 2, MLA_HEADS, 128).transpose(0, 2, 1, 3).reshape(MLA_KVR, -1))
        small.update(mla_q_norm=dgq[0], mla_kv_norm=dgkv[0])
    else:
        big['conv_w_out'] = _row_shards(_mm(sv['yb'], dv0b, ta=True, name="dw_dd", out_dtypes=(BF16,)))
        dy = _mm(dv0b, W['w_out'], tb=True, name="dx_dd")
        db_b, dc_b, du_b, dw8 = _conv_bwd(sv['bcu'], sv['w8'], dy)
        dbcu_b = jnp.concatenate([db_b, dc_b, du_b], axis=1)
        big['conv_w_in'] = _mm(sv['xin_b'], dbcu_b, ta=True, name="conv_in_dw", tn=768, out_sh=True,
                               out_dtypes=(BF16,))
        dxin = _mm(dbcu_b, W['w_in'], tb=True, name="conv_in_dx", tk=768, b_sh=True, **resid)
        small['conv_w'] = dw8[:3]
    return dxin, big, small


def _rope_tables(positions):
    inv_freq = ROPE_BASE ** (-jnp.arange(0, MLA_ROPE // 2, dtype=F32) * (2.0 / MLA_ROPE))
    ang = positions.astype(F32)[:, None] * inv_freq
    zeros = jnp.zeros((positions.shape[0], 64), F32)
    return (jnp.concatenate([jnp.cos(ang), jnp.cos(ang), zeros], axis=1),
            jnp.concatenate([jnp.sin(ang), jnp.sin(ang), zeros], axis=1))


def _gather_layers(w, q):
    layers, small = [], None
    for i in range(DEPTH):
        sh = _layer_shards(w, i, q)
        names = list(sh)
        ops = [sh[n] for n in names]
        if i == 0:
            ops.append(_pack_small_shards(w))
        got = _all_gather(ops, "all_gather_l%d" % i)
        layers.append(_layer_weights(dict(zip(names, got)), i))
        if i == 0:
            small = _unpack_small_gathered(got[-1])
    small['mla_q_norm'], small['mla_kv_norm'] = w['mla_q_norm'], w['mla_kv_norm']
    return layers, small


def _local_shard_grad(name, g, q):
    if name == 'gla_w_in':
        return lax.dynamic_slice_in_dim(g, (GLA_SHARD - GLA_WIN_STEP) * q, GLA_SHARD, axis=1)
    if name == 'mla_w_in':
        return g[:, :MLA_IN]
    return g


def kernel(x, p, positions, gla_w_in, gla_w_gate_up, gla_b_gate, gla_norm_g, gla_w_out, mla_w_in, mla_q_norm, mla_kv_norm, mla_w_uq, mla_w_ukv, mla_w_out, conv_w_in, conv_w, conv_w_out, ln_g, ln_b, mlp_w1, mlp_w2, ple_w_gate, ple_w_proj, loss_target, m_gla_w_in, m_gla_w_gate_up, m_gla_b_gate, m_gla_norm_g, m_gla_w_out, m_mla_w_in, m_mla_q_norm, m_mla_kv_norm, m_mla_w_uq, m_mla_w_ukv, m_mla_w_out, m_conv_w_in, m_conv_w, m_conv_w_out, m_ln_g, m_ln_b, m_mlp_w1, m_mlp_w2, m_ple_w_gate, m_ple_w_proj, v_gla_w_in, v_gla_w_gate_up, v_gla_b_gate, v_gla_norm_g, v_gla_w_out, v_mla_w_in, v_mla_q_norm, v_mla_kv_norm, v_mla_w_uq, v_mla_w_ukv, v_mla_w_out, v_conv_w_in, v_conv_w, v_conv_w_out, v_ln_g, v_ln_b, v_mlp_w1, v_mlp_w2, v_ple_w_gate, v_ple_w_proj):
    args = locals()
    w = {n: args[n] for n in WNAMES}
    m = {n: args['m_' + n] for n in WNAMES}
    v = {n: args['v_' + n] for n in WNAMES}
    q = 2 * lax.axis_index("x") + lax.axis_index("y")
    cq = jnp.stack([lax.axis_index("c"), q]).astype(jnp.int32)

    layers, sm = _gather_layers(w, q)
    cosp, sinp = _rope_tables(positions[0])

    xin, saved = x[0], []
    xin_b = xin.astype(BF16)
    for i in range(DEPTH):
        xin, xin_b, sv = _layer_fwd(i, xin, xin_b, p[i, 0], layers[i], sm, cosp, sinp)
        saved.append(sv)
    dx, loss_cols = _loss_head(xin, loss_target[0])
    loss = lax.psum(jnp.sum(loss_cols[0]), ("x", "y", "c"))

    gbig = {n: [None] * WSPEC[n][0][0] for n in BIG}
    gsmall = {n: [None] * _full_shape(n)[0] for n in SMALL}
    for i in reversed(range(DEPTH)):
        dx, big, small = _layer_bwd(i, dx, p[i, 0], layers[i], sm, saved[i], cosp, sinp)
        names = list(big)
        reduced = _reduce_scatter([big[n] for n in names], cq, "l%d" % i)
        for n, g in zip(names, reduced):
            gbig[n][i if n in COMMON_BIG else i // 3] = _local_shard_grad(n, g, q)
        for n, g in small.items():
            gsmall[n][i if n in ('ln_g', 'ln_b') else i // 3] = g
    gsm = _unpack_small(_all_reduce_small(_pack_small({n: jnp.stack(g) for n, g in gsmall.items()})), q)

    grad, delta, new_m, new_v = {}, {}, {}, {}
    for n in BIG:
        grad[n], delta[n], new_m[n], new_v[n] = _adamw(w[n], m[n], v[n], gbig[n], "adamw_" + n)
    total = sum(_size(WSPEC[n][0]) for n in SMALL)
    rows = -(-total // 128 // 8) * 8

    def pack(dct):
        flat = jnp.concatenate([dct[n].reshape(-1) for n in SMALL])
        return jnp.pad(flat, (0, rows * 128 - total), constant_values=1.0).reshape(1, rows, 128)

    res = _adamw(pack(w), pack(m), pack(v), [pack(gsm)[0]], "adamw_small")
    for out, packed in zip((grad, delta, new_m, new_v), res):
        flat, off = packed.reshape(-1), 0
        for n in SMALL:
            sz = _size(WSPEC[n][0])
            out[n] = flat[off:off + sz].reshape(WSPEC[n][0])
            off += sz
    return (loss, dx[None], *[grad[n] for n in WNAMES], *[delta[n] for n in WNAMES],
            *[new_m[n] for n in WNAMES], *[new_v[n] for n in WNAMES])
```

```python
import functools

import numpy as np
import jax
import jax.numpy as jnp
from jax import lax
from jax.experimental import pallas as pl
from jax.experimental.pallas import tpu as pltpu

F32 = jnp.float32
BF16 = jnp.bfloat16
MESH = pl.DeviceIdType.MESH

D_MODEL = 1024
DEPTH = 4
CHUNK = 64
ALPHA = (2 * DEPTH) ** 0.25
LN_EPS = 1e-5
RMS_EPS = 1e-6
PLE_DIM = 256
D_FF = 4 * D_MODEL
GLA_HEADS = 4
GLA_DK = 128
GLA_DV = 256
GLA_RANK = 16
GLA_TAU = 16.0
GLA_HK = GLA_HEADS * GLA_DK
GLA_HV = GLA_HEADS * GLA_DV
GLA_IN = 2 * GLA_HK + GLA_HV + D_MODEL + GLA_RANK
GLA_IN_PAD = 2 * GLA_HK + GLA_HV + D_MODEL + 128
GLA_SHARD = GLA_IN // 4
GLA_WIN = 896
GLA_WIN_STEP = 768
MLA_HEADS = 8
MLA_NOPE = 128
MLA_ROPE = 64
MLA_V = 128
MLA_QR = 256
MLA_KVR = 256
MLA_IN = MLA_QR + MLA_KVR + MLA_ROPE
MLA_IN_PAD = MLA_QR + MLA_KVR + 128
MLA_QH = 256
ROPE_BASE = 10000.0
ADAM_LR = 0.001
ADAM_B1 = 0.9
ADAM_B2 = 0.999
ADAM_EPS = 1e-08
ADAM_WD = 0.01
ADAM_STEP = 10

VMEM_LIMIT = 48 * 1024 * 1024
N_CHIPS = 4

WSPEC = {
    'gla_w_in': ((2, 1024, 772), 2), 'gla_w_gate_up': ((2, 16, 128), 2), 'gla_b_gate': ((2, 128), 1),
    'gla_norm_g': ((2, 64), 1), 'gla_w_out': ((2, 256, 1024), 1), 'mla_w_in': ((1, 256, 576), 1),
    'mla_q_norm': ((1, 256), None), 'mla_kv_norm': ((1, 256), None), 'mla_w_uq': ((1, 256, 384), 2),
    'mla_w_ukv': ((1, 256, 512), 2), 'mla_w_out': ((1, 256, 1024), 1), 'conv_w_in': ((1, 1024, 768), 2),
    'conv_w': ((1, 3, 256), 2), 'conv_w_out': ((1, 256, 1024), 1), 'ln_g': ((4, 2, 256), 2),
    'ln_b': ((4, 2, 256), 2), 'mlp_w1': ((4, 1024, 1024), 2), 'mlp_w2': ((4, 1024, 1024), 1),
    'ple_w_gate': ((4, 256, 1024), 1), 'ple_w_proj': ((4, 256, 256), 2),
}
WNAMES = list(WSPEC)
BIG = ['gla_w_in', 'gla_w_out', 'mla_w_in', 'mla_w_uq', 'mla_w_ukv', 'mla_w_out', 'conv_w_in', 'conv_w_out',
       'mlp_w1', 'mlp_w2', 'ple_w_gate', 'ple_w_proj']
SMALL_SHARDED = ['gla_w_gate_up', 'gla_b_gate', 'gla_norm_g', 'conv_w', 'ln_g', 'ln_b']
SMALL = SMALL_SHARDED + ['mla_q_norm', 'mla_kv_norm']
MIXER = ['gla', 'mla', 'conv']
LAYER_BIG = {'gla': ['gla_w_in', 'gla_w_out'], 'mla': ['mla_w_in', 'mla_w_uq', 'mla_w_ukv', 'mla_w_out'],
             'conv': ['conv_w_in', 'conv_w_out']}
COMMON_BIG = ['mlp_w1', 'mlp_w2', 'ple_w_gate', 'ple_w_proj']


def _size(shape):
    return int(np.prod(shape))


def _full_shape(name):
    shape, ax = WSPEC[name]
    if ax is None:
        return shape
    return tuple(s * N_CHIPS if i == ax else s for i, s in enumerate(shape))


def _cparams(sem=None):
    return pltpu.CompilerParams(dimension_semantics=sem, vmem_limit_bytes=VMEM_LIMIT)


def _mm(a, b, *, name, ta=False, tb=False, M=None, N=None, K=None, out_dtypes=(F32,), epilogue=None, extras=(),
        tm=1024, tn=512, tk=1024, a_off=(0, 0), b_sh=False, out_sh=False):
    if M is None:
        M = a.shape[1] if ta else a.shape[0]
    if K is None:
        K = a.shape[0] if ta else a.shape[1]
    if b_sh:
        kw, nq = b.shape[1], b.shape[2]
        n_b, k_b = (kw, N_CHIPS * nq) if tb else (N_CHIPS * nq, kw)
        N = n_b if N is None else N
        assert K == k_b
    elif N is None:
        N = b.shape[0] if tb else b.shape[1]
    tm, tn, tk = min(tm, M), min(tn, N), min(tk, K)
    assert M % tm == 0 and N % tn == 0 and K % tk == 0, (name, M, N, K, tm, tn, tk)
    nk = K // tk
    n_ex, n_out = len(extras), len(out_dtypes)

    def body(a_ref, b_ref, *rest):
        ex_refs, out_refs = rest[:n_ex], rest[n_ex:n_ex + n_out]
        part = lax.dot_general(a_ref[...].astype(BF16), b_ref[...].astype(BF16),
                               ((((0,) if ta else (1,)), ((1,) if tb else (0,))), ((), ())),
                               preferred_element_type=F32)

        def finish(acc):
            res = (acc,) if epilogue is None else epilogue(acc, *[r[...] for r in ex_refs])
            for r, v in zip(out_refs, res):
                r[...] = v.astype(r.dtype)

        if nk == 1:
            finish(part)
        else:
            acc_ref = rest[-1]
            k = pl.program_id(2)

            @pl.when(k == 0)
            def _():
                acc_ref[...] = part

            @pl.when(k > 0)
            def _():
                acc_ref[...] += part

            @pl.when(k == nk - 1)
            def _():
                finish(acc_ref[...])

    if ta:
        a_spec = pl.BlockSpec((tk, tm), lambda i, j, k: (k + a_off[0], i + a_off[1]))
    else:
        a_spec = pl.BlockSpec((tm, tk), lambda i, j, k: (i + a_off[0], k + a_off[1]))
    if b_sh and tb:
        assert nq % tk == 0
        per = nq // tk
        b_spec = pl.BlockSpec((None, tn, tk), lambda i, j, k: (k // per, j, k % per))
    elif b_sh:
        assert nq % tn == 0
        per = nq // tn
        b_spec = pl.BlockSpec((None, tk, tn), lambda i, j, k: (j // per, k, j % per))
    elif tb:
        b_spec = pl.BlockSpec((tn, tk), lambda i, j, k: (j, k))
    else:
        b_spec = pl.BlockSpec((tk, tn), lambda i, j, k: (k, j))
    ex_specs = []
    for arr, kind in extras:
        if kind == 'mn':
            ex_specs.append(pl.BlockSpec((tm, tn), lambda i, j, k: (i, j)))
        else:
            ex_specs.append(pl.BlockSpec((1, tn), lambda i, j, k: (0, j)))
    if out_sh:
        assert (N // N_CHIPS) % tn == 0
        per_o = N // N_CHIPS // tn
        o_spec = pl.BlockSpec((None, tm, tn), lambda i, j, k: (j // per_o, i, j % per_o))
        o_shape = (N_CHIPS, M, N // N_CHIPS)
    else:
        o_spec = pl.BlockSpec((tm, tn), lambda i, j, k: (i, j))
        o_shape = (M, N)
    outs = pl.pallas_call(
        body, name=name, grid=(M // tm, N // tn, nk),
        in_specs=[a_spec, b_spec] + ex_specs,
        out_specs=[o_spec for _ in out_dtypes],
        out_shape=[jax.ShapeDtypeStruct(o_shape, d) for d in out_dtypes],
        scratch_shapes=[pltpu.VMEM((tm, tn), F32)] if nk > 1 else [],
        compiler_params=_cparams(("parallel", "parallel", "arbitrary")),
    )(a, b, *[e[0] for e in extras])
    return outs[0] if n_out == 1 else tuple(outs)


def _rowwise(fn, *, name, rows, pars=(), outs=(), accs=(), tm=256):
    S = rows[0][0].shape[0]
    tm = min(tm, S)
    assert S % tm == 0
    n_r, n_p, n_o, n_a = len(rows), len(pars), len(outs), len(accs)

    def body(*refs):
        r_refs, p_refs = refs[:n_r], refs[n_r:n_r + n_p]
        o_refs, a_refs = refs[n_r + n_p:n_r + n_p + n_o], refs[n_r + n_p + n_o:]
        o_vals, a_vals = fn([r[...] for r in r_refs], [p[...] for p in p_refs])
        for r, v in zip(o_refs, o_vals):
            r[...] = v.astype(r.dtype)
        if n_a:
            i = pl.program_id(0)

            @pl.when(i == 0)
            def _():
                for r in a_refs:
                    r[...] = jnp.zeros(r.shape, r.dtype)

            for r, v in zip(a_refs, a_vals):
                r[...] += jnp.broadcast_to(v, r.shape)

    in_specs = [pl.BlockSpec((tm, w), functools.partial(lambda i, o: (i, o), o=off)) for _, w, off in rows]
    in_specs += [pl.BlockSpec(p.shape, functools.partial(lambda i, nd: (0,) * nd, nd=p.ndim)) for p in pars]
    out_specs = [pl.BlockSpec((tm, w), lambda i: (i, 0)) for w, _ in outs]
    out_specs += [pl.BlockSpec((8, w), lambda i: (0, 0)) for w in accs]
    out_shape = [jax.ShapeDtypeStruct((S, w), d) for w, d in outs]
    out_shape += [jax.ShapeDtypeStruct((8, w), F32) for w in accs]
    res = pl.pallas_call(
        body, name=name, grid=(S // tm,), in_specs=in_specs, out_specs=out_specs, out_shape=out_shape,
        compiler_params=_cparams(("arbitrary",)),
    )(*[r[0] for r in rows], *pars)
    return tuple(res)


def _colsum(v):
    return jnp.sum(v, axis=0, keepdims=True)


def _ln_stats(v):
    mu = jnp.mean(v, axis=-1, keepdims=True)
    d = v - mu
    var = jnp.mean(d * d, axis=-1, keepdims=True)
    rstd = lax.rsqrt(var + LN_EPS)
    return d * rstd, rstd


def _ln_fwd(x, h, g, b, name):
    def fn(r, p):
        xhat, _ = _ln_stats(ALPHA * r[0] + r[1])
        y = xhat * p[0] + p[1]
        return [y, y], []
    return _rowwise(fn, name=name, rows=[(x, D_MODEL, 0), (h, D_MODEL, 0)], pars=[g, b],
                    outs=[(D_MODEL, F32), (D_MODEL, BF16)])


def _ln_bwd(x, h, g, dy, name):
    def fn(r, p):
        xhat, rstd = _ln_stats(ALPHA * r[0] + r[1])
        dyv = r[2]
        dxh = dyv * p[0]
        m1 = jnp.mean(dxh, axis=-1, keepdims=True)
        m2 = jnp.mean(dxh * xhat, axis=-1, keepdims=True)
        dv = rstd * (dxh - m1 - xhat * m2)
        return [dv, dv], [_colsum(dyv * xhat), _colsum(dyv)]
    return _rowwise(fn, name=name, rows=[(x, D_MODEL, 0), (h, D_MODEL, 0), (dy, D_MODEL, 0)], pars=[g],
                    outs=[(D_MODEL, F32), (D_MODEL, BF16)], accs=[D_MODEL, D_MODEL])


def _loss_head(y, t):
    def fn(r, p):
        d = r[0] - r[1]
        return [d * (1.0 / D_MODEL)], [_colsum(d * d) * (0.5 / D_MODEL)]
    return _rowwise(fn, name="loss_head", rows=[(y, D_MODEL, 0), (t, D_MODEL, 0)], outs=[(D_MODEL, F32)],
                    accs=[D_MODEL])


def _ple_bwd_gate(dx3, z, pp, token):
    def fn(r, p):
        s = jax.nn.sigmoid(r[1])
        return [r[0] * s, r[0] * r[2] * s * (1.0 - s)], []
    return _rowwise(fn, name="ple_bwd_gate", rows=[(dx3, D_MODEL, 0), (z, D_MODEL, 0), (pp, D_MODEL, 0)],
                    pars=[token], outs=[(D_MODEL, BF16), (D_MODEL, BF16)])


N_LEVELS = 6


def _gla_consts():
    C = CHUNK
    A = np.zeros((N_LEVELS + 3, C, C), np.float32)
    masks = np.zeros((N_LEVELS + 1, C, C), np.float32)
    r = np.arange(C)[:, None]
    u = np.arange(C)[None, :]
    for l in range(N_LEVELS):
        half = C >> (l + 1)
        mid = (r // (2 * half)) * (2 * half) + half - 1
        A[l] = np.where(r > mid, (u > mid) & (u <= r), (u > r) & (u <= mid))
        masks[l] = ((r // (2 * half)) == (u // (2 * half))) & (((r // half) % 2) != ((u // half) % 2))
    masks[N_LEVELS] = (r == u)
    A[N_LEVELS] = (u <= r)
    A[N_LEVELS + 1] = (u > r)
    A[N_LEVELS + 2] = 1.0
    A = A.reshape(-1, C)
    return A, np.ascontiguousarray(A.T), masks


def _split3(v):
    hi = v.astype(BF16)
    r1 = v - hi.astype(F32)
    mid = r1.astype(BF16)
    lo = (r1 - mid.astype(F32)).astype(BF16)
    return hi, mid, lo


def _dot_exact01(a01, v):
    hi, mid, lo = _split3(v)
    f = lambda p: jnp.dot(a01, p, preferred_element_type=F32)
    return f(hi) + f(mid) + f(lo)


def _nt(a, b):
    return lax.dot_general(a, b, (((1,), (1,)), ((), ())), preferred_element_type=F32)


def _tn(a, b):
    return lax.dot_general(a, b, (((0,), (0,)), ((), ())), preferred_element_type=F32)


def _nn(a, b):
    return jnp.dot(a, b, preferred_element_type=F32)


def _gla_chunk_terms(q, k, la, a_ref, m_ref):
    C = CHUNK
    E = jnp.exp(_dot_exact01(a_ref[...], la))
    scores = m_ref[N_LEVELS] * _nt(q.astype(BF16), k.astype(BF16))
    qes, kes = [], []
    for l in range(N_LEVELS):
        El = E[l * C:(l + 1) * C]
        qe, ke = (q * El).astype(BF16), (k * El).astype(BF16)
        qes.append(qe)
        kes.append(ke)
        scores = scores + m_ref[l] * _nt(qe, ke)
    return E, qes, kes, scores


def _gla_fwd(pin, la):
    S = pin.shape[0]
    NC = S // CHUNK
    C = CHUNK
    A, _, masks = _gla_consts()

    def body(q_ref, k_ref, v_ref, la_ref, a_ref, m_ref, o_ref, st_ref, state):
        c = pl.program_id(1)

        @pl.when(c == 0)
        def _():
            state[...] = jnp.zeros(state.shape, F32)

        q = q_ref[...] * (GLA_DK ** -0.5)
        k, v, la_c = k_ref[...], v_ref[...], la_ref[...]
        E, _, _, scores = _gla_chunk_terms(q, k, la_c, a_ref, m_ref)
        Eq, Ek, Ee = E[6 * C:7 * C], E[7 * C:8 * C], E[8 * C:9 * C]
        st = state[...]
        st_ref[...] = st
        vb = v.astype(BF16)
        o_ref[...] = _nn(scores.astype(BF16), vb) + _nt((q * Eq).astype(BF16), st.astype(BF16))
        state[...] = st * jnp.concatenate([Ee] * (GLA_DV // C), axis=0) + _tn(vb, (k * Ek).astype(BF16))

    nkb = GLA_HK // GLA_DK
    return pl.pallas_call(
        body, name="gla_fwd", grid=(GLA_HEADS, NC),
        in_specs=[pl.BlockSpec((C, GLA_DK), lambda h, c: (c, h)),
                  pl.BlockSpec((C, GLA_DK), lambda h, c: (c, nkb + h)),
                  pl.BlockSpec((C, GLA_DV), lambda h, c: (c, 2 * GLA_HK // GLA_DV + h)),
                  pl.BlockSpec((C, GLA_DK), lambda h, c: (c, h)),
                  pl.BlockSpec(A.shape, lambda h, c: (0, 0)),
                  pl.BlockSpec(masks.shape, lambda h, c: (0, 0, 0))],
        out_specs=[pl.BlockSpec((C, GLA_DV), lambda h, c: (c, h)),
                   pl.BlockSpec((None, None, GLA_DV, GLA_DK), lambda h, c: (h, c, 0, 0))],
        out_shape=[jax.ShapeDtypeStruct((S, GLA_HV), F32),
                   jax.ShapeDtypeStruct((GLA_HEADS, NC, GLA_DV, GLA_DK), F32)],
        scratch_shapes=[pltpu.VMEM((GLA_DV, GLA_DK), F32)],
        compiler_params=_cparams(("parallel", "arbitrary")),
    )(pin, pin, pin, la, jnp.asarray(A, BF16), jnp.asarray(masks))


def _gla_bwd(pin, la, states, do):
    S = pin.shape[0]
    NC = S // CHUNK
    C = CHUNK
    A, AT, masks = _gla_consts()
    scale = GLA_DK ** -0.5

    def body(q_ref, k_ref, v_ref, la_ref, st_ref, do_ref, a_ref, at_ref, m_ref,
             dq_ref, dk_ref, dv_ref, dla_ref, dstate):
        c = pl.program_id(1)

        @pl.when(c == 0)
        def _():
            dstate[...] = jnp.zeros(dstate.shape, F32)

        q = q_ref[...] * scale
        k, v, la_c, st, dov = k_ref[...], v_ref[...], la_ref[...], st_ref[...], do_ref[...]
        E, qes, kes, scores = _gla_chunk_terms(q, k, la_c, a_ref, m_ref)
        Eq, Ek, Ee = E[6 * C:7 * C], E[7 * C:8 * C], E[8 * C:9 * C]
        dst = dstate[...]
        dob, vb, dstb = dov.astype(BF16), v.astype(BF16), dst.astype(BF16)
        qEq, kEk = (q * Eq).astype(BF16), (k * Ek).astype(BF16)
        dsc = _nt(dob, vb)
        dv_ref[...] = (_tn(scores.astype(BF16), dob) + _nt(kEk, dstb)).astype(dv_ref.dtype)
        dqEq = _nn(dob, st.astype(BF16))
        dkEk = _nn(vb, dstb)
        Gd = (m_ref[N_LEVELS] * dsc).astype(BF16)
        dq = _nn(Gd, k.astype(BF16)) + dqEq * Eq
        dk = _tn(Gd, q.astype(BF16)) + dkEk * Ek
        dX = []
        for l in range(N_LEVELS):
            El = E[l * C:(l + 1) * C]
            G = (m_ref[l] * dsc).astype(BF16)
            dqe, dke = _nn(G, kes[l]), _tn(G, qes[l])
            dq = dq + dqe * El
            dk = dk + dke * El
            dX.append((dqe * q + dke * k) * El)
        dX.append(dqEq * q * Eq)
        dX.append(dkEk * k * Ek)
        prod = dst * st
        dEe = prod[0:C]
        for i in range(1, GLA_DV // C):
            dEe = dEe + prod[i * C:(i + 1) * C]
        dX.append(dEe * Ee)
        dla_ref[...] = _dot_exact01(at_ref[...], jnp.concatenate(dX, axis=0))
        dq_ref[...] = (dq * scale).astype(dq_ref.dtype)
        dk_ref[...] = dk.astype(dk_ref.dtype)
        dstate[...] = dst * jnp.concatenate([Ee] * (GLA_DV // C), axis=0) + _tn(dob, qEq)

    nkb = GLA_HK // GLA_DK
    rc = lambda c: NC - 1 - c
    return pl.pallas_call(
        body, name="gla_bwd", grid=(GLA_HEADS, NC),
        in_specs=[pl.BlockSpec((C, GLA_DK), lambda h, c: (rc(c), h)),
                  pl.BlockSpec((C, GLA_DK), lambda h, c: (rc(c), nkb + h)),
                  pl.BlockSpec((C, GLA_DV), lambda h, c: (rc(c), 2 * GLA_HK // GLA_DV + h)),
                  pl.BlockSpec((C, GLA_DK), lambda h, c: (rc(c), h)),
                  pl.BlockSpec((None, None, GLA_DV, GLA_DK), lambda h, c: (h, rc(c), 0, 0)),
                  pl.BlockSpec((C, GLA_DV), lambda h, c: (rc(c), h)),
                  pl.BlockSpec(A.shape, lambda h, c: (0, 0)),
                  pl.BlockSpec(AT.shape, lambda h, c: (0, 0)),
                  pl.BlockSpec(masks.shape, lambda h, c: (0, 0, 0))],
        out_specs=[pl.BlockSpec((C, GLA_DK), lambda h, c: (rc(c), h)),
                   pl.BlockSpec((C, GLA_DK), lambda h, c: (rc(c), h)),
                   pl.BlockSpec((C, GLA_DV), lambda h, c: (rc(c), h)),
                   pl.BlockSpec((C, GLA_DK), lambda h, c: (rc(c), h))],
        out_shape=[jax.ShapeDtypeStruct((S, GLA_HK), BF16), jax.ShapeDtypeStruct((S, GLA_HK), BF16),
                   jax.ShapeDtypeStruct((S, GLA_HV), BF16), jax.ShapeDtypeStruct((S, GLA_HK), F32)],
        scratch_shapes=[pltpu.VMEM((GLA_DV, GLA_DK), F32)],
        compiler_params=_cparams(("parallel", "arbitrary")),
    )(pin, pin, pin, la, states, do, jnp.asarray(A, BF16), jnp.asarray(AT, BF16), jnp.asarray(masks))


def _gla_post_fwd(o, pin, g):
    def fn(r, p):
        ov, rv = r
        ys = []
        for h in range(GLA_HEADS):
            oh = ov[:, h * GLA_DV:(h + 1) * GLA_DV]
            rh = rv[:, h * GLA_DV:(h + 1) * GLA_DV]
            rs = lax.rsqrt(jnp.mean(oh * oh, axis=-1, keepdims=True) + RMS_EPS)
            ys.append(oh * rs * p[0] * (rh * jax.nn.sigmoid(rh)))
        return [jnp.concatenate(ys, axis=1)], []
    return _rowwise(fn, name="gla_post_fwd", rows=[(o, GLA_HV, 0), (pin, GLA_HV, (2 * GLA_HK + GLA_HV) // GLA_HV)],
                    pars=[g], outs=[(GLA_HV, BF16)])[0]


def _gla_post_bwd(dy, o, pin, g):
    def fn(r, p):
        dyv, ov, rv = r
        dos, drs, dg = [], [], 0.0
        for h in range(GLA_HEADS):
            sl = slice(h * GLA_DV, (h + 1) * GLA_DV)
            oh, rh, dyh = ov[:, sl], rv[:, sl], dyv[:, sl]
            rs = lax.rsqrt(jnp.mean(oh * oh, axis=-1, keepdims=True) + RMS_EPS)
            xh = oh * rs
            sg = jax.nn.sigmoid(rh)
            d_on = dyh * (rh * sg)
            drs.append(dyh * (xh * p[0]) * (sg * (1.0 + rh * (1.0 - sg))))
            dg = dg + _colsum(d_on * xh)
            dxh = d_on * p[0]
            dos.append(rs * (dxh - xh * jnp.mean(dxh * xh, axis=-1, keepdims=True)))
        return [jnp.concatenate(dos, axis=1), jnp.concatenate(drs, axis=1)], [dg]
    return _rowwise(fn, name="gla_post_bwd",
                    rows=[(dy, GLA_HV, 0), (o, GLA_HV, 0), (pin, GLA_HV, (2 * GLA_HK + GLA_HV) // GLA_HV)],
                    pars=[g], outs=[(GLA_HV, F32), (GLA_HV, BF16)], accs=[GLA_DV])


def _gla_gate_bwd(dla, la):
    def fn(r, p):
        dz = r[0] * (1.0 / GLA_TAU) * (1.0 - jnp.exp(GLA_TAU * r[1]))
        return [dz], [_colsum(dz)]
    return _rowwise(fn, name="gla_gate_bwd", rows=[(dla, GLA_HK, 0), (la, GLA_HK, 0)], outs=[(GLA_HK, BF16)],
                    accs=[GLA_HK])


def _log_sigmoid(z):
    return jnp.minimum(z, 0.0) - jnp.log(1.0 + jnp.exp(-jnp.abs(z)))


def _rot_half(v):
    lane = lax.broadcasted_iota(jnp.int32, v.shape, 1)
    return jnp.where(lane < 32, -pltpu.roll(v, 96, 1), jnp.where(lane < 64, pltpu.roll(v, 32, 1), 0.0))


def _rms(v):
    rs = lax.rsqrt(jnp.mean(v * v, axis=-1, keepdims=True) + RMS_EPS)
    return v * rs, rs


def _mla_norm_fwd(cin, gq, gkv, cosp, sinp):
    def fn(r, p):
        cv, cs, sn = r
        qn, _ = _rms(cv[:, :MLA_QR])
        kvn, _ = _rms(cv[:, MLA_QR:MLA_QR + MLA_KVR])
        kr = cv[:, MLA_QR + MLA_KVR:]
        return [qn * p[0], kvn * p[1], kr * cs + _rot_half(kr) * sn], []
    return _rowwise(fn, name="mla_norm_fwd", rows=[(cin, MLA_IN_PAD, 0), (cosp, 128, 0), (sinp, 128, 0)],
                    pars=[gq, gkv], outs=[(MLA_QR, BF16), (MLA_KVR, BF16), (128, BF16)])


def _mla_qrope_fwd(q, cosp, sinp):
    scale = (MLA_NOPE + MLA_ROPE) ** -0.5

    def fn(r, p):
        qv, cs, sn = r
        parts = []
        for h in range(MLA_HEADS):
            parts.append(qv[:, h * MLA_QH:h * MLA_QH + 128] * scale)
            rp = qv[:, h * MLA_QH + 128:(h + 1) * MLA_QH]
            parts.append((rp * cs + _rot_half(rp) * sn) * scale)
        return [jnp.concatenate(parts, axis=1)], []
    W = MLA_HEADS * MLA_QH
    return _rowwise(fn, name="mla_qrope_fwd", rows=[(q, W, 0), (cosp, 128, 0), (sinp, 128, 0)],
                    outs=[(W, BF16)])[0]


def _mla_qrope_bwd(dq, cosp, sinp):
    scale = (MLA_NOPE + MLA_ROPE) ** -0.5

    def fn(r, p):
        dv, cs, sn = r
        parts = []
        for h in range(MLA_HEADS):
            parts.append(dv[:, h * MLA_QH:h * MLA_QH + 128] * scale)
            rp = dv[:, h * MLA_QH + 128:(h + 1) * MLA_QH]
            parts.append((rp * cs - _rot_half(rp) * sn) * scale)
        return [jnp.concatenate(parts, axis=1)], []
    W = MLA_HEADS * MLA_QH
    return _rowwise(fn, name="mla_qrope_bwd", rows=[(dq, W, 0), (cosp, 128, 0), (sinp, 128, 0)],
                    outs=[(W, BF16)])[0]


def _mla_norm_bwd(cin, dqn, dkvn, dkr, gq, gkv, cosp, sinp):
    def fn(r, p):
        cv, dq_, dkv_, dkr_, cs, sn = r
        outs, accs = [], []
        for (lo, hi), dn, g in (((0, MLA_QR), dq_, p[0]), ((MLA_QR, MLA_QR + MLA_KVR), dkv_, p[1])):
            xh, rs = _rms(cv[:, lo:hi])
            dxh = dn * g
            outs.append(rs * (dxh - xh * jnp.mean(dxh * xh, axis=-1, keepdims=True)))
            accs.append(_colsum(dn * xh))
        dk = dkr_[:, 0:128]
        for h in range(1, MLA_HEADS):
            dk = dk + dkr_[:, h * 128:(h + 1) * 128]
        outs.append(dk * cs - _rot_half(dk) * sn)
        return [jnp.concatenate(outs, axis=1)], accs
    return _rowwise(fn, name="mla_norm_bwd",
                    rows=[(cin, MLA_IN_PAD, 0), (dqn, MLA_QR, 0), (dkvn, MLA_KVR, 0), (dkr, MLA_HEADS * 128, 0),
                          (cosp, 128, 0), (sinp, 128, 0)],
                    pars=[gq, gkv], outs=[(MLA_IN_PAD, BF16)], accs=[MLA_QR, MLA_KVR])


def _mla_probs(q, kn, kr, i, tq):
    s = _nt(q[:, :128], kn) + _nt(q[:, 128:], kr)
    row = (i * tq + lax.broadcasted_iota(jnp.int32, s.shape, 0)) // CHUNK
    col = lax.broadcasted_iota(jnp.int32, s.shape, 1) // CHUNK
    s = jnp.where(col <= row, s, -jnp.inf)
    e = jnp.exp(s - jnp.max(s, axis=-1, keepdims=True))
    return e / jnp.sum(e, axis=-1, keepdims=True)


def _mla_attn_fwd(qr, knv, kr, tq=256):
    S = qr.shape[0]
    tq = min(tq, S)

    def body(q_ref, kn_ref, v_ref, kr_ref, o_ref):
        pr = _mla_probs(q_ref[...], kn_ref[...], kr_ref[...], pl.program_id(1), tq)
        o_ref[...] = _nn(pr.astype(BF16), v_ref[...])

    return pl.pallas_call(
        body, name="mla_attn_fwd", grid=(MLA_HEADS, S // tq),
        in_specs=[pl.BlockSpec((tq, MLA_QH), lambda h, i: (i, h)),
                  pl.BlockSpec((S, 128), lambda h, i: (0, h)),
                  pl.BlockSpec((S, 128), lambda h, i: (0, MLA_HEADS + h)),
                  pl.BlockSpec((S, 128), lambda h, i: (0, 0))],
        out_specs=pl.BlockSpec((tq, 128), lambda h, i: (i, h)),
        out_shape=jax.ShapeDtypeStruct((S, MLA_HEADS * MLA_V), F32),
        compiler_params=_cparams(("parallel", "arbitrary")),
    )(qr, knv, knv, kr)


def _mla_attn_bwd(qr, knv, kr, o, do, tq=256):
    S = qr.shape[0]
    tq = min(tq, S)
    W = MLA_HEADS * 128

    def body(q_ref, kn_ref, v_ref, kr_ref, o_ref, do_ref, dq_ref, dkn_ref, dv_ref, dkr_ref, dkn_acc, dv_acc):
        i = pl.program_id(1)
        q, kn, v, krv = q_ref[...], kn_ref[...], v_ref[...], kr_ref[...]
        pr = _mla_probs(q, kn, krv, i, tq)
        dov = do_ref[...]
        delta = jnp.sum(dov * o_ref[...], axis=-1, keepdims=True)
        dob = dov.astype(BF16)
        ds = (pr * (_nt(dob, v) - delta)).astype(BF16)
        dq_ref[...] = jnp.concatenate([_nn(ds, kn), _nn(ds, krv)], axis=1)

        @pl.when(i == 0)
        def _():
            dkn_acc[...] = jnp.zeros(dkn_acc.shape, F32)
            dv_acc[...] = jnp.zeros(dv_acc.shape, F32)
            dkr_ref[...] = jnp.zeros(dkr_ref.shape, F32)

        dkn_acc[...] += _tn(ds, q[:, :128])
        dkr_ref[...] += _tn(ds, q[:, 128:])
        dv_acc[...] += _tn(pr.astype(BF16), dob)

        @pl.when(i == pl.num_programs(1) - 1)
        def _():
            dkn_ref[...] = dkn_acc[...].astype(dkn_ref.dtype)
            dv_ref[...] = dv_acc[...].astype(dv_ref.dtype)

    return pl.pallas_call(
        body, name="mla_attn_bwd", grid=(MLA_HEADS, S // tq),
        in_specs=[pl.BlockSpec((tq, MLA_QH), lambda h, i: (i, h)),
                  pl.BlockSpec((S, 128), lambda h, i: (0, h)),
                  pl.BlockSpec((S, 128), lambda h, i: (0, MLA_HEADS + h)),
                  pl.BlockSpec((S, 128), lambda h, i: (0, 0)),
                  pl.BlockSpec((tq, 128), lambda h, i: (i, h)),
                  pl.BlockSpec((tq, 128), lambda h, i: (i, h))],
        out_specs=[pl.BlockSpec((tq, MLA_QH), lambda h, i: (i, h)),
                   pl.BlockSpec((S, 128), lambda h, i: (0, h)),
                   pl.BlockSpec((S, 128), lambda h, i: (0, h)),
                   pl.BlockSpec((S, 128), lambda h, i: (0, h))],
        out_shape=[jax.ShapeDtypeStruct((S, MLA_HEADS * MLA_QH), F32), jax.ShapeDtypeStruct((S, W), BF16),
                   jax.ShapeDtypeStruct((S, W), BF16), jax.ShapeDtypeStruct((S, W), F32)],
        scratch_shapes=[pltpu.VMEM((S, 128), F32), pltpu.VMEM((S, 128), F32)],
        compiler_params=_cparams(("parallel", "arbitrary")),
    )(qr, knv, knv, kr, o, do)


CONV_TILE = 256


def _shift_down(v, n):
    row = lax.broadcasted_iota(jnp.int32, v.shape, 0)
    return jnp.where(row >= n, pltpu.roll(v, n, 0), 0.0)


def _shift_up(v, n):
    S = v.shape[0]
    row = lax.broadcasted_iota(jnp.int32, v.shape, 0)
    return jnp.where(row < S - n, pltpu.roll(v, S - n, 0), 0.0)


def _conv_specs(S, n_extra_cols):
    nt = D_MODEL // CONV_TILE
    specs = [pl.BlockSpec((S, CONV_TILE), functools.partial(lambda j, o: (0, o + j), o=part * nt))
             for part in range(3)]
    specs.append(pl.BlockSpec((8, CONV_TILE), lambda j: (0, j)))
    specs += [pl.BlockSpec((S, CONV_TILE), lambda j: (0, j)) for _ in range(n_extra_cols)]
    return specs


def _conv_fwd(bcu, w8):
    S = bcu.shape[0]

    def body(b_ref, c_ref, u_ref, w_ref, y_ref):
        cu = c_ref[...] * u_ref[...]
        z = w_ref[2:3, :] * cu + w_ref[1:2, :] * _shift_down(cu, 1) + w_ref[0:1, :] * _shift_down(cu, 2)
        y_ref[...] = (b_ref[...] * z).astype(y_ref.dtype)

    return pl.pallas_call(
        body, name="conv_fwd", grid=(D_MODEL // CONV_TILE,), in_specs=_conv_specs(S, 0),
        out_specs=pl.BlockSpec((S, CONV_TILE), lambda j: (0, j)),
        out_shape=jax.ShapeDtypeStruct((S, D_MODEL), BF16),
        compiler_params=_cparams(("parallel",)),
    )(bcu, bcu, bcu, w8)


def _conv_bwd(bcu, w8, dy):
    S = bcu.shape[0]

    def body(b_ref, c_ref, u_ref, w_ref, dy_ref, db_ref, dc_ref, du_ref, dw_ref):
        b, c, u, dyv = b_ref[...], c_ref[...], u_ref[...], dy_ref[...]
        w0, w1, w2 = w_ref[0:1, :], w_ref[1:2, :], w_ref[2:3, :]
        cu = c * u
        cu1, cu2 = _shift_down(cu, 1), _shift_down(cu, 2)
        z = w2 * cu + w1 * cu1 + w0 * cu2
        dz = dyv * b
        db_ref[...] = (dyv * z).astype(db_ref.dtype)
        dcu = w2 * dz + w1 * _shift_up(dz, 1) + w0 * _shift_up(dz, 2)
        dc_ref[...] = (dcu * u).astype(dc_ref.dtype)
        du_ref[...] = (dcu * c).astype(du_ref.dtype)
        dw_ref[...] = jnp.zeros(dw_ref.shape, F32)
        dw_ref[0:1, :] = _colsum(dz * cu2)
        dw_ref[1:2, :] = _colsum(dz * cu1)
        dw_ref[2:3, :] = _colsum(dz * cu)

    col = pl.BlockSpec((S, CONV_TILE), lambda j: (0, j))
    return pl.pallas_call(
        body, name="conv_bwd", grid=(D_MODEL // CONV_TILE,), in_specs=_conv_specs(S, 1),
        out_specs=[col, col, col, pl.BlockSpec((8, CONV_TILE), lambda j: (0, j))],
        out_shape=[jax.ShapeDtypeStruct((S, D_MODEL), BF16)] * 3 + [jax.ShapeDtypeStruct((8, D_MODEL), F32)],
        compiler_params=_cparams(("parallel",)),
    )(bcu, bcu, bcu, w8, dy)


def _adamw(w, m, v, gs, name):
    L, R, Cn = w.shape
    assert len(gs) == L
    tr = R if R <= 512 else 512
    assert R % tr == 0

    def body(w_ref, m_ref, v_ref, *rest):
        g_refs, (go_ref, d_ref, nm_ref, nv_ref) = rest[:L], rest[L:]
        layer = pl.program_id(0)
        gv = g_refs[0][...]
        for k in range(1, L):
            gv = jnp.where(layer == k, g_refs[k][...], gv)
        nm = ADAM_B1 * m_ref[...] + (1.0 - ADAM_B1) * gv
        nv = ADAM_B2 * v_ref[...] + (1.0 - ADAM_B2) * jnp.square(gv)
        m_hat = nm / (1.0 - ADAM_B1 ** ADAM_STEP)
        v_hat = nv / (1.0 - ADAM_B2 ** ADAM_STEP)
        d_ref[...] = -ADAM_LR * (m_hat / (jnp.sqrt(v_hat) + ADAM_EPS) + ADAM_WD * w_ref[...])
        go_ref[...] = gv
        nm_ref[...] = nm
        nv_ref[...] = nv

    spec = pl.BlockSpec((None, tr, Cn), lambda l, i: (l, i, 0))
    g_specs = [pl.BlockSpec((tr, Cn), functools.partial(lambda l, i, k: (jnp.where(l == k, i, 0), 0), k=k))
               for k in range(L)]
    return pl.pallas_call(
        body, name=name, grid=(L, R // tr), in_specs=[spec] * 3 + g_specs, out_specs=[spec] * 4,
        out_shape=[jax.ShapeDtypeStruct((L, R, Cn), F32)] * 4,
        compiler_params=_cparams(("arbitrary", "arbitrary")),
    )(w, m, v, *gs)


HBM_SPEC = pl.BlockSpec(memory_space=pltpu.HBM)
BOUNCE_ROWS = 256


def _place():
    return lax.axis_index("x"), lax.axis_index("y"), lax.axis_index("c")


def _other_chips(x, y):
    return [(1 - x, y), (x, 1 - y), (1 - x, 1 - y)]


def _copy_via_vmem(src, dst, buf, sems, rows):
    ch = buf.shape[1]
    n = rows // ch
    cin = lambda i: pltpu.make_async_copy(src.at[pl.ds(i * ch, ch), :], buf.at[i % 2], sems.at[i % 2])
    cout = lambda i: pltpu.make_async_copy(buf.at[i % 2], dst.at[pl.ds(i * ch, ch), :], sems.at[2 + i % 2])
    cin(0).start()
    for i in range(n):
        cin(i).wait()
        cout(i).start()
        if i + 1 < n:
            if i >= 1:
                cout(i - 1).wait()
            cin(i + 1).start()
    if n >= 2:
        cout(n - 2).wait()
    cout(n - 1).wait()


SEM_SPEC = pl.BlockSpec(memory_space=pltpu.SEMAPHORE)
ANY_SPEC = pl.BlockSpec(memory_space=pl.ANY)
VMEM_SPEC = pl.BlockSpec(memory_space=pltpu.VMEM)
EFFECT = pltpu.SideEffectType.DATAFLOW_SIDE_EFFECTING
TOKEN = (8, 128)


def _hbm(v):
    return pltpu.with_memory_space_constraint(v, pltpu.HBM)


def _ici_start(srcs, lands, after, copies, name):
    n, nl = len(srcs), len(lands)

    def body(*refs):
        src_refs, land_refs = refs[:n], refs[n:n + nl]
        send_sems, recv_sems, token = refs[n + nl + 1], refs[n + nl + 2], refs[-1]
        x, y, c = _place()
        for k, src, dst, to in copies(src_refs, land_refs, x, y, c):
            pltpu.make_async_remote_copy(src_ref=src, dst_ref=dst, send_sem=send_sems.at[k], recv_sem=recv_sems.at[k],
                                         device_id=to, device_id_type=MESH).start()
        token[...] = jnp.zeros(TOKEN, F32)

    n_copies = 3 * n
    res = pl.pallas_call(
        body, name=name,
        out_shape=(pltpu.SemaphoreType.DMA((n_copies,)), pltpu.SemaphoreType.DMA((n_copies,)),
                   *[pltpu.HBM(s.shape, s.dtype) for s in srcs], *[pltpu.HBM(l.shape, l.dtype) for l in lands],
                   jax.ShapeDtypeStruct(TOKEN, F32)),
        in_specs=[HBM_SPEC] * (n + nl) + [ANY_SPEC],
        out_specs=(SEM_SPEC, SEM_SPEC, *[HBM_SPEC] * (n + nl), VMEM_SPEC),
        input_output_aliases={t: 2 + t for t in range(n + nl)},
        compiler_params=pltpu.CompilerParams(has_side_effects=EFFECT),
    )(*[_hbm(s) for s in srcs], *[_hbm(l) for l in lands], after)
    return res[0], res[1], list(res[2:2 + n]), list(res[2 + n:2 + n + nl]), res[-1]


def _ici_wait(handle, after, copies, name):
    send_sems, recv_sems, srcs, lands, _ = handle
    n, nl = len(srcs), len(lands)

    def body(*refs):
        src_refs, land_refs = refs[:n], refs[n:n + nl]
        send_s, recv_s = refs[n + nl], refs[n + nl + 1]
        x, y, c = _place()
        for k, src, dst, to in copies(src_refs, land_refs, x, y, c):
            cp = pltpu.make_async_remote_copy(src_ref=src, dst_ref=dst, send_sem=send_s.at[k], recv_sem=recv_s.at[k],
                                              device_id=to, device_id_type=MESH)
            cp.wait_send()
            cp.wait_recv()

    res = pl.pallas_call(
        body, name=name,
        out_shape=(*[pltpu.HBM(s.shape, s.dtype) for s in srcs], *[pltpu.HBM(l.shape, l.dtype) for l in lands]),
        in_specs=[HBM_SPEC] * (n + nl) + [SEM_SPEC, SEM_SPEC, ANY_SPEC],
        out_specs=tuple([HBM_SPEC] * (n + nl)),
        input_output_aliases={t: t for t in range(n + nl)},
        compiler_params=pltpu.CompilerParams(has_side_effects=EFFECT),
    )(*srcs, *lands, send_sems, recv_sems, after)
    return list(res[:n]), list(res[n:])


def _gather_copies(halves):
    def copies(src_refs, land_refs, x, y, c):
        q = 2 * x + y
        out = []
        for t, H in enumerate(halves):
            for j, (cx, cy) in enumerate(_other_chips(x, y)):
                out.append((3 * t + j, src_refs[t].at[pl.ds(c * H, H), :], land_refs[t].at[q, pl.ds(c * H, H), :],
                            (cx, cy, c)))
        return out
    return copies


def _gather_wait_copies(halves):
    def copies(src_refs, land_refs, x, y, c):
        out = []
        for t, H in enumerate(halves):
            for j, (cx, cy) in enumerate(_other_chips(x, y)):
                out.append((3 * t + j, src_refs[t].at[pl.ds(c * H, H), :],
                            land_refs[t].at[2 * cx + cy, pl.ds(c * H, H), :], (cx, cy, c)))
        return out
    return copies


def _gather_start(ops, after, name):
    lands = [lax.empty((N_CHIPS,) + o.shape, o.dtype) for o in ops]
    return _ici_start(ops, lands, after, _gather_copies([o.shape[0] // 2 for o in ops]), name)


def _gather_wait(handle, after, name):
    halves = [s.shape[0] // 2 for s in handle[2]]
    return _ici_wait(handle, after, _gather_wait_copies(halves), name)


def _gather_finish(ops, lands, name):
    n = len(ops)
    halves = [o.shape[0] // 2 for o in ops]
    chunk = [min(o.shape[0], BOUNCE_ROWS) for o in ops]

    def body(*refs):
        in_refs, out_refs = refs[:n], refs[2 * n:3 * n]
        send_sems, recv_sems, local_sems = refs[3 * n:3 * n + 3]
        bufs = refs[3 * n + 3:]
        x, y, c = _place()
        q = 2 * x + y
        chips = _other_chips(x, y)
        sibling = (x, y, 1 - c)

        def copy(t, j, half):
            land = out_refs[t].at[2 * chips[j][0] + chips[j][1], pl.ds(half * halves[t], halves[t]), :]
            return pltpu.make_async_remote_copy(src_ref=land, dst_ref=land, send_sem=send_sems.at[3 * t + j],
                                                recv_sem=recv_sems.at[3 * t + j], device_id=sibling,
                                                device_id_type=MESH)

        passed = [copy(t, j, c) for t in range(n) for j in range(3)]
        for cp in passed:
            cp.start()
        for t in range(n):
            _copy_via_vmem(in_refs[t], out_refs[t].at[q], bufs[t], local_sems, ops[t].shape[0])
        for t in range(n):
            for j in range(3):
                copy(t, j, 1 - c).wait_recv()
        for cp in passed:
            cp.wait_send()

    return pl.pallas_call(
        body, name=name, in_specs=[HBM_SPEC] * (2 * n), out_specs=[HBM_SPEC] * n,
        out_shape=[jax.ShapeDtypeStruct(l.shape, l.dtype) for l in lands],
        input_output_aliases={n + t: t for t in range(n)},
        scratch_shapes=[pltpu.SemaphoreType.DMA((3 * n,)), pltpu.SemaphoreType.DMA((3 * n,)),
                        pltpu.SemaphoreType.DMA((4,))]
        + [pltpu.VMEM((2, chunk[t], ops[t].shape[1]), ops[t].dtype) for t in range(n)],
        compiler_params=pltpu.CompilerParams(vmem_limit_bytes=VMEM_LIMIT),
    )(*ops, *lands)


def _swap_halves(ops, name):
    n = len(ops)

    def body(*refs):
        in_refs, out_refs, send_sems, recv_sems = refs[:n], refs[n:2 * n], refs[2 * n], refs[2 * n + 1]
        x, y, c = _place()
        cps = []
        for t in range(n):
            H = ops[t].shape[1] // 2
            cp = pltpu.make_async_remote_copy(src_ref=in_refs[t].at[:, pl.ds((1 - c) * H, H), :],
                                              dst_ref=out_refs[t], send_sem=send_sems.at[t],
                                              recv_sem=recv_sems.at[t], device_id=(x, y, 1 - c),
                                              device_id_type=MESH)
            cp.start()
            cps.append(cp)
        for cp in cps:
            cp.wait()

    return pl.pallas_call(
        body, name=name, in_specs=[HBM_SPEC] * n, out_specs=[HBM_SPEC] * n,
        out_shape=[jax.ShapeDtypeStruct((N_CHIPS, o.shape[1] // 2, o.shape[2]), o.dtype) for o in ops],
        scratch_shapes=[pltpu.SemaphoreType.DMA((n,)), pltpu.SemaphoreType.DMA((n,))],
    )(*ops)


def _sum_rows_tile(h):
    return h if h <= 512 else 512


def _pair_sum(g, t, cq, name):
    _, a, b = g.shape
    H = a // 2
    tr = _sum_rows_tile(H)

    def body(cq_ref, g_ref, t_ref, o_ref):
        o_ref[...] = (g_ref[...].astype(F32) + t_ref[...].astype(F32)).astype(o_ref.dtype)

    grid_spec = pltpu.PrefetchScalarGridSpec(
        num_scalar_prefetch=1, grid=(N_CHIPS, H // tr),
        in_specs=[pl.BlockSpec((None, None, tr, b), lambda j, i, cq_ref: (j, cq_ref[0], i, 0)),
                  pl.BlockSpec((None, tr, b), lambda j, i, cq_ref: (j, i, 0))],
        out_specs=pl.BlockSpec((None, tr, b), lambda j, i, cq_ref: (j, i, 0)))
    return pl.pallas_call(
        body, name=name, grid_spec=grid_spec, out_shape=jax.ShapeDtypeStruct(t.shape, BF16),
        compiler_params=_cparams(("parallel", "parallel")),
    )(cq, g.reshape(N_CHIPS, 2, H, b), t)


def _scatter_copies(src_refs, land_refs, x, y, c):
    out = []
    for j, (cx, cy) in enumerate(_other_chips(x, y)):
        for t in range(len(src_refs)):
            out.append((3 * t + j, src_refs[t].at[2 * cx + cy], land_refs[t].at[j], (cx, cy, c)))
    return out


def _scatter_start(ops, after, name):
    lands = [lax.empty((3,) + o.shape[1:], o.dtype) for o in ops]
    return _ici_start(ops, lands, after, _scatter_copies, name)


def _scatter_wait(handle, after, name):
    return _ici_wait(handle, after, _scatter_copies, name)


def _chip_sum(p, t, cq, name):
    _, H, b = p.shape
    tr = _sum_rows_tile(H)

    def body(cq_ref, p_ref, t_ref, o_ref):
        acc = p_ref[...].astype(F32)
        for j in range(3):
            acc = acc + t_ref[j].astype(F32)
        o_ref[...] = acc

    grid_spec = pltpu.PrefetchScalarGridSpec(
        num_scalar_prefetch=1, grid=(H // tr,),
        in_specs=[pl.BlockSpec((None, tr, b), lambda i, cq_ref: (cq_ref[1], i, 0)),
                  pl.BlockSpec((3, tr, b), lambda i, cq_ref: (0, i, 0))],
        out_specs=pl.BlockSpec((None, tr, b), lambda i, cq_ref: (cq_ref[0], i, 0)))
    out = pl.pallas_call(
        body, name=name, grid_spec=grid_spec, out_shape=jax.ShapeDtypeStruct((2, H, b), F32),
        compiler_params=_cparams(("parallel",)),
    )(cq, p, t)
    return out.reshape(2 * H, b)


def _join_halves(ops, name):
    n = len(ops)

    def body(*refs):
        out_refs, send_sems, recv_sems = refs[n:2 * n], refs[2 * n], refs[2 * n + 1]
        x, y, c = _place()
        cps = []
        for t in range(n):
            H = ops[t].shape[0] // 2
            mine = out_refs[t].at[pl.ds(c * H, H), :]
            cp = pltpu.make_async_remote_copy(src_ref=mine, dst_ref=mine, send_sem=send_sems.at[t],
                                              recv_sem=recv_sems.at[t], device_id=(x, y, 1 - c),
                                              device_id_type=MESH)
            cp.start()
            cps.append(cp)
        for t in range(n):
            H = ops[t].shape[0] // 2
            other = out_refs[t].at[pl.ds((1 - c) * H, H), :]
            pltpu.make_async_remote_copy(src_ref=other, dst_ref=other, send_sem=send_sems.at[t],
                                         recv_sem=recv_sems.at[t], device_id=(x, y, 1 - c),
                                         device_id_type=MESH).wait_recv()
        for cp in cps:
            cp.wait_send()

    return pl.pallas_call(
        body, name=name, in_specs=[HBM_SPEC] * n, out_specs=[HBM_SPEC] * n,
        out_shape=[jax.ShapeDtypeStruct(o.shape, o.dtype) for o in ops],
        input_output_aliases={t: t for t in range(n)},
        scratch_shapes=[pltpu.SemaphoreType.DMA((n,)), pltpu.SemaphoreType.DMA((n,))],
    )(*ops)


def _reduce_scatter_start(gs, cq, after, tag):
    ts = _swap_halves(gs, "rs_swap_" + tag)
    ps = [_pair_sum(g, t, cq, "rs_pair_sum") for g, t in zip(gs, ts)]
    return _scatter_start(ps, after, "rs_scatter_start_" + tag)


def _reduce_scatter_finish(handle, cq, after, tag):
    ps, rs = _scatter_wait(handle, after, "rs_scatter_wait_" + tag)
    fs = [_chip_sum(p, r, cq, "rs_chip_sum") for p, r in zip(ps, rs)]
    return _join_halves(fs, "rs_join_" + tag)


def _all_reduce_small(v):
    n = v.shape[0]

    def body(v_ref, out_ref, buf, send_sems, recv_sems):
        x, y, c = _place()
        me = 4 * x + 2 * y + c
        buf[me] = v_ref[...]
        cps = []
        for k in range(1, 8):
            peer = (x ^ (k >> 2), y ^ ((k >> 1) & 1), c ^ (k & 1))
            cp = pltpu.make_async_remote_copy(src_ref=v_ref, dst_ref=buf.at[me], send_sem=send_sems.at[k - 1],
                                              recv_sem=recv_sems.at[k - 1], device_id=peer, device_id_type=MESH)
            cp.start()
            cps.append(cp)
        for k in range(1, 8):
            px, py, pc = x ^ (k >> 2), y ^ ((k >> 1) & 1), c ^ (k & 1)
            land = buf.at[4 * px + 2 * py + pc]
            pltpu.make_async_remote_copy(src_ref=land, dst_ref=land, send_sem=send_sems.at[k - 1],
                                         recv_sem=recv_sems.at[k - 1], device_id=(px, py, pc),
                                         device_id_type=MESH).wait_recv()
        for cp in cps:
            cp.wait_send()
        acc = buf[0]
        for d in range(1, 8):
            acc = acc + buf[d]
        out_ref[...] = acc

    vm = pl.BlockSpec(memory_space=pltpu.VMEM)
    return pl.pallas_call(
        body, name="all_reduce_small", in_specs=[vm], out_specs=vm,
        out_shape=jax.ShapeDtypeStruct((n, 128), F32),
        scratch_shapes=[pltpu.VMEM((8, n, 128), F32), pltpu.SemaphoreType.DMA((7,)), pltpu.SemaphoreType.DMA((7,))],
    )(v)


SMALL_GATHER = (16, 1024)
SMALL_FULL = sum(_size(_full_shape(n)) for n in SMALL)
SMALL_FULL_ROWS = -(-SMALL_FULL // 128 // 8) * 8


def _layer_shards(w, i, q):
    kind, j = MIXER[i % 3], i // 3
    out = {n: w[n][i].astype(BF16) for n in COMMON_BIG}
    if kind == 'gla':
        win = jnp.zeros((D_MODEL, GLA_WIN), F32)
        win = lax.dynamic_update_slice(win, w['gla_w_in'][j], (0, (GLA_SHARD - GLA_WIN_STEP) * q))
        out['gla_w_in'] = win.astype(BF16)
        out['gla_w_out'] = w['gla_w_out'][j].astype(BF16)
    elif kind == 'mla':
        out['mla_w_in'] = jnp.pad(w['mla_w_in'][j], ((0, 0), (0, MLA_IN_PAD - MLA_IN))).astype(BF16)
        for n in ('mla_w_uq', 'mla_w_ukv', 'mla_w_out'):
            out[n] = w[n][j].astype(BF16)
    else:
        out['conv_w_in'] = w['conv_w_in'][j].astype(BF16)
        out['conv_w_out'] = w['conv_w_out'][j].astype(BF16)
    return out


def _rows_joined(g):
    return g.reshape(g.shape[0] * g.shape[1], g.shape[2])


def _cols_joined(g):
    return jnp.moveaxis(g, 0, 1).reshape(g.shape[1], -1)


def _layer_weights(g, i):
    kind = MIXER[i % 3]
    W = {'w1': g['mlp_w1'], 'w2': _rows_joined(g['mlp_w2']), 'gate': _rows_joined(g['ple_w_gate']),
         'proj': g['ple_w_proj']}
    if kind == 'gla':
        parts = []
        for qq in range(N_CHIPS):
            lo = g['gla_w_in'][qq][:, :128]
            if qq > 0:
                lo = lo + g['gla_w_in'][qq - 1][:, GLA_WIN_STEP:]
            parts += [lo, g['gla_w_in'][qq][:, 128:GLA_WIN_STEP]]
        parts.append(g['gla_w_in'][N_CHIPS - 1][:, GLA_WIN_STEP:])
        W['w_in'] = jnp.concatenate(parts, axis=1)
        W['w_out'] = _rows_joined(g['gla_w_out'])
    elif kind == 'mla':
        W['w_in'] = _rows_joined(g['mla_w_in'])
        uq = _cols_joined(g['mla_w_uq']).reshape(MLA_QR, MLA_HEADS, MLA_NOPE + MLA_ROPE)
        W['w_uq'] = jnp.pad(uq, ((0, 0), (0, 0), (0, MLA_QH - MLA_NOPE - MLA_ROPE))).reshape(MLA_QR, -1)
        ukv = _cols_joined(g['mla_w_ukv']).reshape(MLA_KVR, MLA_HEADS, 2, 128)
        W['w_ukv'] = ukv.transpose(0, 2, 1, 3).reshape(MLA_KVR, -1)
        W['w_out'] = _rows_joined(g['mla_w_out'])
    else:
        W['w_in'] = g['conv_w_in']
        W['w_out'] = _rows_joined(g['conv_w_out'])
    return W


def _pack_small_shards(w):
    flat = jnp.concatenate([w[n].reshape(-1) for n in SMALL_SHARDED])
    return jnp.pad(flat, (0, _size(SMALL_GATHER) - flat.shape[0])).reshape(SMALL_GATHER)


def _unpack_small_gathered(g):
    flat, out, off = g.reshape(N_CHIPS, -1), {}, 0
    for n in SMALL_SHARDED:
        shape, ax = WSPEC[n]
        seg = flat[:, off:off + _size(shape)].reshape((N_CHIPS,) + shape)
        out[n] = jnp.moveaxis(seg, 0, ax).reshape(_full_shape(n))
        off += _size(shape)
    return out


def _pack_small(vals):
    flat = jnp.concatenate([vals[n].reshape(-1) for n in SMALL])
    return jnp.pad(flat, (0, SMALL_FULL_ROWS * 128 - flat.shape[0])).reshape(SMALL_FULL_ROWS, 128)


def _unpack_small(packed, q):
    flat = packed.reshape(-1)
    out, off = {}, 0
    for n in SMALL:
        shape, ax = WSPEC[n]
        full = flat[off:off + _size(_full_shape(n))].reshape(_full_shape(n))
        off += _size(_full_shape(n))
        out[n] = full if ax is None else lax.dynamic_slice_in_dim(full, q * shape[ax], shape[ax], axis=ax)
    return out


def _row_shards(dw):
    return dw.reshape(N_CHIPS, dw.shape[0] // N_CHIPS, dw.shape[1])


def _col_shards(dw):
    return jnp.moveaxis(dw.reshape(dw.shape[0], N_CHIPS, -1), 1, 0)


def _row(v):
    return v.reshape(1, -1)


def _layer_fwd(i, xin, xin_b, p_i, W, sm, cosp, sinp):
    kind, j = MIXER[i % 3], i // 3
    sv = {'xin': xin, 'xin_b': xin_b}
    if kind == 'gla':
        w_up = jnp.pad(sm['gla_w_gate_up'][j].astype(BF16), ((0, 128 - GLA_RANK), (0, 0)))
        pin = _mm(xin_b, W['w_in'], name="gla_in", tn=640)
        la = _mm(pin, w_up, name="gla_gate", K=128, tk=128, a_off=(0, (GLA_IN_PAD - 128) // 128), tn=512,
                 extras=[(_row(sm['gla_b_gate'][j]), 'n')],
                 epilogue=lambda acc, b: (_log_sigmoid(acc + b) * (1.0 / GLA_TAU),))
        o, states = _gla_fwd(pin, la)
        yb = _gla_post_fwd(o, pin, _row(sm['gla_norm_g'][j]))
        h = _mm(yb, W['w_out'], name="mix_out")
        sv.update(w_up=w_up, pin=pin, la=la, o=o, states=states, yb=yb)
    elif kind == 'mla':
        gq, gkv = sm['mla_q_norm'][j:j + 1], sm['mla_kv_norm'][j:j + 1]
        cin = _mm(xin_b, W['w_in'], name="mla_in", tn=640)
        qn, kvn, kr = _mla_norm_fwd(cin, gq, gkv, cosp, sinp)
        qr = _mla_qrope_fwd(_mm(qn, W['w_uq'], name="mla_uq"), cosp, sinp)
        knv = _mm(kvn, W['w_ukv'], name="mla_ukv", out_dtypes=(BF16,))
        o = _mla_attn_fwd(qr, knv, kr)
        ob = o.astype(BF16)
        h = _mm(ob, W['w_out'], name="mix_out")
        sv.update(gq=gq, gkv=gkv, cin=cin, qn=qn, kvn=kvn, kr=kr, qr=qr, knv=knv, o=o, ob=ob)
    else:
        w8 = jnp.pad(sm['conv_w'][j], ((0, 5), (0, 0)))
        bcu = _mm(xin_b, W['w_in'], name="conv_in", tn=768, b_sh=True)
        yb = _conv_fwd(bcu, w8)
        h = _mm(yb, W['w_out'], name="mix_out")
        sv.update(w8=w8, bcu=bcu, yb=yb)
    g0, b0 = _row(sm['ln_g'][i, 0]), _row(sm['ln_b'][i, 0])
    g1, b1 = _row(sm['ln_g'][i, 1]), _row(sm['ln_b'][i, 1])
    x1, x1b = _ln_fwd(xin, h, g0, b0, "ln_fwd")
    ub, ab = _mm(x1b, W['w1'], name="mlp_up", out_dtypes=(BF16, BF16), b_sh=True,
                 epilogue=lambda acc: (acc, jnp.square(jnp.maximum(acc, 0.0))))
    m = _mm(ab, W['w2'], name="mlp_down")
    x2, x2b = _ln_fwd(x1, m, g1, b1, "ln_fwd")
    pp = _mm(p_i, W['proj'], name="ple_proj", tn=256, b_sh=True)
    z, x3, x3b = _mm(x2b, W['gate'], name="ple_gate", out_dtypes=(F32, F32, BF16),
                     extras=[(x2, 'mn'), (pp, 'mn')],
                     epilogue=lambda acc, xv, pv: (acc,) + (xv + jax.nn.sigmoid(acc) * pv,) * 2)
    sv.update(h=h, x1=x1, x1b=x1b, ub=ub, ab=ab, m=m, x2b=x2b, pp=pp, z=z, g0=g0, g1=g1)
    return x3, x3b, sv


def _layer_bwd(i, dx, p_i, W, sm, sv, cosp, sinp, token):
    kind, j = MIXER[i % 3], i // 3
    big, small = {}, {}
    dpp_b, dz_b = _ple_bwd_gate(dx, sv['z'], sv['pp'], token)
    big['ple_w_proj'] = _mm(p_i, dpp_b, ta=True, name="ple_proj_dw", tn=256, out_sh=True, out_dtypes=(BF16,))
    big['ple_w_gate'] = _row_shards(_mm(sv['x2b'], dz_b, ta=True, name="dw_dd", out_dtypes=(BF16,)))
    dx2 = _mm(dz_b, W['gate'], tb=True, name="dx_dd_add", extras=[(dx, 'mn')], epilogue=lambda acc, r: (acc + r,))
    dv1, dv1b, dg1, db1 = _ln_bwd(sv['x1'], sv['m'], sv['g1'], dx2, "ln_bwd")
    big['mlp_w2'] = _row_shards(_mm(sv['ab'], dv1b, ta=True, name="mlp_down_dw", out_dtypes=(BF16,)))
    dub = _mm(dv1b, W['w2'], tb=True, name="mlp_down_dx", out_dtypes=(BF16,), extras=[(sv['ub'], 'mn')],
              epilogue=lambda acc, u: (acc * (2.0 * jnp.maximum(u.astype(F32), 0.0)),))
    big['mlp_w1'] = _mm(sv['x1b'], dub, ta=True, name="mlp_up_dw", out_sh=True, out_dtypes=(BF16,))
    dx1 = _mm(dub, W['w1'], tb=True, name="mlp_up_dx", b_sh=True, extras=[(dv1, 'mn')],
              epilogue=lambda acc, r: (acc + ALPHA * r,))
    dv0, dv0b, dg0, db0 = _ln_bwd(sv['xin'], sv['h'], sv['g0'], dx1, "ln_bwd")
    small['ln_g'] = jnp.stack([dg0[0], dg1[0]])
    small['ln_b'] = jnp.stack([db0[0], db1[0]])
    resid = dict(extras=[(dv0, 'mn')], epilogue=lambda acc, r: (acc + ALPHA * r,))
    if kind == 'gla':
        big['gla_w_out'] = _row_shards(_mm(sv['yb'], dv0b, ta=True, name="dw_dd", out_dtypes=(BF16,)))
        dy = _mm(dv0b, W['w_out'], tb=True, name="dx_dd")
        do, dr_b, dng = _gla_post_bwd(dy, sv['o'], sv['pin'], _row(sm['gla_norm_g'][j]))
        dq_b, dk_b, dvv_b, dla = _gla_bwd(sv['pin'], sv['la'], sv['states'], do)
        dzg_b, dbg = _gla_gate_bwd(dla, sv['la'])
        dw_up = _mm(sv['pin'], dzg_b, ta=True, name="gla_gate_dw", M=128, tm=128,
                    a_off=(0, (GLA_IN_PAD - 128) // 128))
        dglr_b = _mm(dzg_b, sv['w_up'], tb=True, name="gla_gate_dx", out_dtypes=(BF16,))
        dpin_b = jnp.concatenate([dq_b, dk_b, dvv_b, dr_b, dglr_b], axis=1)
        dw_in = _mm(sv['xin_b'], dpin_b, ta=True, name="gla_in_dw", tn=640, out_dtypes=(BF16,))
        dxin = _mm(dpin_b, W['w_in'], tb=True, name="gla_in_dx", tk=640, **resid)
        big['gla_w_in'] = jnp.stack([dw_in[:, GLA_WIN_STEP * qq:GLA_WIN_STEP * qq + GLA_WIN]
                                     for qq in range(N_CHIPS)])
        small.update(gla_w_gate_up=dw_up[:GLA_RANK], gla_b_gate=dbg[0], gla_norm_g=dng[0])
    elif kind == 'mla':
        big['mla_w_out'] = _row_shards(_mm(sv['ob'], dv0b, ta=True, name="dw_dd", out_dtypes=(BF16,)))
        do = _mm(dv0b, W['w_out'], tb=True, name="dx_dd")
        dqr, dkn_b, dvv_b, dkr = _mla_attn_bwd(sv['qr'], sv['knv'], sv['kr'], sv['o'], do)
        dq_b = _mla_qrope_bwd(dqr, cosp, sinp)
        dw_uq = _mm(sv['qn'], dq_b, ta=True, name="mla_up_dw", out_dtypes=(BF16,))
        dqn = _mm(dq_b, W['w_uq'], tb=True, name="mla_up_dx")
        dknv_b = jnp.concatenate([dkn_b, dvv_b], axis=1)
        dw_ukv = _mm(sv['kvn'], dknv_b, ta=True, name="mla_up_dw", out_dtypes=(BF16,))
        dkvn = _mm(dknv_b, W['w_ukv'], tb=True, name="mla_up_dx")
        dcin_b, dgq, dgkv = _mla_norm_bwd(sv['cin'], dqn, dkvn, dkr, sv['gq'], sv['gkv'], cosp, sinp)
        big['mla_w_in'] = _row_shards(_mm(sv['xin_b'], dcin_b, ta=True, name="mla_in_dw", tn=640,
                                          out_dtypes=(BF16,)))
        dxin = _mm(dcin_b, W['w_in'], tb=True, name="mla_in_dx", tk=640, **resid)
        big['mla_w_uq'] = _col_shards(
            dw_uq.reshape(MLA_QR, MLA_HEADS, MLA_QH)[:, :, :MLA_NOPE + MLA_ROPE].reshape(MLA_QR, -1))
        big['mla_w_ukv'] = _col_shards(
            dw_ukv.reshape(MLA_KVR, 2, MLA_HEADS, 128).transpose(0, 2, 1, 3).reshape(MLA_KVR, -1))
        small.update(mla_q_norm=dgq[0], mla_kv_norm=dgkv[0])
    else:
        big['conv_w_out'] = _row_shards(_mm(sv['yb'], dv0b, ta=True, name="dw_dd", out_dtypes=(BF16,)))
        dy = _mm(dv0b, W['w_out'], tb=True, name="dx_dd")
        db_b, dc_b, du_b, dw8 = _conv_bwd(sv['bcu'], sv['w8'], dy)
        dbcu_b = jnp.concatenate([db_b, dc_b, du_b], axis=1)
        big['conv_w_in'] = _mm(sv['xin_b'], dbcu_b, ta=True, name="conv_in_dw", tn=768, out_sh=True,
                               out_dtypes=(BF16,))
        dxin = _mm(dbcu_b, W['w_in'], tb=True, name="conv_in_dx", tk=768, b_sh=True, **resid)
        small['conv_w'] = dw8[:3]
    return dxin, big, small


def _rope_tables(positions):
    inv_freq = ROPE_BASE ** (-jnp.arange(0, MLA_ROPE // 2, dtype=F32) * (2.0 / MLA_ROPE))
    ang = positions.astype(F32)[:, None] * inv_freq
    zeros = jnp.zeros((positions.shape[0], 64), F32)
    return (jnp.concatenate([jnp.cos(ang), jnp.cos(ang), zeros], axis=1),
            jnp.concatenate([jnp.sin(ang), jnp.sin(ang), zeros], axis=1))


def _start_gathers(w, q):
    token, started = jnp.zeros(TOKEN, F32), []
    for i in range(DEPTH):
        sh = _layer_shards(w, i, q)
        names = list(sh)
        ops = [sh[n] for n in names]
        if i == 0:
            ops.append(_pack_small_shards(w))
        handle = _gather_start(ops, token, "ag_start_l%d" % i)
        token = handle[4]
        started.append((handle, names))
    return started, token


def _finish_gather(started, i, after):
    handle, names = started[i]
    srcs, lands = _gather_wait(handle, after, "ag_wait_l%d" % i)
    got = _gather_finish(srcs, lands, "ag_finish_l%d" % i)
    return dict(zip(names, got)), got[-1]


def _local_shard_grad(name, g, q):
    if name == 'gla_w_in':
        return lax.dynamic_slice_in_dim(g, (GLA_SHARD - GLA_WIN_STEP) * q, GLA_SHARD, axis=1)
    if name == 'mla_w_in':
        return g[:, :MLA_IN]
    return g


def kernel(x, p, positions, gla_w_in, gla_w_gate_up, gla_b_gate, gla_norm_g, gla_w_out, mla_w_in, mla_q_norm, mla_kv_norm, mla_w_uq, mla_w_ukv, mla_w_out, conv_w_in, conv_w, conv_w_out, ln_g, ln_b, mlp_w1, mlp_w2, ple_w_gate, ple_w_proj, loss_target, m_gla_w_in, m_gla_w_gate_up, m_gla_b_gate, m_gla_norm_g, m_gla_w_out, m_mla_w_in, m_mla_q_norm, m_mla_kv_norm, m_mla_w_uq, m_mla_w_ukv, m_mla_w_out, m_conv_w_in, m_conv_w, m_conv_w_out, m_ln_g, m_ln_b, m_mlp_w1, m_mlp_w2, m_ple_w_gate, m_ple_w_proj, v_gla_w_in, v_gla_w_gate_up, v_gla_b_gate, v_gla_norm_g, v_gla_w_out, v_mla_w_in, v_mla_q_norm, v_mla_kv_norm, v_mla_w_uq, v_mla_w_ukv, v_mla_w_out, v_conv_w_in, v_conv_w, v_conv_w_out, v_ln_g, v_ln_b, v_mlp_w1, v_mlp_w2, v_ple_w_gate, v_ple_w_proj):
    args = locals()
    w = {n: args[n] for n in WNAMES}
    m = {n: args['m_' + n] for n in WNAMES}
    v = {n: args['v_' + n] for n in WNAMES}
    q = 2 * lax.axis_index("x") + lax.axis_index("y")
    cq = jnp.stack([lax.axis_index("c"), q]).astype(jnp.int32)

    cosp, sinp = _rope_tables(positions[0])
    started, after = _start_gathers(w, q)
    xin, saved, layers, sm = x[0], [], [], None
    xin_b = xin.astype(BF16)
    for i in range(DEPTH):
        got, last = _finish_gather(started, i, after)
        if i == 0:
            sm = _unpack_small_gathered(last)
            sm['mla_q_norm'], sm['mla_kv_norm'] = w['mla_q_norm'], w['mla_kv_norm']
        layers.append(_layer_weights(got, i))
        xin, xin_b, sv = _layer_fwd(i, xin, xin_b, p[i, 0], layers[i], sm, cosp, sinp)
        saved.append(sv)
        after = xin
    dx, loss_cols = _loss_head(xin, loss_target[0])
    loss = lax.psum(jnp.sum(loss_cols[0]), ("x", "y", "c"))

    gbig = {n: [None] * WSPEC[n][0][0] for n in BIG}
    gsmall = {n: [None] * _full_shape(n)[0] for n in SMALL}

    def finish(pending, after):
        handle, names, i = pending
        for n, g in zip(names, _reduce_scatter_finish(handle, cq, after, "l%d" % i)):
            gbig[n][i if n in COMMON_BIG else i // 3] = _local_shard_grad(n, g, q)

    pending, token = None, jnp.zeros(TOKEN, F32)
    for i in reversed(range(DEPTH)):
        dx, big, small = _layer_bwd(i, dx, p[i, 0], layers[i], sm, saved[i], cosp, sinp, token)
        names = list(big)
        handle = _reduce_scatter_start([big[n] for n in names], cq, jnp.zeros(TOKEN, F32), "l%d" % i)
        if pending is not None:
            finish(pending, dx)
        pending, token = (handle, names, i), handle[4]
        for n, g in small.items():
            gsmall[n][i if n in ('ln_g', 'ln_b') else i // 3] = g
    finish(pending, token)
    gsm = _unpack_small(_all_reduce_small(_pack_small({n: jnp.stack(g) for n, g in gsmall.items()})), q)

    grad, delta, new_m, new_v = {}, {}, {}, {}
    for n in BIG:
        grad[n], delta[n], new_m[n], new_v[n] = _adamw(w[n], m[n], v[n], gbig[n], "adamw_" + n)
    total = sum(_size(WSPEC[n][0]) for n in SMALL)
    rows = -(-total // 128 // 8) * 8

    def pack(dct):
        flat = jnp.concatenate([dct[n].reshape(-1) for n in SMALL])
        return jnp.pad(flat, (0, rows * 128 - total), constant_values=1.0).reshape(1, rows, 128)

    res = _adamw(pack(w), pack(m), pack(v), [pack(gsm)[0]], "adamw_small")
    for out, packed in zip((grad, delta, new_m, new_v), res):
        flat, off = packed.reshape(-1), 0
        for n in SMALL:
            sz = _size(WSPEC[n][0])
            out[n] = flat[off:off + sz].reshape(WSPEC[n][0])
            off += sz
    return (loss, dx[None], *[grad[n] for n in WNAMES], *[delta[n] for n in WNAMES],
            *[new_m[n] for n in WNAMES], *[new_v[n] for n in WNAMES])
```

```python
import functools

import numpy as np
import jax
import jax.numpy as jnp
from jax import lax
from jax.experimental import pallas as pl
from jax.experimental.pallas import tpu as pltpu

F32 = jnp.float32
BF16 = jnp.bfloat16
MESH = pl.DeviceIdType.MESH

D_MODEL = 1024
DEPTH = 4
CHUNK = 64
ALPHA = (2 * DEPTH) ** 0.25
LN_EPS = 1e-5
RMS_EPS = 1e-6
PLE_DIM = 256
D_FF = 4 * D_MODEL
GLA_HEADS = 4
GLA_DK = 128
GLA_DV = 256
GLA_RANK = 16
GLA_TAU = 16.0
GLA_HK = GLA_HEADS * GLA_DK
GLA_HV = GLA_HEADS * GLA_DV
GLA_IN = 2 * GLA_HK + GLA_HV + D_MODEL + GLA_RANK
GLA_IN_PAD = 2 * GLA_HK + GLA_HV + D_MODEL + 128
GLA_SHARD = GLA_IN // 4
GLA_WIN = 896
GLA_WIN_STEP = 768
MLA_HEADS = 8
MLA_NOPE = 128
MLA_ROPE = 64
MLA_V = 128
MLA_QR = 256
MLA_KVR = 256
MLA_IN = MLA_QR + MLA_KVR + MLA_ROPE
MLA_IN_PAD = MLA_QR + MLA_KVR + 128
MLA_QH = 256
ROPE_BASE = 10000.0
ADAM_LR = 0.001
ADAM_B1 = 0.9
ADAM_B2 = 0.999
ADAM_EPS = 1e-08
ADAM_WD = 0.01
ADAM_STEP = 10

VMEM_LIMIT = 48 * 1024 * 1024
FULL_ROWS = 2048
N_CHIPS = 4

WSPEC = {
    'gla_w_in': ((2, 1024, 772), 2), 'gla_w_gate_up': ((2, 16, 128), 2), 'gla_b_gate': ((2, 128), 1),
    'gla_norm_g': ((2, 64), 1), 'gla_w_out': ((2, 256, 1024), 1), 'mla_w_in': ((1, 256, 576), 1),
    'mla_q_norm': ((1, 256), None), 'mla_kv_norm': ((1, 256), None), 'mla_w_uq': ((1, 256, 384), 2),
    'mla_w_ukv': ((1, 256, 512), 2), 'mla_w_out': ((1, 256, 1024), 1), 'conv_w_in': ((1, 1024, 768), 2),
    'conv_w': ((1, 3, 256), 2), 'conv_w_out': ((1, 256, 1024), 1), 'ln_g': ((4, 2, 256), 2),
    'ln_b': ((4, 2, 256), 2), 'mlp_w1': ((4, 1024, 1024), 2), 'mlp_w2': ((4, 1024, 1024), 1),
    'ple_w_gate': ((4, 256, 1024), 1), 'ple_w_proj': ((4, 256, 256), 2),
}
WNAMES = list(WSPEC)
BIG = ['gla_w_in', 'gla_w_out', 'mla_w_in', 'mla_w_uq', 'mla_w_ukv', 'mla_w_out', 'conv_w_in', 'conv_w_out',
       'mlp_w1', 'mlp_w2', 'ple_w_gate', 'ple_w_proj']
SMALL_SHARDED = ['gla_w_gate_up', 'gla_b_gate', 'gla_norm_g', 'conv_w', 'ln_g', 'ln_b']
SMALL = SMALL_SHARDED + ['mla_q_norm', 'mla_kv_norm']
MIXER = ['gla', 'mla', 'conv']
LAYER_BIG = {'gla': ['gla_w_in', 'gla_w_out'], 'mla': ['mla_w_in', 'mla_w_uq', 'mla_w_ukv', 'mla_w_out'],
             'conv': ['conv_w_in', 'conv_w_out']}
COMMON_BIG = ['mlp_w1', 'mlp_w2', 'ple_w_gate', 'ple_w_proj']


def _size(shape):
    return int(np.prod(shape))


def _full_shape(name):
    shape, ax = WSPEC[name]
    if ax is None:
        return shape
    return tuple(s * N_CHIPS if i == ax else s for i, s in enumerate(shape))


def _cparams(sem=None):
    return pltpu.CompilerParams(dimension_semantics=sem, vmem_limit_bytes=VMEM_LIMIT)


def _mm(a, b, *, name, ta=False, tb=False, M=None, N=None, K=None, out_dtypes=(F32,), epilogue=None, extras=(),
        tm=1024, tn=512, tk=None, a_off=(0, 0), b_sh=False, out_sh=False):
    if M is None:
        M = a.shape[1] if ta else a.shape[0]
    if K is None:
        K = a.shape[0] if ta else a.shape[1]
    if b_sh:
        kw, nq = b.shape[1], b.shape[2]
        n_b, k_b = (kw, N_CHIPS * nq) if tb else (N_CHIPS * nq, kw)
        N = n_b if N is None else N
        assert K == k_b
    elif N is None:
        N = b.shape[0] if tb else b.shape[1]
    if tk is None:
        tk = FULL_ROWS if ta else 1024
    tm, tn, tk = min(tm, M), min(tn, N), min(tk, K)
    assert M % tm == 0 and N % tn == 0 and K % tk == 0, (name, M, N, K, tm, tn, tk)
    nk = K // tk
    n_ex, n_out = len(extras), len(out_dtypes)

    def body(a_ref, b_ref, *rest):
        ex_refs, out_refs = rest[:n_ex], rest[n_ex:n_ex + n_out]
        part = lax.dot_general(a_ref[...].astype(BF16), b_ref[...].astype(BF16),
                               ((((0,) if ta else (1,)), ((1,) if tb else (0,))), ((), ())),
                               preferred_element_type=F32)

        def finish(acc):
            res = (acc,) if epilogue is None else epilogue(acc, *[r[...] for r in ex_refs])
            for r, v in zip(out_refs, res):
                r[...] = v.astype(r.dtype)

        if nk == 1:
            finish(part)
        else:
            acc_ref = rest[-1]
            k = pl.program_id(2)

            @pl.when(k == 0)
            def _():
                acc_ref[...] = part

            @pl.when(k > 0)
            def _():
                acc_ref[...] += part

            @pl.when(k == nk - 1)
            def _():
                finish(acc_ref[...])

    if ta:
        a_spec = pl.BlockSpec((tk, tm), lambda i, j, k: (k + a_off[0], i + a_off[1]))
    else:
        a_spec = pl.BlockSpec((tm, tk), lambda i, j, k: (i + a_off[0], k + a_off[1]))
    if b_sh and tb:
        assert nq % tk == 0
        per = nq // tk
        b_spec = pl.BlockSpec((None, tn, tk), lambda i, j, k: (k // per, j, k % per))
    elif b_sh:
        assert nq % tn == 0
        per = nq // tn
        b_spec = pl.BlockSpec((None, tk, tn), lambda i, j, k: (j // per, k, j % per))
    elif tb:
        b_spec = pl.BlockSpec((tn, tk), lambda i, j, k: (j, k))
    else:
        b_spec = pl.BlockSpec((tk, tn), lambda i, j, k: (k, j))
    ex_specs = []
    for arr, kind in extras:
        if kind == 'mn':
            ex_specs.append(pl.BlockSpec((tm, tn), lambda i, j, k: (i, j)))
        else:
            ex_specs.append(pl.BlockSpec((1, tn), lambda i, j, k: (0, j)))
    if out_sh:
        assert (N // N_CHIPS) % tn == 0
        per_o = N // N_CHIPS // tn
        o_spec = pl.BlockSpec((None, tm, tn), lambda i, j, k: (j // per_o, i, j % per_o))
        o_shape = (N_CHIPS, M, N // N_CHIPS)
    else:
        o_spec = pl.BlockSpec((tm, tn), lambda i, j, k: (i, j))
        o_shape = (M, N)
    outs = pl.pallas_call(
        body, name=name, grid=(M // tm, N // tn, nk),
        in_specs=[a_spec, b_spec] + ex_specs,
        out_specs=[o_spec for _ in out_dtypes],
        out_shape=[jax.ShapeDtypeStruct(o_shape, d) for d in out_dtypes],
        scratch_shapes=[pltpu.VMEM((tm, tn), F32)] if nk > 1 else [],
        compiler_params=_cparams(("parallel", "parallel", "arbitrary")),
    )(a, b, *[e[0] for e in extras])
    return outs[0] if n_out == 1 else tuple(outs)


def _rowwise(fn, *, name, rows, pars=(), outs=(), accs=(), tm=256):
    S = rows[0][0].shape[0]
    tm = min(tm, S)
    assert S % tm == 0
    n_r, n_p, n_o, n_a = len(rows), len(pars), len(outs), len(accs)

    def body(*refs):
        r_refs, p_refs = refs[:n_r], refs[n_r:n_r + n_p]
        o_refs, a_refs = refs[n_r + n_p:n_r + n_p + n_o], refs[n_r + n_p + n_o:]
        o_vals, a_vals = fn([r[...] for r in r_refs], [p[...] for p in p_refs])
        for r, v in zip(o_refs, o_vals):
            r[...] = v.astype(r.dtype)
        if n_a:
            i = pl.program_id(0)

            @pl.when(i == 0)
            def _():
                for r in a_refs:
                    r[...] = jnp.zeros(r.shape, r.dtype)

            for r, v in zip(a_refs, a_vals):
                r[...] += jnp.broadcast_to(v, r.shape)

    in_specs = [pl.BlockSpec((tm, w), functools.partial(lambda i, o: (i, o), o=off)) for _, w, off in rows]
    in_specs += [pl.BlockSpec(p.shape, functools.partial(lambda i, nd: (0,) * nd, nd=p.ndim)) for p in pars]
    out_specs = [pl.BlockSpec((tm, w), lambda i: (i, 0)) for w, _ in outs]
    out_specs += [pl.BlockSpec((8, w), lambda i: (0, 0)) for w in accs]
    out_shape = [jax.ShapeDtypeStruct((S, w), d) for w, d in outs]
    out_shape += [jax.ShapeDtypeStruct((8, w), F32) for w in accs]
    res = pl.pallas_call(
        body, name=name, grid=(S // tm,), in_specs=in_specs, out_specs=out_specs, out_shape=out_shape,
        compiler_params=_cparams(("arbitrary",)),
    )(*[r[0] for r in rows], *pars)
    return tuple(res)


def _colsum(v):
    return jnp.sum(v, axis=0, keepdims=True)


def _ln_stats(v):
    mu = jnp.mean(v, axis=-1, keepdims=True)
    d = v - mu
    var = jnp.mean(d * d, axis=-1, keepdims=True)
    rstd = lax.rsqrt(var + LN_EPS)
    return d * rstd, rstd


def _ln_fwd(x, h, g, b, name):
    def fn(r, p):
        xhat, _ = _ln_stats(ALPHA * r[0] + r[1])
        y = xhat * p[0] + p[1]
        return [y, y], []
    return _rowwise(fn, name=name, rows=[(x, D_MODEL, 0), (h, D_MODEL, 0)], pars=[g, b],
                    outs=[(D_MODEL, F32), (D_MODEL, BF16)])


def _ln_bwd(x, h, g, dy, name):
    def fn(r, p):
        xhat, rstd = _ln_stats(ALPHA * r[0] + r[1])
        dyv = r[2]
        dxh = dyv * p[0]
        m1 = jnp.mean(dxh, axis=-1, keepdims=True)
        m2 = jnp.mean(dxh * xhat, axis=-1, keepdims=True)
        dv = rstd * (dxh - m1 - xhat * m2)
        return [dv, dv], [_colsum(dyv * xhat), _colsum(dyv)]
    return _rowwise(fn, name=name, rows=[(x, D_MODEL, 0), (h, D_MODEL, 0), (dy, D_MODEL, 0)], pars=[g],
                    outs=[(D_MODEL, F32), (D_MODEL, BF16)], accs=[D_MODEL, D_MODEL])


def _loss_head(y, t):
    def fn(r, p):
        d = r[0] - r[1]
        return [d * (1.0 / D_MODEL)], [_colsum(d * d) * (0.5 / D_MODEL)]
    return _rowwise(fn, name="loss_head", rows=[(y, D_MODEL, 0), (t, D_MODEL, 0)], outs=[(D_MODEL, F32)],
                    accs=[D_MODEL])


def _ple_bwd_gate(dx3, z, pp, token):
    def fn(r, p):
        s = jax.nn.sigmoid(r[1])
        return [r[0] * s, r[0] * r[2] * s * (1.0 - s)], []
    return _rowwise(fn, name="ple_bwd_gate", rows=[(dx3, D_MODEL, 0), (z, D_MODEL, 0), (pp, D_MODEL, 0)],
                    pars=[token], outs=[(D_MODEL, BF16), (D_MODEL, BF16)])


N_LEVELS = 6


def _gla_consts():
    C = CHUNK
    A = np.zeros((N_LEVELS + 3, C, C), np.float32)
    masks = np.zeros((N_LEVELS + 1, C, C), np.float32)
    r = np.arange(C)[:, None]
    u = np.arange(C)[None, :]
    for l in range(N_LEVELS):
        half = C >> (l + 1)
        mid = (r // (2 * half)) * (2 * half) + half - 1
        A[l] = np.where(r > mid, (u > mid) & (u <= r), (u > r) & (u <= mid))
        masks[l] = ((r // (2 * half)) == (u // (2 * half))) & (((r // half) % 2) != ((u // half) % 2))
    masks[N_LEVELS] = (r == u)
    A[N_LEVELS] = (u <= r)
    A[N_LEVELS + 1] = (u > r)
    A[N_LEVELS + 2] = 1.0
    A = A.reshape(-1, C)
    return A, np.ascontiguousarray(A.T), masks


def _split3(v):
    hi = v.astype(BF16)
    r1 = v - hi.astype(F32)
    mid = r1.astype(BF16)
    lo = (r1 - mid.astype(F32)).astype(BF16)
    return hi, mid, lo


def _dot_exact01(a01, v):
    hi, mid, lo = _split3(v)
    f = lambda p: jnp.dot(a01, p, preferred_element_type=F32)
    return f(hi) + f(mid) + f(lo)


def _nt(a, b):
    return lax.dot_general(a, b, (((1,), (1,)), ((), ())), preferred_element_type=F32)


def _tn(a, b):
    return lax.dot_general(a, b, (((0,), (0,)), ((), ())), preferred_element_type=F32)


def _nn(a, b):
    return jnp.dot(a, b, preferred_element_type=F32)


def _gla_chunk_terms(q, k, E, m_ref):
    C = CHUNK
    scores = m_ref[N_LEVELS] * _nt(q.astype(BF16), k.astype(BF16))
    qes, kes = [], []
    for l in range(N_LEVELS):
        El = E[l * C:(l + 1) * C]
        qe, ke = (q * El).astype(BF16), (k * El).astype(BF16)
        qes.append(qe)
        kes.append(ke)
        scores = scores + m_ref[l] * _nt(qe, ke)
    return qes, kes, scores


def _head(v, h, w):
    return v[:, h * w:(h + 1) * w]


def _gla_fwd(pin, la):
    S = pin.shape[0]
    NC = S // CHUNK
    C = CHUNK
    A, _, masks = _gla_consts()

    def body(q_ref, k_ref, v_ref, la_ref, a_ref, m_ref, o_ref, st_ref, state):
        @pl.when(pl.program_id(0) == 0)
        def _():
            state[...] = jnp.zeros(state.shape, F32)

        E_all = jnp.exp(_dot_exact01(a_ref[...], la_ref[...]))
        q_all = q_ref[...] * (GLA_DK ** -0.5)
        k_all, v_all = k_ref[...], v_ref[...]
        outs = []
        for h in range(GLA_HEADS):
            q, k, E = _head(q_all, h, GLA_DK), _head(k_all, h, GLA_DK), _head(E_all, h, GLA_DK)
            _, _, scores = _gla_chunk_terms(q, k, E, m_ref)
            Eq, Ek, Ee = E[6 * C:7 * C], E[7 * C:8 * C], E[8 * C:9 * C]
            st = state[h]
            st_ref[h] = st
            vb = _head(v_all, h, GLA_DV).astype(BF16)
            outs.append(_nn(scores.astype(BF16), vb) + _nt((q * Eq).astype(BF16), st.astype(BF16)))
            state[h] = st * jnp.concatenate([Ee] * (GLA_DV // C), axis=0) + _tn(vb, (k * Ek).astype(BF16))
        o_ref[...] = jnp.concatenate(outs, axis=1)

    return pl.pallas_call(
        body, name="gla_fwd", grid=(NC,),
        in_specs=[pl.BlockSpec((C, GLA_HK), lambda c: (c, 0)),
                  pl.BlockSpec((C, GLA_HK), lambda c: (c, 1)),
                  pl.BlockSpec((C, GLA_HV), lambda c: (c, 2 * GLA_HK // GLA_HV)),
                  pl.BlockSpec((C, GLA_HK), lambda c: (c, 0)),
                  pl.BlockSpec(A.shape, lambda c: (0, 0)),
                  pl.BlockSpec(masks.shape, lambda c: (0, 0, 0))],
        out_specs=[pl.BlockSpec((C, GLA_HV), lambda c: (c, 0)),
                   pl.BlockSpec((GLA_HEADS, None, GLA_DV, GLA_DK), lambda c: (0, c, 0, 0))],
        out_shape=[jax.ShapeDtypeStruct((S, GLA_HV), F32),
                   jax.ShapeDtypeStruct((GLA_HEADS, NC, GLA_DV, GLA_DK), F32)],
        scratch_shapes=[pltpu.VMEM((GLA_HEADS, GLA_DV, GLA_DK), F32)],
        compiler_params=_cparams(("arbitrary",)),
    )(pin, pin, pin, la, jnp.asarray(A, BF16), jnp.asarray(masks))


def _gla_bwd(pin, la, states, do):
    S = pin.shape[0]
    NC = S // CHUNK
    C = CHUNK
    A, AT, masks = _gla_consts()
    scale = GLA_DK ** -0.5

    def body(q_ref, k_ref, v_ref, la_ref, st_ref, do_ref, a_ref, at_ref, m_ref,
             dq_ref, dk_ref, dv_ref, dla_ref, dstate):
        @pl.when(pl.program_id(0) == 0)
        def _():
            dstate[...] = jnp.zeros(dstate.shape, F32)

        E_all = jnp.exp(_dot_exact01(a_ref[...], la_ref[...]))
        q_all = q_ref[...] * scale
        k_all, v_all, do_all = k_ref[...], v_ref[...], do_ref[...]
        dqs, dks, dvs, dXs = [], [], [], []
        for h in range(GLA_HEADS):
            q, k, E = _head(q_all, h, GLA_DK), _head(k_all, h, GLA_DK), _head(E_all, h, GLA_DK)
            qes, kes, scores = _gla_chunk_terms(q, k, E, m_ref)
            Eq, Ek, Ee = E[6 * C:7 * C], E[7 * C:8 * C], E[8 * C:9 * C]
            st, dst = st_ref[h], dstate[h]
            dob, vb = _head(do_all, h, GLA_DV).astype(BF16), _head(v_all, h, GLA_DV).astype(BF16)
            dstb = dst.astype(BF16)
            qEq, kEk = (q * Eq).astype(BF16), (k * Ek).astype(BF16)
            dsc = _nt(dob, vb)
            dvs.append(_tn(scores.astype(BF16), dob) + _nt(kEk, dstb))
            dqEq = _nn(dob, st.astype(BF16))
            dkEk = _nn(vb, dstb)
            Gd = (m_ref[N_LEVELS] * dsc).astype(BF16)
            dq = _nn(Gd, k.astype(BF16)) + dqEq * Eq
            dk = _tn(Gd, q.astype(BF16)) + dkEk * Ek
            dX = []
            for l in range(N_LEVELS):
                El = E[l * C:(l + 1) * C]
                G = (m_ref[l] * dsc).astype(BF16)
                dqe, dke = _nn(G, kes[l]), _tn(G, qes[l])
                dq = dq + dqe * El
                dk = dk + dke * El
                dX.append((dqe * q + dke * k) * El)
            dX.append(dqEq * q * Eq)
            dX.append(dkEk * k * Ek)
            prod = dst * st
            dEe = prod[0:C]
            for i in range(1, GLA_DV // C):
                dEe = dEe + prod[i * C:(i + 1) * C]
            dX.append(dEe * Ee)
            dXs.append(jnp.concatenate(dX, axis=0))
            dqs.append(dq * scale)
            dks.append(dk)
            dstate[h] = dst * jnp.concatenate([Ee] * (GLA_DV // C), axis=0) + _tn(dob, qEq)
        dla_ref[...] = _dot_exact01(at_ref[...], jnp.concatenate(dXs, axis=1))
        dq_ref[...] = jnp.concatenate(dqs, axis=1).astype(dq_ref.dtype)
        dk_ref[...] = jnp.concatenate(dks, axis=1).astype(dk_ref.dtype)
        dv_ref[...] = jnp.concatenate(dvs, axis=1).astype(dv_ref.dtype)

    rc = lambda c: NC - 1 - c
    return pl.pallas_call(
        body, name="gla_bwd", grid=(NC,),
        in_specs=[pl.BlockSpec((C, GLA_HK), lambda c: (rc(c), 0)),
                  pl.BlockSpec((C, GLA_HK), lambda c: (rc(c), 1)),
                  pl.BlockSpec((C, GLA_HV), lambda c: (rc(c), 2 * GLA_HK // GLA_HV)),
                  pl.BlockSpec((C, GLA_HK), lambda c: (rc(c), 0)),
                  pl.BlockSpec((GLA_HEADS, None, GLA_DV, GLA_DK), lambda c: (0, rc(c), 0, 0)),
                  pl.BlockSpec((C, GLA_HV), lambda c: (rc(c), 0)),
                  pl.BlockSpec(A.shape, lambda c: (0, 0)),
                  pl.BlockSpec(AT.shape, lambda c: (0, 0)),
                  pl.BlockSpec(masks.shape, lambda c: (0, 0, 0))],
        out_specs=[pl.BlockSpec((C, GLA_HK), lambda c: (rc(c), 0)),
                   pl.BlockSpec((C, GLA_HK), lambda c: (rc(c), 0)),
                   pl.BlockSpec((C, GLA_HV), lambda c: (rc(c), 0)),
                   pl.BlockSpec((C, GLA_HK), lambda c: (rc(c), 0))],
        out_shape=[jax.ShapeDtypeStruct((S, GLA_HK), BF16), jax.ShapeDtypeStruct((S, GLA_HK), BF16),
                   jax.ShapeDtypeStruct((S, GLA_HV), BF16), jax.ShapeDtypeStruct((S, GLA_HK), F32)],
        scratch_shapes=[pltpu.VMEM((GLA_HEADS, GLA_DV, GLA_DK), F32)],
        compiler_params=_cparams(("arbitrary",)),
    )(pin, pin, pin, la, states, do, jnp.asarray(A, BF16), jnp.asarray(AT, BF16), jnp.asarray(masks))


def _gla_post_fwd(o, pin, g):
    def fn(r, p):
        ov, rv = r
        ys = []
        for h in range(GLA_HEADS):
            oh = ov[:, h * GLA_DV:(h + 1) * GLA_DV]
            rh = rv[:, h * GLA_DV:(h + 1) * GLA_DV]
            rs = lax.rsqrt(jnp.mean(oh * oh, axis=-1, keepdims=True) + RMS_EPS)
            ys.append(oh * rs * p[0] * (rh * jax.nn.sigmoid(rh)))
        return [jnp.concatenate(ys, axis=1)], []
    return _rowwise(fn, name="gla_post_fwd", rows=[(o, GLA_HV, 0), (pin, GLA_HV, (2 * GLA_HK + GLA_HV) // GLA_HV)],
                    pars=[g], outs=[(GLA_HV, BF16)])[0]


def _gla_post_bwd(dy, o, pin, g):
    def fn(r, p):
        dyv, ov, rv = r
        dos, drs, dg = [], [], 0.0
        for h in range(GLA_HEADS):
            sl = slice(h * GLA_DV, (h + 1) * GLA_DV)
            oh, rh, dyh = ov[:, sl], rv[:, sl], dyv[:, sl]
            rs = lax.rsqrt(jnp.mean(oh * oh, axis=-1, keepdims=True) + RMS_EPS)
            xh = oh * rs
            sg = jax.nn.sigmoid(rh)
            d_on = dyh * (rh * sg)
            drs.append(dyh * (xh * p[0]) * (sg * (1.0 + rh * (1.0 - sg))))
            dg = dg + _colsum(d_on * xh)
            dxh = d_on * p[0]
            dos.append(rs * (dxh - xh * jnp.mean(dxh * xh, axis=-1, keepdims=True)))
        return [jnp.concatenate(dos, axis=1), jnp.concatenate(drs, axis=1)], [dg]
    return _rowwise(fn, name="gla_post_bwd",
                    rows=[(dy, GLA_HV, 0), (o, GLA_HV, 0), (pin, GLA_HV, (2 * GLA_HK + GLA_HV) // GLA_HV)],
                    pars=[g], outs=[(GLA_HV, F32), (GLA_HV, BF16)], accs=[GLA_DV])


def _gla_gate_bwd(dla, la):
    def fn(r, p):
        dz = r[0] * (1.0 / GLA_TAU) * (1.0 - jnp.exp(GLA_TAU * r[1]))
        return [dz], [_colsum(dz)]
    return _rowwise(fn, name="gla_gate_bwd", rows=[(dla, GLA_HK, 0), (la, GLA_HK, 0)], outs=[(GLA_HK, BF16)],
                    accs=[GLA_HK])


def _log_sigmoid(z):
    return jnp.minimum(z, 0.0) - jnp.log(1.0 + jnp.exp(-jnp.abs(z)))


def _rot_half(v):
    lane = lax.broadcasted_iota(jnp.int32, v.shape, 1)
    return jnp.where(lane < 32, -pltpu.roll(v, 96, 1), jnp.where(lane < 64, pltpu.roll(v, 32, 1), 0.0))


def _rms(v):
    rs = lax.rsqrt(jnp.mean(v * v, axis=-1, keepdims=True) + RMS_EPS)
    return v * rs, rs


def _mla_norm_fwd(cin, gq, gkv, cosp, sinp):
    def fn(r, p):
        cv, cs, sn = r
        qn, _ = _rms(cv[:, :MLA_QR])
        kvn, _ = _rms(cv[:, MLA_QR:MLA_QR + MLA_KVR])
        kr = cv[:, MLA_QR + MLA_KVR:]
        return [qn * p[0], kvn * p[1], kr * cs + _rot_half(kr) * sn], []
    return _rowwise(fn, name="mla_norm_fwd", rows=[(cin, MLA_IN_PAD, 0), (cosp, 128, 0), (sinp, 128, 0)],
                    pars=[gq, gkv], outs=[(MLA_QR, BF16), (MLA_KVR, BF16), (128, BF16)])


def _mla_qrope_fwd(q, cosp, sinp):
    scale = (MLA_NOPE + MLA_ROPE) ** -0.5

    def fn(r, p):
        qv, cs, sn = r
        parts = []
        for h in range(MLA_HEADS):
            parts.append(qv[:, h * MLA_QH:h * MLA_QH + 128] * scale)
            rp = qv[:, h * MLA_QH + 128:(h + 1) * MLA_QH]
            parts.append((rp * cs + _rot_half(rp) * sn) * scale)
        return [jnp.concatenate(parts, axis=1)], []
    W = MLA_HEADS * MLA_QH
    return _rowwise(fn, name="mla_qrope_fwd", rows=[(q, W, 0), (cosp, 128, 0), (sinp, 128, 0)],
                    outs=[(W, BF16)])[0]


def _mla_qrope_bwd(dq, cosp, sinp):
    scale = (MLA_NOPE + MLA_ROPE) ** -0.5

    def fn(r, p):
        dv, cs, sn = r
        parts = []
        for h in range(MLA_HEADS):
            parts.append(dv[:, h * MLA_QH:h * MLA_QH + 128] * scale)
            rp = dv[:, h * MLA_QH + 128:(h + 1) * MLA_QH]
            parts.append((rp * cs - _rot_half(rp) * sn) * scale)
        return [jnp.concatenate(parts, axis=1)], []
    W = MLA_HEADS * MLA_QH
    return _rowwise(fn, name="mla_qrope_bwd", rows=[(dq, W, 0), (cosp, 128, 0), (sinp, 128, 0)],
                    outs=[(W, BF16)])[0]


def _mla_norm_bwd(cin, dqn, dkvn, dkr, gq, gkv, cosp, sinp):
    def fn(r, p):
        cv, dq_, dkv_, dkr_, cs, sn = r
        outs, accs = [], []
        for (lo, hi), dn, g in (((0, MLA_QR), dq_, p[0]), ((MLA_QR, MLA_QR + MLA_KVR), dkv_, p[1])):
            xh, rs = _rms(cv[:, lo:hi])
            dxh = dn * g
            outs.append(rs * (dxh - xh * jnp.mean(dxh * xh, axis=-1, keepdims=True)))
            accs.append(_colsum(dn * xh))
        dk = dkr_[:, 0:128]
        for h in range(1, MLA_HEADS):
            dk = dk + dkr_[:, h * 128:(h + 1) * 128]
        outs.append(dk * cs - _rot_half(dk) * sn)
        return [jnp.concatenate(outs, axis=1)], accs
    return _rowwise(fn, name="mla_norm_bwd",
                    rows=[(cin, MLA_IN_PAD, 0), (dqn, MLA_QR, 0), (dkvn, MLA_KVR, 0), (dkr, MLA_HEADS * 128, 0),
                          (cosp, 128, 0), (sinp, 128, 0)],
                    pars=[gq, gkv], outs=[(MLA_IN_PAD, BF16)], accs=[MLA_QR, MLA_KVR])


def _mla_probs(q, kn, kr, i, tq):
    s = _nt(q[:, :128], kn) + _nt(q[:, 128:], kr)
    row = (i * tq + lax.broadcasted_iota(jnp.int32, s.shape, 0)) // CHUNK
    col = lax.broadcasted_iota(jnp.int32, s.shape, 1) // CHUNK
    s = jnp.where(col <= row, s, -jnp.inf)
    e = jnp.exp(s - jnp.max(s, axis=-1, keepdims=True))
    return e / jnp.sum(e, axis=-1, keepdims=True)


def _mla_attn_fwd(qr, knv, kr, tq=256):
    S = qr.shape[0]
    tq = min(tq, S)

    def body(q_ref, kn_ref, v_ref, kr_ref, o_ref):
        for i in range(S // tq):
            rows, keys = pl.ds(i * tq, tq), pl.ds(0, (i + 1) * tq)
            pr = _mla_probs(q_ref[rows, :], kn_ref[keys, :], kr_ref[keys, :], i, tq)
            o_ref[rows, :] = _nn(pr.astype(BF16), v_ref[keys, :])

    return pl.pallas_call(
        body, name="mla_attn_fwd", grid=(MLA_HEADS,),
        in_specs=[pl.BlockSpec((S, MLA_QH), lambda h: (0, h)),
                  pl.BlockSpec((S, 128), lambda h: (0, h)),
                  pl.BlockSpec((S, 128), lambda h: (0, MLA_HEADS + h)),
                  pl.BlockSpec((S, 128), lambda h: (0, 0))],
        out_specs=pl.BlockSpec((S, 128), lambda h: (0, h)),
        out_shape=jax.ShapeDtypeStruct((S, MLA_HEADS * MLA_V), F32),
        compiler_params=_cparams(("parallel",)),
    )(qr, knv, knv, kr)


def _mla_attn_bwd(qr, knv, kr, o, do, tq=256):
    S = qr.shape[0]
    tq = min(tq, S)
    W = MLA_HEADS * 128

    def body(q_ref, kn_ref, v_ref, kr_ref, o_ref, do_ref, dq_ref, dkn_ref, dv_ref, dkr_ref, dkn_acc, dv_acc):
        dkn_acc[...] = jnp.zeros(dkn_acc.shape, F32)
        dv_acc[...] = jnp.zeros(dv_acc.shape, F32)
        dkr_ref[...] = jnp.zeros(dkr_ref.shape, F32)
        for i in range(S // tq):
            rows, keys = pl.ds(i * tq, tq), pl.ds(0, (i + 1) * tq)
            q, kn, v, krv = q_ref[rows, :], kn_ref[keys, :], v_ref[keys, :], kr_ref[keys, :]
            pr = _mla_probs(q, kn, krv, i, tq)
            dov = do_ref[rows, :]
            delta = jnp.sum(dov * o_ref[rows, :], axis=-1, keepdims=True)
            dob = dov.astype(BF16)
            ds = (pr * (_nt(dob, v) - delta)).astype(BF16)
            dq_ref[rows, :] = jnp.concatenate([_nn(ds, kn), _nn(ds, krv)], axis=1)
            dkn_acc[keys, :] += _tn(ds, q[:, :128])
            dkr_ref[keys, :] += _tn(ds, q[:, 128:])
            dv_acc[keys, :] += _tn(pr.astype(BF16), dob)
        dkn_ref[...] = dkn_acc[...].astype(dkn_ref.dtype)
        dv_ref[...] = dv_acc[...].astype(dv_ref.dtype)

    head = lambda w: pl.BlockSpec((S, w), lambda h: (0, h))
    return pl.pallas_call(
        body, name="mla_attn_bwd", grid=(MLA_HEADS,),
        in_specs=[head(MLA_QH), head(128), pl.BlockSpec((S, 128), lambda h: (0, MLA_HEADS + h)),
                  pl.BlockSpec((S, 128), lambda h: (0, 0)), head(128), head(128)],
        out_specs=[head(MLA_QH), head(128), head(128), head(128)],
        out_shape=[jax.ShapeDtypeStruct((S, MLA_HEADS * MLA_QH), F32), jax.ShapeDtypeStruct((S, W), BF16),
                   jax.ShapeDtypeStruct((S, W), BF16), jax.ShapeDtypeStruct((S, W), F32)],
        scratch_shapes=[pltpu.VMEM((S, 128), F32), pltpu.VMEM((S, 128), F32)],
        compiler_params=_cparams(("parallel",)),
    )(qr, knv, knv, kr, o, do)


CONV_TILE = 256


def _shift_down(v, n):
    row = lax.broadcasted_iota(jnp.int32, v.shape, 0)
    return jnp.where(row >= n, pltpu.roll(v, n, 0), 0.0)


def _shift_up(v, n):
    S = v.shape[0]
    row = lax.broadcasted_iota(jnp.int32, v.shape, 0)
    return jnp.where(row < S - n, pltpu.roll(v, S - n, 0), 0.0)


def _conv_specs(S, n_extra_cols):
    nt = D_MODEL // CONV_TILE
    specs = [pl.BlockSpec((S, CONV_TILE), functools.partial(lambda j, o: (0, o + j), o=part * nt))
             for part in range(3)]
    specs.append(pl.BlockSpec((8, CONV_TILE), lambda j: (0, j)))
    specs += [pl.BlockSpec((S, CONV_TILE), lambda j: (0, j)) for _ in range(n_extra_cols)]
    return specs


def _conv_fwd(bcu, w8):
    S = bcu.shape[0]

    def body(b_ref, c_ref, u_ref, w_ref, y_ref):
        cu = c_ref[...] * u_ref[...]
        z = w_ref[2:3, :] * cu + w_ref[1:2, :] * _shift_down(cu, 1) + w_ref[0:1, :] * _shift_down(cu, 2)
        y_ref[...] = (b_ref[...] * z).astype(y_ref.dtype)

    return pl.pallas_call(
        body, name="conv_fwd", grid=(D_MODEL // CONV_TILE,), in_specs=_conv_specs(S, 0),
        out_specs=pl.BlockSpec((S, CONV_TILE), lambda j: (0, j)),
        out_shape=jax.ShapeDtypeStruct((S, D_MODEL), BF16),
        compiler_params=_cparams(("parallel",)),
    )(bcu, bcu, bcu, w8)


def _conv_bwd(bcu, w8, dy):
    S = bcu.shape[0]

    def body(b_ref, c_ref, u_ref, w_ref, dy_ref, db_ref, dc_ref, du_ref, dw_ref):
        b, c, u, dyv = b_ref[...], c_ref[...], u_ref[...], dy_ref[...]
        w0, w1, w2 = w_ref[0:1, :], w_ref[1:2, :], w_ref[2:3, :]
        cu = c * u
        cu1, cu2 = _shift_down(cu, 1), _shift_down(cu, 2)
        z = w2 * cu + w1 * cu1 + w0 * cu2
        dz = dyv * b
        db_ref[...] = (dyv * z).astype(db_ref.dtype)
        dcu = w2 * dz + w1 * _shift_up(dz, 1) + w0 * _shift_up(dz, 2)
        dc_ref[...] = (dcu * u).astype(dc_ref.dtype)
        du_ref[...] = (dcu * c).astype(du_ref.dtype)
        dw_ref[...] = jnp.zeros(dw_ref.shape, F32)
        dw_ref[0:1, :] = _colsum(dz * cu2)
        dw_ref[1:2, :] = _colsum(dz * cu1)
        dw_ref[2:3, :] = _colsum(dz * cu)

    col = pl.BlockSpec((S, CONV_TILE), lambda j: (0, j))
    return pl.pallas_call(
        body, name="conv_bwd", grid=(D_MODEL // CONV_TILE,), in_specs=_conv_specs(S, 1),
        out_specs=[col, col, col, pl.BlockSpec((8, CONV_TILE), lambda j: (0, j))],
        out_shape=[jax.ShapeDtypeStruct((S, D_MODEL), BF16)] * 3 + [jax.ShapeDtypeStruct((8, D_MODEL), F32)],
        compiler_params=_cparams(("parallel",)),
    )(bcu, bcu, bcu, w8, dy)


def _adamw(w, m, v, gs, name):
    L, R, Cn = w.shape
    assert len(gs) == L
    tr = R if R <= 512 else 512
    assert R % tr == 0

    def body(w_ref, m_ref, v_ref, *rest):
        g_refs, (go_ref, d_ref, nm_ref, nv_ref) = rest[:L], rest[L:]
        layer = pl.program_id(0)
        gv = g_refs[0][...]
        for k in range(1, L):
            gv = jnp.where(layer == k, g_refs[k][...], gv)
        nm = ADAM_B1 * m_ref[...] + (1.0 - ADAM_B1) * gv
        nv = ADAM_B2 * v_ref[...] + (1.0 - ADAM_B2) * jnp.square(gv)
        m_hat = nm / (1.0 - ADAM_B1 ** ADAM_STEP)
        v_hat = nv / (1.0 - ADAM_B2 ** ADAM_STEP)
        d_ref[...] = -ADAM_LR * (m_hat / (jnp.sqrt(v_hat) + ADAM_EPS) + ADAM_WD * w_ref[...])
        go_ref[...] = gv
        nm_ref[...] = nm
        nv_ref[...] = nv

    spec = pl.BlockSpec((None, tr, Cn), lambda l, i: (l, i, 0))
    g_specs = [pl.BlockSpec((tr, Cn), functools.partial(lambda l, i, k: (jnp.where(l == k, i, 0), 0), k=k))
               for k in range(L)]
    return pl.pallas_call(
        body, name=name, grid=(L, R // tr), in_specs=[spec] * 3 + g_specs, out_specs=[spec] * 4,
        out_shape=[jax.ShapeDtypeStruct((L, R, Cn), F32)] * 4,
        compiler_params=_cparams(("arbitrary", "arbitrary")),
    )(w, m, v, *gs)


HBM_SPEC = pl.BlockSpec(memory_space=pltpu.HBM)
BOUNCE_ROWS = 256


def _place():
    return lax.axis_index("x"), lax.axis_index("y"), lax.axis_index("c")


def _other_chips(x, y):
    return [(1 - x, y), (x, 1 - y), (1 - x, 1 - y)]


def _copy_via_vmem(src, dst, buf, sems, rows):
    ch = buf.shape[1]
    n = rows // ch
    cin = lambda i: pltpu.make_async_copy(src.at[pl.ds(i * ch, ch), :], buf.at[i % 2], sems.at[i % 2])
    cout = lambda i: pltpu.make_async_copy(buf.at[i % 2], dst.at[pl.ds(i * ch, ch), :], sems.at[2 + i % 2])
    cin(0).start()
    for i in range(n):
        cin(i).wait()
        cout(i).start()
        if i + 1 < n:
            if i >= 1:
                cout(i - 1).wait()
            cin(i + 1).start()
    if n >= 2:
        cout(n - 2).wait()
    cout(n - 1).wait()


SEM_SPEC = pl.BlockSpec(memory_space=pltpu.SEMAPHORE)
ANY_SPEC = pl.BlockSpec(memory_space=pl.ANY)
VMEM_SPEC = pl.BlockSpec(memory_space=pltpu.VMEM)
EFFECT = pltpu.SideEffectType.DATAFLOW_SIDE_EFFECTING
TOKEN = (8, 128)


def _hbm(v):
    return pltpu.with_memory_space_constraint(v, pltpu.HBM)


def _ici_start(srcs, lands, after, copies, name):
    n, nl = len(srcs), len(lands)

    def body(*refs):
        src_refs, land_refs = refs[:n], refs[n:n + nl]
        send_sems, recv_sems, token = refs[n + nl + 1], refs[n + nl + 2], refs[-1]
        x, y, c = _place()
        for k, src, dst, to in copies(src_refs, land_refs, x, y, c):
            pltpu.make_async_remote_copy(src_ref=src, dst_ref=dst, send_sem=send_sems.at[k], recv_sem=recv_sems.at[k],
                                         device_id=to, device_id_type=MESH).start()
        token[...] = jnp.zeros(TOKEN, F32)

    n_copies = 3 * n
    res = pl.pallas_call(
        body, name=name,
        out_shape=(pltpu.SemaphoreType.DMA((n_copies,)), pltpu.SemaphoreType.DMA((n_copies,)),
                   *[pltpu.HBM(s.shape, s.dtype) for s in srcs], *[pltpu.HBM(l.shape, l.dtype) for l in lands],
                   jax.ShapeDtypeStruct(TOKEN, F32)),
        in_specs=[HBM_SPEC] * (n + nl) + [ANY_SPEC],
        out_specs=(SEM_SPEC, SEM_SPEC, *[HBM_SPEC] * (n + nl), VMEM_SPEC),
        input_output_aliases={t: 2 + t for t in range(n + nl)},
        compiler_params=pltpu.CompilerParams(has_side_effects=EFFECT),
    )(*[_hbm(s) for s in srcs], *[_hbm(l) for l in lands], after)
    return res[0], res[1], list(res[2:2 + n]), list(res[2 + n:2 + n + nl]), res[-1]


def _ici_wait(handle, after, copies, name):
    send_sems, recv_sems, srcs, lands, _ = handle
    n, nl = len(srcs), len(lands)

    def body(*refs):
        src_refs, land_refs = refs[:n], refs[n:n + nl]
        send_s, recv_s = refs[n + nl], refs[n + nl + 1]
        x, y, c = _place()
        for k, src, dst, to in copies(src_refs, land_refs, x, y, c):
            cp = pltpu.make_async_remote_copy(src_ref=src, dst_ref=dst, send_sem=send_s.at[k], recv_sem=recv_s.at[k],
                                              device_id=to, device_id_type=MESH)
            cp.wait_send()
            cp.wait_recv()

    res = pl.pallas_call(
        body, name=name,
        out_shape=(*[pltpu.HBM(s.shape, s.dtype) for s in srcs], *[pltpu.HBM(l.shape, l.dtype) for l in lands]),
        in_specs=[HBM_SPEC] * (n + nl) + [SEM_SPEC, SEM_SPEC, ANY_SPEC],
        out_specs=tuple([HBM_SPEC] * (n + nl)),
        input_output_aliases={t: t for t in range(n + nl)},
        compiler_params=pltpu.CompilerParams(has_side_effects=EFFECT),
    )(*srcs, *lands, send_sems, recv_sems, after)
    return list(res[:n]), list(res[n:])


def _gather_copies(halves):
    def copies(src_refs, land_refs, x, y, c):
        q = 2 * x + y
        out = []
        for t, H in enumerate(halves):
            for j, (cx, cy) in enumerate(_other_chips(x, y)):
                out.append((3 * t + j, src_refs[t].at[pl.ds(c * H, H), :], land_refs[t].at[q, pl.ds(c * H, H), :],
                            (cx, cy, c)))
        return out
    return copies


def _gather_wait_copies(halves):
    def copies(src_refs, land_refs, x, y, c):
        out = []
        for t, H in enumerate(halves):
            for j, (cx, cy) in enumerate(_other_chips(x, y)):
                out.append((3 * t + j, src_refs[t].at[pl.ds(c * H, H), :],
                            land_refs[t].at[2 * cx + cy, pl.ds(c * H, H), :], (cx, cy, c)))
        return out
    return copies


def _gather_start(ops, after, name):
    lands = [lax.empty((N_CHIPS,) + o.shape, o.dtype) for o in ops]
    return _ici_start(ops, lands, after, _gather_copies([o.shape[0] // 2 for o in ops]), name)


def _gather_wait(handle, after, name):
    halves = [s.shape[0] // 2 for s in handle[2]]
    return _ici_wait(handle, after, _gather_wait_copies(halves), name)


def _gather_finish(ops, lands, name):
    n = len(ops)
    halves = [o.shape[0] // 2 for o in ops]
    chunk = [min(o.shape[0], BOUNCE_ROWS) for o in ops]

    def body(*refs):
        in_refs, out_refs = refs[:n], refs[2 * n:3 * n]
        send_sems, recv_sems, local_sems = refs[3 * n:3 * n + 3]
        bufs = refs[3 * n + 3:]
        x, y, c = _place()
        q = 2 * x + y
        chips = _other_chips(x, y)
        sibling = (x, y, 1 - c)

        def copy(t, j, half):
            land = out_refs[t].at[2 * chips[j][0] + chips[j][1], pl.ds(half * halves[t], halves[t]), :]
            return pltpu.make_async_remote_copy(src_ref=land, dst_ref=land, send_sem=send_sems.at[3 * t + j],
                                                recv_sem=recv_sems.at[3 * t + j], device_id=sibling,
                                                device_id_type=MESH)

        passed = [copy(t, j, c) for t in range(n) for j in range(3)]
        for cp in passed:
            cp.start()
        for t in range(n):
            _copy_via_vmem(in_refs[t], out_refs[t].at[q], bufs[t], local_sems, ops[t].shape[0])
        for t in range(n):
            for j in range(3):
                copy(t, j, 1 - c).wait_recv()
        for cp in passed:
            cp.wait_send()

    return pl.pallas_call(
        body, name=name, in_specs=[HBM_SPEC] * (2 * n), out_specs=[HBM_SPEC] * n,
        out_shape=[jax.ShapeDtypeStruct(l.shape, l.dtype) for l in lands],
        input_output_aliases={n + t: t for t in range(n)},
        scratch_shapes=[pltpu.SemaphoreType.DMA((3 * n,)), pltpu.SemaphoreType.DMA((3 * n,)),
                        pltpu.SemaphoreType.DMA((4,))]
        + [pltpu.VMEM((2, chunk[t], ops[t].shape[1]), ops[t].dtype) for t in range(n)],
        compiler_params=pltpu.CompilerParams(vmem_limit_bytes=VMEM_LIMIT),
    )(*ops, *lands)


def _swap_halves(ops, name):
    n = len(ops)

    def body(*refs):
        in_refs, out_refs, send_sems, recv_sems = refs[:n], refs[n:2 * n], refs[2 * n], refs[2 * n + 1]
        x, y, c = _place()
        cps = []
        for t in range(n):
            H = ops[t].shape[1] // 2
            cp = pltpu.make_async_remote_copy(src_ref=in_refs[t].at[:, pl.ds((1 - c) * H, H), :],
                                              dst_ref=out_refs[t], send_sem=send_sems.at[t],
                                              recv_sem=recv_sems.at[t], device_id=(x, y, 1 - c),
                                              device_id_type=MESH)
            cp.start()
            cps.append(cp)
        for cp in cps:
            cp.wait()

    return pl.pallas_call(
        body, name=name, in_specs=[HBM_SPEC] * n, out_specs=[HBM_SPEC] * n,
        out_shape=[jax.ShapeDtypeStruct((N_CHIPS, o.shape[1] // 2, o.shape[2]), o.dtype) for o in ops],
        scratch_shapes=[pltpu.SemaphoreType.DMA((n,)), pltpu.SemaphoreType.DMA((n,))],
    )(*ops)


def _sum_rows_tile(h):
    return h if h <= 512 else 512


def _pair_sum(g, t, cq, name):
    _, a, b = g.shape
    H = a // 2
    tr = _sum_rows_tile(H)

    def body(cq_ref, g_ref, t_ref, o_ref):
        o_ref[...] = (g_ref[...].astype(F32) + t_ref[...].astype(F32)).astype(o_ref.dtype)

    grid_spec = pltpu.PrefetchScalarGridSpec(
        num_scalar_prefetch=1, grid=(N_CHIPS, H // tr),
        in_specs=[pl.BlockSpec((None, None, tr, b), lambda j, i, cq_ref: (j, cq_ref[0], i, 0)),
                  pl.BlockSpec((None, tr, b), lambda j, i, cq_ref: (j, i, 0))],
        out_specs=pl.BlockSpec((None, tr, b), lambda j, i, cq_ref: (j, i, 0)))
    return pl.pallas_call(
        body, name=name, grid_spec=grid_spec, out_shape=jax.ShapeDtypeStruct(t.shape, BF16),
        compiler_params=_cparams(("parallel", "parallel")),
    )(cq, g.reshape(N_CHIPS, 2, H, b), t)


def _scatter_copies(src_refs, land_refs, x, y, c):
    out = []
    for j, (cx, cy) in enumerate(_other_chips(x, y)):
        for t in range(len(src_refs)):
            out.append((3 * t + j, src_refs[t].at[2 * cx + cy], land_refs[t].at[j], (cx, cy, c)))
    return out


def _scatter_start(ops, after, name):
    lands = [lax.empty((3,) + o.shape[1:], o.dtype) for o in ops]
    return _ici_start(ops, lands, after, _scatter_copies, name)


def _scatter_wait(handle, after, name):
    return _ici_wait(handle, after, _scatter_copies, name)


def _chip_sum(p, t, cq, name):
    _, H, b = p.shape
    tr = _sum_rows_tile(H)

    def body(cq_ref, p_ref, t_ref, o_ref):
        acc = p_ref[...].astype(F32)
        for j in range(3):
            acc = acc + t_ref[j].astype(F32)
        o_ref[...] = acc

    grid_spec = pltpu.PrefetchScalarGridSpec(
        num_scalar_prefetch=1, grid=(H // tr,),
        in_specs=[pl.BlockSpec((None, tr, b), lambda i, cq_ref: (cq_ref[1], i, 0)),
                  pl.BlockSpec((3, tr, b), lambda i, cq_ref: (0, i, 0))],
        out_specs=pl.BlockSpec((None, tr, b), lambda i, cq_ref: (cq_ref[0], i, 0)))
    out = pl.pallas_call(
        body, name=name, grid_spec=grid_spec, out_shape=jax.ShapeDtypeStruct((2, H, b), F32),
        compiler_params=_cparams(("parallel",)),
    )(cq, p, t)
    return out.reshape(2 * H, b)


def _join_halves(ops, name):
    n = len(ops)

    def body(*refs):
        out_refs, send_sems, recv_sems = refs[n:2 * n], refs[2 * n], refs[2 * n + 1]
        x, y, c = _place()
        cps = []
        for t in range(n):
            H = ops[t].shape[0] // 2
            mine = out_refs[t].at[pl.ds(c * H, H), :]
            cp = pltpu.make_async_remote_copy(src_ref=mine, dst_ref=mine, send_sem=send_sems.at[t],
                                              recv_sem=recv_sems.at[t], device_id=(x, y, 1 - c),
                                              device_id_type=MESH)
            cp.start()
            cps.append(cp)
        for t in range(n):
            H = ops[t].shape[0] // 2
            other = out_refs[t].at[pl.ds((1 - c) * H, H), :]
            pltpu.make_async_remote_copy(src_ref=other, dst_ref=other, send_sem=send_sems.at[t],
                                         recv_sem=recv_sems.at[t], device_id=(x, y, 1 - c),
                                         device_id_type=MESH).wait_recv()
        for cp in cps:
            cp.wait_send()

    return pl.pallas_call(
        body, name=name, in_specs=[HBM_SPEC] * n, out_specs=[HBM_SPEC] * n,
        out_shape=[jax.ShapeDtypeStruct(o.shape, o.dtype) for o in ops],
        input_output_aliases={t: t for t in range(n)},
        scratch_shapes=[pltpu.SemaphoreType.DMA((n,)), pltpu.SemaphoreType.DMA((n,))],
    )(*ops)


def _reduce_scatter_start(gs, cq, after, tag):
    ts = _swap_halves(gs, "rs_swap_" + tag)
    ps = [_pair_sum(g, t, cq, "rs_pair_sum") for g, t in zip(gs, ts)]
    return _scatter_start(ps, after, "rs_scatter_start_" + tag)


def _reduce_scatter_finish(handle, cq, after, tag):
    ps, rs = _scatter_wait(handle, after, "rs_scatter_wait_" + tag)
    fs = [_chip_sum(p, r, cq, "rs_chip_sum") for p, r in zip(ps, rs)]
    return _join_halves(fs, "rs_join_" + tag)


def _all_reduce_small(v):
    n = v.shape[0]

    def body(v_ref, out_ref, buf, send_sems, recv_sems):
        x, y, c = _place()
        me = 4 * x + 2 * y + c
        buf[me] = v_ref[...]
        cps = []
        for k in range(1, 8):
            peer = (x ^ (k >> 2), y ^ ((k >> 1) & 1), c ^ (k & 1))
            cp = pltpu.make_async_remote_copy(src_ref=v_ref, dst_ref=buf.at[me], send_sem=send_sems.at[k - 1],
                                              recv_sem=recv_sems.at[k - 1], device_id=peer, device_id_type=MESH)
            cp.start()
            cps.append(cp)
        for k in range(1, 8):
            px, py, pc = x ^ (k >> 2), y ^ ((k >> 1) & 1), c ^ (k & 1)
            land = buf.at[4 * px + 2 * py + pc]
            pltpu.make_async_remote_copy(src_ref=land, dst_ref=land, send_sem=send_sems.at[k - 1],
                                         recv_sem=recv_sems.at[k - 1], device_id=(px, py, pc),
                                         device_id_type=MESH).wait_recv()
        for cp in cps:
            cp.wait_send()
        acc = buf[0]
        for d in range(1, 8):
            acc = acc + buf[d]
        out_ref[...] = acc

    vm = pl.BlockSpec(memory_space=pltpu.VMEM)
    return pl.pallas_call(
        body, name="all_reduce_small", in_specs=[vm], out_specs=vm,
        out_shape=jax.ShapeDtypeStruct((n, 128), F32),
        scratch_shapes=[pltpu.VMEM((8, n, 128), F32), pltpu.SemaphoreType.DMA((7,)), pltpu.SemaphoreType.DMA((7,))],
    )(v)


SMALL_GATHER = (16, 1024)
SMALL_FULL = sum(_size(_full_shape(n)) for n in SMALL)
SMALL_FULL_ROWS = -(-SMALL_FULL // 128 // 8) * 8


def _layer_shards(w, i, q):
    kind, j = MIXER[i % 3], i // 3
    out = {n: w[n][i].astype(BF16) for n in COMMON_BIG}
    if kind == 'gla':
        win = jnp.zeros((D_MODEL, GLA_WIN), F32)
        win = lax.dynamic_update_slice(win, w['gla_w_in'][j], (0, (GLA_SHARD - GLA_WIN_STEP) * q))
        out['gla_w_in'] = win.astype(BF16)
        out['gla_w_out'] = w['gla_w_out'][j].astype(BF16)
    elif kind == 'mla':
        out['mla_w_in'] = jnp.pad(w['mla_w_in'][j], ((0, 0), (0, MLA_IN_PAD - MLA_IN))).astype(BF16)
        for n in ('mla_w_uq', 'mla_w_ukv', 'mla_w_out'):
            out[n] = w[n][j].astype(BF16)
    else:
        out['conv_w_in'] = w['conv_w_in'][j].astype(BF16)
        out['conv_w_out'] = w['conv_w_out'][j].astype(BF16)
    return out


def _rows_joined(g):
    return g.reshape(g.shape[0] * g.shape[1], g.shape[2])


def _cols_joined(g):
    return jnp.moveaxis(g, 0, 1).reshape(g.shape[1], -1)


def _layer_weights(g, i):
    kind = MIXER[i % 3]
    W = {'w1': g['mlp_w1'], 'w2': _rows_joined(g['mlp_w2']), 'gate': _rows_joined(g['ple_w_gate']),
         'proj': g['ple_w_proj']}
    if kind == 'gla':
        parts = []
        for qq in range(N_CHIPS):
            lo = g['gla_w_in'][qq][:, :128]
            if qq > 0:
                lo = lo + g['gla_w_in'][qq - 1][:, GLA_WIN_STEP:]
            parts += [lo, g['gla_w_in'][qq][:, 128:GLA_WIN_STEP]]
        parts.append(g['gla_w_in'][N_CHIPS - 1][:, GLA_WIN_STEP:])
        W['w_in'] = jnp.concatenate(parts, axis=1)
        W['w_out'] = _rows_joined(g['gla_w_out'])
    elif kind == 'mla':
        W['w_in'] = _rows_joined(g['mla_w_in'])
        uq = _cols_joined(g['mla_w_uq']).reshape(MLA_QR, MLA_HEADS, MLA_NOPE + MLA_ROPE)
        W['w_uq'] = jnp.pad(uq, ((0, 0), (0, 0), (0, MLA_QH - MLA_NOPE - MLA_ROPE))).reshape(MLA_QR, -1)
        ukv = _cols_joined(g['mla_w_ukv']).reshape(MLA_KVR, MLA_HEADS, 2, 128)
        W['w_ukv'] = ukv.transpose(0, 2, 1, 3).reshape(MLA_KVR, -1)
        W['w_out'] = _rows_joined(g['mla_w_out'])
    else:
        W['w_in'] = g['conv_w_in']
        W['w_out'] = _rows_joined(g['conv_w_out'])
    return W


def _pack_small_shards(w):
    flat = jnp.concatenate([w[n].reshape(-1) for n in SMALL_SHARDED])
    return jnp.pad(flat, (0, _size(SMALL_GATHER) - flat.shape[0])).reshape(SMALL_GATHER)


def _unpack_small_gathered(g):
    flat, out, off = g.reshape(N_CHIPS, -1), {}, 0
    for n in SMALL_SHARDED:
        shape, ax = WSPEC[n]
        seg = flat[:, off:off + _size(shape)].reshape((N_CHIPS,) + shape)
        out[n] = jnp.moveaxis(seg, 0, ax).reshape(_full_shape(n))
        off += _size(shape)
    return out


def _pack_small(vals):
    flat = jnp.concatenate([vals[n].reshape(-1) for n in SMALL])
    return jnp.pad(flat, (0, SMALL_FULL_ROWS * 128 - flat.shape[0])).reshape(SMALL_FULL_ROWS, 128)


def _unpack_small(packed, q):
    flat = packed.reshape(-1)
    out, off = {}, 0
    for n in SMALL:
        shape, ax = WSPEC[n]
        full = flat[off:off + _size(_full_shape(n))].reshape(_full_shape(n))
        off += _size(_full_shape(n))
        out[n] = full if ax is None else lax.dynamic_slice_in_dim(full, q * shape[ax], shape[ax], axis=ax)
    return out


def _row_shards(dw):
    return dw.reshape(N_CHIPS, dw.shape[0] // N_CHIPS, dw.shape[1])


def _col_shards(dw):
    return jnp.moveaxis(dw.reshape(dw.shape[0], N_CHIPS, -1), 1, 0)


def _row(v):
    return v.reshape(1, -1)


def _layer_fwd(i, xin, xin_b, p_i, W, sm, cosp, sinp):
    kind, j = MIXER[i % 3], i // 3
    sv = {'xin': xin, 'xin_b': xin_b}
    if kind == 'gla':
        w_up = jnp.pad(sm['gla_w_gate_up'][j].astype(BF16), ((0, 128 - GLA_RANK), (0, 0)))
        pin = _mm(xin_b, W['w_in'], name="gla_in", tn=640, tm=FULL_ROWS)
        la = _mm(pin, w_up, name="gla_gate", K=128, tk=128, a_off=(0, (GLA_IN_PAD - 128) // 128), tn=512,
                 extras=[(_row(sm['gla_b_gate'][j]), 'n')],
                 epilogue=lambda acc, b: (_log_sigmoid(acc + b) * (1.0 / GLA_TAU),))
        o, states = _gla_fwd(pin, la)
        yb = _gla_post_fwd(o, pin, _row(sm['gla_norm_g'][j]))
        h = _mm(yb, W['w_out'], name="mix_out", tm=FULL_ROWS)
        sv.update(w_up=w_up, pin=pin, la=la, o=o, states=states, yb=yb)
    elif kind == 'mla':
        gq, gkv = sm['mla_q_norm'][j:j + 1], sm['mla_kv_norm'][j:j + 1]
        cin = _mm(xin_b, W['w_in'], name="mla_in", tn=640, tm=FULL_ROWS)
        qn, kvn, kr = _mla_norm_fwd(cin, gq, gkv, cosp, sinp)
        qr = _mla_qrope_fwd(_mm(qn, W['w_uq'], name="mla_uq"), cosp, sinp)
        knv = _mm(kvn, W['w_ukv'], name="mla_ukv", out_dtypes=(BF16,))
        o = _mla_attn_fwd(qr, knv, kr)
        ob = o.astype(BF16)
        h = _mm(ob, W['w_out'], name="mix_out", tm=FULL_ROWS)
        sv.update(gq=gq, gkv=gkv, cin=cin, qn=qn, kvn=kvn, kr=kr, qr=qr, knv=knv, o=o, ob=ob)
    else:
        w8 = jnp.pad(sm['conv_w'][j], ((0, 5), (0, 0)))
        bcu = _mm(xin_b, W['w_in'], name="conv_in", tn=768, b_sh=True, tm=FULL_ROWS)
        yb = _conv_fwd(bcu, w8)
        h = _mm(yb, W['w_out'], name="mix_out", tm=FULL_ROWS)
        sv.update(w8=w8, bcu=bcu, yb=yb)
    g0, b0 = _row(sm['ln_g'][i, 0]), _row(sm['ln_b'][i, 0])
    g1, b1 = _row(sm['ln_g'][i, 1]), _row(sm['ln_b'][i, 1])
    x1, x1b = _ln_fwd(xin, h, g0, b0, "ln_fwd")
    ab = _mm(x1b, W['w1'], name="mlp_up", out_dtypes=(BF16,), b_sh=True, tm=FULL_ROWS,
             epilogue=lambda acc: (jnp.square(jnp.maximum(acc, 0.0)),))
    m = _mm(ab, W['w2'], name="mlp_down", tk=D_FF)
    x2, x2b = _ln_fwd(x1, m, g1, b1, "ln_fwd")
    pp = _mm(p_i, W['proj'], name="ple_proj", tn=256, b_sh=True)
    z, x3, x3b = _mm(x2b, W['gate'], name="ple_gate", out_dtypes=(F32, F32, BF16),
                     extras=[(x2, 'mn'), (pp, 'mn')],
                     epilogue=lambda acc, xv, pv: (acc,) + (xv + jax.nn.sigmoid(acc) * pv,) * 2)
    sv.update(h=h, x1=x1, x1b=x1b, ab=ab, m=m, x2b=x2b, pp=pp, z=z, g0=g0, g1=g1)
    return x3, x3b, sv


def _layer_bwd(i, dx, p_i, W, sm, sv, cosp, sinp, token):
    kind, j = MIXER[i % 3], i // 3
    big, small = {}, {}
    dpp_b, dz_b = _ple_bwd_gate(dx, sv['z'], sv['pp'], token)
    big['ple_w_proj'] = _mm(p_i, dpp_b, ta=True, name="ple_proj_dw", tn=256, out_sh=True, out_dtypes=(BF16,))
    big['ple_w_gate'] = _row_shards(_mm(sv['x2b'], dz_b, ta=True, name="dw_dd", out_dtypes=(BF16,)))
    dx2 = _mm(dz_b, W['gate'], tb=True, name="dx_dd_add", tn=1024, extras=[(dx, 'mn')],
              epilogue=lambda acc, r: (acc + r,))
    dv1, dv1b, dg1, db1 = _ln_bwd(sv['x1'], sv['m'], sv['g1'], dx2, "ln_bwd")
    big['mlp_w2'] = _row_shards(_mm(sv['ab'], dv1b, ta=True, name="mlp_down_dw", out_dtypes=(BF16,)))
    dub = _mm(dv1b, W['w2'], tb=True, name="mlp_down_dx", out_dtypes=(BF16,), tm=FULL_ROWS,
              extras=[(sv['ab'], 'mn')], epilogue=lambda acc, a: (acc * (2.0 * jnp.sqrt(a.astype(F32))),))
    big['mlp_w1'] = _mm(sv['x1b'], dub, ta=True, name="mlp_up_dw", out_sh=True, out_dtypes=(BF16,))
    dx1 = _mm(dub, W['w1'], tb=True, name="mlp_up_dx", b_sh=True, tn=1024, extras=[(dv1, 'mn')],
              epilogue=lambda acc, r: (acc + ALPHA * r,))
    dv0, dv0b, dg0, db0 = _ln_bwd(sv['xin'], sv['h'], sv['g0'], dx1, "ln_bwd")
    small['ln_g'] = jnp.stack([dg0[0], dg1[0]])
    small['ln_b'] = jnp.stack([db0[0], db1[0]])
    resid = dict(tn=1024, extras=[(dv0, 'mn')], epilogue=lambda acc, r: (acc + ALPHA * r,))
    if kind == 'gla':
        big['gla_w_out'] = _row_shards(_mm(sv['yb'], dv0b, ta=True, name="dw_dd", out_dtypes=(BF16,)))
        dy = _mm(dv0b, W['w_out'], tb=True, name="dx_dd", tn=1024)
        do, dr_b, dng = _gla_post_bwd(dy, sv['o'], sv['pin'], _row(sm['gla_norm_g'][j]))
        dq_b, dk_b, dvv_b, dla = _gla_bwd(sv['pin'], sv['la'], sv['states'], do)
        dzg_b, dbg = _gla_gate_bwd(dla, sv['la'])
        dw_up = _mm(sv['pin'], dzg_b, ta=True, name="gla_gate_dw", M=128, tm=128,
                    a_off=(0, (GLA_IN_PAD - 128) // 128))
        dglr_b = _mm(dzg_b, sv['w_up'], tb=True, name="gla_gate_dx", out_dtypes=(BF16,))
        dpin_b = jnp.concatenate([dq_b, dk_b, dvv_b, dr_b, dglr_b], axis=1)
        dw_in = _mm(sv['xin_b'], dpin_b, ta=True, name="gla_in_dw", tn=640, out_dtypes=(BF16,))
        dxin = _mm(dpin_b, W['w_in'], tb=True, name="gla_in_dx", tk=640, **resid)
        big['gla_w_in'] = jnp.stack([dw_in[:, GLA_WIN_STEP * qq:GLA_WIN_STEP * qq + GLA_WIN]
                                     for qq in range(N_CHIPS)])
        small.update(gla_w_gate_up=dw_up[:GLA_RANK], gla_b_gate=dbg[0], gla_norm_g=dng[0])
    elif kind == 'mla':
        big['mla_w_out'] = _row_shards(_mm(sv['ob'], dv0b, ta=True, name="dw_dd", out_dtypes=(BF16,)))
        do = _mm(dv0b, W['w_out'], tb=True, name="dx_dd", tn=1024)
        dqr, dkn_b, dvv_b, dkr = _mla_attn_bwd(sv['qr'], sv['knv'], sv['kr'], sv['o'], do)
        dq_b = _mla_qrope_bwd(dqr, cosp, sinp)
        dw_uq = _mm(sv['qn'], dq_b, ta=True, name="mla_up_dw", out_dtypes=(BF16,))
        dqn = _mm(dq_b, W['w_uq'], tb=True, name="mla_up_dx")
        dknv_b = jnp.concatenate([dkn_b, dvv_b], axis=1)
        dw_ukv = _mm(sv['kvn'], dknv_b, ta=True, name="mla_up_dw", out_dtypes=(BF16,))
        dkvn = _mm(dknv_b, W['w_ukv'], tb=True, name="mla_up_dx")
        dcin_b, dgq, dgkv = _mla_norm_bwd(sv['cin'], dqn, dkvn, dkr, sv['gq'], sv['gkv'], cosp, sinp)
        big['mla_w_in'] = _row_shards(_mm(sv['xin_b'], dcin_b, ta=True, name="mla_in_dw", tn=640,
                                          out_dtypes=(BF16,)))
        dxin = _mm(dcin_b, W['w_in'], tb=True, name="mla_in_dx", tk=640, **resid)
        big['mla_w_uq'] = _col_shards(
            dw_uq.reshape(MLA_QR, MLA_HEADS, MLA_QH)[:, :, :MLA_NOPE + MLA_ROPE].reshape(MLA_QR, -1))
        big['mla_w_ukv'] = _col_shards(
            dw_ukv.reshape(MLA_KVR, 2, MLA_HEADS, 128).transpose(0, 2, 1, 3).reshape(MLA_KVR, -1))
        small.update(mla_q_norm=dgq[0], mla_kv_norm=dgkv[0])
    else:
        big['conv_w_out'] = _row_shards(_mm(sv['yb'], dv0b, ta=True, name="dw_dd", out_dtypes=(BF16,)))
        dy = _mm(dv0b, W['w_out'], tb=True, name="dx_dd", tn=1024)
        db_b, dc_b, du_b, dw8 = _conv_bwd(sv['bcu'], sv['w8'], dy)
        dbcu_b = jnp.concatenate([db_b, dc_b, du_b], axis=1)
        big['conv_w_in'] = _mm(sv['xin_b'], dbcu_b, ta=True, name="conv_in_dw", tn=768, out_sh=True,
                               out_dtypes=(BF16,))
        dxin = _mm(dbcu_b, W['w_in'], tb=True, name="conv_in_dx", tk=768, b_sh=True, **resid)
        small['conv_w'] = dw8[:3]
    return dxin, big, small


def _rope_tables(positions):
    inv_freq = ROPE_BASE ** (-jnp.arange(0, MLA_ROPE // 2, dtype=F32) * (2.0 / MLA_ROPE))
    ang = positions.astype(F32)[:, None] * inv_freq
    zeros = jnp.zeros((positions.shape[0], 64), F32)
    return (jnp.concatenate([jnp.cos(ang), jnp.cos(ang), zeros], axis=1),
            jnp.concatenate([jnp.sin(ang), jnp.sin(ang), zeros], axis=1))


def _start_gathers(w, q):
    token, started = jnp.zeros(TOKEN, F32), []
    for i in range(DEPTH):
        sh = _layer_shards(w, i, q)
        names = list(sh)
        ops = [sh[n] for n in names]
        if i == 0:
            ops.append(_pack_small_shards(w))
        handle = _gather_start(ops, token, "ag_start_l%d" % i)
        token = handle[4]
        started.append((handle, names))
    return started, token


def _finish_gather(started, i, after):
    handle, names = started[i]
    srcs, lands = _gather_wait(handle, after, "ag_wait_l%d" % i)
    got = _gather_finish(srcs, lands, "ag_finish_l%d" % i)
    return dict(zip(names, got)), got[-1]


def _local_shard_grad(name, g, q):
    if name == 'gla_w_in':
        return lax.dynamic_slice_in_dim(g, (GLA_SHARD - GLA_WIN_STEP) * q, GLA_SHARD, axis=1)
    if name == 'mla_w_in':
        return g[:, :MLA_IN]
    return g


def kernel(x, p, positions, gla_w_in, gla_w_gate_up, gla_b_gate, gla_norm_g, gla_w_out, mla_w_in, mla_q_norm, mla_kv_norm, mla_w_uq, mla_w_ukv, mla_w_out, conv_w_in, conv_w, conv_w_out, ln_g, ln_b, mlp_w1, mlp_w2, ple_w_gate, ple_w_proj, loss_target, m_gla_w_in, m_gla_w_gate_up, m_gla_b_gate, m_gla_norm_g, m_gla_w_out, m_mla_w_in, m_mla_q_norm, m_mla_kv_norm, m_mla_w_uq, m_mla_w_ukv, m_mla_w_out, m_conv_w_in, m_conv_w, m_conv_w_out, m_ln_g, m_ln_b, m_mlp_w1, m_mlp_w2, m_ple_w_gate, m_ple_w_proj, v_gla_w_in, v_gla_w_gate_up, v_gla_b_gate, v_gla_norm_g, v_gla_w_out, v_mla_w_in, v_mla_q_norm, v_mla_kv_norm, v_mla_w_uq, v_mla_w_ukv, v_mla_w_out, v_conv_w_in, v_conv_w, v_conv_w_out, v_ln_g, v_ln_b, v_mlp_w1, v_mlp_w2, v_ple_w_gate, v_ple_w_proj):
    args = locals()
    w = {n: args[n] for n in WNAMES}
    m = {n: args['m_' + n] for n in WNAMES}
    v = {n: args['v_' + n] for n in WNAMES}
    q = 2 * lax.axis_index("x") + lax.axis_index("y")
    cq = jnp.stack([lax.axis_index("c"), q]).astype(jnp.int32)

    cosp, sinp = _rope_tables(positions[0])
    started, after = _start_gathers(w, q)
    xin, saved, layers, sm = x[0], [], [], None
    xin_b = xin.astype(BF16)
    for i in range(DEPTH):
        got, last = _finish_gather(started, i, after)
        if i == 0:
            sm = _unpack_small_gathered(last)
            sm['mla_q_norm'], sm['mla_kv_norm'] = w['mla_q_norm'], w['mla_kv_norm']
        layers.append(_layer_weights(got, i))
        xin, xin_b, sv = _layer_fwd(i, xin, xin_b, p[i, 0], layers[i], sm, cosp, sinp)
        saved.append(sv)
        after = xin
    dx, loss_cols = _loss_head(xin, loss_target[0])
    loss = lax.psum(jnp.sum(loss_cols[0]), ("x", "y", "c"))

    gbig = {n: [None] * WSPEC[n][0][0] for n in BIG}
    gsmall = {n: [None] * _full_shape(n)[0] for n in SMALL}

    def finish(pending, after):
        handle, names, i = pending
        for n, g in zip(names, _reduce_scatter_finish(handle, cq, after, "l%d" % i)):
            gbig[n][i if n in COMMON_BIG else i // 3] = _local_shard_grad(n, g, q)

    pending, token = None, jnp.zeros(TOKEN, F32)
    for i in reversed(range(DEPTH)):
        dx, big, small = _layer_bwd(i, dx, p[i, 0], layers[i], sm, saved[i], cosp, sinp, token)
        names = list(big)
        handle = _reduce_scatter_start([big[n] for n in names], cq, jnp.zeros(TOKEN, F32), "l%d" % i)
        if pending is not None:
            finish(pending, dx)
        pending, token = (handle, names, i), handle[4]
        for n, g in small.items():
            gsmall[n][i if n in ('ln_g', 'ln_b') else i // 3] = g
    finish(pending, token)
    gsm = _unpack_small(_all_reduce_small(_pack_small({n: jnp.stack(g) for n, g in gsmall.items()})), q)

    grad, delta, new_m, new_v = {}, {}, {}, {}
    for n in BIG:
        grad[n], delta[n], new_m[n], new_v[n] = _adamw(w[n], m[n], v[n], gbig[n], "adamw_" + n)
    total = sum(_size(WSPEC[n][0]) for n in SMALL)
    rows = -(-total // 128 // 8) * 8

    def pack(dct):
        flat = jnp.concatenate([dct[n].reshape(-1) for n in SMALL])
        return jnp.pad(flat, (0, rows * 128 - total), constant_values=1.0).reshape(1, rows, 128)

    res = _adamw(pack(w), pack(m), pack(v), [pack(gsm)[0]], "adamw_small")
    for out, packed in zip((grad, delta, new_m, new_v), res):
        flat, off = packed.reshape(-1), 0
        for n in SMALL:
            sz = _size(WSPEC[n][0])
            out[n] = flat[off:off + sz].reshape(WSPEC[n][0])
            off += sz
    return (loss, dx[None], *[grad[n] for n in WNAMES], *[delta[n] for n in WNAMES],
            *[new_m[n] for n in WNAMES], *[new_v[n] for n in WNAMES])
```

```python
import functools

import numpy as np
import jax
import jax.numpy as jnp
from jax import lax
from jax.experimental import pallas as pl
from jax.experimental.pallas import tpu as pltpu

F32 = jnp.float32
BF16 = jnp.bfloat16
MESH = pl.DeviceIdType.MESH

D_MODEL = 1024
DEPTH = 4
CHUNK = 64
ALPHA = (2 * DEPTH) ** 0.25
LN_EPS = 1e-5
RMS_EPS = 1e-6
PLE_DIM = 256
D_FF = 4 * D_MODEL
GLA_HEADS = 4
GLA_DK = 128
GLA_DV = 256
GLA_RANK = 16
GLA_TAU = 16.0
GLA_HK = GLA_HEADS * GLA_DK
GLA_HV = GLA_HEADS * GLA_DV
GLA_IN = 2 * GLA_HK + GLA_HV + D_MODEL + GLA_RANK
GLA_IN_PAD = 2 * GLA_HK + GLA_HV + D_MODEL + 128
GLA_SHARD = GLA_IN // 4
GLA_WIN = 896
GLA_WIN_STEP = 768
MLA_HEADS = 8
MLA_NOPE = 128
MLA_ROPE = 64
MLA_V = 128
MLA_QR = 256
MLA_KVR = 256
MLA_IN = MLA_QR + MLA_KVR + MLA_ROPE
MLA_IN_PAD = MLA_QR + MLA_KVR + 128
MLA_QH = 256
ROPE_BASE = 10000.0
ADAM_LR = 0.001
ADAM_B1 = 0.9
ADAM_B2 = 0.999
ADAM_EPS = 1e-08
ADAM_WD = 0.01
ADAM_STEP = 10

VMEM_LIMIT = 48 * 1024 * 1024
FULL_ROWS = 2048
N_CHIPS = 4

WSPEC = {
    'gla_w_in': ((2, 1024, 772), 2), 'gla_w_gate_up': ((2, 16, 128), 2), 'gla_b_gate': ((2, 128), 1),
    'gla_norm_g': ((2, 64), 1), 'gla_w_out': ((2, 256, 1024), 1), 'mla_w_in': ((1, 256, 576), 1),
    'mla_q_norm': ((1, 256), None), 'mla_kv_norm': ((1, 256), None), 'mla_w_uq': ((1, 256, 384), 2),
    'mla_w_ukv': ((1, 256, 512), 2), 'mla_w_out': ((1, 256, 1024), 1), 'conv_w_in': ((1, 1024, 768), 2),
    'conv_w': ((1, 3, 256), 2), 'conv_w_out': ((1, 256, 1024), 1), 'ln_g': ((4, 2, 256), 2),
    'ln_b': ((4, 2, 256), 2), 'mlp_w1': ((4, 1024, 1024), 2), 'mlp_w2': ((4, 1024, 1024), 1),
    'ple_w_gate': ((4, 256, 1024), 1), 'ple_w_proj': ((4, 256, 256), 2),
}
WNAMES = list(WSPEC)
BIG = ['gla_w_in', 'gla_w_out', 'mla_w_in', 'mla_w_uq', 'mla_w_ukv', 'mla_w_out', 'conv_w_in', 'conv_w_out',
       'mlp_w1', 'mlp_w2', 'ple_w_gate', 'ple_w_proj']
SMALL_SHARDED = ['gla_w_gate_up', 'gla_b_gate', 'gla_norm_g', 'conv_w', 'ln_g', 'ln_b']
SMALL = SMALL_SHARDED + ['mla_q_norm', 'mla_kv_norm']
MIXER = ['gla', 'mla', 'conv']
LAYER_BIG = {'gla': ['gla_w_in', 'gla_w_out'], 'mla': ['mla_w_in', 'mla_w_uq', 'mla_w_ukv', 'mla_w_out'],
             'conv': ['conv_w_in', 'conv_w_out']}
COMMON_BIG = ['mlp_w1', 'mlp_w2', 'ple_w_gate', 'ple_w_proj']


def _size(shape):
    return int(np.prod(shape))


def _full_shape(name):
    shape, ax = WSPEC[name]
    if ax is None:
        return shape
    return tuple(s * N_CHIPS if i == ax else s for i, s in enumerate(shape))


def _cparams(sem=None):
    return pltpu.CompilerParams(dimension_semantics=sem, vmem_limit_bytes=VMEM_LIMIT)


def _hbm(v):
    return pltpu.with_memory_space_constraint(v, pltpu.HBM)


def _mm(a, b, *, name, ta=False, tb=False, M=None, N=None, K=None, out_dtypes=(F32,), epilogue=None, extras=(),
        tm=1024, tn=512, tk=None, a_off=(0, 0), b_sh=False, out_sh=False):
    if M is None:
        M = a.shape[1] if ta else a.shape[0]
    if K is None:
        K = a.shape[0] if ta else a.shape[1]
    if b_sh:
        kw, nq = b.shape[1], b.shape[2]
        n_b, k_b = (kw, N_CHIPS * nq) if tb else (N_CHIPS * nq, kw)
        N = n_b if N is None else N
        assert K == k_b
    elif N is None:
        N = b.shape[0] if tb else b.shape[1]
    if tk is None:
        tk = FULL_ROWS if ta else 1024
    tm, tn, tk = min(tm, M), min(tn, N), min(tk, K)
    assert M % tm == 0 and N % tn == 0 and K % tk == 0, (name, M, N, K, tm, tn, tk)
    nk = K // tk
    n_ex, n_out = len(extras), len(out_dtypes)

    def body(a_ref, b_ref, *rest):
        ex_refs, out_refs = rest[:n_ex], rest[n_ex:n_ex + n_out]
        part = lax.dot_general(a_ref[...].astype(BF16), b_ref[...].astype(BF16),
                               ((((0,) if ta else (1,)), ((1,) if tb else (0,))), ((), ())),
                               preferred_element_type=F32)

        def finish(acc):
            res = (acc,) if epilogue is None else epilogue(acc, *[r[...] for r in ex_refs])
            for r, v in zip(out_refs, res):
                r[...] = v.astype(r.dtype)

        if nk == 1:
            finish(part)
        else:
            acc_ref = rest[-1]
            k = pl.program_id(2)

            @pl.when(k == 0)
            def _():
                acc_ref[...] = part

            @pl.when(k > 0)
            def _():
                acc_ref[...] += part

            @pl.when(k == nk - 1)
            def _():
                finish(acc_ref[...])

    if ta:
        a_spec = pl.BlockSpec((tk, tm), lambda i, j, k: (k + a_off[0], i + a_off[1]))
    else:
        a_spec = pl.BlockSpec((tm, tk), lambda i, j, k: (i + a_off[0], k + a_off[1]))
    if b_sh and tb:
        assert nq % tk == 0
        per = nq // tk
        b_spec = pl.BlockSpec((None, tn, tk), lambda i, j, k: (k // per, j, k % per))
    elif b_sh:
        assert nq % tn == 0
        per = nq // tn
        b_spec = pl.BlockSpec((None, tk, tn), lambda i, j, k: (j // per, k, j % per))
    elif tb:
        b_spec = pl.BlockSpec((tn, tk), lambda i, j, k: (j, k))
    else:
        b_spec = pl.BlockSpec((tk, tn), lambda i, j, k: (k, j))
    ex_specs = []
    for arr, kind in extras:
        if kind == 'mn':
            ex_specs.append(pl.BlockSpec((tm, tn), lambda i, j, k: (i, j)))
        else:
            ex_specs.append(pl.BlockSpec((1, tn), lambda i, j, k: (0, j)))
    if out_sh:
        assert (N // N_CHIPS) % tn == 0
        per_o = N // N_CHIPS // tn
        o_spec = pl.BlockSpec((None, tm, tn), lambda i, j, k: (j // per_o, i, j % per_o))
        o_shape = (N_CHIPS, M, N // N_CHIPS)
    else:
        o_spec = pl.BlockSpec((tm, tn), lambda i, j, k: (i, j))
        o_shape = (M, N)
    outs = pl.pallas_call(
        body, name=name, grid=(M // tm, N // tn, nk),
        in_specs=[a_spec, b_spec] + ex_specs,
        out_specs=[o_spec for _ in out_dtypes],
        out_shape=[jax.ShapeDtypeStruct(o_shape, d) for d in out_dtypes],
        scratch_shapes=[pltpu.VMEM((tm, tn), F32)] if nk > 1 else [],
        compiler_params=_cparams(("parallel", "parallel", "arbitrary")),
    )(_hbm(a), _hbm(b), *[_hbm(e[0]) for e in extras])
    return outs[0] if n_out == 1 else tuple(outs)


def _rowwise(fn, *, name, rows, pars=(), outs=(), accs=(), tm=256):
    S = rows[0][0].shape[0]
    tm = min(tm, S)
    assert S % tm == 0
    n_r, n_p, n_o, n_a = len(rows), len(pars), len(outs), len(accs)

    def body(*refs):
        r_refs, p_refs = refs[:n_r], refs[n_r:n_r + n_p]
        o_refs, a_refs = refs[n_r + n_p:n_r + n_p + n_o], refs[n_r + n_p + n_o:]
        o_vals, a_vals = fn([r[...] for r in r_refs], [p[...] for p in p_refs])
        for r, v in zip(o_refs, o_vals):
            r[...] = v.astype(r.dtype)
        if n_a:
            i = pl.program_id(0)

            @pl.when(i == 0)
            def _():
                for r in a_refs:
                    r[...] = jnp.zeros(r.shape, r.dtype)

            for r, v in zip(a_refs, a_vals):
                r[...] += jnp.broadcast_to(v, r.shape)

    in_specs = [pl.BlockSpec((tm, w), functools.partial(lambda i, o: (i, o), o=off)) for _, w, off in rows]
    in_specs += [pl.BlockSpec(p.shape, functools.partial(lambda i, nd: (0,) * nd, nd=p.ndim)) for p in pars]
    out_specs = [pl.BlockSpec((tm, w), lambda i: (i, 0)) for w, _ in outs]
    out_specs += [pl.BlockSpec((8, w), lambda i: (0, 0)) for w in accs]
    out_shape = [jax.ShapeDtypeStruct((S, w), d) for w, d in outs]
    out_shape += [jax.ShapeDtypeStruct((8, w), F32) for w in accs]
    res = pl.pallas_call(
        body, name=name, grid=(S // tm,), in_specs=in_specs, out_specs=out_specs, out_shape=out_shape,
        compiler_params=_cparams(("arbitrary",)),
    )(*[_hbm(r[0]) for r in rows], *[_hbm(p) for p in pars])
    return tuple(res)


def _colsum(v):
    return jnp.sum(v, axis=0, keepdims=True)


def _ln_stats(v):
    mu = jnp.mean(v, axis=-1, keepdims=True)
    d = v - mu
    var = jnp.mean(d * d, axis=-1, keepdims=True)
    rstd = lax.rsqrt(var + LN_EPS)
    return d * rstd, rstd


def _ln_fwd(x, h, g, b, name):
    def fn(r, p):
        xhat, _ = _ln_stats(ALPHA * r[0] + r[1])
        y = xhat * p[0] + p[1]
        return [y, y], []
    return _rowwise(fn, name=name, rows=[(x, D_MODEL, 0), (h, D_MODEL, 0)], pars=[g, b],
                    outs=[(D_MODEL, F32), (D_MODEL, BF16)])


def _ln_bwd(x, h, g, dy, name, token=None):
    def fn(r, p):
        xhat, rstd = _ln_stats(ALPHA * r[0] + r[1])
        dyv = r[2]
        dxh = dyv * p[0]
        m1 = jnp.mean(dxh, axis=-1, keepdims=True)
        m2 = jnp.mean(dxh * xhat, axis=-1, keepdims=True)
        dv = rstd * (dxh - m1 - xhat * m2)
        return [dv, dv], [_colsum(dyv * xhat), _colsum(dyv)]
    return _rowwise(fn, name=name, rows=[(x, D_MODEL, 0), (h, D_MODEL, 0), (dy, D_MODEL, 0)],
                    pars=[g] if token is None else [g, token],
                    outs=[(D_MODEL, F32), (D_MODEL, BF16)], accs=[D_MODEL, D_MODEL])


def _loss_head(y, t):
    def fn(r, p):
        d = r[0] - r[1]
        return [d * (1.0 / D_MODEL)], [_colsum(d * d) * (0.5 / D_MODEL)]
    return _rowwise(fn, name="loss_head", rows=[(y, D_MODEL, 0), (t, D_MODEL, 0)], outs=[(D_MODEL, F32)],
                    accs=[D_MODEL])


def _ple_bwd_gate(dx3, z, pp, token):
    def fn(r, p):
        s = jax.nn.sigmoid(r[1])
        return [r[0] * s, r[0] * r[2] * s * (1.0 - s)], []
    return _rowwise(fn, name="ple_bwd_gate", rows=[(dx3, D_MODEL, 0), (z, D_MODEL, 0), (pp, D_MODEL, 0)],
                    pars=[token], outs=[(D_MODEL, BF16), (D_MODEL, BF16)])


N_LEVELS = 6


def _gla_consts():
    C = CHUNK
    A = np.zeros((N_LEVELS + 3, C, C), np.float32)
    masks = np.zeros((N_LEVELS + 1, C, C), np.float32)
    r = np.arange(C)[:, None]
    u = np.arange(C)[None, :]
    for l in range(N_LEVELS):
        half = C >> (l + 1)
        mid = (r // (2 * half)) * (2 * half) + half - 1
        A[l] = np.where(r > mid, (u > mid) & (u <= r), (u > r) & (u <= mid))
        masks[l] = ((r // (2 * half)) == (u // (2 * half))) & (((r // half) % 2) != ((u // half) % 2))
    masks[N_LEVELS] = (r == u)
    A[N_LEVELS] = (u <= r)
    A[N_LEVELS + 1] = (u > r)
    A[N_LEVELS + 2] = 1.0
    A = A.reshape(-1, C)
    return A, np.ascontiguousarray(A.T), masks


def _split3(v):
    hi = v.astype(BF16)
    r1 = v - hi.astype(F32)
    mid = r1.astype(BF16)
    lo = (r1 - mid.astype(F32)).astype(BF16)
    return hi, mid, lo


def _dot_exact01(a01, v):
    hi, mid, lo = _split3(v)
    f = lambda p: jnp.dot(a01, p, preferred_element_type=F32)
    return f(hi) + f(mid) + f(lo)


def _nt(a, b):
    return lax.dot_general(a, b, (((1,), (1,)), ((), ())), preferred_element_type=F32)


def _tn(a, b):
    return lax.dot_general(a, b, (((0,), (0,)), ((), ())), preferred_element_type=F32)


def _nn(a, b):
    return jnp.dot(a, b, preferred_element_type=F32)


def _gla_chunk_terms(q, k, E, m_ref):
    C = CHUNK
    scores = m_ref[N_LEVELS] * _nt(q.astype(BF16), k.astype(BF16))
    qes, kes = [], []
    for l in range(N_LEVELS):
        El = E[l * C:(l + 1) * C]
        qe, ke = (q * El).astype(BF16), (k * El).astype(BF16)
        qes.append(qe)
        kes.append(ke)
        scores = scores + m_ref[l] * _nt(qe, ke)
    return qes, kes, scores


def _head(v, h, w):
    return v[:, h * w:(h + 1) * w]


def _gla_fwd(pin, la):
    S = pin.shape[0]
    NC = S // CHUNK
    C = CHUNK
    A, _, masks = _gla_consts()

    def body(q_ref, k_ref, v_ref, la_ref, a_ref, m_ref, o_ref, st_ref, state):
        @pl.when(pl.program_id(0) == 0)
        def _():
            state[...] = jnp.zeros(state.shape, F32)

        E_all = jnp.exp(_dot_exact01(a_ref[...], la_ref[...]))
        q_all = q_ref[...] * (GLA_DK ** -0.5)
        k_all, v_all = k_ref[...], v_ref[...]
        outs = []
        for h in range(GLA_HEADS):
            q, k, E = _head(q_all, h, GLA_DK), _head(k_all, h, GLA_DK), _head(E_all, h, GLA_DK)
            _, _, scores = _gla_chunk_terms(q, k, E, m_ref)
            Eq, Ek, Ee = E[6 * C:7 * C], E[7 * C:8 * C], E[8 * C:9 * C]
            st = state[h]
            st_ref[h] = st
            vb = _head(v_all, h, GLA_DV).astype(BF16)
            outs.append(_nn(scores.astype(BF16), vb) + _nt((q * Eq).astype(BF16), st.astype(BF16)))
            state[h] = st * jnp.concatenate([Ee] * (GLA_DV // C), axis=0) + _tn(vb, (k * Ek).astype(BF16))
        o_ref[...] = jnp.concatenate(outs, axis=1)

    return pl.pallas_call(
        body, name="gla_fwd", grid=(NC,),
        in_specs=[pl.BlockSpec((C, GLA_HK), lambda c: (c, 0)),
                  pl.BlockSpec((C, GLA_HK), lambda c: (c, 1)),
                  pl.BlockSpec((C, GLA_HV), lambda c: (c, 2 * GLA_HK // GLA_HV)),
                  pl.BlockSpec((C, GLA_HK), lambda c: (c, 0)),
                  pl.BlockSpec(A.shape, lambda c: (0, 0)),
                  pl.BlockSpec(masks.shape, lambda c: (0, 0, 0))],
        out_specs=[pl.BlockSpec((C, GLA_HV), lambda c: (c, 0)),
                   pl.BlockSpec((GLA_HEADS, None, GLA_DV, GLA_DK), lambda c: (0, c, 0, 0))],
        out_shape=[jax.ShapeDtypeStruct((S, GLA_HV), F32),
                   jax.ShapeDtypeStruct((GLA_HEADS, NC, GLA_DV, GLA_DK), F32)],
        scratch_shapes=[pltpu.VMEM((GLA_HEADS, GLA_DV, GLA_DK), F32)],
        compiler_params=_cparams(("arbitrary",)),
    )(*map(_hbm, (pin, pin, pin, la)), jnp.asarray(A, BF16), jnp.asarray(masks))


def _gla_bwd(pin, la, states, do):
    S = pin.shape[0]
    NC = S // CHUNK
    C = CHUNK
    A, AT, masks = _gla_consts()
    scale = GLA_DK ** -0.5

    def body(q_ref, k_ref, v_ref, la_ref, st_ref, do_ref, a_ref, at_ref, m_ref,
             dq_ref, dk_ref, dv_ref, dla_ref, dstate):
        @pl.when(pl.program_id(0) == 0)
        def _():
            dstate[...] = jnp.zeros(dstate.shape, F32)

        E_all = jnp.exp(_dot_exact01(a_ref[...], la_ref[...]))
        q_all = q_ref[...] * scale
        k_all, v_all, do_all = k_ref[...], v_ref[...], do_ref[...]
        dqs, dks, dvs, dXs = [], [], [], []
        for h in range(GLA_HEADS):
            q, k, E = _head(q_all, h, GLA_DK), _head(k_all, h, GLA_DK), _head(E_all, h, GLA_DK)
            qes, kes, scores = _gla_chunk_terms(q, k, E, m_ref)
            Eq, Ek, Ee = E[6 * C:7 * C], E[7 * C:8 * C], E[8 * C:9 * C]
            st, dst = st_ref[h], dstate[h]
            dob, vb = _head(do_all, h, GLA_DV).astype(BF16), _head(v_all, h, GLA_DV).astype(BF16)
            dstb = dst.astype(BF16)
            qEq, kEk = (q * Eq).astype(BF16), (k * Ek).astype(BF16)
            dsc = _nt(dob, vb)
            dvs.append(_tn(scores.astype(BF16), dob) + _nt(kEk, dstb))
            dqEq = _nn(dob, st.astype(BF16))
            dkEk = _nn(vb, dstb)
            Gd = (m_ref[N_LEVELS] * dsc).astype(BF16)
            dq = _nn(Gd, k.astype(BF16)) + dqEq * Eq
            dk = _tn(Gd, q.astype(BF16)) + dkEk * Ek
            dX = []
            for l in range(N_LEVELS):
                El = E[l * C:(l + 1) * C]
                G = (m_ref[l] * dsc).astype(BF16)
                dqe, dke = _nn(G, kes[l]), _tn(G, qes[l])
                dq = dq + dqe * El
                dk = dk + dke * El
                dX.append((dqe * q + dke * k) * El)
            dX.append(dqEq * q * Eq)
            dX.append(dkEk * k * Ek)
            prod = dst * st
            dEe = prod[0:C]
            for i in range(1, GLA_DV // C):
                dEe = dEe + prod[i * C:(i + 1) * C]
            dX.append(dEe * Ee)
            dXs.append(jnp.concatenate(dX, axis=0))
            dqs.append(dq * scale)
            dks.append(dk)
            dstate[h] = dst * jnp.concatenate([Ee] * (GLA_DV // C), axis=0) + _tn(dob, qEq)
        dla_ref[...] = _dot_exact01(at_ref[...], jnp.concatenate(dXs, axis=1))
        dq_ref[...] = jnp.concatenate(dqs, axis=1).astype(dq_ref.dtype)
        dk_ref[...] = jnp.concatenate(dks, axis=1).astype(dk_ref.dtype)
        dv_ref[...] = jnp.concatenate(dvs, axis=1).astype(dv_ref.dtype)

    rc = lambda c: NC - 1 - c
    return pl.pallas_call(
        body, name="gla_bwd", grid=(NC,),
        in_specs=[pl.BlockSpec((C, GLA_HK), lambda c: (rc(c), 0)),
                  pl.BlockSpec((C, GLA_HK), lambda c: (rc(c), 1)),
                  pl.BlockSpec((C, GLA_HV), lambda c: (rc(c), 2 * GLA_HK // GLA_HV)),
                  pl.BlockSpec((C, GLA_HK), lambda c: (rc(c), 0)),
                  pl.BlockSpec((GLA_HEADS, None, GLA_DV, GLA_DK), lambda c: (0, rc(c), 0, 0)),
                  pl.BlockSpec((C, GLA_HV), lambda c: (rc(c), 0)),
                  pl.BlockSpec(A.shape, lambda c: (0, 0)),
                  pl.BlockSpec(AT.shape, lambda c: (0, 0)),
                  pl.BlockSpec(masks.shape, lambda c: (0, 0, 0))],
        out_specs=[pl.BlockSpec((C, GLA_HK), lambda c: (rc(c), 0)),
                   pl.BlockSpec((C, GLA_HK), lambda c: (rc(c), 0)),
                   pl.BlockSpec((C, GLA_HV), lambda c: (rc(c), 0)),
                   pl.BlockSpec((C, GLA_HK), lambda c: (rc(c), 0))],
        out_shape=[jax.ShapeDtypeStruct((S, GLA_HK), BF16), jax.ShapeDtypeStruct((S, GLA_HK), BF16),
                   jax.ShapeDtypeStruct((S, GLA_HV), BF16), jax.ShapeDtypeStruct((S, GLA_HK), F32)],
        scratch_shapes=[pltpu.VMEM((GLA_HEADS, GLA_DV, GLA_DK), F32)],
        compiler_params=_cparams(("arbitrary",)),
    )(*map(_hbm, (pin, pin, pin, la, states, do)), jnp.asarray(A, BF16), jnp.asarray(AT, BF16), jnp.asarray(masks))


def _gla_post_fwd(o, pin, g):
    def fn(r, p):
        ov, rv = r
        ys = []
        for h in range(GLA_HEADS):
            oh = ov[:, h * GLA_DV:(h + 1) * GLA_DV]
            rh = rv[:, h * GLA_DV:(h + 1) * GLA_DV]
            rs = lax.rsqrt(jnp.mean(oh * oh, axis=-1, keepdims=True) + RMS_EPS)
            ys.append(oh * rs * p[0] * (rh * jax.nn.sigmoid(rh)))
        return [jnp.concatenate(ys, axis=1)], []
    return _rowwise(fn, name="gla_post_fwd", rows=[(o, GLA_HV, 0), (pin, GLA_HV, (2 * GLA_HK + GLA_HV) // GLA_HV)],
                    pars=[g], outs=[(GLA_HV, BF16)])[0]


def _gla_post_bwd(dy, o, pin, g):
    def fn(r, p):
        dyv, ov, rv = r
        dos, drs, dg = [], [], 0.0
        for h in range(GLA_HEADS):
            sl = slice(h * GLA_DV, (h + 1) * GLA_DV)
            oh, rh, dyh = ov[:, sl], rv[:, sl], dyv[:, sl]
            rs = lax.rsqrt(jnp.mean(oh * oh, axis=-1, keepdims=True) + RMS_EPS)
            xh = oh * rs
            sg = jax.nn.sigmoid(rh)
            d_on = dyh * (rh * sg)
            drs.append(dyh * (xh * p[0]) * (sg * (1.0 + rh * (1.0 - sg))))
            dg = dg + _colsum(d_on * xh)
            dxh = d_on * p[0]
            dos.append(rs * (dxh - xh * jnp.mean(dxh * xh, axis=-1, keepdims=True)))
        return [jnp.concatenate(dos, axis=1), jnp.concatenate(drs, axis=1)], [dg]
    return _rowwise(fn, name="gla_post_bwd",
                    rows=[(dy, GLA_HV, 0), (o, GLA_HV, 0), (pin, GLA_HV, (2 * GLA_HK + GLA_HV) // GLA_HV)],
                    pars=[g], outs=[(GLA_HV, F32), (GLA_HV, BF16)], accs=[GLA_DV])


def _gla_gate_bwd(dla, la):
    def fn(r, p):
        dz = r[0] * (1.0 / GLA_TAU) * (1.0 - jnp.exp(GLA_TAU * r[1]))
        return [dz], [_colsum(dz)]
    return _rowwise(fn, name="gla_gate_bwd", rows=[(dla, GLA_HK, 0), (la, GLA_HK, 0)], outs=[(GLA_HK, BF16)],
                    accs=[GLA_HK])


def _log_sigmoid(z):
    return jnp.minimum(z, 0.0) - jnp.log(1.0 + jnp.exp(-jnp.abs(z)))


def _rot_half(v):
    lane = lax.broadcasted_iota(jnp.int32, v.shape, 1)
    return jnp.where(lane < 32, -pltpu.roll(v, 96, 1), jnp.where(lane < 64, pltpu.roll(v, 32, 1), 0.0))


def _rms(v):
    rs = lax.rsqrt(jnp.mean(v * v, axis=-1, keepdims=True) + RMS_EPS)
    return v * rs, rs


def _mla_norm_fwd(cin, gq, gkv, cosp, sinp):
    def fn(r, p):
        cv, cs, sn = r
        qn, _ = _rms(cv[:, :MLA_QR])
        kvn, _ = _rms(cv[:, MLA_QR:MLA_QR + MLA_KVR])
        kr = cv[:, MLA_QR + MLA_KVR:]
        return [qn * p[0], kvn * p[1], kr * cs + _rot_half(kr) * sn], []
    return _rowwise(fn, name="mla_norm_fwd", rows=[(cin, MLA_IN_PAD, 0), (cosp, 128, 0), (sinp, 128, 0)],
                    pars=[gq, gkv], outs=[(MLA_QR, BF16), (MLA_KVR, BF16), (128, BF16)])


def _mla_qrope_fwd(q, cosp, sinp):
    scale = (MLA_NOPE + MLA_ROPE) ** -0.5

    def fn(r, p):
        qv, cs, sn = r
        parts = []
        for h in range(MLA_HEADS):
            parts.append(qv[:, h * MLA_QH:h * MLA_QH + 128] * scale)
            rp = qv[:, h * MLA_QH + 128:(h + 1) * MLA_QH]
            parts.append((rp * cs + _rot_half(rp) * sn) * scale)
        return [jnp.concatenate(parts, axis=1)], []
    W = MLA_HEADS * MLA_QH
    return _rowwise(fn, name="mla_qrope_fwd", rows=[(q, W, 0), (cosp, 128, 0), (sinp, 128, 0)],
                    outs=[(W, BF16)])[0]


def _mla_qrope_bwd(dq, cosp, sinp):
    scale = (MLA_NOPE + MLA_ROPE) ** -0.5

    def fn(r, p):
        dv, cs, sn = r
        parts = []
        for h in range(MLA_HEADS):
            parts.append(dv[:, h * MLA_QH:h * MLA_QH + 128] * scale)
            rp = dv[:, h * MLA_QH + 128:(h + 1) * MLA_QH]
            parts.append((rp * cs - _rot_half(rp) * sn) * scale)
        return [jnp.concatenate(parts, axis=1)], []
    W = MLA_HEADS * MLA_QH
    return _rowwise(fn, name="mla_qrope_bwd", rows=[(dq, W, 0), (cosp, 128, 0), (sinp, 128, 0)],
                    outs=[(W, BF16)])[0]


def _mla_norm_bwd(cin, dqn, dkvn, dkr, gq, gkv, cosp, sinp):
    def fn(r, p):
        cv, dq_, dkv_, dkr_, cs, sn = r
        outs, accs = [], []
        for (lo, hi), dn, g in (((0, MLA_QR), dq_, p[0]), ((MLA_QR, MLA_QR + MLA_KVR), dkv_, p[1])):
            xh, rs = _rms(cv[:, lo:hi])
            dxh = dn * g
            outs.append(rs * (dxh - xh * jnp.mean(dxh * xh, axis=-1, keepdims=True)))
            accs.append(_colsum(dn * xh))
        dk = dkr_[:, 0:128]
        for h in range(1, MLA_HEADS):
            dk = dk + dkr_[:, h * 128:(h + 1) * 128]
        outs.append(dk * cs - _rot_half(dk) * sn)
        return [jnp.concatenate(outs, axis=1)], accs
    return _rowwise(fn, name="mla_norm_bwd",
                    rows=[(cin, MLA_IN_PAD, 0), (dqn, MLA_QR, 0), (dkvn, MLA_KVR, 0), (dkr, MLA_HEADS * 128, 0),
                          (cosp, 128, 0), (sinp, 128, 0)],
                    pars=[gq, gkv], outs=[(MLA_IN_PAD, BF16)], accs=[MLA_QR, MLA_KVR])


def _mla_probs(q, kn, kr, i, tq):
    s = _nt(q[:, :128], kn) + _nt(q[:, 128:], kr)
    row = (i * tq + lax.broadcasted_iota(jnp.int32, s.shape, 0)) // CHUNK
    col = lax.broadcasted_iota(jnp.int32, s.shape, 1) // CHUNK
    s = jnp.where(col <= row, s, -jnp.inf)
    e = jnp.exp(s - jnp.max(s, axis=-1, keepdims=True))
    return e / jnp.sum(e, axis=-1, keepdims=True)


def _mla_attn_fwd(qr, knv, kr, tq=256):
    S = qr.shape[0]
    tq = min(tq, S)

    def body(q_ref, kn_ref, v_ref, kr_ref, o_ref):
        for i in range(S // tq):
            rows, keys = pl.ds(i * tq, tq), pl.ds(0, (i + 1) * tq)
            pr = _mla_probs(q_ref[rows, :], kn_ref[keys, :], kr_ref[keys, :], i, tq)
            o_ref[rows, :] = _nn(pr.astype(BF16), v_ref[keys, :])

    return pl.pallas_call(
        body, name="mla_attn_fwd", grid=(MLA_HEADS,),
        in_specs=[pl.BlockSpec((S, MLA_QH), lambda h: (0, h)),
                  pl.BlockSpec((S, 128), lambda h: (0, h)),
                  pl.BlockSpec((S, 128), lambda h: (0, MLA_HEADS + h)),
                  pl.BlockSpec((S, 128), lambda h: (0, 0))],
        out_specs=pl.BlockSpec((S, 128), lambda h: (0, h)),
        out_shape=jax.ShapeDtypeStruct((S, MLA_HEADS * MLA_V), F32),
        compiler_params=_cparams(("parallel",)),
    )(*map(_hbm, (qr, knv, knv, kr)))


def _mla_attn_bwd(qr, knv, kr, o, do, tq=256):
    S = qr.shape[0]
    tq = min(tq, S)
    W = MLA_HEADS * 128

    def body(q_ref, kn_ref, v_ref, kr_ref, o_ref, do_ref, dq_ref, dkn_ref, dv_ref, dkr_ref, dkn_acc, dv_acc):
        dkn_acc[...] = jnp.zeros(dkn_acc.shape, F32)
        dv_acc[...] = jnp.zeros(dv_acc.shape, F32)
        dkr_ref[...] = jnp.zeros(dkr_ref.shape, F32)
        for i in range(S // tq):
            rows, keys = pl.ds(i * tq, tq), pl.ds(0, (i + 1) * tq)
            q, kn, v, krv = q_ref[rows, :], kn_ref[keys, :], v_ref[keys, :], kr_ref[keys, :]
            pr = _mla_probs(q, kn, krv, i, tq)
            dov = do_ref[rows, :]
            delta = jnp.sum(dov * o_ref[rows, :], axis=-1, keepdims=True)
            dob = dov.astype(BF16)
            ds = (pr * (_nt(dob, v) - delta)).astype(BF16)
            dq_ref[rows, :] = jnp.concatenate([_nn(ds, kn), _nn(ds, krv)], axis=1)
            dkn_acc[keys, :] += _tn(ds, q[:, :128])
            dkr_ref[keys, :] += _tn(ds, q[:, 128:])
            dv_acc[keys, :] += _tn(pr.astype(BF16), dob)
        dkn_ref[...] = dkn_acc[...].astype(dkn_ref.dtype)
        dv_ref[...] = dv_acc[...].astype(dv_ref.dtype)

    head = lambda w: pl.BlockSpec((S, w), lambda h: (0, h))
    return pl.pallas_call(
        body, name="mla_attn_bwd", grid=(MLA_HEADS,),
        in_specs=[head(MLA_QH), head(128), pl.BlockSpec((S, 128), lambda h: (0, MLA_HEADS + h)),
                  pl.BlockSpec((S, 128), lambda h: (0, 0)), head(128), head(128)],
        out_specs=[head(MLA_QH), head(128), head(128), head(128)],
        out_shape=[jax.ShapeDtypeStruct((S, MLA_HEADS * MLA_QH), F32), jax.ShapeDtypeStruct((S, W), BF16),
                   jax.ShapeDtypeStruct((S, W), BF16), jax.ShapeDtypeStruct((S, W), F32)],
        scratch_shapes=[pltpu.VMEM((S, 128), F32), pltpu.VMEM((S, 128), F32)],
        compiler_params=_cparams(("parallel",)),
    )(*map(_hbm, (qr, knv, knv, kr, o, do)))


CONV_TILE = 256


def _shift_down(v, n):
    row = lax.broadcasted_iota(jnp.int32, v.shape, 0)
    return jnp.where(row >= n, pltpu.roll(v, n, 0), 0.0)


def _shift_up(v, n):
    S = v.shape[0]
    row = lax.broadcasted_iota(jnp.int32, v.shape, 0)
    return jnp.where(row < S - n, pltpu.roll(v, S - n, 0), 0.0)


def _conv_specs(S, n_extra_cols):
    nt = D_MODEL // CONV_TILE
    specs = [pl.BlockSpec((S, CONV_TILE), functools.partial(lambda j, o: (0, o + j), o=part * nt))
             for part in range(3)]
    specs.append(pl.BlockSpec((8, CONV_TILE), lambda j: (0, j)))
    specs += [pl.BlockSpec((S, CONV_TILE), lambda j: (0, j)) for _ in range(n_extra_cols)]
    return specs


def _conv_fwd(bcu, w8):
    S = bcu.shape[0]

    def body(b_ref, c_ref, u_ref, w_ref, y_ref):
        cu = c_ref[...] * u_ref[...]
        z = w_ref[2:3, :] * cu + w_ref[1:2, :] * _shift_down(cu, 1) + w_ref[0:1, :] * _shift_down(cu, 2)
        y_ref[...] = (b_ref[...] * z).astype(y_ref.dtype)

    return pl.pallas_call(
        body, name="conv_fwd", grid=(D_MODEL // CONV_TILE,), in_specs=_conv_specs(S, 0),
        out_specs=pl.BlockSpec((S, CONV_TILE), lambda j: (0, j)),
        out_shape=jax.ShapeDtypeStruct((S, D_MODEL), BF16),
        compiler_params=_cparams(("parallel",)),
    )(*map(_hbm, (bcu, bcu, bcu, w8)))


def _conv_bwd(bcu, w8, dy):
    S = bcu.shape[0]

    def body(b_ref, c_ref, u_ref, w_ref, dy_ref, db_ref, dc_ref, du_ref, dw_ref):
        b, c, u, dyv = b_ref[...], c_ref[...], u_ref[...], dy_ref[...]
        w0, w1, w2 = w_ref[0:1, :], w_ref[1:2, :], w_ref[2:3, :]
        cu = c * u
        cu1, cu2 = _shift_down(cu, 1), _shift_down(cu, 2)
        z = w2 * cu + w1 * cu1 + w0 * cu2
        dz = dyv * b
        db_ref[...] = (dyv * z).astype(db_ref.dtype)
        dcu = w2 * dz + w1 * _shift_up(dz, 1) + w0 * _shift_up(dz, 2)
        dc_ref[...] = (dcu * u).astype(dc_ref.dtype)
        du_ref[...] = (dcu * c).astype(du_ref.dtype)
        dw_ref[...] = jnp.zeros(dw_ref.shape, F32)
        dw_ref[0:1, :] = _colsum(dz * cu2)
        dw_ref[1:2, :] = _colsum(dz * cu1)
        dw_ref[2:3, :] = _colsum(dz * cu)

    col = pl.BlockSpec((S, CONV_TILE), lambda j: (0, j))
    return pl.pallas_call(
        body, name="conv_bwd", grid=(D_MODEL // CONV_TILE,), in_specs=_conv_specs(S, 1),
        out_specs=[col, col, col, pl.BlockSpec((8, CONV_TILE), lambda j: (0, j))],
        out_shape=[jax.ShapeDtypeStruct((S, D_MODEL), BF16)] * 3 + [jax.ShapeDtypeStruct((8, D_MODEL), F32)],
        compiler_params=_cparams(("parallel",)),
    )(*map(_hbm, (bcu, bcu, bcu, w8, dy)))


def _adamw(w, m, v, gs, name):
    L, R, Cn = w.shape
    assert len(gs) == L
    tr = R if R <= 256 else 256
    assert R % tr == 0

    def body(w_ref, m_ref, v_ref, *rest):
        g_refs, (go_ref, d_ref, nm_ref, nv_ref) = rest[:L], rest[L:]
        layer = pl.program_id(0)
        gv = g_refs[0][...]
        for k in range(1, L):
            gv = jnp.where(layer == k, g_refs[k][...], gv)
        nm = ADAM_B1 * m_ref[...] + (1.0 - ADAM_B1) * gv
        nv = ADAM_B2 * v_ref[...] + (1.0 - ADAM_B2) * jnp.square(gv)
        m_hat = nm / (1.0 - ADAM_B1 ** ADAM_STEP)
        v_hat = nv / (1.0 - ADAM_B2 ** ADAM_STEP)
        d_ref[...] = -ADAM_LR * (m_hat / (jnp.sqrt(v_hat) + ADAM_EPS) + ADAM_WD * w_ref[...])
        go_ref[...] = gv
        nm_ref[...] = nm
        nv_ref[...] = nv

    spec = pl.BlockSpec((None, tr, Cn), lambda l, i: (l, i, 0))
    g_specs = [pl.BlockSpec((tr, Cn), functools.partial(lambda l, i, k: (jnp.where(l == k, i, 0), 0), k=k))
               for k in range(L)]
    return pl.pallas_call(
        body, name=name, grid=(L, R // tr), in_specs=[spec] * 3 + g_specs, out_specs=[spec] * 4,
        out_shape=[jax.ShapeDtypeStruct((L, R, Cn), F32)] * 4,
        compiler_params=_cparams(("arbitrary", "arbitrary")),
    )(*map(_hbm, (w, m, v, *gs)))


HBM_SPEC = pl.BlockSpec(memory_space=pltpu.HBM)
BOUNCE_ROWS = 256


def _place():
    return lax.axis_index("x"), lax.axis_index("y"), lax.axis_index("c")


def _other_chips(x, y):
    return [(1 - x, y), (x, 1 - y), (1 - x, 1 - y)]


def _copy_via_vmem(src, dst, buf, sems, rows):
    ch = buf.shape[1]
    n = rows // ch
    cin = lambda i: pltpu.make_async_copy(src.at[pl.ds(i * ch, ch), :], buf.at[i % 2], sems.at[i % 2])
    cout = lambda i: pltpu.make_async_copy(buf.at[i % 2], dst.at[pl.ds(i * ch, ch), :], sems.at[2 + i % 2])
    cin(0).start()
    for i in range(n):
        cin(i).wait()
        cout(i).start()
        if i + 1 < n:
            if i >= 1:
                cout(i - 1).wait()
            cin(i + 1).start()
    if n >= 2:
        cout(n - 2).wait()
    cout(n - 1).wait()


SEM_SPEC = pl.BlockSpec(memory_space=pltpu.SEMAPHORE)
ANY_SPEC = pl.BlockSpec(memory_space=pl.ANY)
VMEM_SPEC = pl.BlockSpec(memory_space=pltpu.VMEM)
EFFECT = pltpu.SideEffectType.DATAFLOW_SIDE_EFFECTING
TOKEN = (8, 128)


def _ici_start(srcs, lands, after, copies, name):
    n, nl = len(srcs), len(lands)

    def body(*refs):
        src_refs, land_refs = refs[:n], refs[n:n + nl]
        send_sems, recv_sems, token = refs[n + nl + 1], refs[n + nl + 2], refs[-1]
        x, y, c = _place()
        for k, src, dst, to in copies(src_refs, land_refs, x, y, c):
            pltpu.make_async_remote_copy(src_ref=src, dst_ref=dst, send_sem=send_sems.at[k], recv_sem=recv_sems.at[k],
                                         device_id=to, device_id_type=MESH).start()
        token[...] = jnp.zeros(TOKEN, F32)

    n_copies = 3 * n
    res = pl.pallas_call(
        body, name=name,
        out_shape=(pltpu.SemaphoreType.DMA((n_copies,)), pltpu.SemaphoreType.DMA((n_copies,)),
                   *[pltpu.HBM(s.shape, s.dtype) for s in srcs], *[pltpu.HBM(l.shape, l.dtype) for l in lands],
                   jax.ShapeDtypeStruct(TOKEN, F32)),
        in_specs=[HBM_SPEC] * (n + nl) + [ANY_SPEC],
        out_specs=(SEM_SPEC, SEM_SPEC, *[HBM_SPEC] * (n + nl), VMEM_SPEC),
        input_output_aliases={t: 2 + t for t in range(n + nl)},
        compiler_params=pltpu.CompilerParams(has_side_effects=EFFECT),
    )(*[_hbm(s) for s in srcs], *[_hbm(l) for l in lands], after)
    return res[0], res[1], list(res[2:2 + n]), list(res[2 + n:2 + n + nl]), res[-1]


def _ici_wait(handle, after, copies, name):
    send_sems, recv_sems, srcs, lands, _ = handle
    n, nl = len(srcs), len(lands)

    def body(*refs):
        src_refs, land_refs = refs[:n], refs[n:n + nl]
        send_s, recv_s = refs[n + nl], refs[n + nl + 1]
        x, y, c = _place()
        for k, src, dst, to in copies(src_refs, land_refs, x, y, c):
            cp = pltpu.make_async_remote_copy(src_ref=src, dst_ref=dst, send_sem=send_s.at[k], recv_sem=recv_s.at[k],
                                              device_id=to, device_id_type=MESH)
            cp.wait_send()
            cp.wait_recv()

    res = pl.pallas_call(
        body, name=name,
        out_shape=(*[pltpu.HBM(s.shape, s.dtype) for s in srcs], *[pltpu.HBM(l.shape, l.dtype) for l in lands]),
        in_specs=[HBM_SPEC] * (n + nl) + [SEM_SPEC, SEM_SPEC, ANY_SPEC],
        out_specs=tuple([HBM_SPEC] * (n + nl)),
        input_output_aliases={t: t for t in range(n + nl)},
        compiler_params=pltpu.CompilerParams(has_side_effects=EFFECT),
    )(*srcs, *lands, send_sems, recv_sems, after)
    return list(res[:n]), list(res[n:])


def _gather_copies(halves):
    def copies(src_refs, land_refs, x, y, c):
        q = 2 * x + y
        out = []
        for t, H in enumerate(halves):
            for j, (cx, cy) in enumerate(_other_chips(x, y)):
                out.append((3 * t + j, src_refs[t].at[pl.ds(c * H, H), :], land_refs[t].at[q, pl.ds(c * H, H), :],
                            (cx, cy, c)))
        return out
    return copies


def _gather_wait_copies(halves):
    def copies(src_refs, land_refs, x, y, c):
        out = []
        for t, H in enumerate(halves):
            for j, (cx, cy) in enumerate(_other_chips(x, y)):
                out.append((3 * t + j, src_refs[t].at[pl.ds(c * H, H), :],
                            land_refs[t].at[2 * cx + cy, pl.ds(c * H, H), :], (cx, cy, c)))
        return out
    return copies


def _gather_start(ops, after, name):
    lands = [lax.empty((N_CHIPS,) + o.shape, o.dtype) for o in ops]
    return _ici_start(ops, lands, after, _gather_copies([o.shape[0] // 2 for o in ops]), name)


def _gather_wait(handle, after, name):
    halves = [s.shape[0] // 2 for s in handle[2]]
    return _ici_wait(handle, after, _gather_wait_copies(halves), name)


def _gather_finish(ops, lands, name):
    n = len(ops)
    halves = [o.shape[0] // 2 for o in ops]
    chunk = [min(o.shape[0], BOUNCE_ROWS) for o in ops]

    def body(*refs):
        in_refs, out_refs = refs[:n], refs[2 * n:3 * n]
        send_sems, recv_sems, local_sems = refs[3 * n:3 * n + 3]
        bufs = refs[3 * n + 3:]
        x, y, c = _place()
        q = 2 * x + y
        chips = _other_chips(x, y)
        sibling = (x, y, 1 - c)

        def copy(t, j, half):
            land = out_refs[t].at[2 * chips[j][0] + chips[j][1], pl.ds(half * halves[t], halves[t]), :]
            return pltpu.make_async_remote_copy(src_ref=land, dst_ref=land, send_sem=send_sems.at[3 * t + j],
                                                recv_sem=recv_sems.at[3 * t + j], device_id=sibling,
                                                device_id_type=MESH)

        passed = [copy(t, j, c) for t in range(n) for j in range(3)]
        for cp in passed:
            cp.start()
        for t in range(n):
            _copy_via_vmem(in_refs[t], out_refs[t].at[q], bufs[t], local_sems, ops[t].shape[0])
        for t in range(n):
            for j in range(3):
                copy(t, j, 1 - c).wait_recv()
        for cp in passed:
            cp.wait_send()

    return pl.pallas_call(
        body, name=name, in_specs=[HBM_SPEC] * (2 * n), out_specs=[HBM_SPEC] * n,
        out_shape=[jax.ShapeDtypeStruct(l.shape, l.dtype) for l in lands],
        input_output_aliases={n + t: t for t in range(n)},
        scratch_shapes=[pltpu.SemaphoreType.DMA((3 * n,)), pltpu.SemaphoreType.DMA((3 * n,)),
                        pltpu.SemaphoreType.DMA((4,))]
        + [pltpu.VMEM((2, chunk[t], ops[t].shape[1]), ops[t].dtype) for t in range(n)],
        compiler_params=pltpu.CompilerParams(vmem_limit_bytes=VMEM_LIMIT),
    )(*ops, *lands)


def _swap_halves(ops, name):
    n = len(ops)

    def body(*refs):
        in_refs, out_refs, send_sems, recv_sems = refs[:n], refs[n:2 * n], refs[2 * n], refs[2 * n + 1]
        x, y, c = _place()
        cps = []
        for t in range(n):
            H = ops[t].shape[1] // 2
            cp = pltpu.make_async_remote_copy(src_ref=in_refs[t].at[:, pl.ds((1 - c) * H, H), :],
                                              dst_ref=out_refs[t], send_sem=send_sems.at[t],
                                              recv_sem=recv_sems.at[t], device_id=(x, y, 1 - c),
                                              device_id_type=MESH)
            cp.start()
            cps.append(cp)
        for cp in cps:
            cp.wait()

    return pl.pallas_call(
        body, name=name, in_specs=[HBM_SPEC] * n, out_specs=[HBM_SPEC] * n,
        out_shape=[jax.ShapeDtypeStruct((N_CHIPS, o.shape[1] // 2, o.shape[2]), o.dtype) for o in ops],
        scratch_shapes=[pltpu.SemaphoreType.DMA((n,)), pltpu.SemaphoreType.DMA((n,))],
    )(*ops)


def _sum_rows_tile(h):
    return h if h <= 512 else 512


def _pair_sum(g, t, cq, name):
    _, a, b = g.shape
    H = a // 2
    tr = _sum_rows_tile(H)

    def body(cq_ref, g_ref, t_ref, o_ref):
        o_ref[...] = (g_ref[...].astype(F32) + t_ref[...].astype(F32)).astype(o_ref.dtype)

    grid_spec = pltpu.PrefetchScalarGridSpec(
        num_scalar_prefetch=1, grid=(N_CHIPS, H // tr),
        in_specs=[pl.BlockSpec((None, None, tr, b), lambda j, i, cq_ref: (j, cq_ref[0], i, 0)),
                  pl.BlockSpec((None, tr, b), lambda j, i, cq_ref: (j, i, 0))],
        out_specs=pl.BlockSpec((None, tr, b), lambda j, i, cq_ref: (j, i, 0)))
    return pl.pallas_call(
        body, name=name, grid_spec=grid_spec, out_shape=jax.ShapeDtypeStruct(t.shape, BF16),
        compiler_params=_cparams(("parallel", "parallel")),
    )(cq, _hbm(g.reshape(N_CHIPS, 2, H, b)), _hbm(t))


def _scatter_copies(src_refs, land_refs, x, y, c):
    out = []
    for j, (cx, cy) in enumerate(_other_chips(x, y)):
        for t in range(len(src_refs)):
            out.append((3 * t + j, src_refs[t].at[2 * cx + cy], land_refs[t].at[j], (cx, cy, c)))
    return out


def _scatter_start(ops, after, name):
    lands = [lax.empty((3,) + o.shape[1:], o.dtype) for o in ops]
    return _ici_start(ops, lands, after, _scatter_copies, name)


def _scatter_wait(handle, after, name):
    return _ici_wait(handle, after, _scatter_copies, name)


def _chip_sum(p, t, cq, name):
    _, H, b = p.shape
    tr = _sum_rows_tile(H)

    def body(cq_ref, p_ref, t_ref, o_ref):
        acc = p_ref[...].astype(F32)
        for j in range(3):
            acc = acc + t_ref[j].astype(F32)
        o_ref[...] = acc

    grid_spec = pltpu.PrefetchScalarGridSpec(
        num_scalar_prefetch=1, grid=(H // tr,),
        in_specs=[pl.BlockSpec((None, tr, b), lambda i, cq_ref: (cq_ref[1], i, 0)),
                  pl.BlockSpec((3, tr, b), lambda i, cq_ref: (0, i, 0))],
        out_specs=pl.BlockSpec((None, tr, b), lambda i, cq_ref: (cq_ref[0], i, 0)))
    out = pl.pallas_call(
        body, name=name, grid_spec=grid_spec, out_shape=jax.ShapeDtypeStruct((2, H, b), F32),
        compiler_params=_cparams(("parallel",)),
    )(cq, _hbm(p), _hbm(t))
    return out.reshape(2 * H, b)


def _join_halves(ops, name):
    n = len(ops)

    def body(*refs):
        out_refs, send_sems, recv_sems = refs[n:2 * n], refs[2 * n], refs[2 * n + 1]
        x, y, c = _place()
        cps = []
        for t in range(n):
            H = ops[t].shape[0] // 2
            mine = out_refs[t].at[pl.ds(c * H, H), :]
            cp = pltpu.make_async_remote_copy(src_ref=mine, dst_ref=mine, send_sem=send_sems.at[t],
                                              recv_sem=recv_sems.at[t], device_id=(x, y, 1 - c),
                                              device_id_type=MESH)
            cp.start()
            cps.append(cp)
        for t in range(n):
            H = ops[t].shape[0] // 2
            other = out_refs[t].at[pl.ds((1 - c) * H, H), :]
            pltpu.make_async_remote_copy(src_ref=other, dst_ref=other, send_sem=send_sems.at[t],
                                         recv_sem=recv_sems.at[t], device_id=(x, y, 1 - c),
                                         device_id_type=MESH).wait_recv()
        for cp in cps:
            cp.wait_send()

    return pl.pallas_call(
        body, name=name, in_specs=[HBM_SPEC] * n, out_specs=[HBM_SPEC] * n,
        out_shape=[jax.ShapeDtypeStruct(o.shape, o.dtype) for o in ops],
        input_output_aliases={t: t for t in range(n)},
        scratch_shapes=[pltpu.SemaphoreType.DMA((n,)), pltpu.SemaphoreType.DMA((n,))],
    )(*ops)


def _reduce_scatter_start(gs, cq, after, tag):
    ts = _swap_halves(gs, "rs_swap_" + tag)
    ps = [_pair_sum(g, t, cq, "rs_pair_sum") for g, t in zip(gs, ts)]
    return _scatter_start(ps, after, "rs_scatter_start_" + tag)


def _reduce_scatter_finish(handle, cq, after, tag):
    ps, rs = _scatter_wait(handle, after, "rs_scatter_wait_" + tag)
    fs = [_chip_sum(p, r, cq, "rs_chip_sum") for p, r in zip(ps, rs)]
    return _join_halves(fs, "rs_join_" + tag)


def _all_reduce_small(v):
    n = v.shape[0]

    def body(v_ref, out_ref, buf, send_sems, recv_sems):
        x, y, c = _place()
        me = 4 * x + 2 * y + c
        buf[me] = v_ref[...]
        cps = []
        for k in range(1, 8):
            peer = (x ^ (k >> 2), y ^ ((k >> 1) & 1), c ^ (k & 1))
            cp = pltpu.make_async_remote_copy(src_ref=v_ref, dst_ref=buf.at[me], send_sem=send_sems.at[k - 1],
                                              recv_sem=recv_sems.at[k - 1], device_id=peer, device_id_type=MESH)
            cp.start()
            cps.append(cp)
        for k in range(1, 8):
            px, py, pc = x ^ (k >> 2), y ^ ((k >> 1) & 1), c ^ (k & 1)
            land = buf.at[4 * px + 2 * py + pc]
            pltpu.make_async_remote_copy(src_ref=land, dst_ref=land, send_sem=send_sems.at[k - 1],
                                         recv_sem=recv_sems.at[k - 1], device_id=(px, py, pc),
                                         device_id_type=MESH).wait_recv()
        for cp in cps:
            cp.wait_send()
        acc = buf[0]
        for d in range(1, 8):
            acc = acc + buf[d]
        out_ref[...] = acc

    vm = pl.BlockSpec(memory_space=pltpu.VMEM)
    return pl.pallas_call(
        body, name="all_reduce_small", in_specs=[vm], out_specs=vm,
        out_shape=jax.ShapeDtypeStruct((n, 128), F32),
        scratch_shapes=[pltpu.VMEM((8, n, 128), F32), pltpu.SemaphoreType.DMA((7,)), pltpu.SemaphoreType.DMA((7,))],
    )(v)


SMALL_GATHER = (16, 1024)
SMALL_FULL = sum(_size(_full_shape(n)) for n in SMALL)
SMALL_FULL_ROWS = -(-SMALL_FULL // 128 // 8) * 8


def _layer_shards(w, i, q):
    kind, j = MIXER[i % 3], i // 3
    out = {n: w[n][i].astype(BF16) for n in COMMON_BIG}
    if kind == 'gla':
        win = jnp.zeros((D_MODEL, GLA_WIN), F32)
        win = lax.dynamic_update_slice(win, w['gla_w_in'][j], (0, (GLA_SHARD - GLA_WIN_STEP) * q))
        out['gla_w_in'] = win.astype(BF16)
        out['gla_w_out'] = w['gla_w_out'][j].astype(BF16)
    elif kind == 'mla':
        out['mla_w_in'] = jnp.pad(w['mla_w_in'][j], ((0, 0), (0, MLA_IN_PAD - MLA_IN))).astype(BF16)
        for n in ('mla_w_uq', 'mla_w_ukv', 'mla_w_out'):
            out[n] = w[n][j].astype(BF16)
    else:
        out['conv_w_in'] = w['conv_w_in'][j].astype(BF16)
        out['conv_w_out'] = w['conv_w_out'][j].astype(BF16)
    return out


def _rows_joined(g):
    return g.reshape(g.shape[0] * g.shape[1], g.shape[2])


def _cols_joined(g):
    return jnp.moveaxis(g, 0, 1).reshape(g.shape[1], -1)


def _layer_weights(g, i):
    kind = MIXER[i % 3]
    W = {}
    if 'mlp_w1' in g:
        W = {'w1': g['mlp_w1'], 'w2': _rows_joined(g['mlp_w2']), 'gate': _rows_joined(g['ple_w_gate']),
             'proj': g['ple_w_proj']}
    if kind == 'gla' and 'gla_w_out' in g:
        W['w_out'] = _rows_joined(g['gla_w_out'])
    if kind == 'gla' and 'gla_w_in' in g:
        parts = []
        for qq in range(N_CHIPS):
            lo = g['gla_w_in'][qq][:, :128]
            if qq > 0:
                lo = lo + g['gla_w_in'][qq - 1][:, GLA_WIN_STEP:]
            parts += [lo, g['gla_w_in'][qq][:, 128:GLA_WIN_STEP]]
        parts.append(g['gla_w_in'][N_CHIPS - 1][:, GLA_WIN_STEP:])
        W['w_in'] = jnp.concatenate(parts, axis=1)
    elif kind == 'mla':
        W['w_in'] = _rows_joined(g['mla_w_in'])
        uq = _cols_joined(g['mla_w_uq']).reshape(MLA_QR, MLA_HEADS, MLA_NOPE + MLA_ROPE)
        W['w_uq'] = jnp.pad(uq, ((0, 0), (0, 0), (0, MLA_QH - MLA_NOPE - MLA_ROPE))).reshape(MLA_QR, -1)
        ukv = _cols_joined(g['mla_w_ukv']).reshape(MLA_KVR, MLA_HEADS, 2, 128)
        W['w_ukv'] = ukv.transpose(0, 2, 1, 3).reshape(MLA_KVR, -1)
        W['w_out'] = _rows_joined(g['mla_w_out'])
    elif kind == 'conv':
        W['w_in'] = g['conv_w_in']
        W['w_out'] = _rows_joined(g['conv_w_out'])
    return W


def _pack_small_shards(w):
    flat = jnp.concatenate([w[n].reshape(-1) for n in SMALL_SHARDED])
    return jnp.pad(flat, (0, _size(SMALL_GATHER) - flat.shape[0])).reshape(SMALL_GATHER)


def _unpack_small_gathered(g):
    flat, out, off = g.reshape(N_CHIPS, -1), {}, 0
    for n in SMALL_SHARDED:
        shape, ax = WSPEC[n]
        seg = flat[:, off:off + _size(shape)].reshape((N_CHIPS,) + shape)
        out[n] = jnp.moveaxis(seg, 0, ax).reshape(_full_shape(n))
        off += _size(shape)
    return out


def _pack_small(vals):
    flat = jnp.concatenate([vals[n].reshape(-1) for n in SMALL])
    return jnp.pad(flat, (0, SMALL_FULL_ROWS * 128 - flat.shape[0])).reshape(SMALL_FULL_ROWS, 128)


def _unpack_small(packed, q):
    flat = packed.reshape(-1)
    out, off = {}, 0
    for n in SMALL:
        shape, ax = WSPEC[n]
        full = flat[off:off + _size(_full_shape(n))].reshape(_full_shape(n))
        off += _size(_full_shape(n))
        out[n] = full if ax is None else lax.dynamic_slice_in_dim(full, q * shape[ax], shape[ax], axis=ax)
    return out


def _row_shards(dw):
    return dw.reshape(N_CHIPS, dw.shape[0] // N_CHIPS, dw.shape[1])


def _col_shards(dw):
    return jnp.moveaxis(dw.reshape(dw.shape[0], N_CHIPS, -1), 1, 0)


def _row(v):
    return v.reshape(1, -1)


def _layer_fwd(i, xin, xin_b, p_i, W, sm, cosp, sinp, rest=None):
    kind, j = MIXER[i % 3], i // 3
    sv = {'xin': xin, 'xin_b': xin_b}
    if kind == 'gla':
        w_up = jnp.pad(sm['gla_w_gate_up'][j].astype(BF16), ((0, 128 - GLA_RANK), (0, 0)))
        pin = _mm(xin_b, W['w_in'], name="gla_in", tn=640, tm=FULL_ROWS)
        la = _mm(pin, w_up, name="gla_gate", K=128, tk=128, a_off=(0, (GLA_IN_PAD - 128) // 128), tn=512,
                 extras=[(_row(sm['gla_b_gate'][j]), 'n')],
                 epilogue=lambda acc, b: (_log_sigmoid(acc + b) * (1.0 / GLA_TAU),))
        o, states = _gla_fwd(pin, la)
        yb = _gla_post_fwd(o, pin, _row(sm['gla_norm_g'][j]))
        if rest is not None:
            W = {**W, **rest(yb)}
        h = _mm(yb, W['w_out'], name="mix_out", tm=FULL_ROWS)
        sv.update(w_up=w_up, pin=pin, la=la, o=o, states=states, yb=yb)
    elif kind == 'mla':
        gq, gkv = sm['mla_q_norm'][j:j + 1], sm['mla_kv_norm'][j:j + 1]
        cin = _mm(xin_b, W['w_in'], name="mla_in", tn=640, tm=FULL_ROWS)
        qn, kvn, kr = _mla_norm_fwd(cin, gq, gkv, cosp, sinp)
        qr = _mla_qrope_fwd(_mm(qn, W['w_uq'], name="mla_uq"), cosp, sinp)
        knv = _mm(kvn, W['w_ukv'], name="mla_ukv", out_dtypes=(BF16,))
        o = _mla_attn_fwd(qr, knv, kr)
        ob = o.astype(BF16)
        h = _mm(ob, W['w_out'], name="mix_out", tm=FULL_ROWS)
        sv.update(gq=gq, gkv=gkv, cin=cin, qn=qn, kvn=kvn, kr=kr, qr=qr, knv=knv, o=o, ob=ob)
    else:
        w8 = jnp.pad(sm['conv_w'][j], ((0, 5), (0, 0)))
        bcu = _mm(xin_b, W['w_in'], name="conv_in", tn=768, b_sh=True, tm=FULL_ROWS)
        yb = _conv_fwd(bcu, w8)
        h = _mm(yb, W['w_out'], name="mix_out", tm=FULL_ROWS)
        sv.update(w8=w8, bcu=bcu, yb=yb)
    g0, b0 = _row(sm['ln_g'][i, 0]), _row(sm['ln_b'][i, 0])
    g1, b1 = _row(sm['ln_g'][i, 1]), _row(sm['ln_b'][i, 1])
    x1, x1b = _ln_fwd(xin, h, g0, b0, "ln_fwd")
    ab = _mm(x1b, W['w1'], name="mlp_up", out_dtypes=(BF16,), b_sh=True, tm=FULL_ROWS,
             epilogue=lambda acc: (jnp.square(jnp.maximum(acc, 0.0)),))
    m = _mm(ab, W['w2'], name="mlp_down", tk=D_FF)
    x2, x2b = _ln_fwd(x1, m, g1, b1, "ln_fwd")
    pp = _mm(p_i, W['proj'], name="ple_proj", tn=256, b_sh=True)
    z, x3, x3b = _mm(x2b, W['gate'], name="ple_gate", out_dtypes=(F32, F32, BF16),
                     extras=[(x2, 'mn'), (pp, 'mn')],
                     epilogue=lambda acc, xv, pv: (acc,) + (xv + jax.nn.sigmoid(acc) * pv,) * 2)
    sv.update(h=h, x1=x1, x1b=x1b, ab=ab, m=m, x2b=x2b, pp=pp, z=z, g0=g0, g1=g1)
    return x3, x3b, sv, W


def _layer_bwd(i, dx, p_i, W, sm, sv, cosp, sinp, token, early=None):
    kind, j = MIXER[i % 3], i // 3
    big, small = {}, {}
    dpp_b, dz_b = _ple_bwd_gate(dx, sv['z'], sv['pp'], token)
    big['ple_w_proj'] = _mm(p_i, dpp_b, ta=True, name="ple_proj_dw", tn=256, out_sh=True, out_dtypes=(BF16,))
    big['ple_w_gate'] = _row_shards(_mm(sv['x2b'], dz_b, ta=True, name="dw_dd", out_dtypes=(BF16,)))
    dx2 = _mm(dz_b, W['gate'], tb=True, name="dx_dd_add", tn=1024, extras=[(dx, 'mn')],
              epilogue=lambda acc, r: (acc + r,))
    dv1, dv1b, dg1, db1 = _ln_bwd(sv['x1'], sv['m'], sv['g1'], dx2, "ln_bwd")
    big['mlp_w2'] = _row_shards(_mm(sv['ab'], dv1b, ta=True, name="mlp_down_dw", out_dtypes=(BF16,)))
    dub = _mm(dv1b, W['w2'], tb=True, name="mlp_down_dx", out_dtypes=(BF16,), tm=FULL_ROWS,
              extras=[(sv['ab'], 'mn')], epilogue=lambda acc, a: (acc * (2.0 * jnp.sqrt(a.astype(F32))),))
    big['mlp_w1'] = _mm(sv['x1b'], dub, ta=True, name="mlp_up_dw", out_sh=True, out_dtypes=(BF16,))
    dx1 = _mm(dub, W['w1'], tb=True, name="mlp_up_dx", b_sh=True, tn=1024, extras=[(dv1, 'mn')],
              epilogue=lambda acc, r: (acc + ALPHA * r,))
    early_token = None
    if early is not None:
        early_token, big = early(big), {}
    dv0, dv0b, dg0, db0 = _ln_bwd(sv['xin'], sv['h'], sv['g0'], dx1, "ln_bwd", early_token)
    small['ln_g'] = jnp.stack([dg0[0], dg1[0]])
    small['ln_b'] = jnp.stack([db0[0], db1[0]])
    resid = dict(tn=1024, extras=[(dv0, 'mn')], epilogue=lambda acc, r: (acc + ALPHA * r,))
    if kind == 'gla':
        big['gla_w_out'] = _row_shards(_mm(sv['yb'], dv0b, ta=True, name="dw_dd", out_dtypes=(BF16,)))
        dy = _mm(dv0b, W['w_out'], tb=True, name="dx_dd", tn=1024)
        do, dr_b, dng = _gla_post_bwd(dy, sv['o'], sv['pin'], _row(sm['gla_norm_g'][j]))
        dq_b, dk_b, dvv_b, dla = _gla_bwd(sv['pin'], sv['la'], sv['states'], do)
        dzg_b, dbg = _gla_gate_bwd(dla, sv['la'])
        dw_up = _mm(sv['pin'], dzg_b, ta=True, name="gla_gate_dw", M=128, tm=128,
                    a_off=(0, (GLA_IN_PAD - 128) // 128))
        dglr_b = _mm(dzg_b, sv['w_up'], tb=True, name="gla_gate_dx", out_dtypes=(BF16,))
        dpin_b = jnp.concatenate([dq_b, dk_b, dvv_b, dr_b, dglr_b], axis=1)
        dw_in = _mm(sv['xin_b'], dpin_b, ta=True, name="gla_in_dw", tn=640, out_dtypes=(BF16,))
        dxin = _mm(dpin_b, W['w_in'], tb=True, name="gla_in_dx", tk=640, **resid)
        big['gla_w_in'] = jnp.stack([dw_in[:, GLA_WIN_STEP * qq:GLA_WIN_STEP * qq + GLA_WIN]
                                     for qq in range(N_CHIPS)])
        small.update(gla_w_gate_up=dw_up[:GLA_RANK], gla_b_gate=dbg[0], gla_norm_g=dng[0])
    elif kind == 'mla':
        big['mla_w_out'] = _row_shards(_mm(sv['ob'], dv0b, ta=True, name="dw_dd", out_dtypes=(BF16,)))
        do = _mm(dv0b, W['w_out'], tb=True, name="dx_dd", tn=1024)
        dqr, dkn_b, dvv_b, dkr = _mla_attn_bwd(sv['qr'], sv['knv'], sv['kr'], sv['o'], do)
        dq_b = _mla_qrope_bwd(dqr, cosp, sinp)
        dw_uq = _mm(sv['qn'], dq_b, ta=True, name="mla_up_dw", out_dtypes=(BF16,))
        dqn = _mm(dq_b, W['w_uq'], tb=True, name="mla_up_dx")
        dknv_b = jnp.concatenate([dkn_b, dvv_b], axis=1)
        dw_ukv = _mm(sv['kvn'], dknv_b, ta=True, name="mla_up_dw", out_dtypes=(BF16,))
        dkvn = _mm(dknv_b, W['w_ukv'], tb=True, name="mla_up_dx")
        dcin_b, dgq, dgkv = _mla_norm_bwd(sv['cin'], dqn, dkvn, dkr, sv['gq'], sv['gkv'], cosp, sinp)
        big['mla_w_in'] = _row_shards(_mm(sv['xin_b'], dcin_b, ta=True, name="mla_in_dw", tn=640,
                                          out_dtypes=(BF16,)))
        dxin = _mm(dcin_b, W['w_in'], tb=True, name="mla_in_dx", tk=640, **resid)
        big['mla_w_uq'] = _col_shards(
            dw_uq.reshape(MLA_QR, MLA_HEADS, MLA_QH)[:, :, :MLA_NOPE + MLA_ROPE].reshape(MLA_QR, -1))
        big['mla_w_ukv'] = _col_shards(
            dw_ukv.reshape(MLA_KVR, 2, MLA_HEADS, 128).transpose(0, 2, 1, 3).reshape(MLA_KVR, -1))
        small.update(mla_q_norm=dgq[0], mla_kv_norm=dgkv[0])
    else:
        big['conv_w_out'] = _row_shards(_mm(sv['yb'], dv0b, ta=True, name="dw_dd", out_dtypes=(BF16,)))
        dy = _mm(dv0b, W['w_out'], tb=True, name="dx_dd", tn=1024)
        db_b, dc_b, du_b, dw8 = _conv_bwd(sv['bcu'], sv['w8'], dy)
        dbcu_b = jnp.concatenate([db_b, dc_b, du_b], axis=1)
        big['conv_w_in'] = _mm(sv['xin_b'], dbcu_b, ta=True, name="conv_in_dw", tn=768, out_sh=True,
                               out_dtypes=(BF16,))
        dxin = _mm(dbcu_b, W['w_in'], tb=True, name="conv_in_dx", tk=768, b_sh=True, **resid)
        small['conv_w'] = dw8[:3]
    return dxin, big, small


def _rope_tables(positions):
    inv_freq = ROPE_BASE ** (-jnp.arange(0, MLA_ROPE // 2, dtype=F32) * (2.0 / MLA_ROPE))
    ang = positions.astype(F32)[:, None] * inv_freq
    zeros = jnp.zeros((positions.shape[0], 64), F32)
    return (jnp.concatenate([jnp.cos(ang), jnp.cos(ang), zeros], axis=1),
            jnp.concatenate([jnp.sin(ang), jnp.sin(ang), zeros], axis=1))


FIRST_NEEDED = ['gla_w_in']


def _start_gathers(w, q):
    token, started = jnp.zeros(TOKEN, F32), []
    for i in range(DEPTH):
        sh = _layer_shards(w, i, q)
        groups = [list(sh)] if i > 0 else [FIRST_NEEDED, [n for n in sh if n not in FIRST_NEEDED]]
        for k, names in enumerate(groups):
            ops = [sh[n] for n in names]
            if i == 0 and k == 0:
                ops.append(_pack_small_shards(w))
            tag = "l%d%s" % (i, "ab"[k] if i == 0 else "")
            handle = _gather_start(ops, token, "ag_start_" + tag)
            token = handle[4]
            started.append((handle, names, tag))
    return started, token


def _finish_gather(entry, after):
    handle, names, tag = entry
    srcs, lands = _gather_wait(handle, after, "ag_wait_" + tag)
    got = _gather_finish(srcs, lands, "ag_finish_" + tag)
    return dict(zip(names, got)), got[-1]


def _local_shard_grad(name, g, q):
    if name == 'gla_w_in':
        return lax.dynamic_slice_in_dim(g, (GLA_SHARD - GLA_WIN_STEP) * q, GLA_SHARD, axis=1)
    if name == 'mla_w_in':
        return g[:, :MLA_IN]
    return g


def kernel(x, p, positions, gla_w_in, gla_w_gate_up, gla_b_gate, gla_norm_g, gla_w_out, mla_w_in, mla_q_norm, mla_kv_norm, mla_w_uq, mla_w_ukv, mla_w_out, conv_w_in, conv_w, conv_w_out, ln_g, ln_b, mlp_w1, mlp_w2, ple_w_gate, ple_w_proj, loss_target, m_gla_w_in, m_gla_w_gate_up, m_gla_b_gate, m_gla_norm_g, m_gla_w_out, m_mla_w_in, m_mla_q_norm, m_mla_kv_norm, m_mla_w_uq, m_mla_w_ukv, m_mla_w_out, m_conv_w_in, m_conv_w, m_conv_w_out, m_ln_g, m_ln_b, m_mlp_w1, m_mlp_w2, m_ple_w_gate, m_ple_w_proj, v_gla_w_in, v_gla_w_gate_up, v_gla_b_gate, v_gla_norm_g, v_gla_w_out, v_mla_w_in, v_mla_q_norm, v_mla_kv_norm, v_mla_w_uq, v_mla_w_ukv, v_mla_w_out, v_conv_w_in, v_conv_w, v_conv_w_out, v_ln_g, v_ln_b, v_mlp_w1, v_mlp_w2, v_ple_w_gate, v_ple_w_proj):
    args = locals()
    w = {n: args[n] for n in WNAMES}
    m = {n: args['m_' + n] for n in WNAMES}
    v = {n: args['v_' + n] for n in WNAMES}
    q = 2 * lax.axis_index("x") + lax.axis_index("y")
    cq = jnp.stack([lax.axis_index("c"), q]).astype(jnp.int32)

    cosp, sinp = _rope_tables(positions[0])
    started, after = _start_gathers(w, q)
    xin, saved, layers, sm = x[0], [], [], None
    xin_b = xin.astype(BF16)
    for i in range(DEPTH):
        got, last = _finish_gather(started[i + 1 if i else 0], after)
        rest = None
        if i == 0:
            sm = _unpack_small_gathered(last)
            sm['mla_q_norm'], sm['mla_kv_norm'] = w['mla_q_norm'], w['mla_kv_norm']
            rest = lambda after: _layer_weights(_finish_gather(started[1], after)[0], 0)
        xin, xin_b, sv, W = _layer_fwd(i, xin, xin_b, p[i, 0], _layer_weights(got, i), sm, cosp, sinp, rest)
        layers.append(W)
        saved.append(sv)
        after = xin
    dx, loss_cols = _loss_head(xin, loss_target[0])
    loss = lax.psum(jnp.sum(loss_cols[0]), ("x", "y", "c"))

    gbig = {n: [None] * WSPEC[n][0][0] for n in BIG}
    gsmall = {n: [None] * _full_shape(n)[0] for n in SMALL}
    pending = []

    def start(grads, i, tag):
        names = list(grads)
        handle = _reduce_scatter_start([grads[n] for n in names], cq, jnp.zeros(TOKEN, F32), tag)
        pending.append((handle, names, i, tag))
        return handle[4]

    def finish(above, after):
        for entry in [e for e in pending if e[2] > above]:
            pending.remove(entry)
            handle, names, i, tag = entry
            for n, g in zip(names, _reduce_scatter_finish(handle, cq, after, tag)):
                gbig[n][i if n in COMMON_BIG else i // 3] = _local_shard_grad(n, g, q)

    token = jnp.zeros(TOKEN, F32)
    for i in reversed(range(DEPTH)):
        early = (lambda grads: start(grads, 0, "l0a")) if i == 0 else None
        dx, big, small = _layer_bwd(i, dx, p[i, 0], layers[i], sm, saved[i], cosp, sinp, token, early)
        token = start(big, i, "l%d%s" % (i, "b" if i == 0 else ""))
        finish(i, dx)
        for n, g in small.items():
            gsmall[n][i if n in ('ln_g', 'ln_b') else i // 3] = g
    finish(-1, token)
    gsm = _unpack_small(_all_reduce_small(_pack_small({n: jnp.stack(g) for n, g in gsmall.items()})), q)

    grad, delta, new_m, new_v = {}, {}, {}, {}
    for n in BIG:
        grad[n], delta[n], new_m[n], new_v[n] = _adamw(w[n], m[n], v[n], gbig[n], "adamw_" + n)
    total = sum(_size(WSPEC[n][0]) for n in SMALL)
    rows = -(-total // 128 // 8) * 8

    def pack(dct):
        flat = jnp.concatenate([dct[n].reshape(-1) for n in SMALL])
        return jnp.pad(flat, (0, rows * 128 - total), constant_values=1.0).reshape(1, rows, 128)

    res = _adamw(pack(w), pack(m), pack(v), [pack(gsm)[0]], "adamw_small")
    for out, packed in zip((grad, delta, new_m, new_v), res):
        flat, off = packed.reshape(-1), 0
        for n in SMALL:
            sz = _size(WSPEC[n][0])
            out[n] = flat[off:off + sz].reshape(WSPEC[n][0])
            off += sz
    return (loss, dx[None], *[grad[n] for n in WNAMES], *[delta[n] for n in WNAMES],
            *[new_m[n] for n in WNAMES], *[new_v[n] for n in WNAMES])
```

```python
import functools

import numpy as np
import jax
import jax.numpy as jnp
from jax import lax
from jax.experimental import pallas as pl
from jax.experimental.pallas import tpu as pltpu

F32 = jnp.float32
BF16 = jnp.bfloat16
MESH = pl.DeviceIdType.MESH

D_MODEL = 1024
DEPTH = 4
CHUNK = 64
ALPHA = (2 * DEPTH) ** 0.25
LN_EPS = 1e-5
RMS_EPS = 1e-6
PLE_DIM = 256
D_FF = 4 * D_MODEL
GLA_HEADS = 4
GLA_DK = 128
GLA_DV = 256
GLA_RANK = 16
GLA_TAU = 16.0
GLA_HK = GLA_HEADS * GLA_DK
GLA_HV = GLA_HEADS * GLA_DV
GLA_IN = 2 * GLA_HK + GLA_HV + D_MODEL + GLA_RANK
GLA_IN_PAD = 2 * GLA_HK + GLA_HV + D_MODEL + 128
GLA_SHARD = GLA_IN // 4
GLA_WIN = 896
GLA_WIN_STEP = 768
MLA_HEADS = 8
MLA_NOPE = 128
MLA_ROPE = 64
MLA_V = 128
MLA_QR = 256
MLA_KVR = 256
MLA_IN = MLA_QR + MLA_KVR + MLA_ROPE
MLA_IN_PAD = MLA_QR + MLA_KVR + 128
MLA_QH = 256
ROPE_BASE = 10000.0
ADAM_LR = 0.001
ADAM_B1 = 0.9
ADAM_B2 = 0.999
ADAM_EPS = 1e-08
ADAM_WD = 0.01
ADAM_STEP = 10

VMEM_LIMIT = 48 * 1024 * 1024
FULL_ROWS = 2048
N_CHIPS = 4

WSPEC = {
    'gla_w_in': ((2, 1024, 772), 2), 'gla_w_gate_up': ((2, 16, 128), 2), 'gla_b_gate': ((2, 128), 1),
    'gla_norm_g': ((2, 64), 1), 'gla_w_out': ((2, 256, 1024), 1), 'mla_w_in': ((1, 256, 576), 1),
    'mla_q_norm': ((1, 256), None), 'mla_kv_norm': ((1, 256), None), 'mla_w_uq': ((1, 256, 384), 2),
    'mla_w_ukv': ((1, 256, 512), 2), 'mla_w_out': ((1, 256, 1024), 1), 'conv_w_in': ((1, 1024, 768), 2),
    'conv_w': ((1, 3, 256), 2), 'conv_w_out': ((1, 256, 1024), 1), 'ln_g': ((4, 2, 256), 2),
    'ln_b': ((4, 2, 256), 2), 'mlp_w1': ((4, 1024, 1024), 2), 'mlp_w2': ((4, 1024, 1024), 1),
    'ple_w_gate': ((4, 256, 1024), 1), 'ple_w_proj': ((4, 256, 256), 2),
}
WNAMES = list(WSPEC)
BIG = ['gla_w_in', 'gla_w_out', 'mla_w_in', 'mla_w_uq', 'mla_w_ukv', 'mla_w_out', 'conv_w_in', 'conv_w_out',
       'mlp_w1', 'mlp_w2', 'ple_w_gate', 'ple_w_proj']
SMALL_SHARDED = ['gla_w_gate_up', 'gla_b_gate', 'gla_norm_g', 'conv_w', 'ln_g', 'ln_b']
SMALL = SMALL_SHARDED + ['mla_q_norm', 'mla_kv_norm']
MIXER = ['gla', 'mla', 'conv']
LAYER_BIG = {'gla': ['gla_w_in', 'gla_w_out'], 'mla': ['mla_w_in', 'mla_w_uq', 'mla_w_ukv', 'mla_w_out'],
             'conv': ['conv_w_in', 'conv_w_out']}
COMMON_BIG = ['mlp_w1', 'mlp_w2', 'ple_w_gate', 'ple_w_proj']


def _size(shape):
    return int(np.prod(shape))


def _full_shape(name):
    shape, ax = WSPEC[name]
    if ax is None:
        return shape
    return tuple(s * N_CHIPS if i == ax else s for i, s in enumerate(shape))


def _cparams(sem=None):
    return pltpu.CompilerParams(dimension_semantics=sem, vmem_limit_bytes=VMEM_LIMIT)


def _hbm(v):
    return pltpu.with_memory_space_constraint(v, pltpu.HBM)


def _mm(a, b, *, name, ta=False, tb=False, M=None, N=None, K=None, out_dtypes=(F32,), epilogue=None, extras=(),
        tm=1024, tn=512, tk=None, a_off=(0, 0), b_sh=False, out_sh=False, n_sums=0):
    if M is None:
        M = a.shape[1] if ta else a.shape[0]
    if K is None:
        K = a.shape[0] if ta else a.shape[1]
    if b_sh:
        kw, nq = b.shape[1], b.shape[2]
        n_b, k_b = (kw, N_CHIPS * nq) if tb else (N_CHIPS * nq, kw)
        N = n_b if N is None else N
        assert K == k_b
    elif N is None:
        N = b.shape[0] if tb else b.shape[1]
    if tk is None:
        tk = FULL_ROWS if ta else 1024
    tm, tn, tk = min(tm, M), min(tn, N), min(tk, K)
    assert M % tm == 0 and N % tn == 0 and K % tk == 0, (name, M, N, K, tm, tn, tk)
    nk = K // tk
    n_ex, n_out = len(extras), len(out_dtypes)
    assert n_sums == 0 or tn == N

    def body(a_ref, b_ref, *rest):
        ex_refs, out_refs = rest[:n_ex], rest[n_ex:n_ex + n_out]
        sum_refs = rest[n_ex + n_out:n_ex + n_out + n_sums]
        first_rows = pl.program_id(0) == 0
        part = lax.dot_general(a_ref[...].astype(BF16), b_ref[...].astype(BF16),
                               ((((0,) if ta else (1,)), ((1,) if tb else (0,))), ((), ())),
                               preferred_element_type=F32)

        def finish(acc):
            res = (acc,) if epilogue is None else epilogue(acc, *[r[...] for r in ex_refs])
            if n_sums:
                res, sums = res

                @pl.when(first_rows)
                def _():
                    for r in sum_refs:
                        r[...] = jnp.zeros(r.shape, F32)

                for r, v in zip(sum_refs, sums):
                    r[...] += jnp.broadcast_to(v, r.shape)
            for r, v in zip(out_refs, res):
                r[...] = v.astype(r.dtype)

        if nk == 1:
            finish(part)
        else:
            acc_ref = rest[-1]
            k = pl.program_id(2)

            @pl.when(k == 0)
            def _():
                acc_ref[...] = part

            @pl.when(k > 0)
            def _():
                acc_ref[...] += part

            @pl.when(k == nk - 1)
            def _():
                finish(acc_ref[...])

    if ta:
        a_spec = pl.BlockSpec((tk, tm), lambda i, j, k: (k + a_off[0], i + a_off[1]))
    else:
        a_spec = pl.BlockSpec((tm, tk), lambda i, j, k: (i + a_off[0], k + a_off[1]))
    if b_sh and tb:
        assert nq % tk == 0
        per = nq // tk
        b_spec = pl.BlockSpec((None, tn, tk), lambda i, j, k: (k // per, j, k % per))
    elif b_sh:
        assert nq % tn == 0
        per = nq // tn
        b_spec = pl.BlockSpec((None, tk, tn), lambda i, j, k: (j // per, k, j % per))
    elif tb:
        b_spec = pl.BlockSpec((tn, tk), lambda i, j, k: (j, k))
    else:
        b_spec = pl.BlockSpec((tk, tn), lambda i, j, k: (k, j))
    ex_specs = []
    for arr, kind in extras:
        if kind == 'mn':
            ex_specs.append(pl.BlockSpec((tm, tn), lambda i, j, k: (i, j)))
        elif kind == 'n':
            ex_specs.append(pl.BlockSpec((1, tn), lambda i, j, k: (0, j)))
        else:
            ex_specs.append(pl.BlockSpec(arr.shape, lambda i, j, k: (0, 0)))
    if out_sh:
        assert (N // N_CHIPS) % tn == 0
        per_o = N // N_CHIPS // tn
        o_spec = pl.BlockSpec((None, tm, tn), lambda i, j, k: (j // per_o, i, j % per_o))
        o_shape = (N_CHIPS, M, N // N_CHIPS)
    else:
        o_spec = pl.BlockSpec((tm, tn), lambda i, j, k: (i, j))
        o_shape = (M, N)
    outs = pl.pallas_call(
        body, name=name, grid=(M // tm, N // tn, nk),
        in_specs=[a_spec, b_spec] + ex_specs,
        out_specs=[o_spec for _ in out_dtypes] + [pl.BlockSpec((8, N), lambda i, j, k: (0, 0))] * n_sums,
        out_shape=[jax.ShapeDtypeStruct(o_shape, d) for d in out_dtypes]
        + [jax.ShapeDtypeStruct((8, N), F32)] * n_sums,
        scratch_shapes=[pltpu.VMEM((tm, tn), F32)] if nk > 1 else [],
        compiler_params=_cparams(("arbitrary" if n_sums else "parallel", "parallel", "arbitrary")),
    )(a, b, *[e[0] for e in extras])
    if n_sums:
        return tuple(outs[:n_out]), tuple(outs[n_out:])
    return outs[0] if n_out == 1 else tuple(outs)


def _rowwise(fn, *, name, rows, pars=(), outs=(), accs=(), tm=256):
    S = rows[0][0].shape[0]
    tm = min(tm, S)
    assert S % tm == 0
    n_r, n_p, n_o, n_a = len(rows), len(pars), len(outs), len(accs)

    def body(*refs):
        r_refs, p_refs = refs[:n_r], refs[n_r:n_r + n_p]
        o_refs, a_refs = refs[n_r + n_p:n_r + n_p + n_o], refs[n_r + n_p + n_o:]
        o_vals, a_vals = fn([r[...] for r in r_refs], [p[...] for p in p_refs])
        for r, v in zip(o_refs, o_vals):
            r[...] = v.astype(r.dtype)
        if n_a:
            i = pl.program_id(0)

            @pl.when(i == 0)
            def _():
                for r in a_refs:
                    r[...] = jnp.zeros(r.shape, r.dtype)

            for r, v in zip(a_refs, a_vals):
                r[...] += jnp.broadcast_to(v, r.shape)

    in_specs = [pl.BlockSpec((tm, w), functools.partial(lambda i, o: (i, o), o=off)) for _, w, off in rows]
    in_specs += [pl.BlockSpec(p.shape, functools.partial(lambda i, nd: (0,) * nd, nd=p.ndim)) for p in pars]
    out_specs = [pl.BlockSpec((tm, w), lambda i: (i, 0)) for w, _ in outs]
    out_specs += [pl.BlockSpec((8, w), lambda i: (0, 0)) for w in accs]
    out_shape = [jax.ShapeDtypeStruct((S, w), d) for w, d in outs]
    out_shape += [jax.ShapeDtypeStruct((8, w), F32) for w in accs]
    res = pl.pallas_call(
        body, name=name, grid=(S // tm,), in_specs=in_specs, out_specs=out_specs, out_shape=out_shape,
        compiler_params=_cparams(("arbitrary",)),
    )(*[r[0] for r in rows], *pars)
    return tuple(res)


def _colsum(v):
    return jnp.sum(v, axis=0, keepdims=True)


def _ln_stats(v):
    mu = jnp.mean(v, axis=-1, keepdims=True)
    d = v - mu
    var = jnp.mean(d * d, axis=-1, keepdims=True)
    rstd = lax.rsqrt(var + LN_EPS)
    return d * rstd, rstd


def _ln_fwd_epilogue(h, x, g, b):
    v = ALPHA * x + h
    xhat, _ = _ln_stats(v)
    y = xhat * g + b
    return y, y, v


def _ln_bwd_epilogue(scale):
    def epilogue(acc, resid, v, g, *unused):
        dy = acc + scale * resid
        xhat, rstd = _ln_stats(v)
        dxh = dy * g
        m1 = jnp.mean(dxh, axis=-1, keepdims=True)
        m2 = jnp.mean(dxh * xhat, axis=-1, keepdims=True)
        dv = rstd * (dxh - m1 - xhat * m2)
        return (dv, dv), (_colsum(dy * xhat), _colsum(dy))
    return epilogue


def _loss_head(y, t):
    def fn(r, p):
        d = r[0] - r[1]
        return [d * (1.0 / D_MODEL)], [_colsum(d * d) * (0.5 / D_MODEL)]
    return _rowwise(fn, name="loss_head", rows=[(y, D_MODEL, 0), (t, D_MODEL, 0)], outs=[(D_MODEL, F32)],
                    accs=[D_MODEL])


def _ple_bwd_gate(dx3, z, pp, token):
    def fn(r, p):
        s = jax.nn.sigmoid(r[1])
        return [r[0] * s, r[0] * r[2] * s * (1.0 - s)], []
    return _rowwise(fn, name="ple_bwd_gate", rows=[(dx3, D_MODEL, 0), (z, D_MODEL, 0), (pp, D_MODEL, 0)],
                    pars=[token], outs=[(D_MODEL, BF16), (D_MODEL, BF16)])


N_LEVELS = 6


def _gla_consts():
    C = CHUNK
    A = np.zeros((N_LEVELS + 3, C, C), np.float32)
    masks = np.zeros((N_LEVELS + 1, C, C), np.float32)
    r = np.arange(C)[:, None]
    u = np.arange(C)[None, :]
    for l in range(N_LEVELS):
        half = C >> (l + 1)
        mid = (r // (2 * half)) * (2 * half) + half - 1
        A[l] = np.where(r > mid, (u > mid) & (u <= r), (u > r) & (u <= mid))
        masks[l] = ((r // (2 * half)) == (u // (2 * half))) & (((r // half) % 2) != ((u // half) % 2))
    masks[N_LEVELS] = (r == u)
    A[N_LEVELS] = (u <= r)
    A[N_LEVELS + 1] = (u > r)
    A[N_LEVELS + 2] = 1.0
    A = A.reshape(-1, C)
    return A, np.ascontiguousarray(A.T), masks


def _split3(v):
    hi = v.astype(BF16)
    r1 = v - hi.astype(F32)
    mid = r1.astype(BF16)
    lo = (r1 - mid.astype(F32)).astype(BF16)
    return hi, mid, lo


def _dot_exact01(a01, v):
    hi, mid, lo = _split3(v)
    f = lambda p: jnp.dot(a01, p, preferred_element_type=F32)
    return f(hi) + f(mid) + f(lo)


def _nt(a, b):
    return lax.dot_general(a, b, (((1,), (1,)), ((), ())), preferred_element_type=F32)


def _tn(a, b):
    return lax.dot_general(a, b, (((0,), (0,)), ((), ())), preferred_element_type=F32)


def _nn(a, b):
    return jnp.dot(a, b, preferred_element_type=F32)


def _gla_chunk_terms(q, k, E, m_ref):
    C = CHUNK
    scores = m_ref[N_LEVELS] * _nt(q.astype(BF16), k.astype(BF16))
    qes, kes = [], []
    for l in range(N_LEVELS):
        El = E[l * C:(l + 1) * C]
        qe, ke = (q * El).astype(BF16), (k * El).astype(BF16)
        qes.append(qe)
        kes.append(ke)
        scores = scores + m_ref[l] * _nt(qe, ke)
    return qes, kes, scores


def _head(v, h, w):
    return v[:, h * w:(h + 1) * w]


def _gla_fwd(pin, la):
    S = pin.shape[0]
    NC = S // CHUNK
    C = CHUNK
    A, _, masks = _gla_consts()

    def body(q_ref, k_ref, v_ref, la_ref, a_ref, m_ref, o_ref, st_ref, state):
        @pl.when(pl.program_id(0) == 0)
        def _():
            state[...] = jnp.zeros(state.shape, F32)

        E_all = jnp.exp(_dot_exact01(a_ref[...], la_ref[...]))
        q_all = q_ref[...] * (GLA_DK ** -0.5)
        k_all, v_all = k_ref[...], v_ref[...]
        outs = []
        for h in range(GLA_HEADS):
            q, k, E = _head(q_all, h, GLA_DK), _head(k_all, h, GLA_DK), _head(E_all, h, GLA_DK)
            _, _, scores = _gla_chunk_terms(q, k, E, m_ref)
            Eq, Ek, Ee = E[6 * C:7 * C], E[7 * C:8 * C], E[8 * C:9 * C]
            st = state[h]
            st_ref[h] = st
            vb = _head(v_all, h, GLA_DV).astype(BF16)
            outs.append(_nn(scores.astype(BF16), vb) + _nt((q * Eq).astype(BF16), st.astype(BF16)))
            state[h] = st * jnp.concatenate([Ee] * (GLA_DV // C), axis=0) + _tn(vb, (k * Ek).astype(BF16))
        o_ref[...] = jnp.concatenate(outs, axis=1)

    return pl.pallas_call(
        body, name="gla_fwd", grid=(NC,),
        in_specs=[pl.BlockSpec((C, GLA_HK), lambda c: (c, 0)),
                  pl.BlockSpec((C, GLA_HK), lambda c: (c, 1)),
                  pl.BlockSpec((C, GLA_HV), lambda c: (c, 2 * GLA_HK // GLA_HV)),
                  pl.BlockSpec((C, GLA_HK), lambda c: (c, 0)),
                  pl.BlockSpec(A.shape, lambda c: (0, 0)),
                  pl.BlockSpec(masks.shape, lambda c: (0, 0, 0))],
        out_specs=[pl.BlockSpec((C, GLA_HV), lambda c: (c, 0)),
                   pl.BlockSpec((GLA_HEADS, None, GLA_DV, GLA_DK), lambda c: (0, c, 0, 0))],
        out_shape=[jax.ShapeDtypeStruct((S, GLA_HV), F32),
                   jax.ShapeDtypeStruct((GLA_HEADS, NC, GLA_DV, GLA_DK), F32)],
        scratch_shapes=[pltpu.VMEM((GLA_HEADS, GLA_DV, GLA_DK), F32)],
        compiler_params=_cparams(("arbitrary",)),
    )(pin, pin, pin, la, jnp.asarray(A, BF16), jnp.asarray(masks))


def _gla_bwd(pin, la, states, do):
    S = pin.shape[0]
    NC = S // CHUNK
    C = CHUNK
    A, AT, masks = _gla_consts()
    scale = GLA_DK ** -0.5

    def body(q_ref, k_ref, v_ref, la_ref, st_ref, do_ref, a_ref, at_ref, m_ref,
             dq_ref, dk_ref, dv_ref, dla_ref, dstate):
        @pl.when(pl.program_id(0) == 0)
        def _():
            dstate[...] = jnp.zeros(dstate.shape, F32)

        E_all = jnp.exp(_dot_exact01(a_ref[...], la_ref[...]))
        q_all = q_ref[...] * scale
        k_all, v_all, do_all = k_ref[...], v_ref[...], do_ref[...]
        dqs, dks, dvs, dXs = [], [], [], []
        for h in range(GLA_HEADS):
            q, k, E = _head(q_all, h, GLA_DK), _head(k_all, h, GLA_DK), _head(E_all, h, GLA_DK)
            qes, kes, scores = _gla_chunk_terms(q, k, E, m_ref)
            Eq, Ek, Ee = E[6 * C:7 * C], E[7 * C:8 * C], E[8 * C:9 * C]
            st, dst = st_ref[h], dstate[h]
            dob, vb = _head(do_all, h, GLA_DV).astype(BF16), _head(v_all, h, GLA_DV).astype(BF16)
            dstb = dst.astype(BF16)
            qEq, kEk = (q * Eq).astype(BF16), (k * Ek).astype(BF16)
            dsc = _nt(dob, vb)
            dvs.append(_tn(scores.astype(BF16), dob) + _nt(kEk, dstb))
            dqEq = _nn(dob, st.astype(BF16))
            dkEk = _nn(vb, dstb)
            Gd = (m_ref[N_LEVELS] * dsc).astype(BF16)
            dq = _nn(Gd, k.astype(BF16)) + dqEq * Eq
            dk = _tn(Gd, q.astype(BF16)) + dkEk * Ek
            dX = []
            for l in range(N_LEVELS):
                El = E[l * C:(l + 1) * C]
                G = (m_ref[l] * dsc).astype(BF16)
                dqe, dke = _nn(G, kes[l]), _tn(G, qes[l])
                dq = dq + dqe * El
                dk = dk + dke * El
                dX.append((dqe * q + dke * k) * El)
            dX.append(dqEq * q * Eq)
            dX.append(dkEk * k * Ek)
            prod = dst * st
            dEe = prod[0:C]
            for i in range(1, GLA_DV // C):
                dEe = dEe + prod[i * C:(i + 1) * C]
            dX.append(dEe * Ee)
            dXs.append(jnp.concatenate(dX, axis=0))
            dqs.append(dq * scale)
            dks.append(dk)
            dstate[h] = dst * jnp.concatenate([Ee] * (GLA_DV // C), axis=0) + _tn(dob, qEq)
        dla_ref[...] = _dot_exact01(at_ref[...], jnp.concatenate(dXs, axis=1))
        dq_ref[...] = jnp.concatenate(dqs, axis=1).astype(dq_ref.dtype)
        dk_ref[...] = jnp.concatenate(dks, axis=1).astype(dk_ref.dtype)
        dv_ref[...] = jnp.concatenate(dvs, axis=1).astype(dv_ref.dtype)

    rc = lambda c: NC - 1 - c
    return pl.pallas_call(
        body, name="gla_bwd", grid=(NC,),
        in_specs=[pl.BlockSpec((C, GLA_HK), lambda c: (rc(c), 0)),
                  pl.BlockSpec((C, GLA_HK), lambda c: (rc(c), 1)),
                  pl.BlockSpec((C, GLA_HV), lambda c: (rc(c), 2 * GLA_HK // GLA_HV)),
                  pl.BlockSpec((C, GLA_HK), lambda c: (rc(c), 0)),
                  pl.BlockSpec((GLA_HEADS, None, GLA_DV, GLA_DK), lambda c: (0, rc(c), 0, 0)),
                  pl.BlockSpec((C, GLA_HV), lambda c: (rc(c), 0)),
                  pl.BlockSpec(A.shape, lambda c: (0, 0)),
                  pl.BlockSpec(AT.shape, lambda c: (0, 0)),
                  pl.BlockSpec(masks.shape, lambda c: (0, 0, 0))],
        out_specs=[pl.BlockSpec((C, GLA_HK), lambda c: (rc(c), 0)),
                   pl.BlockSpec((C, GLA_HK), lambda c: (rc(c), 0)),
                   pl.BlockSpec((C, GLA_HV), lambda c: (rc(c), 0)),
                   pl.BlockSpec((C, GLA_HK), lambda c: (rc(c), 0))],
        out_shape=[jax.ShapeDtypeStruct((S, GLA_HK), BF16), jax.ShapeDtypeStruct((S, GLA_HK), BF16),
                   jax.ShapeDtypeStruct((S, GLA_HV), BF16), jax.ShapeDtypeStruct((S, GLA_HK), F32)],
        scratch_shapes=[pltpu.VMEM((GLA_HEADS, GLA_DV, GLA_DK), F32)],
        compiler_params=_cparams(("arbitrary",)),
    )(pin, pin, pin, la, states, do, jnp.asarray(A, BF16), jnp.asarray(AT, BF16), jnp.asarray(masks))


def _gla_post_fwd(o, pin, g):
    def fn(r, p):
        ov, rv = r
        ys = []
        for h in range(GLA_HEADS):
            oh = ov[:, h * GLA_DV:(h + 1) * GLA_DV]
            rh = rv[:, h * GLA_DV:(h + 1) * GLA_DV]
            rs = lax.rsqrt(jnp.mean(oh * oh, axis=-1, keepdims=True) + RMS_EPS)
            ys.append(oh * rs * p[0] * (rh * jax.nn.sigmoid(rh)))
        return [jnp.concatenate(ys, axis=1)], []
    return _rowwise(fn, name="gla_post_fwd", rows=[(o, GLA_HV, 0), (pin, GLA_HV, (2 * GLA_HK + GLA_HV) // GLA_HV)],
                    pars=[g], outs=[(GLA_HV, BF16)])[0]


def _gla_post_bwd(dy, o, pin, g):
    def fn(r, p):
        dyv, ov, rv = r
        dos, drs, dg = [], [], 0.0
        for h in range(GLA_HEADS):
            sl = slice(h * GLA_DV, (h + 1) * GLA_DV)
            oh, rh, dyh = ov[:, sl], rv[:, sl], dyv[:, sl]
            rs = lax.rsqrt(jnp.mean(oh * oh, axis=-1, keepdims=True) + RMS_EPS)
            xh = oh * rs
            sg = jax.nn.sigmoid(rh)
            d_on = dyh * (rh * sg)
            drs.append(dyh * (xh * p[0]) * (sg * (1.0 + rh * (1.0 - sg))))
            dg = dg + _colsum(d_on * xh)
            dxh = d_on * p[0]
            dos.append(rs * (dxh - xh * jnp.mean(dxh * xh, axis=-1, keepdims=True)))
        return [jnp.concatenate(dos, axis=1), jnp.concatenate(drs, axis=1)], [dg]
    return _rowwise(fn, name="gla_post_bwd",
                    rows=[(dy, GLA_HV, 0), (o, GLA_HV, 0), (pin, GLA_HV, (2 * GLA_HK + GLA_HV) // GLA_HV)],
                    pars=[g], outs=[(GLA_HV, F32), (GLA_HV, BF16)], accs=[GLA_DV])


def _gla_gate_bwd(dla, la):
    def fn(r, p):
        dz = r[0] * (1.0 / GLA_TAU) * (1.0 - jnp.exp(GLA_TAU * r[1]))
        return [dz], [_colsum(dz)]
    return _rowwise(fn, name="gla_gate_bwd", rows=[(dla, GLA_HK, 0), (la, GLA_HK, 0)], outs=[(GLA_HK, BF16)],
                    accs=[GLA_HK])


def _log_sigmoid(z):
    return jnp.minimum(z, 0.0) - jnp.log(1.0 + jnp.exp(-jnp.abs(z)))


def _rot_half(v):
    lane = lax.broadcasted_iota(jnp.int32, v.shape, 1)
    return jnp.where(lane < 32, -pltpu.roll(v, 96, 1), jnp.where(lane < 64, pltpu.roll(v, 32, 1), 0.0))


def _rms(v):
    rs = lax.rsqrt(jnp.mean(v * v, axis=-1, keepdims=True) + RMS_EPS)
    return v * rs, rs


def _mla_norm_fwd(cin, gq, gkv, cosp, sinp):
    def fn(r, p):
        cv, cs, sn = r
        qn, _ = _rms(cv[:, :MLA_QR])
        kvn, _ = _rms(cv[:, MLA_QR:MLA_QR + MLA_KVR])
        kr = cv[:, MLA_QR + MLA_KVR:]
        return [qn * p[0], kvn * p[1], kr * cs + _rot_half(kr) * sn], []
    return _rowwise(fn, name="mla_norm_fwd", rows=[(cin, MLA_IN_PAD, 0), (cosp, 128, 0), (sinp, 128, 0)],
                    pars=[gq, gkv], outs=[(MLA_QR, BF16), (MLA_KVR, BF16), (128, BF16)])


def _mla_qrope_fwd(q, cosp, sinp):
    scale = (MLA_NOPE + MLA_ROPE) ** -0.5

    def fn(r, p):
        qv, cs, sn = r
        parts = []
        for h in range(MLA_HEADS):
            parts.append(qv[:, h * MLA_QH:h * MLA_QH + 128] * scale)
            rp = qv[:, h * MLA_QH + 128:(h + 1) * MLA_QH]
            parts.append((rp * cs + _rot_half(rp) * sn) * scale)
        return [jnp.concatenate(parts, axis=1)], []
    W = MLA_HEADS * MLA_QH
    return _rowwise(fn, name="mla_qrope_fwd", rows=[(q, W, 0), (cosp, 128, 0), (sinp, 128, 0)],
                    outs=[(W, BF16)])[0]


def _mla_qrope_bwd(dq, cosp, sinp):
    scale = (MLA_NOPE + MLA_ROPE) ** -0.5

    def fn(r, p):
        dv, cs, sn = r
        parts = []
        for h in range(MLA_HEADS):
            parts.append(dv[:, h * MLA_QH:h * MLA_QH + 128] * scale)
            rp = dv[:, h * MLA_QH + 128:(h + 1) * MLA_QH]
            parts.append((rp * cs - _rot_half(rp) * sn) * scale)
        return [jnp.concatenate(parts, axis=1)], []
    W = MLA_HEADS * MLA_QH
    return _rowwise(fn, name="mla_qrope_bwd", rows=[(dq, W, 0), (cosp, 128, 0), (sinp, 128, 0)],
                    outs=[(W, BF16)])[0]


def _mla_norm_bwd(cin, dqn, dkvn, dkr, gq, gkv, cosp, sinp):
    def fn(r, p):
        cv, dq_, dkv_, dkr_, cs, sn = r
        outs, accs = [], []
        for (lo, hi), dn, g in (((0, MLA_QR), dq_, p[0]), ((MLA_QR, MLA_QR + MLA_KVR), dkv_, p[1])):
            xh, rs = _rms(cv[:, lo:hi])
            dxh = dn * g
            outs.append(rs * (dxh - xh * jnp.mean(dxh * xh, axis=-1, keepdims=True)))
            accs.append(_colsum(dn * xh))
        dk = dkr_[:, 0:128]
        for h in range(1, MLA_HEADS):
            dk = dk + dkr_[:, h * 128:(h + 1) * 128]
        outs.append(dk * cs - _rot_half(dk) * sn)
        return [jnp.concatenate(outs, axis=1)], accs
    return _rowwise(fn, name="mla_norm_bwd",
                    rows=[(cin, MLA_IN_PAD, 0), (dqn, MLA_QR, 0), (dkvn, MLA_KVR, 0), (dkr, MLA_HEADS * 128, 0),
                          (cosp, 128, 0), (sinp, 128, 0)],
                    pars=[gq, gkv], outs=[(MLA_IN_PAD, BF16)], accs=[MLA_QR, MLA_KVR])


def _mla_probs(q, kn, kr, i, tq):
    s = _nt(q[:, :128], kn) + _nt(q[:, 128:], kr)
    row = (i * tq + lax.broadcasted_iota(jnp.int32, s.shape, 0)) // CHUNK
    col = lax.broadcasted_iota(jnp.int32, s.shape, 1) // CHUNK
    s = jnp.where(col <= row, s, -jnp.inf)
    e = jnp.exp(s - jnp.max(s, axis=-1, keepdims=True))
    return e / jnp.sum(e, axis=-1, keepdims=True)


def _mla_attn_fwd(qr, knv, kr, tq=256):
    S = qr.shape[0]
    tq = min(tq, S)

    def body(q_ref, kn_ref, v_ref, kr_ref, o_ref):
        for i in range(S // tq):
            rows, keys = pl.ds(i * tq, tq), pl.ds(0, (i + 1) * tq)
            pr = _mla_probs(q_ref[rows, :], kn_ref[keys, :], kr_ref[keys, :], i, tq)
            o_ref[rows, :] = _nn(pr.astype(BF16), v_ref[keys, :])

    return pl.pallas_call(
        body, name="mla_attn_fwd", grid=(MLA_HEADS,),
        in_specs=[pl.BlockSpec((S, MLA_QH), lambda h: (0, h)),
                  pl.BlockSpec((S, 128), lambda h: (0, h)),
                  pl.BlockSpec((S, 128), lambda h: (0, MLA_HEADS + h)),
                  pl.BlockSpec((S, 128), lambda h: (0, 0))],
        out_specs=pl.BlockSpec((S, 128), lambda h: (0, h)),
        out_shape=jax.ShapeDtypeStruct((S, MLA_HEADS * MLA_V), F32),
        compiler_params=_cparams(("parallel",)),
    )(qr, knv, knv, kr)


def _mla_attn_bwd(qr, knv, kr, o, do, tq=256):
    S = qr.shape[0]
    tq = min(tq, S)
    W = MLA_HEADS * 128

    def body(q_ref, kn_ref, v_ref, kr_ref, o_ref, do_ref, dq_ref, dkn_ref, dv_ref, dkr_ref, dkn_acc, dv_acc):
        dkn_acc[...] = jnp.zeros(dkn_acc.shape, F32)
        dv_acc[...] = jnp.zeros(dv_acc.shape, F32)
        dkr_ref[...] = jnp.zeros(dkr_ref.shape, F32)
        for i in range(S // tq):
            rows, keys = pl.ds(i * tq, tq), pl.ds(0, (i + 1) * tq)
            q, kn, v, krv = q_ref[rows, :], kn_ref[keys, :], v_ref[keys, :], kr_ref[keys, :]
            pr = _mla_probs(q, kn, krv, i, tq)
            dov = do_ref[rows, :]
            delta = jnp.sum(dov * o_ref[rows, :], axis=-1, keepdims=True)
            dob = dov.astype(BF16)
            ds = (pr * (_nt(dob, v) - delta)).astype(BF16)
            dq_ref[rows, :] = jnp.concatenate([_nn(ds, kn), _nn(ds, krv)], axis=1)
            dkn_acc[keys, :] += _tn(ds, q[:, :128])
            dkr_ref[keys, :] += _tn(ds, q[:, 128:])
            dv_acc[keys, :] += _tn(pr.astype(BF16), dob)
        dkn_ref[...] = dkn_acc[...].astype(dkn_ref.dtype)
        dv_ref[...] = dv_acc[...].astype(dv_ref.dtype)

    head = lambda w: pl.BlockSpec((S, w), lambda h: (0, h))
    return pl.pallas_call(
        body, name="mla_attn_bwd", grid=(MLA_HEADS,),
        in_specs=[head(MLA_QH), head(128), pl.BlockSpec((S, 128), lambda h: (0, MLA_HEADS + h)),
                  pl.BlockSpec((S, 128), lambda h: (0, 0)), head(128), head(128)],
        out_specs=[head(MLA_QH), head(128), head(128), head(128)],
        out_shape=[jax.ShapeDtypeStruct((S, MLA_HEADS * MLA_QH), F32), jax.ShapeDtypeStruct((S, W), BF16),
                   jax.ShapeDtypeStruct((S, W), BF16), jax.ShapeDtypeStruct((S, W), F32)],
        scratch_shapes=[pltpu.VMEM((S, 128), F32), pltpu.VMEM((S, 128), F32)],
        compiler_params=_cparams(("parallel",)),
    )(qr, knv, knv, kr, o, do)


CONV_TILE = 256


def _shift_down(v, n):
    row = lax.broadcasted_iota(jnp.int32, v.shape, 0)
    return jnp.where(row >= n, pltpu.roll(v, n, 0), 0.0)


def _shift_up(v, n):
    S = v.shape[0]
    row = lax.broadcasted_iota(jnp.int32, v.shape, 0)
    return jnp.where(row < S - n, pltpu.roll(v, S - n, 0), 0.0)


def _conv_specs(S, n_extra_cols):
    nt = D_MODEL // CONV_TILE
    specs = [pl.BlockSpec((S, CONV_TILE), functools.partial(lambda j, o: (0, o + j), o=part * nt))
             for part in range(3)]
    specs.append(pl.BlockSpec((8, CONV_TILE), lambda j: (0, j)))
    specs += [pl.BlockSpec((S, CONV_TILE), lambda j: (0, j)) for _ in range(n_extra_cols)]
    return specs


def _conv_fwd(bcu, w8):
    S = bcu.shape[0]

    def body(b_ref, c_ref, u_ref, w_ref, y_ref):
        cu = c_ref[...] * u_ref[...]
        z = w_ref[2:3, :] * cu + w_ref[1:2, :] * _shift_down(cu, 1) + w_ref[0:1, :] * _shift_down(cu, 2)
        y_ref[...] = (b_ref[...] * z).astype(y_ref.dtype)

    return pl.pallas_call(
        body, name="conv_fwd", grid=(D_MODEL // CONV_TILE,), in_specs=_conv_specs(S, 0),
        out_specs=pl.BlockSpec((S, CONV_TILE), lambda j: (0, j)),
        out_shape=jax.ShapeDtypeStruct((S, D_MODEL), BF16),
        compiler_params=_cparams(("parallel",)),
    )(bcu, bcu, bcu, w8)


def _conv_bwd(bcu, w8, dy):
    S = bcu.shape[0]

    def body(b_ref, c_ref, u_ref, w_ref, dy_ref, db_ref, dc_ref, du_ref, dw_ref):
        b, c, u, dyv = b_ref[...], c_ref[...], u_ref[...], dy_ref[...]
        w0, w1, w2 = w_ref[0:1, :], w_ref[1:2, :], w_ref[2:3, :]
        cu = c * u
        cu1, cu2 = _shift_down(cu, 1), _shift_down(cu, 2)
        z = w2 * cu + w1 * cu1 + w0 * cu2
        dz = dyv * b
        db_ref[...] = (dyv * z).astype(db_ref.dtype)
        dcu = w2 * dz + w1 * _shift_up(dz, 1) + w0 * _shift_up(dz, 2)
        dc_ref[...] = (dcu * u).astype(dc_ref.dtype)
        du_ref[...] = (dcu * c).astype(du_ref.dtype)
        dw_ref[...] = jnp.zeros(dw_ref.shape, F32)
        dw_ref[0:1, :] = _colsum(dz * cu2)
        dw_ref[1:2, :] = _colsum(dz * cu1)
        dw_ref[2:3, :] = _colsum(dz * cu)

    col = pl.BlockSpec((S, CONV_TILE), lambda j: (0, j))
    return pl.pallas_call(
        body, name="conv_bwd", grid=(D_MODEL // CONV_TILE,), in_specs=_conv_specs(S, 1),
        out_specs=[col, col, col, pl.BlockSpec((8, CONV_TILE), lambda j: (0, j))],
        out_shape=[jax.ShapeDtypeStruct((S, D_MODEL), BF16)] * 3 + [jax.ShapeDtypeStruct((8, D_MODEL), F32)],
        compiler_params=_cparams(("parallel",)),
    )(bcu, bcu, bcu, w8, dy)


def _adamw(w, m, v, gs, name):
    L, R, Cn = w.shape
    assert len(gs) == L
    tr = R if R <= 256 else 256
    assert R % tr == 0

    def body(w_ref, m_ref, v_ref, *rest):
        g_refs, (go_ref, d_ref, nm_ref, nv_ref) = rest[:L], rest[L:]
        layer = pl.program_id(0)
        gv = g_refs[0][...]
        for k in range(1, L):
            gv = jnp.where(layer == k, g_refs[k][...], gv)
        nm = ADAM_B1 * m_ref[...] + (1.0 - ADAM_B1) * gv
        nv = ADAM_B2 * v_ref[...] + (1.0 - ADAM_B2) * jnp.square(gv)
        m_hat = nm / (1.0 - ADAM_B1 ** ADAM_STEP)
        v_hat = nv / (1.0 - ADAM_B2 ** ADAM_STEP)
        d_ref[...] = -ADAM_LR * (m_hat / (jnp.sqrt(v_hat) + ADAM_EPS) + ADAM_WD * w_ref[...])
        go_ref[...] = gv
        nm_ref[...] = nm
        nv_ref[...] = nv

    spec = pl.BlockSpec((None, tr, Cn), lambda l, i: (l, i, 0))
    g_specs = [pl.BlockSpec((tr, Cn), functools.partial(lambda l, i, k: (jnp.where(l == k, i, 0), 0), k=k))
               for k in range(L)]
    return pl.pallas_call(
        body, name=name, grid=(L, R // tr), in_specs=[spec] * 3 + g_specs, out_specs=[spec] * 4,
        out_shape=[jax.ShapeDtypeStruct((L, R, Cn), F32)] * 4,
        compiler_params=_cparams(("arbitrary", "arbitrary")),
    )(w, m, v, *gs)


HBM_SPEC = pl.BlockSpec(memory_space=pltpu.HBM)
BOUNCE_ROWS = 256


def _place():
    return lax.axis_index("x"), lax.axis_index("y"), lax.axis_index("c")


def _other_chips(x, y):
    return [(1 - x, y), (x, 1 - y), (1 - x, 1 - y)]


def _copy_via_vmem(src, dst, buf, sems, rows):
    ch = buf.shape[1]
    n = rows // ch
    cin = lambda i: pltpu.make_async_copy(src.at[pl.ds(i * ch, ch), :], buf.at[i % 2], sems.at[i % 2])
    cout = lambda i: pltpu.make_async_copy(buf.at[i % 2], dst.at[pl.ds(i * ch, ch), :], sems.at[2 + i % 2])
    cin(0).start()
    for i in range(n):
        cin(i).wait()
        cout(i).start()
        if i + 1 < n:
            if i >= 1:
                cout(i - 1).wait()
            cin(i + 1).start()
    if n >= 2:
        cout(n - 2).wait()
    cout(n - 1).wait()


SEM_SPEC = pl.BlockSpec(memory_space=pltpu.SEMAPHORE)
ANY_SPEC = pl.BlockSpec(memory_space=pl.ANY)
VMEM_SPEC = pl.BlockSpec(memory_space=pltpu.VMEM)
EFFECT = pltpu.SideEffectType.DATAFLOW_SIDE_EFFECTING
TOKEN = (8, 128)


def _ici_start(srcs, lands, after, copies, name):
    n, nl = len(srcs), len(lands)

    def body(*refs):
        src_refs, land_refs = refs[:n], refs[n:n + nl]
        send_sems, recv_sems, token = refs[n + nl + 1], refs[n + nl + 2], refs[-1]
        x, y, c = _place()
        for k, src, dst, to in copies(src_refs, land_refs, x, y, c):
            pltpu.make_async_remote_copy(src_ref=src, dst_ref=dst, send_sem=send_sems.at[k], recv_sem=recv_sems.at[k],
                                         device_id=to, device_id_type=MESH).start()
        token[...] = jnp.zeros(TOKEN, F32)

    n_copies = 3 * n
    res = pl.pallas_call(
        body, name=name,
        out_shape=(pltpu.SemaphoreType.DMA((n_copies,)), pltpu.SemaphoreType.DMA((n_copies,)),
                   *[pltpu.HBM(s.shape, s.dtype) for s in srcs], *[pltpu.HBM(l.shape, l.dtype) for l in lands],
                   jax.ShapeDtypeStruct(TOKEN, F32)),
        in_specs=[HBM_SPEC] * (n + nl) + [ANY_SPEC],
        out_specs=(SEM_SPEC, SEM_SPEC, *[HBM_SPEC] * (n + nl), VMEM_SPEC),
        input_output_aliases={t: 2 + t for t in range(n + nl)},
        compiler_params=pltpu.CompilerParams(has_side_effects=EFFECT),
    )(*[_hbm(s) for s in srcs], *[_hbm(l) for l in lands], after)
    return res[0], res[1], list(res[2:2 + n]), list(res[2 + n:2 + n + nl]), res[-1]


def _ici_wait(handle, after, copies, name):
    send_sems, recv_sems, srcs, lands, _ = handle
    n, nl = len(srcs), len(lands)

    def body(*refs):
        src_refs, land_refs = refs[:n], refs[n:n + nl]
        send_s, recv_s = refs[n + nl], refs[n + nl + 1]
        x, y, c = _place()
        for k, src, dst, to in copies(src_refs, land_refs, x, y, c):
            cp = pltpu.make_async_remote_copy(src_ref=src, dst_ref=dst, send_sem=send_s.at[k], recv_sem=recv_s.at[k],
                                              device_id=to, device_id_type=MESH)
            cp.wait_send()
            cp.wait_recv()

    res = pl.pallas_call(
        body, name=name,
        out_shape=(*[pltpu.HBM(s.shape, s.dtype) for s in srcs], *[pltpu.HBM(l.shape, l.dtype) for l in lands]),
        in_specs=[HBM_SPEC] * (n + nl) + [SEM_SPEC, SEM_SPEC, ANY_SPEC],
        out_specs=tuple([HBM_SPEC] * (n + nl)),
        input_output_aliases={t: t for t in range(n + nl)},
        compiler_params=pltpu.CompilerParams(has_side_effects=EFFECT),
    )(*srcs, *lands, send_sems, recv_sems, after)
    return list(res[:n]), list(res[n:])


def _gather_copies(halves):
    def copies(src_refs, land_refs, x, y, c):
        q = 2 * x + y
        out = []
        for t, H in enumerate(halves):
            for j, (cx, cy) in enumerate(_other_chips(x, y)):
                out.append((3 * t + j, src_refs[t].at[pl.ds(c * H, H), :], land_refs[t].at[q, pl.ds(c * H, H), :],
                            (cx, cy, c)))
        return out
    return copies


def _gather_wait_copies(halves):
    def copies(src_refs, land_refs, x, y, c):
        out = []
        for t, H in enumerate(halves):
            for j, (cx, cy) in enumerate(_other_chips(x, y)):
                out.append((3 * t + j, src_refs[t].at[pl.ds(c * H, H), :],
                            land_refs[t].at[2 * cx + cy, pl.ds(c * H, H), :], (cx, cy, c)))
        return out
    return copies


def _gather_start(ops, after, name):
    lands = [lax.empty((N_CHIPS,) + o.shape, o.dtype) for o in ops]
    return _ici_start(ops, lands, after, _gather_copies([o.shape[0] // 2 for o in ops]), name)


def _gather_wait(handle, after, name):
    halves = [s.shape[0] // 2 for s in handle[2]]
    return _ici_wait(handle, after, _gather_wait_copies(halves), name)


def _gather_finish(ops, lands, name):
    n = len(ops)
    halves = [o.shape[0] // 2 for o in ops]
    chunk = [min(o.shape[0], BOUNCE_ROWS) for o in ops]

    def body(*refs):
        in_refs, out_refs = refs[:n], refs[2 * n:3 * n]
        send_sems, recv_sems, local_sems = refs[3 * n:3 * n + 3]
        bufs = refs[3 * n + 3:]
        x, y, c = _place()
        q = 2 * x + y
        chips = _other_chips(x, y)
        sibling = (x, y, 1 - c)

        def copy(t, j, half):
            land = out_refs[t].at[2 * chips[j][0] + chips[j][1], pl.ds(half * halves[t], halves[t]), :]
            return pltpu.make_async_remote_copy(src_ref=land, dst_ref=land, send_sem=send_sems.at[3 * t + j],
                                                recv_sem=recv_sems.at[3 * t + j], device_id=sibling,
                                                device_id_type=MESH)

        passed = [copy(t, j, c) for t in range(n) for j in range(3)]
        for cp in passed:
            cp.start()
        for t in range(n):
            _copy_via_vmem(in_refs[t], out_refs[t].at[q], bufs[t], local_sems, ops[t].shape[0])
        for t in range(n):
            for j in range(3):
                copy(t, j, 1 - c).wait_recv()
        for cp in passed:
            cp.wait_send()

    return pl.pallas_call(
        body, name=name, in_specs=[HBM_SPEC] * (2 * n), out_specs=[HBM_SPEC] * n,
        out_shape=[jax.ShapeDtypeStruct(l.shape, l.dtype) for l in lands],
        input_output_aliases={n + t: t for t in range(n)},
        scratch_shapes=[pltpu.SemaphoreType.DMA((3 * n,)), pltpu.SemaphoreType.DMA((3 * n,)),
                        pltpu.SemaphoreType.DMA((4,))]
        + [pltpu.VMEM((2, chunk[t], ops[t].shape[1]), ops[t].dtype) for t in range(n)],
        compiler_params=pltpu.CompilerParams(vmem_limit_bytes=VMEM_LIMIT),
    )(*ops, *lands)


def _swap_halves(ops, name):
    n = len(ops)

    def body(*refs):
        in_refs, out_refs, send_sems, recv_sems = refs[:n], refs[n:2 * n], refs[2 * n], refs[2 * n + 1]
        x, y, c = _place()
        cps = []
        for t in range(n):
            H = ops[t].shape[1] // 2
            cp = pltpu.make_async_remote_copy(src_ref=in_refs[t].at[:, pl.ds((1 - c) * H, H), :],
                                              dst_ref=out_refs[t], send_sem=send_sems.at[t],
                                              recv_sem=recv_sems.at[t], device_id=(x, y, 1 - c),
                                              device_id_type=MESH)
            cp.start()
            cps.append(cp)
        for cp in cps:
            cp.wait()

    return pl.pallas_call(
        body, name=name, in_specs=[HBM_SPEC] * n, out_specs=[HBM_SPEC] * n,
        out_shape=[jax.ShapeDtypeStruct((N_CHIPS, o.shape[1] // 2, o.shape[2]), o.dtype) for o in ops],
        scratch_shapes=[pltpu.SemaphoreType.DMA((n,)), pltpu.SemaphoreType.DMA((n,))],
    )(*ops)


def _sum_rows_tile(h):
    return h if h <= 512 else 512


def _pair_sum(g, t, cq, name):
    _, a, b = g.shape
    H = a // 2
    tr = _sum_rows_tile(H)

    def body(cq_ref, g_ref, t_ref, o_ref):
        o_ref[...] = (g_ref[...].astype(F32) + t_ref[...].astype(F32)).astype(o_ref.dtype)

    grid_spec = pltpu.PrefetchScalarGridSpec(
        num_scalar_prefetch=1, grid=(N_CHIPS, H // tr),
        in_specs=[pl.BlockSpec((None, None, tr, b), lambda j, i, cq_ref: (j, cq_ref[0], i, 0)),
                  pl.BlockSpec((None, tr, b), lambda j, i, cq_ref: (j, i, 0))],
        out_specs=pl.BlockSpec((None, tr, b), lambda j, i, cq_ref: (j, i, 0)))
    return pl.pallas_call(
        body, name=name, grid_spec=grid_spec, out_shape=jax.ShapeDtypeStruct(t.shape, BF16),
        compiler_params=_cparams(("parallel", "parallel")),
    )(cq, g.reshape(N_CHIPS, 2, H, b), t)


def _scatter_copies(src_refs, land_refs, x, y, c):
    out = []
    for j, (cx, cy) in enumerate(_other_chips(x, y)):
        for t in range(len(src_refs)):
            out.append((3 * t + j, src_refs[t].at[2 * cx + cy], land_refs[t].at[j], (cx, cy, c)))
    return out


def _scatter_start(ops, after, name):
    lands = [lax.empty((3,) + o.shape[1:], o.dtype) for o in ops]
    return _ici_start(ops, lands, after, _scatter_copies, name)


def _scatter_wait(handle, after, name):
    return _ici_wait(handle, after, _scatter_copies, name)


def _chip_sum(p, t, cq, name):
    _, H, b = p.shape
    tr = _sum_rows_tile(H)

    def body(cq_ref, p_ref, t_ref, o_ref):
        acc = p_ref[...].astype(F32)
        for j in range(3):
            acc = acc + t_ref[j].astype(F32)
        o_ref[...] = acc

    grid_spec = pltpu.PrefetchScalarGridSpec(
        num_scalar_prefetch=1, grid=(H // tr,),
        in_specs=[pl.BlockSpec((None, tr, b), lambda i, cq_ref: (cq_ref[1], i, 0)),
                  pl.BlockSpec((3, tr, b), lambda i, cq_ref: (0, i, 0))],
        out_specs=pl.BlockSpec((None, tr, b), lambda i, cq_ref: (cq_ref[0], i, 0)))
    out = pl.pallas_call(
        body, name=name, grid_spec=grid_spec, out_shape=jax.ShapeDtypeStruct((2, H, b), F32),
        compiler_params=_cparams(("parallel",)),
    )(cq, p, t)
    return out.reshape(2 * H, b)


def _join_halves(ops, name):
    n = len(ops)

    def body(*refs):
        out_refs, send_sems, recv_sems = refs[n:2 * n], refs[2 * n], refs[2 * n + 1]
        x, y, c = _place()
        cps = []
        for t in range(n):
            H = ops[t].shape[0] // 2
            mine = out_refs[t].at[pl.ds(c * H, H), :]
            cp = pltpu.make_async_remote_copy(src_ref=mine, dst_ref=mine, send_sem=send_sems.at[t],
                                              recv_sem=recv_sems.at[t], device_id=(x, y, 1 - c),
                                              device_id_type=MESH)
            cp.start()
            cps.append(cp)
        for t in range(n):
            H = ops[t].shape[0] // 2
            other = out_refs[t].at[pl.ds((1 - c) * H, H), :]
            pltpu.make_async_remote_copy(src_ref=other, dst_ref=other, send_sem=send_sems.at[t],
                                         recv_sem=recv_sems.at[t], device_id=(x, y, 1 - c),
                                         device_id_type=MESH).wait_recv()
        for cp in cps:
            cp.wait_send()

    return pl.pallas_call(
        body, name=name, in_specs=[HBM_SPEC] * n, out_specs=[HBM_SPEC] * n,
        out_shape=[jax.ShapeDtypeStruct(o.shape, o.dtype) for o in ops],
        input_output_aliases={t: t for t in range(n)},
        scratch_shapes=[pltpu.SemaphoreType.DMA((n,)), pltpu.SemaphoreType.DMA((n,))],
    )(*ops)


def _reduce_scatter_start(gs, cq, after, tag):
    ts = _swap_halves(gs, "rs_swap_" + tag)
    ps = [_pair_sum(g, t, cq, "rs_pair_sum") for g, t in zip(gs, ts)]
    return _scatter_start(ps, after, "rs_scatter_start_" + tag)


def _reduce_scatter_finish(handle, cq, after, tag):
    ps, rs = _scatter_wait(handle, after, "rs_scatter_wait_" + tag)
    fs = [_chip_sum(p, r, cq, "rs_chip_sum") for p, r in zip(ps, rs)]
    return _join_halves(fs, "rs_join_" + tag)


def _all_reduce_small(v):
    n = v.shape[0]

    def body(v_ref, out_ref, buf, send_sems, recv_sems):
        x, y, c = _place()
        me = 4 * x + 2 * y + c
        buf[me] = v_ref[...]
        cps = []
        for k in range(1, 8):
            peer = (x ^ (k >> 2), y ^ ((k >> 1) & 1), c ^ (k & 1))
            cp = pltpu.make_async_remote_copy(src_ref=v_ref, dst_ref=buf.at[me], send_sem=send_sems.at[k - 1],
                                              recv_sem=recv_sems.at[k - 1], device_id=peer, device_id_type=MESH)
            cp.start()
            cps.append(cp)
        for k in range(1, 8):
            px, py, pc = x ^ (k >> 2), y ^ ((k >> 1) & 1), c ^ (k & 1)
            land = buf.at[4 * px + 2 * py + pc]
            pltpu.make_async_remote_copy(src_ref=land, dst_ref=land, send_sem=send_sems.at[k - 1],
                                         recv_sem=recv_sems.at[k - 1], device_id=(px, py, pc),
                                         device_id_type=MESH).wait_recv()
        for cp in cps:
            cp.wait_send()
        acc = buf[0]
        for d in range(1, 8):
            acc = acc + buf[d]
        out_ref[...] = acc

    vm = pl.BlockSpec(memory_space=pltpu.VMEM)
    return pl.pallas_call(
        body, name="all_reduce_small", in_specs=[vm], out_specs=vm,
        out_shape=jax.ShapeDtypeStruct((n, 128), F32),
        scratch_shapes=[pltpu.VMEM((8, n, 128), F32), pltpu.SemaphoreType.DMA((7,)), pltpu.SemaphoreType.DMA((7,))],
    )(v)


SMALL_GATHER = (16, 1024)
SMALL_FULL = sum(_size(_full_shape(n)) for n in SMALL)
SMALL_FULL_ROWS = -(-SMALL_FULL // 128 // 8) * 8


def _layer_shards(w, i, q):
    kind, j = MIXER[i % 3], i // 3
    out = {n: w[n][i].astype(BF16) for n in COMMON_BIG}
    if kind == 'gla':
        win = jnp.zeros((D_MODEL, GLA_WIN), F32)
        win = lax.dynamic_update_slice(win, w['gla_w_in'][j], (0, (GLA_SHARD - GLA_WIN_STEP) * q))
        out['gla_w_in'] = win.astype(BF16)
        out['gla_w_out'] = w['gla_w_out'][j].astype(BF16)
    elif kind == 'mla':
        out['mla_w_in'] = jnp.pad(w['mla_w_in'][j], ((0, 0), (0, MLA_IN_PAD - MLA_IN))).astype(BF16)
        for n in ('mla_w_uq', 'mla_w_ukv', 'mla_w_out'):
            out[n] = w[n][j].astype(BF16)
    else:
        out['conv_w_in'] = w['conv_w_in'][j].astype(BF16)
        out['conv_w_out'] = w['conv_w_out'][j].astype(BF16)
    return out


def _rows_joined(g):
    return g.reshape(g.shape[0] * g.shape[1], g.shape[2])


def _cols_joined(g):
    return jnp.moveaxis(g, 0, 1).reshape(g.shape[1], -1)


def _layer_weights(g, i):
    kind = MIXER[i % 3]
    W = {}
    if 'mlp_w1' in g:
        W = {'w1': g['mlp_w1'], 'w2': _rows_joined(g['mlp_w2']), 'gate': _rows_joined(g['ple_w_gate']),
             'proj': g['ple_w_proj']}
    if kind == 'gla' and 'gla_w_out' in g:
        W['w_out'] = _rows_joined(g['gla_w_out'])
    if kind == 'gla' and 'gla_w_in' in g:
        parts = []
        for qq in range(N_CHIPS):
            lo = g['gla_w_in'][qq][:, :128]
            if qq > 0:
                lo = lo + g['gla_w_in'][qq - 1][:, GLA_WIN_STEP:]
            parts += [lo, g['gla_w_in'][qq][:, 128:GLA_WIN_STEP]]
        parts.append(g['gla_w_in'][N_CHIPS - 1][:, GLA_WIN_STEP:])
        W['w_in'] = jnp.concatenate(parts, axis=1)
    elif kind == 'mla':
        W['w_in'] = _rows_joined(g['mla_w_in'])
        uq = _cols_joined(g['mla_w_uq']).reshape(MLA_QR, MLA_HEADS, MLA_NOPE + MLA_ROPE)
        W['w_uq'] = jnp.pad(uq, ((0, 0), (0, 0), (0, MLA_QH - MLA_NOPE - MLA_ROPE))).reshape(MLA_QR, -1)
        ukv = _cols_joined(g['mla_w_ukv']).reshape(MLA_KVR, MLA_HEADS, 2, 128)
        W['w_ukv'] = ukv.transpose(0, 2, 1, 3).reshape(MLA_KVR, -1)
        W['w_out'] = _rows_joined(g['mla_w_out'])
    elif kind == 'conv':
        W['w_in'] = g['conv_w_in']
        W['w_out'] = _rows_joined(g['conv_w_out'])
    return W


def _pack_small_shards(w):
    flat = jnp.concatenate([w[n].reshape(-1) for n in SMALL_SHARDED])
    return jnp.pad(flat, (0, _size(SMALL_GATHER) - flat.shape[0])).reshape(SMALL_GATHER)


def _unpack_small_gathered(g):
    flat, out, off = g.reshape(N_CHIPS, -1), {}, 0
    for n in SMALL_SHARDED:
        shape, ax = WSPEC[n]
        seg = flat[:, off:off + _size(shape)].reshape((N_CHIPS,) + shape)
        out[n] = jnp.moveaxis(seg, 0, ax).reshape(_full_shape(n))
        off += _size(shape)
    return out


def _pack_small(vals):
    flat = jnp.concatenate([vals[n].reshape(-1) for n in SMALL])
    return jnp.pad(flat, (0, SMALL_FULL_ROWS * 128 - flat.shape[0])).reshape(SMALL_FULL_ROWS, 128)


def _unpack_small(packed, q):
    flat = packed.reshape(-1)
    out, off = {}, 0
    for n in SMALL:
        shape, ax = WSPEC[n]
        full = flat[off:off + _size(_full_shape(n))].reshape(_full_shape(n))
        off += _size(_full_shape(n))
        out[n] = full if ax is None else lax.dynamic_slice_in_dim(full, q * shape[ax], shape[ax], axis=ax)
    return out


def _row_shards(dw):
    return dw.reshape(N_CHIPS, dw.shape[0] // N_CHIPS, dw.shape[1])


def _col_shards(dw):
    return jnp.moveaxis(dw.reshape(dw.shape[0], N_CHIPS, -1), 1, 0)


def _row(v):
    return v.reshape(1, -1)


def _layer_fwd(i, xin, xin_b, p_i, W, sm, cosp, sinp, rest=None):
    kind, j = MIXER[i % 3], i // 3
    sv = {'xin': xin, 'xin_b': xin_b}
    if kind == 'gla':
        w_up = jnp.pad(sm['gla_w_gate_up'][j].astype(BF16), ((0, 128 - GLA_RANK), (0, 0)))
        pin = _mm(xin_b, W['w_in'], name="gla_in", tn=640, tm=FULL_ROWS)
        la = _mm(pin, w_up, name="gla_gate", K=128, tk=128, a_off=(0, (GLA_IN_PAD - 128) // 128), tn=512,
                 extras=[(_row(sm['gla_b_gate'][j]), 'n')],
                 epilogue=lambda acc, b: (_log_sigmoid(acc + b) * (1.0 / GLA_TAU),))
        o, states = _gla_fwd(pin, la)
        yb = _gla_post_fwd(o, pin, _row(sm['gla_norm_g'][j]))
        if rest is not None:
            W = {**W, **rest(yb)}
        mixed = yb
        sv.update(w_up=w_up, pin=pin, la=la, o=o, states=states, yb=yb)
    elif kind == 'mla':
        gq, gkv = sm['mla_q_norm'][j:j + 1], sm['mla_kv_norm'][j:j + 1]
        cin = _mm(xin_b, W['w_in'], name="mla_in", tn=640, tm=FULL_ROWS)
        qn, kvn, kr = _mla_norm_fwd(cin, gq, gkv, cosp, sinp)
        qr = _mla_qrope_fwd(_mm(qn, W['w_uq'], name="mla_uq"), cosp, sinp)
        knv = _mm(kvn, W['w_ukv'], name="mla_ukv", out_dtypes=(BF16,))
        o = _mla_attn_fwd(qr, knv, kr)
        ob = o.astype(BF16)
        mixed = ob
        sv.update(gq=gq, gkv=gkv, cin=cin, qn=qn, kvn=kvn, kr=kr, qr=qr, knv=knv, o=o, ob=ob)
    else:
        w8 = jnp.pad(sm['conv_w'][j], ((0, 5), (0, 0)))
        bcu = _mm(xin_b, W['w_in'], name="conv_in", tn=768, b_sh=True, tm=FULL_ROWS)
        yb = _conv_fwd(bcu, w8)
        mixed = yb
        sv.update(w8=w8, bcu=bcu, yb=yb)
    g0, b0 = _row(sm['ln_g'][i, 0]), _row(sm['ln_b'][i, 0])
    g1, b1 = _row(sm['ln_g'][i, 1]), _row(sm['ln_b'][i, 1])
    ln = dict(tm=512, tn=D_MODEL, out_dtypes=(F32, BF16, F32), epilogue=_ln_fwd_epilogue)
    x1, x1b, v0 = _mm(mixed, W['w_out'], name="mix_out_ln", extras=[(xin, 'mn'), (g0, 'n'), (b0, 'n')], **ln)
    ab = _mm(x1b, W['w1'], name="mlp_up", out_dtypes=(BF16,), b_sh=True, tm=FULL_ROWS,
             epilogue=lambda acc: (jnp.square(jnp.maximum(acc, 0.0)),))
    x2, x2b, v1 = _mm(ab, W['w2'], name="mlp_down_ln", tk=2048, extras=[(x1, 'mn'), (g1, 'n'), (b1, 'n')], **ln)
    pp = _mm(p_i, W['proj'], name="ple_proj", tn=256, b_sh=True)
    z, x3, x3b = _mm(x2b, W['gate'], name="ple_gate", out_dtypes=(F32, F32, BF16),
                     extras=[(x2, 'mn'), (pp, 'mn')],
                     epilogue=lambda acc, xv, pv: (acc,) + (xv + jax.nn.sigmoid(acc) * pv,) * 2)
    sv.update(v0=v0, x1b=x1b, ab=ab, v1=v1, x2b=x2b, pp=pp, z=z, g0=g0, g1=g1)
    return x3, x3b, sv, W


def _layer_bwd(i, dx, p_i, W, sm, sv, cosp, sinp, token, early=None):
    kind, j = MIXER[i % 3], i // 3
    big, small = {}, {}
    dpp_b, dz_b = _ple_bwd_gate(dx, sv['z'], sv['pp'], token)
    big['ple_w_proj'] = _mm(p_i, dpp_b, ta=True, name="ple_proj_dw", tn=256, out_sh=True, out_dtypes=(BF16,))
    big['ple_w_gate'] = _row_shards(_mm(sv['x2b'], dz_b, ta=True, name="dw_dd", out_dtypes=(BF16,)))
    ln = dict(tb=True, tm=512, tn=D_MODEL, out_dtypes=(F32, BF16), n_sums=2)
    (dv1, dv1b), (dg1, db1) = _mm(dz_b, W['gate'], name="ple_gate_dx_ln", epilogue=_ln_bwd_epilogue(1.0),
                                  extras=[(dx, 'mn'), (sv['v1'], 'mn'), (sv['g1'], 'n')], **ln)
    big['mlp_w2'] = _row_shards(_mm(sv['ab'], dv1b, ta=True, name="mlp_down_dw", out_dtypes=(BF16,)))
    dub = _mm(dv1b, W['w2'], tb=True, name="mlp_down_dx", out_dtypes=(BF16,), tm=FULL_ROWS,
              extras=[(sv['ab'], 'mn')], epilogue=lambda acc, a: (acc * (2.0 * jnp.sqrt(a.astype(F32))),))
    big['mlp_w1'] = _mm(sv['x1b'], dub, ta=True, name="mlp_up_dw", out_sh=True, out_dtypes=(BF16,))
    order = []
    if early is not None:
        order, big = [(early(big), 'whole')], {}
    (dv0, dv0b), (dg0, db0) = _mm(dub, W['w1'], name="mlp_up_dx_ln", b_sh=True, epilogue=_ln_bwd_epilogue(ALPHA),
                                  extras=[(dv1, 'mn'), (sv['v0'], 'mn'), (sv['g0'], 'n')] + order, **ln)
    small['ln_g'] = jnp.stack([dg0[0], dg1[0]])
    small['ln_b'] = jnp.stack([db0[0], db1[0]])
    resid = dict(tn=1024, extras=[(dv0, 'mn')], epilogue=lambda acc, r: (acc + ALPHA * r,))
    if kind == 'gla':
        big['gla_w_out'] = _row_shards(_mm(sv['yb'], dv0b, ta=True, name="dw_dd", out_dtypes=(BF16,)))
        dy = _mm(dv0b, W['w_out'], tb=True, name="dx_dd", tn=1024)
        do, dr_b, dng = _gla_post_bwd(dy, sv['o'], sv['pin'], _row(sm['gla_norm_g'][j]))
        dq_b, dk_b, dvv_b, dla = _gla_bwd(sv['pin'], sv['la'], sv['states'], do)
        dzg_b, dbg = _gla_gate_bwd(dla, sv['la'])
        dw_up = _mm(sv['pin'], dzg_b, ta=True, name="gla_gate_dw", M=128, tm=128,
                    a_off=(0, (GLA_IN_PAD - 128) // 128))
        dglr_b = _mm(dzg_b, sv['w_up'], tb=True, name="gla_gate_dx", out_dtypes=(BF16,))
        dpin_b = jnp.concatenate([dq_b, dk_b, dvv_b, dr_b, dglr_b], axis=1)
        dw_in = _mm(sv['xin_b'], dpin_b, ta=True, name="gla_in_dw", tn=640, out_dtypes=(BF16,))
        dxin = _mm(dpin_b, W['w_in'], tb=True, name="gla_in_dx", tk=640, **resid)
        big['gla_w_in'] = jnp.stack([dw_in[:, GLA_WIN_STEP * qq:GLA_WIN_STEP * qq + GLA_WIN]
                                     for qq in range(N_CHIPS)])
        small.update(gla_w_gate_up=dw_up[:GLA_RANK], gla_b_gate=dbg[0], gla_norm_g=dng[0])
    elif kind == 'mla':
        big['mla_w_out'] = _row_shards(_mm(sv['ob'], dv0b, ta=True, name="dw_dd", out_dtypes=(BF16,)))
        do = _mm(dv0b, W['w_out'], tb=True, name="dx_dd", tn=1024)
        dqr, dkn_b, dvv_b, dkr = _mla_attn_bwd(sv['qr'], sv['knv'], sv['kr'], sv['o'], do)
        dq_b = _mla_qrope_bwd(dqr, cosp, sinp)
        dw_uq = _mm(sv['qn'], dq_b, ta=True, name="mla_up_dw", out_dtypes=(BF16,))
        dqn = _mm(dq_b, W['w_uq'], tb=True, name="mla_up_dx")
        dknv_b = jnp.concatenate([dkn_b, dvv_b], axis=1)
        dw_ukv = _mm(sv['kvn'], dknv_b, ta=True, name="mla_up_dw", out_dtypes=(BF16,))
        dkvn = _mm(dknv_b, W['w_ukv'], tb=True, name="mla_up_dx")
        dcin_b, dgq, dgkv = _mla_norm_bwd(sv['cin'], dqn, dkvn, dkr, sv['gq'], sv['gkv'], cosp, sinp)
        big['mla_w_in'] = _row_shards(_mm(sv['xin_b'], dcin_b, ta=True, name="mla_in_dw", tn=640,
                                          out_dtypes=(BF16,)))
        dxin = _mm(dcin_b, W['w_in'], tb=True, name="mla_in_dx", tk=640, **resid)
        big['mla_w_uq'] = _col_shards(
            dw_uq.reshape(MLA_QR, MLA_HEADS, MLA_QH)[:, :, :MLA_NOPE + MLA_ROPE].reshape(MLA_QR, -1))
        big['mla_w_ukv'] = _col_shards(
            dw_ukv.reshape(MLA_KVR, 2, MLA_HEADS, 128).transpose(0, 2, 1, 3).reshape(MLA_KVR, -1))
        small.update(mla_q_norm=dgq[0], mla_kv_norm=dgkv[0])
    else:
        big['conv_w_out'] = _row_shards(_mm(sv['yb'], dv0b, ta=True, name="dw_dd", out_dtypes=(BF16,)))
        dy = _mm(dv0b, W['w_out'], tb=True, name="dx_dd", tn=1024)
        db_b, dc_b, du_b, dw8 = _conv_bwd(sv['bcu'], sv['w8'], dy)
        dbcu_b = jnp.concatenate([db_b, dc_b, du_b], axis=1)
        big['conv_w_in'] = _mm(sv['xin_b'], dbcu_b, ta=True, name="conv_in_dw", tn=768, out_sh=True,
                               out_dtypes=(BF16,))
        dxin = _mm(dbcu_b, W['w_in'], tb=True, name="conv_in_dx", tk=768, b_sh=True, **resid)
        small['conv_w'] = dw8[:3]
    return dxin, big, small


def _rope_tables(positions):
    inv_freq = ROPE_BASE ** (-jnp.arange(0, MLA_ROPE // 2, dtype=F32) * (2.0 / MLA_ROPE))
    ang = positions.astype(F32)[:, None] * inv_freq
    zeros = jnp.zeros((positions.shape[0], 64), F32)
    return (jnp.concatenate([jnp.cos(ang), jnp.cos(ang), zeros], axis=1),
            jnp.concatenate([jnp.sin(ang), jnp.sin(ang), zeros], axis=1))


FIRST_NEEDED = ['gla_w_in']


def _start_gathers(w, q):
    token, started = jnp.zeros(TOKEN, F32), []
    for i in range(DEPTH):
        sh = _layer_shards(w, i, q)
        groups = [list(sh)] if i > 0 else [FIRST_NEEDED, [n for n in sh if n not in FIRST_NEEDED]]
        for k, names in enumerate(groups):
            ops = [sh[n] for n in names]
            if i == 0 and k == 0:
                ops.append(_pack_small_shards(w))
            tag = "l%d%s" % (i, "ab"[k] if i == 0 else "")
            handle = _gather_start(ops, token, "ag_start_" + tag)
            token = handle[4]
            started.append((handle, names, tag))
    return started, token


def _finish_gather(entry, after):
    handle, names, tag = entry
    srcs, lands = _gather_wait(handle, after, "ag_wait_" + tag)
    got = _gather_finish(srcs, lands, "ag_finish_" + tag)
    return dict(zip(names, got)), got[-1]


def _local_shard_grad(name, g, q):
    if name == 'gla_w_in':
        return lax.dynamic_slice_in_dim(g, (GLA_SHARD - GLA_WIN_STEP) * q, GLA_SHARD, axis=1)
    if name == 'mla_w_in':
        return g[:, :MLA_IN]
    return g


def kernel(x, p, positions, gla_w_in, gla_w_gate_up, gla_b_gate, gla_norm_g, gla_w_out, mla_w_in, mla_q_norm, mla_kv_norm, mla_w_uq, mla_w_ukv, mla_w_out, conv_w_in, conv_w, conv_w_out, ln_g, ln_b, mlp_w1, mlp_w2, ple_w_gate, ple_w_proj, loss_target, m_gla_w_in, m_gla_w_gate_up, m_gla_b_gate, m_gla_norm_g, m_gla_w_out, m_mla_w_in, m_mla_q_norm, m_mla_kv_norm, m_mla_w_uq, m_mla_w_ukv, m_mla_w_out, m_conv_w_in, m_conv_w, m_conv_w_out, m_ln_g, m_ln_b, m_mlp_w1, m_mlp_w2, m_ple_w_gate, m_ple_w_proj, v_gla_w_in, v_gla_w_gate_up, v_gla_b_gate, v_gla_norm_g, v_gla_w_out, v_mla_w_in, v_mla_q_norm, v_mla_kv_norm, v_mla_w_uq, v_mla_w_ukv, v_mla_w_out, v_conv_w_in, v_conv_w, v_conv_w_out, v_ln_g, v_ln_b, v_mlp_w1, v_mlp_w2, v_ple_w_gate, v_ple_w_proj):
    args = locals()
    w = {n: args[n] for n in WNAMES}
    m = {n: args['m_' + n] for n in WNAMES}
    v = {n: args['v_' + n] for n in WNAMES}
    q = 2 * lax.axis_index("x") + lax.axis_index("y")
    cq = jnp.stack([lax.axis_index("c"), q]).astype(jnp.int32)

    cosp, sinp = _rope_tables(positions[0])
    started, after = _start_gathers(w, q)
    xin, saved, layers, sm = x[0], [], [], None
    xin_b = xin.astype(BF16)
    for i in range(DEPTH):
        got, last = _finish_gather(started[i + 1 if i else 0], after)
        rest = None
        if i == 0:
            sm = _unpack_small_gathered(last)
            sm['mla_q_norm'], sm['mla_kv_norm'] = w['mla_q_norm'], w['mla_kv_norm']
            rest = lambda after: _layer_weights(_finish_gather(started[1], after)[0], 0)
        xin, xin_b, sv, W = _layer_fwd(i, xin, xin_b, p[i, 0], _layer_weights(got, i), sm, cosp, sinp, rest)
        layers.append(W)
        saved.append(sv)
        after = xin
    dx, loss_cols = _loss_head(xin, loss_target[0])
    loss = lax.psum(jnp.sum(loss_cols[0]), ("x", "y", "c"))

    gbig = {n: [None] * WSPEC[n][0][0] for n in BIG}
    gsmall = {n: [None] * _full_shape(n)[0] for n in SMALL}
    pending = []

    def start(grads, i, tag):
        names = list(grads)
        handle = _reduce_scatter_start([grads[n] for n in names], cq, jnp.zeros(TOKEN, F32), tag)
        pending.append((handle, names, i, tag))
        return handle[4]

    def finish(above, after):
        for entry in [e for e in pending if e[2] > above]:
            pending.remove(entry)
            handle, names, i, tag = entry
            for n, g in zip(names, _reduce_scatter_finish(handle, cq, after, tag)):
                gbig[n][i if n in COMMON_BIG else i // 3] = _local_shard_grad(n, g, q)

    token = jnp.zeros(TOKEN, F32)
    for i in reversed(range(DEPTH)):
        early = (lambda grads: start(grads, 0, "l0a")) if i == 0 else None
        dx, big, small = _layer_bwd(i, dx, p[i, 0], layers[i], sm, saved[i], cosp, sinp, token, early)
        token = start(big, i, "l%d%s" % (i, "b" if i == 0 else ""))
        finish(i, dx)
        for n, g in small.items():
            gsmall[n][i if n in ('ln_g', 'ln_b') else i // 3] = g
    finish(-1, token)
    gsm = _unpack_small(_all_reduce_small(_pack_small({n: jnp.stack(g) for n, g in gsmall.items()})), q)

    grad, delta, new_m, new_v = {}, {}, {}, {}
    for n in BIG:
        grad[n], delta[n], new_m[n], new_v[n] = _adamw(w[n], m[n], v[n], gbig[n], "adamw_" + n)
    total = sum(_size(WSPEC[n][0]) for n in SMALL)
    rows = -(-total // 128 // 8) * 8

    def pack(dct):
        flat = jnp.concatenate([dct[n].reshape(-1) for n in SMALL])
        return jnp.pad(flat, (0, rows * 128 - total), constant_values=1.0).reshape(1, rows, 128)

    res = _adamw(pack(w), pack(m), pack(v), [pack(gsm)[0]], "adamw_small")
    for out, packed in zip((grad, delta, new_m, new_v), res):
        flat, off = packed.reshape(-1), 0
        for n in SMALL:
            sz = _size(WSPEC[n][0])
            out[n] = flat[off:off + sz].reshape(WSPEC[n][0])
            off += sz
    return (loss, dx[None], *[grad[n] for n in WNAMES], *[delta[n] for n in WNAMES],
            *[new_m[n] for n in WNAMES], *[new_v[n] for n in WNAMES])
```

```python
import functools

import numpy as np
import jax
import jax.numpy as jnp
from jax import lax
from jax.experimental import pallas as pl
from jax.experimental.pallas import tpu as pltpu

F32 = jnp.float32
BF16 = jnp.bfloat16
MESH = pl.DeviceIdType.MESH

D_MODEL = 1024
DEPTH = 4
CHUNK = 64
ALPHA = (2 * DEPTH) ** 0.25
LN_EPS = 1e-5
RMS_EPS = 1e-6
PLE_DIM = 256
D_FF = 4 * D_MODEL
GLA_HEADS = 4
GLA_DK = 128
GLA_DV = 256
GLA_RANK = 16
GLA_TAU = 16.0
GLA_HK = GLA_HEADS * GLA_DK
GLA_HV = GLA_HEADS * GLA_DV
GLA_IN = 2 * GLA_HK + GLA_HV + D_MODEL + GLA_RANK
GLA_IN_PAD = 2 * GLA_HK + GLA_HV + D_MODEL + 128
GLA_SHARD = GLA_IN // 4
GLA_WIN = 896
GLA_WIN_STEP = 768
MLA_HEADS = 8
MLA_NOPE = 128
MLA_ROPE = 64
MLA_V = 128
MLA_QR = 256
MLA_KVR = 256
MLA_IN = MLA_QR + MLA_KVR + MLA_ROPE
MLA_IN_PAD = MLA_QR + MLA_KVR + 128
MLA_QH = 256
ROPE_BASE = 10000.0
ADAM_LR = 0.001
ADAM_B1 = 0.9
ADAM_B2 = 0.999
ADAM_EPS = 1e-08
ADAM_WD = 0.01
ADAM_STEP = 10

VMEM_LIMIT = 48 * 1024 * 1024
FULL_ROWS = 2048
N_CHIPS = 4

WSPEC = {
    'gla_w_in': ((2, 1024, 772), 2), 'gla_w_gate_up': ((2, 16, 128), 2), 'gla_b_gate': ((2, 128), 1),
    'gla_norm_g': ((2, 64), 1), 'gla_w_out': ((2, 256, 1024), 1), 'mla_w_in': ((1, 256, 576), 1),
    'mla_q_norm': ((1, 256), None), 'mla_kv_norm': ((1, 256), None), 'mla_w_uq': ((1, 256, 384), 2),
    'mla_w_ukv': ((1, 256, 512), 2), 'mla_w_out': ((1, 256, 1024), 1), 'conv_w_in': ((1, 1024, 768), 2),
    'conv_w': ((1, 3, 256), 2), 'conv_w_out': ((1, 256, 1024), 1), 'ln_g': ((4, 2, 256), 2),
    'ln_b': ((4, 2, 256), 2), 'mlp_w1': ((4, 1024, 1024), 2), 'mlp_w2': ((4, 1024, 1024), 1),
    'ple_w_gate': ((4, 256, 1024), 1), 'ple_w_proj': ((4, 256, 256), 2),
}
WNAMES = list(WSPEC)
BIG = ['gla_w_in', 'gla_w_out', 'mla_w_in', 'mla_w_uq', 'mla_w_ukv', 'mla_w_out', 'conv_w_in', 'conv_w_out',
       'mlp_w1', 'mlp_w2', 'ple_w_gate', 'ple_w_proj']
SMALL_SHARDED = ['gla_w_gate_up', 'gla_b_gate', 'gla_norm_g', 'conv_w', 'ln_g', 'ln_b']
SMALL = SMALL_SHARDED + ['mla_q_norm', 'mla_kv_norm']
MIXER = ['gla', 'mla', 'conv']
LAYER_BIG = {'gla': ['gla_w_in', 'gla_w_out'], 'mla': ['mla_w_in', 'mla_w_uq', 'mla_w_ukv', 'mla_w_out'],
             'conv': ['conv_w_in', 'conv_w_out']}
COMMON_BIG = ['mlp_w1', 'mlp_w2', 'ple_w_gate', 'ple_w_proj']


def _size(shape):
    return int(np.prod(shape))


def _full_shape(name):
    shape, ax = WSPEC[name]
    if ax is None:
        return shape
    return tuple(s * N_CHIPS if i == ax else s for i, s in enumerate(shape))


def _cparams(sem=None):
    return pltpu.CompilerParams(dimension_semantics=sem, vmem_limit_bytes=VMEM_LIMIT)


def _out(shape, dtype):
    return pltpu.HBM(shape, dtype)


def _hbm(v):
    return pltpu.with_memory_space_constraint(v, pltpu.HBM)


def _mm(a, b, *, name, ta=False, tb=False, M=None, N=None, K=None, out_dtypes=(F32,), epilogue=None, extras=(),
        tm=1024, tn=512, tk=None, a_off=(0, 0), b_sh=False, out_sh=False, n_sums=0):
    if M is None:
        M = a.shape[1] if ta else a.shape[0]
    if K is None:
        K = a.shape[0] if ta else a.shape[1]
    if b_sh:
        kw, nq = b.shape[1], b.shape[2]
        n_b, k_b = (kw, N_CHIPS * nq) if tb else (N_CHIPS * nq, kw)
        N = n_b if N is None else N
        assert K == k_b
    elif N is None:
        N = b.shape[0] if tb else b.shape[1]
    if tk is None:
        tk = FULL_ROWS if ta else 1024
    tm, tn, tk = min(tm, M), min(tn, N), min(tk, K)
    assert M % tm == 0 and N % tn == 0 and K % tk == 0, (name, M, N, K, tm, tn, tk)
    nk = K // tk
    n_ex, n_out = len(extras), len(out_dtypes)
    assert n_sums == 0 or tn == N

    n_b = N_CHIPS if (b_sh and tb and tk == K) else 1

    def body(a_ref, *rest):
        b_refs, rest = rest[:n_b], rest[n_b:]
        ex_refs, out_refs = rest[:n_ex], rest[n_ex:n_ex + n_out]
        sum_refs = rest[n_ex + n_out:n_ex + n_out + n_sums]
        first_rows = pl.program_id(0) == 0
        dims = ((((0,) if ta else (1,)), ((1,) if tb else (0,))), ((), ()))
        if n_b == 1:
            part = lax.dot_general(a_ref[...].astype(BF16), b_refs[0][...].astype(BF16), dims,
                                   preferred_element_type=F32)
        else:
            part = sum(lax.dot_general(a_ref[:, s * nq:(s + 1) * nq].astype(BF16), b_refs[s][...].astype(BF16), dims,
                                       preferred_element_type=F32) for s in range(n_b))

        def finish(acc):
            res = (acc,) if epilogue is None else epilogue(acc, *[r[...] for r in ex_refs])
            if n_sums:
                res, sums = res

                @pl.when(first_rows)
                def _():
                    for r in sum_refs:
                        r[...] = jnp.zeros(r.shape, F32)

                for r, v in zip(sum_refs, sums):
                    r[...] += jnp.broadcast_to(v, r.shape)
            for r, v in zip(out_refs, res):
                r[...] = v.astype(r.dtype)

        if nk == 1:
            finish(part)
        else:
            acc_ref = rest[-1]
            k = pl.program_id(2)

            @pl.when(k == 0)
            def _():
                acc_ref[...] = part

            @pl.when(k > 0)
            def _():
                acc_ref[...] += part

            @pl.when(k == nk - 1)
            def _():
                finish(acc_ref[...])

    if ta:
        a_spec = pl.BlockSpec((tk, tm), lambda i, j, k: (k + a_off[0], i + a_off[1]))
    else:
        a_spec = pl.BlockSpec((tm, tk), lambda i, j, k: (i + a_off[0], k + a_off[1]))
    once = dict(pipeline_mode=pl.Buffered(1)) if (tn == N and nk == 1) else {}
    if n_b > 1:
        b_specs = [pl.BlockSpec((None, tn, nq), functools.partial(lambda i, j, k, s: (s, j, 0), s=s), **once)
                   for s in range(n_b)]
    elif b_sh and tb:
        assert nq % tk == 0
        per = nq // tk
        b_spec = pl.BlockSpec((None, tn, tk), lambda i, j, k: (k // per, j, k % per), **once)
    elif b_sh:
        assert nq % tn == 0
        per = nq // tn
        b_spec = pl.BlockSpec((None, tk, tn), lambda i, j, k: (j // per, k, j % per), **once)
    elif tb:
        b_spec = pl.BlockSpec((tn, tk), lambda i, j, k: (j, k), **once)
    else:
        b_spec = pl.BlockSpec((tk, tn), lambda i, j, k: (k, j), **once)
    if n_b == 1:
        b_specs = [b_spec]
    ex_specs = []
    for arr, kind in extras:
        if kind == 'mn':
            ex_specs.append(pl.BlockSpec((tm, tn), lambda i, j, k: (i, j)))
        elif kind == 'n':
            ex_specs.append(pl.BlockSpec((1, tn), lambda i, j, k: (0, j)))
        else:
            ex_specs.append(pl.BlockSpec(arr.shape, lambda i, j, k: (0, 0)))
    if out_sh:
        assert (N // N_CHIPS) % tn == 0
        per_o = N // N_CHIPS // tn
        o_spec = pl.BlockSpec((None, tm, tn), lambda i, j, k: (j // per_o, i, j % per_o))
        o_shape = (N_CHIPS, M, N // N_CHIPS)
    else:
        o_spec = pl.BlockSpec((tm, tn), lambda i, j, k: (i, j))
        o_shape = (M, N)
    outs = pl.pallas_call(
        body, name=name, grid=(M // tm, N // tn, nk),
        in_specs=[a_spec] + b_specs + ex_specs,
        out_specs=[o_spec for _ in out_dtypes] + [pl.BlockSpec((8, N), lambda i, j, k: (0, 0))] * n_sums,
        out_shape=[_out(o_shape, d) for d in out_dtypes] + [_out((8, N), F32)] * n_sums,
        scratch_shapes=[pltpu.VMEM((tm, tn), F32)] if nk > 1 else [],
        compiler_params=_cparams(("arbitrary" if n_sums else "parallel", "parallel", "arbitrary")),
    )(a, *[b] * n_b, *[e[0] for e in extras])
    if n_sums:
        return tuple(outs[:n_out]), tuple(outs[n_out:])
    return outs[0] if n_out == 1 else tuple(outs)


def _rowwise(fn, *, name, rows, pars=(), outs=(), accs=(), tm=256):
    S = rows[0][0].shape[0]
    tm = min(tm, S)
    assert S % tm == 0
    n_r, n_p, n_o, n_a = len(rows), len(pars), len(outs), len(accs)

    def body(*refs):
        r_refs, p_refs = refs[:n_r], refs[n_r:n_r + n_p]
        o_refs, a_refs = refs[n_r + n_p:n_r + n_p + n_o], refs[n_r + n_p + n_o:]
        o_vals, a_vals = fn([r[...] for r in r_refs], [p[...] for p in p_refs])
        for r, v in zip(o_refs, o_vals):
            r[...] = v.astype(r.dtype)
        if n_a:
            i = pl.program_id(0)

            @pl.when(i == 0)
            def _():
                for r in a_refs:
                    r[...] = jnp.zeros(r.shape, r.dtype)

            for r, v in zip(a_refs, a_vals):
                r[...] += jnp.broadcast_to(v, r.shape)

    in_specs = [pl.BlockSpec((tm, w), functools.partial(lambda i, o: (i, o), o=off)) for _, w, off in rows]
    in_specs += [pl.BlockSpec(p.shape, functools.partial(lambda i, nd: (0,) * nd, nd=p.ndim)) for p in pars]
    out_specs = [pl.BlockSpec((tm, w), lambda i: (i, 0)) for w, _ in outs]
    out_specs += [pl.BlockSpec((8, w), lambda i: (0, 0)) for w in accs]
    out_shape = [_out((S, w), d) for w, d in outs]
    out_shape += [_out((8, w), F32) for w in accs]
    res = pl.pallas_call(
        body, name=name, grid=(S // tm,), in_specs=in_specs, out_specs=out_specs, out_shape=out_shape,
        compiler_params=_cparams(("arbitrary",)),
    )(*[r[0] for r in rows], *pars)
    return tuple(res)


def _colsum(v):
    return jnp.sum(v, axis=0, keepdims=True)


def _ln_stats(v):
    mu = jnp.mean(v, axis=-1, keepdims=True)
    d = v - mu
    var = jnp.mean(d * d, axis=-1, keepdims=True)
    rstd = lax.rsqrt(var + LN_EPS)
    return d * rstd, rstd


def _ln_fwd_epilogue(h, x, g, b):
    v = ALPHA * x + h
    xhat, _ = _ln_stats(v)
    y = xhat * g + b
    return y, y, v


def _ln_bwd_epilogue(scale):
    def epilogue(acc, resid, v, g, *unused):
        dy = acc + scale * resid
        xhat, rstd = _ln_stats(v)
        dxh = dy * g
        m1 = jnp.mean(dxh, axis=-1, keepdims=True)
        m2 = jnp.mean(dxh * xhat, axis=-1, keepdims=True)
        dv = rstd * (dxh - m1 - xhat * m2)
        return (dv, dv), (_colsum(dy * xhat), _colsum(dy))
    return epilogue


def _loss_head(y, t):
    def fn(r, p):
        d = r[0] - r[1]
        return [d * (1.0 / D_MODEL)], [_colsum(d * d) * (0.5 / D_MODEL)]
    return _rowwise(fn, name="loss_head", rows=[(y, D_MODEL, 0), (t, D_MODEL, 0)], outs=[(D_MODEL, F32)],
                    accs=[D_MODEL])


def _ple_bwd_gate(dx3, z, pp, token):
    def fn(r, p):
        s = jax.nn.sigmoid(r[1])
        return [r[0] * s, r[0] * r[2] * s * (1.0 - s)], []
    return _rowwise(fn, name="ple_bwd_gate", rows=[(dx3, D_MODEL, 0), (z, D_MODEL, 0), (pp, D_MODEL, 0)],
                    pars=[token], outs=[(D_MODEL, BF16), (D_MODEL, BF16)])


N_LEVELS = 6


def _gla_consts():
    C = CHUNK
    A = np.zeros((N_LEVELS + 3, C, C), np.float32)
    masks = np.zeros((N_LEVELS + 1, C, C), np.float32)
    r = np.arange(C)[:, None]
    u = np.arange(C)[None, :]
    for l in range(N_LEVELS):
        half = C >> (l + 1)
        mid = (r // (2 * half)) * (2 * half) + half - 1
        A[l] = np.where(r > mid, (u > mid) & (u <= r), (u > r) & (u <= mid))
        masks[l] = ((r // (2 * half)) == (u // (2 * half))) & (((r // half) % 2) != ((u // half) % 2))
    masks[N_LEVELS] = (r == u)
    A[N_LEVELS] = (u <= r)
    A[N_LEVELS + 1] = (u > r)
    A[N_LEVELS + 2] = 1.0
    A = A.reshape(-1, C)
    return A, np.ascontiguousarray(A.T), masks


def _split3(v):
    hi = v.astype(BF16)
    r1 = v - hi.astype(F32)
    mid = r1.astype(BF16)
    lo = (r1 - mid.astype(F32)).astype(BF16)
    return hi, mid, lo


def _dot_exact01(a01, v):
    hi, mid, lo = _split3(v)
    f = lambda p: jnp.dot(a01, p, preferred_element_type=F32)
    return f(hi) + f(mid) + f(lo)


def _nt(a, b):
    return lax.dot_general(a, b, (((1,), (1,)), ((), ())), preferred_element_type=F32)


def _tn(a, b):
    return lax.dot_general(a, b, (((0,), (0,)), ((), ())), preferred_element_type=F32)


def _nn(a, b):
    return jnp.dot(a, b, preferred_element_type=F32)


def _gla_chunk_terms(q, k, E, m_ref):
    C = CHUNK
    scores = m_ref[N_LEVELS] * _nt(q.astype(BF16), k.astype(BF16))
    qes, kes = [], []
    for l in range(N_LEVELS):
        El = E[l * C:(l + 1) * C]
        qe, ke = (q * El).astype(BF16), (k * El).astype(BF16)
        qes.append(qe)
        kes.append(ke)
        scores = scores + m_ref[l] * _nt(qe, ke)
    return qes, kes, scores


def _head(v, h, w):
    return v[:, h * w:(h + 1) * w]


def _gla_fwd(pin, la):
    S = pin.shape[0]
    NC = S // CHUNK
    C = CHUNK
    A, _, masks = _gla_consts()

    def body(q_ref, k_ref, v_ref, la_ref, a_ref, m_ref, o_ref, st_ref, state):
        @pl.when(pl.program_id(0) == 0)
        def _():
            state[...] = jnp.zeros(state.shape, F32)

        E_all = jnp.exp(_dot_exact01(a_ref[...], la_ref[...]))
        q_all = q_ref[...] * (GLA_DK ** -0.5)
        k_all, v_all = k_ref[...], v_ref[...]
        outs = []
        for h in range(GLA_HEADS):
            q, k, E = _head(q_all, h, GLA_DK), _head(k_all, h, GLA_DK), _head(E_all, h, GLA_DK)
            _, _, scores = _gla_chunk_terms(q, k, E, m_ref)
            Eq, Ek, Ee = E[6 * C:7 * C], E[7 * C:8 * C], E[8 * C:9 * C]
            st = state[h]
            st_ref[h] = st
            vb = _head(v_all, h, GLA_DV).astype(BF16)
            outs.append(_nn(scores.astype(BF16), vb) + _nt((q * Eq).astype(BF16), st.astype(BF16)))
            state[h] = st * jnp.concatenate([Ee] * (GLA_DV // C), axis=0) + _tn(vb, (k * Ek).astype(BF16))
        o_ref[...] = jnp.concatenate(outs, axis=1)

    return pl.pallas_call(
        body, name="gla_fwd", grid=(NC,),
        in_specs=[pl.BlockSpec((C, GLA_HK), lambda c: (c, 0)),
                  pl.BlockSpec((C, GLA_HK), lambda c: (c, 1)),
                  pl.BlockSpec((C, GLA_HV), lambda c: (c, 2 * GLA_HK // GLA_HV)),
                  pl.BlockSpec((C, GLA_HK), lambda c: (c, 0)),
                  pl.BlockSpec(A.shape, lambda c: (0, 0)),
                  pl.BlockSpec(masks.shape, lambda c: (0, 0, 0))],
        out_specs=[pl.BlockSpec((C, GLA_HV), lambda c: (c, 0)),
                   pl.BlockSpec((GLA_HEADS, None, GLA_DV, GLA_DK), lambda c: (0, c, 0, 0))],
        out_shape=[_out((S, GLA_HV), F32), _out((GLA_HEADS, NC, GLA_DV, GLA_DK), F32)],
        scratch_shapes=[pltpu.VMEM((GLA_HEADS, GLA_DV, GLA_DK), F32)],
        compiler_params=_cparams(("arbitrary",)),
    )(pin, pin, pin, la, jnp.asarray(A, BF16), jnp.asarray(masks))


def _gla_bwd(pin, la, states, do):
    S = pin.shape[0]
    NC = S // CHUNK
    C = CHUNK
    A, AT, masks = _gla_consts()
    scale = GLA_DK ** -0.5

    def body(q_ref, k_ref, v_ref, la_ref, st_ref, do_ref, a_ref, at_ref, m_ref,
             dq_ref, dk_ref, dv_ref, dla_ref, dstate):
        @pl.when(pl.program_id(0) == 0)
        def _():
            dstate[...] = jnp.zeros(dstate.shape, F32)

        E_all = jnp.exp(_dot_exact01(a_ref[...], la_ref[...]))
        q_all = q_ref[...] * scale
        k_all, v_all, do_all = k_ref[...], v_ref[...], do_ref[...]
        dqs, dks, dvs, dXs = [], [], [], []
        for h in range(GLA_HEADS):
            q, k, E = _head(q_all, h, GLA_DK), _head(k_all, h, GLA_DK), _head(E_all, h, GLA_DK)
            qes, kes, scores = _gla_chunk_terms(q, k, E, m_ref)
            Eq, Ek, Ee = E[6 * C:7 * C], E[7 * C:8 * C], E[8 * C:9 * C]
            st, dst = st_ref[h], dstate[h]
            dob, vb = _head(do_all, h, GLA_DV).astype(BF16), _head(v_all, h, GLA_DV).astype(BF16)
            dstb = dst.astype(BF16)
            qEq, kEk = (q * Eq).astype(BF16), (k * Ek).astype(BF16)
            dsc = _nt(dob, vb)
            dvs.append(_tn(scores.astype(BF16), dob) + _nt(kEk, dstb))
            dqEq = _nn(dob, st.astype(BF16))
            dkEk = _nn(vb, dstb)
            Gd = (m_ref[N_LEVELS] * dsc).astype(BF16)
            dq = _nn(Gd, k.astype(BF16)) + dqEq * Eq
            dk = _tn(Gd, q.astype(BF16)) + dkEk * Ek
            dX = []
            for l in range(N_LEVELS):
                El = E[l * C:(l + 1) * C]
                G = (m_ref[l] * dsc).astype(BF16)
                dqe, dke = _nn(G, kes[l]), _tn(G, qes[l])
                dq = dq + dqe * El
                dk = dk + dke * El
                dX.append((dqe * q + dke * k) * El)
            dX.append(dqEq * q * Eq)
            dX.append(dkEk * k * Ek)
            prod = dst * st
            dEe = prod[0:C]
            for i in range(1, GLA_DV // C):
                dEe = dEe + prod[i * C:(i + 1) * C]
            dX.append(dEe * Ee)
            dXs.append(jnp.concatenate(dX, axis=0))
            dqs.append(dq * scale)
            dks.append(dk)
            dstate[h] = dst * jnp.concatenate([Ee] * (GLA_DV // C), axis=0) + _tn(dob, qEq)
        dla_ref[...] = _dot_exact01(at_ref[...], jnp.concatenate(dXs, axis=1))
        dq_ref[...] = jnp.concatenate(dqs, axis=1).astype(dq_ref.dtype)
        dk_ref[...] = jnp.concatenate(dks, axis=1).astype(dk_ref.dtype)
        dv_ref[...] = jnp.concatenate(dvs, axis=1).astype(dv_ref.dtype)

    rc = lambda c: NC - 1 - c
    return pl.pallas_call(
        body, name="gla_bwd", grid=(NC,),
        in_specs=[pl.BlockSpec((C, GLA_HK), lambda c: (rc(c), 0)),
                  pl.BlockSpec((C, GLA_HK), lambda c: (rc(c), 1)),
                  pl.BlockSpec((C, GLA_HV), lambda c: (rc(c), 2 * GLA_HK // GLA_HV)),
                  pl.BlockSpec((C, GLA_HK), lambda c: (rc(c), 0)),
                  pl.BlockSpec((GLA_HEADS, None, GLA_DV, GLA_DK), lambda c: (0, rc(c), 0, 0)),
                  pl.BlockSpec((C, GLA_HV), lambda c: (rc(c), 0)),
                  pl.BlockSpec(A.shape, lambda c: (0, 0)),
                  pl.BlockSpec(AT.shape, lambda c: (0, 0)),
                  pl.BlockSpec(masks.shape, lambda c: (0, 0, 0))],
        out_specs=[pl.BlockSpec((C, GLA_HK), lambda c: (rc(c), 0)),
                   pl.BlockSpec((C, GLA_HK), lambda c: (rc(c), 0)),
                   pl.BlockSpec((C, GLA_HV), lambda c: (rc(c), 0)),
                   pl.BlockSpec((C, GLA_HK), lambda c: (rc(c), 0))],
        out_shape=[_out((S, GLA_HK), BF16), _out((S, GLA_HK), BF16), _out((S, GLA_HV), BF16),
                   _out((S, GLA_HK), F32)],
        scratch_shapes=[pltpu.VMEM((GLA_HEADS, GLA_DV, GLA_DK), F32)],
        compiler_params=_cparams(("arbitrary",)),
    )(pin, pin, pin, la, states, do, jnp.asarray(A, BF16), jnp.asarray(AT, BF16), jnp.asarray(masks))


def _gla_post_fwd(o, pin, g):
    def fn(r, p):
        ov, rv = r
        ys = []
        for h in range(GLA_HEADS):
            oh = ov[:, h * GLA_DV:(h + 1) * GLA_DV]
            rh = rv[:, h * GLA_DV:(h + 1) * GLA_DV]
            rs = lax.rsqrt(jnp.mean(oh * oh, axis=-1, keepdims=True) + RMS_EPS)
            ys.append(oh * rs * p[0] * (rh * jax.nn.sigmoid(rh)))
        return [jnp.concatenate(ys, axis=1)], []
    return _rowwise(fn, name="gla_post_fwd", rows=[(o, GLA_HV, 0), (pin, GLA_HV, (2 * GLA_HK + GLA_HV) // GLA_HV)],
                    pars=[g], outs=[(GLA_HV, BF16)])[0]


def _gla_post_bwd(dy, o, pin, g):
    def fn(r, p):
        dyv, ov, rv = r
        dos, drs, dg = [], [], 0.0
        for h in range(GLA_HEADS):
            sl = slice(h * GLA_DV, (h + 1) * GLA_DV)
            oh, rh, dyh = ov[:, sl], rv[:, sl], dyv[:, sl]
            rs = lax.rsqrt(jnp.mean(oh * oh, axis=-1, keepdims=True) + RMS_EPS)
            xh = oh * rs
            sg = jax.nn.sigmoid(rh)
            d_on = dyh * (rh * sg)
            drs.append(dyh * (xh * p[0]) * (sg * (1.0 + rh * (1.0 - sg))))
            dg = dg + _colsum(d_on * xh)
            dxh = d_on * p[0]
            dos.append(rs * (dxh - xh * jnp.mean(dxh * xh, axis=-1, keepdims=True)))
        return [jnp.concatenate(dos, axis=1), jnp.concatenate(drs, axis=1)], [dg]
    return _rowwise(fn, name="gla_post_bwd",
                    rows=[(dy, GLA_HV, 0), (o, GLA_HV, 0), (pin, GLA_HV, (2 * GLA_HK + GLA_HV) // GLA_HV)],
                    pars=[g], outs=[(GLA_HV, F32), (GLA_HV, BF16)], accs=[GLA_DV])


def _gla_gate_bwd(dla, la):
    def fn(r, p):
        dz = r[0] * (1.0 / GLA_TAU) * (1.0 - jnp.exp(GLA_TAU * r[1]))
        return [dz], [_colsum(dz)]
    return _rowwise(fn, name="gla_gate_bwd", rows=[(dla, GLA_HK, 0), (la, GLA_HK, 0)], outs=[(GLA_HK, BF16)],
                    accs=[GLA_HK])


def _log_sigmoid(z):
    return jnp.minimum(z, 0.0) - jnp.log(1.0 + jnp.exp(-jnp.abs(z)))


def _rot_half(v):
    lane = lax.broadcasted_iota(jnp.int32, v.shape, 1)
    return jnp.where(lane < 32, -pltpu.roll(v, 96, 1), jnp.where(lane < 64, pltpu.roll(v, 32, 1), 0.0))


def _rms(v):
    rs = lax.rsqrt(jnp.mean(v * v, axis=-1, keepdims=True) + RMS_EPS)
    return v * rs, rs


def _mla_norm_fwd(cin, gq, gkv, cosp, sinp):
    def fn(r, p):
        cv, cs, sn = r
        qn, _ = _rms(cv[:, :MLA_QR])
        kvn, _ = _rms(cv[:, MLA_QR:MLA_QR + MLA_KVR])
        kr = cv[:, MLA_QR + MLA_KVR:]
        return [qn * p[0], kvn * p[1], kr * cs + _rot_half(kr) * sn], []
    return _rowwise(fn, name="mla_norm_fwd", rows=[(cin, MLA_IN_PAD, 0), (cosp, 128, 0), (sinp, 128, 0)],
                    pars=[gq, gkv], outs=[(MLA_QR, BF16), (MLA_KVR, BF16), (128, BF16)])


def _mla_qrope_fwd(q, cosp, sinp):
    scale = (MLA_NOPE + MLA_ROPE) ** -0.5

    def fn(r, p):
        qv, cs, sn = r
        parts = []
        for h in range(MLA_HEADS):
            parts.append(qv[:, h * MLA_QH:h * MLA_QH + 128] * scale)
            rp = qv[:, h * MLA_QH + 128:(h + 1) * MLA_QH]
            parts.append((rp * cs + _rot_half(rp) * sn) * scale)
        return [jnp.concatenate(parts, axis=1)], []
    W = MLA_HEADS * MLA_QH
    return _rowwise(fn, name="mla_qrope_fwd", rows=[(q, W, 0), (cosp, 128, 0), (sinp, 128, 0)],
                    outs=[(W, BF16)])[0]


def _mla_qrope_bwd(dq, cosp, sinp):
    scale = (MLA_NOPE + MLA_ROPE) ** -0.5

    def fn(r, p):
        dv, cs, sn = r
        parts = []
        for h in range(MLA_HEADS):
            parts.append(dv[:, h * MLA_QH:h * MLA_QH + 128] * scale)
            rp = dv[:, h * MLA_QH + 128:(h + 1) * MLA_QH]
            parts.append((rp * cs - _rot_half(rp) * sn) * scale)
        return [jnp.concatenate(parts, axis=1)], []
    W = MLA_HEADS * MLA_QH
    return _rowwise(fn, name="mla_qrope_bwd", rows=[(dq, W, 0), (cosp, 128, 0), (sinp, 128, 0)],
                    outs=[(W, BF16)])[0]


def _mla_norm_bwd(cin, dqn, dkvn, dkr, gq, gkv, cosp, sinp):
    def fn(r, p):
        cv, dq_, dkv_, dkr_, cs, sn = r
        outs, accs = [], []
        for (lo, hi), dn, g in (((0, MLA_QR), dq_, p[0]), ((MLA_QR, MLA_QR + MLA_KVR), dkv_, p[1])):
            xh, rs = _rms(cv[:, lo:hi])
            dxh = dn * g
            outs.append(rs * (dxh - xh * jnp.mean(dxh * xh, axis=-1, keepdims=True)))
            accs.append(_colsum(dn * xh))
        dk = dkr_[:, 0:128]
        for h in range(1, MLA_HEADS):
            dk = dk + dkr_[:, h * 128:(h + 1) * 128]
        outs.append(dk * cs - _rot_half(dk) * sn)
        return [jnp.concatenate(outs, axis=1)], accs
    return _rowwise(fn, name="mla_norm_bwd",
                    rows=[(cin, MLA_IN_PAD, 0), (dqn, MLA_QR, 0), (dkvn, MLA_KVR, 0), (dkr, MLA_HEADS * 128, 0),
                          (cosp, 128, 0), (sinp, 128, 0)],
                    pars=[gq, gkv], outs=[(MLA_IN_PAD, BF16)], accs=[MLA_QR, MLA_KVR])


def _mla_probs(q, k, i, tq):
    s = _nt(q, k)
    row = (i * tq + lax.broadcasted_iota(jnp.int32, s.shape, 0)) // CHUNK
    col = lax.broadcasted_iota(jnp.int32, s.shape, 1) // CHUNK
    s = jnp.where(col <= row, s, -jnp.inf)
    e = jnp.exp(s - jnp.max(s, axis=-1, keepdims=True))
    return e / jnp.sum(e, axis=-1, keepdims=True)


def _mla_attn_fwd(qr, knv, kr, tq=256):
    S = qr.shape[0]
    tq = min(tq, S)

    def body(q_ref, kn_ref, v_ref, kr_ref, o_ref, k_cat):
        k_cat[:, :128] = kn_ref[...]
        k_cat[:, 128:] = kr_ref[...]
        for i in range(S // tq):
            rows, keys = pl.ds(i * tq, tq), pl.ds(0, (i + 1) * tq)
            pr = _mla_probs(q_ref[rows, :], k_cat[keys, :], i, tq)
            o_ref[rows, :] = _nn(pr.astype(BF16), v_ref[keys, :])

    return pl.pallas_call(
        body, name="mla_attn_fwd", grid=(MLA_HEADS,),
        in_specs=[pl.BlockSpec((S, MLA_QH), lambda h: (0, h)),
                  pl.BlockSpec((S, 128), lambda h: (0, h)),
                  pl.BlockSpec((S, 128), lambda h: (0, MLA_HEADS + h)),
                  pl.BlockSpec((S, 128), lambda h: (0, 0))],
        out_specs=pl.BlockSpec((S, 128), lambda h: (0, h)),
        out_shape=_out((S, MLA_HEADS * MLA_V), F32),
        scratch_shapes=[pltpu.VMEM((S, MLA_QH), BF16)],
        compiler_params=_cparams(("parallel",)),
    )(qr, knv, knv, kr)


def _mla_attn_bwd(qr, knv, kr, o, do, tq=256):
    S = qr.shape[0]
    tq = min(tq, S)
    W = MLA_HEADS * 128

    def body(q_ref, kn_ref, v_ref, kr_ref, o_ref, do_ref, dq_ref, dkn_ref, dv_ref, dkr_ref, k_cat, dk_acc, dv_acc):
        k_cat[:, :128] = kn_ref[...]
        k_cat[:, 128:] = kr_ref[...]
        dk_acc[...] = jnp.zeros(dk_acc.shape, F32)
        dv_acc[...] = jnp.zeros(dv_acc.shape, F32)
        for i in range(S // tq):
            rows, keys = pl.ds(i * tq, tq), pl.ds(0, (i + 1) * tq)
            q, k, v = q_ref[rows, :], k_cat[keys, :], v_ref[keys, :]
            pr = _mla_probs(q, k, i, tq)
            dov = do_ref[rows, :]
            delta = jnp.sum(dov * o_ref[rows, :], axis=-1, keepdims=True)
            dob = dov.astype(BF16)
            ds = (pr * (_nt(dob, v) - delta)).astype(BF16)
            dq_ref[rows, :] = _nn(ds, k)
            dk_acc[keys, :] += _tn(ds, q)
            dv_acc[keys, :] += _tn(pr.astype(BF16), dob)
        dkn_ref[...] = dk_acc[:, :128].astype(dkn_ref.dtype)
        dkr_ref[...] = dk_acc[:, 128:]
        dv_ref[...] = dv_acc[...].astype(dv_ref.dtype)

    head = lambda w: pl.BlockSpec((S, w), lambda h: (0, h))
    return pl.pallas_call(
        body, name="mla_attn_bwd", grid=(MLA_HEADS,),
        in_specs=[head(MLA_QH), head(128), pl.BlockSpec((S, 128), lambda h: (0, MLA_HEADS + h)),
                  pl.BlockSpec((S, 128), lambda h: (0, 0)), head(128), head(128)],
        out_specs=[head(MLA_QH), head(128), head(128), head(128)],
        out_shape=[_out((S, MLA_HEADS * MLA_QH), F32), _out((S, W), BF16), _out((S, W), BF16), _out((S, W), F32)],
        scratch_shapes=[pltpu.VMEM((S, MLA_QH), BF16), pltpu.VMEM((S, MLA_QH), F32), pltpu.VMEM((S, 128), F32)],
        compiler_params=_cparams(("parallel",)),
    )(qr, knv, knv, kr, o, do)


CONV_TILE = 256


def _shift_down(v, n):
    row = lax.broadcasted_iota(jnp.int32, v.shape, 0)
    return jnp.where(row >= n, pltpu.roll(v, n, 0), 0.0)


def _shift_up(v, n):
    S = v.shape[0]
    row = lax.broadcasted_iota(jnp.int32, v.shape, 0)
    return jnp.where(row < S - n, pltpu.roll(v, S - n, 0), 0.0)


def _conv_specs(S, n_extra_cols):
    nt = D_MODEL // CONV_TILE
    specs = [pl.BlockSpec((S, CONV_TILE), functools.partial(lambda j, o: (0, o + j), o=part * nt))
             for part in range(3)]
    specs.append(pl.BlockSpec((8, CONV_TILE), lambda j: (0, j)))
    specs += [pl.BlockSpec((S, CONV_TILE), lambda j: (0, j)) for _ in range(n_extra_cols)]
    return specs


def _conv_fwd(bcu, w8):
    S = bcu.shape[0]

    def body(b_ref, c_ref, u_ref, w_ref, y_ref):
        cu = c_ref[...] * u_ref[...]
        z = w_ref[2:3, :] * cu + w_ref[1:2, :] * _shift_down(cu, 1) + w_ref[0:1, :] * _shift_down(cu, 2)
        y_ref[...] = (b_ref[...] * z).astype(y_ref.dtype)

    return pl.pallas_call(
        body, name="conv_fwd", grid=(D_MODEL // CONV_TILE,), in_specs=_conv_specs(S, 0),
        out_specs=pl.BlockSpec((S, CONV_TILE), lambda j: (0, j)),
        out_shape=_out((S, D_MODEL), BF16),
        compiler_params=_cparams(("parallel",)),
    )(bcu, bcu, bcu, w8)


def _conv_bwd(bcu, w8, dy):
    S = bcu.shape[0]

    def body(b_ref, c_ref, u_ref, w_ref, dy_ref, db_ref, dc_ref, du_ref, dw_ref):
        b, c, u, dyv = b_ref[...], c_ref[...], u_ref[...], dy_ref[...]
        w0, w1, w2 = w_ref[0:1, :], w_ref[1:2, :], w_ref[2:3, :]
        cu = c * u
        cu1, cu2 = _shift_down(cu, 1), _shift_down(cu, 2)
        z = w2 * cu + w1 * cu1 + w0 * cu2
        dz = dyv * b
        db_ref[...] = (dyv * z).astype(db_ref.dtype)
        dcu = w2 * dz + w1 * _shift_up(dz, 1) + w0 * _shift_up(dz, 2)
        dc_ref[...] = (dcu * u).astype(dc_ref.dtype)
        du_ref[...] = (dcu * c).astype(du_ref.dtype)
        dw_ref[...] = jnp.zeros(dw_ref.shape, F32)
        dw_ref[0:1, :] = _colsum(dz * cu2)
        dw_ref[1:2, :] = _colsum(dz * cu1)
        dw_ref[2:3, :] = _colsum(dz * cu)

    col = pl.BlockSpec((S, CONV_TILE), lambda j: (0, j))
    return pl.pallas_call(
        body, name="conv_bwd", grid=(D_MODEL // CONV_TILE,), in_specs=_conv_specs(S, 1),
        out_specs=[col, col, col, pl.BlockSpec((8, CONV_TILE), lambda j: (0, j))],
        out_shape=[_out((S, D_MODEL), BF16)] * 3 + [_out((8, D_MODEL), F32)],
        compiler_params=_cparams(("parallel",)),
    )(bcu, bcu, bcu, w8, dy)


def _adamw(w, m, v, gs, name):
    L, R, Cn = w.shape
    assert len(gs) == L
    tr = R if R <= 256 else 256
    assert R % tr == 0

    def body(w_ref, m_ref, v_ref, *rest):
        g_refs, (go_ref, d_ref, nm_ref, nv_ref) = rest[:L], rest[L:]
        layer = pl.program_id(0)
        gv = g_refs[0][...]
        for k in range(1, L):
            gv = jnp.where(layer == k, g_refs[k][...], gv)
        nm = ADAM_B1 * m_ref[...] + (1.0 - ADAM_B1) * gv
        nv = ADAM_B2 * v_ref[...] + (1.0 - ADAM_B2) * jnp.square(gv)
        m_hat = nm / (1.0 - ADAM_B1 ** ADAM_STEP)
        v_hat = nv / (1.0 - ADAM_B2 ** ADAM_STEP)
        d_ref[...] = -ADAM_LR * (m_hat / (jnp.sqrt(v_hat) + ADAM_EPS) + ADAM_WD * w_ref[...])
        go_ref[...] = gv
        nm_ref[...] = nm
        nv_ref[...] = nv

    spec = pl.BlockSpec((None, tr, Cn), lambda l, i: (l, i, 0))
    g_specs = [pl.BlockSpec((tr, Cn), functools.partial(lambda l, i, k: (jnp.where(l == k, i, 0), 0), k=k))
               for k in range(L)]
    return pl.pallas_call(
        body, name=name, grid=(L, R // tr), in_specs=[spec] * 3 + g_specs, out_specs=[spec] * 4,
        out_shape=[jax.ShapeDtypeStruct((L, R, Cn), F32)] * 4,
        compiler_params=_cparams(("arbitrary", "arbitrary")),
    )(w, m, v, *gs)


HBM_SPEC = pl.BlockSpec(memory_space=pltpu.HBM)
BOUNCE_ROWS = 256


def _place():
    return lax.axis_index("x"), lax.axis_index("y"), lax.axis_index("c")


def _other_chips(x, y):
    return [(1 - x, y), (x, 1 - y), (1 - x, 1 - y)]


def _copy_via_vmem(src, dst, buf, sems, rows):
    ch = buf.shape[1]
    n = rows // ch
    cin = lambda i: pltpu.make_async_copy(src.at[pl.ds(i * ch, ch), :], buf.at[i % 2], sems.at[i % 2])
    cout = lambda i: pltpu.make_async_copy(buf.at[i % 2], dst.at[pl.ds(i * ch, ch), :], sems.at[2 + i % 2])
    cin(0).start()
    for i in range(n):
        cin(i).wait()
        cout(i).start()
        if i + 1 < n:
            if i >= 1:
                cout(i - 1).wait()
            cin(i + 1).start()
    if n >= 2:
        cout(n - 2).wait()
    cout(n - 1).wait()


SEM_SPEC = pl.BlockSpec(memory_space=pltpu.SEMAPHORE)
ANY_SPEC = pl.BlockSpec(memory_space=pl.ANY)
VMEM_SPEC = pl.BlockSpec(memory_space=pltpu.VMEM)
EFFECT = pltpu.SideEffectType.DATAFLOW_SIDE_EFFECTING
TOKEN = (8, 128)


def _ici_start(srcs, lands, after, copies, name):
    n, nl = len(srcs), len(lands)

    def body(*refs):
        src_refs, land_refs = refs[:n], refs[n:n + nl]
        send_sems, recv_sems, token = refs[n + nl + 1], refs[n + nl + 2], refs[-1]
        x, y, c = _place()
        for k, src, dst, to in copies(src_refs, land_refs, x, y, c):
            pltpu.make_async_remote_copy(src_ref=src, dst_ref=dst, send_sem=send_sems.at[k], recv_sem=recv_sems.at[k],
                                         device_id=to, device_id_type=MESH).start()
        token[...] = jnp.zeros(TOKEN, F32)

    n_copies = 3 * n
    res = pl.pallas_call(
        body, name=name,
        out_shape=(pltpu.SemaphoreType.DMA((n_copies,)), pltpu.SemaphoreType.DMA((n_copies,)),
                   *[pltpu.HBM(s.shape, s.dtype) for s in srcs], *[pltpu.HBM(l.shape, l.dtype) for l in lands],
                   jax.ShapeDtypeStruct(TOKEN, F32)),
        in_specs=[HBM_SPEC] * (n + nl) + [ANY_SPEC],
        out_specs=(SEM_SPEC, SEM_SPEC, *[HBM_SPEC] * (n + nl), VMEM_SPEC),
        input_output_aliases={t: 2 + t for t in range(n + nl)},
        compiler_params=pltpu.CompilerParams(has_side_effects=EFFECT),
    )(*[_hbm(s) for s in srcs], *[_hbm(l) for l in lands], after)
    return res[0], res[1], list(res[2:2 + n]), list(res[2 + n:2 + n + nl]), res[-1]


def _ici_wait(handle, after, copies, name):
    send_sems, recv_sems, srcs, lands, _ = handle
    n, nl = len(srcs), len(lands)

    def body(*refs):
        src_refs, land_refs = refs[:n], refs[n:n + nl]
        send_s, recv_s = refs[n + nl], refs[n + nl + 1]
        x, y, c = _place()
        for k, src, dst, to in copies(src_refs, land_refs, x, y, c):
            cp = pltpu.make_async_remote_copy(src_ref=src, dst_ref=dst, send_sem=send_s.at[k], recv_sem=recv_s.at[k],
                                              device_id=to, device_id_type=MESH)
            cp.wait_send()
            cp.wait_recv()

    res = pl.pallas_call(
        body, name=name,
        out_shape=(*[pltpu.HBM(s.shape, s.dtype) for s in srcs], *[pltpu.HBM(l.shape, l.dtype) for l in lands]),
        in_specs=[HBM_SPEC] * (n + nl) + [SEM_SPEC, SEM_SPEC, ANY_SPEC],
        out_specs=tuple([HBM_SPEC] * (n + nl)),
        input_output_aliases={t: t for t in range(n + nl)},
        compiler_params=pltpu.CompilerParams(has_side_effects=EFFECT),
    )(*srcs, *lands, send_sems, recv_sems, after)
    return list(res[:n]), list(res[n:])


def _gather_copies(halves):
    def copies(src_refs, land_refs, x, y, c):
        q = 2 * x + y
        out = []
        for t, H in enumerate(halves):
            for j, (cx, cy) in enumerate(_other_chips(x, y)):
                out.append((3 * t + j, src_refs[t].at[pl.ds(c * H, H), :], land_refs[t].at[q, pl.ds(c * H, H), :],
                            (cx, cy, c)))
        return out
    return copies


def _gather_wait_copies(halves):
    def copies(src_refs, land_refs, x, y, c):
        out = []
        for t, H in enumerate(halves):
            for j, (cx, cy) in enumerate(_other_chips(x, y)):
                out.append((3 * t + j, src_refs[t].at[pl.ds(c * H, H), :],
                            land_refs[t].at[2 * cx + cy, pl.ds(c * H, H), :], (cx, cy, c)))
        return out
    return copies


def _gather_start(ops, after, name):
    lands = [lax.empty((N_CHIPS,) + o.shape, o.dtype) for o in ops]
    return _ici_start(ops, lands, after, _gather_copies([o.shape[0] // 2 for o in ops]), name)


def _gather_wait(handle, after, name):
    halves = [s.shape[0] // 2 for s in handle[2]]
    return _ici_wait(handle, after, _gather_wait_copies(halves), name)


def _gather_finish(ops, lands, name):
    n = len(ops)
    halves = [o.shape[0] // 2 for o in ops]
    chunk = [min(o.shape[0], BOUNCE_ROWS) for o in ops]

    def body(*refs):
        in_refs, out_refs = refs[:n], refs[2 * n:3 * n]
        send_sems, recv_sems, local_sems = refs[3 * n:3 * n + 3]
        bufs = refs[3 * n + 3:]
        x, y, c = _place()
        q = 2 * x + y
        chips = _other_chips(x, y)
        sibling = (x, y, 1 - c)

        def copy(t, j, half):
            land = out_refs[t].at[2 * chips[j][0] + chips[j][1], pl.ds(half * halves[t], halves[t]), :]
            return pltpu.make_async_remote_copy(src_ref=land, dst_ref=land, send_sem=send_sems.at[3 * t + j],
                                                recv_sem=recv_sems.at[3 * t + j], device_id=sibling,
                                                device_id_type=MESH)

        passed = [copy(t, j, c) for t in range(n) for j in range(3)]
        for cp in passed:
            cp.start()
        for t in range(n):
            _copy_via_vmem(in_refs[t], out_refs[t].at[q], bufs[t], local_sems, ops[t].shape[0])
        for t in range(n):
            for j in range(3):
                copy(t, j, 1 - c).wait_recv()
        for cp in passed:
            cp.wait_send()

    return pl.pallas_call(
        body, name=name, in_specs=[HBM_SPEC] * (2 * n), out_specs=[HBM_SPEC] * n,
        out_shape=[jax.ShapeDtypeStruct(l.shape, l.dtype) for l in lands],
        input_output_aliases={n + t: t for t in range(n)},
        scratch_shapes=[pltpu.SemaphoreType.DMA((3 * n,)), pltpu.SemaphoreType.DMA((3 * n,)),
                        pltpu.SemaphoreType.DMA((4,))]
        + [pltpu.VMEM((2, chunk[t], ops[t].shape[1]), ops[t].dtype) for t in range(n)],
        compiler_params=pltpu.CompilerParams(vmem_limit_bytes=VMEM_LIMIT),
    )(*ops, *lands)


def _swap_halves(ops, name):
    n = len(ops)

    def body(*refs):
        in_refs, out_refs, send_sems, recv_sems = refs[:n], refs[n:2 * n], refs[2 * n], refs[2 * n + 1]
        x, y, c = _place()
        cps = []
        for t in range(n):
            H = ops[t].shape[1] // 2
            cp = pltpu.make_async_remote_copy(src_ref=in_refs[t].at[:, pl.ds((1 - c) * H, H), :],
                                              dst_ref=out_refs[t], send_sem=send_sems.at[t],
                                              recv_sem=recv_sems.at[t], device_id=(x, y, 1 - c),
                                              device_id_type=MESH)
            cp.start()
            cps.append(cp)
        for cp in cps:
            cp.wait()

    return pl.pallas_call(
        body, name=name, in_specs=[HBM_SPEC] * n, out_specs=[HBM_SPEC] * n,
        out_shape=[jax.ShapeDtypeStruct((N_CHIPS, o.shape[1] // 2, o.shape[2]), o.dtype) for o in ops],
        scratch_shapes=[pltpu.SemaphoreType.DMA((n,)), pltpu.SemaphoreType.DMA((n,))],
    )(*ops)


def _sum_rows_tile(h):
    return h if h <= 512 else 512


def _pair_sum(g, t, cq, name):
    _, a, b = g.shape
    H = a // 2
    tr = _sum_rows_tile(H)

    def body(cq_ref, g_ref, t_ref, o_ref):
        o_ref[...] = (g_ref[...].astype(F32) + t_ref[...].astype(F32)).astype(o_ref.dtype)

    grid_spec = pltpu.PrefetchScalarGridSpec(
        num_scalar_prefetch=1, grid=(N_CHIPS, H // tr),
        in_specs=[pl.BlockSpec((None, None, tr, b), lambda j, i, cq_ref: (j, cq_ref[0], i, 0)),
                  pl.BlockSpec((None, tr, b), lambda j, i, cq_ref: (j, i, 0))],
        out_specs=pl.BlockSpec((None, tr, b), lambda j, i, cq_ref: (j, i, 0)))
    return pl.pallas_call(
        body, name=name, grid_spec=grid_spec, out_shape=_out(t.shape, BF16),
        compiler_params=_cparams(("parallel", "parallel")),
    )(cq, g.reshape(N_CHIPS, 2, H, b), t)


def _scatter_copies(src_refs, land_refs, x, y, c):
    out = []
    for j, (cx, cy) in enumerate(_other_chips(x, y)):
        for t in range(len(src_refs)):
            out.append((3 * t + j, src_refs[t].at[2 * cx + cy], land_refs[t].at[j], (cx, cy, c)))
    return out


def _scatter_start(ops, after, name):
    lands = [lax.empty((3,) + o.shape[1:], o.dtype) for o in ops]
    return _ici_start(ops, lands, after, _scatter_copies, name)


def _scatter_wait(handle, after, name):
    return _ici_wait(handle, after, _scatter_copies, name)


def _chip_sum(p, t, cq, name):
    _, H, b = p.shape
    tr = _sum_rows_tile(H)

    def body(cq_ref, p_ref, t_ref, o_ref):
        acc = p_ref[...].astype(F32)
        for j in range(3):
            acc = acc + t_ref[j].astype(F32)
        o_ref[...] = acc

    grid_spec = pltpu.PrefetchScalarGridSpec(
        num_scalar_prefetch=1, grid=(H // tr,),
        in_specs=[pl.BlockSpec((None, tr, b), lambda i, cq_ref: (cq_ref[1], i, 0)),
                  pl.BlockSpec((3, tr, b), lambda i, cq_ref: (0, i, 0))],
        out_specs=pl.BlockSpec((None, tr, b), lambda i, cq_ref: (cq_ref[0], i, 0)))
    out = pl.pallas_call(
        body, name=name, grid_spec=grid_spec, out_shape=_out((2, H, b), F32),
        compiler_params=_cparams(("parallel",)),
    )(cq, p, t)
    return out.reshape(2 * H, b)


def _join_halves(ops, name):
    n = len(ops)

    def body(*refs):
        out_refs, send_sems, recv_sems = refs[n:2 * n], refs[2 * n], refs[2 * n + 1]
        x, y, c = _place()
        cps = []
        for t in range(n):
            H = ops[t].shape[0] // 2
            mine = out_refs[t].at[pl.ds(c * H, H), :]
            cp = pltpu.make_async_remote_copy(src_ref=mine, dst_ref=mine, send_sem=send_sems.at[t],
                                              recv_sem=recv_sems.at[t], device_id=(x, y, 1 - c),
                                              device_id_type=MESH)
            cp.start()
            cps.append(cp)
        for t in range(n):
            H = ops[t].shape[0] // 2
            other = out_refs[t].at[pl.ds((1 - c) * H, H), :]
            pltpu.make_async_remote_copy(src_ref=other, dst_ref=other, send_sem=send_sems.at[t],
                                         recv_sem=recv_sems.at[t], device_id=(x, y, 1 - c),
                                         device_id_type=MESH).wait_recv()
        for cp in cps:
            cp.wait_send()

    return pl.pallas_call(
        body, name=name, in_specs=[HBM_SPEC] * n, out_specs=[HBM_SPEC] * n,
        out_shape=[jax.ShapeDtypeStruct(o.shape, o.dtype) for o in ops],
        input_output_aliases={t: t for t in range(n)},
        scratch_shapes=[pltpu.SemaphoreType.DMA((n,)), pltpu.SemaphoreType.DMA((n,))],
    )(*ops)


def _reduce_scatter_start(gs, cq, after, tag):
    ts = _swap_halves(gs, "rs_swap_" + tag)
    ps = [_pair_sum(g, t, cq, "rs_pair_sum") for g, t in zip(gs, ts)]
    return _scatter_start(ps, after, "rs_scatter_start_" + tag)


def _reduce_scatter_finish(handle, cq, after, tag):
    ps, rs = _scatter_wait(handle, after, "rs_scatter_wait_" + tag)
    fs = [_chip_sum(p, r, cq, "rs_chip_sum") for p, r in zip(ps, rs)]
    return _join_halves(fs, "rs_join_" + tag)


def _all_reduce_small(v):
    n = v.shape[0]

    def body(v_ref, out_ref, buf, send_sems, recv_sems):
        x, y, c = _place()
        me = 4 * x + 2 * y + c
        buf[me] = v_ref[...]
        cps = []
        for k in range(1, 8):
            peer = (x ^ (k >> 2), y ^ ((k >> 1) & 1), c ^ (k & 1))
            cp = pltpu.make_async_remote_copy(src_ref=v_ref, dst_ref=buf.at[me], send_sem=send_sems.at[k - 1],
                                              recv_sem=recv_sems.at[k - 1], device_id=peer, device_id_type=MESH)
            cp.start()
            cps.append(cp)
        for k in range(1, 8):
            px, py, pc = x ^ (k >> 2), y ^ ((k >> 1) & 1), c ^ (k & 1)
            land = buf.at[4 * px + 2 * py + pc]
            pltpu.make_async_remote_copy(src_ref=land, dst_ref=land, send_sem=send_sems.at[k - 1],
                                         recv_sem=recv_sems.at[k - 1], device_id=(px, py, pc),
                                         device_id_type=MESH).wait_recv()
        for cp in cps:
            cp.wait_send()
        acc = buf[0]
        for d in range(1, 8):
            acc = acc + buf[d]
        out_ref[...] = acc

    vm = pl.BlockSpec(memory_space=pltpu.VMEM)
    return pl.pallas_call(
        body, name="all_reduce_small", in_specs=[vm], out_specs=vm,
        out_shape=jax.ShapeDtypeStruct((n, 128), F32),
        scratch_shapes=[pltpu.VMEM((8, n, 128), F32), pltpu.SemaphoreType.DMA((7,)), pltpu.SemaphoreType.DMA((7,))],
    )(v)


SMALL_GATHER = (16, 1024)
SMALL_FULL = sum(_size(_full_shape(n)) for n in SMALL)
SMALL_FULL_ROWS = -(-SMALL_FULL // 128 // 8) * 8


def _layer_shards(w, i, q):
    kind, j = MIXER[i % 3], i // 3
    out = {n: w[n][i].astype(BF16) for n in COMMON_BIG}
    if kind == 'gla':
        win = jnp.zeros((D_MODEL, GLA_WIN), F32)
        win = lax.dynamic_update_slice(win, w['gla_w_in'][j], (0, (GLA_SHARD - GLA_WIN_STEP) * q))
        out['gla_w_in'] = win.astype(BF16)
        out['gla_w_out'] = w['gla_w_out'][j].astype(BF16)
    elif kind == 'mla':
        out['mla_w_in'] = jnp.pad(w['mla_w_in'][j], ((0, 0), (0, MLA_IN_PAD - MLA_IN))).astype(BF16)
        for n in ('mla_w_uq', 'mla_w_ukv', 'mla_w_out'):
            out[n] = w[n][j].astype(BF16)
    else:
        out['conv_w_in'] = w['conv_w_in'][j].astype(BF16)
        out['conv_w_out'] = w['conv_w_out'][j].astype(BF16)
    return out


def _rows_joined(g):
    return g.reshape(g.shape[0] * g.shape[1], g.shape[2])


def _cols_joined(g):
    return jnp.moveaxis(g, 0, 1).reshape(g.shape[1], -1)


def _layer_weights(g, i):
    kind = MIXER[i % 3]
    W = {}
    if 'mlp_w1' in g:
        W = {'w1': g['mlp_w1'], 'w2': _rows_joined(g['mlp_w2']), 'gate': _rows_joined(g['ple_w_gate']),
             'proj': g['ple_w_proj']}
    if kind == 'gla' and 'gla_w_out' in g:
        W['w_out'] = _rows_joined(g['gla_w_out'])
    if kind == 'gla' and 'gla_w_in' in g:
        parts = []
        for qq in range(N_CHIPS):
            lo = g['gla_w_in'][qq][:, :128]
            if qq > 0:
                lo = lo + g['gla_w_in'][qq - 1][:, GLA_WIN_STEP:]
            parts += [lo, g['gla_w_in'][qq][:, 128:GLA_WIN_STEP]]
        parts.append(g['gla_w_in'][N_CHIPS - 1][:, GLA_WIN_STEP:])
        W['w_in'] = jnp.concatenate(parts, axis=1)
    elif kind == 'mla':
        W['w_in'] = _rows_joined(g['mla_w_in'])
        uq = _cols_joined(g['mla_w_uq']).reshape(MLA_QR, MLA_HEADS, MLA_NOPE + MLA_ROPE)
        W['w_uq'] = jnp.pad(uq, ((0, 0), (0, 0), (0, MLA_QH - MLA_NOPE - MLA_ROPE))).reshape(MLA_QR, -1)
        ukv = _cols_joined(g['mla_w_ukv']).reshape(MLA_KVR, MLA_HEADS, 2, 128)
        W['w_ukv'] = ukv.transpose(0, 2, 1, 3).reshape(MLA_KVR, -1)
        W['w_out'] = _rows_joined(g['mla_w_out'])
    elif kind == 'conv':
        W['w_in'] = g['conv_w_in']
        W['w_out'] = _rows_joined(g['conv_w_out'])
    return W


def _pack_small_shards(w):
    flat = jnp.concatenate([w[n].reshape(-1) for n in SMALL_SHARDED])
    return jnp.pad(flat, (0, _size(SMALL_GATHER) - flat.shape[0])).reshape(SMALL_GATHER)


def _unpack_small_gathered(g):
    flat, out, off = g.reshape(N_CHIPS, -1), {}, 0
    for n in SMALL_SHARDED:
        shape, ax = WSPEC[n]
        seg = flat[:, off:off + _size(shape)].reshape((N_CHIPS,) + shape)
        out[n] = jnp.moveaxis(seg, 0, ax).reshape(_full_shape(n))
        off += _size(shape)
    return out


def _pack_small(vals):
    flat = jnp.concatenate([vals[n].reshape(-1) for n in SMALL])
    return jnp.pad(flat, (0, SMALL_FULL_ROWS * 128 - flat.shape[0])).reshape(SMALL_FULL_ROWS, 128)


def _unpack_small(packed, q):
    flat = packed.reshape(-1)
    out, off = {}, 0
    for n in SMALL:
        shape, ax = WSPEC[n]
        full = flat[off:off + _size(_full_shape(n))].reshape(_full_shape(n))
        off += _size(_full_shape(n))
        out[n] = full if ax is None else lax.dynamic_slice_in_dim(full, q * shape[ax], shape[ax], axis=ax)
    return out


def _row_shards(dw):
    return dw.reshape(N_CHIPS, dw.shape[0] // N_CHIPS, dw.shape[1])


def _col_shards(dw):
    return jnp.moveaxis(dw.reshape(dw.shape[0], N_CHIPS, -1), 1, 0)


def _row(v):
    return v.reshape(1, -1)


def _layer_fwd(i, xin, xin_b, p_i, W, sm, cosp, sinp, rest=None):
    kind, j = MIXER[i % 3], i // 3
    sv = {'xin': xin, 'xin_b': xin_b}
    if kind == 'gla':
        w_up = jnp.pad(sm['gla_w_gate_up'][j].astype(BF16), ((0, 128 - GLA_RANK), (0, 0)))
        pin = _mm(xin_b, W['w_in'], name="gla_in", tn=640, tm=FULL_ROWS)
        la = _mm(pin, w_up, name="gla_gate", K=128, tk=128, a_off=(0, (GLA_IN_PAD - 128) // 128), tn=512,
                 extras=[(_row(sm['gla_b_gate'][j]), 'n')],
                 epilogue=lambda acc, b: (_log_sigmoid(acc + b) * (1.0 / GLA_TAU),))
        o, states = _gla_fwd(pin, la)
        yb = _gla_post_fwd(o, pin, _row(sm['gla_norm_g'][j]))
        if rest is not None:
            W = {**W, **rest(yb)}
        mixed = yb
        sv.update(w_up=w_up, pin=pin, la=la, o=o, states=states, yb=yb)
    elif kind == 'mla':
        gq, gkv = sm['mla_q_norm'][j:j + 1], sm['mla_kv_norm'][j:j + 1]
        cin = _mm(xin_b, W['w_in'], name="mla_in", tn=640, tm=FULL_ROWS)
        qn, kvn, kr = _mla_norm_fwd(cin, gq, gkv, cosp, sinp)
        qr = _mla_qrope_fwd(_mm(qn, W['w_uq'], name="mla_uq"), cosp, sinp)
        knv = _mm(kvn, W['w_ukv'], name="mla_ukv", out_dtypes=(BF16,))
        o = _mla_attn_fwd(qr, knv, kr)
        ob = o.astype(BF16)
        mixed = ob
        sv.update(gq=gq, gkv=gkv, cin=cin, qn=qn, kvn=kvn, kr=kr, qr=qr, knv=knv, o=o, ob=ob)
    else:
        w8 = jnp.pad(sm['conv_w'][j], ((0, 5), (0, 0)))
        bcu = _mm(xin_b, W['w_in'], name="conv_in", tn=768, b_sh=True, tm=FULL_ROWS)
        yb = _conv_fwd(bcu, w8)
        mixed = yb
        sv.update(w8=w8, bcu=bcu, yb=yb)
    g0, b0 = _row(sm['ln_g'][i, 0]), _row(sm['ln_b'][i, 0])
    g1, b1 = _row(sm['ln_g'][i, 1]), _row(sm['ln_b'][i, 1])
    ln = dict(tm=512, tn=D_MODEL, out_dtypes=(F32, BF16, F32), epilogue=_ln_fwd_epilogue)
    x1, x1b, v0 = _mm(mixed, W['w_out'], name="mix_out_ln", extras=[(xin, 'mn'), (g0, 'n'), (b0, 'n')], **ln)
    ab = _mm(x1b, W['w1'], name="mlp_up", out_dtypes=(BF16,), b_sh=True, tm=FULL_ROWS,
             epilogue=lambda acc: (jnp.square(jnp.maximum(acc, 0.0)),))
    x2, x2b, v1 = _mm(ab, W['w2'], name="mlp_down_ln", tk=D_FF, extras=[(x1, 'mn'), (g1, 'n'), (b1, 'n')], **ln)
    pp = _mm(p_i, W['proj'], name="ple_proj", tn=256, b_sh=True)
    z, x3, x3b = _mm(x2b, W['gate'], name="ple_gate", out_dtypes=(F32, F32, BF16),
                     extras=[(x2, 'mn'), (pp, 'mn')],
                     epilogue=lambda acc, xv, pv: (acc,) + (xv + jax.nn.sigmoid(acc) * pv,) * 2)
    sv.update(v0=v0, x1b=x1b, ab=ab, v1=v1, x2b=x2b, pp=pp, z=z, g0=g0, g1=g1)
    return x3, x3b, sv, W


def _layer_bwd(i, dx, p_i, W, sm, sv, cosp, sinp, token, early=None):
    kind, j = MIXER[i % 3], i // 3
    big, small = {}, {}
    dpp_b, dz_b = _ple_bwd_gate(dx, sv['z'], sv['pp'], token)
    big['ple_w_proj'] = _mm(p_i, dpp_b, ta=True, name="ple_proj_dw", tn=256, out_sh=True, out_dtypes=(BF16,))
    big['ple_w_gate'] = _row_shards(_mm(sv['x2b'], dz_b, ta=True, name="dw_dd", out_dtypes=(BF16,)))
    ln = dict(tb=True, tm=512, tn=D_MODEL, out_dtypes=(F32, BF16), n_sums=2)
    (dv1, dv1b), (dg1, db1) = _mm(dz_b, W['gate'], name="ple_gate_dx_ln", epilogue=_ln_bwd_epilogue(1.0),
                                  extras=[(dx, 'mn'), (sv['v1'], 'mn'), (sv['g1'], 'n')], **ln)
    big['mlp_w2'] = _row_shards(_mm(sv['ab'], dv1b, ta=True, name="mlp_down_dw", out_dtypes=(BF16,)))
    dub = _mm(dv1b, W['w2'], tb=True, name="mlp_down_dx", out_dtypes=(BF16,), tm=FULL_ROWS,
              extras=[(sv['ab'], 'mn')], epilogue=lambda acc, a: (acc * (2.0 * jnp.sqrt(a.astype(F32))),))
    big['mlp_w1'] = _mm(sv['x1b'], dub, ta=True, name="mlp_up_dw", out_sh=True, out_dtypes=(BF16,))
    order = []
    if early is not None:
        order, big = [(early(big), 'whole')], {}
    (dv0, dv0b), (dg0, db0) = _mm(dub, W['w1'], name="mlp_up_dx_ln", b_sh=True, tk=D_FF, epilogue=_ln_bwd_epilogue(ALPHA),
                                  extras=[(dv1, 'mn'), (sv['v0'], 'mn'), (sv['g0'], 'n')] + order, **ln)
    small['ln_g'] = jnp.stack([dg0[0], dg1[0]])
    small['ln_b'] = jnp.stack([db0[0], db1[0]])
    resid = dict(tn=1024, extras=[(dv0, 'mn')], epilogue=lambda acc, r: (acc + ALPHA * r,))
    if kind == 'gla':
        big['gla_w_out'] = _row_shards(_mm(sv['yb'], dv0b, ta=True, name="dw_dd", out_dtypes=(BF16,)))
        dy = _mm(dv0b, W['w_out'], tb=True, name="dx_dd", tn=1024)
        do, dr_b, dng = _gla_post_bwd(dy, sv['o'], sv['pin'], _row(sm['gla_norm_g'][j]))
        dq_b, dk_b, dvv_b, dla = _gla_bwd(sv['pin'], sv['la'], sv['states'], do)
        dzg_b, dbg = _gla_gate_bwd(dla, sv['la'])
        dw_up = _mm(sv['pin'], dzg_b, ta=True, name="gla_gate_dw", M=128, tm=128,
                    a_off=(0, (GLA_IN_PAD - 128) // 128))
        dglr_b = _mm(dzg_b, sv['w_up'], tb=True, name="gla_gate_dx", out_dtypes=(BF16,))
        dpin_b = jnp.concatenate([dq_b, dk_b, dvv_b, dr_b, dglr_b], axis=1)
        dw_in = _mm(sv['xin_b'], dpin_b, ta=True, name="gla_in_dw", tn=640, out_dtypes=(BF16,))
        dxin = _mm(dpin_b, W['w_in'], tb=True, name="gla_in_dx", tk=640, **resid)
        big['gla_w_in'] = jnp.stack([dw_in[:, GLA_WIN_STEP * qq:GLA_WIN_STEP * qq + GLA_WIN]
                                     for qq in range(N_CHIPS)])
        small.update(gla_w_gate_up=dw_up[:GLA_RANK], gla_b_gate=dbg[0], gla_norm_g=dng[0])
    elif kind == 'mla':
        big['mla_w_out'] = _row_shards(_mm(sv['ob'], dv0b, ta=True, name="dw_dd", out_dtypes=(BF16,)))
        do = _mm(dv0b, W['w_out'], tb=True, name="dx_dd", tn=1024)
        dqr, dkn_b, dvv_b, dkr = _mla_attn_bwd(sv['qr'], sv['knv'], sv['kr'], sv['o'], do)
        dq_b = _mla_qrope_bwd(dqr, cosp, sinp)
        dw_uq = _mm(sv['qn'], dq_b, ta=True, name="mla_up_dw", out_dtypes=(BF16,))
        dqn = _mm(dq_b, W['w_uq'], tb=True, name="mla_up_dx")
        dknv_b = jnp.concatenate([dkn_b, dvv_b], axis=1)
        dw_ukv = _mm(sv['kvn'], dknv_b, ta=True, name="mla_up_dw", out_dtypes=(BF16,))
        dkvn = _mm(dknv_b, W['w_ukv'], tb=True, name="mla_up_dx")
        dcin_b, dgq, dgkv = _mla_norm_bwd(sv['cin'], dqn, dkvn, dkr, sv['gq'], sv['gkv'], cosp, sinp)
        big['mla_w_in'] = _row_shards(_mm(sv['xin_b'], dcin_b, ta=True, name="mla_in_dw", tn=640,
                                          out_dtypes=(BF16,)))
        dxin = _mm(dcin_b, W['w_in'], tb=True, name="mla_in_dx", tk=640, **resid)
        big['mla_w_uq'] = _col_shards(
            dw_uq.reshape(MLA_QR, MLA_HEADS, MLA_QH)[:, :, :MLA_NOPE + MLA_ROPE].reshape(MLA_QR, -1))
        big['mla_w_ukv'] = _col_shards(
            dw_ukv.reshape(MLA_KVR, 2, MLA_HEADS, 128).transpose(0, 2, 1, 3).reshape(MLA_KVR, -1))
        small.update(mla_q_norm=dgq[0], mla_kv_norm=dgkv[0])
    else:
        big['conv_w_out'] = _row_shards(_mm(sv['yb'], dv0b, ta=True, name="dw_dd", out_dtypes=(BF16,)))
        dy = _mm(dv0b, W['w_out'], tb=True, name="dx_dd", tn=1024)
        db_b, dc_b, du_b, dw8 = _conv_bwd(sv['bcu'], sv['w8'], dy)
        dbcu_b = jnp.concatenate([db_b, dc_b, du_b], axis=1)
        big['conv_w_in'] = _mm(sv['xin_b'], dbcu_b, ta=True, name="conv_in_dw", tn=768, out_sh=True,
                               out_dtypes=(BF16,))
        dxin = _mm(dbcu_b, W['w_in'], tb=True, name="conv_in_dx", tk=768, b_sh=True, **resid)
        small['conv_w'] = dw8[:3]
    return dxin, big, small


def _rope_tables(positions):
    inv_freq = ROPE_BASE ** (-jnp.arange(0, MLA_ROPE // 2, dtype=F32) * (2.0 / MLA_ROPE))
    ang = positions.astype(F32)[:, None] * inv_freq
    zeros = jnp.zeros((positions.shape[0], 64), F32)
    return (jnp.concatenate([jnp.cos(ang), jnp.cos(ang), zeros], axis=1),
            jnp.concatenate([jnp.sin(ang), jnp.sin(ang), zeros], axis=1))


FIRST_NEEDED = ['gla_w_in']


def _start_gathers(w, q):
    token, started = jnp.zeros(TOKEN, F32), []
    for i in range(DEPTH):
        sh = _layer_shards(w, i, q)
        groups = [list(sh)] if i > 0 else [FIRST_NEEDED, [n for n in sh if n not in FIRST_NEEDED]]
        for k, names in enumerate(groups):
            ops = [sh[n] for n in names]
            if i == 0 and k == 0:
                ops.append(_pack_small_shards(w))
            tag = "l%d%s" % (i, "ab"[k] if i == 0 else "")
            handle = _gather_start(ops, token, "ag_start_" + tag)
            token = handle[4]
            started.append((handle, names, tag))
    return started, token


def _finish_gather(entry, after):
    handle, names, tag = entry
    srcs, lands = _gather_wait(handle, after, "ag_wait_" + tag)
    got = _gather_finish(srcs, lands, "ag_finish_" + tag)
    return dict(zip(names, got)), got[-1]


def _local_shard_grad(name, g, q):
    if name == 'gla_w_in':
        return lax.dynamic_slice_in_dim(g, (GLA_SHARD - GLA_WIN_STEP) * q, GLA_SHARD, axis=1)
    if name == 'mla_w_in':
        return g[:, :MLA_IN]
    return g


def kernel(x, p, positions, gla_w_in, gla_w_gate_up, gla_b_gate, gla_norm_g, gla_w_out, mla_w_in, mla_q_norm, mla_kv_norm, mla_w_uq, mla_w_ukv, mla_w_out, conv_w_in, conv_w, conv_w_out, ln_g, ln_b, mlp_w1, mlp_w2, ple_w_gate, ple_w_proj, loss_target, m_gla_w_in, m_gla_w_gate_up, m_gla_b_gate, m_gla_norm_g, m_gla_w_out, m_mla_w_in, m_mla_q_norm, m_mla_kv_norm, m_mla_w_uq, m_mla_w_ukv, m_mla_w_out, m_conv_w_in, m_conv_w, m_conv_w_out, m_ln_g, m_ln_b, m_mlp_w1, m_mlp_w2, m_ple_w_gate, m_ple_w_proj, v_gla_w_in, v_gla_w_gate_up, v_gla_b_gate, v_gla_norm_g, v_gla_w_out, v_mla_w_in, v_mla_q_norm, v_mla_kv_norm, v_mla_w_uq, v_mla_w_ukv, v_mla_w_out, v_conv_w_in, v_conv_w, v_conv_w_out, v_ln_g, v_ln_b, v_mlp_w1, v_mlp_w2, v_ple_w_gate, v_ple_w_proj):
    args = locals()
    w = {n: args[n] for n in WNAMES}
    m = {n: args['m_' + n] for n in WNAMES}
    v = {n: args['v_' + n] for n in WNAMES}
    q = 2 * lax.axis_index("x") + lax.axis_index("y")
    cq = jnp.stack([lax.axis_index("c"), q]).astype(jnp.int32)

    cosp, sinp = _rope_tables(positions[0])
    started, after = _start_gathers(w, q)
    xin, saved, layers, sm = x[0], [], [], None
    xin_b = xin.astype(BF16)
    for i in range(DEPTH):
        got, last = _finish_gather(started[i + 1 if i else 0], after)
        rest = None
        if i == 0:
            sm = _unpack_small_gathered(last)
            sm['mla_q_norm'], sm['mla_kv_norm'] = w['mla_q_norm'], w['mla_kv_norm']
            rest = lambda after: _layer_weights(_finish_gather(started[1], after)[0], 0)
        xin, xin_b, sv, W = _layer_fwd(i, xin, xin_b, p[i, 0], _layer_weights(got, i), sm, cosp, sinp, rest)
        layers.append(W)
        saved.append(sv)
        after = xin
    dx, loss_cols = _loss_head(xin, loss_target[0])
    loss = lax.psum(jnp.sum(loss_cols[0]), ("x", "y", "c"))

    gbig = {n: [None] * WSPEC[n][0][0] for n in BIG}
    gsmall = {n: [None] * _full_shape(n)[0] for n in SMALL}
    pending = []

    def start(grads, i, tag):
        names = list(grads)
        handle = _reduce_scatter_start([grads[n] for n in names], cq, jnp.zeros(TOKEN, F32), tag)
        pending.append((handle, names, i, tag))
        return handle[4]

    def finish(above, after):
        for entry in [e for e in pending if e[2] > above]:
            pending.remove(entry)
            handle, names, i, tag = entry
            for n, g in zip(names, _reduce_scatter_finish(handle, cq, after, tag)):
                gbig[n][i if n in COMMON_BIG else i // 3] = _local_shard_grad(n, g, q)

    token = jnp.zeros(TOKEN, F32)
    for i in reversed(range(DEPTH)):
        early = (lambda grads: start(grads, 0, "l0a")) if i == 0 else None
        dx, big, small = _layer_bwd(i, dx, p[i, 0], layers[i], sm, saved[i], cosp, sinp, token, early)
        token = start(big, i, "l%d%s" % (i, "b" if i == 0 else ""))
        finish(i, dx)
        for n, g in small.items():
            gsmall[n][i if n in ('ln_g', 'ln_b') else i // 3] = g
    finish(-1, token)
    gsm = _unpack_small(_all_reduce_small(_pack_small({n: jnp.stack(g) for n, g in gsmall.items()})), q)

    grad, delta, new_m, new_v = {}, {}, {}, {}
    for n in BIG:
        grad[n], delta[n], new_m[n], new_v[n] = _adamw(w[n], m[n], v[n], gbig[n], "adamw_" + n)
    total = sum(_size(WSPEC[n][0]) for n in SMALL)
    rows = -(-total // 128 // 8) * 8

    def pack(dct):
        flat = jnp.concatenate([dct[n].reshape(-1) for n in SMALL])
        return jnp.pad(flat, (0, rows * 128 - total), constant_values=1.0).reshape(1, rows, 128)

    res = _adamw(pack(w), pack(m), pack(v), [pack(gsm)[0]], "adamw_small")
    for out, packed in zip((grad, delta, new_m, new_v), res):
        flat, off = packed.reshape(-1), 0
        for n in SMALL:
            sz = _size(WSPEC[n][0])
            out[n] = flat[off:off + sz].reshape(WSPEC[n][0])
            off += sz
    return (loss, dx[None], *[grad[n] for n in WNAMES], *[delta[n] for n in WNAMES],
            *[new_m[n] for n in WNAMES], *[new_v[n] for n in WNAMES])
```

```python
import functools

import numpy as np
import jax
import jax.numpy as jnp
from jax import lax
from jax.experimental import pallas as pl
from jax.experimental.pallas import tpu as pltpu

F32 = jnp.float32
BF16 = jnp.bfloat16
MESH = pl.DeviceIdType.MESH

D_MODEL = 1024
DEPTH = 4
CHUNK = 64
ALPHA = (2 * DEPTH) ** 0.25
LN_EPS = 1e-5
RMS_EPS = 1e-6
PLE_DIM = 256
D_FF = 4 * D_MODEL
GLA_HEADS = 4
GLA_DK = 128
GLA_DV = 256
GLA_RANK = 16
GLA_TAU = 16.0
GLA_HK = GLA_HEADS * GLA_DK
GLA_HV = GLA_HEADS * GLA_DV
GLA_IN = 2 * GLA_HK + GLA_HV + D_MODEL + GLA_RANK
GLA_IN_PAD = 2 * GLA_HK + GLA_HV + D_MODEL + 128
GLA_SHARD = GLA_IN // 4
GLA_WIN = 896
GLA_WIN_STEP = 768
MLA_HEADS = 8
MLA_NOPE = 128
MLA_ROPE = 64
MLA_V = 128
MLA_QR = 256
MLA_KVR = 256
MLA_IN = MLA_QR + MLA_KVR + MLA_ROPE
MLA_IN_PAD = MLA_QR + MLA_KVR + 128
MLA_QH = 256
ROPE_BASE = 10000.0
ADAM_LR = 0.001
ADAM_B1 = 0.9
ADAM_B2 = 0.999
ADAM_EPS = 1e-08
ADAM_WD = 0.01
ADAM_STEP = 10

VMEM_LIMIT = 48 * 1024 * 1024
FULL_ROWS = 2048
N_CHIPS = 4

WSPEC = {
    'gla_w_in': ((2, 1024, 772), 2), 'gla_w_gate_up': ((2, 16, 128), 2), 'gla_b_gate': ((2, 128), 1),
    'gla_norm_g': ((2, 64), 1), 'gla_w_out': ((2, 256, 1024), 1), 'mla_w_in': ((1, 256, 576), 1),
    'mla_q_norm': ((1, 256), None), 'mla_kv_norm': ((1, 256), None), 'mla_w_uq': ((1, 256, 384), 2),
    'mla_w_ukv': ((1, 256, 512), 2), 'mla_w_out': ((1, 256, 1024), 1), 'conv_w_in': ((1, 1024, 768), 2),
    'conv_w': ((1, 3, 256), 2), 'conv_w_out': ((1, 256, 1024), 1), 'ln_g': ((4, 2, 256), 2),
    'ln_b': ((4, 2, 256), 2), 'mlp_w1': ((4, 1024, 1024), 2), 'mlp_w2': ((4, 1024, 1024), 1),
    'ple_w_gate': ((4, 256, 1024), 1), 'ple_w_proj': ((4, 256, 256), 2),
}
WNAMES = list(WSPEC)
BIG = ['gla_w_in', 'gla_w_out', 'mla_w_in', 'mla_w_uq', 'mla_w_ukv', 'mla_w_out', 'conv_w_in', 'conv_w_out',
       'mlp_w1', 'mlp_w2', 'ple_w_gate', 'ple_w_proj']
SMALL_SHARDED = ['gla_w_gate_up', 'gla_b_gate', 'gla_norm_g', 'conv_w', 'ln_g', 'ln_b']
SMALL = SMALL_SHARDED + ['mla_q_norm', 'mla_kv_norm']
MIXER = ['gla', 'mla', 'conv']
LAYER_BIG = {'gla': ['gla_w_in', 'gla_w_out'], 'mla': ['mla_w_in', 'mla_w_uq', 'mla_w_ukv', 'mla_w_out'],
             'conv': ['conv_w_in', 'conv_w_out']}
COMMON_BIG = ['mlp_w1', 'mlp_w2', 'ple_w_gate', 'ple_w_proj']


def _size(shape):
    return int(np.prod(shape))


def _full_shape(name):
    shape, ax = WSPEC[name]
    if ax is None:
        return shape
    return tuple(s * N_CHIPS if i == ax else s for i, s in enumerate(shape))


def _cparams(sem=None):
    return pltpu.CompilerParams(dimension_semantics=sem, vmem_limit_bytes=VMEM_LIMIT)


def _out(shape, dtype):
    return pltpu.HBM(shape, dtype)


def _hbm(v):
    return pltpu.with_memory_space_constraint(v, pltpu.HBM)


def _mm(a, b, *, name, ta=False, tb=False, M=None, N=None, K=None, out_dtypes=(F32,), epilogue=None, extras=(),
        tm=1024, tn=512, tk=None, a_off=(0, 0), b_sh=False, out_sh=False, n_sums=0):
    if M is None:
        M = a.shape[1] if ta else a.shape[0]
    if K is None:
        K = a.shape[0] if ta else a.shape[1]
    if b_sh:
        kw, nq = b.shape[1], b.shape[2]
        n_b, k_b = (kw, N_CHIPS * nq) if tb else (N_CHIPS * nq, kw)
        N = n_b if N is None else N
        assert K == k_b
    elif N is None:
        N = b.shape[0] if tb else b.shape[1]
    if tk is None:
        tk = FULL_ROWS if ta else 1024
    tm, tn, tk = min(tm, M), min(tn, N), min(tk, K)
    assert M % tm == 0 and N % tn == 0 and K % tk == 0, (name, M, N, K, tm, tn, tk)
    nk = K // tk
    n_ex, n_out = len(extras), len(out_dtypes)
    assert n_sums == 0 or tn == N

    n_b = N_CHIPS if (b_sh and tb and tk == K) else 1

    def body(a_ref, *rest):
        b_refs, rest = rest[:n_b], rest[n_b:]
        ex_refs, out_refs = rest[:n_ex], rest[n_ex:n_ex + n_out]
        sum_refs = rest[n_ex + n_out:n_ex + n_out + n_sums]
        first_rows = pl.program_id(0) == 0
        dims = ((((0,) if ta else (1,)), ((1,) if tb else (0,))), ((), ()))
        if n_b == 1:
            part = lax.dot_general(a_ref[...].astype(BF16), b_refs[0][...].astype(BF16), dims,
                                   preferred_element_type=F32)
        else:
            part = sum(lax.dot_general(a_ref[:, s * nq:(s + 1) * nq].astype(BF16), b_refs[s][...].astype(BF16), dims,
                                       preferred_element_type=F32) for s in range(n_b))

        def finish(acc):
            res = (acc,) if epilogue is None else epilogue(acc, *[r[...] for r in ex_refs])
            if n_sums:
                res, sums = res

                @pl.when(first_rows)
                def _():
                    for r in sum_refs:
                        r[...] = jnp.zeros(r.shape, F32)

                for r, v in zip(sum_refs, sums):
                    r[...] += jnp.broadcast_to(v, r.shape)
            for r, v in zip(out_refs, res):
                r[...] = v.astype(r.dtype)

        if nk == 1:
            finish(part)
        else:
            acc_ref = rest[-1]
            k = pl.program_id(2)

            @pl.when(k == 0)
            def _():
                acc_ref[...] = part

            @pl.when(k > 0)
            def _():
                acc_ref[...] += part

            @pl.when(k == nk - 1)
            def _():
                finish(acc_ref[...])

    if ta:
        a_spec = pl.BlockSpec((tk, tm), lambda i, j, k: (k + a_off[0], i + a_off[1]))
    else:
        a_spec = pl.BlockSpec((tm, tk), lambda i, j, k: (i + a_off[0], k + a_off[1]))
    once = dict(pipeline_mode=pl.Buffered(1)) if (tn == N and nk == 1) else {}
    if n_b > 1:
        b_specs = [pl.BlockSpec((None, tn, nq), functools.partial(lambda i, j, k, s: (s, j, 0), s=s), **once)
                   for s in range(n_b)]
    elif b_sh and tb:
        assert nq % tk == 0
        per = nq // tk
        b_spec = pl.BlockSpec((None, tn, tk), lambda i, j, k: (k // per, j, k % per), **once)
    elif b_sh:
        assert nq % tn == 0
        per = nq // tn
        b_spec = pl.BlockSpec((None, tk, tn), lambda i, j, k: (j // per, k, j % per), **once)
    elif tb:
        b_spec = pl.BlockSpec((tn, tk), lambda i, j, k: (j, k), **once)
    else:
        b_spec = pl.BlockSpec((tk, tn), lambda i, j, k: (k, j), **once)
    if n_b == 1:
        b_specs = [b_spec]
    ex_specs = []
    for arr, kind in extras:
        if kind == 'mn':
            ex_specs.append(pl.BlockSpec((tm, tn), lambda i, j, k: (i, j)))
        elif kind == 'n':
            ex_specs.append(pl.BlockSpec((1, tn), lambda i, j, k: (0, j)))
        else:
            ex_specs.append(pl.BlockSpec(arr.shape, lambda i, j, k: (0, 0)))
    if out_sh:
        assert (N // N_CHIPS) % tn == 0
        per_o = N // N_CHIPS // tn
        o_spec = pl.BlockSpec((None, tm, tn), lambda i, j, k: (j // per_o, i, j % per_o))
        o_shape = (N_CHIPS, M, N // N_CHIPS)
    else:
        o_spec = pl.BlockSpec((tm, tn), lambda i, j, k: (i, j))
        o_shape = (M, N)
    outs = pl.pallas_call(
        body, name=name, grid=(M // tm, N // tn, nk),
        in_specs=[a_spec] + b_specs + ex_specs,
        out_specs=[o_spec for _ in out_dtypes] + [pl.BlockSpec((8, N), lambda i, j, k: (0, 0))] * n_sums,
        out_shape=[_out(o_shape, d) for d in out_dtypes] + [_out((8, N), F32)] * n_sums,
        scratch_shapes=[pltpu.VMEM((tm, tn), F32)] if nk > 1 else [],
        compiler_params=_cparams(("arbitrary" if n_sums else "parallel", "parallel", "arbitrary")),
    )(a, *[b] * n_b, *[e[0] for e in extras])
    if n_sums:
        return tuple(outs[:n_out]), tuple(outs[n_out:])
    return outs[0] if n_out == 1 else tuple(outs)


def _rowwise(fn, *, name, rows, pars=(), outs=(), accs=(), tm=256):
    S = rows[0][0].shape[0]
    tm = min(tm, S)
    assert S % tm == 0
    n_r, n_p, n_o, n_a = len(rows), len(pars), len(outs), len(accs)

    def body(*refs):
        r_refs, p_refs = refs[:n_r], refs[n_r:n_r + n_p]
        o_refs, a_refs = refs[n_r + n_p:n_r + n_p + n_o], refs[n_r + n_p + n_o:]
        o_vals, a_vals = fn([r[...] for r in r_refs], [p[...] for p in p_refs])
        for r, v in zip(o_refs, o_vals):
            r[...] = v.astype(r.dtype)
        if n_a:
            i = pl.program_id(0)

            @pl.when(i == 0)
            def _():
                for r in a_refs:
                    r[...] = jnp.zeros(r.shape, r.dtype)

            for r, v in zip(a_refs, a_vals):
                r[...] += jnp.broadcast_to(v, r.shape)

    in_specs = [pl.BlockSpec((tm, w), functools.partial(lambda i, o: (i, o), o=off)) for _, w, off in rows]
    in_specs += [pl.BlockSpec(p.shape, functools.partial(lambda i, nd: (0,) * nd, nd=p.ndim)) for p in pars]
    out_specs = [pl.BlockSpec((tm, w), lambda i: (i, 0)) for w, _ in outs]
    out_specs += [pl.BlockSpec((8, w), lambda i: (0, 0)) for w in accs]
    out_shape = [_out((S, w), d) for w, d in outs]
    out_shape += [_out((8, w), F32) for w in accs]
    res = pl.pallas_call(
        body, name=name, grid=(S // tm,), in_specs=in_specs, out_specs=out_specs, out_shape=out_shape,
        compiler_params=_cparams(("arbitrary",)),
    )(*[r[0] for r in rows], *pars)
    return tuple(res)


def _colsum(v):
    return jnp.sum(v, axis=0, keepdims=True)


def _ln_stats(v):
    mu = jnp.mean(v, axis=-1, keepdims=True)
    d = v - mu
    var = jnp.mean(d * d, axis=-1, keepdims=True)
    rstd = lax.rsqrt(var + LN_EPS)
    return d * rstd, rstd


def _ln_fwd_epilogue(h, x, g, b):
    v = ALPHA * x + h
    xhat, _ = _ln_stats(v)
    y = xhat * g + b
    return y, y, v


def _ln_bwd_epilogue(scale):
    def epilogue(acc, resid, v, g, *unused):
        dy = acc + scale * resid
        xhat, rstd = _ln_stats(v)
        dxh = dy * g
        m1 = jnp.mean(dxh, axis=-1, keepdims=True)
        m2 = jnp.mean(dxh * xhat, axis=-1, keepdims=True)
        dv = rstd * (dxh - m1 - xhat * m2)
        return (dv, dv), (_colsum(dy * xhat), _colsum(dy))
    return epilogue


def _loss_head(y, t):
    def fn(r, p):
        d = r[0] - r[1]
        return [d * (1.0 / D_MODEL)], [_colsum(d * d) * (0.5 / D_MODEL)]
    return _rowwise(fn, name="loss_head", rows=[(y, D_MODEL, 0), (t, D_MODEL, 0)], outs=[(D_MODEL, F32)],
                    accs=[D_MODEL])


def _ple_bwd_gate(dx3, z, pp, token):
    def fn(r, p):
        s = jax.nn.sigmoid(r[1])
        return [r[0] * s, r[0] * r[2] * s * (1.0 - s)], []
    return _rowwise(fn, name="ple_bwd_gate", rows=[(dx3, D_MODEL, 0), (z, D_MODEL, 0), (pp, D_MODEL, 0)],
                    pars=[token], outs=[(D_MODEL, BF16), (D_MODEL, BF16)])


N_LEVELS = 6


def _gla_consts():
    C = CHUNK
    A = np.zeros((N_LEVELS + 3, C, C), np.float32)
    masks = np.zeros((N_LEVELS + 1, C, C), np.float32)
    r = np.arange(C)[:, None]
    u = np.arange(C)[None, :]
    for l in range(N_LEVELS):
        half = C >> (l + 1)
        mid = (r // (2 * half)) * (2 * half) + half - 1
        A[l] = np.where(r > mid, (u > mid) & (u <= r), (u > r) & (u <= mid))
        masks[l] = ((r // (2 * half)) == (u // (2 * half))) & (((r // half) % 2) != ((u // half) % 2))
    masks[N_LEVELS] = (r == u)
    A[N_LEVELS] = (u <= r)
    A[N_LEVELS + 1] = (u > r)
    A[N_LEVELS + 2] = 1.0
    A = A.reshape(-1, C)
    return A, np.ascontiguousarray(A.T), masks


def _split3(v):
    hi = v.astype(BF16)
    r1 = v - hi.astype(F32)
    mid = r1.astype(BF16)
    lo = (r1 - mid.astype(F32)).astype(BF16)
    return hi, mid, lo


def _dot_exact01(a01, v):
    hi, mid, lo = _split3(v)
    f = lambda p: jnp.dot(a01, p, preferred_element_type=F32)
    return f(hi) + f(mid) + f(lo)


def _nt(a, b):
    return lax.dot_general(a, b, (((1,), (1,)), ((), ())), preferred_element_type=F32)


def _tn(a, b):
    return lax.dot_general(a, b, (((0,), (0,)), ((), ())), preferred_element_type=F32)


def _nn(a, b):
    return jnp.dot(a, b, preferred_element_type=F32)


def _gla_chunk_terms(q, k, E, m_ref):
    C = CHUNK
    scores = m_ref[N_LEVELS] * _nt(q.astype(BF16), k.astype(BF16))
    qes, kes = [], []
    for l in range(N_LEVELS):
        El = E[l * C:(l + 1) * C]
        qe, ke = (q * El).astype(BF16), (k * El).astype(BF16)
        qes.append(qe)
        kes.append(ke)
        scores = scores + m_ref[l] * _nt(qe, ke)
    return qes, kes, scores


def _head(v, h, w):
    return v[:, h * w:(h + 1) * w]


def _gla_fwd(pin, la):
    S = pin.shape[0]
    NC = S // CHUNK
    C = CHUNK
    A, _, masks = _gla_consts()

    def body(q_ref, k_ref, v_ref, la_ref, a_ref, m_ref, o_ref, st_ref, state):
        @pl.when(pl.program_id(0) == 0)
        def _():
            state[...] = jnp.zeros(state.shape, F32)

        E_all = jnp.exp(_dot_exact01(a_ref[...], la_ref[...]))
        q_all = q_ref[...] * (GLA_DK ** -0.5)
        k_all, v_all = k_ref[...], v_ref[...]
        outs = []
        for h in range(GLA_HEADS):
            q, k, E = _head(q_all, h, GLA_DK), _head(k_all, h, GLA_DK), _head(E_all, h, GLA_DK)
            _, _, scores = _gla_chunk_terms(q, k, E, m_ref)
            Eq, Ek, Ee = E[6 * C:7 * C], E[7 * C:8 * C], E[8 * C:9 * C]
            st = state[h]
            st_ref[h] = st
            vb = _head(v_all, h, GLA_DV).astype(BF16)
            outs.append(_nn(scores.astype(BF16), vb) + _nt((q * Eq).astype(BF16), st.astype(BF16)))
            state[h] = st * jnp.concatenate([Ee] * (GLA_DV // C), axis=0) + _tn(vb, (k * Ek).astype(BF16))
        o_ref[...] = jnp.concatenate(outs, axis=1)

    return pl.pallas_call(
        body, name="gla_fwd", grid=(NC,),
        in_specs=[pl.BlockSpec((C, GLA_HK), lambda c: (c, 0)),
                  pl.BlockSpec((C, GLA_HK), lambda c: (c, 1)),
                  pl.BlockSpec((C, GLA_HV), lambda c: (c, 2 * GLA_HK // GLA_HV)),
                  pl.BlockSpec((C, GLA_HK), lambda c: (c, 0)),
                  pl.BlockSpec(A.shape, lambda c: (0, 0)),
                  pl.BlockSpec(masks.shape, lambda c: (0, 0, 0))],
        out_specs=[pl.BlockSpec((C, GLA_HV), lambda c: (c, 0)),
                   pl.BlockSpec((GLA_HEADS, None, GLA_DV, GLA_DK), lambda c: (0, c, 0, 0))],
        out_shape=[_out((S, GLA_HV), F32), _out((GLA_HEADS, NC, GLA_DV, GLA_DK), F32)],
        scratch_shapes=[pltpu.VMEM((GLA_HEADS, GLA_DV, GLA_DK), F32)],
        compiler_params=_cparams(("arbitrary",)),
    )(pin, pin, pin, la, jnp.asarray(A, BF16), jnp.asarray(masks))


def _gla_bwd(pin, la, states, do):
    S = pin.shape[0]
    NC = S // CHUNK
    C = CHUNK
    A, AT, masks = _gla_consts()
    scale = GLA_DK ** -0.5

    def body(q_ref, k_ref, v_ref, la_ref, st_ref, do_ref, a_ref, at_ref, m_ref,
             dq_ref, dk_ref, dv_ref, dla_ref, dstate):
        @pl.when(pl.program_id(0) == 0)
        def _():
            dstate[...] = jnp.zeros(dstate.shape, F32)

        E_all = jnp.exp(_dot_exact01(a_ref[...], la_ref[...]))
        q_all = q_ref[...] * scale
        k_all, v_all, do_all = k_ref[...], v_ref[...], do_ref[...]
        dqs, dks, dvs, dXs = [], [], [], []
        for h in range(GLA_HEADS):
            q, k, E = _head(q_all, h, GLA_DK), _head(k_all, h, GLA_DK), _head(E_all, h, GLA_DK)
            qes, kes, scores = _gla_chunk_terms(q, k, E, m_ref)
            Eq, Ek, Ee = E[6 * C:7 * C], E[7 * C:8 * C], E[8 * C:9 * C]
            st, dst = st_ref[h], dstate[h]
            dob, vb = _head(do_all, h, GLA_DV).astype(BF16), _head(v_all, h, GLA_DV).astype(BF16)
            dstb = dst.astype(BF16)
            qEq, kEk = (q * Eq).astype(BF16), (k * Ek).astype(BF16)
            dsc = _nt(dob, vb)
            dvs.append(_tn(scores.astype(BF16), dob) + _nt(kEk, dstb))
            dqEq = _nn(dob, st.astype(BF16))
            dkEk = _nn(vb, dstb)
            Gd = (m_ref[N_LEVELS] * dsc).astype(BF16)
            dq = _nn(Gd, k.astype(BF16)) + dqEq * Eq
            dk = _tn(Gd, q.astype(BF16)) + dkEk * Ek
            dX = []
            for l in range(N_LEVELS):
                El = E[l * C:(l + 1) * C]
                G = (m_ref[l] * dsc).astype(BF16)
                dqe, dke = _nn(G, kes[l]), _tn(G, qes[l])
                dq = dq + dqe * El
                dk = dk + dke * El
                dX.append((dqe * q + dke * k) * El)
            dX.append(dqEq * q * Eq)
            dX.append(dkEk * k * Ek)
            prod = dst * st
            dEe = prod[0:C]
            for i in range(1, GLA_DV // C):
                dEe = dEe + prod[i * C:(i + 1) * C]
            dX.append(dEe * Ee)
            dXs.append(jnp.concatenate(dX, axis=0))
            dqs.append(dq * scale)
            dks.append(dk)
            dstate[h] = dst * jnp.concatenate([Ee] * (GLA_DV // C), axis=0) + _tn(dob, qEq)
        dla_ref[...] = _dot_exact01(at_ref[...], jnp.concatenate(dXs, axis=1))
        dq_ref[...] = jnp.concatenate(dqs, axis=1).astype(dq_ref.dtype)
        dk_ref[...] = jnp.concatenate(dks, axis=1).astype(dk_ref.dtype)
        dv_ref[...] = jnp.concatenate(dvs, axis=1).astype(dv_ref.dtype)

    rc = lambda c: NC - 1 - c
    return pl.pallas_call(
        body, name="gla_bwd", grid=(NC,),
        in_specs=[pl.BlockSpec((C, GLA_HK), lambda c: (rc(c), 0)),
                  pl.BlockSpec((C, GLA_HK), lambda c: (rc(c), 1)),
                  pl.BlockSpec((C, GLA_HV), lambda c: (rc(c), 2 * GLA_HK // GLA_HV)),
                  pl.BlockSpec((C, GLA_HK), lambda c: (rc(c), 0)),
                  pl.BlockSpec((GLA_HEADS, None, GLA_DV, GLA_DK), lambda c: (0, rc(c), 0, 0)),
                  pl.BlockSpec((C, GLA_HV), lambda c: (rc(c), 0)),
                  pl.BlockSpec(A.shape, lambda c: (0, 0)),
                  pl.BlockSpec(AT.shape, lambda c: (0, 0)),
                  pl.BlockSpec(masks.shape, lambda c: (0, 0, 0))],
        out_specs=[pl.BlockSpec((C, GLA_HK), lambda c: (rc(c), 0)),
                   pl.BlockSpec((C, GLA_HK), lambda c: (rc(c), 0)),
                   pl.BlockSpec((C, GLA_HV), lambda c: (rc(c), 0)),
                   pl.BlockSpec((C, GLA_HK), lambda c: (rc(c), 0))],
        out_shape=[_out((S, GLA_HK), BF16), _out((S, GLA_HK), BF16), _out((S, GLA_HV), BF16),
                   _out((S, GLA_HK), F32)],
        scratch_shapes=[pltpu.VMEM((GLA_HEADS, GLA_DV, GLA_DK), F32)],
        compiler_params=_cparams(("arbitrary",)),
    )(pin, pin, pin, la, states, do, jnp.asarray(A, BF16), jnp.asarray(AT, BF16), jnp.asarray(masks))


def _gla_post_fwd(o, pin, g):
    def fn(r, p):
        ov, rv = r
        ys = []
        for h in range(GLA_HEADS):
            oh = ov[:, h * GLA_DV:(h + 1) * GLA_DV]
            rh = rv[:, h * GLA_DV:(h + 1) * GLA_DV]
            rs = lax.rsqrt(jnp.mean(oh * oh, axis=-1, keepdims=True) + RMS_EPS)
            ys.append(oh * rs * p[0] * (rh * jax.nn.sigmoid(rh)))
        return [jnp.concatenate(ys, axis=1)], []
    return _rowwise(fn, name="gla_post_fwd", rows=[(o, GLA_HV, 0), (pin, GLA_HV, (2 * GLA_HK + GLA_HV) // GLA_HV)],
                    pars=[g], outs=[(GLA_HV, BF16)])[0]


def _gla_post_bwd(dy, o, pin, g):
    def fn(r, p):
        dyv, ov, rv = r
        dos, drs, dg = [], [], 0.0
        for h in range(GLA_HEADS):
            sl = slice(h * GLA_DV, (h + 1) * GLA_DV)
            oh, rh, dyh = ov[:, sl], rv[:, sl], dyv[:, sl]
            rs = lax.rsqrt(jnp.mean(oh * oh, axis=-1, keepdims=True) + RMS_EPS)
            xh = oh * rs
            sg = jax.nn.sigmoid(rh)
            d_on = dyh * (rh * sg)
            drs.append(dyh * (xh * p[0]) * (sg * (1.0 + rh * (1.0 - sg))))
            dg = dg + _colsum(d_on * xh)
            dxh = d_on * p[0]
            dos.append(rs * (dxh - xh * jnp.mean(dxh * xh, axis=-1, keepdims=True)))
        return [jnp.concatenate(dos, axis=1), jnp.concatenate(drs, axis=1)], [dg]
    return _rowwise(fn, name="gla_post_bwd",
                    rows=[(dy, GLA_HV, 0), (o, GLA_HV, 0), (pin, GLA_HV, (2 * GLA_HK + GLA_HV) // GLA_HV)],
                    pars=[g], outs=[(GLA_HV, F32), (GLA_HV, BF16)], accs=[GLA_DV])


def _gla_gate_bwd(dla, la):
    def fn(r, p):
        dz = r[0] * (1.0 / GLA_TAU) * (1.0 - jnp.exp(GLA_TAU * r[1]))
        return [dz], [_colsum(dz)]
    return _rowwise(fn, name="gla_gate_bwd", rows=[(dla, GLA_HK, 0), (la, GLA_HK, 0)], outs=[(GLA_HK, BF16)],
                    accs=[GLA_HK])


def _log_sigmoid(z):
    return jnp.minimum(z, 0.0) - jnp.log(1.0 + jnp.exp(-jnp.abs(z)))


def _rot_half(v):
    lane = lax.broadcasted_iota(jnp.int32, v.shape, 1)
    return jnp.where(lane < 32, -pltpu.roll(v, 96, 1), jnp.where(lane < 64, pltpu.roll(v, 32, 1), 0.0))


def _rms(v):
    rs = lax.rsqrt(jnp.mean(v * v, axis=-1, keepdims=True) + RMS_EPS)
    return v * rs, rs


def _mla_norm_fwd(cin, gq, gkv, cosp, sinp):
    def fn(r, p):
        cv, cs, sn = r
        qn, _ = _rms(cv[:, :MLA_QR])
        kvn, _ = _rms(cv[:, MLA_QR:MLA_QR + MLA_KVR])
        kr = cv[:, MLA_QR + MLA_KVR:]
        return [qn * p[0], kvn * p[1], kr * cs + _rot_half(kr) * sn], []
    return _rowwise(fn, name="mla_norm_fwd", rows=[(cin, MLA_IN_PAD, 0), (cosp, 128, 0), (sinp, 128, 0)],
                    pars=[gq, gkv], outs=[(MLA_QR, BF16), (MLA_KVR, BF16), (128, BF16)])


def _mla_qrope_fwd(q, cosp, sinp):
    scale = (MLA_NOPE + MLA_ROPE) ** -0.5

    def fn(r, p):
        qv, cs, sn = r
        parts = []
        for h in range(MLA_HEADS):
            parts.append(qv[:, h * MLA_QH:h * MLA_QH + 128] * scale)
            rp = qv[:, h * MLA_QH + 128:(h + 1) * MLA_QH]
            parts.append((rp * cs + _rot_half(rp) * sn) * scale)
        return [jnp.concatenate(parts, axis=1)], []
    W = MLA_HEADS * MLA_QH
    return _rowwise(fn, name="mla_qrope_fwd", rows=[(q, W, 0), (cosp, 128, 0), (sinp, 128, 0)],
                    outs=[(W, BF16)])[0]


def _mla_qrope_bwd(dq, cosp, sinp):
    scale = (MLA_NOPE + MLA_ROPE) ** -0.5

    def fn(r, p):
        dv, cs, sn = r
        parts = []
        for h in range(MLA_HEADS):
            parts.append(dv[:, h * MLA_QH:h * MLA_QH + 128] * scale)
            rp = dv[:, h * MLA_QH + 128:(h + 1) * MLA_QH]
            parts.append((rp * cs - _rot_half(rp) * sn) * scale)
        return [jnp.concatenate(parts, axis=1)], []
    W = MLA_HEADS * MLA_QH
    return _rowwise(fn, name="mla_qrope_bwd", rows=[(dq, W, 0), (cosp, 128, 0), (sinp, 128, 0)],
                    outs=[(W, BF16)])[0]


def _mla_norm_bwd(cin, dqn, dkvn, dkr, gq, gkv, cosp, sinp):
    def fn(r, p):
        cv, dq_, dkv_, dkr_, cs, sn = r
        outs, accs = [], []
        for (lo, hi), dn, g in (((0, MLA_QR), dq_, p[0]), ((MLA_QR, MLA_QR + MLA_KVR), dkv_, p[1])):
            xh, rs = _rms(cv[:, lo:hi])
            dxh = dn * g
            outs.append(rs * (dxh - xh * jnp.mean(dxh * xh, axis=-1, keepdims=True)))
            accs.append(_colsum(dn * xh))
        dk = dkr_[:, 0:128]
        for h in range(1, MLA_HEADS):
            dk = dk + dkr_[:, h * 128:(h + 1) * 128]
        outs.append(dk * cs - _rot_half(dk) * sn)
        return [jnp.concatenate(outs, axis=1)], accs
    return _rowwise(fn, name="mla_norm_bwd",
                    rows=[(cin, MLA_IN_PAD, 0), (dqn, MLA_QR, 0), (dkvn, MLA_KVR, 0), (dkr, MLA_HEADS * 128, 0),
                          (cosp, 128, 0), (sinp, 128, 0)],
                    pars=[gq, gkv], outs=[(MLA_IN_PAD, BF16)], accs=[MLA_QR, MLA_KVR])


def _mla_probs(q, k, i, tq):
    s = _nt(q, k)
    row = (i * tq + lax.broadcasted_iota(jnp.int32, s.shape, 0)) // CHUNK
    col = lax.broadcasted_iota(jnp.int32, s.shape, 1) // CHUNK
    s = jnp.where(col <= row, s, -jnp.inf)
    e = jnp.exp(s - jnp.max(s, axis=-1, keepdims=True))
    return e / jnp.sum(e, axis=-1, keepdims=True)


def _mla_attn_fwd(qr, knv, kr, tq=256):
    S = qr.shape[0]
    tq = min(tq, S)

    def body(q_ref, kn_ref, v_ref, kr_ref, o_ref, k_cat):
        k_cat[:, :128] = kn_ref[...]
        k_cat[:, 128:] = kr_ref[...]
        for i in range(S // tq):
            rows, keys = pl.ds(i * tq, tq), pl.ds(0, (i + 1) * tq)
            pr = _mla_probs(q_ref[rows, :], k_cat[keys, :], i, tq)
            o_ref[rows, :] = _nn(pr.astype(BF16), v_ref[keys, :])

    return pl.pallas_call(
        body, name="mla_attn_fwd", grid=(MLA_HEADS,),
        in_specs=[pl.BlockSpec((S, MLA_QH), lambda h: (0, h)),
                  pl.BlockSpec((S, 128), lambda h: (0, h)),
                  pl.BlockSpec((S, 128), lambda h: (0, MLA_HEADS + h)),
                  pl.BlockSpec((S, 128), lambda h: (0, 0))],
        out_specs=pl.BlockSpec((S, 128), lambda h: (0, h)),
        out_shape=_out((S, MLA_HEADS * MLA_V), F32),
        scratch_shapes=[pltpu.VMEM((S, MLA_QH), BF16)],
        compiler_params=_cparams(("parallel",)),
    )(qr, knv, knv, kr)


def _mla_attn_bwd(qr, knv, kr, o, do, tq=256):
    S = qr.shape[0]
    tq = min(tq, S)
    W = MLA_HEADS * 128

    def body(q_ref, kn_ref, v_ref, kr_ref, o_ref, do_ref, dq_ref, dkn_ref, dv_ref, dkr_ref, k_cat, dk_acc, dv_acc):
        k_cat[:, :128] = kn_ref[...]
        k_cat[:, 128:] = kr_ref[...]
        dk_acc[...] = jnp.zeros(dk_acc.shape, F32)
        dv_acc[...] = jnp.zeros(dv_acc.shape, F32)
        for i in range(S // tq):
            rows, keys = pl.ds(i * tq, tq), pl.ds(0, (i + 1) * tq)
            q, k, v = q_ref[rows, :], k_cat[keys, :], v_ref[keys, :]
            pr = _mla_probs(q, k, i, tq)
            dov = do_ref[rows, :]
            delta = jnp.sum(dov * o_ref[rows, :], axis=-1, keepdims=True)
            dob = dov.astype(BF16)
            ds = (pr * (_nt(dob, v) - delta)).astype(BF16)
            dq_ref[rows, :] = _nn(ds, k)
            dk_acc[keys, :] += _tn(ds, q)
            dv_acc[keys, :] += _tn(pr.astype(BF16), dob)
        dkn_ref[...] = dk_acc[:, :128].astype(dkn_ref.dtype)
        dkr_ref[...] = dk_acc[:, 128:]
        dv_ref[...] = dv_acc[...].astype(dv_ref.dtype)

    head = lambda w: pl.BlockSpec((S, w), lambda h: (0, h))
    return pl.pallas_call(
        body, name="mla_attn_bwd", grid=(MLA_HEADS,),
        in_specs=[head(MLA_QH), head(128), pl.BlockSpec((S, 128), lambda h: (0, MLA_HEADS + h)),
                  pl.BlockSpec((S, 128), lambda h: (0, 0)), head(128), head(128)],
        out_specs=[head(MLA_QH), head(128), head(128), head(128)],
        out_shape=[_out((S, MLA_HEADS * MLA_QH), F32), _out((S, W), BF16), _out((S, W), BF16), _out((S, W), F32)],
        scratch_shapes=[pltpu.VMEM((S, MLA_QH), BF16), pltpu.VMEM((S, MLA_QH), F32), pltpu.VMEM((S, 128), F32)],
        compiler_params=_cparams(("parallel",)),
    )(qr, knv, knv, kr, o, do)


CONV_TILE = 256


def _shift_down(v, n):
    row = lax.broadcasted_iota(jnp.int32, v.shape, 0)
    return jnp.where(row >= n, pltpu.roll(v, n, 0), 0.0)


def _shift_up(v, n):
    S = v.shape[0]
    row = lax.broadcasted_iota(jnp.int32, v.shape, 0)
    return jnp.where(row < S - n, pltpu.roll(v, S - n, 0), 0.0)


def _conv_specs(S, n_extra_cols):
    nt = D_MODEL // CONV_TILE
    specs = [pl.BlockSpec((S, CONV_TILE), functools.partial(lambda j, o: (0, o + j), o=part * nt))
             for part in range(3)]
    specs.append(pl.BlockSpec((8, CONV_TILE), lambda j: (0, j)))
    specs += [pl.BlockSpec((S, CONV_TILE), lambda j: (0, j)) for _ in range(n_extra_cols)]
    return specs


def _conv_fwd(bcu, w8):
    S = bcu.shape[0]

    def body(b_ref, c_ref, u_ref, w_ref, y_ref):
        cu = c_ref[...] * u_ref[...]
        z = w_ref[2:3, :] * cu + w_ref[1:2, :] * _shift_down(cu, 1) + w_ref[0:1, :] * _shift_down(cu, 2)
        y_ref[...] = (b_ref[...] * z).astype(y_ref.dtype)

    return pl.pallas_call(
        body, name="conv_fwd", grid=(D_MODEL // CONV_TILE,), in_specs=_conv_specs(S, 0),
        out_specs=pl.BlockSpec((S, CONV_TILE), lambda j: (0, j)),
        out_shape=_out((S, D_MODEL), BF16),
        compiler_params=_cparams(("parallel",)),
    )(bcu, bcu, bcu, w8)


def _conv_bwd(bcu, w8, dy):
    S = bcu.shape[0]

    def body(b_ref, c_ref, u_ref, w_ref, dy_ref, db_ref, dc_ref, du_ref, dw_ref):
        b, c, u, dyv = b_ref[...], c_ref[...], u_ref[...], dy_ref[...]
        w0, w1, w2 = w_ref[0:1, :], w_ref[1:2, :], w_ref[2:3, :]
        cu = c * u
        cu1, cu2 = _shift_down(cu, 1), _shift_down(cu, 2)
        z = w2 * cu + w1 * cu1 + w0 * cu2
        dz = dyv * b
        db_ref[...] = (dyv * z).astype(db_ref.dtype)
        dcu = w2 * dz + w1 * _shift_up(dz, 1) + w0 * _shift_up(dz, 2)
        dc_ref[...] = (dcu * u).astype(dc_ref.dtype)
        du_ref[...] = (dcu * c).astype(du_ref.dtype)
        dw_ref[...] = jnp.zeros(dw_ref.shape, F32)
        dw_ref[0:1, :] = _colsum(dz * cu2)
        dw_ref[1:2, :] = _colsum(dz * cu1)
        dw_ref[2:3, :] = _colsum(dz * cu)

    col = pl.BlockSpec((S, CONV_TILE), lambda j: (0, j))
    return pl.pallas_call(
        body, name="conv_bwd", grid=(D_MODEL // CONV_TILE,), in_specs=_conv_specs(S, 1),
        out_specs=[col, col, col, pl.BlockSpec((8, CONV_TILE), lambda j: (0, j))],
        out_shape=[_out((S, D_MODEL), BF16)] * 3 + [_out((8, D_MODEL), F32)],
        compiler_params=_cparams(("parallel",)),
    )(bcu, bcu, bcu, w8, dy)


def _adamw(w, m, v, gs, name):
    L, R, Cn = w.shape
    assert len(gs) == L
    tr = R if R <= 256 else 256
    assert R % tr == 0

    def body(w_ref, m_ref, v_ref, *rest):
        g_refs, (go_ref, d_ref, nm_ref, nv_ref) = rest[:L], rest[L:]
        layer = pl.program_id(0)
        gv = g_refs[0][...]
        for k in range(1, L):
            gv = jnp.where(layer == k, g_refs[k][...], gv)
        nm = ADAM_B1 * m_ref[...] + (1.0 - ADAM_B1) * gv
        nv = ADAM_B2 * v_ref[...] + (1.0 - ADAM_B2) * jnp.square(gv)
        m_hat = nm / (1.0 - ADAM_B1 ** ADAM_STEP)
        v_hat = nv / (1.0 - ADAM_B2 ** ADAM_STEP)
        d_ref[...] = -ADAM_LR * (m_hat / (jnp.sqrt(v_hat) + ADAM_EPS) + ADAM_WD * w_ref[...])
        go_ref[...] = gv
        nm_ref[...] = nm
        nv_ref[...] = nv

    spec = pl.BlockSpec((None, tr, Cn), lambda l, i: (l, i, 0))
    g_specs = [pl.BlockSpec((tr, Cn), functools.partial(lambda l, i, k: (jnp.where(l == k, i, 0), 0), k=k))
               for k in range(L)]
    return pl.pallas_call(
        body, name=name, grid=(L, R // tr), in_specs=[spec] * 3 + g_specs, out_specs=[spec] * 4,
        out_shape=[jax.ShapeDtypeStruct((L, R, Cn), F32)] * 4,
        compiler_params=_cparams(("arbitrary", "arbitrary")),
    )(w, m, v, *gs)


HBM_SPEC = pl.BlockSpec(memory_space=pltpu.HBM)
BOUNCE_ROWS = 256


def _place():
    return lax.axis_index("x"), lax.axis_index("y"), lax.axis_index("c")


def _other_chips(x, y):
    return [(1 - x, y), (x, 1 - y), (1 - x, 1 - y)]


def _copy_via_vmem(src, dst, buf, sems, rows):
    ch = buf.shape[1]
    n = rows // ch
    cin = lambda i: pltpu.make_async_copy(src.at[pl.ds(i * ch, ch), :], buf.at[i % 2], sems.at[i % 2])
    cout = lambda i: pltpu.make_async_copy(buf.at[i % 2], dst.at[pl.ds(i * ch, ch), :], sems.at[2 + i % 2])
    cin(0).start()
    for i in range(n):
        cin(i).wait()
        cout(i).start()
        if i + 1 < n:
            if i >= 1:
                cout(i - 1).wait()
            cin(i + 1).start()
    if n >= 2:
        cout(n - 2).wait()
    cout(n - 1).wait()


SEM_SPEC = pl.BlockSpec(memory_space=pltpu.SEMAPHORE)
ANY_SPEC = pl.BlockSpec(memory_space=pl.ANY)
VMEM_SPEC = pl.BlockSpec(memory_space=pltpu.VMEM)
EFFECT = pltpu.SideEffectType.DATAFLOW_SIDE_EFFECTING
TOKEN = (8, 128)


def _ici_start(srcs, lands, after, copies, name, per_src=3):
    n, nl = len(srcs), len(lands)

    def body(*refs):
        src_refs, land_refs = refs[:n], refs[n:n + nl]
        send_sems, recv_sems, token = refs[n + nl + 1], refs[n + nl + 2], refs[-1]
        x, y, c = _place()
        for k, src, dst, to in copies(src_refs, land_refs, x, y, c):
            pltpu.make_async_remote_copy(src_ref=src, dst_ref=dst, send_sem=send_sems.at[k], recv_sem=recv_sems.at[k],
                                         device_id=to, device_id_type=MESH).start()
        token[...] = jnp.zeros(TOKEN, F32)

    n_copies = per_src * n
    res = pl.pallas_call(
        body, name=name,
        out_shape=(pltpu.SemaphoreType.DMA((n_copies,)), pltpu.SemaphoreType.DMA((n_copies,)),
                   *[pltpu.HBM(s.shape, s.dtype) for s in srcs], *[pltpu.HBM(l.shape, l.dtype) for l in lands],
                   jax.ShapeDtypeStruct(TOKEN, F32)),
        in_specs=[HBM_SPEC] * (n + nl) + [ANY_SPEC],
        out_specs=(SEM_SPEC, SEM_SPEC, *[HBM_SPEC] * (n + nl), VMEM_SPEC),
        input_output_aliases={t: 2 + t for t in range(n + nl)},
        compiler_params=pltpu.CompilerParams(has_side_effects=EFFECT),
    )(*[_hbm(s) for s in srcs], *[_hbm(l) for l in lands], after)
    return res[0], res[1], list(res[2:2 + n]), list(res[2 + n:2 + n + nl]), res[-1]


def _ici_wait(handle, after, copies, name):
    send_sems, recv_sems, srcs, lands, _ = handle
    n, nl = len(srcs), len(lands)

    def body(*refs):
        src_refs, land_refs = refs[:n], refs[n:n + nl]
        send_s, recv_s = refs[n + nl], refs[n + nl + 1]
        x, y, c = _place()
        for k, src, dst, to in copies(src_refs, land_refs, x, y, c):
            cp = pltpu.make_async_remote_copy(src_ref=src, dst_ref=dst, send_sem=send_s.at[k], recv_sem=recv_s.at[k],
                                              device_id=to, device_id_type=MESH)
            cp.wait_send()
            cp.wait_recv()

    res = pl.pallas_call(
        body, name=name,
        out_shape=(*[pltpu.HBM(s.shape, s.dtype) for s in srcs], *[pltpu.HBM(l.shape, l.dtype) for l in lands]),
        in_specs=[HBM_SPEC] * (n + nl) + [SEM_SPEC, SEM_SPEC, ANY_SPEC],
        out_specs=tuple([HBM_SPEC] * (n + nl)),
        input_output_aliases={t: t for t in range(n + nl)},
        compiler_params=pltpu.CompilerParams(has_side_effects=EFFECT),
    )(*srcs, *lands, send_sems, recv_sems, after)
    return list(res[:n]), list(res[n:])


def _gather_copies(halves):
    def copies(src_refs, land_refs, x, y, c):
        q = 2 * x + y
        out = []
        for t, H in enumerate(halves):
            for j, (cx, cy) in enumerate(_other_chips(x, y)):
                out.append((3 * t + j, src_refs[t].at[pl.ds(c * H, H), :], land_refs[t].at[q, pl.ds(c * H, H), :],
                            (cx, cy, c)))
        return out
    return copies


def _gather_wait_copies(halves):
    def copies(src_refs, land_refs, x, y, c):
        out = []
        for t, H in enumerate(halves):
            for j, (cx, cy) in enumerate(_other_chips(x, y)):
                out.append((3 * t + j, src_refs[t].at[pl.ds(c * H, H), :],
                            land_refs[t].at[2 * cx + cy, pl.ds(c * H, H), :], (cx, cy, c)))
        return out
    return copies


def _gather_start(ops, after, name):
    lands = [lax.empty((N_CHIPS,) + o.shape, o.dtype) for o in ops]
    return _ici_start(ops, lands, after, _gather_copies([o.shape[0] // 2 for o in ops]), name)


def _gather_wait(handle, after, name):
    halves = [s.shape[0] // 2 for s in handle[2]]
    return _ici_wait(handle, after, _gather_wait_copies(halves), name)


def _gather_finish(ops, lands, name):
    n = len(ops)
    halves = [o.shape[0] // 2 for o in ops]
    chunk = [min(o.shape[0], BOUNCE_ROWS) for o in ops]

    def body(*refs):
        in_refs, out_refs = refs[:n], refs[2 * n:3 * n]
        send_sems, recv_sems, local_sems = refs[3 * n:3 * n + 3]
        bufs = refs[3 * n + 3:]
        x, y, c = _place()
        q = 2 * x + y
        chips = _other_chips(x, y)
        sibling = (x, y, 1 - c)

        def copy(t, j, half):
            land = out_refs[t].at[2 * chips[j][0] + chips[j][1], pl.ds(half * halves[t], halves[t]), :]
            return pltpu.make_async_remote_copy(src_ref=land, dst_ref=land, send_sem=send_sems.at[3 * t + j],
                                                recv_sem=recv_sems.at[3 * t + j], device_id=sibling,
                                                device_id_type=MESH)

        passed = [copy(t, j, c) for t in range(n) for j in range(3)]
        for cp in passed:
            cp.start()
        for t in range(n):
            _copy_via_vmem(in_refs[t], out_refs[t].at[q], bufs[t], local_sems, ops[t].shape[0])
        for t in range(n):
            for j in range(3):
                copy(t, j, 1 - c).wait_recv()
        for cp in passed:
            cp.wait_send()

    return pl.pallas_call(
        body, name=name, in_specs=[HBM_SPEC] * (2 * n), out_specs=[HBM_SPEC] * n,
        out_shape=[jax.ShapeDtypeStruct(l.shape, l.dtype) for l in lands],
        input_output_aliases={n + t: t for t in range(n)},
        scratch_shapes=[pltpu.SemaphoreType.DMA((3 * n,)), pltpu.SemaphoreType.DMA((3 * n,)),
                        pltpu.SemaphoreType.DMA((4,))]
        + [pltpu.VMEM((2, chunk[t], ops[t].shape[1]), ops[t].dtype) for t in range(n)],
        compiler_params=pltpu.CompilerParams(vmem_limit_bytes=VMEM_LIMIT),
    )(*ops, *lands)


def _swap_halves(ops, name):
    n = len(ops)

    def body(*refs):
        in_refs, out_refs, send_sems, recv_sems = refs[:n], refs[n:2 * n], refs[2 * n], refs[2 * n + 1]
        x, y, c = _place()
        cps = []
        for t in range(n):
            H = ops[t].shape[1] // 2
            cp = pltpu.make_async_remote_copy(src_ref=in_refs[t].at[:, pl.ds((1 - c) * H, H), :],
                                              dst_ref=out_refs[t], send_sem=send_sems.at[t],
                                              recv_sem=recv_sems.at[t], device_id=(x, y, 1 - c),
                                              device_id_type=MESH)
            cp.start()
            cps.append(cp)
        for cp in cps:
            cp.wait()

    return pl.pallas_call(
        body, name=name, in_specs=[HBM_SPEC] * n, out_specs=[HBM_SPEC] * n,
        out_shape=[jax.ShapeDtypeStruct((N_CHIPS, o.shape[1] // 2, o.shape[2]), o.dtype) for o in ops],
        scratch_shapes=[pltpu.SemaphoreType.DMA((n,)), pltpu.SemaphoreType.DMA((n,))],
    )(*ops)


def _sum_rows_tile(h):
    return h if h <= 512 else 512


def _pair_sum(g, t, cq, name):
    _, a, b = g.shape
    H = a // 2
    tr = _sum_rows_tile(H)

    def body(cq_ref, g_ref, t_ref, o_ref):
        o_ref[...] = (g_ref[...].astype(F32) + t_ref[...].astype(F32)).astype(o_ref.dtype)

    grid_spec = pltpu.PrefetchScalarGridSpec(
        num_scalar_prefetch=1, grid=(N_CHIPS, H // tr),
        in_specs=[pl.BlockSpec((None, None, tr, b), lambda j, i, cq_ref: (j, cq_ref[0], i, 0)),
                  pl.BlockSpec((None, tr, b), lambda j, i, cq_ref: (j, i, 0))],
        out_specs=pl.BlockSpec((None, tr, b), lambda j, i, cq_ref: (j, i, 0)))
    return pl.pallas_call(
        body, name=name, grid_spec=grid_spec, out_shape=_out(t.shape, BF16),
        compiler_params=_cparams(("parallel", "parallel")),
    )(cq, g.reshape(N_CHIPS, 2, H, b), t)


def _scatter_copies(src_refs, land_refs, x, y, c):
    out = []
    for j, (cx, cy) in enumerate(_other_chips(x, y)):
        for t in range(len(src_refs)):
            out.append((3 * t + j, src_refs[t].at[2 * cx + cy], land_refs[t].at[j], (cx, cy, c)))
    return out


def _scatter_start(ops, after, name):
    lands = [lax.empty((3,) + o.shape[1:], o.dtype) for o in ops]
    return _ici_start(ops, lands, after, _scatter_copies, name)


def _scatter_wait(handle, after, name):
    return _ici_wait(handle, after, _scatter_copies, name)


def _chip_sum(p, t, cq, name):
    _, H, b = p.shape
    tr = _sum_rows_tile(H)

    def body(cq_ref, p_ref, t_ref, o_ref):
        acc = p_ref[...].astype(F32)
        for j in range(3):
            acc = acc + t_ref[j].astype(F32)
        o_ref[...] = acc

    grid_spec = pltpu.PrefetchScalarGridSpec(
        num_scalar_prefetch=1, grid=(H // tr,),
        in_specs=[pl.BlockSpec((None, tr, b), lambda i, cq_ref: (cq_ref[1], i, 0)),
                  pl.BlockSpec((3, tr, b), lambda i, cq_ref: (0, i, 0))],
        out_specs=pl.BlockSpec((None, tr, b), lambda i, cq_ref: (cq_ref[0], i, 0)))
    out = pl.pallas_call(
        body, name=name, grid_spec=grid_spec, out_shape=_out((2, H, b), F32),
        compiler_params=_cparams(("parallel",)),
    )(cq, p, t)
    return out.reshape(2 * H, b)


def _join_halves(ops, name):
    n = len(ops)

    def body(*refs):
        out_refs, send_sems, recv_sems = refs[n:2 * n], refs[2 * n], refs[2 * n + 1]
        x, y, c = _place()
        cps = []
        for t in range(n):
            H = ops[t].shape[0] // 2
            mine = out_refs[t].at[pl.ds(c * H, H), :]
            cp = pltpu.make_async_remote_copy(src_ref=mine, dst_ref=mine, send_sem=send_sems.at[t],
                                              recv_sem=recv_sems.at[t], device_id=(x, y, 1 - c),
                                              device_id_type=MESH)
            cp.start()
            cps.append(cp)
        for t in range(n):
            H = ops[t].shape[0] // 2
            other = out_refs[t].at[pl.ds((1 - c) * H, H), :]
            pltpu.make_async_remote_copy(src_ref=other, dst_ref=other, send_sem=send_sems.at[t],
                                         recv_sem=recv_sems.at[t], device_id=(x, y, 1 - c),
                                         device_id_type=MESH).wait_recv()
        for cp in cps:
            cp.wait_send()

    return pl.pallas_call(
        body, name=name, in_specs=[HBM_SPEC] * n, out_specs=[HBM_SPEC] * n,
        out_shape=[jax.ShapeDtypeStruct(o.shape, o.dtype) for o in ops],
        input_output_aliases={t: t for t in range(n)},
        scratch_shapes=[pltpu.SemaphoreType.DMA((n,)), pltpu.SemaphoreType.DMA((n,))],
    )(*ops)


def _direct_copies(src_refs, land_refs, x, y, c):
    out = []
    for t in range(len(src_refs)):
        H = src_refs[t].shape[1] // 2
        for k in range(1, 8):
            px, py, pc = x ^ (k >> 2), y ^ ((k >> 1) & 1), c ^ (k & 1)
            out.append((7 * t + k - 1, src_refs[t].at[2 * px + py, pl.ds(pc * H, H), :], land_refs[t].at[k - 1],
                        (px, py, pc)))
    return out


def _direct_sum(g, t, cq, name):
    _, a, b = g.shape
    H = a // 2
    tr = _sum_rows_tile(H)

    def body(cq_ref, g_ref, t_ref, o_ref):
        acc = g_ref[...].astype(F32)
        for k in range(7):
            acc = acc + t_ref[k].astype(F32)
        o_ref[...] = acc

    grid_spec = pltpu.PrefetchScalarGridSpec(
        num_scalar_prefetch=1, grid=(H // tr,),
        in_specs=[pl.BlockSpec((None, None, tr, b), lambda i, cq_ref: (cq_ref[1], cq_ref[0], i, 0)),
                  pl.BlockSpec((7, tr, b), lambda i, cq_ref: (0, i, 0))],
        out_specs=pl.BlockSpec((None, tr, b), lambda i, cq_ref: (cq_ref[0], i, 0)))
    out = pl.pallas_call(
        body, name=name, grid_spec=grid_spec, out_shape=_out((2, H, b), F32),
        compiler_params=_cparams(("parallel",)),
    )(cq, g.reshape(N_CHIPS, 2, H, b), t)
    return out.reshape(a, b)


def _reduce_direct_start(gs, tag):
    lands = [lax.empty((7, g.shape[1] // 2, g.shape[2]), g.dtype) for g in gs]
    return _ici_start(gs, lands, jnp.zeros(TOKEN, F32), _direct_copies, "rs_direct_start_" + tag, per_src=7)


def _reduce_direct_finish(handle, cq, after, tag):
    gs, rs = _ici_wait(handle, after, _direct_copies, "rs_direct_wait_" + tag)
    fs = [_direct_sum(g, r, cq, "rs_direct_sum") for g, r in zip(gs, rs)]
    return _join_halves(fs, "rs_join_" + tag)


def _reduce_scatter_start(gs, cq, after, tag):
    ts = _swap_halves(gs, "rs_swap_" + tag)
    ps = [_pair_sum(g, t, cq, "rs_pair_sum") for g, t in zip(gs, ts)]
    return _scatter_start(ps, after, "rs_scatter_start_" + tag)


def _reduce_scatter_finish(handle, cq, after, tag):
    ps, rs = _scatter_wait(handle, after, "rs_scatter_wait_" + tag)
    fs = [_chip_sum(p, r, cq, "rs_chip_sum") for p, r in zip(ps, rs)]
    return _join_halves(fs, "rs_join_" + tag)


def _all_reduce_small(v):
    n = v.shape[0]

    def body(v_ref, out_ref, buf, send_sems, recv_sems):
        x, y, c = _place()
        me = 4 * x + 2 * y + c
        buf[me] = v_ref[...]
        cps = []
        for k in range(1, 8):
            peer = (x ^ (k >> 2), y ^ ((k >> 1) & 1), c ^ (k & 1))
            cp = pltpu.make_async_remote_copy(src_ref=v_ref, dst_ref=buf.at[me], send_sem=send_sems.at[k - 1],
                                              recv_sem=recv_sems.at[k - 1], device_id=peer, device_id_type=MESH)
            cp.start()
            cps.append(cp)
        for k in range(1, 8):
            px, py, pc = x ^ (k >> 2), y ^ ((k >> 1) & 1), c ^ (k & 1)
            land = buf.at[4 * px + 2 * py + pc]
            pltpu.make_async_remote_copy(src_ref=land, dst_ref=land, send_sem=send_sems.at[k - 1],
                                         recv_sem=recv_sems.at[k - 1], device_id=(px, py, pc),
                                         device_id_type=MESH).wait_recv()
        for cp in cps:
            cp.wait_send()
        acc = buf[0]
        for d in range(1, 8):
            acc = acc + buf[d]
        out_ref[...] = acc

    vm = pl.BlockSpec(memory_space=pltpu.VMEM)
    return pl.pallas_call(
        body, name="all_reduce_small", in_specs=[vm], out_specs=vm,
        out_shape=jax.ShapeDtypeStruct((n, 128), F32),
        scratch_shapes=[pltpu.VMEM((8, n, 128), F32), pltpu.SemaphoreType.DMA((7,)), pltpu.SemaphoreType.DMA((7,))],
    )(v)


SMALL_GATHER = (16, 1024)
SMALL_FULL = sum(_size(_full_shape(n)) for n in SMALL)
SMALL_FULL_ROWS = -(-SMALL_FULL // 128 // 8) * 8


def _layer_shards(w, i, q):
    kind, j = MIXER[i % 3], i // 3
    out = {n: w[n][i].astype(BF16) for n in COMMON_BIG}
    if kind == 'gla':
        win = jnp.zeros((D_MODEL, GLA_WIN), F32)
        win = lax.dynamic_update_slice(win, w['gla_w_in'][j], (0, (GLA_SHARD - GLA_WIN_STEP) * q))
        out['gla_w_in'] = win.astype(BF16)
        out['gla_w_out'] = w['gla_w_out'][j].astype(BF16)
    elif kind == 'mla':
        out['mla_w_in'] = jnp.pad(w['mla_w_in'][j], ((0, 0), (0, MLA_IN_PAD - MLA_IN))).astype(BF16)
        for n in ('mla_w_uq', 'mla_w_ukv', 'mla_w_out'):
            out[n] = w[n][j].astype(BF16)
    else:
        out['conv_w_in'] = w['conv_w_in'][j].astype(BF16)
        out['conv_w_out'] = w['conv_w_out'][j].astype(BF16)
    return out


def _rows_joined(g):
    return g.reshape(g.shape[0] * g.shape[1], g.shape[2])


def _cols_joined(g):
    return jnp.moveaxis(g, 0, 1).reshape(g.shape[1], -1)


def _layer_weights(g, i):
    kind = MIXER[i % 3]
    W = {}
    if 'mlp_w1' in g:
        W = {'w1': g['mlp_w1'], 'w2': _rows_joined(g['mlp_w2']), 'gate': _rows_joined(g['ple_w_gate']),
             'proj': g['ple_w_proj']}
    if kind == 'gla' and 'gla_w_out' in g:
        W['w_out'] = _rows_joined(g['gla_w_out'])
    if kind == 'gla' and 'gla_w_in' in g:
        parts = []
        for qq in range(N_CHIPS):
            lo = g['gla_w_in'][qq][:, :128]
            if qq > 0:
                lo = lo + g['gla_w_in'][qq - 1][:, GLA_WIN_STEP:]
            parts += [lo, g['gla_w_in'][qq][:, 128:GLA_WIN_STEP]]
        parts.append(g['gla_w_in'][N_CHIPS - 1][:, GLA_WIN_STEP:])
        W['w_in'] = jnp.concatenate(parts, axis=1)
    elif kind == 'mla':
        W['w_in'] = _rows_joined(g['mla_w_in'])
        uq = _cols_joined(g['mla_w_uq']).reshape(MLA_QR, MLA_HEADS, MLA_NOPE + MLA_ROPE)
        W['w_uq'] = jnp.pad(uq, ((0, 0), (0, 0), (0, MLA_QH - MLA_NOPE - MLA_ROPE))).reshape(MLA_QR, -1)
        ukv = _cols_joined(g['mla_w_ukv']).reshape(MLA_KVR, MLA_HEADS, 2, 128)
        W['w_ukv'] = ukv.transpose(0, 2, 1, 3).reshape(MLA_KVR, -1)
        W['w_out'] = _rows_joined(g['mla_w_out'])
    elif kind == 'conv':
        W['w_in'] = g['conv_w_in']
        W['w_out'] = _rows_joined(g['conv_w_out'])
    return W


def _pack_small_shards(w):
    flat = jnp.concatenate([w[n].reshape(-1) for n in SMALL_SHARDED])
    return jnp.pad(flat, (0, _size(SMALL_GATHER) - flat.shape[0])).reshape(SMALL_GATHER)


def _unpack_small_gathered(g):
    flat, out, off = g.reshape(N_CHIPS, -1), {}, 0
    for n in SMALL_SHARDED:
        shape, ax = WSPEC[n]
        seg = flat[:, off:off + _size(shape)].reshape((N_CHIPS,) + shape)
        out[n] = jnp.moveaxis(seg, 0, ax).reshape(_full_shape(n))
        off += _size(shape)
    return out


def _pack_small(vals):
    flat = jnp.concatenate([vals[n].reshape(-1) for n in SMALL])
    return jnp.pad(flat, (0, SMALL_FULL_ROWS * 128 - flat.shape[0])).reshape(SMALL_FULL_ROWS, 128)


def _unpack_small(packed, q):
    flat = packed.reshape(-1)
    out, off = {}, 0
    for n in SMALL:
        shape, ax = WSPEC[n]
        full = flat[off:off + _size(_full_shape(n))].reshape(_full_shape(n))
        off += _size(_full_shape(n))
        out[n] = full if ax is None else lax.dynamic_slice_in_dim(full, q * shape[ax], shape[ax], axis=ax)
    return out


def _row_shards(dw):
    return dw.reshape(N_CHIPS, dw.shape[0] // N_CHIPS, dw.shape[1])


def _col_shards(dw):
    return jnp.moveaxis(dw.reshape(dw.shape[0], N_CHIPS, -1), 1, 0)


def _row(v):
    return v.reshape(1, -1)


def _layer_fwd(i, xin, xin_b, p_i, W, sm, cosp, sinp, rest=None):
    kind, j = MIXER[i % 3], i // 3
    sv = {'xin': xin, 'xin_b': xin_b}
    if kind == 'gla':
        w_up = jnp.pad(sm['gla_w_gate_up'][j].astype(BF16), ((0, 128 - GLA_RANK), (0, 0)))
        pin = _mm(xin_b, W['w_in'], name="gla_in", tn=640, tm=FULL_ROWS)
        la = _mm(pin, w_up, name="gla_gate", K=128, tk=128, a_off=(0, (GLA_IN_PAD - 128) // 128), tn=512,
                 extras=[(_row(sm['gla_b_gate'][j]), 'n')],
                 epilogue=lambda acc, b: (_log_sigmoid(acc + b) * (1.0 / GLA_TAU),))
        o, states = _gla_fwd(pin, la)
        yb = _gla_post_fwd(o, pin, _row(sm['gla_norm_g'][j]))
        if rest is not None:
            W = {**W, **rest(yb)}
        mixed = yb
        sv.update(w_up=w_up, pin=pin, la=la, o=o, states=states, yb=yb)
    elif kind == 'mla':
        gq, gkv = sm['mla_q_norm'][j:j + 1], sm['mla_kv_norm'][j:j + 1]
        cin = _mm(xin_b, W['w_in'], name="mla_in", tn=640, tm=FULL_ROWS)
        qn, kvn, kr = _mla_norm_fwd(cin, gq, gkv, cosp, sinp)
        qr = _mla_qrope_fwd(_mm(qn, W['w_uq'], name="mla_uq"), cosp, sinp)
        knv = _mm(kvn, W['w_ukv'], name="mla_ukv", out_dtypes=(BF16,))
        o = _mla_attn_fwd(qr, knv, kr)
        ob = o.astype(BF16)
        mixed = ob
        sv.update(gq=gq, gkv=gkv, cin=cin, qn=qn, kvn=kvn, kr=kr, qr=qr, knv=knv, o=o, ob=ob)
    else:
        w8 = jnp.pad(sm['conv_w'][j], ((0, 5), (0, 0)))
        bcu = _mm(xin_b, W['w_in'], name="conv_in", tn=768, b_sh=True, tm=FULL_ROWS)
        yb = _conv_fwd(bcu, w8)
        mixed = yb
        sv.update(w8=w8, bcu=bcu, yb=yb)
    g0, b0 = _row(sm['ln_g'][i, 0]), _row(sm['ln_b'][i, 0])
    g1, b1 = _row(sm['ln_g'][i, 1]), _row(sm['ln_b'][i, 1])
    ln = dict(tm=512, tn=D_MODEL, out_dtypes=(F32, BF16, F32), epilogue=_ln_fwd_epilogue)
    x1, x1b, v0 = _mm(mixed, W['w_out'], name="mix_out_ln", extras=[(xin, 'mn'), (g0, 'n'), (b0, 'n')], **ln)
    ab = _mm(x1b, W['w1'], name="mlp_up", out_dtypes=(BF16,), b_sh=True, tm=FULL_ROWS,
             epilogue=lambda acc: (jnp.square(jnp.maximum(acc, 0.0)),))
    x2, x2b, v1 = _mm(ab, W['w2'], name="mlp_down_ln", tk=D_FF, extras=[(x1, 'mn'), (g1, 'n'), (b1, 'n')], **ln)
    pp = _mm(p_i, W['proj'], name="ple_proj", tn=256, b_sh=True)
    z, x3, x3b = _mm(x2b, W['gate'], name="ple_gate", out_dtypes=(F32, F32, BF16),
                     extras=[(x2, 'mn'), (pp, 'mn')],
                     epilogue=lambda acc, xv, pv: (acc,) + (xv + jax.nn.sigmoid(acc) * pv,) * 2)
    sv.update(v0=v0, x1b=x1b, ab=ab, v1=v1, x2b=x2b, pp=pp, z=z, g0=g0, g1=g1)
    return x3, x3b, sv, W


def _layer_bwd(i, dx, p_i, W, sm, sv, cosp, sinp, token, early=None):
    kind, j = MIXER[i % 3], i // 3
    big, small = {}, {}
    dpp_b, dz_b = _ple_bwd_gate(dx, sv['z'], sv['pp'], token)
    big['ple_w_proj'] = _mm(p_i, dpp_b, ta=True, name="ple_proj_dw", tn=256, out_sh=True, out_dtypes=(BF16,))
    big['ple_w_gate'] = _row_shards(_mm(sv['x2b'], dz_b, ta=True, name="dw_dd", out_dtypes=(BF16,)))
    ln = dict(tb=True, tm=512, tn=D_MODEL, out_dtypes=(F32, BF16), n_sums=2)
    (dv1, dv1b), (dg1, db1) = _mm(dz_b, W['gate'], name="ple_gate_dx_ln", epilogue=_ln_bwd_epilogue(1.0),
                                  extras=[(dx, 'mn'), (sv['v1'], 'mn'), (sv['g1'], 'n')], **ln)
    big['mlp_w2'] = _row_shards(_mm(sv['ab'], dv1b, ta=True, name="mlp_down_dw", out_dtypes=(BF16,)))
    dub = _mm(dv1b, W['w2'], tb=True, name="mlp_down_dx", out_dtypes=(BF16,), tm=FULL_ROWS,
              extras=[(sv['ab'], 'mn')], epilogue=lambda acc, a: (acc * (2.0 * jnp.sqrt(a.astype(F32))),))
    big['mlp_w1'] = _mm(sv['x1b'], dub, ta=True, name="mlp_up_dw", out_sh=True, out_dtypes=(BF16,))
    order = []
    if early is not None:
        order, big = [(early(big), 'whole')], {}
    (dv0, dv0b), (dg0, db0) = _mm(dub, W['w1'], name="mlp_up_dx_ln", b_sh=True, tk=D_FF, epilogue=_ln_bwd_epilogue(ALPHA),
                                  extras=[(dv1, 'mn'), (sv['v0'], 'mn'), (sv['g0'], 'n')] + order, **ln)
    small['ln_g'] = jnp.stack([dg0[0], dg1[0]])
    small['ln_b'] = jnp.stack([db0[0], db1[0]])
    resid = dict(tn=1024, extras=[(dv0, 'mn')], epilogue=lambda acc, r: (acc + ALPHA * r,))
    if kind == 'gla':
        big['gla_w_out'] = _row_shards(_mm(sv['yb'], dv0b, ta=True, name="dw_dd", out_dtypes=(BF16,)))
        dy = _mm(dv0b, W['w_out'], tb=True, name="dx_dd", tn=1024)
        do, dr_b, dng = _gla_post_bwd(dy, sv['o'], sv['pin'], _row(sm['gla_norm_g'][j]))
        dq_b, dk_b, dvv_b, dla = _gla_bwd(sv['pin'], sv['la'], sv['states'], do)
        dzg_b, dbg = _gla_gate_bwd(dla, sv['la'])
        dw_up = _mm(sv['pin'], dzg_b, ta=True, name="gla_gate_dw", M=128, tm=128,
                    a_off=(0, (GLA_IN_PAD - 128) // 128))
        dglr_b = _mm(dzg_b, sv['w_up'], tb=True, name="gla_gate_dx", out_dtypes=(BF16,))
        dpin_b = jnp.concatenate([dq_b, dk_b, dvv_b, dr_b, dglr_b], axis=1)
        dw_in = _mm(sv['xin_b'], dpin_b, ta=True, name="gla_in_dw", tn=640, out_dtypes=(BF16,))
        dxin = _mm(dpin_b, W['w_in'], tb=True, name="gla_in_dx", tk=640, **resid)
        big['gla_w_in'] = jnp.stack([dw_in[:, GLA_WIN_STEP * qq:GLA_WIN_STEP * qq + GLA_WIN]
                                     for qq in range(N_CHIPS)])
        small.update(gla_w_gate_up=dw_up[:GLA_RANK], gla_b_gate=dbg[0], gla_norm_g=dng[0])
    elif kind == 'mla':
        big['mla_w_out'] = _row_shards(_mm(sv['ob'], dv0b, ta=True, name="dw_dd", out_dtypes=(BF16,)))
        do = _mm(dv0b, W['w_out'], tb=True, name="dx_dd", tn=1024)
        dqr, dkn_b, dvv_b, dkr = _mla_attn_bwd(sv['qr'], sv['knv'], sv['kr'], sv['o'], do)
        dq_b = _mla_qrope_bwd(dqr, cosp, sinp)
        dw_uq = _mm(sv['qn'], dq_b, ta=True, name="mla_up_dw", out_dtypes=(BF16,))
        dqn = _mm(dq_b, W['w_uq'], tb=True, name="mla_up_dx")
        dknv_b = jnp.concatenate([dkn_b, dvv_b], axis=1)
        dw_ukv = _mm(sv['kvn'], dknv_b, ta=True, name="mla_up_dw", out_dtypes=(BF16,))
        dkvn = _mm(dknv_b, W['w_ukv'], tb=True, name="mla_up_dx")
        dcin_b, dgq, dgkv = _mla_norm_bwd(sv['cin'], dqn, dkvn, dkr, sv['gq'], sv['gkv'], cosp, sinp)
        big['mla_w_in'] = _row_shards(_mm(sv['xin_b'], dcin_b, ta=True, name="mla_in_dw", tn=640,
                                          out_dtypes=(BF16,)))
        dxin = _mm(dcin_b, W['w_in'], tb=True, name="mla_in_dx", tk=640, **resid)
        big['mla_w_uq'] = _col_shards(
            dw_uq.reshape(MLA_QR, MLA_HEADS, MLA_QH)[:, :, :MLA_NOPE + MLA_ROPE].reshape(MLA_QR, -1))
        big['mla_w_ukv'] = _col_shards(
            dw_ukv.reshape(MLA_KVR, 2, MLA_HEADS, 128).transpose(0, 2, 1, 3).reshape(MLA_KVR, -1))
        small.update(mla_q_norm=dgq[0], mla_kv_norm=dgkv[0])
    else:
        big['conv_w_out'] = _row_shards(_mm(sv['yb'], dv0b, ta=True, name="dw_dd", out_dtypes=(BF16,)))
        dy = _mm(dv0b, W['w_out'], tb=True, name="dx_dd", tn=1024)
        db_b, dc_b, du_b, dw8 = _conv_bwd(sv['bcu'], sv['w8'], dy)
        dbcu_b = jnp.concatenate([db_b, dc_b, du_b], axis=1)
        big['conv_w_in'] = _mm(sv['xin_b'], dbcu_b, ta=True, name="conv_in_dw", tn=768, out_sh=True,
                               out_dtypes=(BF16,))
        dxin = _mm(dbcu_b, W['w_in'], tb=True, name="conv_in_dx", tk=768, b_sh=True, **resid)
        small['conv_w'] = dw8[:3]
    return dxin, big, small


def _rope_tables(positions):
    inv_freq = ROPE_BASE ** (-jnp.arange(0, MLA_ROPE // 2, dtype=F32) * (2.0 / MLA_ROPE))
    ang = positions.astype(F32)[:, None] * inv_freq
    zeros = jnp.zeros((positions.shape[0], 64), F32)
    return (jnp.concatenate([jnp.cos(ang), jnp.cos(ang), zeros], axis=1),
            jnp.concatenate([jnp.sin(ang), jnp.sin(ang), zeros], axis=1))


FIRST_NEEDED = ['gla_w_in']


def _start_gathers(w, q):
    token, started = jnp.zeros(TOKEN, F32), []
    for i in range(DEPTH):
        sh = _layer_shards(w, i, q)
        groups = [list(sh)] if i > 0 else [FIRST_NEEDED, [n for n in sh if n not in FIRST_NEEDED]]
        for k, names in enumerate(groups):
            ops = [sh[n] for n in names]
            if i == 0 and k == 0:
                ops.append(_pack_small_shards(w))
            tag = "l%d%s" % (i, "ab"[k] if i == 0 else "")
            handle = _gather_start(ops, token, "ag_start_" + tag)
            token = handle[4]
            started.append((handle, names, tag))
    return started, token


def _finish_gather(entry, after):
    handle, names, tag = entry
    srcs, lands = _gather_wait(handle, after, "ag_wait_" + tag)
    got = _gather_finish(srcs, lands, "ag_finish_" + tag)
    return dict(zip(names, got)), got[-1]


def _local_shard_grad(name, g, q):
    if name == 'gla_w_in':
        return lax.dynamic_slice_in_dim(g, (GLA_SHARD - GLA_WIN_STEP) * q, GLA_SHARD, axis=1)
    if name == 'mla_w_in':
        return g[:, :MLA_IN]
    return g


def kernel(x, p, positions, gla_w_in, gla_w_gate_up, gla_b_gate, gla_norm_g, gla_w_out, mla_w_in, mla_q_norm, mla_kv_norm, mla_w_uq, mla_w_ukv, mla_w_out, conv_w_in, conv_w, conv_w_out, ln_g, ln_b, mlp_w1, mlp_w2, ple_w_gate, ple_w_proj, loss_target, m_gla_w_in, m_gla_w_gate_up, m_gla_b_gate, m_gla_norm_g, m_gla_w_out, m_mla_w_in, m_mla_q_norm, m_mla_kv_norm, m_mla_w_uq, m_mla_w_ukv, m_mla_w_out, m_conv_w_in, m_conv_w, m_conv_w_out, m_ln_g, m_ln_b, m_mlp_w1, m_mlp_w2, m_ple_w_gate, m_ple_w_proj, v_gla_w_in, v_gla_w_gate_up, v_gla_b_gate, v_gla_norm_g, v_gla_w_out, v_mla_w_in, v_mla_q_norm, v_mla_kv_norm, v_mla_w_uq, v_mla_w_ukv, v_mla_w_out, v_conv_w_in, v_conv_w, v_conv_w_out, v_ln_g, v_ln_b, v_mlp_w1, v_mlp_w2, v_ple_w_gate, v_ple_w_proj):
    args = locals()
    w = {n: args[n] for n in WNAMES}
    m = {n: args['m_' + n] for n in WNAMES}
    v = {n: args['v_' + n] for n in WNAMES}
    q = 2 * lax.axis_index("x") + lax.axis_index("y")
    cq = jnp.stack([lax.axis_index("c"), q]).astype(jnp.int32)

    cosp, sinp = _rope_tables(positions[0])
    started, after = _start_gathers(w, q)
    xin, saved, layers, sm = x[0], [], [], None
    xin_b = xin.astype(BF16)
    for i in range(DEPTH):
        got, last = _finish_gather(started[i + 1 if i else 0], after)
        rest = None
        if i == 0:
            sm = _unpack_small_gathered(last)
            sm['mla_q_norm'], sm['mla_kv_norm'] = w['mla_q_norm'], w['mla_kv_norm']
            rest = lambda after: _layer_weights(_finish_gather(started[1], after)[0], 0)
        xin, xin_b, sv, W = _layer_fwd(i, xin, xin_b, p[i, 0], _layer_weights(got, i), sm, cosp, sinp, rest)
        layers.append(W)
        saved.append(sv)
        after = xin
    dx, loss_cols = _loss_head(xin, loss_target[0])
    loss = lax.psum(jnp.sum(loss_cols[0]), ("x", "y", "c"))

    gbig = {n: [None] * WSPEC[n][0][0] for n in BIG}
    gsmall = {n: [None] * _full_shape(n)[0] for n in SMALL}
    pending = []

    def start(grads, i, tag):
        names = list(grads)
        gs = [grads[n] for n in names]
        handle = _reduce_direct_start(gs, tag) if i > 0 else _reduce_scatter_start(gs, cq, jnp.zeros(TOKEN, F32), tag)
        pending.append((handle, names, i, tag))
        return handle[4]

    def finish(above, after):
        for entry in [e for e in pending if e[2] > above]:
            pending.remove(entry)
            handle, names, i, tag = entry
            reduced = (_reduce_direct_finish if i > 0 else _reduce_scatter_finish)(handle, cq, after, tag)
            for n, g in zip(names, reduced):
                gbig[n][i if n in COMMON_BIG else i // 3] = _local_shard_grad(n, g, q)

    token = jnp.zeros(TOKEN, F32)
    for i in reversed(range(DEPTH)):
        early = (lambda grads: start(grads, 0, "l0a")) if i == 0 else None
        dx, big, small = _layer_bwd(i, dx, p[i, 0], layers[i], sm, saved[i], cosp, sinp, token, early)
        token = start(big, i, "l%d%s" % (i, "b" if i == 0 else ""))
        finish(i, dx)
        for n, g in small.items():
            gsmall[n][i if n in ('ln_g', 'ln_b') else i // 3] = g
    finish(-1, token)
    gsm = _unpack_small(_all_reduce_small(_pack_small({n: jnp.stack(g) for n, g in gsmall.items()})), q)

    grad, delta, new_m, new_v = {}, {}, {}, {}
    for n in BIG:
        grad[n], delta[n], new_m[n], new_v[n] = _adamw(w[n], m[n], v[n], gbig[n], "adamw_" + n)
    total = sum(_size(WSPEC[n][0]) for n in SMALL)
    rows = -(-total // 128 // 8) * 8

    def pack(dct):
        flat = jnp.concatenate([dct[n].reshape(-1) for n in SMALL])
        return jnp.pad(flat, (0, rows * 128 - total), constant_values=1.0).reshape(1, rows, 128)

    res = _adamw(pack(w), pack(m), pack(v), [pack(gsm)[0]], "adamw_small")
    for out, packed in zip((grad, delta, new_m, new_v), res):
        flat, off = packed.reshape(-1), 0
        for n in SMALL:
            sz = _size(WSPEC[n][0])
            out[n] = flat[off:off + sz].reshape(WSPEC[n][0])
            off += sz
    return (loss, dx[None], *[grad[n] for n in WNAMES], *[delta[n] for n in WNAMES],
            *[new_m[n] for n in WNAMES], *[new_v[n] for n in WNAMES])
```

```python
import functools

import numpy as np
import jax
import jax.numpy as jnp
from jax import lax
from jax.experimental import pallas as pl
from jax.experimental.pallas import tpu as pltpu

F32 = jnp.float32
BF16 = jnp.bfloat16
MESH = pl.DeviceIdType.MESH

D_MODEL = 1024
DEPTH = 4
CHUNK = 64
ALPHA = (2 * DEPTH) ** 0.25
LN_EPS = 1e-5
RMS_EPS = 1e-6
PLE_DIM = 256
D_FF = 4 * D_MODEL
GLA_HEADS = 4
GLA_DK = 128
GLA_DV = 256
GLA_RANK = 16
GLA_TAU = 16.0
GLA_HK = GLA_HEADS * GLA_DK
GLA_HV = GLA_HEADS * GLA_DV
GLA_IN = 2 * GLA_HK + GLA_HV + D_MODEL + GLA_RANK
GLA_IN_PAD = 2 * GLA_HK + GLA_HV + D_MODEL + 128
GLA_SHARD = GLA_IN // 4
GLA_WIN = 896
GLA_WIN_STEP = 768
MLA_HEADS = 8
MLA_NOPE = 128
MLA_ROPE = 64
MLA_V = 128
MLA_QR = 256
MLA_KVR = 256
MLA_IN = MLA_QR + MLA_KVR + MLA_ROPE
MLA_IN_PAD = MLA_QR + MLA_KVR + 128
MLA_QH = 256
ROPE_BASE = 10000.0
ADAM_LR = 0.001
ADAM_B1 = 0.9
ADAM_B2 = 0.999
ADAM_EPS = 1e-08
ADAM_WD = 0.01
ADAM_STEP = 10

VMEM_LIMIT = 48 * 1024 * 1024
FULL_ROWS = 2048
N_CHIPS = 4

WSPEC = {
    'gla_w_in': ((2, 1024, 772), 2), 'gla_w_gate_up': ((2, 16, 128), 2), 'gla_b_gate': ((2, 128), 1),
    'gla_norm_g': ((2, 64), 1), 'gla_w_out': ((2, 256, 1024), 1), 'mla_w_in': ((1, 256, 576), 1),
    'mla_q_norm': ((1, 256), None), 'mla_kv_norm': ((1, 256), None), 'mla_w_uq': ((1, 256, 384), 2),
    'mla_w_ukv': ((1, 256, 512), 2), 'mla_w_out': ((1, 256, 1024), 1), 'conv_w_in': ((1, 1024, 768), 2),
    'conv_w': ((1, 3, 256), 2), 'conv_w_out': ((1, 256, 1024), 1), 'ln_g': ((4, 2, 256), 2),
    'ln_b': ((4, 2, 256), 2), 'mlp_w1': ((4, 1024, 1024), 2), 'mlp_w2': ((4, 1024, 1024), 1),
    'ple_w_gate': ((4, 256, 1024), 1), 'ple_w_proj': ((4, 256, 256), 2),
}
WNAMES = list(WSPEC)
BIG = ['gla_w_in', 'gla_w_out', 'mla_w_in', 'mla_w_uq', 'mla_w_ukv', 'mla_w_out', 'conv_w_in', 'conv_w_out',
       'mlp_w1', 'mlp_w2', 'ple_w_gate', 'ple_w_proj']
SMALL_SHARDED = ['gla_w_gate_up', 'gla_b_gate', 'gla_norm_g', 'conv_w', 'ln_g', 'ln_b']
SMALL = SMALL_SHARDED + ['mla_q_norm', 'mla_kv_norm']
MIXER = ['gla', 'mla', 'conv']
LAYER_BIG = {'gla': ['gla_w_in', 'gla_w_out'], 'mla': ['mla_w_in', 'mla_w_uq', 'mla_w_ukv', 'mla_w_out'],
             'conv': ['conv_w_in', 'conv_w_out']}
COMMON_BIG = ['mlp_w1', 'mlp_w2', 'ple_w_gate', 'ple_w_proj']


def _size(shape):
    return int(np.prod(shape))


def _full_shape(name):
    shape, ax = WSPEC[name]
    if ax is None:
        return shape
    return tuple(s * N_CHIPS if i == ax else s for i, s in enumerate(shape))


def _cparams(sem=None):
    return pltpu.CompilerParams(dimension_semantics=sem, vmem_limit_bytes=VMEM_LIMIT)


def _out(shape, dtype):
    return pltpu.HBM(shape, dtype)


def _hbm(v):
    return pltpu.with_memory_space_constraint(v, pltpu.HBM)


def _mm(a, b, *, name, ta=False, tb=False, M=None, N=None, K=None, out_dtypes=(F32,), epilogue=None, extras=(),
        tm=1024, tn=512, tk=None, a_off=(0, 0), b_sh=False, out_sh=False, n_sums=0):
    if M is None:
        M = a.shape[1] if ta else a.shape[0]
    if K is None:
        K = a.shape[0] if ta else a.shape[1]
    if b_sh:
        kw, nq = b.shape[1], b.shape[2]
        n_b, k_b = (kw, N_CHIPS * nq) if tb else (N_CHIPS * nq, kw)
        N = n_b if N is None else N
        assert K == k_b
    elif N is None:
        N = b.shape[0] if tb else b.shape[1]
    if tk is None:
        tk = FULL_ROWS if ta else 1024
    tm, tn, tk = min(tm, M), min(tn, N), min(tk, K)
    assert M % tm == 0 and N % tn == 0 and K % tk == 0, (name, M, N, K, tm, tn, tk)
    nk = K // tk
    n_ex, n_out = len(extras), len(out_dtypes)
    assert n_sums == 0 or tn == N

    n_b = N_CHIPS if (b_sh and tb and tk == K) else 1

    def body(a_ref, *rest):
        b_refs, rest = rest[:n_b], rest[n_b:]
        ex_refs, out_refs = rest[:n_ex], rest[n_ex:n_ex + n_out]
        sum_refs = rest[n_ex + n_out:n_ex + n_out + n_sums]
        first_rows = pl.program_id(0) == 0
        dims = ((((0,) if ta else (1,)), ((1,) if tb else (0,))), ((), ()))
        if n_b == 1:
            part = lax.dot_general(a_ref[...].astype(BF16), b_refs[0][...].astype(BF16), dims,
                                   preferred_element_type=F32)
        else:
            part = sum(lax.dot_general(a_ref[:, s * nq:(s + 1) * nq].astype(BF16), b_refs[s][...].astype(BF16), dims,
                                       preferred_element_type=F32) for s in range(n_b))

        def finish(acc):
            res = (acc,) if epilogue is None else epilogue(acc, *[r[...] for r in ex_refs])
            if n_sums:
                res, sums = res

                @pl.when(first_rows)
                def _():
                    for r in sum_refs:
                        r[...] = jnp.zeros(r.shape, F32)

                for r, v in zip(sum_refs, sums):
                    r[...] += jnp.broadcast_to(v, r.shape)
            for r, v in zip(out_refs, res):
                r[...] = v.astype(r.dtype)

        if nk == 1:
            finish(part)
        else:
            acc_ref = rest[-1]
            k = pl.program_id(2)

            @pl.when(k == 0)
            def _():
                acc_ref[...] = part

            @pl.when(k > 0)
            def _():
                acc_ref[...] += part

            @pl.when(k == nk - 1)
            def _():
                finish(acc_ref[...])

    if ta:
        a_spec = pl.BlockSpec((tk, tm), lambda i, j, k: (k + a_off[0], i + a_off[1]))
    else:
        a_spec = pl.BlockSpec((tm, tk), lambda i, j, k: (i + a_off[0], k + a_off[1]))
    once = dict(pipeline_mode=pl.Buffered(1)) if (tn == N and nk == 1) else {}
    if n_b > 1:
        b_specs = [pl.BlockSpec((None, tn, nq), functools.partial(lambda i, j, k, s: (s, j, 0), s=s), **once)
                   for s in range(n_b)]
    elif b_sh and tb:
        assert nq % tk == 0
        per = nq // tk
        b_spec = pl.BlockSpec((None, tn, tk), lambda i, j, k: (k // per, j, k % per), **once)
    elif b_sh:
        assert nq % tn == 0
        per = nq // tn
        b_spec = pl.BlockSpec((None, tk, tn), lambda i, j, k: (j // per, k, j % per), **once)
    elif tb:
        b_spec = pl.BlockSpec((tn, tk), lambda i, j, k: (j, k), **once)
    else:
        b_spec = pl.BlockSpec((tk, tn), lambda i, j, k: (k, j), **once)
    if n_b == 1:
        b_specs = [b_spec]
    ex_specs = []
    for arr, kind in extras:
        if kind == 'mn':
            ex_specs.append(pl.BlockSpec((tm, tn), lambda i, j, k: (i, j)))
        elif kind == 'n':
            ex_specs.append(pl.BlockSpec((1, tn), lambda i, j, k: (0, j)))
        else:
            ex_specs.append(pl.BlockSpec(arr.shape, lambda i, j, k: (0, 0)))
    if out_sh:
        assert (N // N_CHIPS) % tn == 0
        per_o = N // N_CHIPS // tn
        o_spec = pl.BlockSpec((None, tm, tn), lambda i, j, k: (j // per_o, i, j % per_o))
        o_shape = (N_CHIPS, M, N // N_CHIPS)
    else:
        o_spec = pl.BlockSpec((tm, tn), lambda i, j, k: (i, j))
        o_shape = (M, N)
    outs = pl.pallas_call(
        body, name=name, grid=(M // tm, N // tn, nk),
        in_specs=[a_spec] + b_specs + ex_specs,
        out_specs=[o_spec for _ in out_dtypes] + [pl.BlockSpec((8, N), lambda i, j, k: (0, 0))] * n_sums,
        out_shape=[_out(o_shape, d) for d in out_dtypes] + [_out((8, N), F32)] * n_sums,
        scratch_shapes=[pltpu.VMEM((tm, tn), F32)] if nk > 1 else [],
        compiler_params=_cparams(("arbitrary" if n_sums else "parallel", "parallel", "arbitrary")),
    )(a, *[b] * n_b, *[e[0] for e in extras])
    if n_sums:
        return tuple(outs[:n_out]), tuple(outs[n_out:])
    return outs[0] if n_out == 1 else tuple(outs)


def _rowwise(fn, *, name, rows, pars=(), outs=(), accs=(), tm=256):
    S = rows[0][0].shape[0]
    tm = min(tm, S)
    assert S % tm == 0
    n_r, n_p, n_o, n_a = len(rows), len(pars), len(outs), len(accs)

    def body(*refs):
        r_refs, p_refs = refs[:n_r], refs[n_r:n_r + n_p]
        o_refs, a_refs = refs[n_r + n_p:n_r + n_p + n_o], refs[n_r + n_p + n_o:]
        o_vals, a_vals = fn([r[...] for r in r_refs], [p[...] for p in p_refs])
        for r, v in zip(o_refs, o_vals):
            r[...] = v.astype(r.dtype)
        if n_a:
            i = pl.program_id(0)

            @pl.when(i == 0)
            def _():
                for r in a_refs:
                    r[...] = jnp.zeros(r.shape, r.dtype)

            for r, v in zip(a_refs, a_vals):
                r[...] += jnp.broadcast_to(v, r.shape)

    in_specs = [pl.BlockSpec((tm, w), functools.partial(lambda i, o: (i, o), o=off)) for _, w, off in rows]
    in_specs += [pl.BlockSpec(p.shape, functools.partial(lambda i, nd: (0,) * nd, nd=p.ndim)) for p in pars]
    out_specs = [pl.BlockSpec((tm, w), lambda i: (i, 0)) for w, _ in outs]
    out_specs += [pl.BlockSpec((8, w), lambda i: (0, 0)) for w in accs]
    out_shape = [_out((S, w), d) for w, d in outs]
    out_shape += [_out((8, w), F32) for w in accs]
    res = pl.pallas_call(
        body, name=name, grid=(S // tm,), in_specs=in_specs, out_specs=out_specs, out_shape=out_shape,
        compiler_params=_cparams(("arbitrary",)),
    )(*[r[0] for r in rows], *pars)
    return tuple(res)


def _colsum(v):
    return jnp.sum(v, axis=0, keepdims=True)


def _ln_stats(v):
    mu = jnp.mean(v, axis=-1, keepdims=True)
    d = v - mu
    var = jnp.mean(d * d, axis=-1, keepdims=True)
    rstd = lax.rsqrt(var + LN_EPS)
    return d * rstd, rstd


def _ln_fwd_epilogue(h, x, g, b):
    v = ALPHA * x + h
    xhat, _ = _ln_stats(v)
    y = xhat * g + b
    return y, y, v


def _ln_bwd_epilogue(scale):
    def epilogue(acc, resid, v, g, *unused):
        dy = acc + scale * resid
        xhat, rstd = _ln_stats(v)
        dxh = dy * g
        m1 = jnp.mean(dxh, axis=-1, keepdims=True)
        m2 = jnp.mean(dxh * xhat, axis=-1, keepdims=True)
        dv = rstd * (dxh - m1 - xhat * m2)
        return (dv, dv), (_colsum(dy * xhat), _colsum(dy))
    return epilogue


def _ple_gate_grads(dx3, z, pp):
    s = jax.nn.sigmoid(z)
    return dx3 * s, dx3 * pp * s * (1.0 - s)


def _loss_head(y, t, z, pp):
    def fn(r, p):
        d = r[0] - r[1]
        dy = d * (1.0 / D_MODEL)
        return [dy, *_ple_gate_grads(dy, r[2], r[3])], [_colsum(d * d) * (0.5 / D_MODEL)]
    return _rowwise(fn, name="loss_head", rows=[(a, D_MODEL, 0) for a in (y, t, z, pp)],
                    outs=[(D_MODEL, F32), (D_MODEL, BF16), (D_MODEL, BF16)], accs=[D_MODEL])


def _input_grad_epilogue(acc, dv, *below):
    dx = acc + ALPHA * dv
    return (dx, *_ple_gate_grads(dx, *below)) if below else (dx,)


N_LEVELS = 6
GLA_STEP = 2


def _gla_consts():
    C = CHUNK
    A = np.zeros((N_LEVELS + 3, C, C), np.float32)
    masks = np.zeros((N_LEVELS + 1, C, C), np.float32)
    r = np.arange(C)[:, None]
    u = np.arange(C)[None, :]
    for l in range(N_LEVELS):
        half = C >> (l + 1)
        mid = (r // (2 * half)) * (2 * half) + half - 1
        A[l] = np.where(r > mid, (u > mid) & (u <= r), (u > r) & (u <= mid))
        masks[l] = ((r // (2 * half)) == (u // (2 * half))) & (((r // half) % 2) != ((u // half) % 2))
    masks[N_LEVELS] = (r == u)
    A[N_LEVELS] = (u <= r)
    A[N_LEVELS + 1] = (u > r)
    A[N_LEVELS + 2] = 1.0
    A = A.reshape(-1, C)
    return A, np.ascontiguousarray(A.T), masks


def _split3(v):
    hi = v.astype(BF16)
    r1 = v - hi.astype(F32)
    mid = r1.astype(BF16)
    lo = (r1 - mid.astype(F32)).astype(BF16)
    return hi, mid, lo


def _dot_exact01(a01, v):
    hi, mid, lo = _split3(v)
    f = lambda p: jnp.dot(a01, p, preferred_element_type=F32)
    return f(hi) + f(mid) + f(lo)


def _nt(a, b):
    return lax.dot_general(a, b, (((1,), (1,)), ((), ())), preferred_element_type=F32)


def _tn(a, b):
    return lax.dot_general(a, b, (((0,), (0,)), ((), ())), preferred_element_type=F32)


def _nn(a, b):
    return jnp.dot(a, b, preferred_element_type=F32)


def _gla_chunk_terms(q, k, E, m_ref):
    C = CHUNK
    scores = m_ref[N_LEVELS] * _nt(q.astype(BF16), k.astype(BF16))
    qes, kes = [], []
    for l in range(N_LEVELS):
        El = E[l * C:(l + 1) * C]
        qe, ke = (q * El).astype(BF16), (k * El).astype(BF16)
        qes.append(qe)
        kes.append(ke)
        scores = scores + m_ref[l] * _nt(qe, ke)
    return qes, kes, scores


def _head(v, h, w):
    return v[:, h * w:(h + 1) * w]


def _gla_fwd(pin, la):
    S = pin.shape[0]
    NC = S // CHUNK
    C, R = CHUNK, CHUNK * GLA_STEP
    A, _, masks = _gla_consts()

    def body(q_ref, k_ref, v_ref, la_ref, a_ref, m_ref, o_ref, st_ref, state):
        @pl.when(pl.program_id(0) == 0)
        def _():
            state[...] = jnp.zeros(state.shape, F32)

        for ci in range(GLA_STEP):
            rows = pl.ds(ci * C, C)
            E_all = jnp.exp(_dot_exact01(a_ref[...], la_ref[rows, :]))
            q_all = q_ref[rows, :] * (GLA_DK ** -0.5)
            k_all, v_all = k_ref[rows, :], v_ref[rows, :]
            outs = []
            for h in range(GLA_HEADS):
                q, k, E = _head(q_all, h, GLA_DK), _head(k_all, h, GLA_DK), _head(E_all, h, GLA_DK)
                _, _, scores = _gla_chunk_terms(q, k, E, m_ref)
                Eq, Ek, Ee = E[6 * C:7 * C], E[7 * C:8 * C], E[8 * C:9 * C]
                st = state[h]
                st_ref[h, ci] = st
                vb = _head(v_all, h, GLA_DV).astype(BF16)
                outs.append(_nn(scores.astype(BF16), vb) + _nt((q * Eq).astype(BF16), st.astype(BF16)))
                state[h] = st * jnp.concatenate([Ee] * (GLA_DV // C), axis=0) + _tn(vb, (k * Ek).astype(BF16))
            o_ref[rows, :] = jnp.concatenate(outs, axis=1)

    return pl.pallas_call(
        body, name="gla_fwd", grid=(NC // GLA_STEP,),
        in_specs=[pl.BlockSpec((R, GLA_HK), lambda c: (c, 0)),
                  pl.BlockSpec((R, GLA_HK), lambda c: (c, 1)),
                  pl.BlockSpec((R, GLA_HV), lambda c: (c, 2 * GLA_HK // GLA_HV)),
                  pl.BlockSpec((R, GLA_HK), lambda c: (c, 0)),
                  pl.BlockSpec(A.shape, lambda c: (0, 0)),
                  pl.BlockSpec(masks.shape, lambda c: (0, 0, 0))],
        out_specs=[pl.BlockSpec((R, GLA_HV), lambda c: (c, 0)),
                   pl.BlockSpec((GLA_HEADS, GLA_STEP, GLA_DV, GLA_DK), lambda c: (0, c, 0, 0))],
        out_shape=[_out((S, GLA_HV), F32), _out((GLA_HEADS, NC, GLA_DV, GLA_DK), F32)],
        scratch_shapes=[pltpu.VMEM((GLA_HEADS, GLA_DV, GLA_DK), F32)],
        compiler_params=_cparams(("arbitrary",)),
    )(pin, pin, pin, la, jnp.asarray(A, BF16), jnp.asarray(masks))


def _gla_bwd(pin, la, states, do):
    S = pin.shape[0]
    NC = S // CHUNK
    C, R = CHUNK, CHUNK * GLA_STEP
    A, AT, masks = _gla_consts()
    scale = GLA_DK ** -0.5

    def body(q_ref, k_ref, v_ref, la_ref, st_ref, do_ref, a_ref, at_ref, m_ref,
             dq_ref, dk_ref, dv_ref, dla_ref, dstate):
        @pl.when(pl.program_id(0) == 0)
        def _():
            dstate[...] = jnp.zeros(dstate.shape, F32)

        for ci in reversed(range(GLA_STEP)):
            one_chunk(ci, pl.ds(ci * C, C), q_ref, k_ref, v_ref, la_ref, st_ref, do_ref, a_ref, at_ref, m_ref,
                      dq_ref, dk_ref, dv_ref, dla_ref, dstate)

    def one_chunk(ci, rows, q_ref, k_ref, v_ref, la_ref, st_ref, do_ref, a_ref, at_ref, m_ref,
                  dq_ref, dk_ref, dv_ref, dla_ref, dstate):
        E_all = jnp.exp(_dot_exact01(a_ref[...], la_ref[rows, :]))
        q_all = q_ref[rows, :] * scale
        k_all, v_all, do_all = k_ref[rows, :], v_ref[rows, :], do_ref[rows, :]
        dqs, dks, dvs, dXs = [], [], [], []
        for h in range(GLA_HEADS):
            q, k, E = _head(q_all, h, GLA_DK), _head(k_all, h, GLA_DK), _head(E_all, h, GLA_DK)
            qes, kes, scores = _gla_chunk_terms(q, k, E, m_ref)
            Eq, Ek, Ee = E[6 * C:7 * C], E[7 * C:8 * C], E[8 * C:9 * C]
            st, dst = st_ref[h, ci], dstate[h]
            dob, vb = _head(do_all, h, GLA_DV).astype(BF16), _head(v_all, h, GLA_DV).astype(BF16)
            dstb = dst.astype(BF16)
            qEq, kEk = (q * Eq).astype(BF16), (k * Ek).astype(BF16)
            dsc = _nt(dob, vb)
            dvs.append(_tn(scores.astype(BF16), dob) + _nt(kEk, dstb))
            dqEq = _nn(dob, st.astype(BF16))
            dkEk = _nn(vb, dstb)
            Gd = (m_ref[N_LEVELS] * dsc).astype(BF16)
            dq = _nn(Gd, k.astype(BF16)) + dqEq * Eq
            dk = _tn(Gd, q.astype(BF16)) + dkEk * Ek
            dX = []
            for l in range(N_LEVELS):
                El = E[l * C:(l + 1) * C]
                G = (m_ref[l] * dsc).astype(BF16)
                dqe, dke = _nn(G, kes[l]), _tn(G, qes[l])
                dq = dq + dqe * El
                dk = dk + dke * El
                dX.append((dqe * q + dke * k) * El)
            dX.append(dqEq * q * Eq)
            dX.append(dkEk * k * Ek)
            prod = dst * st
            dEe = prod[0:C]
            for i in range(1, GLA_DV // C):
                dEe = dEe + prod[i * C:(i + 1) * C]
            dX.append(dEe * Ee)
            dXs.append(jnp.concatenate(dX, axis=0))
            dqs.append(dq * scale)
            dks.append(dk)
            dstate[h] = dst * jnp.concatenate([Ee] * (GLA_DV // C), axis=0) + _tn(dob, qEq)
        dla_ref[rows, :] = _dot_exact01(at_ref[...], jnp.concatenate(dXs, axis=1))
        dq_ref[rows, :] = jnp.concatenate(dqs, axis=1).astype(dq_ref.dtype)
        dk_ref[rows, :] = jnp.concatenate(dks, axis=1).astype(dk_ref.dtype)
        dv_ref[rows, :] = jnp.concatenate(dvs, axis=1).astype(dv_ref.dtype)

    rc = lambda c: NC // GLA_STEP - 1 - c
    return pl.pallas_call(
        body, name="gla_bwd", grid=(NC // GLA_STEP,),
        in_specs=[pl.BlockSpec((R, GLA_HK), lambda c: (rc(c), 0)),
                  pl.BlockSpec((R, GLA_HK), lambda c: (rc(c), 1)),
                  pl.BlockSpec((R, GLA_HV), lambda c: (rc(c), 2 * GLA_HK // GLA_HV)),
                  pl.BlockSpec((R, GLA_HK), lambda c: (rc(c), 0)),
                  pl.BlockSpec((GLA_HEADS, GLA_STEP, GLA_DV, GLA_DK), lambda c: (0, rc(c), 0, 0)),
                  pl.BlockSpec((R, GLA_HV), lambda c: (rc(c), 0)),
                  pl.BlockSpec(A.shape, lambda c: (0, 0)),
                  pl.BlockSpec(AT.shape, lambda c: (0, 0)),
                  pl.BlockSpec(masks.shape, lambda c: (0, 0, 0))],
        out_specs=[pl.BlockSpec((R, GLA_HK), lambda c: (rc(c), 0)),
                   pl.BlockSpec((R, GLA_HK), lambda c: (rc(c), 0)),
                   pl.BlockSpec((R, GLA_HV), lambda c: (rc(c), 0)),
                   pl.BlockSpec((R, GLA_HK), lambda c: (rc(c), 0))],
        out_shape=[_out((S, GLA_HK), BF16), _out((S, GLA_HK), BF16), _out((S, GLA_HV), BF16),
                   _out((S, GLA_HK), F32)],
        scratch_shapes=[pltpu.VMEM((GLA_HEADS, GLA_DV, GLA_DK), F32)],
        compiler_params=_cparams(("arbitrary",)),
    )(pin, pin, pin, la, states, do, jnp.asarray(A, BF16), jnp.asarray(AT, BF16), jnp.asarray(masks))


def _gla_post_fwd(o, pin, g):
    def fn(r, p):
        ov, rv = r
        ys = []
        for h in range(GLA_HEADS):
            oh = ov[:, h * GLA_DV:(h + 1) * GLA_DV]
            rh = rv[:, h * GLA_DV:(h + 1) * GLA_DV]
            rs = lax.rsqrt(jnp.mean(oh * oh, axis=-1, keepdims=True) + RMS_EPS)
            ys.append(oh * rs * p[0] * (rh * jax.nn.sigmoid(rh)))
        return [jnp.concatenate(ys, axis=1)], []
    return _rowwise(fn, name="gla_post_fwd", rows=[(o, GLA_HV, 0), (pin, GLA_HV, (2 * GLA_HK + GLA_HV) // GLA_HV)],
                    pars=[g], outs=[(GLA_HV, BF16)])[0]


def _gla_post_bwd(dy, o, pin, g):
    def fn(r, p):
        dyv, ov, rv = r
        dos, drs, dg = [], [], 0.0
        for h in range(GLA_HEADS):
            sl = slice(h * GLA_DV, (h + 1) * GLA_DV)
            oh, rh, dyh = ov[:, sl], rv[:, sl], dyv[:, sl]
            rs = lax.rsqrt(jnp.mean(oh * oh, axis=-1, keepdims=True) + RMS_EPS)
            xh = oh * rs
            sg = jax.nn.sigmoid(rh)
            d_on = dyh * (rh * sg)
            drs.append(dyh * (xh * p[0]) * (sg * (1.0 + rh * (1.0 - sg))))
            dg = dg + _colsum(d_on * xh)
            dxh = d_on * p[0]
            dos.append(rs * (dxh - xh * jnp.mean(dxh * xh, axis=-1, keepdims=True)))
        return [jnp.concatenate(dos, axis=1), jnp.concatenate(drs, axis=1)], [dg]
    return _rowwise(fn, name="gla_post_bwd",
                    rows=[(dy, GLA_HV, 0), (o, GLA_HV, 0), (pin, GLA_HV, (2 * GLA_HK + GLA_HV) // GLA_HV)],
                    pars=[g], outs=[(GLA_HV, F32), (GLA_HV, BF16)], accs=[GLA_DV])


def _gla_gate_bwd(dla, la):
    def fn(r, p):
        dz = r[0] * (1.0 / GLA_TAU) * (1.0 - jnp.exp(GLA_TAU * r[1]))
        return [dz], [_colsum(dz)]
    return _rowwise(fn, name="gla_gate_bwd", rows=[(dla, GLA_HK, 0), (la, GLA_HK, 0)], outs=[(GLA_HK, BF16)],
                    accs=[GLA_HK])


def _log_sigmoid(z):
    return jnp.minimum(z, 0.0) - jnp.log(1.0 + jnp.exp(-jnp.abs(z)))


def _rot_half(v):
    lane = lax.broadcasted_iota(jnp.int32, v.shape, 1)
    return jnp.where(lane < 32, -pltpu.roll(v, 96, 1), jnp.where(lane < 64, pltpu.roll(v, 32, 1), 0.0))


def _rms(v):
    rs = lax.rsqrt(jnp.mean(v * v, axis=-1, keepdims=True) + RMS_EPS)
    return v * rs, rs


def _mla_norm_fwd(cin, gq, gkv, cosp, sinp):
    def fn(r, p):
        cv, cs, sn = r
        qn, _ = _rms(cv[:, :MLA_QR])
        kvn, _ = _rms(cv[:, MLA_QR:MLA_QR + MLA_KVR])
        kr = cv[:, MLA_QR + MLA_KVR:]
        return [qn * p[0], kvn * p[1], kr * cs + _rot_half(kr) * sn], []
    return _rowwise(fn, name="mla_norm_fwd", rows=[(cin, MLA_IN_PAD, 0), (cosp, 128, 0), (sinp, 128, 0)],
                    pars=[gq, gkv], outs=[(MLA_QR, BF16), (MLA_KVR, BF16), (128, BF16)])


def _mla_qrope_fwd(q, cosp, sinp):
    scale = (MLA_NOPE + MLA_ROPE) ** -0.5

    def fn(r, p):
        qv, cs, sn = r
        parts = []
        for h in range(MLA_HEADS):
            parts.append(qv[:, h * MLA_QH:h * MLA_QH + 128] * scale)
            rp = qv[:, h * MLA_QH + 128:(h + 1) * MLA_QH]
            parts.append((rp * cs + _rot_half(rp) * sn) * scale)
        return [jnp.concatenate(parts, axis=1)], []
    W = MLA_HEADS * MLA_QH
    return _rowwise(fn, name="mla_qrope_fwd", rows=[(q, W, 0), (cosp, 128, 0), (sinp, 128, 0)],
                    outs=[(W, BF16)])[0]


def _mla_qrope_bwd(dq, cosp, sinp):
    scale = (MLA_NOPE + MLA_ROPE) ** -0.5

    def fn(r, p):
        dv, cs, sn = r
        parts = []
        for h in range(MLA_HEADS):
            parts.append(dv[:, h * MLA_QH:h * MLA_QH + 128] * scale)
            rp = dv[:, h * MLA_QH + 128:(h + 1) * MLA_QH]
            parts.append((rp * cs - _rot_half(rp) * sn) * scale)
        return [jnp.concatenate(parts, axis=1)], []
    W = MLA_HEADS * MLA_QH
    return _rowwise(fn, name="mla_qrope_bwd", rows=[(dq, W, 0), (cosp, 128, 0), (sinp, 128, 0)],
                    outs=[(W, BF16)])[0]


def _mla_norm_bwd(cin, dqn, dkvn, dkr, gq, gkv, cosp, sinp):
    def fn(r, p):
        cv, dq_, dkv_, dkr_, cs, sn = r
        outs, accs = [], []
        for (lo, hi), dn, g in (((0, MLA_QR), dq_, p[0]), ((MLA_QR, MLA_QR + MLA_KVR), dkv_, p[1])):
            xh, rs = _rms(cv[:, lo:hi])
            dxh = dn * g
            outs.append(rs * (dxh - xh * jnp.mean(dxh * xh, axis=-1, keepdims=True)))
            accs.append(_colsum(dn * xh))
        dk = dkr_[:, 0:128]
        for h in range(1, MLA_HEADS):
            dk = dk + dkr_[:, h * 128:(h + 1) * 128]
        outs.append(dk * cs - _rot_half(dk) * sn)
        return [jnp.concatenate(outs, axis=1)], accs
    return _rowwise(fn, name="mla_norm_bwd",
                    rows=[(cin, MLA_IN_PAD, 0), (dqn, MLA_QR, 0), (dkvn, MLA_KVR, 0), (dkr, MLA_HEADS * 128, 0),
                          (cosp, 128, 0), (sinp, 128, 0)],
                    pars=[gq, gkv], outs=[(MLA_IN_PAD, BF16)], accs=[MLA_QR, MLA_KVR])


def _mla_probs(q, k, i, tq):
    s = _nt(q, k)
    row = (i * tq + lax.broadcasted_iota(jnp.int32, s.shape, 0)) // CHUNK
    col = lax.broadcasted_iota(jnp.int32, s.shape, 1) // CHUNK
    s = jnp.where(col <= row, s, -jnp.inf)
    e = jnp.exp(s - jnp.max(s, axis=-1, keepdims=True))
    return e / jnp.sum(e, axis=-1, keepdims=True)


def _mla_attn_fwd(qr, knv, kr, tq=256):
    S = qr.shape[0]
    tq = min(tq, S)

    def body(q_ref, kn_ref, v_ref, kr_ref, o_ref, k_cat):
        k_cat[:, :128] = kn_ref[...]
        k_cat[:, 128:] = kr_ref[...]
        for i in range(S // tq):
            rows, keys = pl.ds(i * tq, tq), pl.ds(0, (i + 1) * tq)
            pr = _mla_probs(q_ref[rows, :], k_cat[keys, :], i, tq)
            o_ref[rows, :] = _nn(pr.astype(BF16), v_ref[keys, :])

    return pl.pallas_call(
        body, name="mla_attn_fwd", grid=(MLA_HEADS,),
        in_specs=[pl.BlockSpec((S, MLA_QH), lambda h: (0, h)),
                  pl.BlockSpec((S, 128), lambda h: (0, h)),
                  pl.BlockSpec((S, 128), lambda h: (0, MLA_HEADS + h)),
                  pl.BlockSpec((S, 128), lambda h: (0, 0))],
        out_specs=pl.BlockSpec((S, 128), lambda h: (0, h)),
        out_shape=_out((S, MLA_HEADS * MLA_V), F32),
        scratch_shapes=[pltpu.VMEM((S, MLA_QH), BF16)],
        compiler_params=_cparams(("parallel",)),
    )(qr, knv, knv, kr)


def _mla_attn_bwd(qr, knv, kr, o, do, tq=256):
    S = qr.shape[0]
    tq = min(tq, S)
    W = MLA_HEADS * 128

    def body(q_ref, kn_ref, v_ref, kr_ref, o_ref, do_ref, dq_ref, dkn_ref, dv_ref, dkr_ref, k_cat, dk_acc, dv_acc):
        k_cat[:, :128] = kn_ref[...]
        k_cat[:, 128:] = kr_ref[...]
        dk_acc[...] = jnp.zeros(dk_acc.shape, F32)
        dv_acc[...] = jnp.zeros(dv_acc.shape, F32)
        for i in range(S // tq):
            rows, keys = pl.ds(i * tq, tq), pl.ds(0, (i + 1) * tq)
            q, k, v = q_ref[rows, :], k_cat[keys, :], v_ref[keys, :]
            pr = _mla_probs(q, k, i, tq)
            dov = do_ref[rows, :]
            delta = jnp.sum(dov * o_ref[rows, :], axis=-1, keepdims=True)
            dob = dov.astype(BF16)
            ds = (pr * (_nt(dob, v) - delta)).astype(BF16)
            dq_ref[rows, :] = _nn(ds, k)
            dk_acc[keys, :] += _tn(ds, q)
            dv_acc[keys, :] += _tn(pr.astype(BF16), dob)
        dkn_ref[...] = dk_acc[:, :128].astype(dkn_ref.dtype)
        dkr_ref[...] = dk_acc[:, 128:]
        dv_ref[...] = dv_acc[...].astype(dv_ref.dtype)

    head = lambda w: pl.BlockSpec((S, w), lambda h: (0, h))
    return pl.pallas_call(
        body, name="mla_attn_bwd", grid=(MLA_HEADS,),
        in_specs=[head(MLA_QH), head(128), pl.BlockSpec((S, 128), lambda h: (0, MLA_HEADS + h)),
                  pl.BlockSpec((S, 128), lambda h: (0, 0)), head(128), head(128)],
        out_specs=[head(MLA_QH), head(128), head(128), head(128)],
        out_shape=[_out((S, MLA_HEADS * MLA_QH), F32), _out((S, W), BF16), _out((S, W), BF16), _out((S, W), F32)],
        scratch_shapes=[pltpu.VMEM((S, MLA_QH), BF16), pltpu.VMEM((S, MLA_QH), F32), pltpu.VMEM((S, 128), F32)],
        compiler_params=_cparams(("parallel",)),
    )(qr, knv, knv, kr, o, do)


CONV_TILE = 256


def _shift_down(v, n):
    row = lax.broadcasted_iota(jnp.int32, v.shape, 0)
    return jnp.where(row >= n, pltpu.roll(v, n, 0), 0.0)


def _shift_up(v, n):
    S = v.shape[0]
    row = lax.broadcasted_iota(jnp.int32, v.shape, 0)
    return jnp.where(row < S - n, pltpu.roll(v, S - n, 0), 0.0)


def _conv_specs(S, n_extra_cols):
    nt = D_MODEL // CONV_TILE
    specs = [pl.BlockSpec((S, CONV_TILE), functools.partial(lambda j, o: (0, o + j), o=part * nt))
             for part in range(3)]
    specs.append(pl.BlockSpec((8, CONV_TILE), lambda j: (0, j)))
    specs += [pl.BlockSpec((S, CONV_TILE), lambda j: (0, j)) for _ in range(n_extra_cols)]
    return specs


def _conv_fwd(bcu, w8):
    S = bcu.shape[0]

    def body(b_ref, c_ref, u_ref, w_ref, y_ref):
        cu = c_ref[...] * u_ref[...]
        z = w_ref[2:3, :] * cu + w_ref[1:2, :] * _shift_down(cu, 1) + w_ref[0:1, :] * _shift_down(cu, 2)
        y_ref[...] = (b_ref[...] * z).astype(y_ref.dtype)

    return pl.pallas_call(
        body, name="conv_fwd", grid=(D_MODEL // CONV_TILE,), in_specs=_conv_specs(S, 0),
        out_specs=pl.BlockSpec((S, CONV_TILE), lambda j: (0, j)),
        out_shape=_out((S, D_MODEL), BF16),
        compiler_params=_cparams(("parallel",)),
    )(bcu, bcu, bcu, w8)


def _conv_bwd(bcu, w8, dy):
    S = bcu.shape[0]

    def body(b_ref, c_ref, u_ref, w_ref, dy_ref, db_ref, dc_ref, du_ref, dw_ref):
        b, c, u, dyv = b_ref[...], c_ref[...], u_ref[...], dy_ref[...]
        w0, w1, w2 = w_ref[0:1, :], w_ref[1:2, :], w_ref[2:3, :]
        cu = c * u
        cu1, cu2 = _shift_down(cu, 1), _shift_down(cu, 2)
        z = w2 * cu + w1 * cu1 + w0 * cu2
        dz = dyv * b
        db_ref[...] = (dyv * z).astype(db_ref.dtype)
        dcu = w2 * dz + w1 * _shift_up(dz, 1) + w0 * _shift_up(dz, 2)
        dc_ref[...] = (dcu * u).astype(dc_ref.dtype)
        du_ref[...] = (dcu * c).astype(du_ref.dtype)
        dw_ref[...] = jnp.zeros(dw_ref.shape, F32)
        dw_ref[0:1, :] = _colsum(dz * cu2)
        dw_ref[1:2, :] = _colsum(dz * cu1)
        dw_ref[2:3, :] = _colsum(dz * cu)

    col = pl.BlockSpec((S, CONV_TILE), lambda j: (0, j))
    return pl.pallas_call(
        body, name="conv_bwd", grid=(D_MODEL // CONV_TILE,), in_specs=_conv_specs(S, 1),
        out_specs=[col, col, col, pl.BlockSpec((8, CONV_TILE), lambda j: (0, j))],
        out_shape=[_out((S, D_MODEL), BF16)] * 3 + [_out((8, D_MODEL), F32)],
        compiler_params=_cparams(("parallel",)),
    )(bcu, bcu, bcu, w8, dy)


def _adamw(w, m, v, gs, name):
    L, R, Cn = w.shape
    assert len(gs) == L
    tr = R if R <= 256 else 256
    assert R % tr == 0

    def body(w_ref, m_ref, v_ref, *rest):
        g_refs, (go_ref, d_ref, nm_ref, nv_ref) = rest[:L], rest[L:]
        layer = pl.program_id(0)
        gv = g_refs[0][...]
        for k in range(1, L):
            gv = jnp.where(layer == k, g_refs[k][...], gv)
        nm = ADAM_B1 * m_ref[...] + (1.0 - ADAM_B1) * gv
        nv = ADAM_B2 * v_ref[...] + (1.0 - ADAM_B2) * jnp.square(gv)
        m_hat = nm / (1.0 - ADAM_B1 ** ADAM_STEP)
        v_hat = nv / (1.0 - ADAM_B2 ** ADAM_STEP)
        d_ref[...] = -ADAM_LR * (m_hat / (jnp.sqrt(v_hat) + ADAM_EPS) + ADAM_WD * w_ref[...])
        go_ref[...] = gv
        nm_ref[...] = nm
        nv_ref[...] = nv

    spec = pl.BlockSpec((None, tr, Cn), lambda l, i: (l, i, 0))
    g_specs = [pl.BlockSpec((tr, Cn), functools.partial(lambda l, i, k: (jnp.where(l == k, i, 0), 0), k=k))
               for k in range(L)]
    return pl.pallas_call(
        body, name=name, grid=(L, R // tr), in_specs=[spec] * 3 + g_specs, out_specs=[spec] * 4,
        out_shape=[jax.ShapeDtypeStruct((L, R, Cn), F32)] * 4,
        compiler_params=_cparams(("arbitrary", "arbitrary")),
    )(w, m, v, *gs)


HBM_SPEC = pl.BlockSpec(memory_space=pltpu.HBM)
BOUNCE_ROWS = 256


def _place():
    return lax.axis_index("x"), lax.axis_index("y"), lax.axis_index("c")


def _other_chips(x, y):
    return [(1 - x, y), (x, 1 - y), (1 - x, 1 - y)]


def _copy_via_vmem(src, dst, buf, sems, rows):
    ch = buf.shape[1]
    n = rows // ch
    cin = lambda i: pltpu.make_async_copy(src.at[pl.ds(i * ch, ch), :], buf.at[i % 2], sems.at[i % 2])
    cout = lambda i: pltpu.make_async_copy(buf.at[i % 2], dst.at[pl.ds(i * ch, ch), :], sems.at[2 + i % 2])
    cin(0).start()
    for i in range(n):
        cin(i).wait()
        cout(i).start()
        if i + 1 < n:
            if i >= 1:
                cout(i - 1).wait()
            cin(i + 1).start()
    if n >= 2:
        cout(n - 2).wait()
    cout(n - 1).wait()


SEM_SPEC = pl.BlockSpec(memory_space=pltpu.SEMAPHORE)
ANY_SPEC = pl.BlockSpec(memory_space=pl.ANY)
VMEM_SPEC = pl.BlockSpec(memory_space=pltpu.VMEM)
EFFECT = pltpu.SideEffectType.DATAFLOW_SIDE_EFFECTING
TOKEN = (8, 128)


def _ici_start(srcs, lands, after, copies, name, per_src=3):
    n, nl = len(srcs), len(lands)

    def body(*refs):
        src_refs, land_refs = refs[:n], refs[n:n + nl]
        send_sems, recv_sems, token = refs[n + nl + 1], refs[n + nl + 2], refs[-1]
        x, y, c = _place()
        for k, src, dst, to in copies(src_refs, land_refs, x, y, c):
            pltpu.make_async_remote_copy(src_ref=src, dst_ref=dst, send_sem=send_sems.at[k], recv_sem=recv_sems.at[k],
                                         device_id=to, device_id_type=MESH).start()
        token[...] = jnp.zeros(TOKEN, F32)

    n_copies = per_src * n
    res = pl.pallas_call(
        body, name=name,
        out_shape=(pltpu.SemaphoreType.DMA((n_copies,)), pltpu.SemaphoreType.DMA((n_copies,)),
                   *[pltpu.HBM(s.shape, s.dtype) for s in srcs], *[pltpu.HBM(l.shape, l.dtype) for l in lands],
                   jax.ShapeDtypeStruct(TOKEN, F32)),
        in_specs=[HBM_SPEC] * (n + nl) + [ANY_SPEC],
        out_specs=(SEM_SPEC, SEM_SPEC, *[HBM_SPEC] * (n + nl), VMEM_SPEC),
        input_output_aliases={t: 2 + t for t in range(n + nl)},
        compiler_params=pltpu.CompilerParams(has_side_effects=EFFECT),
    )(*[_hbm(s) for s in srcs], *[_hbm(l) for l in lands], after)
    return res[0], res[1], list(res[2:2 + n]), list(res[2 + n:2 + n + nl]), res[-1]


def _ici_wait(handle, after, copies, name):
    send_sems, recv_sems, srcs, lands, _ = handle
    n, nl = len(srcs), len(lands)

    def body(*refs):
        src_refs, land_refs = refs[:n], refs[n:n + nl]
        send_s, recv_s = refs[n + nl], refs[n + nl + 1]
        x, y, c = _place()
        for k, src, dst, to in copies(src_refs, land_refs, x, y, c):
            cp = pltpu.make_async_remote_copy(src_ref=src, dst_ref=dst, send_sem=send_s.at[k], recv_sem=recv_s.at[k],
                                              device_id=to, device_id_type=MESH)
            cp.wait_send()
            cp.wait_recv()

    res = pl.pallas_call(
        body, name=name,
        out_shape=(*[pltpu.HBM(s.shape, s.dtype) for s in srcs], *[pltpu.HBM(l.shape, l.dtype) for l in lands]),
        in_specs=[HBM_SPEC] * (n + nl) + [SEM_SPEC, SEM_SPEC, ANY_SPEC],
        out_specs=tuple([HBM_SPEC] * (n + nl)),
        input_output_aliases={t: t for t in range(n + nl)},
        compiler_params=pltpu.CompilerParams(has_side_effects=EFFECT),
    )(*srcs, *lands, send_sems, recv_sems, after)
    return list(res[:n]), list(res[n:])


def _gather_copies(halves):
    def copies(src_refs, land_refs, x, y, c):
        q = 2 * x + y
        out = []
        for t, H in enumerate(halves):
            for j, (cx, cy) in enumerate(_other_chips(x, y)):
                out.append((3 * t + j, src_refs[t].at[pl.ds(c * H, H), :], land_refs[t].at[q, pl.ds(c * H, H), :],
                            (cx, cy, c)))
        return out
    return copies


def _gather_wait_copies(halves):
    def copies(src_refs, land_refs, x, y, c):
        out = []
        for t, H in enumerate(halves):
            for j, (cx, cy) in enumerate(_other_chips(x, y)):
                out.append((3 * t + j, src_refs[t].at[pl.ds(c * H, H), :],
                            land_refs[t].at[2 * cx + cy, pl.ds(c * H, H), :], (cx, cy, c)))
        return out
    return copies


def _gather_start(ops, after, name):
    lands = [lax.empty((N_CHIPS,) + o.shape, o.dtype) for o in ops]
    return _ici_start(ops, lands, after, _gather_copies([o.shape[0] // 2 for o in ops]), name)


def _gather_wait(handle, after, name):
    halves = [s.shape[0] // 2 for s in handle[2]]
    return _ici_wait(handle, after, _gather_wait_copies(halves), name)


def _gather_finish(ops, lands, name):
    n = len(ops)
    halves = [o.shape[0] // 2 for o in ops]
    chunk = [min(o.shape[0], BOUNCE_ROWS) for o in ops]

    def body(*refs):
        in_refs, out_refs = refs[:n], refs[2 * n:3 * n]
        send_sems, recv_sems, local_sems = refs[3 * n:3 * n + 3]
        bufs = refs[3 * n + 3:]
        x, y, c = _place()
        q = 2 * x + y
        chips = _other_chips(x, y)
        sibling = (x, y, 1 - c)

        def copy(t, j, half):
            land = out_refs[t].at[2 * chips[j][0] + chips[j][1], pl.ds(half * halves[t], halves[t]), :]
            return pltpu.make_async_remote_copy(src_ref=land, dst_ref=land, send_sem=send_sems.at[3 * t + j],
                                                recv_sem=recv_sems.at[3 * t + j], device_id=sibling,
                                                device_id_type=MESH)

        passed = [copy(t, j, c) for t in range(n) for j in range(3)]
        for cp in passed:
            cp.start()
        for t in range(n):
            _copy_via_vmem(in_refs[t], out_refs[t].at[q], bufs[t], local_sems, ops[t].shape[0])
        for t in range(n):
            for j in range(3):
                copy(t, j, 1 - c).wait_recv()
        for cp in passed:
            cp.wait_send()

    return pl.pallas_call(
        body, name=name, in_specs=[HBM_SPEC] * (2 * n), out_specs=[HBM_SPEC] * n,
        out_shape=[jax.ShapeDtypeStruct(l.shape, l.dtype) for l in lands],
        input_output_aliases={n + t: t for t in range(n)},
        scratch_shapes=[pltpu.SemaphoreType.DMA((3 * n,)), pltpu.SemaphoreType.DMA((3 * n,)),
                        pltpu.SemaphoreType.DMA((4,))]
        + [pltpu.VMEM((2, chunk[t], ops[t].shape[1]), ops[t].dtype) for t in range(n)],
        compiler_params=pltpu.CompilerParams(vmem_limit_bytes=VMEM_LIMIT),
    )(*ops, *lands)


def _swap_halves(ops, name):
    n = len(ops)

    def body(*refs):
        in_refs, out_refs, send_sems, recv_sems = refs[:n], refs[n:2 * n], refs[2 * n], refs[2 * n + 1]
        x, y, c = _place()
        cps = []
        for t in range(n):
            H = ops[t].shape[1] // 2
            cp = pltpu.make_async_remote_copy(src_ref=in_refs[t].at[:, pl.ds((1 - c) * H, H), :],
                                              dst_ref=out_refs[t], send_sem=send_sems.at[t],
                                              recv_sem=recv_sems.at[t], device_id=(x, y, 1 - c),
                                              device_id_type=MESH)
            cp.start()
            cps.append(cp)
        for cp in cps:
            cp.wait()

    return pl.pallas_call(
        body, name=name, in_specs=[HBM_SPEC] * n, out_specs=[HBM_SPEC] * n,
        out_shape=[jax.ShapeDtypeStruct((N_CHIPS, o.shape[1] // 2, o.shape[2]), o.dtype) for o in ops],
        scratch_shapes=[pltpu.SemaphoreType.DMA((n,)), pltpu.SemaphoreType.DMA((n,))],
    )(*ops)


def _sum_rows_tile(h):
    return h if h <= 512 else 512


def _pair_sum(g, t, cq, name):
    _, a, b = g.shape
    H = a // 2
    tr = _sum_rows_tile(H)

    def body(cq_ref, g_ref, t_ref, o_ref):
        o_ref[...] = (g_ref[...].astype(F32) + t_ref[...].astype(F32)).astype(o_ref.dtype)

    grid_spec = pltpu.PrefetchScalarGridSpec(
        num_scalar_prefetch=1, grid=(N_CHIPS, H // tr),
        in_specs=[pl.BlockSpec((None, None, tr, b), lambda j, i, cq_ref: (j, cq_ref[0], i, 0)),
                  pl.BlockSpec((None, tr, b), lambda j, i, cq_ref: (j, i, 0))],
        out_specs=pl.BlockSpec((None, tr, b), lambda j, i, cq_ref: (j, i, 0)))
    return pl.pallas_call(
        body, name=name, grid_spec=grid_spec, out_shape=_out(t.shape, BF16),
        compiler_params=_cparams(("parallel", "parallel")),
    )(cq, g.reshape(N_CHIPS, 2, H, b), t)


def _scatter_copies(src_refs, land_refs, x, y, c):
    out = []
    for j, (cx, cy) in enumerate(_other_chips(x, y)):
        for t in range(len(src_refs)):
            out.append((3 * t + j, src_refs[t].at[2 * cx + cy], land_refs[t].at[j], (cx, cy, c)))
    return out


def _scatter_start(ops, after, name):
    lands = [lax.empty((3,) + o.shape[1:], o.dtype) for o in ops]
    return _ici_start(ops, lands, after, _scatter_copies, name)


def _scatter_wait(handle, after, name):
    return _ici_wait(handle, after, _scatter_copies, name)


def _chip_sum(p, t, cq, name):
    _, H, b = p.shape
    tr = _sum_rows_tile(H)

    def body(cq_ref, p_ref, t_ref, o_ref):
        acc = p_ref[...].astype(F32)
        for j in range(3):
            acc = acc + t_ref[j].astype(F32)
        o_ref[...] = acc

    grid_spec = pltpu.PrefetchScalarGridSpec(
        num_scalar_prefetch=1, grid=(H // tr,),
        in_specs=[pl.BlockSpec((None, tr, b), lambda i, cq_ref: (cq_ref[1], i, 0)),
                  pl.BlockSpec((3, tr, b), lambda i, cq_ref: (0, i, 0))],
        out_specs=pl.BlockSpec((None, tr, b), lambda i, cq_ref: (cq_ref[0], i, 0)))
    out = pl.pallas_call(
        body, name=name, grid_spec=grid_spec, out_shape=_out((2, H, b), F32),
        compiler_params=_cparams(("parallel",)),
    )(cq, p, t)
    return out.reshape(2 * H, b)


def _join_halves(ops, name):
    n = len(ops)

    def body(*refs):
        out_refs, send_sems, recv_sems = refs[n:2 * n], refs[2 * n], refs[2 * n + 1]
        x, y, c = _place()
        cps = []
        for t in range(n):
            H = ops[t].shape[0] // 2
            mine = out_refs[t].at[pl.ds(c * H, H), :]
            cp = pltpu.make_async_remote_copy(src_ref=mine, dst_ref=mine, send_sem=send_sems.at[t],
                                              recv_sem=recv_sems.at[t], device_id=(x, y, 1 - c),
                                              device_id_type=MESH)
            cp.start()
            cps.append(cp)
        for t in range(n):
            H = ops[t].shape[0] // 2
            other = out_refs[t].at[pl.ds((1 - c) * H, H), :]
            pltpu.make_async_remote_copy(src_ref=other, dst_ref=other, send_sem=send_sems.at[t],
                                         recv_sem=recv_sems.at[t], device_id=(x, y, 1 - c),
                                         device_id_type=MESH).wait_recv()
        for cp in cps:
            cp.wait_send()

    return pl.pallas_call(
        body, name=name, in_specs=[HBM_SPEC] * n, out_specs=[HBM_SPEC] * n,
        out_shape=[jax.ShapeDtypeStruct(o.shape, o.dtype) for o in ops],
        input_output_aliases={t: t for t in range(n)},
        scratch_shapes=[pltpu.SemaphoreType.DMA((n,)), pltpu.SemaphoreType.DMA((n,))],
    )(*ops)


def _direct_copies(src_refs, land_refs, x, y, c):
    out = []
    for t in range(len(src_refs)):
        H = src_refs[t].shape[1] // 2
        for k in range(1, 8):
            px, py, pc = x ^ (k >> 2), y ^ ((k >> 1) & 1), c ^ (k & 1)
            out.append((7 * t + k - 1, src_refs[t].at[2 * px + py, pl.ds(pc * H, H), :], land_refs[t].at[k - 1],
                        (px, py, pc)))
    return out


def _direct_sum(g, t, cq, name):
    _, a, b = g.shape
    H = a // 2
    tr = _sum_rows_tile(H)

    def body(cq_ref, g_ref, t_ref, o_ref):
        acc = g_ref[...].astype(F32)
        for k in range(7):
            acc = acc + t_ref[k].astype(F32)
        o_ref[...] = acc

    grid_spec = pltpu.PrefetchScalarGridSpec(
        num_scalar_prefetch=1, grid=(H // tr,),
        in_specs=[pl.BlockSpec((None, None, tr, b), lambda i, cq_ref: (cq_ref[1], cq_ref[0], i, 0)),
                  pl.BlockSpec((7, tr, b), lambda i, cq_ref: (0, i, 0))],
        out_specs=pl.BlockSpec((None, tr, b), lambda i, cq_ref: (cq_ref[0], i, 0)))
    out = pl.pallas_call(
        body, name=name, grid_spec=grid_spec, out_shape=_out((2, H, b), F32),
        compiler_params=_cparams(("parallel",)),
    )(cq, g.reshape(N_CHIPS, 2, H, b), t)
    return out.reshape(a, b)


def _reduce_direct_start(gs, tag):
    lands = [lax.empty((7, g.shape[1] // 2, g.shape[2]), g.dtype) for g in gs]
    return _ici_start(gs, lands, jnp.zeros(TOKEN, F32), _direct_copies, "rs_direct_start_" + tag, per_src=7)


def _reduce_direct_finish(handle, cq, after, tag):
    gs, rs = _ici_wait(handle, after, _direct_copies, "rs_direct_wait_" + tag)
    fs = [_direct_sum(g, r, cq, "rs_direct_sum") for g, r in zip(gs, rs)]
    return _join_halves(fs, "rs_join_" + tag)


def _reduce_scatter_start(gs, cq, after, tag):
    ts = _swap_halves(gs, "rs_swap_" + tag)
    ps = [_pair_sum(g, t, cq, "rs_pair_sum") for g, t in zip(gs, ts)]
    return _scatter_start(ps, after, "rs_scatter_start_" + tag)


def _reduce_scatter_finish(handle, cq, after, tag):
    ps, rs = _scatter_wait(handle, after, "rs_scatter_wait_" + tag)
    fs = [_chip_sum(p, r, cq, "rs_chip_sum") for p, r in zip(ps, rs)]
    return _join_halves(fs, "rs_join_" + tag)


def _all_reduce_small(v):
    n = v.shape[0]

    def body(v_ref, out_ref, buf, send_sems, recv_sems):
        x, y, c = _place()
        me = 4 * x + 2 * y + c
        buf[me] = v_ref[...]
        cps = []
        for k in range(1, 8):
            peer = (x ^ (k >> 2), y ^ ((k >> 1) & 1), c ^ (k & 1))
            cp = pltpu.make_async_remote_copy(src_ref=v_ref, dst_ref=buf.at[me], send_sem=send_sems.at[k - 1],
                                              recv_sem=recv_sems.at[k - 1], device_id=peer, device_id_type=MESH)
            cp.start()
            cps.append(cp)
        for k in range(1, 8):
            px, py, pc = x ^ (k >> 2), y ^ ((k >> 1) & 1), c ^ (k & 1)
            land = buf.at[4 * px + 2 * py + pc]
            pltpu.make_async_remote_copy(src_ref=land, dst_ref=land, send_sem=send_sems.at[k - 1],
                                         recv_sem=recv_sems.at[k - 1], device_id=(px, py, pc),
                                         device_id_type=MESH).wait_recv()
        for cp in cps:
            cp.wait_send()
        acc = buf[0]
        for d in range(1, 8):
            acc = acc + buf[d]
        out_ref[...] = acc

    vm = pl.BlockSpec(memory_space=pltpu.VMEM)
    return pl.pallas_call(
        body, name="all_reduce_small", in_specs=[vm], out_specs=vm,
        out_shape=jax.ShapeDtypeStruct((n, 128), F32),
        scratch_shapes=[pltpu.VMEM((8, n, 128), F32), pltpu.SemaphoreType.DMA((7,)), pltpu.SemaphoreType.DMA((7,))],
    )(v)


SMALL_GATHER = (16, 1024)
SMALL_FULL = sum(_size(_full_shape(n)) for n in SMALL)
SMALL_FULL_ROWS = -(-SMALL_FULL // 128 // 8) * 8


def _layer_shards(w, i, q):
    kind, j = MIXER[i % 3], i // 3
    out = {n: w[n][i].astype(BF16) for n in COMMON_BIG}
    if kind == 'gla':
        win = jnp.zeros((D_MODEL, GLA_WIN), F32)
        win = lax.dynamic_update_slice(win, w['gla_w_in'][j], (0, (GLA_SHARD - GLA_WIN_STEP) * q))
        out['gla_w_in'] = win.astype(BF16)
        out['gla_w_out'] = w['gla_w_out'][j].astype(BF16)
    elif kind == 'mla':
        out['mla_w_in'] = jnp.pad(w['mla_w_in'][j], ((0, 0), (0, MLA_IN_PAD - MLA_IN))).astype(BF16)
        for n in ('mla_w_uq', 'mla_w_ukv', 'mla_w_out'):
            out[n] = w[n][j].astype(BF16)
    else:
        out['conv_w_in'] = w['conv_w_in'][j].astype(BF16)
        out['conv_w_out'] = w['conv_w_out'][j].astype(BF16)
    return out


def _rows_joined(g):
    return g.reshape(g.shape[0] * g.shape[1], g.shape[2])


def _cols_joined(g):
    return jnp.moveaxis(g, 0, 1).reshape(g.shape[1], -1)


def _layer_weights(g, i):
    kind = MIXER[i % 3]
    W = {}
    if 'mlp_w1' in g:
        W = {'w1': g['mlp_w1'], 'w2': _rows_joined(g['mlp_w2']), 'gate': _rows_joined(g['ple_w_gate']),
             'proj': g['ple_w_proj']}
    if kind == 'gla' and 'gla_w_out' in g:
        W['w_out'] = _rows_joined(g['gla_w_out'])
    if kind == 'gla' and 'gla_w_in' in g:
        parts = []
        for qq in range(N_CHIPS):
            lo = g['gla_w_in'][qq][:, :128]
            if qq > 0:
                lo = lo + g['gla_w_in'][qq - 1][:, GLA_WIN_STEP:]
            parts += [lo, g['gla_w_in'][qq][:, 128:GLA_WIN_STEP]]
        parts.append(g['gla_w_in'][N_CHIPS - 1][:, GLA_WIN_STEP:])
        W['w_in'] = jnp.concatenate(parts, axis=1)
    elif kind == 'mla':
        W['w_in'] = _rows_joined(g['mla_w_in'])
        uq = _cols_joined(g['mla_w_uq']).reshape(MLA_QR, MLA_HEADS, MLA_NOPE + MLA_ROPE)
        W['w_uq'] = jnp.pad(uq, ((0, 0), (0, 0), (0, MLA_QH - MLA_NOPE - MLA_ROPE))).reshape(MLA_QR, -1)
        ukv = _cols_joined(g['mla_w_ukv']).reshape(MLA_KVR, MLA_HEADS, 2, 128)
        W['w_ukv'] = ukv.transpose(0, 2, 1, 3).reshape(MLA_KVR, -1)
        W['w_out'] = _rows_joined(g['mla_w_out'])
    elif kind == 'conv':
        W['w_in'] = g['conv_w_in']
        W['w_out'] = _rows_joined(g['conv_w_out'])
    return W


def _pack_small_shards(w):
    flat = jnp.concatenate([w[n].reshape(-1) for n in SMALL_SHARDED])
    return jnp.pad(flat, (0, _size(SMALL_GATHER) - flat.shape[0])).reshape(SMALL_GATHER)


def _unpack_small_gathered(g):
    flat, out, off = g.reshape(N_CHIPS, -1), {}, 0
    for n in SMALL_SHARDED:
        shape, ax = WSPEC[n]
        seg = flat[:, off:off + _size(shape)].reshape((N_CHIPS,) + shape)
        out[n] = jnp.moveaxis(seg, 0, ax).reshape(_full_shape(n))
        off += _size(shape)
    return out


def _pack_small(vals):
    flat = jnp.concatenate([vals[n].reshape(-1) for n in SMALL])
    return jnp.pad(flat, (0, SMALL_FULL_ROWS * 128 - flat.shape[0])).reshape(SMALL_FULL_ROWS, 128)


def _unpack_small(packed, q):
    flat = packed.reshape(-1)
    out, off = {}, 0
    for n in SMALL:
        shape, ax = WSPEC[n]
        full = flat[off:off + _size(_full_shape(n))].reshape(_full_shape(n))
        off += _size(_full_shape(n))
        out[n] = full if ax is None else lax.dynamic_slice_in_dim(full, q * shape[ax], shape[ax], axis=ax)
    return out


def _row_shards(dw):
    return dw.reshape(N_CHIPS, dw.shape[0] // N_CHIPS, dw.shape[1])


def _col_shards(dw):
    return jnp.moveaxis(dw.reshape(dw.shape[0], N_CHIPS, -1), 1, 0)


def _row(v):
    return v.reshape(1, -1)


def _layer_fwd(i, xin, xin_b, p_i, W, sm, cosp, sinp, rest=None):
    kind, j = MIXER[i % 3], i // 3
    sv = {'xin': xin, 'xin_b': xin_b}
    if kind == 'gla':
        w_up = jnp.pad(sm['gla_w_gate_up'][j].astype(BF16), ((0, 128 - GLA_RANK), (0, 0)))
        pin = _mm(xin_b, W['w_in'], name="gla_in", tn=640, tm=FULL_ROWS)
        la = _mm(pin, w_up, name="gla_gate", K=128, tk=128, a_off=(0, (GLA_IN_PAD - 128) // 128), tn=512,
                 extras=[(_row(sm['gla_b_gate'][j]), 'n')],
                 epilogue=lambda acc, b: (_log_sigmoid(acc + b) * (1.0 / GLA_TAU),))
        o, states = _gla_fwd(pin, la)
        yb = _gla_post_fwd(o, pin, _row(sm['gla_norm_g'][j]))
        if rest is not None:
            W = {**W, **rest(yb)}
        mixed = yb
        sv.update(w_up=w_up, pin=pin, la=la, o=o, states=states, yb=yb)
    elif kind == 'mla':
        gq, gkv = sm['mla_q_norm'][j:j + 1], sm['mla_kv_norm'][j:j + 1]
        cin = _mm(xin_b, W['w_in'], name="mla_in", tn=640, tm=FULL_ROWS)
        qn, kvn, kr = _mla_norm_fwd(cin, gq, gkv, cosp, sinp)
        qr = _mla_qrope_fwd(_mm(qn, W['w_uq'], name="mla_uq"), cosp, sinp)
        knv = _mm(kvn, W['w_ukv'], name="mla_ukv", out_dtypes=(BF16,))
        o = _mla_attn_fwd(qr, knv, kr)
        ob = o.astype(BF16)
        mixed = ob
        sv.update(gq=gq, gkv=gkv, cin=cin, qn=qn, kvn=kvn, kr=kr, qr=qr, knv=knv, o=o, ob=ob)
    else:
        w8 = jnp.pad(sm['conv_w'][j], ((0, 5), (0, 0)))
        bcu = _mm(xin_b, W['w_in'], name="conv_in", tn=768, b_sh=True, tm=FULL_ROWS)
        yb = _conv_fwd(bcu, w8)
        mixed = yb
        sv.update(w8=w8, bcu=bcu, yb=yb)
    g0, b0 = _row(sm['ln_g'][i, 0]), _row(sm['ln_b'][i, 0])
    g1, b1 = _row(sm['ln_g'][i, 1]), _row(sm['ln_b'][i, 1])
    ln = dict(tm=512, tn=D_MODEL, out_dtypes=(F32, BF16, F32), epilogue=_ln_fwd_epilogue)
    x1, x1b, v0 = _mm(mixed, W['w_out'], name="mix_out_ln", extras=[(xin, 'mn'), (g0, 'n'), (b0, 'n')], **ln)
    ab = _mm(x1b, W['w1'], name="mlp_up", out_dtypes=(BF16,), b_sh=True, tm=FULL_ROWS,
             epilogue=lambda acc: (jnp.square(jnp.maximum(acc, 0.0)),))
    x2, x2b, v1 = _mm(ab, W['w2'], name="mlp_down_ln", tk=D_FF, extras=[(x1, 'mn'), (g1, 'n'), (b1, 'n')], **ln)
    pp = _mm(p_i, W['proj'], name="ple_proj", tn=256, b_sh=True)
    z, x3, x3b = _mm(x2b, W['gate'], name="ple_gate", out_dtypes=(F32, F32, BF16),
                     extras=[(x2, 'mn'), (pp, 'mn')],
                     epilogue=lambda acc, xv, pv: (acc,) + (xv + jax.nn.sigmoid(acc) * pv,) * 2)
    sv.update(v0=v0, x1b=x1b, ab=ab, v1=v1, x2b=x2b, pp=pp, z=z, g0=g0, g1=g1)
    return x3, x3b, sv, W


def _layer_bwd(i, grads_in, p_i, W, sm, sv, cosp, sinp, token, early=None, below=None):
    kind, j = MIXER[i % 3], i // 3
    big, small = {}, {}
    dx, dpp_b, dz_b = grads_in
    big['ple_w_proj'] = _mm(p_i, dpp_b, ta=True, name="ple_proj_dw", tn=256, out_sh=True, out_dtypes=(BF16,))
    big['ple_w_gate'] = _row_shards(_mm(sv['x2b'], dz_b, ta=True, name="dw_dd", out_dtypes=(BF16,)))
    ln = dict(tb=True, tm=512, tn=D_MODEL, out_dtypes=(F32, BF16), n_sums=2)
    (dv1, dv1b), (dg1, db1) = _mm(dz_b, W['gate'], name="ple_gate_dx_ln", epilogue=_ln_bwd_epilogue(1.0),
                                  extras=[(dx, 'mn'), (sv['v1'], 'mn'), (sv['g1'], 'n'), (token, 'whole')], **ln)
    big['mlp_w2'] = _row_shards(_mm(sv['ab'], dv1b, ta=True, name="mlp_down_dw", out_dtypes=(BF16,)))
    dub = _mm(dv1b, W['w2'], tb=True, name="mlp_down_dx", out_dtypes=(BF16,), tm=FULL_ROWS,
              extras=[(sv['ab'], 'mn')], epilogue=lambda acc, a: (acc * (2.0 * jnp.sqrt(a.astype(F32))),))
    big['mlp_w1'] = _mm(sv['x1b'], dub, ta=True, name="mlp_up_dw", out_sh=True, out_dtypes=(BF16,))
    order = []
    if early is not None:
        order, big = [(early(big), 'whole')], {}
    (dv0, dv0b), (dg0, db0) = _mm(dub, W['w1'], name="mlp_up_dx_ln", b_sh=True, tk=D_FF, epilogue=_ln_bwd_epilogue(ALPHA),
                                  extras=[(dv1, 'mn'), (sv['v0'], 'mn'), (sv['g0'], 'n')] + order, **ln)
    small['ln_g'] = jnp.stack([dg0[0], dg1[0]])
    small['ln_b'] = jnp.stack([db0[0], db1[0]])
    resid = dict(tb=True, tn=D_MODEL, tm=512 if below else 1024, epilogue=_input_grad_epilogue,
                 extras=[(dv0, 'mn')] + [(a, 'mn') for a in below or ()],
                 out_dtypes=(F32, BF16, BF16) if below else (F32,))
    if kind == 'gla':
        big['gla_w_out'] = _row_shards(_mm(sv['yb'], dv0b, ta=True, name="dw_dd", out_dtypes=(BF16,)))
        dy = _mm(dv0b, W['w_out'], tb=True, name="dx_dd", tn=1024)
        do, dr_b, dng = _gla_post_bwd(dy, sv['o'], sv['pin'], _row(sm['gla_norm_g'][j]))
        dq_b, dk_b, dvv_b, dla = _gla_bwd(sv['pin'], sv['la'], sv['states'], do)
        dzg_b, dbg = _gla_gate_bwd(dla, sv['la'])
        dw_up = _mm(sv['pin'], dzg_b, ta=True, name="gla_gate_dw", M=128, tm=128,
                    a_off=(0, (GLA_IN_PAD - 128) // 128))
        dglr_b = _mm(dzg_b, sv['w_up'], tb=True, name="gla_gate_dx", out_dtypes=(BF16,))
        dpin_b = jnp.concatenate([dq_b, dk_b, dvv_b, dr_b, dglr_b], axis=1)
        dw_in = _mm(sv['xin_b'], dpin_b, ta=True, name="gla_in_dw", tn=640, out_dtypes=(BF16,))
        dxin = _mm(dpin_b, W['w_in'], name="gla_in_dx", tk=GLA_IN_PAD, **resid)
        big['gla_w_in'] = jnp.stack([dw_in[:, GLA_WIN_STEP * qq:GLA_WIN_STEP * qq + GLA_WIN]
                                     for qq in range(N_CHIPS)])
        small.update(gla_w_gate_up=dw_up[:GLA_RANK], gla_b_gate=dbg[0], gla_norm_g=dng[0])
    elif kind == 'mla':
        big['mla_w_out'] = _row_shards(_mm(sv['ob'], dv0b, ta=True, name="dw_dd", out_dtypes=(BF16,)))
        do = _mm(dv0b, W['w_out'], tb=True, name="dx_dd", tn=1024)
        dqr, dkn_b, dvv_b, dkr = _mla_attn_bwd(sv['qr'], sv['knv'], sv['kr'], sv['o'], do)
        dq_b = _mla_qrope_bwd(dqr, cosp, sinp)
        dw_uq = _mm(sv['qn'], dq_b, ta=True, name="mla_up_dw", out_dtypes=(BF16,))
        dqn = _mm(dq_b, W['w_uq'], tb=True, name="mla_up_dx")
        dknv_b = jnp.concatenate([dkn_b, dvv_b], axis=1)
        dw_ukv = _mm(sv['kvn'], dknv_b, ta=True, name="mla_up_dw", out_dtypes=(BF16,))
        dkvn = _mm(dknv_b, W['w_ukv'], tb=True, name="mla_up_dx")
        dcin_b, dgq, dgkv = _mla_norm_bwd(sv['cin'], dqn, dkvn, dkr, sv['gq'], sv['gkv'], cosp, sinp)
        big['mla_w_in'] = _row_shards(_mm(sv['xin_b'], dcin_b, ta=True, name="mla_in_dw", tn=640,
                                          out_dtypes=(BF16,)))
        dxin = _mm(dcin_b, W['w_in'], name="mla_in_dx", tk=MLA_IN_PAD, **resid)
        big['mla_w_uq'] = _col_shards(
            dw_uq.reshape(MLA_QR, MLA_HEADS, MLA_QH)[:, :, :MLA_NOPE + MLA_ROPE].reshape(MLA_QR, -1))
        big['mla_w_ukv'] = _col_shards(
            dw_ukv.reshape(MLA_KVR, 2, MLA_HEADS, 128).transpose(0, 2, 1, 3).reshape(MLA_KVR, -1))
        small.update(mla_q_norm=dgq[0], mla_kv_norm=dgkv[0])
    else:
        big['conv_w_out'] = _row_shards(_mm(sv['yb'], dv0b, ta=True, name="dw_dd", out_dtypes=(BF16,)))
        dy = _mm(dv0b, W['w_out'], tb=True, name="dx_dd", tn=1024)
        db_b, dc_b, du_b, dw8 = _conv_bwd(sv['bcu'], sv['w8'], dy)
        dbcu_b = jnp.concatenate([db_b, dc_b, du_b], axis=1)
        big['conv_w_in'] = _mm(sv['xin_b'], dbcu_b, ta=True, name="conv_in_dw", tn=768, out_sh=True,
                               out_dtypes=(BF16,))
        dxin = _mm(dbcu_b, W['w_in'], name="conv_in_dx", tk=3 * D_MODEL, b_sh=True, **resid)
        small['conv_w'] = dw8[:3]
    return (dxin if below else (dxin,)), big, small


def _rope_tables(positions):
    inv_freq = ROPE_BASE ** (-jnp.arange(0, MLA_ROPE // 2, dtype=F32) * (2.0 / MLA_ROPE))
    ang = positions.astype(F32)[:, None] * inv_freq
    zeros = jnp.zeros((positions.shape[0], 64), F32)
    return (jnp.concatenate([jnp.cos(ang), jnp.cos(ang), zeros], axis=1),
            jnp.concatenate([jnp.sin(ang), jnp.sin(ang), zeros], axis=1))


FIRST_NEEDED = ['gla_w_in']


def _start_gathers(w, q):
    token, started = jnp.zeros(TOKEN, F32), []
    for i in range(DEPTH):
        sh = _layer_shards(w, i, q)
        groups = [list(sh)] if i > 0 else [FIRST_NEEDED, [n for n in sh if n not in FIRST_NEEDED]]
        for k, names in enumerate(groups):
            ops = [sh[n] for n in names]
            if i == 0 and k == 0:
                ops.append(_pack_small_shards(w))
            tag = "l%d%s" % (i, "ab"[k] if i == 0 else "")
            handle = _gather_start(ops, token, "ag_start_" + tag)
            token = handle[4]
            started.append((handle, names, tag))
    return started, token


def _finish_gather(entry, after):
    handle, names, tag = entry
    srcs, lands = _gather_wait(handle, after, "ag_wait_" + tag)
    got = _gather_finish(srcs, lands, "ag_finish_" + tag)
    return dict(zip(names, got)), got[-1]


def _local_shard_grad(name, g, q):
    if name == 'gla_w_in':
        return lax.dynamic_slice_in_dim(g, (GLA_SHARD - GLA_WIN_STEP) * q, GLA_SHARD, axis=1)
    if name == 'mla_w_in':
        return g[:, :MLA_IN]
    return g


def kernel(x, p, positions, gla_w_in, gla_w_gate_up, gla_b_gate, gla_norm_g, gla_w_out, mla_w_in, mla_q_norm, mla_kv_norm, mla_w_uq, mla_w_ukv, mla_w_out, conv_w_in, conv_w, conv_w_out, ln_g, ln_b, mlp_w1, mlp_w2, ple_w_gate, ple_w_proj, loss_target, m_gla_w_in, m_gla_w_gate_up, m_gla_b_gate, m_gla_norm_g, m_gla_w_out, m_mla_w_in, m_mla_q_norm, m_mla_kv_norm, m_mla_w_uq, m_mla_w_ukv, m_mla_w_out, m_conv_w_in, m_conv_w, m_conv_w_out, m_ln_g, m_ln_b, m_mlp_w1, m_mlp_w2, m_ple_w_gate, m_ple_w_proj, v_gla_w_in, v_gla_w_gate_up, v_gla_b_gate, v_gla_norm_g, v_gla_w_out, v_mla_w_in, v_mla_q_norm, v_mla_kv_norm, v_mla_w_uq, v_mla_w_ukv, v_mla_w_out, v_conv_w_in, v_conv_w, v_conv_w_out, v_ln_g, v_ln_b, v_mlp_w1, v_mlp_w2, v_ple_w_gate, v_ple_w_proj):
    args = locals()
    w = {n: args[n] for n in WNAMES}
    m = {n: args['m_' + n] for n in WNAMES}
    v = {n: args['v_' + n] for n in WNAMES}
    q = 2 * lax.axis_index("x") + lax.axis_index("y")
    cq = jnp.stack([lax.axis_index("c"), q]).astype(jnp.int32)

    cosp, sinp = _rope_tables(positions[0])
    started, after = _start_gathers(w, q)
    xin, saved, layers, sm = x[0], [], [], None
    xin_b = xin.astype(BF16)
    for i in range(DEPTH):
        got, last = _finish_gather(started[i + 1 if i else 0], after)
        rest = None
        if i == 0:
            sm = _unpack_small_gathered(last)
            sm['mla_q_norm'], sm['mla_kv_norm'] = w['mla_q_norm'], w['mla_kv_norm']
            rest = lambda after: _layer_weights(_finish_gather(started[1], after)[0], 0)
        xin, xin_b, sv, W = _layer_fwd(i, xin, xin_b, p[i, 0], _layer_weights(got, i), sm, cosp, sinp, rest)
        layers.append(W)
        saved.append(sv)
        after = xin
    *grads_in, loss_cols = _loss_head(xin, loss_target[0], saved[-1]['z'], saved[-1]['pp'])
    loss = lax.psum(jnp.sum(loss_cols[0]), ("x", "y", "c"))

    gbig = {n: [None] * WSPEC[n][0][0] for n in BIG}
    gsmall = {n: [None] * _full_shape(n)[0] for n in SMALL}
    pending = []

    def start(grads, i, tag):
        names = list(grads)
        gs = [grads[n] for n in names]
        handle = _reduce_direct_start(gs, tag) if i > 0 else _reduce_scatter_start(gs, cq, jnp.zeros(TOKEN, F32), tag)
        pending.append((handle, names, i, tag))
        return handle[4]

    def finish(above, after):
        for entry in [e for e in pending if e[2] > above]:
            pending.remove(entry)
            handle, names, i, tag = entry
            reduced = (_reduce_direct_finish if i > 0 else _reduce_scatter_finish)(handle, cq, after, tag)
            for n, g in zip(names, reduced):
                gbig[n][i if n in COMMON_BIG else i // 3] = _local_shard_grad(n, g, q)

    token = jnp.zeros(TOKEN, F32)
    for i in reversed(range(DEPTH)):
        early = (lambda grads: start(grads, 0, "l0a")) if i == 0 else None
        below = (saved[i - 1]['z'], saved[i - 1]['pp']) if i > 0 else None
        grads_in, big, small = _layer_bwd(i, grads_in, p[i, 0], layers[i], sm, saved[i], cosp, sinp, token, early,
                                          below)
        dx = grads_in[0]
        token = start(big, i, "l%d%s" % (i, "b" if i == 0 else ""))
        finish(i, dx)
        for n, g in small.items():
            gsmall[n][i if n in ('ln_g', 'ln_b') else i // 3] = g
    finish(-1, token)
    gsm = _unpack_small(_all_reduce_small(_pack_small({n: jnp.stack(g) for n, g in gsmall.items()})), q)

    grad, delta, new_m, new_v = {}, {}, {}, {}
    for n in BIG:
        grad[n], delta[n], new_m[n], new_v[n] = _adamw(w[n], m[n], v[n], gbig[n], "adamw_" + n)
    total = sum(_size(WSPEC[n][0]) for n in SMALL)
    rows = -(-total // 128 // 8) * 8

    def pack(dct):
        flat = jnp.concatenate([dct[n].reshape(-1) for n in SMALL])
        return jnp.pad(flat, (0, rows * 128 - total), constant_values=1.0).reshape(1, rows, 128)

    res = _adamw(pack(w), pack(m), pack(v), [pack(gsm)[0]], "adamw_small")
    for out, packed in zip((grad, delta, new_m, new_v), res):
        flat, off = packed.reshape(-1), 0
        for n in SMALL:
            sz = _size(WSPEC[n][0])
            out[n] = flat[off:off + sz].reshape(WSPEC[n][0])
            off += sz
    return (loss, dx[None], *[grad[n] for n in WNAMES], *[delta[n] for n in WNAMES],
            *[new_m[n] for n in WNAMES], *[new_v[n] for n in WNAMES])
```

```python
import functools

import numpy as np
import jax
import jax.numpy as jnp
from jax import lax
from jax.experimental import pallas as pl
from jax.experimental.pallas import tpu as pltpu

F32 = jnp.float32
BF16 = jnp.bfloat16
MESH = pl.DeviceIdType.MESH

D_MODEL = 1024
DEPTH = 4
CHUNK = 64
ALPHA = (2 * DEPTH) ** 0.25
LN_EPS = 1e-5
RMS_EPS = 1e-6
PLE_DIM = 256
D_FF = 4 * D_MODEL
GLA_HEADS = 4
GLA_DK = 128
GLA_DV = 256
GLA_RANK = 16
GLA_TAU = 16.0
GLA_HK = GLA_HEADS * GLA_DK
GLA_HV = GLA_HEADS * GLA_DV
GLA_IN = 2 * GLA_HK + GLA_HV + D_MODEL + GLA_RANK
GLA_IN_PAD = 2 * GLA_HK + GLA_HV + D_MODEL + 128
GLA_SHARD = GLA_IN // 4
GLA_WIN = 896
GLA_WIN_STEP = 768
MLA_HEADS = 8
MLA_NOPE = 128
MLA_ROPE = 64
MLA_V = 128
MLA_QR = 256
MLA_KVR = 256
MLA_IN = MLA_QR + MLA_KVR + MLA_ROPE
MLA_IN_PAD = MLA_QR + MLA_KVR + 128
MLA_QH = 256
ROPE_BASE = 10000.0
ADAM_LR = 0.001
ADAM_B1 = 0.9
ADAM_B2 = 0.999
ADAM_EPS = 1e-08
ADAM_WD = 0.01
ADAM_STEP = 10

VMEM_LIMIT = 48 * 1024 * 1024
FULL_ROWS = 2048
N_CHIPS = 4

WSPEC = {
    'gla_w_in': ((2, 1024, 772), 2), 'gla_w_gate_up': ((2, 16, 128), 2), 'gla_b_gate': ((2, 128), 1),
    'gla_norm_g': ((2, 64), 1), 'gla_w_out': ((2, 256, 1024), 1), 'mla_w_in': ((1, 256, 576), 1),
    'mla_q_norm': ((1, 256), None), 'mla_kv_norm': ((1, 256), None), 'mla_w_uq': ((1, 256, 384), 2),
    'mla_w_ukv': ((1, 256, 512), 2), 'mla_w_out': ((1, 256, 1024), 1), 'conv_w_in': ((1, 1024, 768), 2),
    'conv_w': ((1, 3, 256), 2), 'conv_w_out': ((1, 256, 1024), 1), 'ln_g': ((4, 2, 256), 2),
    'ln_b': ((4, 2, 256), 2), 'mlp_w1': ((4, 1024, 1024), 2), 'mlp_w2': ((4, 1024, 1024), 1),
    'ple_w_gate': ((4, 256, 1024), 1), 'ple_w_proj': ((4, 256, 256), 2),
}
WNAMES = list(WSPEC)
BIG = ['gla_w_in', 'gla_w_out', 'mla_w_in', 'mla_w_uq', 'mla_w_ukv', 'mla_w_out', 'conv_w_in', 'conv_w_out',
       'mlp_w1', 'mlp_w2', 'ple_w_gate', 'ple_w_proj']
SMALL_SHARDED = ['gla_w_gate_up', 'gla_b_gate', 'gla_norm_g', 'conv_w', 'ln_g', 'ln_b']
SMALL = SMALL_SHARDED + ['mla_q_norm', 'mla_kv_norm']
MIXER = ['gla', 'mla', 'conv']
LAYER_BIG = {'gla': ['gla_w_in', 'gla_w_out'], 'mla': ['mla_w_in', 'mla_w_uq', 'mla_w_ukv', 'mla_w_out'],
             'conv': ['conv_w_in', 'conv_w_out']}
COMMON_BIG = ['mlp_w1', 'mlp_w2', 'ple_w_gate', 'ple_w_proj']


def _size(shape):
    return int(np.prod(shape))


def _full_shape(name):
    shape, ax = WSPEC[name]
    if ax is None:
        return shape
    return tuple(s * N_CHIPS if i == ax else s for i, s in enumerate(shape))


def _cparams(sem=None):
    return pltpu.CompilerParams(dimension_semantics=sem, vmem_limit_bytes=VMEM_LIMIT)


def _out(shape, dtype):
    return pltpu.HBM(shape, dtype)


def _hbm(v):
    return pltpu.with_memory_space_constraint(v, pltpu.HBM)


def _mm(a, b, *, name, ta=False, tb=False, M=None, N=None, K=None, out_dtypes=(F32,), epilogue=None, extras=(),
        tm=1024, tn=512, tk=None, a_off=(0, 0), b_sh=False, out_sh=False, n_sums=0):
    if M is None:
        M = a.shape[1] if ta else a.shape[0]
    if K is None:
        K = a.shape[0] if ta else a.shape[1]
    if b_sh:
        kw, nq = b.shape[1], b.shape[2]
        n_b, k_b = (kw, N_CHIPS * nq) if tb else (N_CHIPS * nq, kw)
        N = n_b if N is None else N
        assert K == k_b
    elif N is None:
        N = b.shape[0] if tb else b.shape[1]
    if tk is None:
        tk = FULL_ROWS if ta else 1024
    tm, tn, tk = min(tm, M), min(tn, N), min(tk, K)
    assert M % tm == 0 and N % tn == 0 and K % tk == 0, (name, M, N, K, tm, tn, tk)
    nk = K // tk
    n_ex, n_out = len(extras), len(out_dtypes)
    assert n_sums == 0 or tn == N

    n_b = N_CHIPS if (b_sh and tb and tk == K) else 1

    def body(a_ref, *rest):
        b_refs, rest = rest[:n_b], rest[n_b:]
        ex_refs, out_refs = rest[:n_ex], rest[n_ex:n_ex + n_out]
        sum_refs = rest[n_ex + n_out:n_ex + n_out + n_sums]
        first_rows = pl.program_id(0) == 0
        dims = ((((0,) if ta else (1,)), ((1,) if tb else (0,))), ((), ()))
        if n_b == 1:
            part = lax.dot_general(a_ref[...].astype(BF16), b_refs[0][...].astype(BF16), dims,
                                   preferred_element_type=F32)
        else:
            part = sum(lax.dot_general(a_ref[:, s * nq:(s + 1) * nq].astype(BF16), b_refs[s][...].astype(BF16), dims,
                                       preferred_element_type=F32) for s in range(n_b))

        def finish(acc):
            res = (acc,) if epilogue is None else epilogue(acc, *[r[...] for r in ex_refs])
            if n_sums:
                res, sums = res

                @pl.when(first_rows)
                def _():
                    for r in sum_refs:
                        r[...] = jnp.zeros(r.shape, F32)

                for r, v in zip(sum_refs, sums):
                    r[...] += jnp.broadcast_to(v, r.shape)
            for r, v in zip(out_refs, res):
                r[...] = v.astype(r.dtype)

        if nk == 1:
            finish(part)
        else:
            acc_ref = rest[-1]
            k = pl.program_id(2)

            @pl.when(k == 0)
            def _():
                acc_ref[...] = part

            @pl.when(k > 0)
            def _():
                acc_ref[...] += part

            @pl.when(k == nk - 1)
            def _():
                finish(acc_ref[...])

    if ta:
        a_spec = pl.BlockSpec((tk, tm), lambda i, j, k: (k + a_off[0], i + a_off[1]))
    else:
        a_spec = pl.BlockSpec((tm, tk), lambda i, j, k: (i + a_off[0], k + a_off[1]))
    once = dict(pipeline_mode=pl.Buffered(1)) if (tn == N and nk == 1) else {}
    if n_b > 1:
        b_specs = [pl.BlockSpec((None, tn, nq), functools.partial(lambda i, j, k, s: (s, j, 0), s=s), **once)
                   for s in range(n_b)]
    elif b_sh and tb:
        assert nq % tk == 0
        per = nq // tk
        b_spec = pl.BlockSpec((None, tn, tk), lambda i, j, k: (k // per, j, k % per), **once)
    elif b_sh:
        assert nq % tn == 0
        per = nq // tn
        b_spec = pl.BlockSpec((None, tk, tn), lambda i, j, k: (j // per, k, j % per), **once)
    elif tb:
        b_spec = pl.BlockSpec((tn, tk), lambda i, j, k: (j, k), **once)
    else:
        b_spec = pl.BlockSpec((tk, tn), lambda i, j, k: (k, j), **once)
    if n_b == 1:
        b_specs = [b_spec]
    ex_specs = []
    for arr, kind in extras:
        if kind == 'mn':
            ex_specs.append(pl.BlockSpec((tm, tn), lambda i, j, k: (i, j)))
        elif kind == 'n':
            ex_specs.append(pl.BlockSpec((1, tn), lambda i, j, k: (0, j)))
        else:
            ex_specs.append(pl.BlockSpec(arr.shape, lambda i, j, k: (0, 0)))
    if out_sh:
        assert (N // N_CHIPS) % tn == 0
        per_o = N // N_CHIPS // tn
        o_spec = pl.BlockSpec((None, tm, tn), lambda i, j, k: (j // per_o, i, j % per_o))
        o_shape = (N_CHIPS, M, N // N_CHIPS)
    else:
        o_spec = pl.BlockSpec((tm, tn), lambda i, j, k: (i, j))
        o_shape = (M, N)
    outs = pl.pallas_call(
        body, name=name, grid=(M // tm, N // tn, nk),
        in_specs=[a_spec] + b_specs + ex_specs,
        out_specs=[o_spec for _ in out_dtypes] + [pl.BlockSpec((8, N), lambda i, j, k: (0, 0))] * n_sums,
        out_shape=[_out(o_shape, d) for d in out_dtypes] + [_out((8, N), F32)] * n_sums,
        scratch_shapes=[pltpu.VMEM((tm, tn), F32)] if nk > 1 else [],
        compiler_params=_cparams(("arbitrary" if n_sums else "parallel", "parallel", "arbitrary")),
    )(a, *[b] * n_b, *[e[0] for e in extras])
    if n_sums:
        return tuple(outs[:n_out]), tuple(outs[n_out:])
    return outs[0] if n_out == 1 else tuple(outs)


def _rowwise(fn, *, name, rows, pars=(), outs=(), accs=(), tm=256):
    S = rows[0][0].shape[0]
    tm = min(tm, S)
    assert S % tm == 0
    n_r, n_p, n_o, n_a = len(rows), len(pars), len(outs), len(accs)

    def body(*refs):
        r_refs, p_refs = refs[:n_r], refs[n_r:n_r + n_p]
        o_refs, a_refs = refs[n_r + n_p:n_r + n_p + n_o], refs[n_r + n_p + n_o:]
        o_vals, a_vals = fn([r[...] for r in r_refs], [p[...] for p in p_refs])
        for r, v in zip(o_refs, o_vals):
            r[...] = v.astype(r.dtype)
        if n_a:
            i = pl.program_id(0)

            @pl.when(i == 0)
            def _():
                for r in a_refs:
                    r[...] = jnp.zeros(r.shape, r.dtype)

            for r, v in zip(a_refs, a_vals):
                r[...] += jnp.broadcast_to(v, r.shape)

    in_specs = [pl.BlockSpec((tm, w), functools.partial(lambda i, o: (i, o), o=off)) for _, w, off in rows]
    in_specs += [pl.BlockSpec(p.shape, functools.partial(lambda i, nd: (0,) * nd, nd=p.ndim)) for p in pars]
    out_specs = [pl.BlockSpec((tm, w), lambda i: (i, 0)) for w, _ in outs]
    out_specs += [pl.BlockSpec((8, w), lambda i: (0, 0)) for w in accs]
    out_shape = [_out((S, w), d) for w, d in outs]
    out_shape += [_out((8, w), F32) for w in accs]
    res = pl.pallas_call(
        body, name=name, grid=(S // tm,), in_specs=in_specs, out_specs=out_specs, out_shape=out_shape,
        compiler_params=_cparams(("arbitrary",)),
    )(*[r[0] for r in rows], *pars)
    return tuple(res)


def _colsum(v):
    return jnp.sum(v, axis=0, keepdims=True)


def _ln_stats(v):
    mu = jnp.mean(v, axis=-1, keepdims=True)
    d = v - mu
    var = jnp.mean(d * d, axis=-1, keepdims=True)
    rstd = lax.rsqrt(var + LN_EPS)
    return d * rstd, rstd


def _ln_fwd_epilogue(h, x, g, b):
    v = ALPHA * x + h
    xhat, _ = _ln_stats(v)
    y = xhat * g + b
    return y, y, v


def _ln_bwd_epilogue(scale):
    def epilogue(acc, resid, v, g, *unused):
        dy = acc + scale * resid
        xhat, rstd = _ln_stats(v)
        dxh = dy * g
        m1 = jnp.mean(dxh, axis=-1, keepdims=True)
        m2 = jnp.mean(dxh * xhat, axis=-1, keepdims=True)
        dv = rstd * (dxh - m1 - xhat * m2)
        return (dv, dv), (_colsum(dy * xhat), _colsum(dy))
    return epilogue


def _ple_gate_grads(dx3, z, pp):
    s = jax.nn.sigmoid(z)
    return dx3 * s, dx3 * pp * s * (1.0 - s)


def _loss_head(y, t, z, pp):
    def fn(r, p):
        d = r[0] - r[1]
        dy = d * (1.0 / D_MODEL)
        return [dy, *_ple_gate_grads(dy, r[2], r[3])], [_colsum(d * d) * (0.5 / D_MODEL)]
    return _rowwise(fn, name="loss_head", rows=[(a, D_MODEL, 0) for a in (y, t, z, pp)],
                    outs=[(D_MODEL, F32), (D_MODEL, BF16), (D_MODEL, BF16)], accs=[D_MODEL])


def _input_grad_epilogue(acc, dv, *below):
    dx = acc + ALPHA * dv
    return (dx, *_ple_gate_grads(dx, *below)) if below else (dx,)


N_LEVELS = 6
GLA_STEP = 2


def _gla_consts():
    C = CHUNK
    A = np.zeros((N_LEVELS + 3, C, C), np.float32)
    masks = np.zeros((N_LEVELS + 1, C, C), np.float32)
    r = np.arange(C)[:, None]
    u = np.arange(C)[None, :]
    for l in range(N_LEVELS):
        half = C >> (l + 1)
        mid = (r // (2 * half)) * (2 * half) + half - 1
        A[l] = np.where(r > mid, (u > mid) & (u <= r), (u > r) & (u <= mid))
        masks[l] = ((r // (2 * half)) == (u // (2 * half))) & (((r // half) % 2) != ((u // half) % 2))
    masks[N_LEVELS] = (r == u)
    A[N_LEVELS] = (u <= r)
    A[N_LEVELS + 1] = (u > r)
    A[N_LEVELS + 2] = 1.0
    A = A.reshape(-1, C)
    return A, np.ascontiguousarray(A.T), masks


def _split3(v):
    hi = v.astype(BF16)
    r1 = v - hi.astype(F32)
    mid = r1.astype(BF16)
    lo = (r1 - mid.astype(F32)).astype(BF16)
    return hi, mid, lo


def _dot_exact01(a01, v):
    hi, mid, lo = _split3(v)
    f = lambda p: jnp.dot(a01, p, preferred_element_type=F32)
    return f(hi) + f(mid) + f(lo)


def _nt(a, b):
    return lax.dot_general(a, b, (((1,), (1,)), ((), ())), preferred_element_type=F32)


def _tn(a, b):
    return lax.dot_general(a, b, (((0,), (0,)), ((), ())), preferred_element_type=F32)


def _nn(a, b):
    return jnp.dot(a, b, preferred_element_type=F32)


def _gla_chunk_terms(q, k, E, m_ref):
    C = CHUNK
    scores = m_ref[N_LEVELS] * _nt(q.astype(BF16), k.astype(BF16))
    qes, kes = [], []
    for l in range(N_LEVELS):
        El = E[l * C:(l + 1) * C]
        qe, ke = (q * El).astype(BF16), (k * El).astype(BF16)
        qes.append(qe)
        kes.append(ke)
        scores = scores + m_ref[l] * _nt(qe, ke)
    return qes, kes, scores


def _head(v, h, w):
    return v[:, h * w:(h + 1) * w]


def _gla_fwd(pin, la):
    S = pin.shape[0]
    NC = S // CHUNK
    C, R = CHUNK, CHUNK * GLA_STEP
    A, _, masks = _gla_consts()

    def body(q_ref, k_ref, v_ref, la_ref, a_ref, m_ref, o_ref, st_ref, state):
        @pl.when(pl.program_id(0) == 0)
        def _():
            state[...] = jnp.zeros(state.shape, F32)

        for ci in range(GLA_STEP):
            rows = pl.ds(ci * C, C)
            E_all = jnp.exp(_dot_exact01(a_ref[...], la_ref[rows, :]))
            q_all = q_ref[rows, :] * (GLA_DK ** -0.5)
            k_all, v_all = k_ref[rows, :], v_ref[rows, :]
            outs = []
            for h in range(GLA_HEADS):
                q, k, E = _head(q_all, h, GLA_DK), _head(k_all, h, GLA_DK), _head(E_all, h, GLA_DK)
                _, _, scores = _gla_chunk_terms(q, k, E, m_ref)
                Eq, Ek, Ee = E[6 * C:7 * C], E[7 * C:8 * C], E[8 * C:9 * C]
                st = state[h]
                st_ref[h, ci] = st
                vb = _head(v_all, h, GLA_DV).astype(BF16)
                outs.append(_nn(scores.astype(BF16), vb) + _nt((q * Eq).astype(BF16), st.astype(BF16)))
                state[h] = st * jnp.concatenate([Ee] * (GLA_DV // C), axis=0) + _tn(vb, (k * Ek).astype(BF16))
            o_ref[rows, :] = jnp.concatenate(outs, axis=1)

    return pl.pallas_call(
        body, name="gla_fwd", grid=(NC // GLA_STEP,),
        in_specs=[pl.BlockSpec((R, GLA_HK), lambda c: (c, 0)),
                  pl.BlockSpec((R, GLA_HK), lambda c: (c, 1)),
                  pl.BlockSpec((R, GLA_HV), lambda c: (c, 2 * GLA_HK // GLA_HV)),
                  pl.BlockSpec((R, GLA_HK), lambda c: (c, 0)),
                  pl.BlockSpec(A.shape, lambda c: (0, 0)),
                  pl.BlockSpec(masks.shape, lambda c: (0, 0, 0))],
        out_specs=[pl.BlockSpec((R, GLA_HV), lambda c: (c, 0)),
                   pl.BlockSpec((GLA_HEADS, GLA_STEP, GLA_DV, GLA_DK), lambda c: (0, c, 0, 0))],
        out_shape=[_out((S, GLA_HV), F32), _out((GLA_HEADS, NC, GLA_DV, GLA_DK), F32)],
        scratch_shapes=[pltpu.VMEM((GLA_HEADS, GLA_DV, GLA_DK), F32)],
        compiler_params=_cparams(("arbitrary",)),
    )(pin, pin, pin, la, jnp.asarray(A, BF16), jnp.asarray(masks))


def _gla_bwd(pin, la, states, do):
    S = pin.shape[0]
    NC = S // CHUNK
    C, R = CHUNK, CHUNK * GLA_STEP
    A, AT, masks = _gla_consts()
    scale = GLA_DK ** -0.5

    def body(q_ref, k_ref, v_ref, la_ref, st_ref, do_ref, a_ref, at_ref, m_ref,
             dq_ref, dk_ref, dv_ref, dla_ref, dstate):
        @pl.when(pl.program_id(0) == 0)
        def _():
            dstate[...] = jnp.zeros(dstate.shape, F32)

        for ci in reversed(range(GLA_STEP)):
            one_chunk(ci, pl.ds(ci * C, C), q_ref, k_ref, v_ref, la_ref, st_ref, do_ref, a_ref, at_ref, m_ref,
                      dq_ref, dk_ref, dv_ref, dla_ref, dstate)

    def one_chunk(ci, rows, q_ref, k_ref, v_ref, la_ref, st_ref, do_ref, a_ref, at_ref, m_ref,
                  dq_ref, dk_ref, dv_ref, dla_ref, dstate):
        E_all = jnp.exp(_dot_exact01(a_ref[...], la_ref[rows, :]))
        q_all = q_ref[rows, :] * scale
        k_all, v_all, do_all = k_ref[rows, :], v_ref[rows, :], do_ref[rows, :]
        dqs, dks, dvs, dXs = [], [], [], []
        for h in range(GLA_HEADS):
            q, k, E = _head(q_all, h, GLA_DK), _head(k_all, h, GLA_DK), _head(E_all, h, GLA_DK)
            qes, kes, scores = _gla_chunk_terms(q, k, E, m_ref)
            Eq, Ek, Ee = E[6 * C:7 * C], E[7 * C:8 * C], E[8 * C:9 * C]
            st, dst = st_ref[h, ci], dstate[h]
            dob, vb = _head(do_all, h, GLA_DV).astype(BF16), _head(v_all, h, GLA_DV).astype(BF16)
            dstb = dst.astype(BF16)
            qEq, kEk = (q * Eq).astype(BF16), (k * Ek).astype(BF16)
            dsc = _nt(dob, vb)
            dvs.append(_tn(scores.astype(BF16), dob) + _nt(kEk, dstb))
            dqEq = _nn(dob, st.astype(BF16))
            dkEk = _nn(vb, dstb)
            Gd = (m_ref[N_LEVELS] * dsc).astype(BF16)
            dq = _nn(Gd, k.astype(BF16)) + dqEq * Eq
            dk = _tn(Gd, q.astype(BF16)) + dkEk * Ek
            dX = []
            for l in range(N_LEVELS):
                El = E[l * C:(l + 1) * C]
                G = (m_ref[l] * dsc).astype(BF16)
                dqe, dke = _nn(G, kes[l]), _tn(G, qes[l])
                dq = dq + dqe * El
                dk = dk + dke * El
                dX.append((dqe * q + dke * k) * El)
            dX.append(dqEq * q * Eq)
            dX.append(dkEk * k * Ek)
            prod = dst * st
            dEe = prod[0:C]
            for i in range(1, GLA_DV // C):
                dEe = dEe + prod[i * C:(i + 1) * C]
            dX.append(dEe * Ee)
            dXs.append(jnp.concatenate(dX, axis=0))
            dqs.append(dq * scale)
            dks.append(dk)
            dstate[h] = dst * jnp.concatenate([Ee] * (GLA_DV // C), axis=0) + _tn(dob, qEq)
        dla_ref[rows, :] = _dot_exact01(at_ref[...], jnp.concatenate(dXs, axis=1))
        dq_ref[rows, :] = jnp.concatenate(dqs, axis=1).astype(dq_ref.dtype)
        dk_ref[rows, :] = jnp.concatenate(dks, axis=1).astype(dk_ref.dtype)
        dv_ref[rows, :] = jnp.concatenate(dvs, axis=1).astype(dv_ref.dtype)

    rc = lambda c: NC // GLA_STEP - 1 - c
    return pl.pallas_call(
        body, name="gla_bwd", grid=(NC // GLA_STEP,),
        in_specs=[pl.BlockSpec((R, GLA_HK), lambda c: (rc(c), 0)),
                  pl.BlockSpec((R, GLA_HK), lambda c: (rc(c), 1)),
                  pl.BlockSpec((R, GLA_HV), lambda c: (rc(c), 2 * GLA_HK // GLA_HV)),
                  pl.BlockSpec((R, GLA_HK), lambda c: (rc(c), 0)),
                  pl.BlockSpec((GLA_HEADS, GLA_STEP, GLA_DV, GLA_DK), lambda c: (0, rc(c), 0, 0)),
                  pl.BlockSpec((R, GLA_HV), lambda c: (rc(c), 0)),
                  pl.BlockSpec(A.shape, lambda c: (0, 0)),
                  pl.BlockSpec(AT.shape, lambda c: (0, 0)),
                  pl.BlockSpec(masks.shape, lambda c: (0, 0, 0))],
        out_specs=[pl.BlockSpec((R, GLA_HK), lambda c: (rc(c), 0)),
                   pl.BlockSpec((R, GLA_HK), lambda c: (rc(c), 0)),
                   pl.BlockSpec((R, GLA_HV), lambda c: (rc(c), 0)),
                   pl.BlockSpec((R, GLA_HK), lambda c: (rc(c), 0))],
        out_shape=[_out((S, GLA_HK), BF16), _out((S, GLA_HK), BF16), _out((S, GLA_HV), BF16),
                   _out((S, GLA_HK), F32)],
        scratch_shapes=[pltpu.VMEM((GLA_HEADS, GLA_DV, GLA_DK), F32)],
        compiler_params=_cparams(("arbitrary",)),
    )(pin, pin, pin, la, states, do, jnp.asarray(A, BF16), jnp.asarray(AT, BF16), jnp.asarray(masks))


def _gla_post_fwd(o, pin, g):
    def fn(r, p):
        ov, rv = r
        ys = []
        for h in range(GLA_HEADS):
            oh = ov[:, h * GLA_DV:(h + 1) * GLA_DV]
            rh = rv[:, h * GLA_DV:(h + 1) * GLA_DV]
            rs = lax.rsqrt(jnp.mean(oh * oh, axis=-1, keepdims=True) + RMS_EPS)
            ys.append(oh * rs * p[0] * (rh * jax.nn.sigmoid(rh)))
        return [jnp.concatenate(ys, axis=1)], []
    return _rowwise(fn, name="gla_post_fwd", rows=[(o, GLA_HV, 0), (pin, GLA_HV, (2 * GLA_HK + GLA_HV) // GLA_HV)],
                    pars=[g], outs=[(GLA_HV, BF16)])[0]


def _gla_post_bwd(dy, o, pin, g):
    def fn(r, p):
        dyv, ov, rv = r
        dos, drs, dg = [], [], 0.0
        for h in range(GLA_HEADS):
            sl = slice(h * GLA_DV, (h + 1) * GLA_DV)
            oh, rh, dyh = ov[:, sl], rv[:, sl], dyv[:, sl]
            rs = lax.rsqrt(jnp.mean(oh * oh, axis=-1, keepdims=True) + RMS_EPS)
            xh = oh * rs
            sg = jax.nn.sigmoid(rh)
            d_on = dyh * (rh * sg)
            drs.append(dyh * (xh * p[0]) * (sg * (1.0 + rh * (1.0 - sg))))
            dg = dg + _colsum(d_on * xh)
            dxh = d_on * p[0]
            dos.append(rs * (dxh - xh * jnp.mean(dxh * xh, axis=-1, keepdims=True)))
        return [jnp.concatenate(dos, axis=1), jnp.concatenate(drs, axis=1)], [dg]
    return _rowwise(fn, name="gla_post_bwd",
                    rows=[(dy, GLA_HV, 0), (o, GLA_HV, 0), (pin, GLA_HV, (2 * GLA_HK + GLA_HV) // GLA_HV)],
                    pars=[g], outs=[(GLA_HV, F32), (GLA_HV, BF16)], accs=[GLA_DV])


def _gla_gate_bwd(dla, la):
    def fn(r, p):
        dz = r[0] * (1.0 / GLA_TAU) * (1.0 - jnp.exp(GLA_TAU * r[1]))
        return [dz], [_colsum(dz)]
    return _rowwise(fn, name="gla_gate_bwd", rows=[(dla, GLA_HK, 0), (la, GLA_HK, 0)], outs=[(GLA_HK, BF16)],
                    accs=[GLA_HK])


def _log_sigmoid(z):
    return jnp.minimum(z, 0.0) - jnp.log(1.0 + jnp.exp(-jnp.abs(z)))


def _rot_half(v):
    lane = lax.broadcasted_iota(jnp.int32, v.shape, 1)
    return jnp.where(lane < 32, -pltpu.roll(v, 96, 1), jnp.where(lane < 64, pltpu.roll(v, 32, 1), 0.0))


def _rms(v):
    rs = lax.rsqrt(jnp.mean(v * v, axis=-1, keepdims=True) + RMS_EPS)
    return v * rs, rs


def _mla_norm_fwd(cin, gq, gkv, cosp, sinp):
    def fn(r, p):
        cv, cs, sn = r
        qn, _ = _rms(cv[:, :MLA_QR])
        kvn, _ = _rms(cv[:, MLA_QR:MLA_QR + MLA_KVR])
        kr = cv[:, MLA_QR + MLA_KVR:]
        return [qn * p[0], kvn * p[1], kr * cs + _rot_half(kr) * sn], []
    return _rowwise(fn, name="mla_norm_fwd", rows=[(cin, MLA_IN_PAD, 0), (cosp, 128, 0), (sinp, 128, 0)],
                    pars=[gq, gkv], outs=[(MLA_QR, BF16), (MLA_KVR, BF16), (128, BF16)])


def _mla_qrope_fwd(q, cosp, sinp):
    scale = (MLA_NOPE + MLA_ROPE) ** -0.5

    def fn(r, p):
        qv, cs, sn = r
        parts = []
        for h in range(MLA_HEADS):
            parts.append(qv[:, h * MLA_QH:h * MLA_QH + 128] * scale)
            rp = qv[:, h * MLA_QH + 128:(h + 1) * MLA_QH]
            parts.append((rp * cs + _rot_half(rp) * sn) * scale)
        return [jnp.concatenate(parts, axis=1)], []
    W = MLA_HEADS * MLA_QH
    return _rowwise(fn, name="mla_qrope_fwd", rows=[(q, W, 0), (cosp, 128, 0), (sinp, 128, 0)],
                    outs=[(W, BF16)])[0]


def _mla_qrope_bwd(dq, cosp, sinp):
    scale = (MLA_NOPE + MLA_ROPE) ** -0.5

    def fn(r, p):
        dv, cs, sn = r
        parts = []
        for h in range(MLA_HEADS):
            parts.append(dv[:, h * MLA_QH:h * MLA_QH + 128] * scale)
            rp = dv[:, h * MLA_QH + 128:(h + 1) * MLA_QH]
            parts.append((rp * cs - _rot_half(rp) * sn) * scale)
        return [jnp.concatenate(parts, axis=1)], []
    W = MLA_HEADS * MLA_QH
    return _rowwise(fn, name="mla_qrope_bwd", rows=[(dq, W, 0), (cosp, 128, 0), (sinp, 128, 0)],
                    outs=[(W, BF16)])[0]


def _mla_norm_bwd(cin, dqn, dkvn, dkr, gq, gkv, cosp, sinp):
    def fn(r, p):
        cv, dq_, dkv_, dkr_, cs, sn = r
        outs, accs = [], []
        for (lo, hi), dn, g in (((0, MLA_QR), dq_, p[0]), ((MLA_QR, MLA_QR + MLA_KVR), dkv_, p[1])):
            xh, rs = _rms(cv[:, lo:hi])
            dxh = dn * g
            outs.append(rs * (dxh - xh * jnp.mean(dxh * xh, axis=-1, keepdims=True)))
            accs.append(_colsum(dn * xh))
        dk = dkr_[:, 0:128]
        for h in range(1, MLA_HEADS):
            dk = dk + dkr_[:, h * 128:(h + 1) * 128]
        outs.append(dk * cs - _rot_half(dk) * sn)
        return [jnp.concatenate(outs, axis=1)], accs
    return _rowwise(fn, name="mla_norm_bwd",
                    rows=[(cin, MLA_IN_PAD, 0), (dqn, MLA_QR, 0), (dkvn, MLA_KVR, 0), (dkr, MLA_HEADS * 128, 0),
                          (cosp, 128, 0), (sinp, 128, 0)],
                    pars=[gq, gkv], outs=[(MLA_IN_PAD, BF16)], accs=[MLA_QR, MLA_KVR])


def _mla_probs(q, k, i, tq):
    s = _nt(q, k)
    row = (i * tq + lax.broadcasted_iota(jnp.int32, s.shape, 0)) // CHUNK
    col = lax.broadcasted_iota(jnp.int32, s.shape, 1) // CHUNK
    s = jnp.where(col <= row, s, -jnp.inf)
    e = jnp.exp(s - jnp.max(s, axis=-1, keepdims=True))
    return e / jnp.sum(e, axis=-1, keepdims=True)


def _mla_attn_fwd(qr, knv, kr, tq=256):
    S = qr.shape[0]
    tq = min(tq, S)

    def body(q_ref, kn_ref, v_ref, kr_ref, o_ref, k_cat):
        k_cat[:, :128] = kn_ref[...]
        k_cat[:, 128:] = kr_ref[...]
        for i in range(S // tq):
            rows, keys = pl.ds(i * tq, tq), pl.ds(0, (i + 1) * tq)
            pr = _mla_probs(q_ref[rows, :], k_cat[keys, :], i, tq)
            o_ref[rows, :] = _nn(pr.astype(BF16), v_ref[keys, :])

    return pl.pallas_call(
        body, name="mla_attn_fwd", grid=(MLA_HEADS,),
        in_specs=[pl.BlockSpec((S, MLA_QH), lambda h: (0, h)),
                  pl.BlockSpec((S, 128), lambda h: (0, h)),
                  pl.BlockSpec((S, 128), lambda h: (0, MLA_HEADS + h)),
                  pl.BlockSpec((S, 128), lambda h: (0, 0))],
        out_specs=pl.BlockSpec((S, 128), lambda h: (0, h)),
        out_shape=_out((S, MLA_HEADS * MLA_V), F32),
        scratch_shapes=[pltpu.VMEM((S, MLA_QH), BF16)],
        compiler_params=_cparams(("parallel",)),
    )(qr, knv, knv, kr)


def _mla_attn_bwd(qr, knv, kr, o, do, tq=256):
    S = qr.shape[0]
    tq = min(tq, S)
    W = MLA_HEADS * 128

    def body(q_ref, kn_ref, v_ref, kr_ref, o_ref, do_ref, dq_ref, dkn_ref, dv_ref, dkr_ref, k_cat, dk_acc, dv_acc):
        k_cat[:, :128] = kn_ref[...]
        k_cat[:, 128:] = kr_ref[...]
        dk_acc[...] = jnp.zeros(dk_acc.shape, F32)
        dv_acc[...] = jnp.zeros(dv_acc.shape, F32)
        for i in range(S // tq):
            rows, keys = pl.ds(i * tq, tq), pl.ds(0, (i + 1) * tq)
            q, k, v = q_ref[rows, :], k_cat[keys, :], v_ref[keys, :]
            pr = _mla_probs(q, k, i, tq)
            dov = do_ref[rows, :]
            delta = jnp.sum(dov * o_ref[rows, :], axis=-1, keepdims=True)
            dob = dov.astype(BF16)
            ds = (pr * (_nt(dob, v) - delta)).astype(BF16)
            dq_ref[rows, :] = _nn(ds, k)
            dk_acc[keys, :] += _tn(ds, q)
            dv_acc[keys, :] += _tn(pr.astype(BF16), dob)
        dkn_ref[...] = dk_acc[:, :128].astype(dkn_ref.dtype)
        dkr_ref[...] = dk_acc[:, 128:]
        dv_ref[...] = dv_acc[...].astype(dv_ref.dtype)

    head = lambda w: pl.BlockSpec((S, w), lambda h: (0, h))
    return pl.pallas_call(
        body, name="mla_attn_bwd", grid=(MLA_HEADS,),
        in_specs=[head(MLA_QH), head(128), pl.BlockSpec((S, 128), lambda h: (0, MLA_HEADS + h)),
                  pl.BlockSpec((S, 128), lambda h: (0, 0)), head(128), head(128)],
        out_specs=[head(MLA_QH), head(128), head(128), head(128)],
        out_shape=[_out((S, MLA_HEADS * MLA_QH), F32), _out((S, W), BF16), _out((S, W), BF16), _out((S, W), F32)],
        scratch_shapes=[pltpu.VMEM((S, MLA_QH), BF16), pltpu.VMEM((S, MLA_QH), F32), pltpu.VMEM((S, 128), F32)],
        compiler_params=_cparams(("parallel",)),
    )(qr, knv, knv, kr, o, do)


CONV_TILE = 256


def _shift_down(v, n):
    row = lax.broadcasted_iota(jnp.int32, v.shape, 0)
    return jnp.where(row >= n, pltpu.roll(v, n, 0), 0.0)


def _shift_up(v, n):
    S = v.shape[0]
    row = lax.broadcasted_iota(jnp.int32, v.shape, 0)
    return jnp.where(row < S - n, pltpu.roll(v, S - n, 0), 0.0)


def _conv_specs(S, n_extra_cols):
    nt = D_MODEL // CONV_TILE
    specs = [pl.BlockSpec((S, CONV_TILE), functools.partial(lambda j, o: (0, o + j), o=part * nt))
             for part in range(3)]
    specs.append(pl.BlockSpec((8, CONV_TILE), lambda j: (0, j)))
    specs += [pl.BlockSpec((S, CONV_TILE), lambda j: (0, j)) for _ in range(n_extra_cols)]
    return specs


def _conv_fwd(bcu, w8):
    S = bcu.shape[0]

    def body(b_ref, c_ref, u_ref, w_ref, y_ref):
        cu = c_ref[...] * u_ref[...]
        z = w_ref[2:3, :] * cu + w_ref[1:2, :] * _shift_down(cu, 1) + w_ref[0:1, :] * _shift_down(cu, 2)
        y_ref[...] = (b_ref[...] * z).astype(y_ref.dtype)

    return pl.pallas_call(
        body, name="conv_fwd", grid=(D_MODEL // CONV_TILE,), in_specs=_conv_specs(S, 0),
        out_specs=pl.BlockSpec((S, CONV_TILE), lambda j: (0, j)),
        out_shape=_out((S, D_MODEL), BF16),
        compiler_params=_cparams(("parallel",)),
    )(bcu, bcu, bcu, w8)


def _conv_bwd(bcu, w8, dy):
    S = bcu.shape[0]

    def body(b_ref, c_ref, u_ref, w_ref, dy_ref, db_ref, dc_ref, du_ref, dw_ref):
        b, c, u, dyv = b_ref[...], c_ref[...], u_ref[...], dy_ref[...]
        w0, w1, w2 = w_ref[0:1, :], w_ref[1:2, :], w_ref[2:3, :]
        cu = c * u
        cu1, cu2 = _shift_down(cu, 1), _shift_down(cu, 2)
        z = w2 * cu + w1 * cu1 + w0 * cu2
        dz = dyv * b
        db_ref[...] = (dyv * z).astype(db_ref.dtype)
        dcu = w2 * dz + w1 * _shift_up(dz, 1) + w0 * _shift_up(dz, 2)
        dc_ref[...] = (dcu * u).astype(dc_ref.dtype)
        du_ref[...] = (dcu * c).astype(du_ref.dtype)
        dw_ref[...] = jnp.zeros(dw_ref.shape, F32)
        dw_ref[0:1, :] = _colsum(dz * cu2)
        dw_ref[1:2, :] = _colsum(dz * cu1)
        dw_ref[2:3, :] = _colsum(dz * cu)

    col = pl.BlockSpec((S, CONV_TILE), lambda j: (0, j))
    return pl.pallas_call(
        body, name="conv_bwd", grid=(D_MODEL // CONV_TILE,), in_specs=_conv_specs(S, 1),
        out_specs=[col, col, col, pl.BlockSpec((8, CONV_TILE), lambda j: (0, j))],
        out_shape=[_out((S, D_MODEL), BF16)] * 3 + [_out((8, D_MODEL), F32)],
        compiler_params=_cparams(("parallel",)),
    )(bcu, bcu, bcu, w8, dy)


def _adamw_update(w, g, m, v):
    nm = ADAM_B1 * m + (1.0 - ADAM_B1) * g
    nv = ADAM_B2 * v + (1.0 - ADAM_B2) * jnp.square(g)
    m_hat = nm / (1.0 - ADAM_B1 ** ADAM_STEP)
    v_hat = nv / (1.0 - ADAM_B2 ** ADAM_STEP)
    return -ADAM_LR * (m_hat / (jnp.sqrt(v_hat) + ADAM_EPS) + ADAM_WD * w), nm, nv


def _adamw_shard_major(w, m, v, gs, name):
    view = lambda a: jnp.transpose(a, (2, 0, 1))
    g = jnp.stack([x.T for x in gs], axis=1)
    n, L, k = g.shape
    rows = n // 4
    assert n % 4 == 0

    def body(w_ref, m_ref, v_ref, g_ref, go_ref, d_ref, nm_ref, nv_ref):
        gv = g_ref[...]
        d_ref[...], nm_ref[...], nv_ref[...] = _adamw_update(w_ref[...], gv, m_ref[...], v_ref[...])
        go_ref[...] = gv

    spec = pl.BlockSpec((rows, L, k), lambda i: (i, 0, 0))
    outs = pl.pallas_call(
        body, name=name, grid=(4,), in_specs=[spec] * 4, out_specs=[spec] * 4,
        out_shape=[jax.ShapeDtypeStruct((n, L, k), F32)] * 4,
        compiler_params=_cparams(("parallel",)),
    )(view(w), view(m), view(v), g)
    return [jnp.transpose(o, (1, 2, 0)) for o in outs]


def _adamw(w, m, v, gs, name):
    L, R, Cn = w.shape
    assert len(gs) == L
    tr = R if R <= 256 else 256
    assert R % tr == 0

    def body(w_ref, m_ref, v_ref, *rest):
        g_refs, (go_ref, d_ref, nm_ref, nv_ref) = rest[:L], rest[L:]
        layer = pl.program_id(0)
        gv = g_refs[0][...]
        for k in range(1, L):
            gv = jnp.where(layer == k, g_refs[k][...], gv)
        d_ref[...], nm_ref[...], nv_ref[...] = _adamw_update(w_ref[...], gv, m_ref[...], v_ref[...])
        go_ref[...] = gv

    spec = pl.BlockSpec((None, tr, Cn), lambda l, i: (l, i, 0))
    g_specs = [pl.BlockSpec((tr, Cn), functools.partial(lambda l, i, k: (jnp.where(l == k, i, 0), 0), k=k))
               for k in range(L)]
    return pl.pallas_call(
        body, name=name, grid=(L, R // tr), in_specs=[spec] * 3 + g_specs, out_specs=[spec] * 4,
        out_shape=[jax.ShapeDtypeStruct((L, R, Cn), F32)] * 4,
        compiler_params=_cparams(("arbitrary", "arbitrary")),
    )(w, m, v, *gs)


HBM_SPEC = pl.BlockSpec(memory_space=pltpu.HBM)
BOUNCE_ROWS = 256


def _place():
    return lax.axis_index("x"), lax.axis_index("y"), lax.axis_index("c")


def _other_chips(x, y):
    return [(1 - x, y), (x, 1 - y), (1 - x, 1 - y)]


def _copy_via_vmem(src, dst, buf, sems, rows):
    ch = buf.shape[1]
    n = rows // ch
    cin = lambda i: pltpu.make_async_copy(src.at[pl.ds(i * ch, ch), :], buf.at[i % 2], sems.at[i % 2])
    cout = lambda i: pltpu.make_async_copy(buf.at[i % 2], dst.at[pl.ds(i * ch, ch), :], sems.at[2 + i % 2])
    cin(0).start()
    for i in range(n):
        cin(i).wait()
        cout(i).start()
        if i + 1 < n:
            if i >= 1:
                cout(i - 1).wait()
            cin(i + 1).start()
    if n >= 2:
        cout(n - 2).wait()
    cout(n - 1).wait()


SEM_SPEC = pl.BlockSpec(memory_space=pltpu.SEMAPHORE)
ANY_SPEC = pl.BlockSpec(memory_space=pl.ANY)
VMEM_SPEC = pl.BlockSpec(memory_space=pltpu.VMEM)
EFFECT = pltpu.SideEffectType.DATAFLOW_SIDE_EFFECTING
TOKEN = (8, 128)


def _ici_start(srcs, lands, after, copies, name, per_src=3):
    n, nl = len(srcs), len(lands)

    def body(*refs):
        src_refs, land_refs = refs[:n], refs[n:n + nl]
        send_sems, recv_sems, token = refs[n + nl + 1], refs[n + nl + 2], refs[-1]
        x, y, c = _place()
        for k, src, dst, to in copies(src_refs, land_refs, x, y, c):
            pltpu.make_async_remote_copy(src_ref=src, dst_ref=dst, send_sem=send_sems.at[k], recv_sem=recv_sems.at[k],
                                         device_id=to, device_id_type=MESH).start()
        token[...] = jnp.zeros(TOKEN, F32)

    n_copies = per_src * n
    res = pl.pallas_call(
        body, name=name,
        out_shape=(pltpu.SemaphoreType.DMA((n_copies,)), pltpu.SemaphoreType.DMA((n_copies,)),
                   *[pltpu.HBM(s.shape, s.dtype) for s in srcs], *[pltpu.HBM(l.shape, l.dtype) for l in lands],
                   jax.ShapeDtypeStruct(TOKEN, F32)),
        in_specs=[HBM_SPEC] * (n + nl) + [ANY_SPEC],
        out_specs=(SEM_SPEC, SEM_SPEC, *[HBM_SPEC] * (n + nl), VMEM_SPEC),
        input_output_aliases={t: 2 + t for t in range(n + nl)},
        compiler_params=pltpu.CompilerParams(has_side_effects=EFFECT),
    )(*[_hbm(s) for s in srcs], *[_hbm(l) for l in lands], after)
    return res[0], res[1], list(res[2:2 + n]), list(res[2 + n:2 + n + nl]), res[-1]


def _ici_wait(handle, after, copies, name):
    send_sems, recv_sems, srcs, lands, _ = handle
    n, nl = len(srcs), len(lands)

    def body(*refs):
        src_refs, land_refs = refs[:n], refs[n:n + nl]
        send_s, recv_s = refs[n + nl], refs[n + nl + 1]
        x, y, c = _place()
        for k, src, dst, to in copies(src_refs, land_refs, x, y, c):
            cp = pltpu.make_async_remote_copy(src_ref=src, dst_ref=dst, send_sem=send_s.at[k], recv_sem=recv_s.at[k],
                                              device_id=to, device_id_type=MESH)
            cp.wait_send()
            cp.wait_recv()

    res = pl.pallas_call(
        body, name=name,
        out_shape=(*[pltpu.HBM(s.shape, s.dtype) for s in srcs], *[pltpu.HBM(l.shape, l.dtype) for l in lands]),
        in_specs=[HBM_SPEC] * (n + nl) + [SEM_SPEC, SEM_SPEC, ANY_SPEC],
        out_specs=tuple([HBM_SPEC] * (n + nl)),
        input_output_aliases={t: t for t in range(n + nl)},
        compiler_params=pltpu.CompilerParams(has_side_effects=EFFECT),
    )(*srcs, *lands, send_sems, recv_sems, after)
    return list(res[:n]), list(res[n:])


def _gather_copies(halves):
    def copies(src_refs, land_refs, x, y, c):
        q = 2 * x + y
        out = []
        for t, H in enumerate(halves):
            for j, (cx, cy) in enumerate(_other_chips(x, y)):
                out.append((3 * t + j, src_refs[t].at[pl.ds(c * H, H), :], land_refs[t].at[q, pl.ds(c * H, H), :],
                            (cx, cy, c)))
        return out
    return copies


def _gather_wait_copies(halves):
    def copies(src_refs, land_refs, x, y, c):
        out = []
        for t, H in enumerate(halves):
            for j, (cx, cy) in enumerate(_other_chips(x, y)):
                out.append((3 * t + j, src_refs[t].at[pl.ds(c * H, H), :],
                            land_refs[t].at[2 * cx + cy, pl.ds(c * H, H), :], (cx, cy, c)))
        return out
    return copies


def _gather_start(ops, after, name):
    lands = [lax.empty((N_CHIPS,) + o.shape, o.dtype) for o in ops]
    return _ici_start(ops, lands, after, _gather_copies([o.shape[0] // 2 for o in ops]), name)


def _gather_wait(handle, after, name):
    halves = [s.shape[0] // 2 for s in handle[2]]
    return _ici_wait(handle, after, _gather_wait_copies(halves), name)


def _gather_finish(ops, lands, name):
    n = len(ops)
    halves = [o.shape[0] // 2 for o in ops]
    chunk = [min(o.shape[0], BOUNCE_ROWS) for o in ops]

    def body(*refs):
        in_refs, out_refs = refs[:n], refs[2 * n:3 * n]
        send_sems, recv_sems, local_sems = refs[3 * n:3 * n + 3]
        bufs = refs[3 * n + 3:]
        x, y, c = _place()
        q = 2 * x + y
        chips = _other_chips(x, y)
        sibling = (x, y, 1 - c)

        def copy(t, j, half):
            land = out_refs[t].at[2 * chips[j][0] + chips[j][1], pl.ds(half * halves[t], halves[t]), :]
            return pltpu.make_async_remote_copy(src_ref=land, dst_ref=land, send_sem=send_sems.at[3 * t + j],
                                                recv_sem=recv_sems.at[3 * t + j], device_id=sibling,
                                                device_id_type=MESH)

        passed = [copy(t, j, c) for t in range(n) for j in range(3)]
        for cp in passed:
            cp.start()
        for t in range(n):
            _copy_via_vmem(in_refs[t], out_refs[t].at[q], bufs[t], local_sems, ops[t].shape[0])
        for t in range(n):
            for j in range(3):
                copy(t, j, 1 - c).wait_recv()
        for cp in passed:
            cp.wait_send()

    return pl.pallas_call(
        body, name=name, in_specs=[HBM_SPEC] * (2 * n), out_specs=[HBM_SPEC] * n,
        out_shape=[jax.ShapeDtypeStruct(l.shape, l.dtype) for l in lands],
        input_output_aliases={n + t: t for t in range(n)},
        scratch_shapes=[pltpu.SemaphoreType.DMA((3 * n,)), pltpu.SemaphoreType.DMA((3 * n,)),
                        pltpu.SemaphoreType.DMA((4,))]
        + [pltpu.VMEM((2, chunk[t], ops[t].shape[1]), ops[t].dtype) for t in range(n)],
        compiler_params=pltpu.CompilerParams(vmem_limit_bytes=VMEM_LIMIT),
    )(*ops, *lands)


def _swap_halves(ops, name):
    n = len(ops)

    def body(*refs):
        in_refs, out_refs, send_sems, recv_sems = refs[:n], refs[n:2 * n], refs[2 * n], refs[2 * n + 1]
        x, y, c = _place()
        cps = []
        for t in range(n):
            H = ops[t].shape[1] // 2
            cp = pltpu.make_async_remote_copy(src_ref=in_refs[t].at[:, pl.ds((1 - c) * H, H), :],
                                              dst_ref=out_refs[t], send_sem=send_sems.at[t],
                                              recv_sem=recv_sems.at[t], device_id=(x, y, 1 - c),
                                              device_id_type=MESH)
            cp.start()
            cps.append(cp)
        for cp in cps:
            cp.wait()

    return pl.pallas_call(
        body, name=name, in_specs=[HBM_SPEC] * n, out_specs=[HBM_SPEC] * n,
        out_shape=[jax.ShapeDtypeStruct((N_CHIPS, o.shape[1] // 2, o.shape[2]), o.dtype) for o in ops],
        scratch_shapes=[pltpu.SemaphoreType.DMA((n,)), pltpu.SemaphoreType.DMA((n,))],
    )(*ops)


def _sum_rows_tile(h):
    return h if h <= 512 else 512


def _pair_sum(g, t, cq, name):
    _, a, b = g.shape
    H = a // 2
    tr = _sum_rows_tile(H)

    def body(cq_ref, g_ref, t_ref, o_ref):
        o_ref[...] = (g_ref[...].astype(F32) + t_ref[...].astype(F32)).astype(o_ref.dtype)

    grid_spec = pltpu.PrefetchScalarGridSpec(
        num_scalar_prefetch=1, grid=(N_CHIPS, H // tr),
        in_specs=[pl.BlockSpec((None, None, tr, b), lambda j, i, cq_ref: (j, cq_ref[0], i, 0)),
                  pl.BlockSpec((None, tr, b), lambda j, i, cq_ref: (j, i, 0))],
        out_specs=pl.BlockSpec((None, tr, b), lambda j, i, cq_ref: (j, i, 0)))
    return pl.pallas_call(
        body, name=name, grid_spec=grid_spec, out_shape=_out(t.shape, BF16),
        compiler_params=_cparams(("parallel", "parallel")),
    )(cq, g.reshape(N_CHIPS, 2, H, b), t)


def _scatter_copies(src_refs, land_refs, x, y, c):
    out = []
    for j, (cx, cy) in enumerate(_other_chips(x, y)):
        for t in range(len(src_refs)):
            out.append((3 * t + j, src_refs[t].at[2 * cx + cy], land_refs[t].at[j], (cx, cy, c)))
    return out


def _scatter_start(ops, after, name):
    lands = [lax.empty((3,) + o.shape[1:], o.dtype) for o in ops]
    return _ici_start(ops, lands, after, _scatter_copies, name)


def _scatter_wait(handle, after, name):
    return _ici_wait(handle, after, _scatter_copies, name)


def _chip_sum(p, t, cq, name):
    _, H, b = p.shape
    tr = _sum_rows_tile(H)

    def body(cq_ref, p_ref, t_ref, o_ref):
        acc = p_ref[...].astype(F32)
        for j in range(3):
            acc = acc + t_ref[j].astype(F32)
        o_ref[...] = acc

    grid_spec = pltpu.PrefetchScalarGridSpec(
        num_scalar_prefetch=1, grid=(H // tr,),
        in_specs=[pl.BlockSpec((None, tr, b), lambda i, cq_ref: (cq_ref[1], i, 0)),
                  pl.BlockSpec((3, tr, b), lambda i, cq_ref: (0, i, 0))],
        out_specs=pl.BlockSpec((None, tr, b), lambda i, cq_ref: (cq_ref[0], i, 0)))
    out = pl.pallas_call(
        body, name=name, grid_spec=grid_spec, out_shape=_out((2, H, b), F32),
        compiler_params=_cparams(("parallel",)),
    )(cq, p, t)
    return out.reshape(2 * H, b)


def _join_halves(ops, name):
    n = len(ops)

    def body(*refs):
        out_refs, send_sems, recv_sems = refs[n:2 * n], refs[2 * n], refs[2 * n + 1]
        x, y, c = _place()
        cps = []
        for t in range(n):
            H = ops[t].shape[0] // 2
            mine = out_refs[t].at[pl.ds(c * H, H), :]
            cp = pltpu.make_async_remote_copy(src_ref=mine, dst_ref=mine, send_sem=send_sems.at[t],
                                              recv_sem=recv_sems.at[t], device_id=(x, y, 1 - c),
                                              device_id_type=MESH)
            cp.start()
            cps.append(cp)
        for t in range(n):
            H = ops[t].shape[0] // 2
            other = out_refs[t].at[pl.ds((1 - c) * H, H), :]
            pltpu.make_async_remote_copy(src_ref=other, dst_ref=other, send_sem=send_sems.at[t],
                                         recv_sem=recv_sems.at[t], device_id=(x, y, 1 - c),
                                         device_id_type=MESH).wait_recv()
        for cp in cps:
            cp.wait_send()

    return pl.pallas_call(
        body, name=name, in_specs=[HBM_SPEC] * n, out_specs=[HBM_SPEC] * n,
        out_shape=[jax.ShapeDtypeStruct(o.shape, o.dtype) for o in ops],
        input_output_aliases={t: t for t in range(n)},
        scratch_shapes=[pltpu.SemaphoreType.DMA((n,)), pltpu.SemaphoreType.DMA((n,))],
    )(*ops)


def _direct_copies(src_refs, land_refs, x, y, c):
    out = []
    for t in range(len(src_refs)):
        H = src_refs[t].shape[1] // 2
        for k in range(1, 8):
            px, py, pc = x ^ (k >> 2), y ^ ((k >> 1) & 1), c ^ (k & 1)
            out.append((7 * t + k - 1, src_refs[t].at[2 * px + py, pl.ds(pc * H, H), :], land_refs[t].at[k - 1],
                        (px, py, pc)))
    return out


def _direct_sum(g, t, cq, name):
    _, a, b = g.shape
    H = a // 2
    tr = _sum_rows_tile(H)

    def body(cq_ref, g_ref, t_ref, o_ref):
        acc = g_ref[...].astype(F32)
        for k in range(7):
            acc = acc + t_ref[k].astype(F32)
        o_ref[...] = acc

    grid_spec = pltpu.PrefetchScalarGridSpec(
        num_scalar_prefetch=1, grid=(H // tr,),
        in_specs=[pl.BlockSpec((None, None, tr, b), lambda i, cq_ref: (cq_ref[1], cq_ref[0], i, 0)),
                  pl.BlockSpec((7, tr, b), lambda i, cq_ref: (0, i, 0))],
        out_specs=pl.BlockSpec((None, tr, b), lambda i, cq_ref: (cq_ref[0], i, 0)))
    out = pl.pallas_call(
        body, name=name, grid_spec=grid_spec, out_shape=_out((2, H, b), F32),
        compiler_params=_cparams(("parallel",)),
    )(cq, g.reshape(N_CHIPS, 2, H, b), t)
    return out.reshape(a, b)


def _reduce_direct_start(gs, tag):
    lands = [lax.empty((7, g.shape[1] // 2, g.shape[2]), g.dtype) for g in gs]
    return _ici_start(gs, lands, jnp.zeros(TOKEN, F32), _direct_copies, "rs_direct_start_" + tag, per_src=7)


def _reduce_direct_finish(handle, cq, after, tag):
    gs, rs = _ici_wait(handle, after, _direct_copies, "rs_direct_wait_" + tag)
    fs = [_direct_sum(g, r, cq, "rs_direct_sum") for g, r in zip(gs, rs)]
    return _join_halves(fs, "rs_join_" + tag)


def _reduce_scatter_start(gs, cq, after, tag):
    ts = _swap_halves(gs, "rs_swap_" + tag)
    ps = [_pair_sum(g, t, cq, "rs_pair_sum") for g, t in zip(gs, ts)]
    return _scatter_start(ps, after, "rs_scatter_start_" + tag)


def _reduce_scatter_finish(handle, cq, after, tag):
    ps, rs = _scatter_wait(handle, after, "rs_scatter_wait_" + tag)
    fs = [_chip_sum(p, r, cq, "rs_chip_sum") for p, r in zip(ps, rs)]
    return _join_halves(fs, "rs_join_" + tag)


def _all_reduce_small(v):
    n = v.shape[0]

    def body(v_ref, out_ref, buf, send_sems, recv_sems):
        x, y, c = _place()
        me = 4 * x + 2 * y + c
        buf[me] = v_ref[...]
        cps = []
        for k in range(1, 8):
            peer = (x ^ (k >> 2), y ^ ((k >> 1) & 1), c ^ (k & 1))
            cp = pltpu.make_async_remote_copy(src_ref=v_ref, dst_ref=buf.at[me], send_sem=send_sems.at[k - 1],
                                              recv_sem=recv_sems.at[k - 1], device_id=peer, device_id_type=MESH)
            cp.start()
            cps.append(cp)
        for k in range(1, 8):
            px, py, pc = x ^ (k >> 2), y ^ ((k >> 1) & 1), c ^ (k & 1)
            land = buf.at[4 * px + 2 * py + pc]
            pltpu.make_async_remote_copy(src_ref=land, dst_ref=land, send_sem=send_sems.at[k - 1],
                                         recv_sem=recv_sems.at[k - 1], device_id=(px, py, pc),
                                         device_id_type=MESH).wait_recv()
        for cp in cps:
            cp.wait_send()
        acc = buf[0]
        for d in range(1, 8):
            acc = acc + buf[d]
        out_ref[...] = acc

    vm = pl.BlockSpec(memory_space=pltpu.VMEM)
    return pl.pallas_call(
        body, name="all_reduce_small", in_specs=[vm], out_specs=vm,
        out_shape=jax.ShapeDtypeStruct((n, 128), F32),
        scratch_shapes=[pltpu.VMEM((8, n, 128), F32), pltpu.SemaphoreType.DMA((7,)), pltpu.SemaphoreType.DMA((7,))],
    )(v)


SMALL_GATHER = (16, 1024)
SMALL_FULL = sum(_size(_full_shape(n)) for n in SMALL)
SMALL_FULL_ROWS = -(-SMALL_FULL // 128 // 8) * 8


def _layer_shards(w, i, q):
    kind, j = MIXER[i % 3], i // 3
    out = {n: w[n][i].astype(BF16) for n in COMMON_BIG}
    if kind == 'gla':
        win = jnp.zeros((D_MODEL, GLA_WIN), F32)
        win = lax.dynamic_update_slice(win, w['gla_w_in'][j], (0, (GLA_SHARD - GLA_WIN_STEP) * q))
        out['gla_w_in'] = win.astype(BF16)
        out['gla_w_out'] = w['gla_w_out'][j].astype(BF16)
    elif kind == 'mla':
        out['mla_w_in'] = jnp.pad(w['mla_w_in'][j], ((0, 0), (0, MLA_IN_PAD - MLA_IN))).astype(BF16)
        for n in ('mla_w_uq', 'mla_w_ukv', 'mla_w_out'):
            out[n] = w[n][j].astype(BF16)
    else:
        out['conv_w_in'] = w['conv_w_in'][j].astype(BF16)
        out['conv_w_out'] = w['conv_w_out'][j].astype(BF16)
    return out


def _rows_joined(g):
    return g.reshape(g.shape[0] * g.shape[1], g.shape[2])


def _cols_joined(g):
    return jnp.moveaxis(g, 0, 1).reshape(g.shape[1], -1)


def _layer_weights(g, i):
    kind = MIXER[i % 3]
    W = {}
    if 'mlp_w1' in g:
        W = {'w1': g['mlp_w1'], 'w2': _rows_joined(g['mlp_w2']), 'gate': _rows_joined(g['ple_w_gate']),
             'proj': g['ple_w_proj']}
    if kind == 'gla' and 'gla_w_out' in g:
        W['w_out'] = _rows_joined(g['gla_w_out'])
    if kind == 'gla' and 'gla_w_in' in g:
        parts = []
        for qq in range(N_CHIPS):
            lo = g['gla_w_in'][qq][:, :128]
            if qq > 0:
                lo = lo + g['gla_w_in'][qq - 1][:, GLA_WIN_STEP:]
            parts += [lo, g['gla_w_in'][qq][:, 128:GLA_WIN_STEP]]
        parts.append(g['gla_w_in'][N_CHIPS - 1][:, GLA_WIN_STEP:])
        W['w_in'] = jnp.concatenate(parts, axis=1)
    elif kind == 'mla':
        W['w_in'] = _rows_joined(g['mla_w_in'])
        uq = _cols_joined(g['mla_w_uq']).reshape(MLA_QR, MLA_HEADS, MLA_NOPE + MLA_ROPE)
        W['w_uq'] = jnp.pad(uq, ((0, 0), (0, 0), (0, MLA_QH - MLA_NOPE - MLA_ROPE))).reshape(MLA_QR, -1)
        ukv = _cols_joined(g['mla_w_ukv']).reshape(MLA_KVR, MLA_HEADS, 2, 128)
        W['w_ukv'] = ukv.transpose(0, 2, 1, 3).reshape(MLA_KVR, -1)
        W['w_out'] = _rows_joined(g['mla_w_out'])
    elif kind == 'conv':
        W['w_in'] = g['conv_w_in']
        W['w_out'] = _rows_joined(g['conv_w_out'])
    return W


def _pack_small_shards(w):
    flat = jnp.concatenate([w[n].reshape(-1) for n in SMALL_SHARDED])
    return jnp.pad(flat, (0, _size(SMALL_GATHER) - flat.shape[0])).reshape(SMALL_GATHER)


def _unpack_small_gathered(g):
    flat, out, off = g.reshape(N_CHIPS, -1), {}, 0
    for n in SMALL_SHARDED:
        shape, ax = WSPEC[n]
        seg = flat[:, off:off + _size(shape)].reshape((N_CHIPS,) + shape)
        out[n] = jnp.moveaxis(seg, 0, ax).reshape(_full_shape(n))
        off += _size(shape)
    return out


def _pack_small(vals):
    flat = jnp.concatenate([vals[n].reshape(-1) for n in SMALL])
    return jnp.pad(flat, (0, SMALL_FULL_ROWS * 128 - flat.shape[0])).reshape(SMALL_FULL_ROWS, 128)


def _unpack_small(packed, q):
    flat = packed.reshape(-1)
    out, off = {}, 0
    for n in SMALL:
        shape, ax = WSPEC[n]
        full = flat[off:off + _size(_full_shape(n))].reshape(_full_shape(n))
        off += _size(_full_shape(n))
        out[n] = full if ax is None else lax.dynamic_slice_in_dim(full, q * shape[ax], shape[ax], axis=ax)
    return out


def _row_shards(dw):
    return dw.reshape(N_CHIPS, dw.shape[0] // N_CHIPS, dw.shape[1])


def _col_shards(dw):
    return jnp.moveaxis(dw.reshape(dw.shape[0], N_CHIPS, -1), 1, 0)


def _row(v):
    return v.reshape(1, -1)


def _layer_fwd(i, xin, xin_b, p_i, W, sm, cosp, sinp, rest=None):
    kind, j = MIXER[i % 3], i // 3
    sv = {'xin': xin, 'xin_b': xin_b}
    if kind == 'gla':
        w_up = jnp.pad(sm['gla_w_gate_up'][j].astype(BF16), ((0, 128 - GLA_RANK), (0, 0)))
        pin = _mm(xin_b, W['w_in'], name="gla_in", tn=640, tm=FULL_ROWS)
        la = _mm(pin, w_up, name="gla_gate", K=128, tk=128, a_off=(0, (GLA_IN_PAD - 128) // 128), tn=512,
                 extras=[(_row(sm['gla_b_gate'][j]), 'n')],
                 epilogue=lambda acc, b: (_log_sigmoid(acc + b) * (1.0 / GLA_TAU),))
        o, states = _gla_fwd(pin, la)
        yb = _gla_post_fwd(o, pin, _row(sm['gla_norm_g'][j]))
        if rest is not None:
            W = {**W, **rest(yb)}
        mixed = yb
        sv.update(w_up=w_up, pin=pin, la=la, o=o, states=states, yb=yb)
    elif kind == 'mla':
        gq, gkv = sm['mla_q_norm'][j:j + 1], sm['mla_kv_norm'][j:j + 1]
        cin = _mm(xin_b, W['w_in'], name="mla_in", tn=640, tm=FULL_ROWS)
        qn, kvn, kr = _mla_norm_fwd(cin, gq, gkv, cosp, sinp)
        qr = _mla_qrope_fwd(_mm(qn, W['w_uq'], name="mla_uq"), cosp, sinp)
        knv = _mm(kvn, W['w_ukv'], name="mla_ukv", out_dtypes=(BF16,))
        o = _mla_attn_fwd(qr, knv, kr)
        ob = o.astype(BF16)
        mixed = ob
        sv.update(gq=gq, gkv=gkv, cin=cin, qn=qn, kvn=kvn, kr=kr, qr=qr, knv=knv, o=o, ob=ob)
    else:
        w8 = jnp.pad(sm['conv_w'][j], ((0, 5), (0, 0)))
        bcu = _mm(xin_b, W['w_in'], name="conv_in", tn=768, b_sh=True, tm=FULL_ROWS)
        yb = _conv_fwd(bcu, w8)
        mixed = yb
        sv.update(w8=w8, bcu=bcu, yb=yb)
    g0, b0 = _row(sm['ln_g'][i, 0]), _row(sm['ln_b'][i, 0])
    g1, b1 = _row(sm['ln_g'][i, 1]), _row(sm['ln_b'][i, 1])
    ln = dict(tm=512, tn=D_MODEL, out_dtypes=(F32, BF16, F32), epilogue=_ln_fwd_epilogue)
    x1, x1b, v0 = _mm(mixed, W['w_out'], name="mix_out_ln", extras=[(xin, 'mn'), (g0, 'n'), (b0, 'n')], **ln)
    ab = _mm(x1b, W['w1'], name="mlp_up", out_dtypes=(BF16,), b_sh=True, tm=FULL_ROWS,
             epilogue=lambda acc: (jnp.square(jnp.maximum(acc, 0.0)),))
    x2, x2b, v1 = _mm(ab, W['w2'], name="mlp_down_ln", tk=D_FF, extras=[(x1, 'mn'), (g1, 'n'), (b1, 'n')], **ln)
    pp = _mm(p_i, W['proj'], name="ple_proj", tn=256, b_sh=True)
    z, x3, x3b = _mm(x2b, W['gate'], name="ple_gate", out_dtypes=(F32, F32, BF16),
                     extras=[(x2, 'mn'), (pp, 'mn')],
                     epilogue=lambda acc, xv, pv: (acc,) + (xv + jax.nn.sigmoid(acc) * pv,) * 2)
    sv.update(v0=v0, x1b=x1b, ab=ab, v1=v1, x2b=x2b, pp=pp, z=z, g0=g0, g1=g1)
    return x3, x3b, sv, W


def _layer_bwd(i, grads_in, p_i, W, sm, sv, cosp, sinp, token, early=None, below=None):
    kind, j = MIXER[i % 3], i // 3
    big, small = {}, {}
    dx, dpp_b, dz_b = grads_in
    big['ple_w_proj'] = _mm(p_i, dpp_b, ta=True, name="ple_proj_dw", tn=256, out_sh=True, out_dtypes=(BF16,))
    big['ple_w_gate'] = _row_shards(_mm(sv['x2b'], dz_b, ta=True, name="dw_dd", out_dtypes=(BF16,)))
    ln = dict(tb=True, tm=512, tn=D_MODEL, out_dtypes=(F32, BF16), n_sums=2)
    (dv1, dv1b), (dg1, db1) = _mm(dz_b, W['gate'], name="ple_gate_dx_ln", epilogue=_ln_bwd_epilogue(1.0),
                                  extras=[(dx, 'mn'), (sv['v1'], 'mn'), (sv['g1'], 'n'), (token, 'whole')], **ln)
    big['mlp_w2'] = _row_shards(_mm(sv['ab'], dv1b, ta=True, name="mlp_down_dw", out_dtypes=(BF16,)))
    dub = _mm(dv1b, W['w2'], tb=True, name="mlp_down_dx", out_dtypes=(BF16,), tm=FULL_ROWS,
              extras=[(sv['ab'], 'mn')], epilogue=lambda acc, a: (acc * (2.0 * jnp.sqrt(a.astype(F32))),))
    big['mlp_w1'] = _mm(sv['x1b'], dub, ta=True, name="mlp_up_dw", out_sh=True, out_dtypes=(BF16,))
    order = []
    if early is not None:
        order, big = [(early(big), 'whole')], {}
    (dv0, dv0b), (dg0, db0) = _mm(dub, W['w1'], name="mlp_up_dx_ln", b_sh=True, tk=D_FF, epilogue=_ln_bwd_epilogue(ALPHA),
                                  extras=[(dv1, 'mn'), (sv['v0'], 'mn'), (sv['g0'], 'n')] + order, **ln)
    small['ln_g'] = jnp.stack([dg0[0], dg1[0]])
    small['ln_b'] = jnp.stack([db0[0], db1[0]])
    resid = dict(tb=True, tn=D_MODEL, tm=512 if below else 1024, epilogue=_input_grad_epilogue,
                 extras=[(dv0, 'mn')] + [(a, 'mn') for a in below or ()],
                 out_dtypes=(F32, BF16, BF16) if below else (F32,))
    if kind == 'gla':
        big['gla_w_out'] = _row_shards(_mm(sv['yb'], dv0b, ta=True, name="dw_dd", out_dtypes=(BF16,)))
        dy = _mm(dv0b, W['w_out'], tb=True, name="dx_dd", tn=1024)
        do, dr_b, dng = _gla_post_bwd(dy, sv['o'], sv['pin'], _row(sm['gla_norm_g'][j]))
        dq_b, dk_b, dvv_b, dla = _gla_bwd(sv['pin'], sv['la'], sv['states'], do)
        dzg_b, dbg = _gla_gate_bwd(dla, sv['la'])
        dw_up = _mm(sv['pin'], dzg_b, ta=True, name="gla_gate_dw", M=128, tm=128,
                    a_off=(0, (GLA_IN_PAD - 128) // 128))
        dglr_b = _mm(dzg_b, sv['w_up'], tb=True, name="gla_gate_dx", out_dtypes=(BF16,))
        dpin_b = jnp.concatenate([dq_b, dk_b, dvv_b, dr_b, dglr_b], axis=1)
        dw_in = _mm(sv['xin_b'], dpin_b, ta=True, name="gla_in_dw", tn=640, out_dtypes=(BF16,))
        dxin = _mm(dpin_b, W['w_in'], name="gla_in_dx", tk=GLA_IN_PAD, **resid)
        big['gla_w_in'] = jnp.stack([dw_in[:, GLA_WIN_STEP * qq:GLA_WIN_STEP * qq + GLA_WIN]
                                     for qq in range(N_CHIPS)])
        small.update(gla_w_gate_up=dw_up[:GLA_RANK], gla_b_gate=dbg[0], gla_norm_g=dng[0])
    elif kind == 'mla':
        big['mla_w_out'] = _row_shards(_mm(sv['ob'], dv0b, ta=True, name="dw_dd", out_dtypes=(BF16,)))
        do = _mm(dv0b, W['w_out'], tb=True, name="dx_dd", tn=1024)
        dqr, dkn_b, dvv_b, dkr = _mla_attn_bwd(sv['qr'], sv['knv'], sv['kr'], sv['o'], do)
        dq_b = _mla_qrope_bwd(dqr, cosp, sinp)
        dw_uq = _mm(sv['qn'], dq_b, ta=True, name="mla_up_dw", out_dtypes=(BF16,))
        dqn = _mm(dq_b, W['w_uq'], tb=True, name="mla_up_dx")
        dknv_b = jnp.concatenate([dkn_b, dvv_b], axis=1)
        dw_ukv = _mm(sv['kvn'], dknv_b, ta=True, name="mla_up_dw", out_dtypes=(BF16,))
        dkvn = _mm(dknv_b, W['w_ukv'], tb=True, name="mla_up_dx")
        dcin_b, dgq, dgkv = _mla_norm_bwd(sv['cin'], dqn, dkvn, dkr, sv['gq'], sv['gkv'], cosp, sinp)
        big['mla_w_in'] = _row_shards(_mm(sv['xin_b'], dcin_b, ta=True, name="mla_in_dw", tn=640,
                                          out_dtypes=(BF16,)))
        dxin = _mm(dcin_b, W['w_in'], name="mla_in_dx", tk=MLA_IN_PAD, **resid)
        big['mla_w_uq'] = _col_shards(
            dw_uq.reshape(MLA_QR, MLA_HEADS, MLA_QH)[:, :, :MLA_NOPE + MLA_ROPE].reshape(MLA_QR, -1))
        big['mla_w_ukv'] = _col_shards(
            dw_ukv.reshape(MLA_KVR, 2, MLA_HEADS, 128).transpose(0, 2, 1, 3).reshape(MLA_KVR, -1))
        small.update(mla_q_norm=dgq[0], mla_kv_norm=dgkv[0])
    else:
        big['conv_w_out'] = _row_shards(_mm(sv['yb'], dv0b, ta=True, name="dw_dd", out_dtypes=(BF16,)))
        dy = _mm(dv0b, W['w_out'], tb=True, name="dx_dd", tn=1024)
        db_b, dc_b, du_b, dw8 = _conv_bwd(sv['bcu'], sv['w8'], dy)
        dbcu_b = jnp.concatenate([db_b, dc_b, du_b], axis=1)
        big['conv_w_in'] = _mm(sv['xin_b'], dbcu_b, ta=True, name="conv_in_dw", tn=768, out_sh=True,
                               out_dtypes=(BF16,))
        dxin = _mm(dbcu_b, W['w_in'], name="conv_in_dx", tk=3 * D_MODEL, b_sh=True, **resid)
        small['conv_w'] = dw8[:3]
    return (dxin if below else (dxin,)), big, small


def _rope_tables(positions):
    inv_freq = ROPE_BASE ** (-jnp.arange(0, MLA_ROPE // 2, dtype=F32) * (2.0 / MLA_ROPE))
    ang = positions.astype(F32)[:, None] * inv_freq
    zeros = jnp.zeros((positions.shape[0], 64), F32)
    return (jnp.concatenate([jnp.cos(ang), jnp.cos(ang), zeros], axis=1),
            jnp.concatenate([jnp.sin(ang), jnp.sin(ang), zeros], axis=1))


FIRST_NEEDED = ['gla_w_in']


def _start_gathers(w, q):
    token, started = jnp.zeros(TOKEN, F32), []
    for i in range(DEPTH):
        sh = _layer_shards(w, i, q)
        groups = [list(sh)] if i > 0 else [FIRST_NEEDED, [n for n in sh if n not in FIRST_NEEDED]]
        for k, names in enumerate(groups):
            ops = [sh[n] for n in names]
            if i == 0 and k == 0:
                ops.append(_pack_small_shards(w))
            tag = "l%d%s" % (i, "ab"[k] if i == 0 else "")
            handle = _gather_start(ops, token, "ag_start_" + tag)
            token = handle[4]
            started.append((handle, names, tag))
    return started, token


def _finish_gather(entry, after):
    handle, names, tag = entry
    srcs, lands = _gather_wait(handle, after, "ag_wait_" + tag)
    got = _gather_finish(srcs, lands, "ag_finish_" + tag)
    return dict(zip(names, got)), got[-1]


def _local_shard_grad(name, g, q):
    if name == 'gla_w_in':
        return lax.dynamic_slice_in_dim(g, (GLA_SHARD - GLA_WIN_STEP) * q, GLA_SHARD, axis=1)
    if name == 'mla_w_in':
        return g[:, :MLA_IN]
    return g


def kernel(x, p, positions, gla_w_in, gla_w_gate_up, gla_b_gate, gla_norm_g, gla_w_out, mla_w_in, mla_q_norm, mla_kv_norm, mla_w_uq, mla_w_ukv, mla_w_out, conv_w_in, conv_w, conv_w_out, ln_g, ln_b, mlp_w1, mlp_w2, ple_w_gate, ple_w_proj, loss_target, m_gla_w_in, m_gla_w_gate_up, m_gla_b_gate, m_gla_norm_g, m_gla_w_out, m_mla_w_in, m_mla_q_norm, m_mla_kv_norm, m_mla_w_uq, m_mla_w_ukv, m_mla_w_out, m_conv_w_in, m_conv_w, m_conv_w_out, m_ln_g, m_ln_b, m_mlp_w1, m_mlp_w2, m_ple_w_gate, m_ple_w_proj, v_gla_w_in, v_gla_w_gate_up, v_gla_b_gate, v_gla_norm_g, v_gla_w_out, v_mla_w_in, v_mla_q_norm, v_mla_kv_norm, v_mla_w_uq, v_mla_w_ukv, v_mla_w_out, v_conv_w_in, v_conv_w, v_conv_w_out, v_ln_g, v_ln_b, v_mlp_w1, v_mlp_w2, v_ple_w_gate, v_ple_w_proj):
    args = locals()
    w = {n: args[n] for n in WNAMES}
    m = {n: args['m_' + n] for n in WNAMES}
    v = {n: args['v_' + n] for n in WNAMES}
    q = 2 * lax.axis_index("x") + lax.axis_index("y")
    cq = jnp.stack([lax.axis_index("c"), q]).astype(jnp.int32)

    cosp, sinp = _rope_tables(positions[0])
    started, after = _start_gathers(w, q)
    xin, saved, layers, sm = x[0], [], [], None
    xin_b = xin.astype(BF16)
    for i in range(DEPTH):
        got, last = _finish_gather(started[i + 1 if i else 0], after)
        rest = None
        if i == 0:
            sm = _unpack_small_gathered(last)
            sm['mla_q_norm'], sm['mla_kv_norm'] = w['mla_q_norm'], w['mla_kv_norm']
            rest = lambda after: _layer_weights(_finish_gather(started[1], after)[0], 0)
        xin, xin_b, sv, W = _layer_fwd(i, xin, xin_b, p[i, 0], _layer_weights(got, i), sm, cosp, sinp, rest)
        layers.append(W)
        saved.append(sv)
        after = xin
    *grads_in, loss_cols = _loss_head(xin, loss_target[0], saved[-1]['z'], saved[-1]['pp'])
    loss = lax.psum(jnp.sum(loss_cols[0]), ("x", "y", "c"))

    gbig = {n: [None] * WSPEC[n][0][0] for n in BIG}
    gsmall = {n: [None] * _full_shape(n)[0] for n in SMALL}
    pending = []

    def start(grads, i, tag):
        names = list(grads)
        gs = [grads[n] for n in names]
        handle = _reduce_direct_start(gs, tag) if i > 0 else _reduce_scatter_start(gs, cq, jnp.zeros(TOKEN, F32), tag)
        pending.append((handle, names, i, tag))
        return handle[4]

    def finish(above, after):
        for entry in [e for e in pending if e[2] > above]:
            pending.remove(entry)
            handle, names, i, tag = entry
            reduced = (_reduce_direct_finish if i > 0 else _reduce_scatter_finish)(handle, cq, after, tag)
            for n, g in zip(names, reduced):
                gbig[n][i if n in COMMON_BIG else i // 3] = _local_shard_grad(n, g, q)

    token = jnp.zeros(TOKEN, F32)
    for i in reversed(range(DEPTH)):
        early = (lambda grads: start(grads, 0, "l0a")) if i == 0 else None
        below = (saved[i - 1]['z'], saved[i - 1]['pp']) if i > 0 else None
        grads_in, big, small = _layer_bwd(i, grads_in, p[i, 0], layers[i], sm, saved[i], cosp, sinp, token, early,
                                          below)
        dx = grads_in[0]
        token = start(big, i, "l%d%s" % (i, "b" if i == 0 else ""))
        finish(i, dx)
        for n, g in small.items():
            gsmall[n][i if n in ('ln_g', 'ln_b') else i // 3] = g
    finish(-1, token)
    gsm = _unpack_small(_all_reduce_small(_pack_small({n: jnp.stack(g) for n, g in gsmall.items()})), q)

    grad, delta, new_m, new_v = {}, {}, {}, {}
    for n in BIG:
        update = _adamw_shard_major if n == 'gla_w_in' else _adamw
        grad[n], delta[n], new_m[n], new_v[n] = update(w[n], m[n], v[n], gbig[n], "adamw_" + n)
    total = sum(_size(WSPEC[n][0]) for n in SMALL)
    rows = -(-total // 128 // 8) * 8

    def pack(dct):
        flat = jnp.concatenate([dct[n].reshape(-1) for n in SMALL])
        return jnp.pad(flat, (0, rows * 128 - total), constant_values=1.0).reshape(1, rows, 128)

    res = _adamw(pack(w), pack(m), pack(v), [pack(gsm)[0]], "adamw_small")
    for out, packed in zip((grad, delta, new_m, new_v), res):
        flat, off = packed.reshape(-1), 0
        for n in SMALL:
            sz = _size(WSPEC[n][0])
            out[n] = flat[off:off + sz].reshape(WSPEC[n][0])
            off += sz
    return (loss, dx[None], *[grad[n] for n in WNAMES], *[delta[n] for n in WNAMES],
            *[new_m[n] for n in WNAMES], *[new_v[n] for n in WNAMES])
```

```python
import functools

import numpy as np
import jax
import jax.numpy as jnp
from jax import lax
from jax.experimental import pallas as pl
from jax.experimental.pallas import tpu as pltpu

F32 = jnp.float32
BF16 = jnp.bfloat16
MESH = pl.DeviceIdType.MESH

D_MODEL = 1024
DEPTH = 4
CHUNK = 64
ALPHA = (2 * DEPTH) ** 0.25
LN_EPS = 1e-5
RMS_EPS = 1e-6
PLE_DIM = 256
D_FF = 4 * D_MODEL
GLA_HEADS = 4
GLA_DK = 128
GLA_DV = 256
GLA_RANK = 16
GLA_TAU = 16.0
GLA_HK = GLA_HEADS * GLA_DK
GLA_HV = GLA_HEADS * GLA_DV
GLA_IN = 2 * GLA_HK + GLA_HV + D_MODEL + GLA_RANK
GLA_IN_PAD = 2 * GLA_HK + GLA_HV + D_MODEL + 128
GLA_SHARD = GLA_IN // 4
GLA_WIN = 896
GLA_WIN_STEP = 768
MLA_HEADS = 8
MLA_NOPE = 128
MLA_ROPE = 64
MLA_V = 128
MLA_QR = 256
MLA_KVR = 256
MLA_IN = MLA_QR + MLA_KVR + MLA_ROPE
MLA_IN_PAD = MLA_QR + MLA_KVR + 128
MLA_QH = 256
ROPE_BASE = 10000.0
ADAM_LR = 0.001
ADAM_B1 = 0.9
ADAM_B2 = 0.999
ADAM_EPS = 1e-08
ADAM_WD = 0.01
ADAM_STEP = 10

VMEM_LIMIT = 48 * 1024 * 1024
FULL_ROWS = 2048
N_CHIPS = 4

WSPEC = {
    'gla_w_in': ((2, 1024, 772), 2), 'gla_w_gate_up': ((2, 16, 128), 2), 'gla_b_gate': ((2, 128), 1),
    'gla_norm_g': ((2, 64), 1), 'gla_w_out': ((2, 256, 1024), 1), 'mla_w_in': ((1, 256, 576), 1),
    'mla_q_norm': ((1, 256), None), 'mla_kv_norm': ((1, 256), None), 'mla_w_uq': ((1, 256, 384), 2),
    'mla_w_ukv': ((1, 256, 512), 2), 'mla_w_out': ((1, 256, 1024), 1), 'conv_w_in': ((1, 1024, 768), 2),
    'conv_w': ((1, 3, 256), 2), 'conv_w_out': ((1, 256, 1024), 1), 'ln_g': ((4, 2, 256), 2),
    'ln_b': ((4, 2, 256), 2), 'mlp_w1': ((4, 1024, 1024), 2), 'mlp_w2': ((4, 1024, 1024), 1),
    'ple_w_gate': ((4, 256, 1024), 1), 'ple_w_proj': ((4, 256, 256), 2),
}
WNAMES = list(WSPEC)
BIG = ['gla_w_in', 'gla_w_out', 'mla_w_in', 'mla_w_uq', 'mla_w_ukv', 'mla_w_out', 'conv_w_in', 'conv_w_out',
       'mlp_w1', 'mlp_w2', 'ple_w_gate', 'ple_w_proj']
SMALL_SHARDED = ['gla_w_gate_up', 'gla_b_gate', 'gla_norm_g', 'conv_w', 'ln_g', 'ln_b']
SMALL = SMALL_SHARDED + ['mla_q_norm', 'mla_kv_norm']
MIXER = ['gla', 'mla', 'conv']
LAYER_BIG = {'gla': ['gla_w_in', 'gla_w_out'], 'mla': ['mla_w_in', 'mla_w_uq', 'mla_w_ukv', 'mla_w_out'],
             'conv': ['conv_w_in', 'conv_w_out']}
COMMON_BIG = ['mlp_w1', 'mlp_w2', 'ple_w_gate', 'ple_w_proj']


def _size(shape):
    return int(np.prod(shape))


def _full_shape(name):
    shape, ax = WSPEC[name]
    if ax is None:
        return shape
    return tuple(s * N_CHIPS if i == ax else s for i, s in enumerate(shape))


def _cparams(sem=None):
    return pltpu.CompilerParams(dimension_semantics=sem, vmem_limit_bytes=VMEM_LIMIT)


def _out(shape, dtype):
    return pltpu.HBM(shape, dtype)


def _hbm(v):
    return pltpu.with_memory_space_constraint(v, pltpu.HBM)


def _mm(a, b, *, name, ta=False, tb=False, M=None, N=None, K=None, out_dtypes=(F32,), epilogue=None, extras=(),
        tm=1024, tn=512, tk=None, a_off=(0, 0), b_sh=False, out_sh=False, n_sums=0):
    if M is None:
        M = a.shape[1] if ta else a.shape[0]
    if K is None:
        K = a.shape[0] if ta else a.shape[1]
    if b_sh:
        kw, nq = b.shape[1], b.shape[2]
        n_b, k_b = (kw, N_CHIPS * nq) if tb else (N_CHIPS * nq, kw)
        N = n_b if N is None else N
        assert K == k_b
    elif N is None:
        N = b.shape[0] if tb else b.shape[1]
    if tk is None:
        tk = FULL_ROWS if ta else 1024
    tm, tn, tk = min(tm, M), min(tn, N), min(tk, K)
    assert M % tm == 0 and N % tn == 0 and K % tk == 0, (name, M, N, K, tm, tn, tk)
    nk = K // tk
    n_ex, n_out = len(extras), len(out_dtypes)
    assert n_sums == 0 or tn == N

    n_b = N_CHIPS if (b_sh and tb and tk == K) else 1

    def body(a_ref, *rest):
        b_refs, rest = rest[:n_b], rest[n_b:]
        ex_refs, out_refs = rest[:n_ex], rest[n_ex:n_ex + n_out]
        sum_refs = rest[n_ex + n_out:n_ex + n_out + n_sums]
        first_rows = pl.program_id(0) == 0
        dims = ((((0,) if ta else (1,)), ((1,) if tb else (0,))), ((), ()))
        if n_b == 1:
            part = lax.dot_general(a_ref[...].astype(BF16), b_refs[0][...].astype(BF16), dims,
                                   preferred_element_type=F32)
        else:
            part = sum(lax.dot_general(a_ref[:, s * nq:(s + 1) * nq].astype(BF16), b_refs[s][...].astype(BF16), dims,
                                       preferred_element_type=F32) for s in range(n_b))

        def finish(acc):
            res = (acc,) if epilogue is None else epilogue(acc, *[r[...] for r in ex_refs])
            if n_sums:
                res, sums = res

                @pl.when(first_rows)
                def _():
                    for r in sum_refs:
                        r[...] = jnp.zeros(r.shape, F32)

                for r, v in zip(sum_refs, sums):
                    r[...] += jnp.broadcast_to(v, r.shape)
            for r, v in zip(out_refs, res):
                r[...] = v.astype(r.dtype)

        if nk == 1:
            finish(part)
        else:
            acc_ref = rest[-1]
            k = pl.program_id(2)

            @pl.when(k == 0)
            def _():
                acc_ref[...] = part

            @pl.when(k > 0)
            def _():
                acc_ref[...] += part

            @pl.when(k == nk - 1)
            def _():
                finish(acc_ref[...])

    if ta:
        a_spec = pl.BlockSpec((tk, tm), lambda i, j, k: (k + a_off[0], i + a_off[1]))
    else:
        a_spec = pl.BlockSpec((tm, tk), lambda i, j, k: (i + a_off[0], k + a_off[1]))
    once = dict(pipeline_mode=pl.Buffered(1)) if (tn == N and nk == 1) else {}
    if n_b > 1:
        b_specs = [pl.BlockSpec((None, tn, nq), functools.partial(lambda i, j, k, s: (s, j, 0), s=s), **once)
                   for s in range(n_b)]
    elif b_sh and tb:
        assert nq % tk == 0
        per = nq // tk
        b_spec = pl.BlockSpec((None, tn, tk), lambda i, j, k: (k // per, j, k % per), **once)
    elif b_sh:
        assert nq % tn == 0
        per = nq // tn
        b_spec = pl.BlockSpec((None, tk, tn), lambda i, j, k: (j // per, k, j % per), **once)
    elif tb:
        b_spec = pl.BlockSpec((tn, tk), lambda i, j, k: (j, k), **once)
    else:
        b_spec = pl.BlockSpec((tk, tn), lambda i, j, k: (k, j), **once)
    if n_b == 1:
        b_specs = [b_spec]
    ex_specs = []
    for arr, kind in extras:
        if kind == 'mn':
            ex_specs.append(pl.BlockSpec((tm, tn), lambda i, j, k: (i, j)))
        elif kind == 'n':
            ex_specs.append(pl.BlockSpec((1, tn), lambda i, j, k: (0, j)))
        else:
            ex_specs.append(pl.BlockSpec(arr.shape, lambda i, j, k: (0, 0)))
    if out_sh:
        assert (N // N_CHIPS) % tn == 0
        per_o = N // N_CHIPS // tn
        o_spec = pl.BlockSpec((None, tm, tn), lambda i, j, k: (j // per_o, i, j % per_o))
        o_shape = (N_CHIPS, M, N // N_CHIPS)
    else:
        o_spec = pl.BlockSpec((tm, tn), lambda i, j, k: (i, j))
        o_shape = (M, N)
    outs = pl.pallas_call(
        body, name=name, grid=(M // tm, N // tn, nk),
        in_specs=[a_spec] + b_specs + ex_specs,
        out_specs=[o_spec for _ in out_dtypes] + [pl.BlockSpec((8, N), lambda i, j, k: (0, 0))] * n_sums,
        out_shape=[_out(o_shape, d) for d in out_dtypes] + [_out((8, N), F32)] * n_sums,
        scratch_shapes=[pltpu.VMEM((tm, tn), F32)] if nk > 1 else [],
        compiler_params=_cparams(("arbitrary" if n_sums else "parallel", "parallel", "arbitrary")),
    )(a, *[b] * n_b, *[e[0] for e in extras])
    if n_sums:
        return tuple(outs[:n_out]), tuple(outs[n_out:])
    return outs[0] if n_out == 1 else tuple(outs)


def _rowwise(fn, *, name, rows, pars=(), outs=(), accs=(), tm=256):
    S = rows[0][0].shape[0]
    tm = min(tm, S)
    assert S % tm == 0
    n_r, n_p, n_o, n_a = len(rows), len(pars), len(outs), len(accs)

    def body(*refs):
        r_refs, p_refs = refs[:n_r], refs[n_r:n_r + n_p]
        o_refs, a_refs = refs[n_r + n_p:n_r + n_p + n_o], refs[n_r + n_p + n_o:]
        o_vals, a_vals = fn([r[...] for r in r_refs], [p[...] for p in p_refs])
        for r, v in zip(o_refs, o_vals):
            r[...] = v.astype(r.dtype)
        if n_a:
            i = pl.program_id(0)

            @pl.when(i == 0)
            def _():
                for r in a_refs:
                    r[...] = jnp.zeros(r.shape, r.dtype)

            for r, v in zip(a_refs, a_vals):
                r[...] += jnp.broadcast_to(v, r.shape)

    in_specs = [pl.BlockSpec((tm, w), functools.partial(lambda i, o: (i, o), o=off)) for _, w, off in rows]
    in_specs += [pl.BlockSpec(p.shape, functools.partial(lambda i, nd: (0,) * nd, nd=p.ndim)) for p in pars]
    out_specs = [pl.BlockSpec((tm, w), lambda i: (i, 0)) for w, _ in outs]
    out_specs += [pl.BlockSpec((8, w), lambda i: (0, 0)) for w in accs]
    out_shape = [_out((S, w), d) for w, d in outs]
    out_shape += [_out((8, w), F32) for w in accs]
    res = pl.pallas_call(
        body, name=name, grid=(S // tm,), in_specs=in_specs, out_specs=out_specs, out_shape=out_shape,
        compiler_params=_cparams(("arbitrary",)),
    )(*[r[0] for r in rows], *pars)
    return tuple(res)


def _colsum(v):
    return jnp.sum(v, axis=0, keepdims=True)


def _ln_stats(v):
    mu = jnp.mean(v, axis=-1, keepdims=True)
    d = v - mu
    var = jnp.mean(d * d, axis=-1, keepdims=True)
    rstd = lax.rsqrt(var + LN_EPS)
    return d * rstd, rstd


def _ln_fwd_epilogue(h, x, g, b, *unused):
    v = ALPHA * x + h
    xhat, _ = _ln_stats(v)
    y = xhat * g + b
    return y, y, v


def _ln_bwd_epilogue(scale):
    def epilogue(acc, resid, v, g, *unused):
        dy = acc + scale * resid
        xhat, rstd = _ln_stats(v)
        dxh = dy * g
        m1 = jnp.mean(dxh, axis=-1, keepdims=True)
        m2 = jnp.mean(dxh * xhat, axis=-1, keepdims=True)
        dv = rstd * (dxh - m1 - xhat * m2)
        return (dv, dv), (_colsum(dy * xhat), _colsum(dy))
    return epilogue


def _ple_gate_grads(dx3, z, pp):
    s = jax.nn.sigmoid(z)
    return dx3 * s, dx3 * pp * s * (1.0 - s)


def _loss_head(y, t, z, pp):
    def fn(r, p):
        d = r[0] - r[1]
        dy = d * (1.0 / D_MODEL)
        return [dy, *_ple_gate_grads(dy, r[2], r[3])], [_colsum(d * d) * (0.5 / D_MODEL)]
    return _rowwise(fn, name="loss_head", rows=[(a, D_MODEL, 0) for a in (y, t, z, pp)],
                    outs=[(D_MODEL, F32), (D_MODEL, BF16), (D_MODEL, BF16)], accs=[D_MODEL])


def _input_grad_epilogue(acc, dv, *below):
    dx = acc + ALPHA * dv
    return (dx, *_ple_gate_grads(dx, *below)) if below else (dx,)


N_LEVELS = 6
GLA_STEP = 2


def _gla_consts():
    C = CHUNK
    A = np.zeros((N_LEVELS + 3, C, C), np.float32)
    masks = np.zeros((N_LEVELS + 1, C, C), np.float32)
    r = np.arange(C)[:, None]
    u = np.arange(C)[None, :]
    for l in range(N_LEVELS):
        half = C >> (l + 1)
        mid = (r // (2 * half)) * (2 * half) + half - 1
        A[l] = np.where(r > mid, (u > mid) & (u <= r), (u > r) & (u <= mid))
        masks[l] = ((r // (2 * half)) == (u // (2 * half))) & (((r // half) % 2) != ((u // half) % 2))
    masks[N_LEVELS] = (r == u)
    A[N_LEVELS] = (u <= r)
    A[N_LEVELS + 1] = (u > r)
    A[N_LEVELS + 2] = 1.0
    A = A.reshape(-1, C)
    return A, np.ascontiguousarray(A.T), masks


def _split3(v):
    hi = v.astype(BF16)
    r1 = v - hi.astype(F32)
    mid = r1.astype(BF16)
    lo = (r1 - mid.astype(F32)).astype(BF16)
    return hi, mid, lo


def _dot_exact01(a01, v):
    hi, mid, lo = _split3(v)
    f = lambda p: jnp.dot(a01, p, preferred_element_type=F32)
    return f(hi) + f(mid) + f(lo)


def _nt(a, b):
    return lax.dot_general(a, b, (((1,), (1,)), ((), ())), preferred_element_type=F32)


def _tn(a, b):
    return lax.dot_general(a, b, (((0,), (0,)), ((), ())), preferred_element_type=F32)


def _nn(a, b):
    return jnp.dot(a, b, preferred_element_type=F32)


def _gla_chunk_terms(q, k, E, m_ref):
    C = CHUNK
    scores = m_ref[N_LEVELS] * _nt(q.astype(BF16), k.astype(BF16))
    qes, kes = [], []
    for l in range(N_LEVELS):
        El = E[l * C:(l + 1) * C]
        qe, ke = (q * El).astype(BF16), (k * El).astype(BF16)
        qes.append(qe)
        kes.append(ke)
        scores = scores + m_ref[l] * _nt(qe, ke)
    return qes, kes, scores


def _head(v, h, w):
    return v[:, h * w:(h + 1) * w]


def _gla_fwd(pin, la):
    S = pin.shape[0]
    NC = S // CHUNK
    C, R = CHUNK, CHUNK * GLA_STEP
    A, _, masks = _gla_consts()

    def body(q_ref, k_ref, v_ref, la_ref, a_ref, m_ref, o_ref, st_ref, state):
        @pl.when(pl.program_id(0) == 0)
        def _():
            state[...] = jnp.zeros(state.shape, F32)

        for ci in range(GLA_STEP):
            rows = pl.ds(ci * C, C)
            E_all = jnp.exp(_dot_exact01(a_ref[...], la_ref[rows, :]))
            q_all = q_ref[rows, :] * (GLA_DK ** -0.5)
            k_all, v_all = k_ref[rows, :], v_ref[rows, :]
            outs = []
            for h in range(GLA_HEADS):
                q, k, E = _head(q_all, h, GLA_DK), _head(k_all, h, GLA_DK), _head(E_all, h, GLA_DK)
                _, _, scores = _gla_chunk_terms(q, k, E, m_ref)
                Eq, Ek, Ee = E[6 * C:7 * C], E[7 * C:8 * C], E[8 * C:9 * C]
                st = state[h]
                st_ref[h, ci] = st
                vb = _head(v_all, h, GLA_DV).astype(BF16)
                outs.append(_nn(scores.astype(BF16), vb) + _nt((q * Eq).astype(BF16), st.astype(BF16)))
                state[h] = st * jnp.concatenate([Ee] * (GLA_DV // C), axis=0) + _tn(vb, (k * Ek).astype(BF16))
            o_ref[rows, :] = jnp.concatenate(outs, axis=1)

    return pl.pallas_call(
        body, name="gla_fwd", grid=(NC // GLA_STEP,),
        in_specs=[pl.BlockSpec((R, GLA_HK), lambda c: (c, 0)),
                  pl.BlockSpec((R, GLA_HK), lambda c: (c, 1)),
                  pl.BlockSpec((R, GLA_HV), lambda c: (c, 2 * GLA_HK // GLA_HV)),
                  pl.BlockSpec((R, GLA_HK), lambda c: (c, 0)),
                  pl.BlockSpec(A.shape, lambda c: (0, 0)),
                  pl.BlockSpec(masks.shape, lambda c: (0, 0, 0))],
        out_specs=[pl.BlockSpec((R, GLA_HV), lambda c: (c, 0)),
                   pl.BlockSpec((GLA_HEADS, GLA_STEP, GLA_DV, GLA_DK), lambda c: (0, c, 0, 0))],
        out_shape=[_out((S, GLA_HV), F32), _out((GLA_HEADS, NC, GLA_DV, GLA_DK), F32)],
        scratch_shapes=[pltpu.VMEM((GLA_HEADS, GLA_DV, GLA_DK), F32)],
        compiler_params=_cparams(("arbitrary",)),
    )(pin, pin, pin, la, jnp.asarray(A, BF16), jnp.asarray(masks))


def _gla_bwd(pin, la, states, do):
    S = pin.shape[0]
    NC = S // CHUNK
    C, R = CHUNK, CHUNK * GLA_STEP
    A, AT, masks = _gla_consts()
    scale = GLA_DK ** -0.5

    def body(q_ref, k_ref, v_ref, la_ref, st_ref, do_ref, a_ref, at_ref, m_ref,
             dq_ref, dk_ref, dv_ref, dla_ref, dstate):
        @pl.when(pl.program_id(0) == 0)
        def _():
            dstate[...] = jnp.zeros(dstate.shape, F32)

        for ci in reversed(range(GLA_STEP)):
            one_chunk(ci, pl.ds(ci * C, C), q_ref, k_ref, v_ref, la_ref, st_ref, do_ref, a_ref, at_ref, m_ref,
                      dq_ref, dk_ref, dv_ref, dla_ref, dstate)

    def one_chunk(ci, rows, q_ref, k_ref, v_ref, la_ref, st_ref, do_ref, a_ref, at_ref, m_ref,
                  dq_ref, dk_ref, dv_ref, dla_ref, dstate):
        E_all = jnp.exp(_dot_exact01(a_ref[...], la_ref[rows, :]))
        q_all = q_ref[rows, :] * scale
        k_all, v_all, do_all = k_ref[rows, :], v_ref[rows, :], do_ref[rows, :]
        dqs, dks, dvs, dXs = [], [], [], []
        for h in range(GLA_HEADS):
            q, k, E = _head(q_all, h, GLA_DK), _head(k_all, h, GLA_DK), _head(E_all, h, GLA_DK)
            qes, kes, scores = _gla_chunk_terms(q, k, E, m_ref)
            Eq, Ek, Ee = E[6 * C:7 * C], E[7 * C:8 * C], E[8 * C:9 * C]
            st, dst = st_ref[h, ci], dstate[h]
            dob, vb = _head(do_all, h, GLA_DV).astype(BF16), _head(v_all, h, GLA_DV).astype(BF16)
            dstb = dst.astype(BF16)
            qEq, kEk = (q * Eq).astype(BF16), (k * Ek).astype(BF16)
            dsc = _nt(dob, vb)
            dvs.append(_tn(scores.astype(BF16), dob) + _nt(kEk, dstb))
            dqEq = _nn(dob, st.astype(BF16))
            dkEk = _nn(vb, dstb)
            Gd = (m_ref[N_LEVELS] * dsc).astype(BF16)
            dq = _nn(Gd, k.astype(BF16)) + dqEq * Eq
            dk = _tn(Gd, q.astype(BF16)) + dkEk * Ek
            dX = []
            for l in range(N_LEVELS):
                El = E[l * C:(l + 1) * C]
                G = (m_ref[l] * dsc).astype(BF16)
                dqe, dke = _nn(G, kes[l]), _tn(G, qes[l])
                dq = dq + dqe * El
                dk = dk + dke * El
                dX.append((dqe * q + dke * k) * El)
            dX.append(dqEq * q * Eq)
            dX.append(dkEk * k * Ek)
            prod = dst * st
            dEe = prod[0:C]
            for i in range(1, GLA_DV // C):
                dEe = dEe + prod[i * C:(i + 1) * C]
            dX.append(dEe * Ee)
            dXs.append(jnp.concatenate(dX, axis=0))
            dqs.append(dq * scale)
            dks.append(dk)
            dstate[h] = dst * jnp.concatenate([Ee] * (GLA_DV // C), axis=0) + _tn(dob, qEq)
        dla_ref[rows, :] = _dot_exact01(at_ref[...], jnp.concatenate(dXs, axis=1))
        dq_ref[rows, :] = jnp.concatenate(dqs, axis=1).astype(dq_ref.dtype)
        dk_ref[rows, :] = jnp.concatenate(dks, axis=1).astype(dk_ref.dtype)
        dv_ref[rows, :] = jnp.concatenate(dvs, axis=1).astype(dv_ref.dtype)

    rc = lambda c: NC // GLA_STEP - 1 - c
    return pl.pallas_call(
        body, name="gla_bwd", grid=(NC // GLA_STEP,),
        in_specs=[pl.BlockSpec((R, GLA_HK), lambda c: (rc(c), 0)),
                  pl.BlockSpec((R, GLA_HK), lambda c: (rc(c), 1)),
                  pl.BlockSpec((R, GLA_HV), lambda c: (rc(c), 2 * GLA_HK // GLA_HV)),
                  pl.BlockSpec((R, GLA_HK), lambda c: (rc(c), 0)),
                  pl.BlockSpec((GLA_HEADS, GLA_STEP, GLA_DV, GLA_DK), lambda c: (0, rc(c), 0, 0)),
                  pl.BlockSpec((R, GLA_HV), lambda c: (rc(c), 0)),
                  pl.BlockSpec(A.shape, lambda c: (0, 0)),
                  pl.BlockSpec(AT.shape, lambda c: (0, 0)),
                  pl.BlockSpec(masks.shape, lambda c: (0, 0, 0))],
        out_specs=[pl.BlockSpec((R, GLA_HK), lambda c: (rc(c), 0)),
                   pl.BlockSpec((R, GLA_HK), lambda c: (rc(c), 0)),
                   pl.BlockSpec((R, GLA_HV), lambda c: (rc(c), 0)),
                   pl.BlockSpec((R, GLA_HK), lambda c: (rc(c), 0))],
        out_shape=[_out((S, GLA_HK), BF16), _out((S, GLA_HK), BF16), _out((S, GLA_HV), BF16),
                   _out((S, GLA_HK), F32)],
        scratch_shapes=[pltpu.VMEM((GLA_HEADS, GLA_DV, GLA_DK), F32)],
        compiler_params=_cparams(("arbitrary",)),
    )(pin, pin, pin, la, states, do, jnp.asarray(A, BF16), jnp.asarray(AT, BF16), jnp.asarray(masks))


def _gla_post_fwd(o, pin, g):
    def fn(r, p):
        ov, rv = r
        ys = []
        for h in range(GLA_HEADS):
            oh = ov[:, h * GLA_DV:(h + 1) * GLA_DV]
            rh = rv[:, h * GLA_DV:(h + 1) * GLA_DV]
            rs = lax.rsqrt(jnp.mean(oh * oh, axis=-1, keepdims=True) + RMS_EPS)
            ys.append(oh * rs * p[0] * (rh * jax.nn.sigmoid(rh)))
        return [jnp.concatenate(ys, axis=1)], []
    return _rowwise(fn, name="gla_post_fwd", rows=[(o, GLA_HV, 0), (pin, GLA_HV, (2 * GLA_HK + GLA_HV) // GLA_HV)],
                    pars=[g], outs=[(GLA_HV, BF16)])[0]


def _gla_post_bwd(dy, o, pin, g):
    def fn(r, p):
        dyv, ov, rv = r
        dos, drs, dg = [], [], 0.0
        for h in range(GLA_HEADS):
            sl = slice(h * GLA_DV, (h + 1) * GLA_DV)
            oh, rh, dyh = ov[:, sl], rv[:, sl], dyv[:, sl]
            rs = lax.rsqrt(jnp.mean(oh * oh, axis=-1, keepdims=True) + RMS_EPS)
            xh = oh * rs
            sg = jax.nn.sigmoid(rh)
            d_on = dyh * (rh * sg)
            drs.append(dyh * (xh * p[0]) * (sg * (1.0 + rh * (1.0 - sg))))
            dg = dg + _colsum(d_on * xh)
            dxh = d_on * p[0]
            dos.append(rs * (dxh - xh * jnp.mean(dxh * xh, axis=-1, keepdims=True)))
        return [jnp.concatenate(dos, axis=1), jnp.concatenate(drs, axis=1)], [dg]
    return _rowwise(fn, name="gla_post_bwd",
                    rows=[(dy, GLA_HV, 0), (o, GLA_HV, 0), (pin, GLA_HV, (2 * GLA_HK + GLA_HV) // GLA_HV)],
                    pars=[g], outs=[(GLA_HV, F32), (GLA_HV, BF16)], accs=[GLA_DV])


def _gla_gate_bwd(dla, la):
    def fn(r, p):
        dz = r[0] * (1.0 / GLA_TAU) * (1.0 - jnp.exp(GLA_TAU * r[1]))
        return [dz], [_colsum(dz)]
    return _rowwise(fn, name="gla_gate_bwd", rows=[(dla, GLA_HK, 0), (la, GLA_HK, 0)], outs=[(GLA_HK, BF16)],
                    accs=[GLA_HK])


def _log_sigmoid(z):
    return jnp.minimum(z, 0.0) - jnp.log(1.0 + jnp.exp(-jnp.abs(z)))


def _rot_half(v):
    lane = lax.broadcasted_iota(jnp.int32, v.shape, 1)
    return jnp.where(lane < 32, -pltpu.roll(v, 96, 1), jnp.where(lane < 64, pltpu.roll(v, 32, 1), 0.0))


def _rms(v):
    rs = lax.rsqrt(jnp.mean(v * v, axis=-1, keepdims=True) + RMS_EPS)
    return v * rs, rs


def _mla_norm_fwd(cin, gq, gkv, cosp, sinp):
    def fn(r, p):
        cv, cs, sn = r
        qn, _ = _rms(cv[:, :MLA_QR])
        kvn, _ = _rms(cv[:, MLA_QR:MLA_QR + MLA_KVR])
        kr = cv[:, MLA_QR + MLA_KVR:]
        return [qn * p[0], kvn * p[1], kr * cs + _rot_half(kr) * sn], []
    return _rowwise(fn, name="mla_norm_fwd", rows=[(cin, MLA_IN_PAD, 0), (cosp, 128, 0), (sinp, 128, 0)],
                    pars=[gq, gkv], outs=[(MLA_QR, BF16), (MLA_KVR, BF16), (128, BF16)])


def _mla_qrope_fwd(q, cosp, sinp):
    scale = (MLA_NOPE + MLA_ROPE) ** -0.5

    def fn(r, p):
        qv, cs, sn = r
        parts = []
        for h in range(MLA_HEADS):
            parts.append(qv[:, h * MLA_QH:h * MLA_QH + 128] * scale)
            rp = qv[:, h * MLA_QH + 128:(h + 1) * MLA_QH]
            parts.append((rp * cs + _rot_half(rp) * sn) * scale)
        return [jnp.concatenate(parts, axis=1)], []
    W = MLA_HEADS * MLA_QH
    return _rowwise(fn, name="mla_qrope_fwd", rows=[(q, W, 0), (cosp, 128, 0), (sinp, 128, 0)],
                    outs=[(W, BF16)])[0]


def _mla_qrope_bwd(dq, cosp, sinp):
    scale = (MLA_NOPE + MLA_ROPE) ** -0.5

    def fn(r, p):
        dv, cs, sn = r
        parts = []
        for h in range(MLA_HEADS):
            parts.append(dv[:, h * MLA_QH:h * MLA_QH + 128] * scale)
            rp = dv[:, h * MLA_QH + 128:(h + 1) * MLA_QH]
            parts.append((rp * cs - _rot_half(rp) * sn) * scale)
        return [jnp.concatenate(parts, axis=1)], []
    W = MLA_HEADS * MLA_QH
    return _rowwise(fn, name="mla_qrope_bwd", rows=[(dq, W, 0), (cosp, 128, 0), (sinp, 128, 0)],
                    outs=[(W, BF16)])[0]


def _mla_norm_bwd(cin, dqn, dkvn, dkr, gq, gkv, cosp, sinp):
    def fn(r, p):
        cv, dq_, dkv_, dkr_, cs, sn = r
        outs, accs = [], []
        for (lo, hi), dn, g in (((0, MLA_QR), dq_, p[0]), ((MLA_QR, MLA_QR + MLA_KVR), dkv_, p[1])):
            xh, rs = _rms(cv[:, lo:hi])
            dxh = dn * g
            outs.append(rs * (dxh - xh * jnp.mean(dxh * xh, axis=-1, keepdims=True)))
            accs.append(_colsum(dn * xh))
        dk = dkr_[:, 0:128]
        for h in range(1, MLA_HEADS):
            dk = dk + dkr_[:, h * 128:(h + 1) * 128]
        outs.append(dk * cs - _rot_half(dk) * sn)
        return [jnp.concatenate(outs, axis=1)], accs
    return _rowwise(fn, name="mla_norm_bwd",
                    rows=[(cin, MLA_IN_PAD, 0), (dqn, MLA_QR, 0), (dkvn, MLA_KVR, 0), (dkr, MLA_HEADS * 128, 0),
                          (cosp, 128, 0), (sinp, 128, 0)],
                    pars=[gq, gkv], outs=[(MLA_IN_PAD, BF16)], accs=[MLA_QR, MLA_KVR])


def _mla_probs(q, k, i, tq):
    s = _nt(q, k)
    row = (i * tq + lax.broadcasted_iota(jnp.int32, s.shape, 0)) // CHUNK
    col = lax.broadcasted_iota(jnp.int32, s.shape, 1) // CHUNK
    s = jnp.where(col <= row, s, -jnp.inf)
    e = jnp.exp(s - jnp.max(s, axis=-1, keepdims=True))
    return e / jnp.sum(e, axis=-1, keepdims=True)


def _mla_attn_fwd(qr, knv, kr, tq=256):
    S = qr.shape[0]
    tq = min(tq, S)

    def body(q_ref, kn_ref, v_ref, kr_ref, o_ref, k_cat):
        k_cat[:, :128] = kn_ref[...]
        k_cat[:, 128:] = kr_ref[...]
        for i in range(S // tq):
            rows, keys = pl.ds(i * tq, tq), pl.ds(0, (i + 1) * tq)
            pr = _mla_probs(q_ref[rows, :], k_cat[keys, :], i, tq)
            o_ref[rows, :] = _nn(pr.astype(BF16), v_ref[keys, :])

    return pl.pallas_call(
        body, name="mla_attn_fwd", grid=(MLA_HEADS,),
        in_specs=[pl.BlockSpec((S, MLA_QH), lambda h: (0, h)),
                  pl.BlockSpec((S, 128), lambda h: (0, h)),
                  pl.BlockSpec((S, 128), lambda h: (0, MLA_HEADS + h)),
                  pl.BlockSpec((S, 128), lambda h: (0, 0))],
        out_specs=pl.BlockSpec((S, 128), lambda h: (0, h)),
        out_shape=_out((S, MLA_HEADS * MLA_V), F32),
        scratch_shapes=[pltpu.VMEM((S, MLA_QH), BF16)],
        compiler_params=_cparams(("parallel",)),
    )(qr, knv, knv, kr)


def _mla_attn_bwd(qr, knv, kr, o, do, tq=256):
    S = qr.shape[0]
    tq = min(tq, S)
    W = MLA_HEADS * 128

    def body(q_ref, kn_ref, v_ref, kr_ref, o_ref, do_ref, dq_ref, dkn_ref, dv_ref, dkr_ref, k_cat, dk_acc, dv_acc):
        k_cat[:, :128] = kn_ref[...]
        k_cat[:, 128:] = kr_ref[...]
        dk_acc[...] = jnp.zeros(dk_acc.shape, F32)
        dv_acc[...] = jnp.zeros(dv_acc.shape, F32)
        for i in range(S // tq):
            rows, keys = pl.ds(i * tq, tq), pl.ds(0, (i + 1) * tq)
            q, k, v = q_ref[rows, :], k_cat[keys, :], v_ref[keys, :]
            pr = _mla_probs(q, k, i, tq)
            dov = do_ref[rows, :]
            delta = jnp.sum(dov * o_ref[rows, :], axis=-1, keepdims=True)
            dob = dov.astype(BF16)
            ds = (pr * (_nt(dob, v) - delta)).astype(BF16)
            dq_ref[rows, :] = _nn(ds, k)
            dk_acc[keys, :] += _tn(ds, q)
            dv_acc[keys, :] += _tn(pr.astype(BF16), dob)
        dkn_ref[...] = dk_acc[:, :128].astype(dkn_ref.dtype)
        dkr_ref[...] = dk_acc[:, 128:]
        dv_ref[...] = dv_acc[...].astype(dv_ref.dtype)

    head = lambda w: pl.BlockSpec((S, w), lambda h: (0, h))
    return pl.pallas_call(
        body, name="mla_attn_bwd", grid=(MLA_HEADS,),
        in_specs=[head(MLA_QH), head(128), pl.BlockSpec((S, 128), lambda h: (0, MLA_HEADS + h)),
                  pl.BlockSpec((S, 128), lambda h: (0, 0)), head(128), head(128)],
        out_specs=[head(MLA_QH), head(128), head(128), head(128)],
        out_shape=[_out((S, MLA_HEADS * MLA_QH), F32), _out((S, W), BF16), _out((S, W), BF16), _out((S, W), F32)],
        scratch_shapes=[pltpu.VMEM((S, MLA_QH), BF16), pltpu.VMEM((S, MLA_QH), F32), pltpu.VMEM((S, 128), F32)],
        compiler_params=_cparams(("parallel",)),
    )(qr, knv, knv, kr, o, do)


CONV_TILE = 256


def _shift_down(v, n):
    row = lax.broadcasted_iota(jnp.int32, v.shape, 0)
    return jnp.where(row >= n, pltpu.roll(v, n, 0), 0.0)


def _shift_up(v, n):
    S = v.shape[0]
    row = lax.broadcasted_iota(jnp.int32, v.shape, 0)
    return jnp.where(row < S - n, pltpu.roll(v, S - n, 0), 0.0)


def _conv_specs(S, n_extra_cols):
    nt = D_MODEL // CONV_TILE
    specs = [pl.BlockSpec((S, CONV_TILE), functools.partial(lambda j, o: (0, o + j), o=part * nt))
             for part in range(3)]
    specs.append(pl.BlockSpec((8, CONV_TILE), lambda j: (0, j)))
    specs += [pl.BlockSpec((S, CONV_TILE), lambda j: (0, j)) for _ in range(n_extra_cols)]
    return specs


def _conv_fwd(bcu, w8):
    S = bcu.shape[0]

    def body(b_ref, c_ref, u_ref, w_ref, y_ref):
        cu = c_ref[...] * u_ref[...]
        z = w_ref[2:3, :] * cu + w_ref[1:2, :] * _shift_down(cu, 1) + w_ref[0:1, :] * _shift_down(cu, 2)
        y_ref[...] = (b_ref[...] * z).astype(y_ref.dtype)

    return pl.pallas_call(
        body, name="conv_fwd", grid=(D_MODEL // CONV_TILE,), in_specs=_conv_specs(S, 0),
        out_specs=pl.BlockSpec((S, CONV_TILE), lambda j: (0, j)),
        out_shape=_out((S, D_MODEL), BF16),
        compiler_params=_cparams(("parallel",)),
    )(bcu, bcu, bcu, w8)


def _conv_bwd(bcu, w8, dy):
    S = bcu.shape[0]

    def body(b_ref, c_ref, u_ref, w_ref, dy_ref, db_ref, dc_ref, du_ref, dw_ref):
        b, c, u, dyv = b_ref[...], c_ref[...], u_ref[...], dy_ref[...]
        w0, w1, w2 = w_ref[0:1, :], w_ref[1:2, :], w_ref[2:3, :]
        cu = c * u
        cu1, cu2 = _shift_down(cu, 1), _shift_down(cu, 2)
        z = w2 * cu + w1 * cu1 + w0 * cu2
        dz = dyv * b
        db_ref[...] = (dyv * z).astype(db_ref.dtype)
        dcu = w2 * dz + w1 * _shift_up(dz, 1) + w0 * _shift_up(dz, 2)
        dc_ref[...] = (dcu * u).astype(dc_ref.dtype)
        du_ref[...] = (dcu * c).astype(du_ref.dtype)
        dw_ref[...] = jnp.zeros(dw_ref.shape, F32)
        dw_ref[0:1, :] = _colsum(dz * cu2)
        dw_ref[1:2, :] = _colsum(dz * cu1)
        dw_ref[2:3, :] = _colsum(dz * cu)

    col = pl.BlockSpec((S, CONV_TILE), lambda j: (0, j))
    return pl.pallas_call(
        body, name="conv_bwd", grid=(D_MODEL // CONV_TILE,), in_specs=_conv_specs(S, 1),
        out_specs=[col, col, col, pl.BlockSpec((8, CONV_TILE), lambda j: (0, j))],
        out_shape=[_out((S, D_MODEL), BF16)] * 3 + [_out((8, D_MODEL), F32)],
        compiler_params=_cparams(("parallel",)),
    )(bcu, bcu, bcu, w8, dy)


def _adamw_update(w, g, m, v):
    nm = ADAM_B1 * m + (1.0 - ADAM_B1) * g
    nv = ADAM_B2 * v + (1.0 - ADAM_B2) * jnp.square(g)
    m_hat = nm / (1.0 - ADAM_B1 ** ADAM_STEP)
    v_hat = nv / (1.0 - ADAM_B2 ** ADAM_STEP)
    return -ADAM_LR * (m_hat / (jnp.sqrt(v_hat) + ADAM_EPS) + ADAM_WD * w), nm, nv


def _adamw_shard_major(w, m, v, gs, name):
    view = lambda a: jnp.transpose(a, (2, 0, 1))
    g = jnp.stack([x.T for x in gs], axis=1)
    n, L, k = g.shape
    rows = n // 4
    assert n % 4 == 0

    def body(w_ref, m_ref, v_ref, g_ref, go_ref, d_ref, nm_ref, nv_ref):
        gv = g_ref[...]
        d_ref[...], nm_ref[...], nv_ref[...] = _adamw_update(w_ref[...], gv, m_ref[...], v_ref[...])
        go_ref[...] = gv

    spec = pl.BlockSpec((rows, L, k), lambda i: (i, 0, 0))
    outs = pl.pallas_call(
        body, name=name, grid=(4,), in_specs=[spec] * 4, out_specs=[spec] * 4,
        out_shape=[jax.ShapeDtypeStruct((n, L, k), F32)] * 4,
        compiler_params=_cparams(("parallel",)),
    )(view(w), view(m), view(v), g)
    return [jnp.transpose(o, (1, 2, 0)) for o in outs]


def _adamw(w, m, v, gs, name):
    L, R, Cn = w.shape
    assert len(gs) == L
    tr = R if R <= 256 else 256
    assert R % tr == 0

    def body(w_ref, m_ref, v_ref, *rest):
        g_refs, (go_ref, d_ref, nm_ref, nv_ref) = rest[:L], rest[L:]
        layer = pl.program_id(0)
        gv = g_refs[0][...]
        for k in range(1, L):
            gv = jnp.where(layer == k, g_refs[k][...], gv)
        d_ref[...], nm_ref[...], nv_ref[...] = _adamw_update(w_ref[...], gv, m_ref[...], v_ref[...])
        go_ref[...] = gv

    spec = pl.BlockSpec((None, tr, Cn), lambda l, i: (l, i, 0))
    g_specs = [pl.BlockSpec((tr, Cn), functools.partial(lambda l, i, k: (jnp.where(l == k, i, 0), 0), k=k))
               for k in range(L)]
    return pl.pallas_call(
        body, name=name, grid=(L, R // tr), in_specs=[spec] * 3 + g_specs, out_specs=[spec] * 4,
        out_shape=[jax.ShapeDtypeStruct((L, R, Cn), F32)] * 4,
        compiler_params=_cparams(("arbitrary", "arbitrary")),
    )(w, m, v, *gs)


HBM_SPEC = pl.BlockSpec(memory_space=pltpu.HBM)


def _place():
    return lax.axis_index("x"), lax.axis_index("y"), lax.axis_index("c")


def _other_chips(x, y):
    return [(1 - x, y), (x, 1 - y), (1 - x, 1 - y)]


SEM_SPEC = pl.BlockSpec(memory_space=pltpu.SEMAPHORE)
ANY_SPEC = pl.BlockSpec(memory_space=pl.ANY)
VMEM_SPEC = pl.BlockSpec(memory_space=pltpu.VMEM)
EFFECT = pltpu.SideEffectType.DATAFLOW_SIDE_EFFECTING
TOKEN = (8, 128)


def _ici_start(srcs, lands, after, copies, name, per_src=3):
    n, nl = len(srcs), len(lands)

    def body(*refs):
        src_refs, land_refs = refs[:n], refs[n:n + nl]
        send_sems, recv_sems, token = refs[n + nl + 1], refs[n + nl + 2], refs[-1]
        x, y, c = _place()
        for k, src, dst, to in copies(src_refs, land_refs, x, y, c):
            pltpu.make_async_remote_copy(src_ref=src, dst_ref=dst, send_sem=send_sems.at[k], recv_sem=recv_sems.at[k],
                                         device_id=to, device_id_type=MESH).start()
        token[...] = jnp.zeros(TOKEN, F32)

    n_copies = per_src * max(n, nl if n == 0 else 0)
    res = pl.pallas_call(
        body, name=name,
        out_shape=(pltpu.SemaphoreType.DMA((n_copies,)), pltpu.SemaphoreType.DMA((n_copies,)),
                   *[pltpu.HBM(s.shape, s.dtype) for s in srcs], *[pltpu.HBM(l.shape, l.dtype) for l in lands],
                   jax.ShapeDtypeStruct(TOKEN, F32)),
        in_specs=[HBM_SPEC] * (n + nl) + [ANY_SPEC],
        out_specs=(SEM_SPEC, SEM_SPEC, *[HBM_SPEC] * (n + nl), VMEM_SPEC),
        input_output_aliases={t: 2 + t for t in range(n + nl)},
        compiler_params=pltpu.CompilerParams(has_side_effects=EFFECT),
    )(*[_hbm(s) for s in srcs], *[_hbm(l) for l in lands], after)
    return res[0], res[1], list(res[2:2 + n]), list(res[2 + n:2 + n + nl]), res[-1]


def _ici_wait(handle, after, copies, name):
    send_sems, recv_sems, srcs, lands, _ = handle
    n, nl = len(srcs), len(lands)

    def body(*refs):
        src_refs, land_refs = refs[:n], refs[n:n + nl]
        send_s, recv_s = refs[n + nl], refs[n + nl + 1]
        x, y, c = _place()
        for k, src, dst, to in copies(src_refs, land_refs, x, y, c):
            cp = pltpu.make_async_remote_copy(src_ref=src, dst_ref=dst, send_sem=send_s.at[k], recv_sem=recv_s.at[k],
                                              device_id=to, device_id_type=MESH)
            cp.wait_send()
            cp.wait_recv()

    res = pl.pallas_call(
        body, name=name,
        out_shape=(*[pltpu.HBM(s.shape, s.dtype) for s in srcs], *[pltpu.HBM(l.shape, l.dtype) for l in lands]),
        in_specs=[HBM_SPEC] * (n + nl) + [SEM_SPEC, SEM_SPEC, ANY_SPEC],
        out_specs=tuple([HBM_SPEC] * (n + nl)),
        input_output_aliases={t: t for t in range(n + nl)},
        compiler_params=pltpu.CompilerParams(has_side_effects=EFFECT),
    )(*srcs, *lands, send_sems, recv_sems, after)
    return list(res[:n]), list(res[n:])


def _gather_copies(halves, arriving):
    def copies(src_refs, land_refs, x, y, c):
        q = 2 * x + y
        out = []
        for t, H in enumerate(halves):
            mine = land_refs[t].at[q, pl.ds(c * H, H), :]
            for j, (cx, cy) in enumerate(_other_chips(x, y)):
                theirs = land_refs[t].at[2 * cx + cy, pl.ds(c * H, H), :]
                out.append((3 * t + j, mine, theirs if arriving else mine, (cx, cy, c)))
        return out
    return copies


def _place_own(ops, name):
    n = len(ops)
    kinds = sorted({(o.shape, str(o.dtype)) for o in ops})
    kind_of = [kinds.index((o.shape, str(o.dtype))) for o in ops]

    def body(*refs):
        in_refs, out_refs = refs[:n], refs[n:2 * n]
        rd_sems, wr_sems, bufs = refs[2 * n], refs[2 * n + 1], refs[2 * n + 2:]
        x, y, _ = _place()
        used = [0] * len(kinds)
        slot, busy = [], {}
        for t in range(n):
            slot.append((kind_of[t], used[kind_of[t]] % 2))
            used[kind_of[t]] += 1
        rd = lambda t: pltpu.make_async_copy(in_refs[t], bufs[slot[t][0]].at[slot[t][1]], rd_sems.at[t])
        wr = lambda t: pltpu.make_async_copy(bufs[slot[t][0]].at[slot[t][1]], out_refs[t].at[2 * x + y],
                                             wr_sems.at[t])
        rd(0).start()
        for t in range(n):
            rd(t).wait()
            wr(t).start()
            busy[slot[t]] = t
            if t + 1 < n:
                if slot[t + 1] in busy:
                    wr(busy.pop(slot[t + 1])).wait()
                rd(t + 1).start()
        for t in busy.values():
            wr(t).wait()

    return pl.pallas_call(
        body, name=name, in_specs=[HBM_SPEC] * n, out_specs=[HBM_SPEC] * n,
        out_shape=[jax.ShapeDtypeStruct((N_CHIPS,) + o.shape, o.dtype) for o in ops],
        scratch_shapes=[pltpu.SemaphoreType.DMA((n,)), pltpu.SemaphoreType.DMA((n,))]
        + [pltpu.VMEM((2,) + shape, jnp.dtype(dt)) for shape, dt in kinds],
        compiler_params=pltpu.CompilerParams(vmem_limit_bytes=VMEM_LIMIT),
    )(*ops)


def _gather_start(lands, after, name):
    return _ici_start([], lands, after, _gather_copies([l.shape[1] // 2 for l in lands], False), name)


def _gather_wait(handle, after, name):
    halves = [l.shape[1] // 2 for l in handle[3]]
    return _ici_wait(handle, after, _gather_copies(halves, True), name)


def _forward_copies(halves, arriving):
    def copies(src_refs, land_refs, x, y, c):
        out = []
        for t, H in enumerate(halves):
            for j, (cx, cy) in enumerate(_other_chips(x, y)):
                mine = land_refs[t].at[2 * cx + cy, pl.ds(c * H, H), :]
                theirs = land_refs[t].at[2 * cx + cy, pl.ds((1 - c) * H, H), :]
                out.append((3 * t + j, mine, theirs if arriving else mine, (x, y, 1 - c)))
        return out
    return copies


def _forward_start(lands, after, name):
    halves = [l.shape[1] // 2 for l in lands]
    return _ici_start([], lands, after, _forward_copies(halves, False), name)


def _forward_wait(handle, after, name):
    halves = [l.shape[1] // 2 for l in handle[3]]
    return _ici_wait(handle, after, _forward_copies(halves, True), name)[1]


def _swap_halves(ops, name):
    n = len(ops)

    def body(*refs):
        in_refs, out_refs, send_sems, recv_sems = refs[:n], refs[n:2 * n], refs[2 * n], refs[2 * n + 1]
        x, y, c = _place()
        cps = []
        for t in range(n):
            H = ops[t].shape[1] // 2
            cp = pltpu.make_async_remote_copy(src_ref=in_refs[t].at[:, pl.ds((1 - c) * H, H), :],
                                              dst_ref=out_refs[t], send_sem=send_sems.at[t],
                                              recv_sem=recv_sems.at[t], device_id=(x, y, 1 - c),
                                              device_id_type=MESH)
            cp.start()
            cps.append(cp)
        for cp in cps:
            cp.wait()

    return pl.pallas_call(
        body, name=name, in_specs=[HBM_SPEC] * n, out_specs=[HBM_SPEC] * n,
        out_shape=[jax.ShapeDtypeStruct((N_CHIPS, o.shape[1] // 2, o.shape[2]), o.dtype) for o in ops],
        scratch_shapes=[pltpu.SemaphoreType.DMA((n,)), pltpu.SemaphoreType.DMA((n,))],
    )(*ops)


def _sum_rows_tile(h):
    return h if h <= 512 else 512


def _pair_sum(g, t, cq, name):
    _, a, b = g.shape
    H = a // 2
    tr = _sum_rows_tile(H)

    def body(cq_ref, g_ref, t_ref, o_ref):
        o_ref[...] = (g_ref[...].astype(F32) + t_ref[...].astype(F32)).astype(o_ref.dtype)

    grid_spec = pltpu.PrefetchScalarGridSpec(
        num_scalar_prefetch=1, grid=(N_CHIPS, H // tr),
        in_specs=[pl.BlockSpec((None, None, tr, b), lambda j, i, cq_ref: (j, cq_ref[0], i, 0)),
                  pl.BlockSpec((None, tr, b), lambda j, i, cq_ref: (j, i, 0))],
        out_specs=pl.BlockSpec((None, tr, b), lambda j, i, cq_ref: (j, i, 0)))
    return pl.pallas_call(
        body, name=name, grid_spec=grid_spec, out_shape=_out(t.shape, BF16),
        compiler_params=_cparams(("parallel", "parallel")),
    )(cq, g.reshape(N_CHIPS, 2, H, b), t)


def _scatter_copies(src_refs, land_refs, x, y, c):
    out = []
    for j, (cx, cy) in enumerate(_other_chips(x, y)):
        for t in range(len(src_refs)):
            out.append((3 * t + j, src_refs[t].at[2 * cx + cy], land_refs[t].at[j], (cx, cy, c)))
    return out


def _scatter_start(ops, after, name):
    lands = [lax.empty((3,) + o.shape[1:], o.dtype) for o in ops]
    return _ici_start(ops, lands, after, _scatter_copies, name)


def _scatter_wait(handle, after, name):
    return _ici_wait(handle, after, _scatter_copies, name)


def _chip_sum(p, t, cq, name):
    _, H, b = p.shape
    tr = _sum_rows_tile(H)

    def body(cq_ref, p_ref, t_ref, o_ref):
        acc = p_ref[...].astype(F32)
        for j in range(3):
            acc = acc + t_ref[j].astype(F32)
        o_ref[...] = acc

    grid_spec = pltpu.PrefetchScalarGridSpec(
        num_scalar_prefetch=1, grid=(H // tr,),
        in_specs=[pl.BlockSpec((None, tr, b), lambda i, cq_ref: (cq_ref[1], i, 0)),
                  pl.BlockSpec((3, tr, b), lambda i, cq_ref: (0, i, 0))],
        out_specs=pl.BlockSpec((None, tr, b), lambda i, cq_ref: (cq_ref[0], i, 0)))
    out = pl.pallas_call(
        body, name=name, grid_spec=grid_spec, out_shape=_out((2, H, b), F32),
        compiler_params=_cparams(("parallel",)),
    )(cq, p, t)
    return out.reshape(2 * H, b)


def _join_copies(arriving):
    def copies(src_refs, land_refs, x, y, c):
        out = []
        for t, land in enumerate(land_refs):
            H = land.shape[0] // 2
            mine, theirs = land.at[pl.ds(c * H, H), :], land.at[pl.ds((1 - c) * H, H), :]
            out.append((t, mine, theirs if arriving else mine, (x, y, 1 - c)))
        return out
    return copies


def _join_start(fs, name):
    return _ici_start([], fs, jnp.zeros(TOKEN, F32), _join_copies(False), name, per_src=1)


def _join_wait(handle, after, name):
    return _ici_wait(handle, after, _join_copies(True), name)[1]


def _direct_copies(src_refs, land_refs, x, y, c):
    out = []
    for t in range(len(src_refs)):
        H = src_refs[t].shape[1] // 2
        for k in range(1, 8):
            px, py, pc = x ^ (k >> 2), y ^ ((k >> 1) & 1), c ^ (k & 1)
            out.append((7 * t + k - 1, src_refs[t].at[2 * px + py, pl.ds(pc * H, H), :], land_refs[t].at[k - 1],
                        (px, py, pc)))
    return out


def _direct_sum(g, t, cq, name):
    _, a, b = g.shape
    H = a // 2
    tr = _sum_rows_tile(H)

    def body(cq_ref, g_ref, t_ref, o_ref):
        acc = g_ref[...].astype(F32)
        for k in range(7):
            acc = acc + t_ref[k].astype(F32)
        o_ref[...] = acc

    grid_spec = pltpu.PrefetchScalarGridSpec(
        num_scalar_prefetch=1, grid=(H // tr,),
        in_specs=[pl.BlockSpec((None, None, tr, b), lambda i, cq_ref: (cq_ref[1], cq_ref[0], i, 0)),
                  pl.BlockSpec((7, tr, b), lambda i, cq_ref: (0, i, 0))],
        out_specs=pl.BlockSpec((None, tr, b), lambda i, cq_ref: (cq_ref[0], i, 0)))
    out = pl.pallas_call(
        body, name=name, grid_spec=grid_spec, out_shape=_out((2, H, b), F32),
        compiler_params=_cparams(("parallel",)),
    )(cq, g.reshape(N_CHIPS, 2, H, b), t)
    return out.reshape(a, b)


def _reduce_direct_start(gs, tag):
    lands = [lax.empty((7, g.shape[1] // 2, g.shape[2]), g.dtype) for g in gs]
    return _ici_start(gs, lands, jnp.zeros(TOKEN, F32), _direct_copies, "rs_direct_start_" + tag, per_src=7)


def _reduce_direct_finish(handle, cq, after, tag):
    gs, rs = _ici_wait(handle, after, _direct_copies, "rs_direct_wait_" + tag)
    fs = [_direct_sum(g, r, cq, "rs_direct_sum") for g, r in zip(gs, rs)]
    return _join_start(fs, "rs_join_start_" + tag)


def _reduce_scatter_start(gs, cq, after, tag):
    ts = _swap_halves(gs, "rs_swap_" + tag)
    ps = [_pair_sum(g, t, cq, "rs_pair_sum") for g, t in zip(gs, ts)]
    return _scatter_start(ps, after, "rs_scatter_start_" + tag)


def _reduce_scatter_finish(handle, cq, after, tag):
    ps, rs = _scatter_wait(handle, after, "rs_scatter_wait_" + tag)
    fs = [_chip_sum(p, r, cq, "rs_chip_sum") for p, r in zip(ps, rs)]
    return _join_start(fs, "rs_join_start_" + tag)


def _all_reduce_small(v):
    n = v.shape[0]

    def body(v_ref, out_ref, buf, send_sems, recv_sems):
        x, y, c = _place()
        me = 4 * x + 2 * y + c
        buf[me] = v_ref[...]
        cps = []
        for k in range(1, 8):
            peer = (x ^ (k >> 2), y ^ ((k >> 1) & 1), c ^ (k & 1))
            cp = pltpu.make_async_remote_copy(src_ref=v_ref, dst_ref=buf.at[me], send_sem=send_sems.at[k - 1],
                                              recv_sem=recv_sems.at[k - 1], device_id=peer, device_id_type=MESH)
            cp.start()
            cps.append(cp)
        for k in range(1, 8):
            px, py, pc = x ^ (k >> 2), y ^ ((k >> 1) & 1), c ^ (k & 1)
            land = buf.at[4 * px + 2 * py + pc]
            pltpu.make_async_remote_copy(src_ref=land, dst_ref=land, send_sem=send_sems.at[k - 1],
                                         recv_sem=recv_sems.at[k - 1], device_id=(px, py, pc),
                                         device_id_type=MESH).wait_recv()
        for cp in cps:
            cp.wait_send()
        acc = buf[0]
        for d in range(1, 8):
            acc = acc + buf[d]
        out_ref[...] = acc

    vm = pl.BlockSpec(memory_space=pltpu.VMEM)
    return pl.pallas_call(
        body, name="all_reduce_small", in_specs=[vm], out_specs=vm,
        out_shape=jax.ShapeDtypeStruct((n, 128), F32),
        scratch_shapes=[pltpu.VMEM((8, n, 128), F32), pltpu.SemaphoreType.DMA((7,)), pltpu.SemaphoreType.DMA((7,))],
    )(v)


SMALL_GATHER = (16, 1024)
SMALL_FULL = sum(_size(_full_shape(n)) for n in SMALL)
SMALL_FULL_ROWS = -(-SMALL_FULL // 128 // 8) * 8


def _layer_shards(w, i, q):
    kind, j = MIXER[i % 3], i // 3
    out = {n: w[n][i].astype(BF16) for n in COMMON_BIG}
    if kind == 'gla':
        win = jnp.zeros((D_MODEL, GLA_WIN), F32)
        win = lax.dynamic_update_slice(win, w['gla_w_in'][j], (0, (GLA_SHARD - GLA_WIN_STEP) * q))
        out['gla_w_in'] = win.astype(BF16)
        out['gla_w_out'] = w['gla_w_out'][j].astype(BF16)
    elif kind == 'mla':
        out['mla_w_in'] = jnp.pad(w['mla_w_in'][j], ((0, 0), (0, MLA_IN_PAD - MLA_IN))).astype(BF16)
        for n in ('mla_w_uq', 'mla_w_ukv', 'mla_w_out'):
            out[n] = w[n][j].astype(BF16)
    else:
        out['conv_w_in'] = w['conv_w_in'][j].astype(BF16)
        out['conv_w_out'] = w['conv_w_out'][j].astype(BF16)
    return out


def _rows_joined(g):
    return g.reshape(g.shape[0] * g.shape[1], g.shape[2])


def _cols_joined(g):
    return jnp.moveaxis(g, 0, 1).reshape(g.shape[1], -1)


def _layer_weights(g, i):
    kind = MIXER[i % 3]
    W = {}
    if 'mlp_w1' in g:
        W = {'w1': g['mlp_w1'], 'w2': _rows_joined(g['mlp_w2']), 'gate': _rows_joined(g['ple_w_gate']),
             'proj': g['ple_w_proj']}
    if kind == 'gla' and 'gla_w_out' in g:
        W['w_out'] = _rows_joined(g['gla_w_out'])
    if kind == 'gla' and 'gla_w_in' in g:
        parts = []
        for qq in range(N_CHIPS):
            lo = g['gla_w_in'][qq][:, :128]
            if qq > 0:
                lo = lo + g['gla_w_in'][qq - 1][:, GLA_WIN_STEP:]
            parts += [lo, g['gla_w_in'][qq][:, 128:GLA_WIN_STEP]]
        parts.append(g['gla_w_in'][N_CHIPS - 1][:, GLA_WIN_STEP:])
        W['w_in'] = jnp.concatenate(parts, axis=1)
    elif kind == 'mla':
        W['w_in'] = _rows_joined(g['mla_w_in'])
        uq = _cols_joined(g['mla_w_uq']).reshape(MLA_QR, MLA_HEADS, MLA_NOPE + MLA_ROPE)
        W['w_uq'] = jnp.pad(uq, ((0, 0), (0, 0), (0, MLA_QH - MLA_NOPE - MLA_ROPE))).reshape(MLA_QR, -1)
        ukv = _cols_joined(g['mla_w_ukv']).reshape(MLA_KVR, MLA_HEADS, 2, 128)
        W['w_ukv'] = ukv.transpose(0, 2, 1, 3).reshape(MLA_KVR, -1)
        W['w_out'] = _rows_joined(g['mla_w_out'])
    elif kind == 'conv':
        W['w_in'] = g['conv_w_in']
        W['w_out'] = _rows_joined(g['conv_w_out'])
    return W


def _pack_small_shards(w):
    flat = jnp.concatenate([w[n].reshape(-1) for n in SMALL_SHARDED])
    return jnp.pad(flat, (0, _size(SMALL_GATHER) - flat.shape[0])).reshape(SMALL_GATHER)


def _unpack_small_gathered(g):
    flat, out, off = g.reshape(N_CHIPS, -1), {}, 0
    for n in SMALL_SHARDED:
        shape, ax = WSPEC[n]
        seg = flat[:, off:off + _size(shape)].reshape((N_CHIPS,) + shape)
        out[n] = jnp.moveaxis(seg, 0, ax).reshape(_full_shape(n))
        off += _size(shape)
    return out


def _pack_small(vals):
    flat = jnp.concatenate([vals[n].reshape(-1) for n in SMALL])
    return jnp.pad(flat, (0, SMALL_FULL_ROWS * 128 - flat.shape[0])).reshape(SMALL_FULL_ROWS, 128)


def _unpack_small(packed, q):
    flat = packed.reshape(-1)
    out, off = {}, 0
    for n in SMALL:
        shape, ax = WSPEC[n]
        full = flat[off:off + _size(_full_shape(n))].reshape(_full_shape(n))
        off += _size(_full_shape(n))
        out[n] = full if ax is None else lax.dynamic_slice_in_dim(full, q * shape[ax], shape[ax], axis=ax)
    return out


def _row_shards(dw):
    return dw.reshape(N_CHIPS, dw.shape[0] // N_CHIPS, dw.shape[1])


def _col_shards(dw):
    return jnp.moveaxis(dw.reshape(dw.shape[0], N_CHIPS, -1), 1, 0)


def _row(v):
    return v.reshape(1, -1)


def _layer_fwd(i, xin, xin_b, p_i, W, sm, cosp, sinp, rest=None, mid=None):
    kind, j = MIXER[i % 3], i // 3
    sv = {'xin': xin, 'xin_b': xin_b}
    if kind == 'gla':
        w_up = jnp.pad(sm['gla_w_gate_up'][j].astype(BF16), ((0, 128 - GLA_RANK), (0, 0)))
        pin = _mm(xin_b, W['w_in'], name="gla_in", tn=640, tm=FULL_ROWS)
        la = _mm(pin, w_up, name="gla_gate", K=128, tk=128, a_off=(0, (GLA_IN_PAD - 128) // 128), tn=512,
                 extras=[(_row(sm['gla_b_gate'][j]), 'n')],
                 epilogue=lambda acc, b: (_log_sigmoid(acc + b) * (1.0 / GLA_TAU),))
        o, states = _gla_fwd(pin, la)
        yb = _gla_post_fwd(o, pin, _row(sm['gla_norm_g'][j]))
        if rest is not None:
            W = {**W, **rest(yb)}
        mixed = yb
        sv.update(w_up=w_up, pin=pin, la=la, o=o, states=states, yb=yb)
    elif kind == 'mla':
        gq, gkv = sm['mla_q_norm'][j:j + 1], sm['mla_kv_norm'][j:j + 1]
        cin = _mm(xin_b, W['w_in'], name="mla_in", tn=640, tm=FULL_ROWS)
        qn, kvn, kr = _mla_norm_fwd(cin, gq, gkv, cosp, sinp)
        qr = _mla_qrope_fwd(_mm(qn, W['w_uq'], name="mla_uq"), cosp, sinp)
        knv = _mm(kvn, W['w_ukv'], name="mla_ukv", out_dtypes=(BF16,))
        o = _mla_attn_fwd(qr, knv, kr)
        ob = o.astype(BF16)
        mixed = ob
        sv.update(gq=gq, gkv=gkv, cin=cin, qn=qn, kvn=kvn, kr=kr, qr=qr, knv=knv, o=o, ob=ob)
    else:
        w8 = jnp.pad(sm['conv_w'][j], ((0, 5), (0, 0)))
        bcu = _mm(xin_b, W['w_in'], name="conv_in", tn=768, b_sh=True, tm=FULL_ROWS)
        yb = _conv_fwd(bcu, w8)
        mixed = yb
        sv.update(w8=w8, bcu=bcu, yb=yb)
    g0, b0 = _row(sm['ln_g'][i, 0]), _row(sm['ln_b'][i, 0])
    g1, b1 = _row(sm['ln_g'][i, 1]), _row(sm['ln_b'][i, 1])
    ln = dict(tm=512, tn=D_MODEL, out_dtypes=(F32, BF16, F32), epilogue=_ln_fwd_epilogue)
    x1, x1b, v0 = _mm(mixed, W['w_out'], name="mix_out_ln", extras=[(xin, 'mn'), (g0, 'n'), (b0, 'n')], **ln)
    ab = _mm(x1b, W['w1'], name="mlp_up", out_dtypes=(BF16,), b_sh=True, tm=FULL_ROWS,
             epilogue=lambda acc: (jnp.square(jnp.maximum(acc, 0.0)),))
    order = [(mid(ab), 'whole')] if mid else []
    x2, x2b, v1 = _mm(ab, W['w2'], name="mlp_down_ln", tk=D_FF,
                      extras=[(x1, 'mn'), (g1, 'n'), (b1, 'n')] + order, **ln)
    pp = _mm(p_i, W['proj'], name="ple_proj", tn=256, b_sh=True)
    z, x3, x3b = _mm(x2b, W['gate'], name="ple_gate", out_dtypes=(F32, F32, BF16),
                     extras=[(x2, 'mn'), (pp, 'mn')],
                     epilogue=lambda acc, xv, pv: (acc,) + (xv + jax.nn.sigmoid(acc) * pv,) * 2)
    sv.update(v0=v0, x1b=x1b, ab=ab, v1=v1, x2b=x2b, pp=pp, z=z, g0=g0, g1=g1)
    return x3, x3b, sv, W


def _layer_bwd(i, grads_in, p_i, W, sm, sv, cosp, sinp, token, early=None, below=None):
    kind, j = MIXER[i % 3], i // 3
    big, small = {}, {}
    dx, dpp_b, dz_b = grads_in
    big['ple_w_proj'] = _mm(p_i, dpp_b, ta=True, name="ple_proj_dw", tn=256, out_sh=True, out_dtypes=(BF16,))
    big['ple_w_gate'] = _row_shards(_mm(sv['x2b'], dz_b, ta=True, name="dw_dd", out_dtypes=(BF16,)))
    ln = dict(tb=True, tm=512, tn=D_MODEL, out_dtypes=(F32, BF16), n_sums=2)
    (dv1, dv1b), (dg1, db1) = _mm(dz_b, W['gate'], name="ple_gate_dx_ln", epilogue=_ln_bwd_epilogue(1.0),
                                  extras=[(dx, 'mn'), (sv['v1'], 'mn'), (sv['g1'], 'n'), (token, 'whole')], **ln)
    big['mlp_w2'] = _row_shards(_mm(sv['ab'], dv1b, ta=True, name="mlp_down_dw", out_dtypes=(BF16,)))
    dub = _mm(dv1b, W['w2'], tb=True, name="mlp_down_dx", out_dtypes=(BF16,), tm=FULL_ROWS,
              extras=[(sv['ab'], 'mn')], epilogue=lambda acc, a: (acc * (2.0 * jnp.sqrt(a.astype(F32))),))
    big['mlp_w1'] = _mm(sv['x1b'], dub, ta=True, name="mlp_up_dw", out_sh=True, out_dtypes=(BF16,))
    order = []
    if early is not None:
        order, big = [(early(big), 'whole')], {}
    (dv0, dv0b), (dg0, db0) = _mm(dub, W['w1'], name="mlp_up_dx_ln", b_sh=True, tk=D_FF, epilogue=_ln_bwd_epilogue(ALPHA),
                                  extras=[(dv1, 'mn'), (sv['v0'], 'mn'), (sv['g0'], 'n')] + order, **ln)
    small['ln_g'] = jnp.stack([dg0[0], dg1[0]])
    small['ln_b'] = jnp.stack([db0[0], db1[0]])
    resid = dict(tb=True, tn=D_MODEL, tm=512 if below else 1024, epilogue=_input_grad_epilogue,
                 extras=[(dv0, 'mn')] + [(a, 'mn') for a in below or ()],
                 out_dtypes=(F32, BF16, BF16) if below else (F32,))
    if kind == 'gla':
        big['gla_w_out'] = _row_shards(_mm(sv['yb'], dv0b, ta=True, name="dw_dd", out_dtypes=(BF16,)))
        dy = _mm(dv0b, W['w_out'], tb=True, name="dx_dd", tn=1024)
        do, dr_b, dng = _gla_post_bwd(dy, sv['o'], sv['pin'], _row(sm['gla_norm_g'][j]))
        dq_b, dk_b, dvv_b, dla = _gla_bwd(sv['pin'], sv['la'], sv['states'], do)
        dzg_b, dbg = _gla_gate_bwd(dla, sv['la'])
        dw_up = _mm(sv['pin'], dzg_b, ta=True, name="gla_gate_dw", M=128, tm=128,
                    a_off=(0, (GLA_IN_PAD - 128) // 128))
        dglr_b = _mm(dzg_b, sv['w_up'], tb=True, name="gla_gate_dx", out_dtypes=(BF16,))
        dpin_b = jnp.concatenate([dq_b, dk_b, dvv_b, dr_b, dglr_b], axis=1)
        dw_in = _mm(sv['xin_b'], dpin_b, ta=True, name="gla_in_dw", tn=640, out_dtypes=(BF16,))
        dxin = _mm(dpin_b, W['w_in'], name="gla_in_dx", tk=GLA_IN_PAD, **resid)
        big['gla_w_in'] = jnp.stack([dw_in[:, GLA_WIN_STEP * qq:GLA_WIN_STEP * qq + GLA_WIN]
                                     for qq in range(N_CHIPS)])
        small.update(gla_w_gate_up=dw_up[:GLA_RANK], gla_b_gate=dbg[0], gla_norm_g=dng[0])
    elif kind == 'mla':
        big['mla_w_out'] = _row_shards(_mm(sv['ob'], dv0b, ta=True, name="dw_dd", out_dtypes=(BF16,)))
        do = _mm(dv0b, W['w_out'], tb=True, name="dx_dd", tn=1024)
        dqr, dkn_b, dvv_b, dkr = _mla_attn_bwd(sv['qr'], sv['knv'], sv['kr'], sv['o'], do)
        dq_b = _mla_qrope_bwd(dqr, cosp, sinp)
        dw_uq = _mm(sv['qn'], dq_b, ta=True, name="mla_up_dw", out_dtypes=(BF16,))
        dqn = _mm(dq_b, W['w_uq'], tb=True, name="mla_up_dx")
        dknv_b = jnp.concatenate([dkn_b, dvv_b], axis=1)
        dw_ukv = _mm(sv['kvn'], dknv_b, ta=True, name="mla_up_dw", out_dtypes=(BF16,))
        dkvn = _mm(dknv_b, W['w_ukv'], tb=True, name="mla_up_dx")
        dcin_b, dgq, dgkv = _mla_norm_bwd(sv['cin'], dqn, dkvn, dkr, sv['gq'], sv['gkv'], cosp, sinp)
        big['mla_w_in'] = _row_shards(_mm(sv['xin_b'], dcin_b, ta=True, name="mla_in_dw", tn=640,
                                          out_dtypes=(BF16,)))
        dxin = _mm(dcin_b, W['w_in'], name="mla_in_dx", tk=MLA_IN_PAD, **resid)
        big['mla_w_uq'] = _col_shards(
            dw_uq.reshape(MLA_QR, MLA_HEADS, MLA_QH)[:, :, :MLA_NOPE + MLA_ROPE].reshape(MLA_QR, -1))
        big['mla_w_ukv'] = _col_shards(
            dw_ukv.reshape(MLA_KVR, 2, MLA_HEADS, 128).transpose(0, 2, 1, 3).reshape(MLA_KVR, -1))
        small.update(mla_q_norm=dgq[0], mla_kv_norm=dgkv[0])
    else:
        big['conv_w_out'] = _row_shards(_mm(sv['yb'], dv0b, ta=True, name="dw_dd", out_dtypes=(BF16,)))
        dy = _mm(dv0b, W['w_out'], tb=True, name="dx_dd", tn=1024)
        db_b, dc_b, du_b, dw8 = _conv_bwd(sv['bcu'], sv['w8'], dy)
        dbcu_b = jnp.concatenate([db_b, dc_b, du_b], axis=1)
        big['conv_w_in'] = _mm(sv['xin_b'], dbcu_b, ta=True, name="conv_in_dw", tn=768, out_sh=True,
                               out_dtypes=(BF16,))
        dxin = _mm(dbcu_b, W['w_in'], name="conv_in_dx", tk=3 * D_MODEL, b_sh=True, **resid)
        small['conv_w'] = dw8[:3]
    return (dxin if below else (dxin,)), big, small


def _rope_tables(positions):
    inv_freq = ROPE_BASE ** (-jnp.arange(0, MLA_ROPE // 2, dtype=F32) * (2.0 / MLA_ROPE))
    ang = positions.astype(F32)[:, None] * inv_freq
    zeros = jnp.zeros((positions.shape[0], 64), F32)
    return (jnp.concatenate([jnp.cos(ang), jnp.cos(ang), zeros], axis=1),
            jnp.concatenate([jnp.sin(ang), jnp.sin(ang), zeros], axis=1))


FIRST_NEEDED = ['gla_w_in']


def _start_gathers(w, q):
    token, started = jnp.zeros(TOKEN, F32), []
    for i in range(DEPTH):
        sh = _layer_shards(w, i, q)
        for k, names in enumerate([list(sh)] if i > 0 else [FIRST_NEEDED, [n for n in sh if n not in FIRST_NEEDED]]):
            ops = [sh[n] for n in names]
            if i == 0 and k == 0:
                ops.append(_pack_small_shards(w))
            tag = "l%d%s" % (i, "ab"[k] if i == 0 else "")
            handle = _gather_start(_place_own(ops, "ag_own_" + tag), token, "ag_start_" + tag)
            token = handle[4]
            started.append((handle, names, tag))
    return started, token


def _pass_on(entry, after):
    handle, names, tag = entry
    _, lands = _gather_wait(handle, after, "ag_wait_" + tag)
    passing = _forward_start(lands, jnp.zeros(TOKEN, F32), "ag_pass_start_" + tag)
    return (passing, names, tag), passing[4]


def _gathered(passed, after):
    passing, names, tag = passed
    got = _forward_wait(passing, after, "ag_pass_wait_" + tag)
    return dict(zip(names, got)), got[-1]


def _local_shard_grad(name, g, q):
    if name == 'gla_w_in':
        return lax.dynamic_slice_in_dim(g, (GLA_SHARD - GLA_WIN_STEP) * q, GLA_SHARD, axis=1)
    if name == 'mla_w_in':
        return g[:, :MLA_IN]
    return g


def kernel(x, p, positions, gla_w_in, gla_w_gate_up, gla_b_gate, gla_norm_g, gla_w_out, mla_w_in, mla_q_norm, mla_kv_norm, mla_w_uq, mla_w_ukv, mla_w_out, conv_w_in, conv_w, conv_w_out, ln_g, ln_b, mlp_w1, mlp_w2, ple_w_gate, ple_w_proj, loss_target, m_gla_w_in, m_gla_w_gate_up, m_gla_b_gate, m_gla_norm_g, m_gla_w_out, m_mla_w_in, m_mla_q_norm, m_mla_kv_norm, m_mla_w_uq, m_mla_w_ukv, m_mla_w_out, m_conv_w_in, m_conv_w, m_conv_w_out, m_ln_g, m_ln_b, m_mlp_w1, m_mlp_w2, m_ple_w_gate, m_ple_w_proj, v_gla_w_in, v_gla_w_gate_up, v_gla_b_gate, v_gla_norm_g, v_gla_w_out, v_mla_w_in, v_mla_q_norm, v_mla_kv_norm, v_mla_w_uq, v_mla_w_ukv, v_mla_w_out, v_conv_w_in, v_conv_w, v_conv_w_out, v_ln_g, v_ln_b, v_mlp_w1, v_mlp_w2, v_ple_w_gate, v_ple_w_proj):
    args = locals()
    w = {n: args[n] for n in WNAMES}
    m = {n: args['m_' + n] for n in WNAMES}
    v = {n: args['v_' + n] for n in WNAMES}
    q = 2 * lax.axis_index("x") + lax.axis_index("y")
    cq = jnp.stack([lax.axis_index("c"), q]).astype(jnp.int32)

    cosp, sinp = _rope_tables(positions[0])
    started, after = _start_gathers(w, q)
    xin, saved, layers, sm = x[0], [], [], None
    xin_b = xin.astype(BF16)
    passed, after = _pass_on(started[0], after)
    for i in range(DEPTH):
        got, last = _gathered(passed, after)
        rest = mid = None
        if i == 0:
            sm = _unpack_small_gathered(last)
            sm['mla_q_norm'], sm['mla_kv_norm'] = w['mla_q_norm'], w['mla_kv_norm']
            rest = lambda after: _layer_weights(_gathered(*_pass_on(started[1], after))[0], 0)
        coming = {}
        if i + 1 < DEPTH:
            def mid(after, entry=started[i + 2], coming=coming):
                coming['passed'], token = _pass_on(entry, after)
                return token
        xin, xin_b, sv, W = _layer_fwd(i, xin, xin_b, p[i, 0], _layer_weights(got, i), sm, cosp, sinp, rest, mid)
        layers.append(W)
        saved.append(sv)
        passed, after = coming.get('passed'), xin
    *grads_in, loss_cols = _loss_head(xin, loss_target[0], saved[-1]['z'], saved[-1]['pp'])
    loss = lax.psum(jnp.sum(loss_cols[0]), ("x", "y", "c"))

    gbig = {n: [None] * WSPEC[n][0][0] for n in BIG}
    gsmall = {n: [None] * _full_shape(n)[0] for n in SMALL}
    pending = []

    def start(grads, i, tag):
        names = list(grads)
        gs = [grads[n] for n in names]
        handle = _reduce_direct_start(gs, tag) if i > 0 else _reduce_scatter_start(gs, cq, jnp.zeros(TOKEN, F32), tag)
        pending.append((handle, names, i, tag))
        return handle[4]

    joining = []

    def finish(above, after, token):
        for entry in [e for e in pending if e[2] > above]:
            pending.remove(entry)
            handle, names, i, tag = entry
            handle = (_reduce_direct_finish if i > 0 else _reduce_scatter_finish)(handle, cq, after, tag)
            joining.append((handle, names, i, tag))
            token = token + handle[4]
        return token

    token = jnp.zeros(TOKEN, F32)
    for i in reversed(range(DEPTH)):
        early = (lambda grads: start(grads, 0, "l0a")) if i == 0 else None
        below = (saved[i - 1]['z'], saved[i - 1]['pp']) if i > 0 else None
        grads_in, big, small = _layer_bwd(i, grads_in, p[i, 0], layers[i], sm, saved[i], cosp, sinp, token, early,
                                          below)
        dx = grads_in[0]
        token = finish(i + 1, dx, start(big, i, "l%d%s" % (i, "b" if i == 0 else "")))
        for n, g in small.items():
            gsmall[n][i if n in ('ln_g', 'ln_b') else i // 3] = g
    finish(-1, token, token)
    for handle, names, i, tag in joining:
        for n, g in zip(names, _join_wait(handle, joining[-1][0][4], "rs_join_wait_" + tag)):
            gbig[n][i if n in COMMON_BIG else i // 3] = _local_shard_grad(n, g, q)
    gsm = _unpack_small(_all_reduce_small(_pack_small({n: jnp.stack(g) for n, g in gsmall.items()})), q)

    grad, delta, new_m, new_v = {}, {}, {}, {}
    for n in BIG:
        update = _adamw_shard_major if n == 'gla_w_in' else _adamw
        grad[n], delta[n], new_m[n], new_v[n] = update(w[n], m[n], v[n], gbig[n], "adamw_" + n)
    total = sum(_size(WSPEC[n][0]) for n in SMALL)
    rows = -(-total // 128 // 8) * 8

    def pack(dct):
        flat = jnp.concatenate([dct[n].reshape(-1) for n in SMALL])
        return jnp.pad(flat, (0, rows * 128 - total), constant_values=1.0).reshape(1, rows, 128)

    res = _adamw(pack(w), pack(m), pack(v), [pack(gsm)[0]], "adamw_small")
    for out, packed in zip((grad, delta, new_m, new_v), res):
        flat, off = packed.reshape(-1), 0
        for n in SMALL:
            sz = _size(WSPEC[n][0])
            out[n] = flat[off:off + sz].reshape(WSPEC[n][0])
            off += sz
    return (loss, dx[None], *[grad[n] for n in WNAMES], *[delta[n] for n in WNAMES],
            *[new_m[n] for n in WNAMES], *[new_v[n] for n in WNAMES])
```

```python
import functools

import numpy as np
import jax
import jax.numpy as jnp
from jax import lax
from jax.experimental import pallas as pl
from jax.experimental.pallas import tpu as pltpu

F32 = jnp.float32
BF16 = jnp.bfloat16
MESH = pl.DeviceIdType.MESH

D_MODEL = 1024
DEPTH = 4
CHUNK = 64
ALPHA = (2 * DEPTH) ** 0.25
LN_EPS = 1e-5
RMS_EPS = 1e-6
D_FF = 4 * D_MODEL
GLA_HEADS = 4
GLA_DK = 128
GLA_DV = 256
GLA_RANK = 16
GLA_TAU = 16.0
GLA_HK = GLA_HEADS * GLA_DK
GLA_HV = GLA_HEADS * GLA_DV
GLA_IN = 2 * GLA_HK + GLA_HV + D_MODEL + GLA_RANK
GLA_IN_PAD = 2 * GLA_HK + GLA_HV + D_MODEL + 128
GLA_SHARD = GLA_IN // 4
GLA_WIN = 896
GLA_WIN_STEP = 768
MLA_HEADS = 8
MLA_NOPE = 128
MLA_ROPE = 64
MLA_V = 128
MLA_QR = 256
MLA_KVR = 256
MLA_IN = MLA_QR + MLA_KVR + MLA_ROPE
MLA_IN_PAD = MLA_QR + MLA_KVR + 128
MLA_QH = 256
ROPE_BASE = 10000.0
ADAM_LR = 0.001
ADAM_B1 = 0.9
ADAM_B2 = 0.999
ADAM_EPS = 1e-08
ADAM_WD = 0.01
ADAM_STEP = 10

VMEM_LIMIT = 48 * 1024 * 1024
FULL_ROWS = 2048
N_CHIPS = 4

WSPEC = {
    'gla_w_in': ((2, 1024, 772), 2), 'gla_w_gate_up': ((2, 16, 128), 2), 'gla_b_gate': ((2, 128), 1),
    'gla_norm_g': ((2, 64), 1), 'gla_w_out': ((2, 256, 1024), 1), 'mla_w_in': ((1, 256, 576), 1),
    'mla_q_norm': ((1, 256), None), 'mla_kv_norm': ((1, 256), None), 'mla_w_uq': ((1, 256, 384), 2),
    'mla_w_ukv': ((1, 256, 512), 2), 'mla_w_out': ((1, 256, 1024), 1), 'conv_w_in': ((1, 1024, 768), 2),
    'conv_w': ((1, 3, 256), 2), 'conv_w_out': ((1, 256, 1024), 1), 'ln_g': ((4, 2, 256), 2),
    'ln_b': ((4, 2, 256), 2), 'mlp_w1': ((4, 1024, 1024), 2), 'mlp_w2': ((4, 1024, 1024), 1),
    'ple_w_gate': ((4, 256, 1024), 1), 'ple_w_proj': ((4, 256, 256), 2),
}
WNAMES = list(WSPEC)
BIG = ['gla_w_in', 'gla_w_out', 'mla_w_in', 'mla_w_uq', 'mla_w_ukv', 'mla_w_out', 'conv_w_in', 'conv_w_out',
       'mlp_w1', 'mlp_w2', 'ple_w_gate', 'ple_w_proj']
SMALL_SHARDED = ['gla_w_gate_up', 'gla_b_gate', 'gla_norm_g', 'conv_w', 'ln_g', 'ln_b']
SMALL = SMALL_SHARDED + ['mla_q_norm', 'mla_kv_norm']
MIXER = ['gla', 'mla', 'conv']
COMMON_BIG = ['mlp_w1', 'mlp_w2', 'ple_w_gate', 'ple_w_proj']


def _size(shape):
    return int(np.prod(shape))


def _full_shape(name):
    shape, ax = WSPEC[name]
    if ax is None:
        return shape
    return tuple(s * N_CHIPS if i == ax else s for i, s in enumerate(shape))


def _cparams(sem=None):
    return pltpu.CompilerParams(dimension_semantics=sem, vmem_limit_bytes=VMEM_LIMIT)


def _out(shape, dtype):
    return pltpu.HBM(shape, dtype)


def _hbm(v):
    return pltpu.with_memory_space_constraint(v, pltpu.HBM)


def _mm(a, b, *, name, ta=False, tb=False, M=None, N=None, K=None, out_dtypes=(F32,), epilogue=None, extras=(),
        tm=1024, tn=512, tk=None, a_off=(0, 0), b_sh=False, out_sh=False, n_sums=0):
    if M is None:
        M = a.shape[1] if ta else a.shape[0]
    if K is None:
        K = a.shape[0] if ta else a.shape[1]
    if b_sh:
        kw, nq = b.shape[1], b.shape[2]
        n_b, k_b = (kw, N_CHIPS * nq) if tb else (N_CHIPS * nq, kw)
        N = n_b if N is None else N
        assert K == k_b
    elif N is None:
        N = b.shape[0] if tb else b.shape[1]
    if tk is None:
        tk = FULL_ROWS if ta else 1024
    tm, tn, tk = min(tm, M), min(tn, N), min(tk, K)
    assert M % tm == 0 and N % tn == 0 and K % tk == 0, (name, M, N, K, tm, tn, tk)
    nk = K // tk
    n_ex, n_out = len(extras), len(out_dtypes)
    assert n_sums == 0 or tn == N

    n_b = N_CHIPS if (b_sh and tb and tk == K) else 1

    def body(a_ref, *rest):
        b_refs, rest = rest[:n_b], rest[n_b:]
        ex_refs, out_refs = rest[:n_ex], rest[n_ex:n_ex + n_out]
        sum_refs = rest[n_ex + n_out:n_ex + n_out + n_sums]
        first_rows = pl.program_id(0) == 0
        dims = ((((0,) if ta else (1,)), ((1,) if tb else (0,))), ((), ()))
        if n_b == 1:
            part = lax.dot_general(a_ref[...].astype(BF16), b_refs[0][...].astype(BF16), dims,
                                   preferred_element_type=F32)
        else:
            part = sum(lax.dot_general(a_ref[:, s * nq:(s + 1) * nq].astype(BF16), b_refs[s][...].astype(BF16), dims,
                                       preferred_element_type=F32) for s in range(n_b))

        def finish(acc):
            res = (acc,) if epilogue is None else epilogue(acc, *[r[...] for r in ex_refs])
            if n_sums:
                res, sums = res

                @pl.when(first_rows)
                def _():
                    for r in sum_refs:
                        r[...] = jnp.zeros(r.shape, F32)

                for r, v in zip(sum_refs, sums):
                    r[...] += jnp.broadcast_to(v, r.shape)
            for r, v in zip(out_refs, res):
                r[...] = v.astype(r.dtype)

        if nk == 1:
            finish(part)
        else:
            acc_ref = rest[-1]
            k = pl.program_id(2)

            @pl.when(k == 0)
            def _():
                acc_ref[...] = part

            @pl.when(k > 0)
            def _():
                acc_ref[...] += part

            @pl.when(k == nk - 1)
            def _():
                finish(acc_ref[...])

    if ta:
        a_spec = pl.BlockSpec((tk, tm), lambda i, j, k: (k + a_off[0], i + a_off[1]))
    else:
        a_spec = pl.BlockSpec((tm, tk), lambda i, j, k: (i + a_off[0], k + a_off[1]))
    once = dict(pipeline_mode=pl.Buffered(1)) if (tn == N and nk == 1) else {}
    if n_b > 1:
        b_specs = [pl.BlockSpec((None, tn, nq), functools.partial(lambda i, j, k, s: (s, j, 0), s=s), **once)
                   for s in range(n_b)]
    elif b_sh and tb:
        assert nq % tk == 0
        per = nq // tk
        b_spec = pl.BlockSpec((None, tn, tk), lambda i, j, k: (k // per, j, k % per), **once)
    elif b_sh:
        assert nq % tn == 0
        per = nq // tn
        b_spec = pl.BlockSpec((None, tk, tn), lambda i, j, k: (j // per, k, j % per), **once)
    elif tb:
        b_spec = pl.BlockSpec((tn, tk), lambda i, j, k: (j, k), **once)
    else:
        b_spec = pl.BlockSpec((tk, tn), lambda i, j, k: (k, j), **once)
    if n_b == 1:
        b_specs = [b_spec]
    ex_specs = []
    for arr, kind in extras:
        if kind == 'mn':
            ex_specs.append(pl.BlockSpec((tm, tn), lambda i, j, k: (i, j)))
        elif kind == 'n':
            ex_specs.append(pl.BlockSpec((1, tn), lambda i, j, k: (0, j)))
        else:
            ex_specs.append(pl.BlockSpec(arr.shape, lambda i, j, k: (0, 0)))
    if out_sh:
        assert (N // N_CHIPS) % tn == 0
        per_o = N // N_CHIPS // tn
        o_spec = pl.BlockSpec((None, tm, tn), lambda i, j, k: (j // per_o, i, j % per_o))
        o_shape = (N_CHIPS, M, N // N_CHIPS)
    else:
        o_spec = pl.BlockSpec((tm, tn), lambda i, j, k: (i, j))
        o_shape = (M, N)
    outs = pl.pallas_call(
        body, name=name, grid=(M // tm, N // tn, nk),
        in_specs=[a_spec] + b_specs + ex_specs,
        out_specs=[o_spec for _ in out_dtypes] + [pl.BlockSpec((8, N), lambda i, j, k: (0, 0))] * n_sums,
        out_shape=[_out(o_shape, d) for d in out_dtypes] + [_out((8, N), F32)] * n_sums,
        scratch_shapes=[pltpu.VMEM((tm, tn), F32)] if nk > 1 else [],
        compiler_params=_cparams(("arbitrary" if n_sums else "parallel", "parallel", "arbitrary")),
    )(a, *[b] * n_b, *[e[0] for e in extras])
    if n_sums:
        return tuple(outs[:n_out]), tuple(outs[n_out:])
    return outs[0] if n_out == 1 else tuple(outs)


def _rowwise(fn, *, name, rows, pars=(), outs=(), accs=(), tm=256):
    S = rows[0][0].shape[0]
    tm = min(tm, S)
    assert S % tm == 0
    n_r, n_p, n_o, n_a = len(rows), len(pars), len(outs), len(accs)

    def body(*refs):
        r_refs, p_refs = refs[:n_r], refs[n_r:n_r + n_p]
        o_refs, a_refs = refs[n_r + n_p:n_r + n_p + n_o], refs[n_r + n_p + n_o:]
        o_vals, a_vals = fn([r[...] for r in r_refs], [p[...] for p in p_refs])
        for r, v in zip(o_refs, o_vals):
            r[...] = v.astype(r.dtype)
        if n_a:
            i = pl.program_id(0)

            @pl.when(i == 0)
            def _():
                for r in a_refs:
                    r[...] = jnp.zeros(r.shape, r.dtype)

            for r, v in zip(a_refs, a_vals):
                r[...] += jnp.broadcast_to(v, r.shape)

    in_specs = [pl.BlockSpec((tm, w), functools.partial(lambda i, o: (i, o), o=off)) for _, w, off in rows]
    in_specs += [pl.BlockSpec(p.shape, functools.partial(lambda i, nd: (0,) * nd, nd=p.ndim)) for p in pars]
    out_specs = [pl.BlockSpec((tm, w), lambda i: (i, 0)) for w, _ in outs]
    out_specs += [pl.BlockSpec((8, w), lambda i: (0, 0)) for w in accs]
    out_shape = [_out((S, w), d) for w, d in outs]
    out_shape += [_out((8, w), F32) for w in accs]
    res = pl.pallas_call(
        body, name=name, grid=(S // tm,), in_specs=in_specs, out_specs=out_specs, out_shape=out_shape,
        compiler_params=_cparams(("arbitrary",)),
    )(*[r[0] for r in rows], *pars)
    return tuple(res)


def _colsum(v):
    return jnp.sum(v, axis=0, keepdims=True)


def _ln_stats(v):
    mu = jnp.mean(v, axis=-1, keepdims=True)
    d = v - mu
    var = jnp.mean(d * d, axis=-1, keepdims=True)
    rstd = lax.rsqrt(var + LN_EPS)
    return d * rstd, rstd


def _ln_fwd_epilogue(h, x, g, b, *unused):
    v = ALPHA * x + h
    xhat, _ = _ln_stats(v)
    y = xhat * g + b
    return y, y, v


def _ln_bwd_epilogue(scale):
    def epilogue(acc, resid, v, g, *unused):
        dy = acc + scale * resid
        xhat, rstd = _ln_stats(v)
        dxh = dy * g
        m1 = jnp.mean(dxh, axis=-1, keepdims=True)
        m2 = jnp.mean(dxh * xhat, axis=-1, keepdims=True)
        dv = rstd * (dxh - m1 - xhat * m2)
        return (dv, dv), (_colsum(dy * xhat), _colsum(dy))
    return epilogue


def _ple_gate_grads(dx3, z, pp):
    s = jax.nn.sigmoid(z)
    return dx3 * s, dx3 * pp * s * (1.0 - s)


def _loss_head(y, t, z, pp):
    def fn(r, p):
        d = r[0] - r[1]
        dy = d * (1.0 / D_MODEL)
        return [dy, *_ple_gate_grads(dy, r[2], r[3])], [_colsum(d * d) * (0.5 / D_MODEL)]
    return _rowwise(fn, name="loss_head", rows=[(a, D_MODEL, 0) for a in (y, t, z, pp)],
                    outs=[(D_MODEL, F32), (D_MODEL, BF16), (D_MODEL, BF16)], accs=[D_MODEL])


def _input_grad_epilogue(acc, dv, *below):
    dx = acc + ALPHA * dv
    return (dx, *_ple_gate_grads(dx, *below)) if below else (dx,)


N_LEVELS = 6
GLA_STEP = 2


def _gla_consts():
    C = CHUNK
    A = np.zeros((N_LEVELS + 3, C, C), np.float32)
    masks = np.zeros((N_LEVELS + 1, C, C), np.float32)
    r = np.arange(C)[:, None]
    u = np.arange(C)[None, :]
    for l in range(N_LEVELS):
        half = C >> (l + 1)
        mid = (r // (2 * half)) * (2 * half) + half - 1
        A[l] = np.where(r > mid, (u > mid) & (u <= r), (u > r) & (u <= mid))
        masks[l] = ((r // (2 * half)) == (u // (2 * half))) & (((r // half) % 2) != ((u // half) % 2))
    masks[N_LEVELS] = (r == u)
    A[N_LEVELS] = (u <= r)
    A[N_LEVELS + 1] = (u > r)
    A[N_LEVELS + 2] = 1.0
    A = A.reshape(-1, C)
    return A, np.ascontiguousarray(A.T), masks


def _split3(v):
    hi = v.astype(BF16)
    r1 = v - hi.astype(F32)
    mid = r1.astype(BF16)
    lo = (r1 - mid.astype(F32)).astype(BF16)
    return hi, mid, lo


def _dot_exact01(a01, v):
    hi, mid, lo = _split3(v)
    f = lambda p: jnp.dot(a01, p, preferred_element_type=F32)
    return f(hi) + f(mid) + f(lo)


def _nt(a, b):
    return lax.dot_general(a, b, (((1,), (1,)), ((), ())), preferred_element_type=F32)


def _tn(a, b):
    return lax.dot_general(a, b, (((0,), (0,)), ((), ())), preferred_element_type=F32)


def _nn(a, b):
    return jnp.dot(a, b, preferred_element_type=F32)


def _gla_chunk_terms(q, k, E, m_ref):
    C = CHUNK
    scores = m_ref[N_LEVELS] * _nt(q.astype(BF16), k.astype(BF16))
    qes, kes = [], []
    for l in range(N_LEVELS):
        El = E[l * C:(l + 1) * C]
        qe, ke = (q * El).astype(BF16), (k * El).astype(BF16)
        qes.append(qe)
        kes.append(ke)
        scores = scores + m_ref[l] * _nt(qe, ke)
    return qes, kes, scores


def _head(v, h, w):
    return v[:, h * w:(h + 1) * w]


def _gla_fwd(pin, la):
    S = pin.shape[0]
    NC = S // CHUNK
    C, R = CHUNK, CHUNK * GLA_STEP
    A, _, masks = _gla_consts()

    def body(q_ref, k_ref, v_ref, la_ref, a_ref, m_ref, o_ref, st_ref, state):
        @pl.when(pl.program_id(0) == 0)
        def _():
            state[...] = jnp.zeros(state.shape, F32)

        for ci in range(GLA_STEP):
            rows = pl.ds(ci * C, C)
            E_all = jnp.exp(_dot_exact01(a_ref[...], la_ref[rows, :]))
            q_all = q_ref[rows, :] * (GLA_DK ** -0.5)
            k_all, v_all = k_ref[rows, :], v_ref[rows, :]
            outs = []
            for h in range(GLA_HEADS):
                q, k, E = _head(q_all, h, GLA_DK), _head(k_all, h, GLA_DK), _head(E_all, h, GLA_DK)
                _, _, scores = _gla_chunk_terms(q, k, E, m_ref)
                Eq, Ek, Ee = E[6 * C:7 * C], E[7 * C:8 * C], E[8 * C:9 * C]
                st = state[h]
                st_ref[h, ci] = st
                vb = _head(v_all, h, GLA_DV).astype(BF16)
                outs.append(_nn(scores.astype(BF16), vb) + _nt((q * Eq).astype(BF16), st.astype(BF16)))
                state[h] = st * jnp.concatenate([Ee] * (GLA_DV // C), axis=0) + _tn(vb, (k * Ek).astype(BF16))
            o_ref[rows, :] = jnp.concatenate(outs, axis=1)

    return pl.pallas_call(
        body, name="gla_fwd", grid=(NC // GLA_STEP,),
        in_specs=[pl.BlockSpec((R, GLA_HK), lambda c: (c, 0)),
                  pl.BlockSpec((R, GLA_HK), lambda c: (c, 1)),
                  pl.BlockSpec((R, GLA_HV), lambda c: (c, 2 * GLA_HK // GLA_HV)),
                  pl.BlockSpec((R, GLA_HK), lambda c: (c, 0)),
                  pl.BlockSpec(A.shape, lambda c: (0, 0)),
                  pl.BlockSpec(masks.shape, lambda c: (0, 0, 0))],
        out_specs=[pl.BlockSpec((R, GLA_HV), lambda c: (c, 0)),
                   pl.BlockSpec((GLA_HEADS, GLA_STEP, GLA_DV, GLA_DK), lambda c: (0, c, 0, 0))],
        out_shape=[_out((S, GLA_HV), F32), _out((GLA_HEADS, NC, GLA_DV, GLA_DK), F32)],
        scratch_shapes=[pltpu.VMEM((GLA_HEADS, GLA_DV, GLA_DK), F32)],
        compiler_params=_cparams(("arbitrary",)),
    )(pin, pin, pin, la, jnp.asarray(A, BF16), jnp.asarray(masks))


def _gla_bwd(pin, la, states, do):
    S = pin.shape[0]
    NC = S // CHUNK
    C, R = CHUNK, CHUNK * GLA_STEP
    A, AT, masks = _gla_consts()
    scale = GLA_DK ** -0.5

    def body(q_ref, k_ref, v_ref, la_ref, st_ref, do_ref, a_ref, at_ref, m_ref,
             dq_ref, dk_ref, dv_ref, dla_ref, dstate):
        @pl.when(pl.program_id(0) == 0)
        def _():
            dstate[...] = jnp.zeros(dstate.shape, F32)

        for ci in reversed(range(GLA_STEP)):
            one_chunk(ci, pl.ds(ci * C, C), q_ref, k_ref, v_ref, la_ref, st_ref, do_ref, a_ref, at_ref, m_ref,
                      dq_ref, dk_ref, dv_ref, dla_ref, dstate)

    def one_chunk(ci, rows, q_ref, k_ref, v_ref, la_ref, st_ref, do_ref, a_ref, at_ref, m_ref,
                  dq_ref, dk_ref, dv_ref, dla_ref, dstate):
        E_all = jnp.exp(_dot_exact01(a_ref[...], la_ref[rows, :]))
        q_all = q_ref[rows, :] * scale
        k_all, v_all, do_all = k_ref[rows, :], v_ref[rows, :], do_ref[rows, :]
        dqs, dks, dvs, dXs = [], [], [], []
        for h in range(GLA_HEADS):
            q, k, E = _head(q_all, h, GLA_DK), _head(k_all, h, GLA_DK), _head(E_all, h, GLA_DK)
            qes, kes, scores = _gla_chunk_terms(q, k, E, m_ref)
            Eq, Ek, Ee = E[6 * C:7 * C], E[7 * C:8 * C], E[8 * C:9 * C]
            st, dst = st_ref[h, ci], dstate[h]
            dob, vb = _head(do_all, h, GLA_DV).astype(BF16), _head(v_all, h, GLA_DV).astype(BF16)
            dstb = dst.astype(BF16)
            qEq, kEk = (q * Eq).astype(BF16), (k * Ek).astype(BF16)
            dsc = _nt(dob, vb)
            dvs.append(_tn(scores.astype(BF16), dob) + _nt(kEk, dstb))
            dqEq = _nn(dob, st.astype(BF16))
            dkEk = _nn(vb, dstb)
            Gd = (m_ref[N_LEVELS] * dsc).astype(BF16)
            dq = _nn(Gd, k.astype(BF16)) + dqEq * Eq
            dk = _tn(Gd, q.astype(BF16)) + dkEk * Ek
            dX = []
            for l in range(N_LEVELS):
                El = E[l * C:(l + 1) * C]
                G = (m_ref[l] * dsc).astype(BF16)
                dqe, dke = _nn(G, kes[l]), _tn(G, qes[l])
                dq = dq + dqe * El
                dk = dk + dke * El
                dX.append((dqe * q + dke * k) * El)
            dX.append(dqEq * q * Eq)
            dX.append(dkEk * k * Ek)
            prod = dst * st
            dEe = prod[0:C]
            for i in range(1, GLA_DV // C):
                dEe = dEe + prod[i * C:(i + 1) * C]
            dX.append(dEe * Ee)
            dXs.append(jnp.concatenate(dX, axis=0))
            dqs.append(dq * scale)
            dks.append(dk)
            dstate[h] = dst * jnp.concatenate([Ee] * (GLA_DV // C), axis=0) + _tn(dob, qEq)
        dla_ref[rows, :] = _dot_exact01(at_ref[...], jnp.concatenate(dXs, axis=1))
        dq_ref[rows, :] = jnp.concatenate(dqs, axis=1).astype(dq_ref.dtype)
        dk_ref[rows, :] = jnp.concatenate(dks, axis=1).astype(dk_ref.dtype)
        dv_ref[rows, :] = jnp.concatenate(dvs, axis=1).astype(dv_ref.dtype)

    rc = lambda c: NC // GLA_STEP - 1 - c
    return pl.pallas_call(
        body, name="gla_bwd", grid=(NC // GLA_STEP,),
        in_specs=[pl.BlockSpec((R, GLA_HK), lambda c: (rc(c), 0)),
                  pl.BlockSpec((R, GLA_HK), lambda c: (rc(c), 1)),
                  pl.BlockSpec((R, GLA_HV), lambda c: (rc(c), 2 * GLA_HK // GLA_HV)),
                  pl.BlockSpec((R, GLA_HK), lambda c: (rc(c), 0)),
                  pl.BlockSpec((GLA_HEADS, GLA_STEP, GLA_DV, GLA_DK), lambda c: (0, rc(c), 0, 0)),
                  pl.BlockSpec((R, GLA_HV), lambda c: (rc(c), 0)),
                  pl.BlockSpec(A.shape, lambda c: (0, 0)),
                  pl.BlockSpec(AT.shape, lambda c: (0, 0)),
                  pl.BlockSpec(masks.shape, lambda c: (0, 0, 0))],
        out_specs=[pl.BlockSpec((R, GLA_HK), lambda c: (rc(c), 0)),
                   pl.BlockSpec((R, GLA_HK), lambda c: (rc(c), 0)),
                   pl.BlockSpec((R, GLA_HV), lambda c: (rc(c), 0)),
                   pl.BlockSpec((R, GLA_HK), lambda c: (rc(c), 0))],
        out_shape=[_out((S, GLA_HK), BF16), _out((S, GLA_HK), BF16), _out((S, GLA_HV), BF16),
                   _out((S, GLA_HK), F32)],
        scratch_shapes=[pltpu.VMEM((GLA_HEADS, GLA_DV, GLA_DK), F32)],
        compiler_params=_cparams(("arbitrary",)),
    )(pin, pin, pin, la, states, do, jnp.asarray(A, BF16), jnp.asarray(AT, BF16), jnp.asarray(masks))


def _gla_post_fwd(o, pin, g):
    def fn(r, p):
        ov, rv = r
        ys = []
        for h in range(GLA_HEADS):
            oh = ov[:, h * GLA_DV:(h + 1) * GLA_DV]
            rh = rv[:, h * GLA_DV:(h + 1) * GLA_DV]
            rs = lax.rsqrt(jnp.mean(oh * oh, axis=-1, keepdims=True) + RMS_EPS)
            ys.append(oh * rs * p[0] * (rh * jax.nn.sigmoid(rh)))
        return [jnp.concatenate(ys, axis=1)], []
    return _rowwise(fn, name="gla_post_fwd", rows=[(o, GLA_HV, 0), (pin, GLA_HV, (2 * GLA_HK + GLA_HV) // GLA_HV)],
                    pars=[g], outs=[(GLA_HV, BF16)])[0]


def _gla_post_bwd(dy, o, pin, g):
    def fn(r, p):
        dyv, ov, rv = r
        dos, drs, dg = [], [], 0.0
        for h in range(GLA_HEADS):
            sl = slice(h * GLA_DV, (h + 1) * GLA_DV)
            oh, rh, dyh = ov[:, sl], rv[:, sl], dyv[:, sl]
            rs = lax.rsqrt(jnp.mean(oh * oh, axis=-1, keepdims=True) + RMS_EPS)
            xh = oh * rs
            sg = jax.nn.sigmoid(rh)
            d_on = dyh * (rh * sg)
            drs.append(dyh * (xh * p[0]) * (sg * (1.0 + rh * (1.0 - sg))))
            dg = dg + _colsum(d_on * xh)
            dxh = d_on * p[0]
            dos.append(rs * (dxh - xh * jnp.mean(dxh * xh, axis=-1, keepdims=True)))
        return [jnp.concatenate(dos, axis=1), jnp.concatenate(drs, axis=1)], [dg]
    return _rowwise(fn, name="gla_post_bwd",
                    rows=[(dy, GLA_HV, 0), (o, GLA_HV, 0), (pin, GLA_HV, (2 * GLA_HK + GLA_HV) // GLA_HV)],
                    pars=[g], outs=[(GLA_HV, F32), (GLA_HV, BF16)], accs=[GLA_DV])


def _gla_gate_bwd(dla, la):
    def fn(r, p):
        dz = r[0] * (1.0 / GLA_TAU) * (1.0 - jnp.exp(GLA_TAU * r[1]))
        return [dz], [_colsum(dz)]
    return _rowwise(fn, name="gla_gate_bwd", rows=[(dla, GLA_HK, 0), (la, GLA_HK, 0)], outs=[(GLA_HK, BF16)],
                    accs=[GLA_HK])


def _log_sigmoid(z):
    return jnp.minimum(z, 0.0) - jnp.log(1.0 + jnp.exp(-jnp.abs(z)))


def _rot_half(v):
    lane = lax.broadcasted_iota(jnp.int32, v.shape, 1)
    return jnp.where(lane < 32, -pltpu.roll(v, 96, 1), jnp.where(lane < 64, pltpu.roll(v, 32, 1), 0.0))


def _rms(v):
    rs = lax.rsqrt(jnp.mean(v * v, axis=-1, keepdims=True) + RMS_EPS)
    return v * rs, rs


def _mla_norm_fwd(cin, gq, gkv, cosp, sinp):
    def fn(r, p):
        cv, cs, sn = r
        qn, _ = _rms(cv[:, :MLA_QR])
        kvn, _ = _rms(cv[:, MLA_QR:MLA_QR + MLA_KVR])
        kr = cv[:, MLA_QR + MLA_KVR:]
        return [qn * p[0], kvn * p[1], kr * cs + _rot_half(kr) * sn], []
    return _rowwise(fn, name="mla_norm_fwd", rows=[(cin, MLA_IN_PAD, 0), (cosp, 128, 0), (sinp, 128, 0)],
                    pars=[gq, gkv], outs=[(MLA_QR, BF16), (MLA_KVR, BF16), (128, BF16)])


def _mla_qrope_fwd(q, cosp, sinp):
    scale = (MLA_NOPE + MLA_ROPE) ** -0.5

    def fn(r, p):
        qv, cs, sn = r
        parts = []
        for h in range(MLA_HEADS):
            parts.append(qv[:, h * MLA_QH:h * MLA_QH + 128] * scale)
            rp = qv[:, h * MLA_QH + 128:(h + 1) * MLA_QH]
            parts.append((rp * cs + _rot_half(rp) * sn) * scale)
        return [jnp.concatenate(parts, axis=1)], []
    W = MLA_HEADS * MLA_QH
    return _rowwise(fn, name="mla_qrope_fwd", rows=[(q, W, 0), (cosp, 128, 0), (sinp, 128, 0)],
                    outs=[(W, BF16)])[0]


def _mla_qrope_bwd(dq, cosp, sinp):
    scale = (MLA_NOPE + MLA_ROPE) ** -0.5

    def fn(r, p):
        dv, cs, sn = r
        parts = []
        for h in range(MLA_HEADS):
            parts.append(dv[:, h * MLA_QH:h * MLA_QH + 128] * scale)
            rp = dv[:, h * MLA_QH + 128:(h + 1) * MLA_QH]
            parts.append((rp * cs - _rot_half(rp) * sn) * scale)
        return [jnp.concatenate(parts, axis=1)], []
    W = MLA_HEADS * MLA_QH
    return _rowwise(fn, name="mla_qrope_bwd", rows=[(dq, W, 0), (cosp, 128, 0), (sinp, 128, 0)],
                    outs=[(W, BF16)])[0]


def _mla_norm_bwd(cin, dqn, dkvn, dkr, gq, gkv, cosp, sinp):
    def fn(r, p):
        cv, dq_, dkv_, dkr_, cs, sn = r
        outs, accs = [], []
        for (lo, hi), dn, g in (((0, MLA_QR), dq_, p[0]), ((MLA_QR, MLA_QR + MLA_KVR), dkv_, p[1])):
            xh, rs = _rms(cv[:, lo:hi])
            dxh = dn * g
            outs.append(rs * (dxh - xh * jnp.mean(dxh * xh, axis=-1, keepdims=True)))
            accs.append(_colsum(dn * xh))
        dk = dkr_[:, 0:128]
        for h in range(1, MLA_HEADS):
            dk = dk + dkr_[:, h * 128:(h + 1) * 128]
        outs.append(dk * cs - _rot_half(dk) * sn)
        return [jnp.concatenate(outs, axis=1)], accs
    return _rowwise(fn, name="mla_norm_bwd",
                    rows=[(cin, MLA_IN_PAD, 0), (dqn, MLA_QR, 0), (dkvn, MLA_KVR, 0), (dkr, MLA_HEADS * 128, 0),
                          (cosp, 128, 0), (sinp, 128, 0)],
                    pars=[gq, gkv], outs=[(MLA_IN_PAD, BF16)], accs=[MLA_QR, MLA_KVR])


def _mla_probs(q, k, i, tq):
    s = _nt(q, k)
    row = (i * tq + lax.broadcasted_iota(jnp.int32, s.shape, 0)) // CHUNK
    col = lax.broadcasted_iota(jnp.int32, s.shape, 1) // CHUNK
    s = jnp.where(col <= row, s, -jnp.inf)
    e = jnp.exp(s - jnp.max(s, axis=-1, keepdims=True))
    return e / jnp.sum(e, axis=-1, keepdims=True)


def _mla_attn_fwd(qr, knv, kr, tq=256):
    S = qr.shape[0]
    tq = min(tq, S)

    def body(q_ref, kn_ref, v_ref, kr_ref, o_ref, k_cat):
        k_cat[:, :128] = kn_ref[...]
        k_cat[:, 128:] = kr_ref[...]
        for i in range(S // tq):
            rows, keys = pl.ds(i * tq, tq), pl.ds(0, (i + 1) * tq)
            pr = _mla_probs(q_ref[rows, :], k_cat[keys, :], i, tq)
            o_ref[rows, :] = _nn(pr.astype(BF16), v_ref[keys, :])

    return pl.pallas_call(
        body, name="mla_attn_fwd", grid=(MLA_HEADS,),
        in_specs=[pl.BlockSpec((S, MLA_QH), lambda h: (0, h)),
                  pl.BlockSpec((S, 128), lambda h: (0, h)),
                  pl.BlockSpec((S, 128), lambda h: (0, MLA_HEADS + h)),
                  pl.BlockSpec((S, 128), lambda h: (0, 0))],
        out_specs=pl.BlockSpec((S, 128), lambda h: (0, h)),
        out_shape=_out((S, MLA_HEADS * MLA_V), F32),
        scratch_shapes=[pltpu.VMEM((S, MLA_QH), BF16)],
        compiler_params=_cparams(("parallel",)),
    )(qr, knv, knv, kr)


def _mla_attn_bwd(qr, knv, kr, o, do, tq=256):
    S = qr.shape[0]
    tq = min(tq, S)
    W = MLA_HEADS * 128

    def body(q_ref, kn_ref, v_ref, kr_ref, o_ref, do_ref, dq_ref, dkn_ref, dv_ref, dkr_ref, k_cat, dk_acc, dv_acc):
        k_cat[:, :128] = kn_ref[...]
        k_cat[:, 128:] = kr_ref[...]
        dk_acc[...] = jnp.zeros(dk_acc.shape, F32)
        dv_acc[...] = jnp.zeros(dv_acc.shape, F32)
        for i in range(S // tq):
            rows, keys = pl.ds(i * tq, tq), pl.ds(0, (i + 1) * tq)
            q, k, v = q_ref[rows, :], k_cat[keys, :], v_ref[keys, :]
            pr = _mla_probs(q, k, i, tq)
            dov = do_ref[rows, :]
            delta = jnp.sum(dov * o_ref[rows, :], axis=-1, keepdims=True)
            dob = dov.astype(BF16)
            ds = (pr * (_nt(dob, v) - delta)).astype(BF16)
            dq_ref[rows, :] = _nn(ds, k)
            dk_acc[keys, :] += _tn(ds, q)
            dv_acc[keys, :] += _tn(pr.astype(BF16), dob)
        dkn_ref[...] = dk_acc[:, :128].astype(dkn_ref.dtype)
        dkr_ref[...] = dk_acc[:, 128:]
        dv_ref[...] = dv_acc[...].astype(dv_ref.dtype)

    head = lambda w: pl.BlockSpec((S, w), lambda h: (0, h))
    return pl.pallas_call(
        body, name="mla_attn_bwd", grid=(MLA_HEADS,),
        in_specs=[head(MLA_QH), head(128), pl.BlockSpec((S, 128), lambda h: (0, MLA_HEADS + h)),
                  pl.BlockSpec((S, 128), lambda h: (0, 0)), head(128), head(128)],
        out_specs=[head(MLA_QH), head(128), head(128), head(128)],
        out_shape=[_out((S, MLA_HEADS * MLA_QH), F32), _out((S, W), BF16), _out((S, W), BF16), _out((S, W), F32)],
        scratch_shapes=[pltpu.VMEM((S, MLA_QH), BF16), pltpu.VMEM((S, MLA_QH), F32), pltpu.VMEM((S, 128), F32)],
        compiler_params=_cparams(("parallel",)),
    )(qr, knv, knv, kr, o, do)


CONV_TILE = 256


def _shift_down(v, n):
    row = lax.broadcasted_iota(jnp.int32, v.shape, 0)
    return jnp.where(row >= n, pltpu.roll(v, n, 0), 0.0)


def _shift_up(v, n):
    S = v.shape[0]
    row = lax.broadcasted_iota(jnp.int32, v.shape, 0)
    return jnp.where(row < S - n, pltpu.roll(v, S - n, 0), 0.0)


def _conv_specs(S, n_extra_cols):
    nt = D_MODEL // CONV_TILE
    specs = [pl.BlockSpec((S, CONV_TILE), functools.partial(lambda j, o: (0, o + j), o=part * nt))
             for part in range(3)]
    specs.append(pl.BlockSpec((8, CONV_TILE), lambda j: (0, j)))
    specs += [pl.BlockSpec((S, CONV_TILE), lambda j: (0, j)) for _ in range(n_extra_cols)]
    return specs


def _conv_fwd(bcu, w8):
    S = bcu.shape[0]

    def body(b_ref, c_ref, u_ref, w_ref, y_ref):
        cu = c_ref[...] * u_ref[...]
        z = w_ref[2:3, :] * cu + w_ref[1:2, :] * _shift_down(cu, 1) + w_ref[0:1, :] * _shift_down(cu, 2)
        y_ref[...] = (b_ref[...] * z).astype(y_ref.dtype)

    return pl.pallas_call(
        body, name="conv_fwd", grid=(D_MODEL // CONV_TILE,), in_specs=_conv_specs(S, 0),
        out_specs=pl.BlockSpec((S, CONV_TILE), lambda j: (0, j)),
        out_shape=_out((S, D_MODEL), BF16),
        compiler_params=_cparams(("parallel",)),
    )(bcu, bcu, bcu, w8)


def _conv_bwd(bcu, w8, dy):
    S = bcu.shape[0]

    def body(b_ref, c_ref, u_ref, w_ref, dy_ref, db_ref, dc_ref, du_ref, dw_ref):
        b, c, u, dyv = b_ref[...], c_ref[...], u_ref[...], dy_ref[...]
        w0, w1, w2 = w_ref[0:1, :], w_ref[1:2, :], w_ref[2:3, :]
        cu = c * u
        cu1, cu2 = _shift_down(cu, 1), _shift_down(cu, 2)
        z = w2 * cu + w1 * cu1 + w0 * cu2
        dz = dyv * b
        db_ref[...] = (dyv * z).astype(db_ref.dtype)
        dcu = w2 * dz + w1 * _shift_up(dz, 1) + w0 * _shift_up(dz, 2)
        dc_ref[...] = (dcu * u).astype(dc_ref.dtype)
        du_ref[...] = (dcu * c).astype(du_ref.dtype)
        dw_ref[...] = jnp.zeros(dw_ref.shape, F32)
        dw_ref[0:1, :] = _colsum(dz * cu2)
        dw_ref[1:2, :] = _colsum(dz * cu1)
        dw_ref[2:3, :] = _colsum(dz * cu)

    col = pl.BlockSpec((S, CONV_TILE), lambda j: (0, j))
    return pl.pallas_call(
        body, name="conv_bwd", grid=(D_MODEL // CONV_TILE,), in_specs=_conv_specs(S, 1),
        out_specs=[col, col, col, pl.BlockSpec((8, CONV_TILE), lambda j: (0, j))],
        out_shape=[_out((S, D_MODEL), BF16)] * 3 + [_out((8, D_MODEL), F32)],
        compiler_params=_cparams(("parallel",)),
    )(bcu, bcu, bcu, w8, dy)


def _adamw_update(w, g, m, v):
    nm = ADAM_B1 * m + (1.0 - ADAM_B1) * g
    nv = ADAM_B2 * v + (1.0 - ADAM_B2) * jnp.square(g)
    m_hat = nm / (1.0 - ADAM_B1 ** ADAM_STEP)
    v_hat = nv / (1.0 - ADAM_B2 ** ADAM_STEP)
    return -ADAM_LR * (m_hat / (jnp.sqrt(v_hat) + ADAM_EPS) + ADAM_WD * w), nm, nv


def _adamw_shard_major(w, m, v, gs, name):
    view = lambda a: jnp.transpose(a, (2, 0, 1))
    g = jnp.stack([x.T for x in gs], axis=1)
    n, L, k = g.shape
    rows = n // 4
    assert n % 4 == 0

    def body(w_ref, m_ref, v_ref, g_ref, go_ref, d_ref, nm_ref, nv_ref):
        gv = g_ref[...]
        d_ref[...], nm_ref[...], nv_ref[...] = _adamw_update(w_ref[...], gv, m_ref[...], v_ref[...])
        go_ref[...] = gv

    spec = pl.BlockSpec((rows, L, k), lambda i: (i, 0, 0))
    outs = pl.pallas_call(
        body, name=name, grid=(4,), in_specs=[spec] * 4, out_specs=[spec] * 4,
        out_shape=[jax.ShapeDtypeStruct((n, L, k), F32)] * 4,
        compiler_params=_cparams(("parallel",)),
    )(view(w), view(m), view(v), g)
    return [jnp.transpose(o, (1, 2, 0)) for o in outs]


def _adamw_small(ws, gs, ms, vs):
    n = len(ws)

    def body(*refs):
        ins, outs = refs[:4 * n], refs[4 * n:]
        for t in range(n):
            w_ref, g_ref, m_ref, v_ref = (ins[k * n + t] for k in range(4))
            gv = g_ref[...]
            outs[4 * t][...] = gv
            outs[4 * t + 1][...], outs[4 * t + 2][...], outs[4 * t + 3][...] = _adamw_update(
                w_ref[...], gv, m_ref[...], v_ref[...])

    return pl.pallas_call(
        body, name="adamw_small",
        out_shape=[jax.ShapeDtypeStruct(a.shape, F32) for a in ws for _ in range(4)],
    )(*ws, *gs, *ms, *vs)


def _adamw(w, m, v, gs, name):
    L, R, Cn = w.shape
    assert len(gs) == L
    tr = R if R <= 256 else 256
    assert R % tr == 0

    def body(w_ref, m_ref, v_ref, *rest):
        g_refs, (go_ref, d_ref, nm_ref, nv_ref) = rest[:L], rest[L:]
        layer = pl.program_id(0)
        gv = g_refs[0][...]
        for k in range(1, L):
            gv = jnp.where(layer == k, g_refs[k][...], gv)
        d_ref[...], nm_ref[...], nv_ref[...] = _adamw_update(w_ref[...], gv, m_ref[...], v_ref[...])
        go_ref[...] = gv

    spec = pl.BlockSpec((None, tr, Cn), lambda l, i: (l, i, 0))
    g_specs = [pl.BlockSpec((tr, Cn), functools.partial(lambda l, i, k: (jnp.where(l == k, i, 0), 0), k=k))
               for k in range(L)]
    return pl.pallas_call(
        body, name=name, grid=(L, R // tr), in_specs=[spec] * 3 + g_specs, out_specs=[spec] * 4,
        out_shape=[jax.ShapeDtypeStruct((L, R, Cn), F32)] * 4,
        compiler_params=_cparams(("arbitrary", "arbitrary")),
    )(w, m, v, *gs)


HBM_SPEC = pl.BlockSpec(memory_space=pltpu.HBM)


def _place():
    return lax.axis_index("x"), lax.axis_index("y"), lax.axis_index("c")


def _other_chips(x, y):
    return [(1 - x, y), (x, 1 - y), (1 - x, 1 - y)]


SEM_SPEC = pl.BlockSpec(memory_space=pltpu.SEMAPHORE)
ANY_SPEC = pl.BlockSpec(memory_space=pl.ANY)
VMEM_SPEC = pl.BlockSpec(memory_space=pltpu.VMEM)
EFFECT = pltpu.SideEffectType.DATAFLOW_SIDE_EFFECTING
TOKEN = (8, 128)


def _ici_start(srcs, lands, after, copies, name, per_src=3):
    n, nl = len(srcs), len(lands)

    def body(*refs):
        src_refs, land_refs = refs[:n], refs[n:n + nl]
        send_sems, recv_sems, token = refs[n + nl + 1], refs[n + nl + 2], refs[-1]
        x, y, c = _place()
        for k, src, dst, to in copies(src_refs, land_refs, x, y, c):
            pltpu.make_async_remote_copy(src_ref=src, dst_ref=dst, send_sem=send_sems.at[k], recv_sem=recv_sems.at[k],
                                         device_id=to, device_id_type=MESH).start()
        token[...] = jnp.zeros(TOKEN, F32)

    n_copies = per_src * max(n, nl if n == 0 else 0)
    res = pl.pallas_call(
        body, name=name,
        out_shape=(pltpu.SemaphoreType.DMA((n_copies,)), pltpu.SemaphoreType.DMA((n_copies,)),
                   *[pltpu.HBM(s.shape, s.dtype) for s in srcs], *[pltpu.HBM(l.shape, l.dtype) for l in lands],
                   jax.ShapeDtypeStruct(TOKEN, F32)),
        in_specs=[HBM_SPEC] * (n + nl) + [ANY_SPEC],
        out_specs=(SEM_SPEC, SEM_SPEC, *[HBM_SPEC] * (n + nl), VMEM_SPEC),
        input_output_aliases={t: 2 + t for t in range(n + nl)},
        compiler_params=pltpu.CompilerParams(has_side_effects=EFFECT),
    )(*[_hbm(s) for s in srcs], *[_hbm(l) for l in lands], after)
    return res[0], res[1], list(res[2:2 + n]), list(res[2 + n:2 + n + nl]), res[-1]


def _ici_wait(handle, after, copies, name):
    send_sems, recv_sems, srcs, lands, _ = handle
    n, nl = len(srcs), len(lands)

    def body(*refs):
        src_refs, land_refs = refs[:n], refs[n:n + nl]
        send_s, recv_s = refs[n + nl], refs[n + nl + 1]
        x, y, c = _place()
        for k, src, dst, to in copies(src_refs, land_refs, x, y, c):
            cp = pltpu.make_async_remote_copy(src_ref=src, dst_ref=dst, send_sem=send_s.at[k], recv_sem=recv_s.at[k],
                                              device_id=to, device_id_type=MESH)
            cp.wait_send()
            cp.wait_recv()

    res = pl.pallas_call(
        body, name=name,
        out_shape=(*[pltpu.HBM(s.shape, s.dtype) for s in srcs], *[pltpu.HBM(l.shape, l.dtype) for l in lands]),
        in_specs=[HBM_SPEC] * (n + nl) + [SEM_SPEC, SEM_SPEC, ANY_SPEC],
        out_specs=tuple([HBM_SPEC] * (n + nl)),
        input_output_aliases={t: t for t in range(n + nl)},
        compiler_params=pltpu.CompilerParams(has_side_effects=EFFECT),
    )(*srcs, *lands, send_sems, recv_sems, after)
    return list(res[:n]), list(res[n:])


def _gather_copies(halves, arriving):
    def copies(src_refs, land_refs, x, y, c):
        q = 2 * x + y
        out = []
        for t, H in enumerate(halves):
            mine = land_refs[t].at[q, pl.ds(c * H, H), :]
            for j, (cx, cy) in enumerate(_other_chips(x, y)):
                theirs = land_refs[t].at[2 * cx + cy, pl.ds(c * H, H), :]
                out.append((3 * t + j, mine, theirs if arriving else mine, (cx, cy, c)))
        return out
    return copies


def _place_own(ops, name):
    n = len(ops)
    kinds = sorted({(o.shape, str(o.dtype)) for o in ops})
    kind_of = [kinds.index((o.shape, str(o.dtype))) for o in ops]

    def body(*refs):
        in_refs, out_refs = refs[:n], refs[n:2 * n]
        rd_sems, wr_sems, bufs = refs[2 * n], refs[2 * n + 1], refs[2 * n + 2:]
        x, y, _ = _place()
        used = [0] * len(kinds)
        slot, busy = [], {}
        for t in range(n):
            slot.append((kind_of[t], used[kind_of[t]] % 2))
            used[kind_of[t]] += 1
        rd = lambda t: pltpu.make_async_copy(in_refs[t], bufs[slot[t][0]].at[slot[t][1]], rd_sems.at[t])
        wr = lambda t: pltpu.make_async_copy(bufs[slot[t][0]].at[slot[t][1]], out_refs[t].at[2 * x + y],
                                             wr_sems.at[t])
        rd(0).start()
        for t in range(n):
            rd(t).wait()
            wr(t).start()
            busy[slot[t]] = t
            if t + 1 < n:
                if slot[t + 1] in busy:
                    wr(busy.pop(slot[t + 1])).wait()
                rd(t + 1).start()
        for t in busy.values():
            wr(t).wait()

    return pl.pallas_call(
        body, name=name, in_specs=[HBM_SPEC] * n, out_specs=[HBM_SPEC] * n,
        out_shape=[jax.ShapeDtypeStruct((N_CHIPS,) + o.shape, o.dtype) for o in ops],
        scratch_shapes=[pltpu.SemaphoreType.DMA((n,)), pltpu.SemaphoreType.DMA((n,))]
        + [pltpu.VMEM((2,) + shape, jnp.dtype(dt)) for shape, dt in kinds],
        compiler_params=pltpu.CompilerParams(vmem_limit_bytes=VMEM_LIMIT),
    )(*ops)


def _gather_start(lands, after, name):
    return _ici_start([], lands, after, _gather_copies([l.shape[1] // 2 for l in lands], False), name)


def _gather_wait(handle, after, name):
    halves = [l.shape[1] // 2 for l in handle[3]]
    return _ici_wait(handle, after, _gather_copies(halves, True), name)


def _forward_copies(halves, arriving):
    def copies(src_refs, land_refs, x, y, c):
        out = []
        for t, H in enumerate(halves):
            for j, (cx, cy) in enumerate(_other_chips(x, y)):
                mine = land_refs[t].at[2 * cx + cy, pl.ds(c * H, H), :]
                theirs = land_refs[t].at[2 * cx + cy, pl.ds((1 - c) * H, H), :]
                out.append((3 * t + j, mine, theirs if arriving else mine, (x, y, 1 - c)))
        return out
    return copies


def _forward_start(lands, after, name):
    halves = [l.shape[1] // 2 for l in lands]
    return _ici_start([], lands, after, _forward_copies(halves, False), name)


def _forward_wait(handle, after, name):
    halves = [l.shape[1] // 2 for l in handle[3]]
    return _ici_wait(handle, after, _forward_copies(halves, True), name)[1]


def _swap_halves(ops, name):
    n = len(ops)

    def body(*refs):
        in_refs, out_refs, send_sems, recv_sems = refs[:n], refs[n:2 * n], refs[2 * n], refs[2 * n + 1]
        x, y, c = _place()
        cps = []
        for t in range(n):
            H = ops[t].shape[1] // 2
            cp = pltpu.make_async_remote_copy(src_ref=in_refs[t].at[:, pl.ds((1 - c) * H, H), :],
                                              dst_ref=out_refs[t], send_sem=send_sems.at[t],
                                              recv_sem=recv_sems.at[t], device_id=(x, y, 1 - c),
                                              device_id_type=MESH)
            cp.start()
            cps.append(cp)
        for cp in cps:
            cp.wait()

    return pl.pallas_call(
        body, name=name, in_specs=[HBM_SPEC] * n, out_specs=[HBM_SPEC] * n,
        out_shape=[jax.ShapeDtypeStruct((N_CHIPS, o.shape[1] // 2, o.shape[2]), o.dtype) for o in ops],
        scratch_shapes=[pltpu.SemaphoreType.DMA((n,)), pltpu.SemaphoreType.DMA((n,))],
    )(*ops)


def _sum_rows_tile(h):
    return h if h <= 512 else 512


def _pair_sum(g, t, cq, name):
    _, a, b = g.shape
    H = a // 2
    tr = _sum_rows_tile(H)

    def body(cq_ref, g_ref, t_ref, o_ref):
        o_ref[...] = (g_ref[...].astype(F32) + t_ref[...].astype(F32)).astype(o_ref.dtype)

    grid_spec = pltpu.PrefetchScalarGridSpec(
        num_scalar_prefetch=1, grid=(N_CHIPS, H // tr),
        in_specs=[pl.BlockSpec((None, None, tr, b), lambda j, i, cq_ref: (j, cq_ref[0], i, 0)),
                  pl.BlockSpec((None, tr, b), lambda j, i, cq_ref: (j, i, 0))],
        out_specs=pl.BlockSpec((None, tr, b), lambda j, i, cq_ref: (j, i, 0)))
    return pl.pallas_call(
        body, name=name, grid_spec=grid_spec, out_shape=_out(t.shape, BF16),
        compiler_params=_cparams(("parallel", "parallel")),
    )(cq, g.reshape(N_CHIPS, 2, H, b), t)


def _scatter_copies(src_refs, land_refs, x, y, c):
    out = []
    for j, (cx, cy) in enumerate(_other_chips(x, y)):
        for t in range(len(src_refs)):
            out.append((3 * t + j, src_refs[t].at[2 * cx + cy], land_refs[t].at[j], (cx, cy, c)))
    return out


def _scatter_start(ops, after, name):
    lands = [lax.empty((3,) + o.shape[1:], o.dtype) for o in ops]
    return _ici_start(ops, lands, after, _scatter_copies, name)


def _scatter_wait(handle, after, name):
    return _ici_wait(handle, after, _scatter_copies, name)


def _chip_sum(p, t, cq, name):
    _, H, b = p.shape
    tr = _sum_rows_tile(H)

    def body(cq_ref, p_ref, t_ref, o_ref):
        acc = p_ref[...].astype(F32)
        for j in range(3):
            acc = acc + t_ref[j].astype(F32)
        o_ref[...] = acc

    grid_spec = pltpu.PrefetchScalarGridSpec(
        num_scalar_prefetch=1, grid=(H // tr,),
        in_specs=[pl.BlockSpec((None, tr, b), lambda i, cq_ref: (cq_ref[1], i, 0)),
                  pl.BlockSpec((3, tr, b), lambda i, cq_ref: (0, i, 0))],
        out_specs=pl.BlockSpec((None, tr, b), lambda i, cq_ref: (cq_ref[0], i, 0)))
    out = pl.pallas_call(
        body, name=name, grid_spec=grid_spec, out_shape=_out((2, H, b), F32),
        compiler_params=_cparams(("parallel",)),
    )(cq, p, t)
    return out.reshape(2 * H, b)


def _join_copies(arriving):
    def copies(src_refs, land_refs, x, y, c):
        out = []
        for t, land in enumerate(land_refs):
            H = land.shape[0] // 2
            mine, theirs = land.at[pl.ds(c * H, H), :], land.at[pl.ds((1 - c) * H, H), :]
            out.append((t, mine, theirs if arriving else mine, (x, y, 1 - c)))
        return out
    return copies


def _join_start(fs, name):
    return _ici_start([], fs, jnp.zeros(TOKEN, F32), _join_copies(False), name, per_src=1)


def _join_wait(handle, after, name):
    return _ici_wait(handle, after, _join_copies(True), name)[1]


def _direct_copies(src_refs, land_refs, x, y, c):
    out = []
    for t in range(len(src_refs)):
        H = src_refs[t].shape[1] // 2
        for k in range(1, 8):
            px, py, pc = x ^ (k >> 2), y ^ ((k >> 1) & 1), c ^ (k & 1)
            out.append((7 * t + k - 1, src_refs[t].at[2 * px + py, pl.ds(pc * H, H), :], land_refs[t].at[k - 1],
                        (px, py, pc)))
    return out


def _direct_sum(g, t, cq, name):
    _, a, b = g.shape
    H = a // 2
    tr = _sum_rows_tile(H)

    def body(cq_ref, g_ref, t_ref, o_ref):
        acc = g_ref[...].astype(F32)
        for k in range(7):
            acc = acc + t_ref[k].astype(F32)
        o_ref[...] = acc

    grid_spec = pltpu.PrefetchScalarGridSpec(
        num_scalar_prefetch=1, grid=(H // tr,),
        in_specs=[pl.BlockSpec((None, None, tr, b), lambda i, cq_ref: (cq_ref[1], cq_ref[0], i, 0)),
                  pl.BlockSpec((7, tr, b), lambda i, cq_ref: (0, i, 0))],
        out_specs=pl.BlockSpec((None, tr, b), lambda i, cq_ref: (cq_ref[0], i, 0)))
    out = pl.pallas_call(
        body, name=name, grid_spec=grid_spec, out_shape=_out((2, H, b), F32),
        compiler_params=_cparams(("parallel",)),
    )(cq, g.reshape(N_CHIPS, 2, H, b), t)
    return out.reshape(a, b)


def _reduce_direct_start(gs, tag):
    lands = [lax.empty((7, g.shape[1] // 2, g.shape[2]), g.dtype) for g in gs]
    return _ici_start(gs, lands, jnp.zeros(TOKEN, F32), _direct_copies, "rs_direct_start_" + tag, per_src=7)


def _reduce_direct_finish(handle, cq, after, tag):
    gs, rs = _ici_wait(handle, after, _direct_copies, "rs_direct_wait_" + tag)
    fs = [_direct_sum(g, r, cq, "rs_direct_sum") for g, r in zip(gs, rs)]
    return _join_start(fs, "rs_join_start_" + tag)


def _reduce_scatter_start(gs, cq, after, tag):
    ts = _swap_halves(gs, "rs_swap_" + tag)
    ps = [_pair_sum(g, t, cq, "rs_pair_sum") for g, t in zip(gs, ts)]
    return _scatter_start(ps, after, "rs_scatter_start_" + tag)


def _reduce_scatter_finish(handle, cq, after, tag):
    ps, rs = _scatter_wait(handle, after, "rs_scatter_wait_" + tag)
    fs = [_chip_sum(p, r, cq, "rs_chip_sum") for p, r in zip(ps, rs)]
    return _join_start(fs, "rs_join_start_" + tag)


def _all_reduce_small(v):
    n = v.shape[0]

    def body(v_ref, out_ref, buf, send_sems, recv_sems):
        x, y, c = _place()
        me = 4 * x + 2 * y + c
        buf[me] = v_ref[...]
        cps = []
        for k in range(1, 8):
            peer = (x ^ (k >> 2), y ^ ((k >> 1) & 1), c ^ (k & 1))
            cp = pltpu.make_async_remote_copy(src_ref=v_ref, dst_ref=buf.at[me], send_sem=send_sems.at[k - 1],
                                              recv_sem=recv_sems.at[k - 1], device_id=peer, device_id_type=MESH)
            cp.start()
            cps.append(cp)
        for k in range(1, 8):
            px, py, pc = x ^ (k >> 2), y ^ ((k >> 1) & 1), c ^ (k & 1)
            land = buf.at[4 * px + 2 * py + pc]
            pltpu.make_async_remote_copy(src_ref=land, dst_ref=land, send_sem=send_sems.at[k - 1],
                                         recv_sem=recv_sems.at[k - 1], device_id=(px, py, pc),
                                         device_id_type=MESH).wait_recv()
        for cp in cps:
            cp.wait_send()
        acc = buf[0]
        for d in range(1, 8):
            acc = acc + buf[d]
        out_ref[...] = acc

    vm = pl.BlockSpec(memory_space=pltpu.VMEM)
    return pl.pallas_call(
        body, name="all_reduce_small", in_specs=[vm], out_specs=vm,
        out_shape=jax.ShapeDtypeStruct((n, 128), F32),
        scratch_shapes=[pltpu.VMEM((8, n, 128), F32), pltpu.SemaphoreType.DMA((7,)), pltpu.SemaphoreType.DMA((7,))],
    )(v)


SMALL_GATHER = (16, 1024)
SMALL_FULL = sum(_size(_full_shape(n)) for n in SMALL)
SMALL_FULL_ROWS = -(-(SMALL_FULL + 1) // 128 // 8) * 8


def _layer_shards(w, i, q):
    kind, j = MIXER[i % 3], i // 3
    out = {n: w[n][i].astype(BF16) for n in COMMON_BIG}
    if kind == 'gla':
        win = jnp.zeros((D_MODEL, GLA_WIN), F32)
        win = lax.dynamic_update_slice(win, w['gla_w_in'][j], (0, (GLA_SHARD - GLA_WIN_STEP) * q))
        out['gla_w_in'] = win.astype(BF16)
        out['gla_w_out'] = w['gla_w_out'][j].astype(BF16)
    elif kind == 'mla':
        out['mla_w_in'] = jnp.pad(w['mla_w_in'][j], ((0, 0), (0, MLA_IN_PAD - MLA_IN))).astype(BF16)
        for n in ('mla_w_uq', 'mla_w_ukv', 'mla_w_out'):
            out[n] = w[n][j].astype(BF16)
    else:
        out['conv_w_in'] = w['conv_w_in'][j].astype(BF16)
        out['conv_w_out'] = w['conv_w_out'][j].astype(BF16)
    return out


def _rows_joined(g):
    return g.reshape(g.shape[0] * g.shape[1], g.shape[2])


def _cols_joined(g):
    return jnp.moveaxis(g, 0, 1).reshape(g.shape[1], -1)


def _layer_weights(g, i):
    kind = MIXER[i % 3]
    W = {}
    if 'mlp_w1' in g:
        W = {'w1': g['mlp_w1'], 'w2': _rows_joined(g['mlp_w2']), 'gate': _rows_joined(g['ple_w_gate']),
             'proj': g['ple_w_proj']}
    if kind == 'gla' and 'gla_w_out' in g:
        W['w_out'] = _rows_joined(g['gla_w_out'])
    if kind == 'gla' and 'gla_w_in' in g:
        parts = []
        for qq in range(N_CHIPS):
            lo = g['gla_w_in'][qq][:, :128]
            if qq > 0:
                lo = lo + g['gla_w_in'][qq - 1][:, GLA_WIN_STEP:]
            parts += [lo, g['gla_w_in'][qq][:, 128:GLA_WIN_STEP]]
        parts.append(g['gla_w_in'][N_CHIPS - 1][:, GLA_WIN_STEP:])
        W['w_in'] = jnp.concatenate(parts, axis=1)
    elif kind == 'mla':
        W['w_in'] = _rows_joined(g['mla_w_in'])
        uq = _cols_joined(g['mla_w_uq']).reshape(MLA_QR, MLA_HEADS, MLA_NOPE + MLA_ROPE)
        W['w_uq'] = jnp.pad(uq, ((0, 0), (0, 0), (0, MLA_QH - MLA_NOPE - MLA_ROPE))).reshape(MLA_QR, -1)
        ukv = _cols_joined(g['mla_w_ukv']).reshape(MLA_KVR, MLA_HEADS, 2, 128)
        W['w_ukv'] = ukv.transpose(0, 2, 1, 3).reshape(MLA_KVR, -1)
        W['w_out'] = _rows_joined(g['mla_w_out'])
    elif kind == 'conv':
        W['w_in'] = g['conv_w_in']
        W['w_out'] = _rows_joined(g['conv_w_out'])
    return W


def _pack_small_shards(w):
    flat = jnp.concatenate([w[n].reshape(-1) for n in SMALL_SHARDED])
    return jnp.pad(flat, (0, _size(SMALL_GATHER) - flat.shape[0])).reshape(SMALL_GATHER)


def _unpack_small_gathered(g):
    flat, out, off = g.reshape(N_CHIPS, -1), {}, 0
    for n in SMALL_SHARDED:
        shape, ax = WSPEC[n]
        seg = flat[:, off:off + _size(shape)].reshape((N_CHIPS,) + shape)
        out[n] = jnp.moveaxis(seg, 0, ax).reshape(_full_shape(n))
        off += _size(shape)
    return out


def _pack_small(vals, loss):
    flat = jnp.concatenate([vals[n].reshape(-1) for n in SMALL] + [loss.reshape(1)])
    return jnp.pad(flat, (0, SMALL_FULL_ROWS * 128 - flat.shape[0])).reshape(SMALL_FULL_ROWS, 128)


def _unpack_small(packed, q):
    flat = packed.reshape(-1)
    out, off = {}, 0
    for n in SMALL:
        shape, ax = WSPEC[n]
        full = flat[off:off + _size(_full_shape(n))].reshape(_full_shape(n))
        off += _size(_full_shape(n))
        out[n] = full if ax is None else lax.dynamic_slice_in_dim(full, q * shape[ax], shape[ax], axis=ax)
    return out


def _row_shards(dw):
    return dw.reshape(N_CHIPS, dw.shape[0] // N_CHIPS, dw.shape[1])


def _col_shards(dw):
    return jnp.moveaxis(dw.reshape(dw.shape[0], N_CHIPS, -1), 1, 0)


def _row(v):
    return v.reshape(1, -1)


def _layer_fwd(i, xin, xin_b, p_i, W, sm, cosp, sinp, rest=None, mid=None):
    kind, j = MIXER[i % 3], i // 3
    sv = {'xin': xin, 'xin_b': xin_b}
    if kind == 'gla':
        w_up = jnp.pad(sm['gla_w_gate_up'][j].astype(BF16), ((0, 128 - GLA_RANK), (0, 0)))
        pin = _mm(xin_b, W['w_in'], name="gla_in", tn=640, tm=FULL_ROWS)
        la = _mm(pin, w_up, name="gla_gate", K=128, tk=128, a_off=(0, (GLA_IN_PAD - 128) // 128), tn=512,
                 extras=[(_row(sm['gla_b_gate'][j]), 'n')],
                 epilogue=lambda acc, b: (_log_sigmoid(acc + b) * (1.0 / GLA_TAU),))
        o, states = _gla_fwd(pin, la)
        yb = _gla_post_fwd(o, pin, _row(sm['gla_norm_g'][j]))
        if rest is not None:
            W = {**W, **rest(yb)}
        mixed = yb
        sv.update(w_up=w_up, pin=pin, la=la, o=o, states=states, yb=yb)
    elif kind == 'mla':
        gq, gkv = sm['mla_q_norm'][j:j + 1], sm['mla_kv_norm'][j:j + 1]
        cin = _mm(xin_b, W['w_in'], name="mla_in", tn=640, tm=FULL_ROWS)
        qn, kvn, kr = _mla_norm_fwd(cin, gq, gkv, cosp, sinp)
        qr = _mla_qrope_fwd(_mm(qn, W['w_uq'], name="mla_uq"), cosp, sinp)
        knv = _mm(kvn, W['w_ukv'], name="mla_ukv", out_dtypes=(BF16,))
        o = _mla_attn_fwd(qr, knv, kr)
        ob = o.astype(BF16)
        mixed = ob
        sv.update(gq=gq, gkv=gkv, cin=cin, qn=qn, kvn=kvn, kr=kr, qr=qr, knv=knv, o=o, ob=ob)
    else:
        w8 = jnp.pad(sm['conv_w'][j], ((0, 5), (0, 0)))
        bcu = _mm(xin_b, W['w_in'], name="conv_in", tn=768, b_sh=True, tm=FULL_ROWS)
        yb = _conv_fwd(bcu, w8)
        mixed = yb
        sv.update(w8=w8, bcu=bcu, yb=yb)
    g0, b0 = _row(sm['ln_g'][i, 0]), _row(sm['ln_b'][i, 0])
    g1, b1 = _row(sm['ln_g'][i, 1]), _row(sm['ln_b'][i, 1])
    ln = dict(tm=512, tn=D_MODEL, out_dtypes=(F32, BF16, F32), epilogue=_ln_fwd_epilogue)
    x1, x1b, v0 = _mm(mixed, W['w_out'], name="mix_out_ln", extras=[(xin, 'mn'), (g0, 'n'), (b0, 'n')], **ln)
    ab = _mm(x1b, W['w1'], name="mlp_up", out_dtypes=(BF16,), b_sh=True, tm=FULL_ROWS,
             epilogue=lambda acc: (jnp.square(jnp.maximum(acc, 0.0)),))
    order = [(mid(ab), 'whole')] if mid else []
    x2, x2b, v1 = _mm(ab, W['w2'], name="mlp_down_ln", tk=D_FF,
                      extras=[(x1, 'mn'), (g1, 'n'), (b1, 'n')] + order, **ln)
    pp = _mm(p_i, W['proj'], name="ple_proj", tn=256, b_sh=True)
    z, x3, x3b = _mm(x2b, W['gate'], name="ple_gate", out_dtypes=(F32, F32, BF16),
                     extras=[(x2, 'mn'), (pp, 'mn')],
                     epilogue=lambda acc, xv, pv: (acc,) + (xv + jax.nn.sigmoid(acc) * pv,) * 2)
    sv.update(v0=v0, x1b=x1b, ab=ab, v1=v1, x2b=x2b, pp=pp, z=z, g0=g0, g1=g1)
    return x3, x3b, sv, W


def _layer_bwd(i, grads_in, p_i, W, sm, sv, cosp, sinp, token, early=None, below=None):
    kind, j = MIXER[i % 3], i // 3
    big, small = {}, {}
    dx, dpp_b, dz_b = grads_in
    big['ple_w_proj'] = _mm(p_i, dpp_b, ta=True, name="ple_proj_dw", tn=256, out_sh=True, out_dtypes=(BF16,))
    big['ple_w_gate'] = _row_shards(_mm(sv['x2b'], dz_b, ta=True, name="dw_dd", out_dtypes=(BF16,)))
    ln = dict(tb=True, tm=512, tn=D_MODEL, out_dtypes=(F32, BF16), n_sums=2)
    (dv1, dv1b), (dg1, db1) = _mm(dz_b, W['gate'], name="ple_gate_dx_ln", epilogue=_ln_bwd_epilogue(1.0),
                                  extras=[(dx, 'mn'), (sv['v1'], 'mn'), (sv['g1'], 'n'), (token, 'whole')], **ln)
    big['mlp_w2'] = _row_shards(_mm(sv['ab'], dv1b, ta=True, name="mlp_down_dw", out_dtypes=(BF16,)))
    dub = _mm(dv1b, W['w2'], tb=True, name="mlp_down_dx", out_dtypes=(BF16,), tm=FULL_ROWS,
              extras=[(sv['ab'], 'mn')], epilogue=lambda acc, a: (acc * (2.0 * jnp.sqrt(a.astype(F32))),))
    big['mlp_w1'] = _mm(sv['x1b'], dub, ta=True, name="mlp_up_dw", out_sh=True, out_dtypes=(BF16,))
    order = []
    if early is not None:
        order, big = [(early(big), 'whole')], {}
    (dv0, dv0b), (dg0, db0) = _mm(dub, W['w1'], name="mlp_up_dx_ln", b_sh=True, tk=D_FF, epilogue=_ln_bwd_epilogue(ALPHA),
                                  extras=[(dv1, 'mn'), (sv['v0'], 'mn'), (sv['g0'], 'n')] + order, **ln)
    small['ln_g'] = jnp.stack([dg0[0], dg1[0]])
    small['ln_b'] = jnp.stack([db0[0], db1[0]])
    resid = dict(tb=True, tn=D_MODEL, tm=512 if below else 1024, epilogue=_input_grad_epilogue,
                 extras=[(dv0, 'mn')] + [(a, 'mn') for a in below or ()],
                 out_dtypes=(F32, BF16, BF16) if below else (F32,))
    if kind == 'gla':
        big['gla_w_out'] = _row_shards(_mm(sv['yb'], dv0b, ta=True, name="dw_dd", out_dtypes=(BF16,)))
        dy = _mm(dv0b, W['w_out'], tb=True, name="dx_dd", tn=1024)
        do, dr_b, dng = _gla_post_bwd(dy, sv['o'], sv['pin'], _row(sm['gla_norm_g'][j]))
        dq_b, dk_b, dvv_b, dla = _gla_bwd(sv['pin'], sv['la'], sv['states'], do)
        dzg_b, dbg = _gla_gate_bwd(dla, sv['la'])
        dw_up = _mm(sv['pin'], dzg_b, ta=True, name="gla_gate_dw", M=128, tm=128,
                    a_off=(0, (GLA_IN_PAD - 128) // 128))
        dglr_b = _mm(dzg_b, sv['w_up'], tb=True, name="gla_gate_dx", out_dtypes=(BF16,))
        dpin_b = jnp.concatenate([dq_b, dk_b, dvv_b, dr_b, dglr_b], axis=1)
        dw_in = _mm(sv['xin_b'], dpin_b, ta=True, name="gla_in_dw", tn=640, out_dtypes=(BF16,))
        dxin = _mm(dpin_b, W['w_in'], name="gla_in_dx", tk=GLA_IN_PAD, **resid)
        big['gla_w_in'] = jnp.stack([dw_in[:, GLA_WIN_STEP * qq:GLA_WIN_STEP * qq + GLA_WIN]
                                     for qq in range(N_CHIPS)])
        small.update(gla_w_gate_up=dw_up[:GLA_RANK], gla_b_gate=dbg[0], gla_norm_g=dng[0])
    elif kind == 'mla':
        big['mla_w_out'] = _row_shards(_mm(sv['ob'], dv0b, ta=True, name="dw_dd", out_dtypes=(BF16,)))
        do = _mm(dv0b, W['w_out'], tb=True, name="dx_dd", tn=1024)
        dqr, dkn_b, dvv_b, dkr = _mla_attn_bwd(sv['qr'], sv['knv'], sv['kr'], sv['o'], do)
        dq_b = _mla_qrope_bwd(dqr, cosp, sinp)
        dw_uq = _mm(sv['qn'], dq_b, ta=True, name="mla_up_dw", out_dtypes=(BF16,))
        dqn = _mm(dq_b, W['w_uq'], tb=True, name="mla_up_dx")
        dknv_b = jnp.concatenate([dkn_b, dvv_b], axis=1)
        dw_ukv = _mm(sv['kvn'], dknv_b, ta=True, name="mla_up_dw", out_dtypes=(BF16,))
        dkvn = _mm(dknv_b, W['w_ukv'], tb=True, name="mla_up_dx")
        dcin_b, dgq, dgkv = _mla_norm_bwd(sv['cin'], dqn, dkvn, dkr, sv['gq'], sv['gkv'], cosp, sinp)
        big['mla_w_in'] = _row_shards(_mm(sv['xin_b'], dcin_b, ta=True, name="mla_in_dw", tn=640,
                                          out_dtypes=(BF16,)))
        dxin = _mm(dcin_b, W['w_in'], name="mla_in_dx", tk=MLA_IN_PAD, **resid)
        big['mla_w_uq'] = _col_shards(
            dw_uq.reshape(MLA_QR, MLA_HEADS, MLA_QH)[:, :, :MLA_NOPE + MLA_ROPE].reshape(MLA_QR, -1))
        big['mla_w_ukv'] = _col_shards(
            dw_ukv.reshape(MLA_KVR, 2, MLA_HEADS, 128).transpose(0, 2, 1, 3).reshape(MLA_KVR, -1))
        small.update(mla_q_norm=dgq[0], mla_kv_norm=dgkv[0])
    else:
        big['conv_w_out'] = _row_shards(_mm(sv['yb'], dv0b, ta=True, name="dw_dd", out_dtypes=(BF16,)))
        dy = _mm(dv0b, W['w_out'], tb=True, name="dx_dd", tn=1024)
        db_b, dc_b, du_b, dw8 = _conv_bwd(sv['bcu'], sv['w8'], dy)
        dbcu_b = jnp.concatenate([db_b, dc_b, du_b], axis=1)
        big['conv_w_in'] = _mm(sv['xin_b'], dbcu_b, ta=True, name="conv_in_dw", tn=768, out_sh=True,
                               out_dtypes=(BF16,))
        dxin = _mm(dbcu_b, W['w_in'], name="conv_in_dx", tk=3 * D_MODEL, b_sh=True, **resid)
        small['conv_w'] = dw8[:3]
    return (dxin if below else (dxin,)), big, small


def _rope_tables(positions):
    inv_freq = ROPE_BASE ** (-jnp.arange(0, MLA_ROPE // 2, dtype=F32) * (2.0 / MLA_ROPE))
    ang = positions.astype(F32)[:, None] * inv_freq
    zeros = jnp.zeros((positions.shape[0], 64), F32)
    return (jnp.concatenate([jnp.cos(ang), jnp.cos(ang), zeros], axis=1),
            jnp.concatenate([jnp.sin(ang), jnp.sin(ang), zeros], axis=1))


FIRST_NEEDED = ['gla_w_in']


def _start_gathers(w, q):
    token, started = jnp.zeros(TOKEN, F32), []
    for i in range(DEPTH):
        sh = _layer_shards(w, i, q)
        for k, names in enumerate([list(sh)] if i > 0 else [FIRST_NEEDED, [n for n in sh if n not in FIRST_NEEDED]]):
            ops = [sh[n] for n in names]
            if i == 0 and k == 0:
                ops.append(_pack_small_shards(w))
            tag = "l%d%s" % (i, "ab"[k] if i == 0 else "")
            handle = _gather_start(_place_own(ops, "ag_own_" + tag), token, "ag_start_" + tag)
            token = handle[4]
            started.append((handle, names, tag))
    return started, token


def _pass_on(entry, after):
    handle, names, tag = entry
    _, lands = _gather_wait(handle, after, "ag_wait_" + tag)
    passing = _forward_start(lands, jnp.zeros(TOKEN, F32), "ag_pass_start_" + tag)
    return (passing, names, tag), passing[4]


def _gathered(passed, after):
    passing, names, tag = passed
    got = _forward_wait(passing, after, "ag_pass_wait_" + tag)
    return dict(zip(names, got)), got[-1]


def _local_shard_grad(name, g, q):
    if name == 'gla_w_in':
        return lax.dynamic_slice_in_dim(g, (GLA_SHARD - GLA_WIN_STEP) * q, GLA_SHARD, axis=1)
    if name == 'mla_w_in':
        return g[:, :MLA_IN]
    return g


def kernel(x, p, positions, gla_w_in, gla_w_gate_up, gla_b_gate, gla_norm_g, gla_w_out, mla_w_in, mla_q_norm, mla_kv_norm, mla_w_uq, mla_w_ukv, mla_w_out, conv_w_in, conv_w, conv_w_out, ln_g, ln_b, mlp_w1, mlp_w2, ple_w_gate, ple_w_proj, loss_target, m_gla_w_in, m_gla_w_gate_up, m_gla_b_gate, m_gla_norm_g, m_gla_w_out, m_mla_w_in, m_mla_q_norm, m_mla_kv_norm, m_mla_w_uq, m_mla_w_ukv, m_mla_w_out, m_conv_w_in, m_conv_w, m_conv_w_out, m_ln_g, m_ln_b, m_mlp_w1, m_mlp_w2, m_ple_w_gate, m_ple_w_proj, v_gla_w_in, v_gla_w_gate_up, v_gla_b_gate, v_gla_norm_g, v_gla_w_out, v_mla_w_in, v_mla_q_norm, v_mla_kv_norm, v_mla_w_uq, v_mla_w_ukv, v_mla_w_out, v_conv_w_in, v_conv_w, v_conv_w_out, v_ln_g, v_ln_b, v_mlp_w1, v_mlp_w2, v_ple_w_gate, v_ple_w_proj):
    args = locals()
    w = {n: args[n] for n in WNAMES}
    m = {n: args['m_' + n] for n in WNAMES}
    v = {n: args['v_' + n] for n in WNAMES}
    q = 2 * lax.axis_index("x") + lax.axis_index("y")
    cq = jnp.stack([lax.axis_index("c"), q]).astype(jnp.int32)

    cosp, sinp = _rope_tables(positions[0])
    started, after = _start_gathers(w, q)
    xin, saved, layers, sm = x[0], [], [], None
    xin_b = xin.astype(BF16)
    passed, after = _pass_on(started[0], after)
    for i in range(DEPTH):
        got, last = _gathered(passed, after)
        rest = mid = None
        if i == 0:
            sm = _unpack_small_gathered(last)
            sm['mla_q_norm'], sm['mla_kv_norm'] = w['mla_q_norm'], w['mla_kv_norm']
            rest = lambda after: _layer_weights(_gathered(*_pass_on(started[1], after))[0], 0)
        coming = {}
        if i + 1 < DEPTH:
            def mid(after, entry=started[i + 2], coming=coming):
                coming['passed'], token = _pass_on(entry, after)
                return token
        xin, xin_b, sv, W = _layer_fwd(i, xin, xin_b, p[i, 0], _layer_weights(got, i), sm, cosp, sinp, rest, mid)
        layers.append(W)
        saved.append(sv)
        passed, after = coming.get('passed'), xin
    *grads_in, loss_cols = _loss_head(xin, loss_target[0], saved[-1]['z'], saved[-1]['pp'])
    loss = jnp.sum(loss_cols[0])

    gbig = {n: [None] * WSPEC[n][0][0] for n in BIG}
    gsmall = {n: [None] * _full_shape(n)[0] for n in SMALL}
    pending = []

    def start(grads, i, tag):
        names = list(grads)
        gs = [grads[n] for n in names]
        handle = _reduce_direct_start(gs, tag) if i > 0 else _reduce_scatter_start(gs, cq, jnp.zeros(TOKEN, F32), tag)
        pending.append((handle, names, i, tag))
        return handle[4]

    joining = []

    def finish(above, after, token):
        for entry in [e for e in pending if e[2] > above]:
            pending.remove(entry)
            handle, names, i, tag = entry
            handle = (_reduce_direct_finish if i > 0 else _reduce_scatter_finish)(handle, cq, after, tag)
            joining.append((handle, names, i, tag))
            token = token + handle[4]
        return token

    token = jnp.zeros(TOKEN, F32)
    for i in reversed(range(DEPTH)):
        early = (lambda grads: start(grads, 0, "l0a")) if i == 0 else None
        below = (saved[i - 1]['z'], saved[i - 1]['pp']) if i > 0 else None
        grads_in, big, small = _layer_bwd(i, grads_in, p[i, 0], layers[i], sm, saved[i], cosp, sinp, token, early,
                                          below)
        dx = grads_in[0]
        token = finish(i + 1, dx, start(big, i, "l%d%s" % (i, "b" if i == 0 else "")))
        for n, g in small.items():
            gsmall[n][i if n in ('ln_g', 'ln_b') else i // 3] = g
    finish(-1, token, token)
    for handle, names, i, tag in joining:
        for n, g in zip(names, _join_wait(handle, joining[-1][0][4], "rs_join_wait_" + tag)):
            gbig[n][i if n in COMMON_BIG else i // 3] = _local_shard_grad(n, g, q)
    small_sum = _all_reduce_small(_pack_small({n: jnp.stack(g) for n, g in gsmall.items()}, loss))
    gsm, loss = _unpack_small(small_sum, q), small_sum.reshape(-1)[SMALL_FULL]

    grad, delta, new_m, new_v = {}, {}, {}, {}
    for n in BIG:
        update = _adamw_shard_major if n == 'gla_w_in' else _adamw
        grad[n], delta[n], new_m[n], new_v[n] = update(w[n], m[n], v[n], gbig[n], "adamw_" + n)
    flat2 = lambda a: a.reshape(-1, a.shape[-1])
    res = _adamw_small(*[[flat2(d[n]) for n in SMALL] for d in (w, gsm, m, v)])
    for k, out in enumerate((grad, delta, new_m, new_v)):
        for n, r in zip(SMALL, res[k::4]):
            out[n] = r.reshape(WSPEC[n][0])
    return (loss, dx[None], *[grad[n] for n in WNAMES], *[delta[n] for n in WNAMES],
            *[new_m[n] for n in WNAMES], *[new_v[n] for n in WNAMES])
```

```python
import functools

import numpy as np
import jax
import jax.numpy as jnp
from jax import lax
from jax.experimental import pallas as pl
from jax.experimental.pallas import tpu as pltpu

F32 = jnp.float32
BF16 = jnp.bfloat16
MESH = pl.DeviceIdType.MESH

D_MODEL = 1024
DEPTH = 4
CHUNK = 64
ALPHA = (2 * DEPTH) ** 0.25
LN_EPS = 1e-5
RMS_EPS = 1e-6
D_FF = 4 * D_MODEL
GLA_HEADS = 4
GLA_DK = 128
GLA_DV = 256
GLA_RANK = 16
GLA_TAU = 16.0
GLA_HK = GLA_HEADS * GLA_DK
GLA_HV = GLA_HEADS * GLA_DV
GLA_IN = 2 * GLA_HK + GLA_HV + D_MODEL + GLA_RANK
GLA_IN_PAD = 2 * GLA_HK + GLA_HV + D_MODEL + 128
GLA_SHARD = GLA_IN // 4
GLA_WIN = 896
GLA_WIN_STEP = 768
MLA_HEADS = 8
MLA_NOPE = 128
MLA_ROPE = 64
MLA_V = 128
MLA_QR = 256
MLA_KVR = 256
MLA_IN = MLA_QR + MLA_KVR + MLA_ROPE
MLA_IN_PAD = MLA_QR + MLA_KVR + 128
MLA_QH = 256
ROPE_BASE = 10000.0
ADAM_LR = 0.001
ADAM_B1 = 0.9
ADAM_B2 = 0.999
ADAM_EPS = 1e-08
ADAM_WD = 0.01
ADAM_STEP = 10

VMEM_LIMIT = 48 * 1024 * 1024
FULL_ROWS = 2048
N_CHIPS = 4

WSPEC = {
    'gla_w_in': ((2, 1024, 772), 2), 'gla_w_gate_up': ((2, 16, 128), 2), 'gla_b_gate': ((2, 128), 1),
    'gla_norm_g': ((2, 64), 1), 'gla_w_out': ((2, 256, 1024), 1), 'mla_w_in': ((1, 256, 576), 1),
    'mla_q_norm': ((1, 256), None), 'mla_kv_norm': ((1, 256), None), 'mla_w_uq': ((1, 256, 384), 2),
    'mla_w_ukv': ((1, 256, 512), 2), 'mla_w_out': ((1, 256, 1024), 1), 'conv_w_in': ((1, 1024, 768), 2),
    'conv_w': ((1, 3, 256), 2), 'conv_w_out': ((1, 256, 1024), 1), 'ln_g': ((4, 2, 256), 2),
    'ln_b': ((4, 2, 256), 2), 'mlp_w1': ((4, 1024, 1024), 2), 'mlp_w2': ((4, 1024, 1024), 1),
    'ple_w_gate': ((4, 256, 1024), 1), 'ple_w_proj': ((4, 256, 256), 2),
}
WNAMES = list(WSPEC)
BIG = ['gla_w_in', 'gla_w_out', 'mla_w_in', 'mla_w_uq', 'mla_w_ukv', 'mla_w_out', 'conv_w_in', 'conv_w_out',
       'mlp_w1', 'mlp_w2', 'ple_w_gate', 'ple_w_proj']
SMALL_SHARDED = ['gla_w_gate_up', 'gla_b_gate', 'gla_norm_g', 'conv_w', 'ln_g', 'ln_b']
SMALL = SMALL_SHARDED + ['mla_q_norm', 'mla_kv_norm']
MIXER = ['gla', 'mla', 'conv']
COMMON_BIG = ['mlp_w1', 'mlp_w2', 'ple_w_gate', 'ple_w_proj']


def _size(shape):
    return int(np.prod(shape))


def _full_shape(name):
    shape, ax = WSPEC[name]
    if ax is None:
        return shape
    return tuple(s * N_CHIPS if i == ax else s for i, s in enumerate(shape))


def _cparams(sem=None):
    return pltpu.CompilerParams(dimension_semantics=sem, vmem_limit_bytes=VMEM_LIMIT)


def _out(shape, dtype):
    return pltpu.HBM(shape, dtype)


def _hbm(v):
    return pltpu.with_memory_space_constraint(v, pltpu.HBM)


def _mm(a, b, *, name, ta=False, tb=False, M=None, N=None, K=None, out_dtypes=(F32,), epilogue=None, extras=(),
        tm=1024, tn=512, tk=None, a_off=(0, 0), b_sh=False, out_sh=False, n_sums=0):
    if M is None:
        M = a.shape[1] if ta else a.shape[0]
    if K is None:
        K = a.shape[0] if ta else a.shape[1]
    if b_sh:
        kw, nq = b.shape[1], b.shape[2]
        n_b, k_b = (kw, N_CHIPS * nq) if tb else (N_CHIPS * nq, kw)
        N = n_b if N is None else N
        assert K == k_b
    elif N is None:
        N = b.shape[0] if tb else b.shape[1]
    if tk is None:
        tk = FULL_ROWS if ta else 1024
    tm, tn, tk = min(tm, M), min(tn, N), min(tk, K)
    assert M % tm == 0 and N % tn == 0 and K % tk == 0, (name, M, N, K, tm, tn, tk)
    nk = K // tk
    n_ex, n_out = len(extras), len(out_dtypes)
    assert n_sums == 0 or tn == N

    n_b = N_CHIPS if (b_sh and tb and tk == K) else 1

    def body(a_ref, *rest):
        b_refs, rest = rest[:n_b], rest[n_b:]
        ex_refs, out_refs = rest[:n_ex], rest[n_ex:n_ex + n_out]
        sum_refs = rest[n_ex + n_out:n_ex + n_out + n_sums]
        first_rows = pl.program_id(0) == 0
        dims = ((((0,) if ta else (1,)), ((1,) if tb else (0,))), ((), ()))
        if n_b == 1:
            part = lax.dot_general(a_ref[...].astype(BF16), b_refs[0][...].astype(BF16), dims,
                                   preferred_element_type=F32)
        else:
            part = sum(lax.dot_general(a_ref[:, s * nq:(s + 1) * nq].astype(BF16), b_refs[s][...].astype(BF16), dims,
                                       preferred_element_type=F32) for s in range(n_b))

        def finish(acc):
            res = (acc,) if epilogue is None else epilogue(acc, *[r[...] for r in ex_refs])
            if n_sums:
                res, sums = res

                @pl.when(first_rows)
                def _():
                    for r in sum_refs:
                        r[...] = jnp.zeros(r.shape, F32)

                for r, v in zip(sum_refs, sums):
                    r[...] += jnp.broadcast_to(v, r.shape)
            for r, v in zip(out_refs, res):
                r[...] = v.astype(r.dtype)

        if nk == 1:
            finish(part)
        else:
            acc_ref = rest[-1]
            k = pl.program_id(2)

            @pl.when(k == 0)
            def _():
                acc_ref[...] = part

            @pl.when(k > 0)
            def _():
                acc_ref[...] += part

            @pl.when(k == nk - 1)
            def _():
                finish(acc_ref[...])

    if ta:
        a_spec = pl.BlockSpec((tk, tm), lambda i, j, k: (k + a_off[0], i + a_off[1]))
    else:
        a_spec = pl.BlockSpec((tm, tk), lambda i, j, k: (i + a_off[0], k + a_off[1]))
    once = dict(pipeline_mode=pl.Buffered(1)) if (tn == N and nk == 1) else {}
    if n_b > 1:
        b_specs = [pl.BlockSpec((None, tn, nq), functools.partial(lambda i, j, k, s: (s, j, 0), s=s), **once)
                   for s in range(n_b)]
    elif b_sh and tb:
        assert nq % tk == 0
        per = nq // tk
        b_spec = pl.BlockSpec((None, tn, tk), lambda i, j, k: (k // per, j, k % per), **once)
    elif b_sh:
        assert nq % tn == 0
        per = nq // tn
        b_spec = pl.BlockSpec((None, tk, tn), lambda i, j, k: (j // per, k, j % per), **once)
    elif tb:
        b_spec = pl.BlockSpec((tn, tk), lambda i, j, k: (j, k), **once)
    else:
        b_spec = pl.BlockSpec((tk, tn), lambda i, j, k: (k, j), **once)
    if n_b == 1:
        b_specs = [b_spec]
    ex_specs = []
    for arr, kind in extras:
        if kind == 'mn':
            ex_specs.append(pl.BlockSpec((tm, tn), lambda i, j, k: (i, j)))
        elif kind == 'n':
            ex_specs.append(pl.BlockSpec((1, tn), lambda i, j, k: (0, j)))
        else:
            ex_specs.append(pl.BlockSpec(arr.shape, lambda i, j, k: (0, 0)))
    if out_sh:
        assert (N // N_CHIPS) % tn == 0
        per_o = N // N_CHIPS // tn
        o_spec = pl.BlockSpec((None, tm, tn), lambda i, j, k: (j // per_o, i, j % per_o))
        o_shape = (N_CHIPS, M, N // N_CHIPS)
    else:
        o_spec = pl.BlockSpec((tm, tn), lambda i, j, k: (i, j))
        o_shape = (M, N)
    outs = pl.pallas_call(
        body, name=name, grid=(M // tm, N // tn, nk),
        in_specs=[a_spec] + b_specs + ex_specs,
        out_specs=[o_spec for _ in out_dtypes] + [pl.BlockSpec((8, N), lambda i, j, k: (0, 0))] * n_sums,
        out_shape=[_out(o_shape, d) for d in out_dtypes] + [_out((8, N), F32)] * n_sums,
        scratch_shapes=[pltpu.VMEM((tm, tn), F32)] if nk > 1 else [],
        compiler_params=_cparams(("arbitrary" if n_sums else "parallel", "parallel", "arbitrary")),
    )(a, *[b] * n_b, *[e[0] for e in extras])
    if n_sums:
        return tuple(outs[:n_out]), tuple(outs[n_out:])
    return outs[0] if n_out == 1 else tuple(outs)


def _rowwise(fn, *, name, rows, pars=(), outs=(), accs=(), tm=256):
    S = rows[0][0].shape[0]
    tm = min(tm, S)
    assert S % tm == 0
    n_r, n_p, n_o, n_a = len(rows), len(pars), len(outs), len(accs)

    def body(*refs):
        r_refs, p_refs = refs[:n_r], refs[n_r:n_r + n_p]
        o_refs, a_refs = refs[n_r + n_p:n_r + n_p + n_o], refs[n_r + n_p + n_o:]
        o_vals, a_vals = fn([r[...] for r in r_refs], [p[...] for p in p_refs])
        for r, v in zip(o_refs, o_vals):
            r[...] = v.astype(r.dtype)
        if n_a:
            i = pl.program_id(0)

            @pl.when(i == 0)
            def _():
                for r in a_refs:
                    r[...] = jnp.zeros(r.shape, r.dtype)

            for r, v in zip(a_refs, a_vals):
                r[...] += jnp.broadcast_to(v, r.shape)

    in_specs = [pl.BlockSpec((tm, w), functools.partial(lambda i, o: (i, o), o=off)) for _, w, off in rows]
    in_specs += [pl.BlockSpec(p.shape, functools.partial(lambda i, nd: (0,) * nd, nd=p.ndim)) for p in pars]
    out_specs = [pl.BlockSpec((tm, w), lambda i: (i, 0)) for w, _ in outs]
    out_specs += [pl.BlockSpec((8, w), lambda i: (0, 0)) for w in accs]
    out_shape = [_out((S, w), d) for w, d in outs]
    out_shape += [_out((8, w), F32) for w in accs]
    res = pl.pallas_call(
        body, name=name, grid=(S // tm,), in_specs=in_specs, out_specs=out_specs, out_shape=out_shape,
        compiler_params=_cparams(("arbitrary",)),
    )(*[r[0] for r in rows], *pars)
    return tuple(res)


def _colsum(v):
    return jnp.sum(v, axis=0, keepdims=True)


def _ln_stats(v):
    mu = jnp.mean(v, axis=-1, keepdims=True)
    d = v - mu
    var = jnp.mean(d * d, axis=-1, keepdims=True)
    rstd = lax.rsqrt(var + LN_EPS)
    return d * rstd, rstd


def _ln_fwd_epilogue(h, x, g, b, *unused):
    v = ALPHA * x + h
    xhat, _ = _ln_stats(v)
    y = xhat * g + b
    return y, y, v


def _ln_bwd_epilogue(scale):
    def epilogue(acc, resid, v, g, *unused):
        dy = acc + scale * resid
        xhat, rstd = _ln_stats(v)
        dxh = dy * g
        m1 = jnp.mean(dxh, axis=-1, keepdims=True)
        m2 = jnp.mean(dxh * xhat, axis=-1, keepdims=True)
        dv = rstd * (dxh - m1 - xhat * m2)
        return (dv, dv), (_colsum(dy * xhat), _colsum(dy))
    return epilogue


def _ple_gate_grads(dx3, z, pp):
    s = jax.nn.sigmoid(z)
    return dx3 * s, dx3 * pp * s * (1.0 - s)


def _loss_head(y, t, z, pp):
    def fn(r, p):
        d = r[0] - r[1]
        dy = d * (1.0 / D_MODEL)
        return [dy, *_ple_gate_grads(dy, r[2], r[3])], [_colsum(d * d) * (0.5 / D_MODEL)]
    return _rowwise(fn, name="loss_head", rows=[(a, D_MODEL, 0) for a in (y, t, z, pp)],
                    outs=[(D_MODEL, F32), (D_MODEL, BF16), (D_MODEL, BF16)], accs=[D_MODEL])


def _input_grad_epilogue(acc, dv, *below):
    dx = acc + ALPHA * dv
    return (dx, *_ple_gate_grads(dx, *below)) if below else (dx,)


N_LEVELS = 6
GLA_STEP = 2


def _gla_consts():
    C = CHUNK
    A = np.zeros((N_LEVELS + 3, C, C), np.float32)
    masks = np.zeros((N_LEVELS + 1, C, C), np.float32)
    r = np.arange(C)[:, None]
    u = np.arange(C)[None, :]
    for l in range(N_LEVELS):
        half = C >> (l + 1)
        mid = (r // (2 * half)) * (2 * half) + half - 1
        A[l] = np.where(r > mid, (u > mid) & (u <= r), (u > r) & (u <= mid))
        masks[l] = ((r // (2 * half)) == (u // (2 * half))) & (((r // half) % 2) != ((u // half) % 2))
    masks[N_LEVELS] = (r == u)
    A[N_LEVELS] = (u <= r)
    A[N_LEVELS + 1] = (u > r)
    A[N_LEVELS + 2] = 1.0
    A = A.reshape(-1, C)
    return A, np.ascontiguousarray(A.T), masks


def _split3(v):
    hi = v.astype(BF16)
    r1 = v - hi.astype(F32)
    mid = r1.astype(BF16)
    lo = (r1 - mid.astype(F32)).astype(BF16)
    return hi, mid, lo


def _dot_exact01(a01, v):
    hi, mid, lo = _split3(v)
    f = lambda p: jnp.dot(a01, p, preferred_element_type=F32)
    return f(hi) + f(mid) + f(lo)


def _nt(a, b):
    return lax.dot_general(a, b, (((1,), (1,)), ((), ())), preferred_element_type=F32)


def _tn(a, b):
    return lax.dot_general(a, b, (((0,), (0,)), ((), ())), preferred_element_type=F32)


def _nn(a, b):
    return jnp.dot(a, b, preferred_element_type=F32)


def _gla_chunk_terms(q, k, E, m_ref):
    C = CHUNK
    scores = m_ref[N_LEVELS] * _nt(q.astype(BF16), k.astype(BF16))
    qes, kes = [], []
    for l in range(N_LEVELS):
        El = E[l * C:(l + 1) * C]
        qe, ke = (q * El).astype(BF16), (k * El).astype(BF16)
        qes.append(qe)
        kes.append(ke)
        scores = scores + m_ref[l] * _nt(qe, ke)
    return qes, kes, scores


def _head(v, h, w):
    return v[:, h * w:(h + 1) * w]


def _gla_fwd(pin, la):
    S = pin.shape[0]
    NC = S // CHUNK
    C, R = CHUNK, CHUNK * GLA_STEP
    A, _, masks = _gla_consts()

    def body(q_ref, k_ref, v_ref, la_ref, a_ref, m_ref, o_ref, st_ref, state):
        @pl.when(pl.program_id(0) == 0)
        def _():
            state[...] = jnp.zeros(state.shape, F32)

        for ci in range(GLA_STEP):
            rows = pl.ds(ci * C, C)
            E_all = jnp.exp(_dot_exact01(a_ref[...], la_ref[rows, :]))
            q_all = q_ref[rows, :] * (GLA_DK ** -0.5)
            k_all, v_all = k_ref[rows, :], v_ref[rows, :]
            outs = []
            for h in range(GLA_HEADS):
                q, k, E = _head(q_all, h, GLA_DK), _head(k_all, h, GLA_DK), _head(E_all, h, GLA_DK)
                _, _, scores = _gla_chunk_terms(q, k, E, m_ref)
                Eq, Ek, Ee = E[6 * C:7 * C], E[7 * C:8 * C], E[8 * C:9 * C]
                st = state[h]
                st_ref[h, ci] = st
                vb = _head(v_all, h, GLA_DV).astype(BF16)
                outs.append(_nn(scores.astype(BF16), vb) + _nt((q * Eq).astype(BF16), st.astype(BF16)))
                state[h] = st * jnp.concatenate([Ee] * (GLA_DV // C), axis=0) + _tn(vb, (k * Ek).astype(BF16))
            o_ref[rows, :] = jnp.concatenate(outs, axis=1)

    return pl.pallas_call(
        body, name="gla_fwd", grid=(NC // GLA_STEP,),
        in_specs=[pl.BlockSpec((R, GLA_HK), lambda c: (c, 0)),
                  pl.BlockSpec((R, GLA_HK), lambda c: (c, 1)),
                  pl.BlockSpec((R, GLA_HV), lambda c: (c, 2 * GLA_HK // GLA_HV)),
                  pl.BlockSpec((R, GLA_HK), lambda c: (c, 0)),
                  pl.BlockSpec(A.shape, lambda c: (0, 0)),
                  pl.BlockSpec(masks.shape, lambda c: (0, 0, 0))],
        out_specs=[pl.BlockSpec((R, GLA_HV), lambda c: (c, 0)),
                   pl.BlockSpec((GLA_HEADS, GLA_STEP, GLA_DV, GLA_DK), lambda c: (0, c, 0, 0))],
        out_shape=[_out((S, GLA_HV), F32), _out((GLA_HEADS, NC, GLA_DV, GLA_DK), F32)],
        scratch_shapes=[pltpu.VMEM((GLA_HEADS, GLA_DV, GLA_DK), F32)],
        compiler_params=_cparams(("arbitrary",)),
    )(pin, pin, pin, la, jnp.asarray(A, BF16), jnp.asarray(masks))


def _gla_bwd(pin, la, states, do):
    S = pin.shape[0]
    NC = S // CHUNK
    C, R = CHUNK, CHUNK * GLA_STEP
    A, AT, masks = _gla_consts()
    scale = GLA_DK ** -0.5

    def body(q_ref, k_ref, v_ref, la_ref, st_ref, do_ref, a_ref, at_ref, m_ref,
             dq_ref, dk_ref, dv_ref, dla_ref, dstate):
        @pl.when(pl.program_id(0) == 0)
        def _():
            dstate[...] = jnp.zeros(dstate.shape, F32)

        for ci in reversed(range(GLA_STEP)):
            one_chunk(ci, pl.ds(ci * C, C), q_ref, k_ref, v_ref, la_ref, st_ref, do_ref, a_ref, at_ref, m_ref,
                      dq_ref, dk_ref, dv_ref, dla_ref, dstate)

    def one_chunk(ci, rows, q_ref, k_ref, v_ref, la_ref, st_ref, do_ref, a_ref, at_ref, m_ref,
                  dq_ref, dk_ref, dv_ref, dla_ref, dstate):
        E_all = jnp.exp(_dot_exact01(a_ref[...], la_ref[rows, :]))
        q_all = q_ref[rows, :] * scale
        k_all, v_all, do_all = k_ref[rows, :], v_ref[rows, :], do_ref[rows, :]
        dqs, dks, dvs, dXs = [], [], [], []
        for h in range(GLA_HEADS):
            q, k, E = _head(q_all, h, GLA_DK), _head(k_all, h, GLA_DK), _head(E_all, h, GLA_DK)
            qes, kes, scores = _gla_chunk_terms(q, k, E, m_ref)
            Eq, Ek, Ee = E[6 * C:7 * C], E[7 * C:8 * C], E[8 * C:9 * C]
            st, dst = st_ref[h, ci], dstate[h]
            dob, vb = _head(do_all, h, GLA_DV).astype(BF16), _head(v_all, h, GLA_DV).astype(BF16)
            dstb = dst.astype(BF16)
            qEq, kEk = (q * Eq).astype(BF16), (k * Ek).astype(BF16)
            dsc = _nt(dob, vb)
            dvs.append(_tn(scores.astype(BF16), dob) + _nt(kEk, dstb))
            dqEq = _nn(dob, st.astype(BF16))
            dkEk = _nn(vb, dstb)
            Gd = (m_ref[N_LEVELS] * dsc).astype(BF16)
            dq = _nn(Gd, k.astype(BF16)) + dqEq * Eq
            dk = _tn(Gd, q.astype(BF16)) + dkEk * Ek
            dX = []
            for l in range(N_LEVELS):
                El = E[l * C:(l + 1) * C]
                G = (m_ref[l] * dsc).astype(BF16)
                dqe, dke = _nn(G, kes[l]), _tn(G, qes[l])
                dq = dq + dqe * El
                dk = dk + dke * El
                dX.append((dqe * q + dke * k) * El)
            dX.append(dqEq * q * Eq)
            dX.append(dkEk * k * Ek)
            prod = dst * st
            dEe = prod[0:C]
            for i in range(1, GLA_DV // C):
                dEe = dEe + prod[i * C:(i + 1) * C]
            dX.append(dEe * Ee)
            dXs.append(jnp.concatenate(dX, axis=0))
            dqs.append(dq * scale)
            dks.append(dk)
            dstate[h] = dst * jnp.concatenate([Ee] * (GLA_DV // C), axis=0) + _tn(dob, qEq)
        dla_ref[rows, :] = _dot_exact01(at_ref[...], jnp.concatenate(dXs, axis=1))
        dq_ref[rows, :] = jnp.concatenate(dqs, axis=1).astype(dq_ref.dtype)
        dk_ref[rows, :] = jnp.concatenate(dks, axis=1).astype(dk_ref.dtype)
        dv_ref[rows, :] = jnp.concatenate(dvs, axis=1).astype(dv_ref.dtype)

    rc = lambda c: NC // GLA_STEP - 1 - c
    return pl.pallas_call(
        body, name="gla_bwd", grid=(NC // GLA_STEP,),
        in_specs=[pl.BlockSpec((R, GLA_HK), lambda c: (rc(c), 0)),
                  pl.BlockSpec((R, GLA_HK), lambda c: (rc(c), 1)),
                  pl.BlockSpec((R, GLA_HV), lambda c: (rc(c), 2 * GLA_HK // GLA_HV)),
                  pl.BlockSpec((R, GLA_HK), lambda c: (rc(c), 0)),
                  pl.BlockSpec((GLA_HEADS, GLA_STEP, GLA_DV, GLA_DK), lambda c: (0, rc(c), 0, 0)),
                  pl.BlockSpec((R, GLA_HV), lambda c: (rc(c), 0)),
                  pl.BlockSpec(A.shape, lambda c: (0, 0)),
                  pl.BlockSpec(AT.shape, lambda c: (0, 0)),
                  pl.BlockSpec(masks.shape, lambda c: (0, 0, 0))],
        out_specs=[pl.BlockSpec((R, GLA_HK), lambda c: (rc(c), 0)),
                   pl.BlockSpec((R, GLA_HK), lambda c: (rc(c), 0)),
                   pl.BlockSpec((R, GLA_HV), lambda c: (rc(c), 0)),
                   pl.BlockSpec((R, GLA_HK), lambda c: (rc(c), 0))],
        out_shape=[_out((S, GLA_HK), BF16), _out((S, GLA_HK), BF16), _out((S, GLA_HV), BF16),
                   _out((S, GLA_HK), F32)],
        scratch_shapes=[pltpu.VMEM((GLA_HEADS, GLA_DV, GLA_DK), F32)],
        compiler_params=_cparams(("arbitrary",)),
    )(pin, pin, pin, la, states, do, jnp.asarray(A, BF16), jnp.asarray(AT, BF16), jnp.asarray(masks))


def _gla_post_fwd(o, pin, g):
    def fn(r, p):
        ov, rv = r
        ys = []
        for h in range(GLA_HEADS):
            oh = ov[:, h * GLA_DV:(h + 1) * GLA_DV]
            rh = rv[:, h * GLA_DV:(h + 1) * GLA_DV]
            rs = lax.rsqrt(jnp.mean(oh * oh, axis=-1, keepdims=True) + RMS_EPS)
            ys.append(oh * rs * p[0] * (rh * jax.nn.sigmoid(rh)))
        return [jnp.concatenate(ys, axis=1)], []
    return _rowwise(fn, name="gla_post_fwd", rows=[(o, GLA_HV, 0), (pin, GLA_HV, (2 * GLA_HK + GLA_HV) // GLA_HV)],
                    pars=[g], outs=[(GLA_HV, BF16)])[0]


def _gla_post_bwd(dy, o, pin, g):
    def fn(r, p):
        dyv, ov, rv = r
        dos, drs, dg = [], [], 0.0
        for h in range(GLA_HEADS):
            sl = slice(h * GLA_DV, (h + 1) * GLA_DV)
            oh, rh, dyh = ov[:, sl], rv[:, sl], dyv[:, sl]
            rs = lax.rsqrt(jnp.mean(oh * oh, axis=-1, keepdims=True) + RMS_EPS)
            xh = oh * rs
            sg = jax.nn.sigmoid(rh)
            d_on = dyh * (rh * sg)
            drs.append(dyh * (xh * p[0]) * (sg * (1.0 + rh * (1.0 - sg))))
            dg = dg + _colsum(d_on * xh)
            dxh = d_on * p[0]
            dos.append(rs * (dxh - xh * jnp.mean(dxh * xh, axis=-1, keepdims=True)))
        return [jnp.concatenate(dos, axis=1), jnp.concatenate(drs, axis=1)], [dg]
    return _rowwise(fn, name="gla_post_bwd",
                    rows=[(dy, GLA_HV, 0), (o, GLA_HV, 0), (pin, GLA_HV, (2 * GLA_HK + GLA_HV) // GLA_HV)],
                    pars=[g], outs=[(GLA_HV, F32), (GLA_HV, BF16)], accs=[GLA_DV])


def _gla_gate_bwd(dla, la):
    def fn(r, p):
        dz = r[0] * (1.0 / GLA_TAU) * (1.0 - jnp.exp(GLA_TAU * r[1]))
        return [dz], [_colsum(dz)]
    return _rowwise(fn, name="gla_gate_bwd", rows=[(dla, GLA_HK, 0), (la, GLA_HK, 0)], outs=[(GLA_HK, BF16)],
                    accs=[GLA_HK])


def _log_sigmoid(z):
    return jnp.minimum(z, 0.0) - jnp.log(1.0 + jnp.exp(-jnp.abs(z)))


def _rot_half(v):
    lane = lax.broadcasted_iota(jnp.int32, v.shape, 1)
    return jnp.where(lane < 32, -pltpu.roll(v, 96, 1), jnp.where(lane < 64, pltpu.roll(v, 32, 1), 0.0))


def _rms(v):
    rs = lax.rsqrt(jnp.mean(v * v, axis=-1, keepdims=True) + RMS_EPS)
    return v * rs, rs


def _mla_norm_fwd(cin, gq, gkv, cosp, sinp):
    def fn(r, p):
        cv, cs, sn = r
        qn, _ = _rms(cv[:, :MLA_QR])
        kvn, _ = _rms(cv[:, MLA_QR:MLA_QR + MLA_KVR])
        kr = cv[:, MLA_QR + MLA_KVR:]
        return [qn * p[0], kvn * p[1], kr * cs + _rot_half(kr) * sn], []
    return _rowwise(fn, name="mla_norm_fwd", rows=[(cin, MLA_IN_PAD, 0), (cosp, 128, 0), (sinp, 128, 0)],
                    pars=[gq, gkv], outs=[(MLA_QR, BF16), (MLA_KVR, BF16), (128, BF16)])


def _mla_qrope_fwd(q, cosp, sinp):
    scale = (MLA_NOPE + MLA_ROPE) ** -0.5

    def fn(r, p):
        qv, cs, sn = r
        parts = []
        for h in range(MLA_HEADS):
            parts.append(qv[:, h * MLA_QH:h * MLA_QH + 128] * scale)
            rp = qv[:, h * MLA_QH + 128:(h + 1) * MLA_QH]
            parts.append((rp * cs + _rot_half(rp) * sn) * scale)
        return [jnp.concatenate(parts, axis=1)], []
    W = MLA_HEADS * MLA_QH
    return _rowwise(fn, name="mla_qrope_fwd", rows=[(q, W, 0), (cosp, 128, 0), (sinp, 128, 0)],
                    outs=[(W, BF16)])[0]


def _mla_qrope_bwd(dq, cosp, sinp):
    scale = (MLA_NOPE + MLA_ROPE) ** -0.5

    def fn(r, p):
        dv, cs, sn = r
        parts = []
        for h in range(MLA_HEADS):
            parts.append(dv[:, h * MLA_QH:h * MLA_QH + 128] * scale)
            rp = dv[:, h * MLA_QH + 128:(h + 1) * MLA_QH]
            parts.append((rp * cs - _rot_half(rp) * sn) * scale)
        return [jnp.concatenate(parts, axis=1)], []
    W = MLA_HEADS * MLA_QH
    return _rowwise(fn, name="mla_qrope_bwd", rows=[(dq, W, 0), (cosp, 128, 0), (sinp, 128, 0)],
                    outs=[(W, BF16)])[0]


def _mla_norm_bwd(cin, dqn, dkvn, dkr, gq, gkv, cosp, sinp):
    def fn(r, p):
        cv, dq_, dkv_, dkr_, cs, sn = r
        outs, accs = [], []
        for (lo, hi), dn, g in (((0, MLA_QR), dq_, p[0]), ((MLA_QR, MLA_QR + MLA_KVR), dkv_, p[1])):
            xh, rs = _rms(cv[:, lo:hi])
            dxh = dn * g
            outs.append(rs * (dxh - xh * jnp.mean(dxh * xh, axis=-1, keepdims=True)))
            accs.append(_colsum(dn * xh))
        dk = dkr_[:, 0:128]
        for h in range(1, MLA_HEADS):
            dk = dk + dkr_[:, h * 128:(h + 1) * 128]
        outs.append(dk * cs - _rot_half(dk) * sn)
        return [jnp.concatenate(outs, axis=1)], accs
    return _rowwise(fn, name="mla_norm_bwd",
                    rows=[(cin, MLA_IN_PAD, 0), (dqn, MLA_QR, 0), (dkvn, MLA_KVR, 0), (dkr, MLA_HEADS * 128, 0),
                          (cosp, 128, 0), (sinp, 128, 0)],
                    pars=[gq, gkv], outs=[(MLA_IN_PAD, BF16)], accs=[MLA_QR, MLA_KVR])


def _mla_probs(q, k, i, tq):
    s = _nt(q, k)
    row = (i * tq + lax.broadcasted_iota(jnp.int32, s.shape, 0)) // CHUNK
    col = lax.broadcasted_iota(jnp.int32, s.shape, 1) // CHUNK
    s = jnp.where(col <= row, s, -jnp.inf)
    e = jnp.exp(s - jnp.max(s, axis=-1, keepdims=True))
    return e / jnp.sum(e, axis=-1, keepdims=True)


def _mla_attn_fwd(qr, knv, kr, tq=256):
    S = qr.shape[0]
    tq = min(tq, S)

    def body(q_ref, kn_ref, v_ref, kr_ref, o_ref, k_cat):
        k_cat[:, :128] = kn_ref[...]
        k_cat[:, 128:] = kr_ref[...]
        for i in range(S // tq):
            rows, keys = pl.ds(i * tq, tq), pl.ds(0, (i + 1) * tq)
            pr = _mla_probs(q_ref[rows, :], k_cat[keys, :], i, tq)
            o_ref[rows, :] = _nn(pr.astype(BF16), v_ref[keys, :])

    return pl.pallas_call(
        body, name="mla_attn_fwd", grid=(MLA_HEADS,),
        in_specs=[pl.BlockSpec((S, MLA_QH), lambda h: (0, h)),
                  pl.BlockSpec((S, 128), lambda h: (0, h)),
                  pl.BlockSpec((S, 128), lambda h: (0, MLA_HEADS + h)),
                  pl.BlockSpec((S, 128), lambda h: (0, 0))],
        out_specs=pl.BlockSpec((S, 128), lambda h: (0, h)),
        out_shape=_out((S, MLA_HEADS * MLA_V), F32),
        scratch_shapes=[pltpu.VMEM((S, MLA_QH), BF16)],
        compiler_params=_cparams(("parallel",)),
    )(qr, knv, knv, kr)


def _mla_attn_bwd(qr, knv, kr, o, do, tq=256):
    S = qr.shape[0]
    tq = min(tq, S)
    W = MLA_HEADS * 128

    def body(q_ref, kn_ref, v_ref, kr_ref, o_ref, do_ref, dq_ref, dkn_ref, dv_ref, dkr_ref, k_cat, dk_acc, dv_acc):
        k_cat[:, :128] = kn_ref[...]
        k_cat[:, 128:] = kr_ref[...]
        dk_acc[...] = jnp.zeros(dk_acc.shape, F32)
        dv_acc[...] = jnp.zeros(dv_acc.shape, F32)
        for i in range(S // tq):
            rows, keys = pl.ds(i * tq, tq), pl.ds(0, (i + 1) * tq)
            q, k, v = q_ref[rows, :], k_cat[keys, :], v_ref[keys, :]
            pr = _mla_probs(q, k, i, tq)
            dov = do_ref[rows, :]
            delta = jnp.sum(dov * o_ref[rows, :], axis=-1, keepdims=True)
            dob = dov.astype(BF16)
            ds = (pr * (_nt(dob, v) - delta)).astype(BF16)
            dq_ref[rows, :] = _nn(ds, k)
            dk_acc[keys, :] += _tn(ds, q)
            dv_acc[keys, :] += _tn(pr.astype(BF16), dob)
        dkn_ref[...] = dk_acc[:, :128].astype(dkn_ref.dtype)
        dkr_ref[...] = dk_acc[:, 128:]
        dv_ref[...] = dv_acc[...].astype(dv_ref.dtype)

    head = lambda w: pl.BlockSpec((S, w), lambda h: (0, h))
    return pl.pallas_call(
        body, name="mla_attn_bwd", grid=(MLA_HEADS,),
        in_specs=[head(MLA_QH), head(128), pl.BlockSpec((S, 128), lambda h: (0, MLA_HEADS + h)),
                  pl.BlockSpec((S, 128), lambda h: (0, 0)), head(128), head(128)],
        out_specs=[head(MLA_QH), head(128), head(128), head(128)],
        out_shape=[_out((S, MLA_HEADS * MLA_QH), F32), _out((S, W), BF16), _out((S, W), BF16), _out((S, W), F32)],
        scratch_shapes=[pltpu.VMEM((S, MLA_QH), BF16), pltpu.VMEM((S, MLA_QH), F32), pltpu.VMEM((S, 128), F32)],
        compiler_params=_cparams(("parallel",)),
    )(qr, knv, knv, kr, o, do)


CONV_TILE = 256


def _shift_down(v, n):
    row = lax.broadcasted_iota(jnp.int32, v.shape, 0)
    return jnp.where(row >= n, pltpu.roll(v, n, 0), 0.0)


def _shift_up(v, n):
    S = v.shape[0]
    row = lax.broadcasted_iota(jnp.int32, v.shape, 0)
    return jnp.where(row < S - n, pltpu.roll(v, S - n, 0), 0.0)


def _conv_specs(S, n_extra_cols):
    nt = D_MODEL // CONV_TILE
    specs = [pl.BlockSpec((S, CONV_TILE), functools.partial(lambda j, o: (0, o + j), o=part * nt))
             for part in range(3)]
    specs.append(pl.BlockSpec((8, CONV_TILE), lambda j: (0, j)))
    specs += [pl.BlockSpec((S, CONV_TILE), lambda j: (0, j)) for _ in range(n_extra_cols)]
    return specs


def _conv_fwd(bcu, w8):
    S = bcu.shape[0]

    def body(b_ref, c_ref, u_ref, w_ref, y_ref):
        cu = c_ref[...] * u_ref[...]
        z = w_ref[2:3, :] * cu + w_ref[1:2, :] * _shift_down(cu, 1) + w_ref[0:1, :] * _shift_down(cu, 2)
        y_ref[...] = (b_ref[...] * z).astype(y_ref.dtype)

    return pl.pallas_call(
        body, name="conv_fwd", grid=(D_MODEL // CONV_TILE,), in_specs=_conv_specs(S, 0),
        out_specs=pl.BlockSpec((S, CONV_TILE), lambda j: (0, j)),
        out_shape=_out((S, D_MODEL), BF16),
        compiler_params=_cparams(("parallel",)),
    )(bcu, bcu, bcu, w8)


def _conv_bwd(bcu, w8, dy):
    S = bcu.shape[0]

    def body(b_ref, c_ref, u_ref, w_ref, dy_ref, db_ref, dc_ref, du_ref, dw_ref):
        b, c, u, dyv = b_ref[...], c_ref[...], u_ref[...], dy_ref[...]
        w0, w1, w2 = w_ref[0:1, :], w_ref[1:2, :], w_ref[2:3, :]
        cu = c * u
        cu1, cu2 = _shift_down(cu, 1), _shift_down(cu, 2)
        z = w2 * cu + w1 * cu1 + w0 * cu2
        dz = dyv * b
        db_ref[...] = (dyv * z).astype(db_ref.dtype)
        dcu = w2 * dz + w1 * _shift_up(dz, 1) + w0 * _shift_up(dz, 2)
        dc_ref[...] = (dcu * u).astype(dc_ref.dtype)
        du_ref[...] = (dcu * c).astype(du_ref.dtype)
        dw_ref[...] = jnp.zeros(dw_ref.shape, F32)
        dw_ref[0:1, :] = _colsum(dz * cu2)
        dw_ref[1:2, :] = _colsum(dz * cu1)
        dw_ref[2:3, :] = _colsum(dz * cu)

    col = pl.BlockSpec((S, CONV_TILE), lambda j: (0, j))
    return pl.pallas_call(
        body, name="conv_bwd", grid=(D_MODEL // CONV_TILE,), in_specs=_conv_specs(S, 1),
        out_specs=[col, col, col, pl.BlockSpec((8, CONV_TILE), lambda j: (0, j))],
        out_shape=[_out((S, D_MODEL), BF16)] * 3 + [_out((8, D_MODEL), F32)],
        compiler_params=_cparams(("parallel",)),
    )(bcu, bcu, bcu, w8, dy)


def _adamw_update(w, g, m, v):
    nm = ADAM_B1 * m + (1.0 - ADAM_B1) * g
    nv = ADAM_B2 * v + (1.0 - ADAM_B2) * jnp.square(g)
    m_hat = nm / (1.0 - ADAM_B1 ** ADAM_STEP)
    v_hat = nv / (1.0 - ADAM_B2 ** ADAM_STEP)
    return -ADAM_LR * (m_hat / (jnp.sqrt(v_hat) + ADAM_EPS) + ADAM_WD * w), nm, nv


def _adamw_shard_major(w, m, v, gs, name):
    view = lambda a: jnp.transpose(a, (2, 0, 1))
    g = jnp.stack([x.T for x in gs], axis=1)
    n, L, k = g.shape
    rows = n // 4
    assert n % 4 == 0

    def body(w_ref, m_ref, v_ref, g_ref, go_ref, d_ref, nm_ref, nv_ref):
        gv = g_ref[...]
        d_ref[...], nm_ref[...], nv_ref[...] = _adamw_update(w_ref[...], gv, m_ref[...], v_ref[...])
        go_ref[...] = gv

    spec = pl.BlockSpec((rows, L, k), lambda i: (i, 0, 0))
    outs = pl.pallas_call(
        body, name=name, grid=(4,), in_specs=[spec] * 4, out_specs=[spec] * 4,
        out_shape=[jax.ShapeDtypeStruct((n, L, k), F32)] * 4,
        compiler_params=_cparams(("parallel",)),
    )(view(w), view(m), view(v), g)
    return [jnp.transpose(o, (1, 2, 0)) for o in outs]


def _adamw_small(ws, gs, ms, vs):
    n = len(ws)

    def body(*refs):
        ins, outs = refs[:4 * n], refs[4 * n:]
        for t in range(n):
            w_ref, g_ref, m_ref, v_ref = (ins[k * n + t] for k in range(4))
            gv = g_ref[...]
            outs[4 * t][...] = gv
            outs[4 * t + 1][...], outs[4 * t + 2][...], outs[4 * t + 3][...] = _adamw_update(
                w_ref[...], gv, m_ref[...], v_ref[...])

    return pl.pallas_call(
        body, name="adamw_small",
        out_shape=[jax.ShapeDtypeStruct(a.shape, F32) for a in ws for _ in range(4)],
    )(*ws, *gs, *ms, *vs)


def _adamw(w, m, v, gs, name):
    L, R, Cn = w.shape
    assert len(gs) == L
    tr = R if R <= 256 else 256
    assert R % tr == 0

    def body(w_ref, m_ref, v_ref, *rest):
        g_refs, (go_ref, d_ref, nm_ref, nv_ref) = rest[:L], rest[L:]
        layer = pl.program_id(0)
        gv = g_refs[0][...]
        for k in range(1, L):
            gv = jnp.where(layer == k, g_refs[k][...], gv)
        d_ref[...], nm_ref[...], nv_ref[...] = _adamw_update(w_ref[...], gv, m_ref[...], v_ref[...])
        go_ref[...] = gv

    spec = pl.BlockSpec((None, tr, Cn), lambda l, i: (l, i, 0))
    g_specs = [pl.BlockSpec((tr, Cn), functools.partial(lambda l, i, k: (jnp.where(l == k, i, 0), 0), k=k))
               for k in range(L)]
    return pl.pallas_call(
        body, name=name, grid=(L, R // tr), in_specs=[spec] * 3 + g_specs, out_specs=[spec] * 4,
        out_shape=[jax.ShapeDtypeStruct((L, R, Cn), F32)] * 4,
        compiler_params=_cparams(("arbitrary", "arbitrary")),
    )(w, m, v, *gs)


HBM_SPEC = pl.BlockSpec(memory_space=pltpu.HBM)


def _place():
    return lax.axis_index("x"), lax.axis_index("y"), lax.axis_index("c")


def _other_chips(x, y):
    return [(1 - x, y), (x, 1 - y), (1 - x, 1 - y)]


SEM_SPEC = pl.BlockSpec(memory_space=pltpu.SEMAPHORE)
ANY_SPEC = pl.BlockSpec(memory_space=pl.ANY)
VMEM_SPEC = pl.BlockSpec(memory_space=pltpu.VMEM)
EFFECT = pltpu.SideEffectType.DATAFLOW_SIDE_EFFECTING
TOKEN = (8, 128)


def _ici_start(srcs, lands, after, copies, name, per_src=3):
    n, nl = len(srcs), len(lands)

    def body(*refs):
        src_refs, land_refs = refs[:n], refs[n:n + nl]
        send_sems, recv_sems, token = refs[n + nl + 1], refs[n + nl + 2], refs[-1]
        x, y, c = _place()
        for k, src, dst, to in copies(src_refs, land_refs, x, y, c):
            pltpu.make_async_remote_copy(src_ref=src, dst_ref=dst, send_sem=send_sems.at[k], recv_sem=recv_sems.at[k],
                                         device_id=to, device_id_type=MESH).start()
        token[...] = jnp.zeros(TOKEN, F32)

    n_copies = per_src * max(n, nl if n == 0 else 0)
    res = pl.pallas_call(
        body, name=name,
        out_shape=(pltpu.SemaphoreType.DMA((n_copies,)), pltpu.SemaphoreType.DMA((n_copies,)),
                   *[pltpu.HBM(s.shape, s.dtype) for s in srcs], *[pltpu.HBM(l.shape, l.dtype) for l in lands],
                   jax.ShapeDtypeStruct(TOKEN, F32)),
        in_specs=[HBM_SPEC] * (n + nl) + [ANY_SPEC],
        out_specs=(SEM_SPEC, SEM_SPEC, *[HBM_SPEC] * (n + nl), VMEM_SPEC),
        input_output_aliases={t: 2 + t for t in range(n + nl)},
        compiler_params=pltpu.CompilerParams(has_side_effects=EFFECT),
    )(*[_hbm(s) for s in srcs], *[_hbm(l) for l in lands], after)
    return res[0], res[1], list(res[2:2 + n]), list(res[2 + n:2 + n + nl]), res[-1]


def _ici_wait(handle, after, copies, name):
    send_sems, recv_sems, srcs, lands, _ = handle
    n, nl = len(srcs), len(lands)

    def body(*refs):
        src_refs, land_refs = refs[:n], refs[n:n + nl]
        send_s, recv_s = refs[n + nl], refs[n + nl + 1]
        x, y, c = _place()
        for k, src, dst, to in copies(src_refs, land_refs, x, y, c):
            cp = pltpu.make_async_remote_copy(src_ref=src, dst_ref=dst, send_sem=send_s.at[k], recv_sem=recv_s.at[k],
                                              device_id=to, device_id_type=MESH)
            cp.wait_send()
            cp.wait_recv()

    res = pl.pallas_call(
        body, name=name,
        out_shape=(*[pltpu.HBM(s.shape, s.dtype) for s in srcs], *[pltpu.HBM(l.shape, l.dtype) for l in lands]),
        in_specs=[HBM_SPEC] * (n + nl) + [SEM_SPEC, SEM_SPEC, ANY_SPEC],
        out_specs=tuple([HBM_SPEC] * (n + nl)),
        input_output_aliases={t: t for t in range(n + nl)},
        compiler_params=pltpu.CompilerParams(has_side_effects=EFFECT),
    )(*srcs, *lands, send_sems, recv_sems, after)
    return list(res[:n]), list(res[n:])


def _gather_copies(halves, arriving):
    def copies(src_refs, land_refs, x, y, c):
        q = 2 * x + y
        out = []
        for t, H in enumerate(halves):
            mine = land_refs[t].at[q, pl.ds(c * H, H), :]
            for j, (cx, cy) in enumerate(_other_chips(x, y)):
                theirs = land_refs[t].at[2 * cx + cy, pl.ds(c * H, H), :]
                out.append((3 * t + j, mine, theirs if arriving else mine, (cx, cy, c)))
        return out
    return copies


def _place_own(ops, after, name):
    n = len(ops)
    kinds = sorted({(o.shape, str(o.dtype)) for o in ops})
    kind_of = [kinds.index((o.shape, str(o.dtype))) for o in ops]

    def body(*refs):
        in_refs, out_refs = refs[:n], refs[n + 1:2 * n + 1]
        rd_sems, wr_sems, bufs = refs[2 * n + 1], refs[2 * n + 2], refs[2 * n + 3:]
        x, y, _ = _place()
        used = [0] * len(kinds)
        slot, busy = [], {}
        for t in range(n):
            slot.append((kind_of[t], used[kind_of[t]] % 2))
            used[kind_of[t]] += 1
        rd = lambda t: pltpu.make_async_copy(in_refs[t], bufs[slot[t][0]].at[slot[t][1]], rd_sems.at[t])
        wr = lambda t: pltpu.make_async_copy(bufs[slot[t][0]].at[slot[t][1]], out_refs[t].at[2 * x + y],
                                             wr_sems.at[t])
        rd(0).start()
        for t in range(n):
            rd(t).wait()
            wr(t).start()
            busy[slot[t]] = t
            if t + 1 < n:
                if slot[t + 1] in busy:
                    wr(busy.pop(slot[t + 1])).wait()
                rd(t + 1).start()
        for t in busy.values():
            wr(t).wait()

    return pl.pallas_call(
        body, name=name, in_specs=[HBM_SPEC] * n + [ANY_SPEC], out_specs=[HBM_SPEC] * n,
        out_shape=[jax.ShapeDtypeStruct((N_CHIPS,) + o.shape, o.dtype) for o in ops],
        scratch_shapes=[pltpu.SemaphoreType.DMA((n,)), pltpu.SemaphoreType.DMA((n,))]
        + [pltpu.VMEM((2,) + shape, jnp.dtype(dt)) for shape, dt in kinds],
        compiler_params=pltpu.CompilerParams(vmem_limit_bytes=VMEM_LIMIT),
    )(*ops, after)


def _gather_start(lands, after, name):
    return _ici_start([], lands, after, _gather_copies([l.shape[1] // 2 for l in lands], False), name)


def _gather_wait(handle, after, name):
    halves = [l.shape[1] // 2 for l in handle[3]]
    return _ici_wait(handle, after, _gather_copies(halves, True), name)


def _forward_copies(halves, arriving):
    def copies(src_refs, land_refs, x, y, c):
        out = []
        for t, H in enumerate(halves):
            for j, (cx, cy) in enumerate(_other_chips(x, y)):
                mine = land_refs[t].at[2 * cx + cy, pl.ds(c * H, H), :]
                theirs = land_refs[t].at[2 * cx + cy, pl.ds((1 - c) * H, H), :]
                out.append((3 * t + j, mine, theirs if arriving else mine, (x, y, 1 - c)))
        return out
    return copies


def _forward_start(lands, after, name):
    halves = [l.shape[1] // 2 for l in lands]
    return _ici_start([], lands, after, _forward_copies(halves, False), name)


def _forward_wait(handle, after, name):
    halves = [l.shape[1] // 2 for l in handle[3]]
    return _ici_wait(handle, after, _forward_copies(halves, True), name)[1]


def _swap_halves(ops, name):
    n = len(ops)

    def body(*refs):
        in_refs, out_refs, send_sems, recv_sems = refs[:n], refs[n:2 * n], refs[2 * n], refs[2 * n + 1]
        x, y, c = _place()
        cps = []
        for t in range(n):
            H = ops[t].shape[1] // 2
            cp = pltpu.make_async_remote_copy(src_ref=in_refs[t].at[:, pl.ds((1 - c) * H, H), :],
                                              dst_ref=out_refs[t], send_sem=send_sems.at[t],
                                              recv_sem=recv_sems.at[t], device_id=(x, y, 1 - c),
                                              device_id_type=MESH)
            cp.start()
            cps.append(cp)
        for cp in cps:
            cp.wait()

    return pl.pallas_call(
        body, name=name, in_specs=[HBM_SPEC] * n, out_specs=[HBM_SPEC] * n,
        out_shape=[jax.ShapeDtypeStruct((N_CHIPS, o.shape[1] // 2, o.shape[2]), o.dtype) for o in ops],
        scratch_shapes=[pltpu.SemaphoreType.DMA((n,)), pltpu.SemaphoreType.DMA((n,))],
    )(*ops)


def _sum_rows_tile(h):
    return h if h <= 512 else 512


def _pair_sum(g, t, cq, name):
    _, a, b = g.shape
    H = a // 2
    tr = _sum_rows_tile(H)

    def body(cq_ref, g_ref, t_ref, o_ref):
        o_ref[...] = (g_ref[...].astype(F32) + t_ref[...].astype(F32)).astype(o_ref.dtype)

    grid_spec = pltpu.PrefetchScalarGridSpec(
        num_scalar_prefetch=1, grid=(N_CHIPS, H // tr),
        in_specs=[pl.BlockSpec((None, None, tr, b), lambda j, i, cq_ref: (j, cq_ref[0], i, 0)),
                  pl.BlockSpec((None, tr, b), lambda j, i, cq_ref: (j, i, 0))],
        out_specs=pl.BlockSpec((None, tr, b), lambda j, i, cq_ref: (j, i, 0)))
    return pl.pallas_call(
        body, name=name, grid_spec=grid_spec, out_shape=_out(t.shape, BF16),
        compiler_params=_cparams(("parallel", "parallel")),
    )(cq, g.reshape(N_CHIPS, 2, H, b), t)


def _scatter_copies(src_refs, land_refs, x, y, c):
    out = []
    for j, (cx, cy) in enumerate(_other_chips(x, y)):
        for t in range(len(src_refs)):
            out.append((3 * t + j, src_refs[t].at[2 * cx + cy], land_refs[t].at[j], (cx, cy, c)))
    return out


def _scatter_start(ops, after, name):
    lands = [lax.empty((3,) + o.shape[1:], o.dtype) for o in ops]
    return _ici_start(ops, lands, after, _scatter_copies, name)


def _scatter_wait(handle, after, name):
    return _ici_wait(handle, after, _scatter_copies, name)


def _chip_sum(p, t, cq, name):
    _, H, b = p.shape
    tr = _sum_rows_tile(H)

    def body(cq_ref, p_ref, t_ref, o_ref):
        acc = p_ref[...].astype(F32)
        for j in range(3):
            acc = acc + t_ref[j].astype(F32)
        o_ref[...] = acc

    grid_spec = pltpu.PrefetchScalarGridSpec(
        num_scalar_prefetch=1, grid=(H // tr,),
        in_specs=[pl.BlockSpec((None, tr, b), lambda i, cq_ref: (cq_ref[1], i, 0)),
                  pl.BlockSpec((3, tr, b), lambda i, cq_ref: (0, i, 0))],
        out_specs=pl.BlockSpec((None, tr, b), lambda i, cq_ref: (cq_ref[0], i, 0)))
    out = pl.pallas_call(
        body, name=name, grid_spec=grid_spec, out_shape=_out((2, H, b), F32),
        compiler_params=_cparams(("parallel",)),
    )(cq, p, t)
    return out.reshape(2 * H, b)


def _join_copies(arriving):
    def copies(src_refs, land_refs, x, y, c):
        out = []
        for t, land in enumerate(land_refs):
            H = land.shape[0] // 2
            mine, theirs = land.at[pl.ds(c * H, H), :], land.at[pl.ds((1 - c) * H, H), :]
            out.append((t, mine, theirs if arriving else mine, (x, y, 1 - c)))
        return out
    return copies


def _join_start(fs, name):
    return _ici_start([], fs, jnp.zeros(TOKEN, F32), _join_copies(False), name, per_src=1)


def _join_wait(handle, after, name):
    return _ici_wait(handle, after, _join_copies(True), name)[1]


def _direct_copies(src_refs, land_refs, x, y, c):
    out = []
    for t in range(len(src_refs)):
        H = src_refs[t].shape[1] // 2
        for k in range(1, 8):
            px, py, pc = x ^ (k >> 2), y ^ ((k >> 1) & 1), c ^ (k & 1)
            out.append((7 * t + k - 1, src_refs[t].at[2 * px + py, pl.ds(pc * H, H), :], land_refs[t].at[k - 1],
                        (px, py, pc)))
    return out


def _direct_sum(g, t, cq, name):
    _, a, b = g.shape
    H = a // 2
    tr = _sum_rows_tile(H)

    def body(cq_ref, g_ref, t_ref, o_ref):
        acc = g_ref[...].astype(F32)
        for k in range(7):
            acc = acc + t_ref[k].astype(F32)
        o_ref[...] = acc

    grid_spec = pltpu.PrefetchScalarGridSpec(
        num_scalar_prefetch=1, grid=(H // tr,),
        in_specs=[pl.BlockSpec((None, None, tr, b), lambda i, cq_ref: (cq_ref[1], cq_ref[0], i, 0)),
                  pl.BlockSpec((7, tr, b), lambda i, cq_ref: (0, i, 0))],
        out_specs=pl.BlockSpec((None, tr, b), lambda i, cq_ref: (cq_ref[0], i, 0)))
    out = pl.pallas_call(
        body, name=name, grid_spec=grid_spec, out_shape=_out((2, H, b), F32),
        compiler_params=_cparams(("parallel",)),
    )(cq, g.reshape(N_CHIPS, 2, H, b), t)
    return out.reshape(a, b)


def _reduce_direct_start(gs, tag):
    lands = [lax.empty((7, g.shape[1] // 2, g.shape[2]), g.dtype) for g in gs]
    return _ici_start(gs, lands, jnp.zeros(TOKEN, F32), _direct_copies, "rs_direct_start_" + tag, per_src=7)


def _reduce_direct_finish(handle, cq, after, tag):
    gs, rs = _ici_wait(handle, after, _direct_copies, "rs_direct_wait_" + tag)
    fs = [_direct_sum(g, r, cq, "rs_direct_sum") for g, r in zip(gs, rs)]
    return _join_start(fs, "rs_join_start_" + tag)


def _reduce_scatter_start(gs, cq, after, tag):
    ts = _swap_halves(gs, "rs_swap_" + tag)
    ps = [_pair_sum(g, t, cq, "rs_pair_sum") for g, t in zip(gs, ts)]
    return _scatter_start(ps, after, "rs_scatter_start_" + tag)


def _reduce_scatter_finish(handle, cq, after, tag):
    ps, rs = _scatter_wait(handle, after, "rs_scatter_wait_" + tag)
    fs = [_chip_sum(p, r, cq, "rs_chip_sum") for p, r in zip(ps, rs)]
    return _join_start(fs, "rs_join_start_" + tag)


def _all_reduce_small(v):
    n = v.shape[0]

    def body(v_ref, out_ref, buf, send_sems, recv_sems):
        x, y, c = _place()
        me = 4 * x + 2 * y + c
        buf[me] = v_ref[...]
        cps = []
        for k in range(1, 8):
            peer = (x ^ (k >> 2), y ^ ((k >> 1) & 1), c ^ (k & 1))
            cp = pltpu.make_async_remote_copy(src_ref=v_ref, dst_ref=buf.at[me], send_sem=send_sems.at[k - 1],
                                              recv_sem=recv_sems.at[k - 1], device_id=peer, device_id_type=MESH)
            cp.start()
            cps.append(cp)
        for k in range(1, 8):
            px, py, pc = x ^ (k >> 2), y ^ ((k >> 1) & 1), c ^ (k & 1)
            land = buf.at[4 * px + 2 * py + pc]
            pltpu.make_async_remote_copy(src_ref=land, dst_ref=land, send_sem=send_sems.at[k - 1],
                                         recv_sem=recv_sems.at[k - 1], device_id=(px, py, pc),
                                         device_id_type=MESH).wait_recv()
        for cp in cps:
            cp.wait_send()
        acc = buf[0]
        for d in range(1, 8):
            acc = acc + buf[d]
        out_ref[...] = acc

    vm = pl.BlockSpec(memory_space=pltpu.VMEM)
    return pl.pallas_call(
        body, name="all_reduce_small", in_specs=[vm], out_specs=vm,
        out_shape=jax.ShapeDtypeStruct((n, 128), F32),
        scratch_shapes=[pltpu.VMEM((8, n, 128), F32), pltpu.SemaphoreType.DMA((7,)), pltpu.SemaphoreType.DMA((7,))],
    )(v)


SMALL_GATHER = (16, 1024)
SMALL_FULL = sum(_size(_full_shape(n)) for n in SMALL)
SMALL_FULL_ROWS = -(-(SMALL_FULL + 1) // 128 // 8) * 8


def _layer_shards(w, i, q):
    kind, j = MIXER[i % 3], i // 3
    out = {n: w[n][i].astype(BF16) for n in COMMON_BIG}
    if kind == 'gla':
        win = jnp.zeros((D_MODEL, GLA_WIN), F32)
        win = lax.dynamic_update_slice(win, w['gla_w_in'][j], (0, (GLA_SHARD - GLA_WIN_STEP) * q))
        out['gla_w_in'] = win.astype(BF16)
        out['gla_w_out'] = w['gla_w_out'][j].astype(BF16)
    elif kind == 'mla':
        out['mla_w_in'] = jnp.pad(w['mla_w_in'][j], ((0, 0), (0, MLA_IN_PAD - MLA_IN))).astype(BF16)
        for n in ('mla_w_uq', 'mla_w_ukv', 'mla_w_out'):
            out[n] = w[n][j].astype(BF16)
    else:
        out['conv_w_in'] = w['conv_w_in'][j].astype(BF16)
        out['conv_w_out'] = w['conv_w_out'][j].astype(BF16)
    return out


def _rows_joined(g):
    return g.reshape(g.shape[0] * g.shape[1], g.shape[2])


def _cols_joined(g):
    return jnp.moveaxis(g, 0, 1).reshape(g.shape[1], -1)


def _layer_weights(g, i):
    kind = MIXER[i % 3]
    W = {}
    if 'mlp_w1' in g:
        W = {'w1': g['mlp_w1'], 'w2': _rows_joined(g['mlp_w2']), 'gate': _rows_joined(g['ple_w_gate']),
             'proj': g['ple_w_proj']}
    if kind == 'gla' and 'gla_w_out' in g:
        W['w_out'] = _rows_joined(g['gla_w_out'])
    if kind == 'gla' and 'gla_w_in' in g:
        parts = []
        for qq in range(N_CHIPS):
            lo = g['gla_w_in'][qq][:, :128]
            if qq > 0:
                lo = lo + g['gla_w_in'][qq - 1][:, GLA_WIN_STEP:]
            parts += [lo, g['gla_w_in'][qq][:, 128:GLA_WIN_STEP]]
        parts.append(g['gla_w_in'][N_CHIPS - 1][:, GLA_WIN_STEP:])
        W['w_in'] = jnp.concatenate(parts, axis=1)
    elif kind == 'mla':
        W['w_in'] = _rows_joined(g['mla_w_in'])
        uq = _cols_joined(g['mla_w_uq']).reshape(MLA_QR, MLA_HEADS, MLA_NOPE + MLA_ROPE)
        W['w_uq'] = jnp.pad(uq, ((0, 0), (0, 0), (0, MLA_QH - MLA_NOPE - MLA_ROPE))).reshape(MLA_QR, -1)
        ukv = _cols_joined(g['mla_w_ukv']).reshape(MLA_KVR, MLA_HEADS, 2, 128)
        W['w_ukv'] = ukv.transpose(0, 2, 1, 3).reshape(MLA_KVR, -1)
        W['w_out'] = _rows_joined(g['mla_w_out'])
    elif kind == 'conv':
        W['w_in'] = g['conv_w_in']
        W['w_out'] = _rows_joined(g['conv_w_out'])
    return W


def _pack_small_shards(w):
    flat = jnp.concatenate([w[n].reshape(-1) for n in SMALL_SHARDED])
    return jnp.pad(flat, (0, _size(SMALL_GATHER) - flat.shape[0])).reshape(SMALL_GATHER)


def _unpack_small_gathered(g):
    flat, out, off = g.reshape(N_CHIPS, -1), {}, 0
    for n in SMALL_SHARDED:
        shape, ax = WSPEC[n]
        seg = flat[:, off:off + _size(shape)].reshape((N_CHIPS,) + shape)
        out[n] = jnp.moveaxis(seg, 0, ax).reshape(_full_shape(n))
        off += _size(shape)
    return out


def _pack_small(vals, loss):
    flat = jnp.concatenate([vals[n].reshape(-1) for n in SMALL] + [loss.reshape(1)])
    return jnp.pad(flat, (0, SMALL_FULL_ROWS * 128 - flat.shape[0])).reshape(SMALL_FULL_ROWS, 128)


def _unpack_small(packed, q):
    flat = packed.reshape(-1)
    out, off = {}, 0
    for n in SMALL:
        shape, ax = WSPEC[n]
        full = flat[off:off + _size(_full_shape(n))].reshape(_full_shape(n))
        off += _size(_full_shape(n))
        out[n] = full if ax is None else lax.dynamic_slice_in_dim(full, q * shape[ax], shape[ax], axis=ax)
    return out


def _row_shards(dw):
    return dw.reshape(N_CHIPS, dw.shape[0] // N_CHIPS, dw.shape[1])


def _col_shards(dw):
    return jnp.moveaxis(dw.reshape(dw.shape[0], N_CHIPS, -1), 1, 0)


def _row(v):
    return v.reshape(1, -1)


def _layer_fwd(i, xin, xin_b, p_i, W, sm, cosp, sinp, rest=None, mid=None):
    kind, j = MIXER[i % 3], i // 3
    sv = {'xin': xin, 'xin_b': xin_b}
    if kind == 'gla':
        w_up = jnp.pad(sm['gla_w_gate_up'][j].astype(BF16), ((0, 128 - GLA_RANK), (0, 0)))
        pin = _mm(xin_b, W['w_in'], name="gla_in", tn=640, tm=FULL_ROWS)
        la = _mm(pin, w_up, name="gla_gate", K=128, tk=128, a_off=(0, (GLA_IN_PAD - 128) // 128), tn=512,
                 extras=[(_row(sm['gla_b_gate'][j]), 'n')],
                 epilogue=lambda acc, b: (_log_sigmoid(acc + b) * (1.0 / GLA_TAU),))
        o, states = _gla_fwd(pin, la)
        yb = _gla_post_fwd(o, pin, _row(sm['gla_norm_g'][j]))
        if rest is not None:
            W = {**W, **rest(yb)}
        mixed = yb
        sv.update(w_up=w_up, pin=pin, la=la, o=o, states=states, yb=yb)
    elif kind == 'mla':
        gq, gkv = sm['mla_q_norm'][j:j + 1], sm['mla_kv_norm'][j:j + 1]
        cin = _mm(xin_b, W['w_in'], name="mla_in", tn=640, tm=FULL_ROWS)
        qn, kvn, kr = _mla_norm_fwd(cin, gq, gkv, cosp, sinp)
        qr = _mla_qrope_fwd(_mm(qn, W['w_uq'], name="mla_uq"), cosp, sinp)
        knv = _mm(kvn, W['w_ukv'], name="mla_ukv", out_dtypes=(BF16,))
        o = _mla_attn_fwd(qr, knv, kr)
        ob = o.astype(BF16)
        mixed = ob
        sv.update(gq=gq, gkv=gkv, cin=cin, qn=qn, kvn=kvn, kr=kr, qr=qr, knv=knv, o=o, ob=ob)
    else:
        w8 = jnp.pad(sm['conv_w'][j], ((0, 5), (0, 0)))
        bcu = _mm(xin_b, W['w_in'], name="conv_in", tn=768, b_sh=True, tm=FULL_ROWS)
        yb = _conv_fwd(bcu, w8)
        mixed = yb
        sv.update(w8=w8, bcu=bcu, yb=yb)
    g0, b0 = _row(sm['ln_g'][i, 0]), _row(sm['ln_b'][i, 0])
    g1, b1 = _row(sm['ln_g'][i, 1]), _row(sm['ln_b'][i, 1])
    ln = dict(tm=512, tn=D_MODEL, out_dtypes=(F32, BF16, F32), epilogue=_ln_fwd_epilogue)
    x1, x1b, v0 = _mm(mixed, W['w_out'], name="mix_out_ln", extras=[(xin, 'mn'), (g0, 'n'), (b0, 'n')], **ln)
    ab = _mm(x1b, W['w1'], name="mlp_up", out_dtypes=(BF16,), b_sh=True, tm=FULL_ROWS,
             epilogue=lambda acc: (jnp.square(jnp.maximum(acc, 0.0)),))
    x2, x2b, v1 = _mm(ab, W['w2'], name="mlp_down_ln", tk=D_FF, extras=[(x1, 'mn'), (g1, 'n'), (b1, 'n')], **ln)
    order = [(mid(x2b), 'whole')] if mid else []
    pp = _mm(p_i, W['proj'], name="ple_proj", tn=256, b_sh=True, extras=order,
             epilogue=lambda acc, *unused: (acc,))
    z, x3, x3b = _mm(x2b, W['gate'], name="ple_gate", out_dtypes=(F32, F32, BF16),
                     extras=[(x2, 'mn'), (pp, 'mn')],
                     epilogue=lambda acc, xv, pv: (acc,) + (xv + jax.nn.sigmoid(acc) * pv,) * 2)
    sv.update(v0=v0, x1b=x1b, ab=ab, v1=v1, x2b=x2b, pp=pp, z=z, g0=g0, g1=g1)
    return x3, x3b, sv, W


def _layer_bwd(i, grads_in, p_i, W, sm, sv, cosp, sinp, token, early=None, below=None):
    kind, j = MIXER[i % 3], i // 3
    big, small = {}, {}
    dx, dpp_b, dz_b = grads_in
    big['ple_w_proj'] = _mm(p_i, dpp_b, ta=True, name="ple_proj_dw", tn=256, out_sh=True, out_dtypes=(BF16,))
    big['ple_w_gate'] = _row_shards(_mm(sv['x2b'], dz_b, ta=True, name="dw_dd", out_dtypes=(BF16,)))
    ln = dict(tb=True, tm=512, tn=D_MODEL, out_dtypes=(F32, BF16), n_sums=2)
    (dv1, dv1b), (dg1, db1) = _mm(dz_b, W['gate'], name="ple_gate_dx_ln", epilogue=_ln_bwd_epilogue(1.0),
                                  extras=[(dx, 'mn'), (sv['v1'], 'mn'), (sv['g1'], 'n'), (token, 'whole')], **ln)
    big['mlp_w2'] = _row_shards(_mm(sv['ab'], dv1b, ta=True, name="mlp_down_dw", out_dtypes=(BF16,)))
    dub = _mm(dv1b, W['w2'], tb=True, name="mlp_down_dx", out_dtypes=(BF16,), tm=FULL_ROWS,
              extras=[(sv['ab'], 'mn')], epilogue=lambda acc, a: (acc * (2.0 * jnp.sqrt(a.astype(F32))),))
    big['mlp_w1'] = _mm(sv['x1b'], dub, ta=True, name="mlp_up_dw", out_sh=True, out_dtypes=(BF16,))
    order = []
    if early is not None:
        order, big = [(early(big), 'whole')], {}
    (dv0, dv0b), (dg0, db0) = _mm(dub, W['w1'], name="mlp_up_dx_ln", b_sh=True, tk=D_FF, epilogue=_ln_bwd_epilogue(ALPHA),
                                  extras=[(dv1, 'mn'), (sv['v0'], 'mn'), (sv['g0'], 'n')] + order, **ln)
    small['ln_g'] = jnp.stack([dg0[0], dg1[0]])
    small['ln_b'] = jnp.stack([db0[0], db1[0]])
    resid = dict(tb=True, tn=D_MODEL, tm=512 if below else 1024, epilogue=_input_grad_epilogue,
                 extras=[(dv0, 'mn')] + [(a, 'mn') for a in below or ()],
                 out_dtypes=(F32, BF16, BF16) if below else (F32,))
    if kind == 'gla':
        big['gla_w_out'] = _row_shards(_mm(sv['yb'], dv0b, ta=True, name="dw_dd", out_dtypes=(BF16,)))
        dy = _mm(dv0b, W['w_out'], tb=True, name="dx_dd", tn=1024)
        do, dr_b, dng = _gla_post_bwd(dy, sv['o'], sv['pin'], _row(sm['gla_norm_g'][j]))
        dq_b, dk_b, dvv_b, dla = _gla_bwd(sv['pin'], sv['la'], sv['states'], do)
        dzg_b, dbg = _gla_gate_bwd(dla, sv['la'])
        dw_up = _mm(sv['pin'], dzg_b, ta=True, name="gla_gate_dw", M=128, tm=128,
                    a_off=(0, (GLA_IN_PAD - 128) // 128))
        dglr_b = _mm(dzg_b, sv['w_up'], tb=True, name="gla_gate_dx", out_dtypes=(BF16,))
        dpin_b = jnp.concatenate([dq_b, dk_b, dvv_b, dr_b, dglr_b], axis=1)
        dw_in = _mm(sv['xin_b'], dpin_b, ta=True, name="gla_in_dw", tn=640, out_dtypes=(BF16,))
        dxin = _mm(dpin_b, W['w_in'], name="gla_in_dx", tk=GLA_IN_PAD, **resid)
        big['gla_w_in'] = jnp.stack([dw_in[:, GLA_WIN_STEP * qq:GLA_WIN_STEP * qq + GLA_WIN]
                                     for qq in range(N_CHIPS)])
        small.update(gla_w_gate_up=dw_up[:GLA_RANK], gla_b_gate=dbg[0], gla_norm_g=dng[0])
    elif kind == 'mla':
        big['mla_w_out'] = _row_shards(_mm(sv['ob'], dv0b, ta=True, name="dw_dd", out_dtypes=(BF16,)))
        do = _mm(dv0b, W['w_out'], tb=True, name="dx_dd", tn=1024)
        dqr, dkn_b, dvv_b, dkr = _mla_attn_bwd(sv['qr'], sv['knv'], sv['kr'], sv['o'], do)
        dq_b = _mla_qrope_bwd(dqr, cosp, sinp)
        dw_uq = _mm(sv['qn'], dq_b, ta=True, name="mla_up_dw", out_dtypes=(BF16,))
        dqn = _mm(dq_b, W['w_uq'], tb=True, name="mla_up_dx")
        dknv_b = jnp.concatenate([dkn_b, dvv_b], axis=1)
        dw_ukv = _mm(sv['kvn'], dknv_b, ta=True, name="mla_up_dw", out_dtypes=(BF16,))
        dkvn = _mm(dknv_b, W['w_ukv'], tb=True, name="mla_up_dx")
        dcin_b, dgq, dgkv = _mla_norm_bwd(sv['cin'], dqn, dkvn, dkr, sv['gq'], sv['gkv'], cosp, sinp)
        big['mla_w_in'] = _row_shards(_mm(sv['xin_b'], dcin_b, ta=True, name="mla_in_dw", tn=640,
                                          out_dtypes=(BF16,)))
        dxin = _mm(dcin_b, W['w_in'], name="mla_in_dx", tk=MLA_IN_PAD, **resid)
        big['mla_w_uq'] = _col_shards(
            dw_uq.reshape(MLA_QR, MLA_HEADS, MLA_QH)[:, :, :MLA_NOPE + MLA_ROPE].reshape(MLA_QR, -1))
        big['mla_w_ukv'] = _col_shards(
            dw_ukv.reshape(MLA_KVR, 2, MLA_HEADS, 128).transpose(0, 2, 1, 3).reshape(MLA_KVR, -1))
        small.update(mla_q_norm=dgq[0], mla_kv_norm=dgkv[0])
    else:
        big['conv_w_out'] = _row_shards(_mm(sv['yb'], dv0b, ta=True, name="dw_dd", out_dtypes=(BF16,)))
        dy = _mm(dv0b, W['w_out'], tb=True, name="dx_dd", tn=1024)
        db_b, dc_b, du_b, dw8 = _conv_bwd(sv['bcu'], sv['w8'], dy)
        dbcu_b = jnp.concatenate([db_b, dc_b, du_b], axis=1)
        big['conv_w_in'] = _mm(sv['xin_b'], dbcu_b, ta=True, name="conv_in_dw", tn=768, out_sh=True,
                               out_dtypes=(BF16,))
        dxin = _mm(dbcu_b, W['w_in'], name="conv_in_dx", tk=3 * D_MODEL, b_sh=True, **resid)
        small['conv_w'] = dw8[:3]
    return (dxin if below else (dxin,)), big, small


def _rope_tables(positions):
    inv_freq = ROPE_BASE ** (-jnp.arange(0, MLA_ROPE // 2, dtype=F32) * (2.0 / MLA_ROPE))
    ang = positions.astype(F32)[:, None] * inv_freq
    zeros = jnp.zeros((positions.shape[0], 64), F32)
    return (jnp.concatenate([jnp.cos(ang), jnp.cos(ang), zeros], axis=1),
            jnp.concatenate([jnp.sin(ang), jnp.sin(ang), zeros], axis=1))


FIRST_NEEDED = ['gla_w_in']


def _start_gathers(w, q):
    token, started = jnp.zeros(TOKEN, F32), []
    for i in range(DEPTH):
        sh = _layer_shards(w, i, q)
        for k, names in enumerate([list(sh)] if i > 0 else [FIRST_NEEDED, [n for n in sh if n not in FIRST_NEEDED]]):
            ops = [sh[n] for n in names]
            if i == 0 and k == 0:
                ops.append(_pack_small_shards(w))
            tag = "l%d%s" % (i, "ab"[k] if i == 0 else "")
            handle = _gather_start(_place_own(ops, token, "ag_own_" + tag), token, "ag_start_" + tag)
            token = handle[4]
            started.append((handle, names, tag))
    return started, token


def _pass_on(entry, after):
    handle, names, tag = entry
    _, lands = _gather_wait(handle, after, "ag_wait_" + tag)
    passing = _forward_start(lands, jnp.zeros(TOKEN, F32), "ag_pass_start_" + tag)
    return (passing, names, tag), passing[4]


def _gathered(passed, after):
    passing, names, tag = passed
    got = _forward_wait(passing, after, "ag_pass_wait_" + tag)
    return dict(zip(names, got)), got[-1]


def _local_shard_grad(name, g, q):
    if name == 'gla_w_in':
        return lax.dynamic_slice_in_dim(g, (GLA_SHARD - GLA_WIN_STEP) * q, GLA_SHARD, axis=1)
    if name == 'mla_w_in':
        return g[:, :MLA_IN]
    return g


def kernel(x, p, positions, gla_w_in, gla_w_gate_up, gla_b_gate, gla_norm_g, gla_w_out, mla_w_in, mla_q_norm, mla_kv_norm, mla_w_uq, mla_w_ukv, mla_w_out, conv_w_in, conv_w, conv_w_out, ln_g, ln_b, mlp_w1, mlp_w2, ple_w_gate, ple_w_proj, loss_target, m_gla_w_in, m_gla_w_gate_up, m_gla_b_gate, m_gla_norm_g, m_gla_w_out, m_mla_w_in, m_mla_q_norm, m_mla_kv_norm, m_mla_w_uq, m_mla_w_ukv, m_mla_w_out, m_conv_w_in, m_conv_w, m_conv_w_out, m_ln_g, m_ln_b, m_mlp_w1, m_mlp_w2, m_ple_w_gate, m_ple_w_proj, v_gla_w_in, v_gla_w_gate_up, v_gla_b_gate, v_gla_norm_g, v_gla_w_out, v_mla_w_in, v_mla_q_norm, v_mla_kv_norm, v_mla_w_uq, v_mla_w_ukv, v_mla_w_out, v_conv_w_in, v_conv_w, v_conv_w_out, v_ln_g, v_ln_b, v_mlp_w1, v_mlp_w2, v_ple_w_gate, v_ple_w_proj):
    args = locals()
    w = {n: args[n] for n in WNAMES}
    m = {n: args['m_' + n] for n in WNAMES}
    v = {n: args['v_' + n] for n in WNAMES}
    q = 2 * lax.axis_index("x") + lax.axis_index("y")
    cq = jnp.stack([lax.axis_index("c"), q]).astype(jnp.int32)

    cosp, sinp = _rope_tables(positions[0])
    started, after = _start_gathers(w, q)
    xin, saved, layers, sm = x[0], [], [], None
    xin_b = xin.astype(BF16)
    passed, after = _pass_on(started[0], after)
    for i in range(DEPTH):
        got, last = _gathered(passed, after)
        rest = mid = None
        if i == 0:
            sm = _unpack_small_gathered(last)
            sm['mla_q_norm'], sm['mla_kv_norm'] = w['mla_q_norm'], w['mla_kv_norm']
            rest = lambda after: _layer_weights(_gathered(*_pass_on(started[1], after))[0], 0)
        coming = {}
        if i + 1 < DEPTH:
            def mid(after, entry=started[i + 2], coming=coming):
                coming['passed'], token = _pass_on(entry, after)
                return token
        xin, xin_b, sv, W = _layer_fwd(i, xin, xin_b, p[i, 0], _layer_weights(got, i), sm, cosp, sinp, rest, mid)
        layers.append(W)
        saved.append(sv)
        passed, after = coming.get('passed'), xin
    *grads_in, loss_cols = _loss_head(xin, loss_target[0], saved[-1]['z'], saved[-1]['pp'])
    loss = jnp.sum(loss_cols[0])

    gbig = {n: [None] * WSPEC[n][0][0] for n in BIG}
    gsmall = {n: [None] * _full_shape(n)[0] for n in SMALL}
    pending = []

    def start(grads, i, tag):
        names = list(grads)
        gs = [grads[n] for n in names]
        handle = _reduce_direct_start(gs, tag) if i > 0 else _reduce_scatter_start(gs, cq, jnp.zeros(TOKEN, F32), tag)
        pending.append((handle, names, i, tag))
        return handle[4]

    joining = []

    def finish(above, after, token):
        for entry in [e for e in pending if e[2] > above]:
            pending.remove(entry)
            handle, names, i, tag = entry
            handle = (_reduce_direct_finish if i > 0 else _reduce_scatter_finish)(handle, cq, after, tag)
            joining.append((handle, names, i, tag))
            token = token + handle[4]
        return token

    token = jnp.zeros(TOKEN, F32)
    for i in reversed(range(DEPTH)):
        early = (lambda grads: start(grads, 0, "l0a")) if i == 0 else None
        below = (saved[i - 1]['z'], saved[i - 1]['pp']) if i > 0 else None
        grads_in, big, small = _layer_bwd(i, grads_in, p[i, 0], layers[i], sm, saved[i], cosp, sinp, token, early,
                                          below)
        dx = grads_in[0]
        token = finish(i + 1, dx, start(big, i, "l%d%s" % (i, "b" if i == 0 else "")))
        for n, g in small.items():
            gsmall[n][i if n in ('ln_g', 'ln_b') else i // 3] = g
    finish(-1, token, token)
    for handle, names, i, tag in joining:
        for n, g in zip(names, _join_wait(handle, joining[-1][0][4], "rs_join_wait_" + tag)):
            gbig[n][i if n in COMMON_BIG else i // 3] = _local_shard_grad(n, g, q)
    small_sum = _all_reduce_small(_pack_small({n: jnp.stack(g) for n, g in gsmall.items()}, loss))
    gsm, loss = _unpack_small(small_sum, q), small_sum.reshape(-1)[SMALL_FULL]

    grad, delta, new_m, new_v = {}, {}, {}, {}
    for n in BIG:
        update = _adamw_shard_major if n == 'gla_w_in' else _adamw
        grad[n], delta[n], new_m[n], new_v[n] = update(w[n], m[n], v[n], gbig[n], "adamw_" + n)
    flat2 = lambda a: a.reshape(-1, a.shape[-1])
    res = _adamw_small(*[[flat2(d[n]) for n in SMALL] for d in (w, gsm, m, v)])
    for k, out in enumerate((grad, delta, new_m, new_v)):
        for n, r in zip(SMALL, res[k::4]):
            out[n] = r.reshape(WSPEC[n][0])
    return (loss, dx[None], *[grad[n] for n in WNAMES], *[delta[n] for n in WNAMES],
            *[new_m[n] for n in WNAMES], *[new_v[n] for n in WNAMES])
```

```python
import functools

import numpy as np
import jax
import jax.numpy as jnp
from jax import lax
from jax.experimental import pallas as pl
from jax.experimental.pallas import tpu as pltpu

F32 = jnp.float32
BF16 = jnp.bfloat16
MESH = pl.DeviceIdType.MESH

D_MODEL = 1024
DEPTH = 4
CHUNK = 64
ALPHA = (2 * DEPTH) ** 0.25
LN_EPS = 1e-5
RMS_EPS = 1e-6
D_FF = 4 * D_MODEL
GLA_HEADS = 4
GLA_DK = 128
GLA_DV = 256
GLA_RANK = 16
GLA_TAU = 16.0
GLA_HK = GLA_HEADS * GLA_DK
GLA_HV = GLA_HEADS * GLA_DV
GLA_IN = 2 * GLA_HK + GLA_HV + D_MODEL + GLA_RANK
GLA_IN_PAD = 2 * GLA_HK + GLA_HV + D_MODEL + 128
GLA_SHARD = GLA_IN // 4
GLA_WIN = 896
GLA_WIN_STEP = 768
MLA_HEADS = 8
MLA_NOPE = 128
MLA_ROPE = 64
MLA_V = 128
MLA_QR = 256
MLA_KVR = 256
MLA_IN = MLA_QR + MLA_KVR + MLA_ROPE
MLA_IN_PAD = MLA_QR + MLA_KVR + 128
MLA_QH = 256
ROPE_BASE = 10000.0
ADAM_LR = 0.001
ADAM_B1 = 0.9
ADAM_B2 = 0.999
ADAM_EPS = 1e-08
ADAM_WD = 0.01
ADAM_STEP = 10

VMEM_LIMIT = 48 * 1024 * 1024
FULL_ROWS = 2048
N_CHIPS = 4

WSPEC = {
    'gla_w_in': ((2, 1024, 772), 2), 'gla_w_gate_up': ((2, 16, 128), 2), 'gla_b_gate': ((2, 128), 1),
    'gla_norm_g': ((2, 64), 1), 'gla_w_out': ((2, 256, 1024), 1), 'mla_w_in': ((1, 256, 576), 1),
    'mla_q_norm': ((1, 256), None), 'mla_kv_norm': ((1, 256), None), 'mla_w_uq': ((1, 256, 384), 2),
    'mla_w_ukv': ((1, 256, 512), 2), 'mla_w_out': ((1, 256, 1024), 1), 'conv_w_in': ((1, 1024, 768), 2),
    'conv_w': ((1, 3, 256), 2), 'conv_w_out': ((1, 256, 1024), 1), 'ln_g': ((4, 2, 256), 2),
    'ln_b': ((4, 2, 256), 2), 'mlp_w1': ((4, 1024, 1024), 2), 'mlp_w2': ((4, 1024, 1024), 1),
    'ple_w_gate': ((4, 256, 1024), 1), 'ple_w_proj': ((4, 256, 256), 2),
}
WNAMES = list(WSPEC)
BIG = ['gla_w_in', 'gla_w_out', 'mla_w_in', 'mla_w_uq', 'mla_w_ukv', 'mla_w_out', 'conv_w_in', 'conv_w_out',
       'mlp_w1', 'mlp_w2', 'ple_w_gate', 'ple_w_proj']
SMALL_SHARDED = ['gla_w_gate_up', 'gla_b_gate', 'gla_norm_g', 'conv_w', 'ln_g', 'ln_b']
SMALL = SMALL_SHARDED + ['mla_q_norm', 'mla_kv_norm']
MIXER = ['gla', 'mla', 'conv']
COMMON_BIG = ['mlp_w1', 'mlp_w2', 'ple_w_gate', 'ple_w_proj']


def _size(shape):
    return int(np.prod(shape))


def _full_shape(name):
    shape, ax = WSPEC[name]
    if ax is None:
        return shape
    return tuple(s * N_CHIPS if i == ax else s for i, s in enumerate(shape))


def _cparams(sem=None):
    return pltpu.CompilerParams(dimension_semantics=sem, vmem_limit_bytes=VMEM_LIMIT)


def _out(shape, dtype):
    return pltpu.HBM(shape, dtype)


def _hbm(v):
    return pltpu.with_memory_space_constraint(v, pltpu.HBM)


def _mm(a, b, *, name, ta=False, tb=False, M=None, N=None, K=None, out_dtypes=(F32,), epilogue=None, extras=(),
        tm=1024, tn=512, tk=None, a_off=(0, 0), b_sh=False, out_sh=False, n_sums=0):
    if M is None:
        M = a.shape[1] if ta else a.shape[0]
    if K is None:
        K = a.shape[0] if ta else a.shape[1]
    if b_sh:
        kw, nq = b.shape[1], b.shape[2]
        n_b, k_b = (kw, N_CHIPS * nq) if tb else (N_CHIPS * nq, kw)
        N = n_b if N is None else N
        assert K == k_b
    elif N is None:
        N = b.shape[0] if tb else b.shape[1]
    if tk is None:
        tk = FULL_ROWS if ta else 1024
    tm, tn, tk = min(tm, M), min(tn, N), min(tk, K)
    assert M % tm == 0 and N % tn == 0 and K % tk == 0, (name, M, N, K, tm, tn, tk)
    nk = K // tk
    n_ex, n_out = len(extras), len(out_dtypes)
    assert n_sums == 0 or tn == N

    n_b = N_CHIPS if (b_sh and tb and tk == K) else 1

    def body(a_ref, *rest):
        b_refs, rest = rest[:n_b], rest[n_b:]
        ex_refs, out_refs = rest[:n_ex], rest[n_ex:n_ex + n_out]
        sum_refs = rest[n_ex + n_out:n_ex + n_out + n_sums]
        first_rows = pl.program_id(0) == 0
        dims = ((((0,) if ta else (1,)), ((1,) if tb else (0,))), ((), ()))
        if n_b == 1:
            part = lax.dot_general(a_ref[...].astype(BF16), b_refs[0][...].astype(BF16), dims,
                                   preferred_element_type=F32)
        else:
            part = sum(lax.dot_general(a_ref[:, s * nq:(s + 1) * nq].astype(BF16), b_refs[s][...].astype(BF16), dims,
                                       preferred_element_type=F32) for s in range(n_b))

        def finish(acc):
            res = (acc,) if epilogue is None else epilogue(acc, *[r[...] for r in ex_refs])
            if n_sums:
                res, sums = res

                @pl.when(first_rows)
                def _():
                    for r in sum_refs:
                        r[...] = jnp.zeros(r.shape, F32)

                for r, v in zip(sum_refs, sums):
                    r[...] += jnp.broadcast_to(v, r.shape)
            for r, v in zip(out_refs, res):
                r[...] = v.astype(r.dtype)

        if nk == 1:
            finish(part)
        else:
            acc_ref = rest[-1]
            k = pl.program_id(2)

            @pl.when(k == 0)
            def _():
                acc_ref[...] = part

            @pl.when(k > 0)
            def _():
                acc_ref[...] += part

            @pl.when(k == nk - 1)
            def _():
                finish(acc_ref[...])

    if ta:
        a_spec = pl.BlockSpec((tk, tm), lambda i, j, k: (k + a_off[0], i + a_off[1]))
    else:
        a_spec = pl.BlockSpec((tm, tk), lambda i, j, k: (i + a_off[0], k + a_off[1]))
    once = dict(pipeline_mode=pl.Buffered(1)) if (tn == N and nk == 1) else {}
    if n_b > 1:
        b_specs = [pl.BlockSpec((None, tn, nq), functools.partial(lambda i, j, k, s: (s, j, 0), s=s), **once)
                   for s in range(n_b)]
    elif b_sh and tb:
        assert nq % tk == 0
        per = nq // tk
        b_spec = pl.BlockSpec((None, tn, tk), lambda i, j, k: (k // per, j, k % per), **once)
    elif b_sh:
        assert nq % tn == 0
        per = nq // tn
        b_spec = pl.BlockSpec((None, tk, tn), lambda i, j, k: (j // per, k, j % per), **once)
    elif tb:
        b_spec = pl.BlockSpec((tn, tk), lambda i, j, k: (j, k), **once)
    else:
        b_spec = pl.BlockSpec((tk, tn), lambda i, j, k: (k, j), **once)
    if n_b == 1:
        b_specs = [b_spec]
    ex_specs = []
    for arr, kind in extras:
        if kind == 'mn':
            ex_specs.append(pl.BlockSpec((tm, tn), lambda i, j, k: (i, j)))
        elif kind == 'n':
            ex_specs.append(pl.BlockSpec((1, tn), lambda i, j, k: (0, j)))
        else:
            ex_specs.append(pl.BlockSpec(arr.shape, lambda i, j, k: (0, 0)))
    if out_sh:
        assert (N // N_CHIPS) % tn == 0
        per_o = N // N_CHIPS // tn
        o_spec = pl.BlockSpec((None, tm, tn), lambda i, j, k: (j // per_o, i, j % per_o))
        o_shape = (N_CHIPS, M, N // N_CHIPS)
    else:
        o_spec = pl.BlockSpec((tm, tn), lambda i, j, k: (i, j))
        o_shape = (M, N)
    outs = pl.pallas_call(
        body, name=name, grid=(M // tm, N // tn, nk),
        in_specs=[a_spec] + b_specs + ex_specs,
        out_specs=[o_spec for _ in out_dtypes] + [pl.BlockSpec((8, N), lambda i, j, k: (0, 0))] * n_sums,
        out_shape=[_out(o_shape, d) for d in out_dtypes] + [_out((8, N), F32)] * n_sums,
        scratch_shapes=[pltpu.VMEM((tm, tn), F32)] if nk > 1 else [],
        compiler_params=_cparams(("arbitrary" if n_sums else "parallel", "parallel", "arbitrary")),
    )(a, *[b] * n_b, *[e[0] for e in extras])
    if n_sums:
        return tuple(outs[:n_out]), tuple(outs[n_out:])
    return outs[0] if n_out == 1 else tuple(outs)


def _rowwise(fn, *, name, rows, pars=(), outs=(), accs=(), tm=256):
    S = rows[0][0].shape[0]
    tm = min(tm, S)
    assert S % tm == 0
    n_r, n_p, n_o, n_a = len(rows), len(pars), len(outs), len(accs)

    def body(*refs):
        r_refs, p_refs = refs[:n_r], refs[n_r:n_r + n_p]
        o_refs, a_refs = refs[n_r + n_p:n_r + n_p + n_o], refs[n_r + n_p + n_o:]
        o_vals, a_vals = fn([r[...] for r in r_refs], [p[...] for p in p_refs])
        for r, v in zip(o_refs, o_vals):
            r[...] = v.astype(r.dtype)
        if n_a:
            i = pl.program_id(0)

            @pl.when(i == 0)
            def _():
                for r in a_refs:
                    r[...] = jnp.zeros(r.shape, r.dtype)

            for r, v in zip(a_refs, a_vals):
                r[...] += jnp.broadcast_to(v, r.shape)

    in_specs = [pl.BlockSpec((tm, w), functools.partial(lambda i, o: (i, o), o=off)) for _, w, off in rows]
    in_specs += [pl.BlockSpec(p.shape, functools.partial(lambda i, nd: (0,) * nd, nd=p.ndim)) for p in pars]
    out_specs = [pl.BlockSpec((tm, w), lambda i: (i, 0)) for w, _ in outs]
    out_specs += [pl.BlockSpec((8, w), lambda i: (0, 0)) for w in accs]
    out_shape = [_out((S, w), d) for w, d in outs]
    out_shape += [_out((8, w), F32) for w in accs]
    res = pl.pallas_call(
        body, name=name, grid=(S // tm,), in_specs=in_specs, out_specs=out_specs, out_shape=out_shape,
        compiler_params=_cparams(("arbitrary",)),
    )(*[r[0] for r in rows], *pars)
    return tuple(res)


def _colsum(v):
    return jnp.sum(v, axis=0, keepdims=True)


def _ln_stats(v):
    mu = jnp.mean(v, axis=-1, keepdims=True)
    d = v - mu
    var = jnp.mean(d * d, axis=-1, keepdims=True)
    rstd = lax.rsqrt(var + LN_EPS)
    return d * rstd, rstd


def _ln_fwd_epilogue(h, x, g, b, *unused):
    v = ALPHA * x + h
    xhat, _ = _ln_stats(v)
    y = xhat * g + b
    return y, y, v


def _ln_bwd_epilogue(scale):
    def epilogue(acc, resid, v, g, *unused):
        dy = acc + scale * resid
        xhat, rstd = _ln_stats(v)
        dxh = dy * g
        m1 = jnp.mean(dxh, axis=-1, keepdims=True)
        m2 = jnp.mean(dxh * xhat, axis=-1, keepdims=True)
        dv = rstd * (dxh - m1 - xhat * m2)
        return (dv, dv), (_colsum(dy * xhat), _colsum(dy))
    return epilogue


def _ple_gate_grads(dx3, z, pp):
    s = jax.nn.sigmoid(z)
    return dx3 * s, dx3 * pp * s * (1.0 - s)


def _loss_head(y, t, z, pp):
    def fn(r, p):
        d = r[0] - r[1]
        dy = d * (1.0 / D_MODEL)
        return [dy, *_ple_gate_grads(dy, r[2], r[3])], [_colsum(d * d) * (0.5 / D_MODEL)]
    return _rowwise(fn, name="loss_head", rows=[(a, D_MODEL, 0) for a in (y, t, z, pp)],
                    outs=[(D_MODEL, F32), (D_MODEL, BF16), (D_MODEL, BF16)], accs=[D_MODEL])


def _input_grad_epilogue(acc, dv, *below):
    dx = acc + ALPHA * dv
    return (dx, *_ple_gate_grads(dx, *below)) if below else (dx,)


N_LEVELS = 6
GLA_STEP = 4


def _gla_consts():
    C = CHUNK
    A = np.zeros((N_LEVELS + 3, C, C), np.float32)
    masks = np.zeros((N_LEVELS + 1, C, C), np.float32)
    r = np.arange(C)[:, None]
    u = np.arange(C)[None, :]
    for l in range(N_LEVELS):
        half = C >> (l + 1)
        mid = (r // (2 * half)) * (2 * half) + half - 1
        A[l] = np.where(r > mid, (u > mid) & (u <= r), (u > r) & (u <= mid))
        masks[l] = ((r // (2 * half)) == (u // (2 * half))) & (((r // half) % 2) != ((u // half) % 2))
    masks[N_LEVELS] = (r == u)
    A[N_LEVELS] = (u <= r)
    A[N_LEVELS + 1] = (u > r)
    A[N_LEVELS + 2] = 1.0
    A = A.reshape(-1, C)
    return A, np.ascontiguousarray(A.T), masks


def _split3(v):
    hi = v.astype(BF16)
    r1 = v - hi.astype(F32)
    mid = r1.astype(BF16)
    lo = (r1 - mid.astype(F32)).astype(BF16)
    return hi, mid, lo


def _dot_exact01(a01, v):
    hi, mid, lo = _split3(v)
    f = lambda p: jnp.dot(a01, p, preferred_element_type=F32)
    return f(hi) + f(mid) + f(lo)


def _nt(a, b):
    return lax.dot_general(a, b, (((1,), (1,)), ((), ())), preferred_element_type=F32)


def _tn(a, b):
    return lax.dot_general(a, b, (((0,), (0,)), ((), ())), preferred_element_type=F32)


def _nn(a, b):
    return jnp.dot(a, b, preferred_element_type=F32)


def _gla_chunk_terms(q, k, E, m_ref):
    C = CHUNK
    scores = m_ref[N_LEVELS] * _nt(q.astype(BF16), k.astype(BF16))
    qes, kes = [], []
    for l in range(N_LEVELS):
        El = E[l * C:(l + 1) * C]
        qe, ke = (q * El).astype(BF16), (k * El).astype(BF16)
        qes.append(qe)
        kes.append(ke)
        scores = scores + m_ref[l] * _nt(qe, ke)
    return qes, kes, scores


def _head(v, h, w):
    return v[:, h * w:(h + 1) * w]


def _gla_fwd(pin, la):
    S = pin.shape[0]
    NC = S // CHUNK
    C, R = CHUNK, CHUNK * GLA_STEP
    A, _, masks = _gla_consts()

    def body(q_ref, k_ref, v_ref, la_ref, a_ref, m_ref, o_ref, st_ref, state):
        @pl.when(pl.program_id(0) == 0)
        def _():
            state[...] = jnp.zeros(state.shape, F32)

        for ci in range(GLA_STEP):
            rows = pl.ds(ci * C, C)
            E_all = jnp.exp(_dot_exact01(a_ref[...], la_ref[rows, :]))
            q_all = q_ref[rows, :] * (GLA_DK ** -0.5)
            k_all, v_all = k_ref[rows, :], v_ref[rows, :]
            outs = []
            for h in range(GLA_HEADS):
                q, k, E = _head(q_all, h, GLA_DK), _head(k_all, h, GLA_DK), _head(E_all, h, GLA_DK)
                _, _, scores = _gla_chunk_terms(q, k, E, m_ref)
                Eq, Ek, Ee = E[6 * C:7 * C], E[7 * C:8 * C], E[8 * C:9 * C]
                st = state[h]
                st_ref[h, ci] = st
                vb = _head(v_all, h, GLA_DV).astype(BF16)
                outs.append(_nn(scores.astype(BF16), vb) + _nt((q * Eq).astype(BF16), st.astype(BF16)))
                state[h] = st * jnp.concatenate([Ee] * (GLA_DV // C), axis=0) + _tn(vb, (k * Ek).astype(BF16))
            o_ref[rows, :] = jnp.concatenate(outs, axis=1)

    return pl.pallas_call(
        body, name="gla_fwd", grid=(NC // GLA_STEP,),
        in_specs=[pl.BlockSpec((R, GLA_HK), lambda c: (c, 0)),
                  pl.BlockSpec((R, GLA_HK), lambda c: (c, 1)),
                  pl.BlockSpec((R, GLA_HV), lambda c: (c, 2 * GLA_HK // GLA_HV)),
                  pl.BlockSpec((R, GLA_HK), lambda c: (c, 0)),
                  pl.BlockSpec(A.shape, lambda c: (0, 0)),
                  pl.BlockSpec(masks.shape, lambda c: (0, 0, 0))],
        out_specs=[pl.BlockSpec((R, GLA_HV), lambda c: (c, 0)),
                   pl.BlockSpec((GLA_HEADS, GLA_STEP, GLA_DV, GLA_DK), lambda c: (0, c, 0, 0))],
        out_shape=[_out((S, GLA_HV), F32), _out((GLA_HEADS, NC, GLA_DV, GLA_DK), F32)],
        scratch_shapes=[pltpu.VMEM((GLA_HEADS, GLA_DV, GLA_DK), F32)],
        compiler_params=_cparams(("arbitrary",)),
    )(pin, pin, pin, la, jnp.asarray(A, BF16), jnp.asarray(masks))


def _gla_bwd(pin, la, states, do):
    S = pin.shape[0]
    NC = S // CHUNK
    C, R = CHUNK, CHUNK * GLA_STEP
    A, AT, masks = _gla_consts()
    scale = GLA_DK ** -0.5

    def body(q_ref, k_ref, v_ref, la_ref, st_ref, do_ref, a_ref, at_ref, m_ref,
             dq_ref, dk_ref, dv_ref, dla_ref, dstate):
        @pl.when(pl.program_id(0) == 0)
        def _():
            dstate[...] = jnp.zeros(dstate.shape, F32)

        for ci in reversed(range(GLA_STEP)):
            one_chunk(ci, pl.ds(ci * C, C), q_ref, k_ref, v_ref, la_ref, st_ref, do_ref, a_ref, at_ref, m_ref,
                      dq_ref, dk_ref, dv_ref, dla_ref, dstate)

    def one_chunk(ci, rows, q_ref, k_ref, v_ref, la_ref, st_ref, do_ref, a_ref, at_ref, m_ref,
                  dq_ref, dk_ref, dv_ref, dla_ref, dstate):
        E_all = jnp.exp(_dot_exact01(a_ref[...], la_ref[rows, :]))
        q_all = q_ref[rows, :] * scale
        k_all, v_all, do_all = k_ref[rows, :], v_ref[rows, :], do_ref[rows, :]
        dqs, dks, dvs, dXs = [], [], [], []
        for h in range(GLA_HEADS):
            q, k, E = _head(q_all, h, GLA_DK), _head(k_all, h, GLA_DK), _head(E_all, h, GLA_DK)
            qes, kes, scores = _gla_chunk_terms(q, k, E, m_ref)
            Eq, Ek, Ee = E[6 * C:7 * C], E[7 * C:8 * C], E[8 * C:9 * C]
            st, dst = st_ref[h, ci], dstate[h]
            dob, vb = _head(do_all, h, GLA_DV).astype(BF16), _head(v_all, h, GLA_DV).astype(BF16)
            dstb = dst.astype(BF16)
            qEq, kEk = (q * Eq).astype(BF16), (k * Ek).astype(BF16)
            dsc = _nt(dob, vb)
            dvs.append(_tn(scores.astype(BF16), dob) + _nt(kEk, dstb))
            dqEq = _nn(dob, st.astype(BF16))
            dkEk = _nn(vb, dstb)
            Gd = (m_ref[N_LEVELS] * dsc).astype(BF16)
            dq = _nn(Gd, k.astype(BF16)) + dqEq * Eq
            dk = _tn(Gd, q.astype(BF16)) + dkEk * Ek
            dX = []
            for l in range(N_LEVELS):
                El = E[l * C:(l + 1) * C]
                G = (m_ref[l] * dsc).astype(BF16)
                dqe, dke = _nn(G, kes[l]), _tn(G, qes[l])
                dq = dq + dqe * El
                dk = dk + dke * El
                dX.append((dqe * q + dke * k) * El)
            dX.append(dqEq * q * Eq)
            dX.append(dkEk * k * Ek)
            prod = dst * st
            dEe = prod[0:C]
            for i in range(1, GLA_DV // C):
                dEe = dEe + prod[i * C:(i + 1) * C]
            dX.append(dEe * Ee)
            dXs.append(jnp.concatenate(dX, axis=0))
            dqs.append(dq * scale)
            dks.append(dk)
            dstate[h] = dst * jnp.concatenate([Ee] * (GLA_DV // C), axis=0) + _tn(dob, qEq)
        dla_ref[rows, :] = _dot_exact01(at_ref[...], jnp.concatenate(dXs, axis=1))
        dq_ref[rows, :] = jnp.concatenate(dqs, axis=1).astype(dq_ref.dtype)
        dk_ref[rows, :] = jnp.concatenate(dks, axis=1).astype(dk_ref.dtype)
        dv_ref[rows, :] = jnp.concatenate(dvs, axis=1).astype(dv_ref.dtype)

    rc = lambda c: NC // GLA_STEP - 1 - c
    return pl.pallas_call(
        body, name="gla_bwd", grid=(NC // GLA_STEP,),
        in_specs=[pl.BlockSpec((R, GLA_HK), lambda c: (rc(c), 0)),
                  pl.BlockSpec((R, GLA_HK), lambda c: (rc(c), 1)),
                  pl.BlockSpec((R, GLA_HV), lambda c: (rc(c), 2 * GLA_HK // GLA_HV)),
                  pl.BlockSpec((R, GLA_HK), lambda c: (rc(c), 0)),
                  pl.BlockSpec((GLA_HEADS, GLA_STEP, GLA_DV, GLA_DK), lambda c: (0, rc(c), 0, 0)),
                  pl.BlockSpec((R, GLA_HV), lambda c: (rc(c), 0)),
                  pl.BlockSpec(A.shape, lambda c: (0, 0)),
                  pl.BlockSpec(AT.shape, lambda c: (0, 0)),
                  pl.BlockSpec(masks.shape, lambda c: (0, 0, 0))],
        out_specs=[pl.BlockSpec((R, GLA_HK), lambda c: (rc(c), 0)),
                   pl.BlockSpec((R, GLA_HK), lambda c: (rc(c), 0)),
                   pl.BlockSpec((R, GLA_HV), lambda c: (rc(c), 0)),
                   pl.BlockSpec((R, GLA_HK), lambda c: (rc(c), 0))],
        out_shape=[_out((S, GLA_HK), BF16), _out((S, GLA_HK), BF16), _out((S, GLA_HV), BF16),
                   _out((S, GLA_HK), F32)],
        scratch_shapes=[pltpu.VMEM((GLA_HEADS, GLA_DV, GLA_DK), F32)],
        compiler_params=_cparams(("arbitrary",)),
    )(pin, pin, pin, la, states, do, jnp.asarray(A, BF16), jnp.asarray(AT, BF16), jnp.asarray(masks))


def _gla_post_fwd(o, pin, g):
    def fn(r, p):
        ov, rv = r
        ys = []
        for h in range(GLA_HEADS):
            oh = ov[:, h * GLA_DV:(h + 1) * GLA_DV]
            rh = rv[:, h * GLA_DV:(h + 1) * GLA_DV]
            rs = lax.rsqrt(jnp.mean(oh * oh, axis=-1, keepdims=True) + RMS_EPS)
            ys.append(oh * rs * p[0] * (rh * jax.nn.sigmoid(rh)))
        return [jnp.concatenate(ys, axis=1)], []
    return _rowwise(fn, name="gla_post_fwd", rows=[(o, GLA_HV, 0), (pin, GLA_HV, (2 * GLA_HK + GLA_HV) // GLA_HV)],
                    pars=[g], outs=[(GLA_HV, BF16)])[0]


def _gla_post_bwd(dy, o, pin, g):
    def fn(r, p):
        dyv, ov, rv = r
        dos, drs, dg = [], [], 0.0
        for h in range(GLA_HEADS):
            sl = slice(h * GLA_DV, (h + 1) * GLA_DV)
            oh, rh, dyh = ov[:, sl], rv[:, sl], dyv[:, sl]
            rs = lax.rsqrt(jnp.mean(oh * oh, axis=-1, keepdims=True) + RMS_EPS)
            xh = oh * rs
            sg = jax.nn.sigmoid(rh)
            d_on = dyh * (rh * sg)
            drs.append(dyh * (xh * p[0]) * (sg * (1.0 + rh * (1.0 - sg))))
            dg = dg + _colsum(d_on * xh)
            dxh = d_on * p[0]
            dos.append(rs * (dxh - xh * jnp.mean(dxh * xh, axis=-1, keepdims=True)))
        return [jnp.concatenate(dos, axis=1), jnp.concatenate(drs, axis=1)], [dg]
    return _rowwise(fn, name="gla_post_bwd",
                    rows=[(dy, GLA_HV, 0), (o, GLA_HV, 0), (pin, GLA_HV, (2 * GLA_HK + GLA_HV) // GLA_HV)],
                    pars=[g], outs=[(GLA_HV, F32), (GLA_HV, BF16)], accs=[GLA_DV])


def _gla_gate_bwd(dla, la):
    def fn(r, p):
        dz = r[0] * (1.0 / GLA_TAU) * (1.0 - jnp.exp(GLA_TAU * r[1]))
        return [dz], [_colsum(dz)]
    return _rowwise(fn, name="gla_gate_bwd", rows=[(dla, GLA_HK, 0), (la, GLA_HK, 0)], outs=[(GLA_HK, BF16)],
                    accs=[GLA_HK])


def _log_sigmoid(z):
    return jnp.minimum(z, 0.0) - jnp.log(1.0 + jnp.exp(-jnp.abs(z)))


def _rot_half(v):
    lane = lax.broadcasted_iota(jnp.int32, v.shape, 1)
    return jnp.where(lane < 32, -pltpu.roll(v, 96, 1), jnp.where(lane < 64, pltpu.roll(v, 32, 1), 0.0))


def _rms(v):
    rs = lax.rsqrt(jnp.mean(v * v, axis=-1, keepdims=True) + RMS_EPS)
    return v * rs, rs


def _mla_norm_fwd(cin, gq, gkv, cosp, sinp):
    def fn(r, p):
        cv, cs, sn = r
        qn, _ = _rms(cv[:, :MLA_QR])
        kvn, _ = _rms(cv[:, MLA_QR:MLA_QR + MLA_KVR])
        kr = cv[:, MLA_QR + MLA_KVR:]
        return [qn * p[0], kvn * p[1], kr * cs + _rot_half(kr) * sn], []
    return _rowwise(fn, name="mla_norm_fwd", rows=[(cin, MLA_IN_PAD, 0), (cosp, 128, 0), (sinp, 128, 0)],
                    pars=[gq, gkv], outs=[(MLA_QR, BF16), (MLA_KVR, BF16), (128, BF16)])


def _mla_qrope_fwd(q, cosp, sinp):
    scale = (MLA_NOPE + MLA_ROPE) ** -0.5

    def fn(r, p):
        qv, cs, sn = r
        parts = []
        for h in range(MLA_HEADS):
            parts.append(qv[:, h * MLA_QH:h * MLA_QH + 128] * scale)
            rp = qv[:, h * MLA_QH + 128:(h + 1) * MLA_QH]
            parts.append((rp * cs + _rot_half(rp) * sn) * scale)
        return [jnp.concatenate(parts, axis=1)], []
    W = MLA_HEADS * MLA_QH
    return _rowwise(fn, name="mla_qrope_fwd", rows=[(q, W, 0), (cosp, 128, 0), (sinp, 128, 0)],
                    outs=[(W, BF16)])[0]


def _mla_qrope_bwd(dq, cosp, sinp):
    scale = (MLA_NOPE + MLA_ROPE) ** -0.5

    def fn(r, p):
        dv, cs, sn = r
        parts = []
        for h in range(MLA_HEADS):
            parts.append(dv[:, h * MLA_QH:h * MLA_QH + 128] * scale)
            rp = dv[:, h * MLA_QH + 128:(h + 1) * MLA_QH]
            parts.append((rp * cs - _rot_half(rp) * sn) * scale)
        return [jnp.concatenate(parts, axis=1)], []
    W = MLA_HEADS * MLA_QH
    return _rowwise(fn, name="mla_qrope_bwd", rows=[(dq, W, 0), (cosp, 128, 0), (sinp, 128, 0)],
                    outs=[(W, BF16)])[0]


def _mla_norm_bwd(cin, dqn, dkvn, dkr, gq, gkv, cosp, sinp):
    def fn(r, p):
        cv, dq_, dkv_, dkr_, cs, sn = r
        outs, accs = [], []
        for (lo, hi), dn, g in (((0, MLA_QR), dq_, p[0]), ((MLA_QR, MLA_QR + MLA_KVR), dkv_, p[1])):
            xh, rs = _rms(cv[:, lo:hi])
            dxh = dn * g
            outs.append(rs * (dxh - xh * jnp.mean(dxh * xh, axis=-1, keepdims=True)))
            accs.append(_colsum(dn * xh))
        dk = dkr_[:, 0:128]
        for h in range(1, MLA_HEADS):
            dk = dk + dkr_[:, h * 128:(h + 1) * 128]
        outs.append(dk * cs - _rot_half(dk) * sn)
        return [jnp.concatenate(outs, axis=1)], accs
    return _rowwise(fn, name="mla_norm_bwd",
                    rows=[(cin, MLA_IN_PAD, 0), (dqn, MLA_QR, 0), (dkvn, MLA_KVR, 0), (dkr, MLA_HEADS * 128, 0),
                          (cosp, 128, 0), (sinp, 128, 0)],
                    pars=[gq, gkv], outs=[(MLA_IN_PAD, BF16)], accs=[MLA_QR, MLA_KVR])


def _mla_probs(q, k, i, tq):
    s = _nt(q, k)
    row = (i * tq + lax.broadcasted_iota(jnp.int32, s.shape, 0)) // CHUNK
    col = lax.broadcasted_iota(jnp.int32, s.shape, 1) // CHUNK
    s = jnp.where(col <= row, s, -jnp.inf)
    e = jnp.exp(s - jnp.max(s, axis=-1, keepdims=True))
    return e / jnp.sum(e, axis=-1, keepdims=True)


def _mla_attn_fwd(qr, knv, kr, tq=256):
    S = qr.shape[0]
    tq = min(tq, S)

    def body(q_ref, kn_ref, v_ref, kr_ref, o_ref, k_cat):
        k_cat[:, :128] = kn_ref[...]
        k_cat[:, 128:] = kr_ref[...]
        for i in range(S // tq):
            rows, keys = pl.ds(i * tq, tq), pl.ds(0, (i + 1) * tq)
            pr = _mla_probs(q_ref[rows, :], k_cat[keys, :], i, tq)
            o_ref[rows, :] = _nn(pr.astype(BF16), v_ref[keys, :])

    return pl.pallas_call(
        body, name="mla_attn_fwd", grid=(MLA_HEADS,),
        in_specs=[pl.BlockSpec((S, MLA_QH), lambda h: (0, h)),
                  pl.BlockSpec((S, 128), lambda h: (0, h)),
                  pl.BlockSpec((S, 128), lambda h: (0, MLA_HEADS + h)),
                  pl.BlockSpec((S, 128), lambda h: (0, 0))],
        out_specs=pl.BlockSpec((S, 128), lambda h: (0, h)),
        out_shape=_out((S, MLA_HEADS * MLA_V), F32),
        scratch_shapes=[pltpu.VMEM((S, MLA_QH), BF16)],
        compiler_params=_cparams(("parallel",)),
    )(qr, knv, knv, kr)


def _mla_attn_bwd(qr, knv, kr, o, do, tq=256):
    S = qr.shape[0]
    tq = min(tq, S)
    W = MLA_HEADS * 128

    def body(q_ref, kn_ref, v_ref, kr_ref, o_ref, do_ref, dq_ref, dkn_ref, dv_ref, dkr_ref, k_cat, dk_acc, dv_acc):
        k_cat[:, :128] = kn_ref[...]
        k_cat[:, 128:] = kr_ref[...]
        dk_acc[...] = jnp.zeros(dk_acc.shape, F32)
        dv_acc[...] = jnp.zeros(dv_acc.shape, F32)
        for i in range(S // tq):
            rows, keys = pl.ds(i * tq, tq), pl.ds(0, (i + 1) * tq)
            q, k, v = q_ref[rows, :], k_cat[keys, :], v_ref[keys, :]
            pr = _mla_probs(q, k, i, tq)
            dov = do_ref[rows, :]
            delta = jnp.sum(dov * o_ref[rows, :], axis=-1, keepdims=True)
            dob = dov.astype(BF16)
            ds = (pr * (_nt(dob, v) - delta)).astype(BF16)
            dq_ref[rows, :] = _nn(ds, k)
            dk_acc[keys, :] += _tn(ds, q)
            dv_acc[keys, :] += _tn(pr.astype(BF16), dob)
        dkn_ref[...] = dk_acc[:, :128].astype(dkn_ref.dtype)
        dkr_ref[...] = dk_acc[:, 128:]
        dv_ref[...] = dv_acc[...].astype(dv_ref.dtype)

    head = lambda w: pl.BlockSpec((S, w), lambda h: (0, h))
    return pl.pallas_call(
        body, name="mla_attn_bwd", grid=(MLA_HEADS,),
        in_specs=[head(MLA_QH), head(128), pl.BlockSpec((S, 128), lambda h: (0, MLA_HEADS + h)),
                  pl.BlockSpec((S, 128), lambda h: (0, 0)), head(128), head(128)],
        out_specs=[head(MLA_QH), head(128), head(128), head(128)],
        out_shape=[_out((S, MLA_HEADS * MLA_QH), F32), _out((S, W), BF16), _out((S, W), BF16), _out((S, W), F32)],
        scratch_shapes=[pltpu.VMEM((S, MLA_QH), BF16), pltpu.VMEM((S, MLA_QH), F32), pltpu.VMEM((S, 128), F32)],
        compiler_params=_cparams(("parallel",)),
    )(qr, knv, knv, kr, o, do)


CONV_TILE = 256


def _shift_down(v, n):
    row = lax.broadcasted_iota(jnp.int32, v.shape, 0)
    return jnp.where(row >= n, pltpu.roll(v, n, 0), 0.0)


def _shift_up(v, n):
    S = v.shape[0]
    row = lax.broadcasted_iota(jnp.int32, v.shape, 0)
    return jnp.where(row < S - n, pltpu.roll(v, S - n, 0), 0.0)


def _conv_specs(S, n_extra_cols):
    nt = D_MODEL // CONV_TILE
    specs = [pl.BlockSpec((S, CONV_TILE), functools.partial(lambda j, o: (0, o + j), o=part * nt))
             for part in range(3)]
    specs.append(pl.BlockSpec((8, CONV_TILE), lambda j: (0, j)))
    specs += [pl.BlockSpec((S, CONV_TILE), lambda j: (0, j)) for _ in range(n_extra_cols)]
    return specs


def _conv_fwd(bcu, w8):
    S = bcu.shape[0]

    def body(b_ref, c_ref, u_ref, w_ref, y_ref):
        cu = c_ref[...] * u_ref[...]
        z = w_ref[2:3, :] * cu + w_ref[1:2, :] * _shift_down(cu, 1) + w_ref[0:1, :] * _shift_down(cu, 2)
        y_ref[...] = (b_ref[...] * z).astype(y_ref.dtype)

    return pl.pallas_call(
        body, name="conv_fwd", grid=(D_MODEL // CONV_TILE,), in_specs=_conv_specs(S, 0),
        out_specs=pl.BlockSpec((S, CONV_TILE), lambda j: (0, j)),
        out_shape=_out((S, D_MODEL), BF16),
        compiler_params=_cparams(("parallel",)),
    )(bcu, bcu, bcu, w8)


def _conv_bwd(bcu, w8, dy):
    S = bcu.shape[0]

    def body(b_ref, c_ref, u_ref, w_ref, dy_ref, db_ref, dc_ref, du_ref, dw_ref):
        b, c, u, dyv = b_ref[...], c_ref[...], u_ref[...], dy_ref[...]
        w0, w1, w2 = w_ref[0:1, :], w_ref[1:2, :], w_ref[2:3, :]
        cu = c * u
        cu1, cu2 = _shift_down(cu, 1), _shift_down(cu, 2)
        z = w2 * cu + w1 * cu1 + w0 * cu2
        dz = dyv * b
        db_ref[...] = (dyv * z).astype(db_ref.dtype)
        dcu = w2 * dz + w1 * _shift_up(dz, 1) + w0 * _shift_up(dz, 2)
        dc_ref[...] = (dcu * u).astype(dc_ref.dtype)
        du_ref[...] = (dcu * c).astype(du_ref.dtype)
        dw_ref[...] = jnp.zeros(dw_ref.shape, F32)
        dw_ref[0:1, :] = _colsum(dz * cu2)
        dw_ref[1:2, :] = _colsum(dz * cu1)
        dw_ref[2:3, :] = _colsum(dz * cu)

    col = pl.BlockSpec((S, CONV_TILE), lambda j: (0, j))
    return pl.pallas_call(
        body, name="conv_bwd", grid=(D_MODEL // CONV_TILE,), in_specs=_conv_specs(S, 1),
        out_specs=[col, col, col, pl.BlockSpec((8, CONV_TILE), lambda j: (0, j))],
        out_shape=[_out((S, D_MODEL), BF16)] * 3 + [_out((8, D_MODEL), F32)],
        compiler_params=_cparams(("parallel",)),
    )(bcu, bcu, bcu, w8, dy)


def _adamw_update(w, g, m, v):
    nm = ADAM_B1 * m + (1.0 - ADAM_B1) * g
    nv = ADAM_B2 * v + (1.0 - ADAM_B2) * jnp.square(g)
    m_hat = nm / (1.0 - ADAM_B1 ** ADAM_STEP)
    v_hat = nv / (1.0 - ADAM_B2 ** ADAM_STEP)
    return -ADAM_LR * (m_hat / (jnp.sqrt(v_hat) + ADAM_EPS) + ADAM_WD * w), nm, nv


def _adamw_shard_major(w, m, v, gs, name):
    view = lambda a: jnp.transpose(a, (2, 0, 1))
    g = jnp.stack([x.T for x in gs], axis=1)
    n, L, k = g.shape
    rows = n // 4
    assert n % 4 == 0

    def body(w_ref, m_ref, v_ref, g_ref, go_ref, d_ref, nm_ref, nv_ref):
        gv = g_ref[...]
        d_ref[...], nm_ref[...], nv_ref[...] = _adamw_update(w_ref[...], gv, m_ref[...], v_ref[...])
        go_ref[...] = gv

    spec = pl.BlockSpec((rows, L, k), lambda i: (i, 0, 0))
    outs = pl.pallas_call(
        body, name=name, grid=(4,), in_specs=[spec] * 4, out_specs=[spec] * 4,
        out_shape=[jax.ShapeDtypeStruct((n, L, k), F32)] * 4,
        compiler_params=_cparams(("parallel",)),
    )(view(w), view(m), view(v), g)
    return [jnp.transpose(o, (1, 2, 0)) for o in outs]


def _adamw_small(ws, gs, ms, vs):
    n = len(ws)

    def body(*refs):
        ins, outs = refs[:4 * n], refs[4 * n:]
        for t in range(n):
            w_ref, g_ref, m_ref, v_ref = (ins[k * n + t] for k in range(4))
            gv = g_ref[...]
            outs[4 * t][...] = gv
            outs[4 * t + 1][...], outs[4 * t + 2][...], outs[4 * t + 3][...] = _adamw_update(
                w_ref[...], gv, m_ref[...], v_ref[...])

    return pl.pallas_call(
        body, name="adamw_small",
        out_shape=[jax.ShapeDtypeStruct(a.shape, F32) for a in ws for _ in range(4)],
    )(*ws, *gs, *ms, *vs)


def _adamw(w, m, v, gs, name):
    L, R, Cn = w.shape
    assert len(gs) == L
    tr = R if R <= 256 else 256
    assert R % tr == 0

    def body(w_ref, m_ref, v_ref, *rest):
        g_refs, (go_ref, d_ref, nm_ref, nv_ref) = rest[:L], rest[L:]
        layer = pl.program_id(0)
        gv = g_refs[0][...]
        for k in range(1, L):
            gv = jnp.where(layer == k, g_refs[k][...], gv)
        d_ref[...], nm_ref[...], nv_ref[...] = _adamw_update(w_ref[...], gv, m_ref[...], v_ref[...])
        go_ref[...] = gv

    spec = pl.BlockSpec((None, tr, Cn), lambda l, i: (l, i, 0))
    g_specs = [pl.BlockSpec((tr, Cn), functools.partial(lambda l, i, k: (jnp.where(l == k, i, 0), 0), k=k))
               for k in range(L)]
    return pl.pallas_call(
        body, name=name, grid=(L, R // tr), in_specs=[spec] * 3 + g_specs, out_specs=[spec] * 4,
        out_shape=[jax.ShapeDtypeStruct((L, R, Cn), F32)] * 4,
        compiler_params=_cparams(("arbitrary", "arbitrary")),
    )(w, m, v, *gs)


HBM_SPEC = pl.BlockSpec(memory_space=pltpu.HBM)


def _place():
    return lax.axis_index("x"), lax.axis_index("y"), lax.axis_index("c")


def _other_chips(x, y):
    return [(1 - x, y), (x, 1 - y), (1 - x, 1 - y)]


SEM_SPEC = pl.BlockSpec(memory_space=pltpu.SEMAPHORE)
ANY_SPEC = pl.BlockSpec(memory_space=pl.ANY)
VMEM_SPEC = pl.BlockSpec(memory_space=pltpu.VMEM)
EFFECT = pltpu.SideEffectType.DATAFLOW_SIDE_EFFECTING
TOKEN = (8, 128)


def _ici_start(srcs, lands, after, copies, name, per_src=3):
    n, nl = len(srcs), len(lands)

    def body(*refs):
        src_refs, land_refs = refs[:n], refs[n:n + nl]
        send_sems, recv_sems, token = refs[n + nl + 1], refs[n + nl + 2], refs[-1]
        x, y, c = _place()
        for k, src, dst, to in copies(src_refs, land_refs, x, y, c):
            pltpu.make_async_remote_copy(src_ref=src, dst_ref=dst, send_sem=send_sems.at[k], recv_sem=recv_sems.at[k],
                                         device_id=to, device_id_type=MESH).start()
        token[...] = jnp.zeros(TOKEN, F32)

    n_copies = per_src * max(n, nl if n == 0 else 0)
    res = pl.pallas_call(
        body, name=name,
        out_shape=(pltpu.SemaphoreType.DMA((n_copies,)), pltpu.SemaphoreType.DMA((n_copies,)),
                   *[pltpu.HBM(s.shape, s.dtype) for s in srcs], *[pltpu.HBM(l.shape, l.dtype) for l in lands],
                   jax.ShapeDtypeStruct(TOKEN, F32)),
        in_specs=[HBM_SPEC] * (n + nl) + [ANY_SPEC],
        out_specs=(SEM_SPEC, SEM_SPEC, *[HBM_SPEC] * (n + nl), VMEM_SPEC),
        input_output_aliases={t: 2 + t for t in range(n + nl)},
        compiler_params=pltpu.CompilerParams(has_side_effects=EFFECT),
    )(*[_hbm(s) for s in srcs], *[_hbm(l) for l in lands], after)
    return res[0], res[1], list(res[2:2 + n]), list(res[2 + n:2 + n + nl]), res[-1]


def _ici_wait(handle, after, copies, name):
    send_sems, recv_sems, srcs, lands, _ = handle
    n, nl = len(srcs), len(lands)

    def body(*refs):
        src_refs, land_refs = refs[:n], refs[n:n + nl]
        send_s, recv_s = refs[n + nl], refs[n + nl + 1]
        x, y, c = _place()
        for k, src, dst, to in copies(src_refs, land_refs, x, y, c):
            cp = pltpu.make_async_remote_copy(src_ref=src, dst_ref=dst, send_sem=send_s.at[k], recv_sem=recv_s.at[k],
                                              device_id=to, device_id_type=MESH)
            cp.wait_send()
            cp.wait_recv()

    res = pl.pallas_call(
        body, name=name,
        out_shape=(*[pltpu.HBM(s.shape, s.dtype) for s in srcs], *[pltpu.HBM(l.shape, l.dtype) for l in lands]),
        in_specs=[HBM_SPEC] * (n + nl) + [SEM_SPEC, SEM_SPEC, ANY_SPEC],
        out_specs=tuple([HBM_SPEC] * (n + nl)),
        input_output_aliases={t: t for t in range(n + nl)},
        compiler_params=pltpu.CompilerParams(has_side_effects=EFFECT),
    )(*srcs, *lands, send_sems, recv_sems, after)
    return list(res[:n]), list(res[n:])


def _gather_copies(halves, arriving):
    def copies(src_refs, land_refs, x, y, c):
        q = 2 * x + y
        out = []
        for t, H in enumerate(halves):
            mine = land_refs[t].at[q, pl.ds(c * H, H), :]
            for j, (cx, cy) in enumerate(_other_chips(x, y)):
                theirs = land_refs[t].at[2 * cx + cy, pl.ds(c * H, H), :]
                out.append((3 * t + j, mine, theirs if arriving else mine, (cx, cy, c)))
        return out
    return copies


def _place_own(ops, after, name):
    n = len(ops)
    kinds = sorted({(o.shape, str(o.dtype)) for o in ops})
    kind_of = [kinds.index((o.shape, str(o.dtype))) for o in ops]

    def body(*refs):
        in_refs, out_refs = refs[:n], refs[n + 1:2 * n + 1]
        rd_sems, wr_sems, bufs = refs[2 * n + 1], refs[2 * n + 2], refs[2 * n + 3:]
        x, y, _ = _place()
        used = [0] * len(kinds)
        slot, busy = [], {}
        for t in range(n):
            slot.append((kind_of[t], used[kind_of[t]] % 2))
            used[kind_of[t]] += 1
        rd = lambda t: pltpu.make_async_copy(in_refs[t], bufs[slot[t][0]].at[slot[t][1]], rd_sems.at[t])
        wr = lambda t: pltpu.make_async_copy(bufs[slot[t][0]].at[slot[t][1]], out_refs[t].at[2 * x + y],
                                             wr_sems.at[t])
        rd(0).start()
        for t in range(n):
            rd(t).wait()
            wr(t).start()
            busy[slot[t]] = t
            if t + 1 < n:
                if slot[t + 1] in busy:
                    wr(busy.pop(slot[t + 1])).wait()
                rd(t + 1).start()
        for t in busy.values():
            wr(t).wait()

    return pl.pallas_call(
        body, name=name, in_specs=[HBM_SPEC] * n + [ANY_SPEC], out_specs=[HBM_SPEC] * n,
        out_shape=[jax.ShapeDtypeStruct((N_CHIPS,) + o.shape, o.dtype) for o in ops],
        scratch_shapes=[pltpu.SemaphoreType.DMA((n,)), pltpu.SemaphoreType.DMA((n,))]
        + [pltpu.VMEM((2,) + shape, jnp.dtype(dt)) for shape, dt in kinds],
        compiler_params=pltpu.CompilerParams(vmem_limit_bytes=VMEM_LIMIT),
    )(*ops, after)


def _gather_start(lands, after, name):
    return _ici_start([], lands, after, _gather_copies([l.shape[1] // 2 for l in lands], False), name)


def _gather_wait(handle, after, name):
    halves = [l.shape[1] // 2 for l in handle[3]]
    return _ici_wait(handle, after, _gather_copies(halves, True), name)


def _forward_copies(halves, arriving):
    def copies(src_refs, land_refs, x, y, c):
        out = []
        for t, H in enumerate(halves):
            for j, (cx, cy) in enumerate(_other_chips(x, y)):
                mine = land_refs[t].at[2 * cx + cy, pl.ds(c * H, H), :]
                theirs = land_refs[t].at[2 * cx + cy, pl.ds((1 - c) * H, H), :]
                out.append((3 * t + j, mine, theirs if arriving else mine, (x, y, 1 - c)))
        return out
    return copies


def _forward_start(lands, after, name):
    halves = [l.shape[1] // 2 for l in lands]
    return _ici_start([], lands, after, _forward_copies(halves, False), name)


def _forward_wait(handle, after, name):
    halves = [l.shape[1] // 2 for l in handle[3]]
    return _ici_wait(handle, after, _forward_copies(halves, True), name)[1]


def _swap_halves(ops, name):
    n = len(ops)

    def body(*refs):
        in_refs, out_refs, send_sems, recv_sems = refs[:n], refs[n:2 * n], refs[2 * n], refs[2 * n + 1]
        x, y, c = _place()
        cps = []
        for t in range(n):
            H = ops[t].shape[1] // 2
            cp = pltpu.make_async_remote_copy(src_ref=in_refs[t].at[:, pl.ds((1 - c) * H, H), :],
                                              dst_ref=out_refs[t], send_sem=send_sems.at[t],
                                              recv_sem=recv_sems.at[t], device_id=(x, y, 1 - c),
                                              device_id_type=MESH)
            cp.start()
            cps.append(cp)
        for cp in cps:
            cp.wait()

    return pl.pallas_call(
        body, name=name, in_specs=[HBM_SPEC] * n, out_specs=[HBM_SPEC] * n,
        out_shape=[jax.ShapeDtypeStruct((N_CHIPS, o.shape[1] // 2, o.shape[2]), o.dtype) for o in ops],
        scratch_shapes=[pltpu.SemaphoreType.DMA((n,)), pltpu.SemaphoreType.DMA((n,))],
    )(*ops)


def _sum_rows_tile(h):
    return h if h <= 512 else 512


def _pair_sum(g, t, cq, name):
    _, a, b = g.shape
    H = a // 2
    tr = _sum_rows_tile(H)

    def body(cq_ref, g_ref, t_ref, o_ref):
        o_ref[...] = (g_ref[...].astype(F32) + t_ref[...].astype(F32)).astype(o_ref.dtype)

    grid_spec = pltpu.PrefetchScalarGridSpec(
        num_scalar_prefetch=1, grid=(N_CHIPS, H // tr),
        in_specs=[pl.BlockSpec((None, None, tr, b), lambda j, i, cq_ref: (j, cq_ref[0], i, 0)),
                  pl.BlockSpec((None, tr, b), lambda j, i, cq_ref: (j, i, 0))],
        out_specs=pl.BlockSpec((None, tr, b), lambda j, i, cq_ref: (j, i, 0)))
    return pl.pallas_call(
        body, name=name, grid_spec=grid_spec, out_shape=_out(t.shape, BF16),
        compiler_params=_cparams(("parallel", "parallel")),
    )(cq, g.reshape(N_CHIPS, 2, H, b), t)


def _scatter_copies(src_refs, land_refs, x, y, c):
    out = []
    for j, (cx, cy) in enumerate(_other_chips(x, y)):
        for t in range(len(src_refs)):
            out.append((3 * t + j, src_refs[t].at[2 * cx + cy], land_refs[t].at[j], (cx, cy, c)))
    return out


def _scatter_start(ops, after, name):
    lands = [lax.empty((3,) + o.shape[1:], o.dtype) for o in ops]
    return _ici_start(ops, lands, after, _scatter_copies, name)


def _scatter_wait(handle, after, name):
    return _ici_wait(handle, after, _scatter_copies, name)


def _chip_sum(p, t, cq, name):
    _, H, b = p.shape
    tr = _sum_rows_tile(H)

    def body(cq_ref, p_ref, t_ref, o_ref):
        acc = p_ref[...].astype(F32)
        for j in range(3):
            acc = acc + t_ref[j].astype(F32)
        o_ref[...] = acc

    grid_spec = pltpu.PrefetchScalarGridSpec(
        num_scalar_prefetch=1, grid=(H // tr,),
        in_specs=[pl.BlockSpec((None, tr, b), lambda i, cq_ref: (cq_ref[1], i, 0)),
                  pl.BlockSpec((3, tr, b), lambda i, cq_ref: (0, i, 0))],
        out_specs=pl.BlockSpec((None, tr, b), lambda i, cq_ref: (cq_ref[0], i, 0)))
    out = pl.pallas_call(
        body, name=name, grid_spec=grid_spec, out_shape=_out((2, H, b), F32),
        compiler_params=_cparams(("parallel",)),
    )(cq, p, t)
    return out.reshape(2 * H, b)


def _join_copies(arriving):
    def copies(src_refs, land_refs, x, y, c):
        out = []
        for t, land in enumerate(land_refs):
            H = land.shape[0] // 2
            mine, theirs = land.at[pl.ds(c * H, H), :], land.at[pl.ds((1 - c) * H, H), :]
            out.append((t, mine, theirs if arriving else mine, (x, y, 1 - c)))
        return out
    return copies


def _join_start(fs, name):
    return _ici_start([], fs, jnp.zeros(TOKEN, F32), _join_copies(False), name, per_src=1)


def _join_wait(handle, after, name):
    return _ici_wait(handle, after, _join_copies(True), name)[1]


def _direct_copies(src_refs, land_refs, x, y, c):
    out = []
    for t in range(len(src_refs)):
        H = src_refs[t].shape[1] // 2
        for k in range(1, 8):
            px, py, pc = x ^ (k >> 2), y ^ ((k >> 1) & 1), c ^ (k & 1)
            out.append((7 * t + k - 1, src_refs[t].at[2 * px + py, pl.ds(pc * H, H), :], land_refs[t].at[k - 1],
                        (px, py, pc)))
    return out


def _direct_sum(g, t, cq, name):
    _, a, b = g.shape
    H = a // 2
    tr = _sum_rows_tile(H)

    def body(cq_ref, g_ref, t_ref, o_ref):
        acc = g_ref[...].astype(F32)
        for k in range(7):
            acc = acc + t_ref[k].astype(F32)
        o_ref[...] = acc

    grid_spec = pltpu.PrefetchScalarGridSpec(
        num_scalar_prefetch=1, grid=(H // tr,),
        in_specs=[pl.BlockSpec((None, None, tr, b), lambda i, cq_ref: (cq_ref[1], cq_ref[0], i, 0)),
                  pl.BlockSpec((7, tr, b), lambda i, cq_ref: (0, i, 0))],
        out_specs=pl.BlockSpec((None, tr, b), lambda i, cq_ref: (cq_ref[0], i, 0)))
    out = pl.pallas_call(
        body, name=name, grid_spec=grid_spec, out_shape=_out((2, H, b), F32),
        compiler_params=_cparams(("parallel",)),
    )(cq, g.reshape(N_CHIPS, 2, H, b), t)
    return out.reshape(a, b)


def _reduce_direct_start(gs, tag):
    lands = [lax.empty((7, g.shape[1] // 2, g.shape[2]), g.dtype) for g in gs]
    return _ici_start(gs, lands, jnp.zeros(TOKEN, F32), _direct_copies, "rs_direct_start_" + tag, per_src=7)


def _reduce_direct_finish(handle, cq, after, tag):
    gs, rs = _ici_wait(handle, after, _direct_copies, "rs_direct_wait_" + tag)
    fs = [_direct_sum(g, r, cq, "rs_direct_sum") for g, r in zip(gs, rs)]
    return _join_start(fs, "rs_join_start_" + tag)


def _reduce_scatter_start(gs, cq, after, tag):
    ts = _swap_halves(gs, "rs_swap_" + tag)
    ps = [_pair_sum(g, t, cq, "rs_pair_sum") for g, t in zip(gs, ts)]
    return _scatter_start(ps, after, "rs_scatter_start_" + tag)


def _reduce_scatter_finish(handle, cq, after, tag):
    ps, rs = _scatter_wait(handle, after, "rs_scatter_wait_" + tag)
    fs = [_chip_sum(p, r, cq, "rs_chip_sum") for p, r in zip(ps, rs)]
    return _join_start(fs, "rs_join_start_" + tag)


def _all_reduce_small(v):
    n = v.shape[0]

    def body(v_ref, out_ref, buf, send_sems, recv_sems):
        x, y, c = _place()
        me = 4 * x + 2 * y + c
        buf[me] = v_ref[...]
        cps = []
        for k in range(1, 8):
            peer = (x ^ (k >> 2), y ^ ((k >> 1) & 1), c ^ (k & 1))
            cp = pltpu.make_async_remote_copy(src_ref=v_ref, dst_ref=buf.at[me], send_sem=send_sems.at[k - 1],
                                              recv_sem=recv_sems.at[k - 1], device_id=peer, device_id_type=MESH)
            cp.start()
            cps.append(cp)
        for k in range(1, 8):
            px, py, pc = x ^ (k >> 2), y ^ ((k >> 1) & 1), c ^ (k & 1)
            land = buf.at[4 * px + 2 * py + pc]
            pltpu.make_async_remote_copy(src_ref=land, dst_ref=land, send_sem=send_sems.at[k - 1],
                                         recv_sem=recv_sems.at[k - 1], device_id=(px, py, pc),
                                         device_id_type=MESH).wait_recv()
        for cp in cps:
            cp.wait_send()
        acc = buf[0]
        for d in range(1, 8):
            acc = acc + buf[d]
        out_ref[...] = acc

    vm = pl.BlockSpec(memory_space=pltpu.VMEM)
    return pl.pallas_call(
        body, name="all_reduce_small", in_specs=[vm], out_specs=vm,
        out_shape=jax.ShapeDtypeStruct((n, 128), F32),
        scratch_shapes=[pltpu.VMEM((8, n, 128), F32), pltpu.SemaphoreType.DMA((7,)), pltpu.SemaphoreType.DMA((7,))],
    )(v)


SMALL_GATHER = (16, 1024)
SMALL_FULL = sum(_size(_full_shape(n)) for n in SMALL)
SMALL_FULL_ROWS = -(-(SMALL_FULL + 1) // 128 // 8) * 8


def _layer_shards(w, i, q):
    kind, j = MIXER[i % 3], i // 3
    out = {n: w[n][i].astype(BF16) for n in COMMON_BIG}
    if kind == 'gla':
        win = jnp.zeros((D_MODEL, GLA_WIN), F32)
        win = lax.dynamic_update_slice(win, w['gla_w_in'][j], (0, (GLA_SHARD - GLA_WIN_STEP) * q))
        out['gla_w_in'] = win.astype(BF16)
        out['gla_w_out'] = w['gla_w_out'][j].astype(BF16)
    elif kind == 'mla':
        out['mla_w_in'] = jnp.pad(w['mla_w_in'][j], ((0, 0), (0, MLA_IN_PAD - MLA_IN))).astype(BF16)
        for n in ('mla_w_uq', 'mla_w_ukv', 'mla_w_out'):
            out[n] = w[n][j].astype(BF16)
    else:
        out['conv_w_in'] = w['conv_w_in'][j].astype(BF16)
        out['conv_w_out'] = w['conv_w_out'][j].astype(BF16)
    return out


def _rows_joined(g):
    return g.reshape(g.shape[0] * g.shape[1], g.shape[2])


def _cols_joined(g):
    return jnp.moveaxis(g, 0, 1).reshape(g.shape[1], -1)


def _layer_weights(g, i):
    kind = MIXER[i % 3]
    W = {}
    if 'mlp_w1' in g:
        W = {'w1': g['mlp_w1'], 'w2': _rows_joined(g['mlp_w2']), 'gate': _rows_joined(g['ple_w_gate']),
             'proj': g['ple_w_proj']}
    if kind == 'gla' and 'gla_w_out' in g:
        W['w_out'] = _rows_joined(g['gla_w_out'])
    if kind == 'gla' and 'gla_w_in' in g:
        parts = []
        for qq in range(N_CHIPS):
            lo = g['gla_w_in'][qq][:, :128]
            if qq > 0:
                lo = lo + g['gla_w_in'][qq - 1][:, GLA_WIN_STEP:]
            parts += [lo, g['gla_w_in'][qq][:, 128:GLA_WIN_STEP]]
        parts.append(g['gla_w_in'][N_CHIPS - 1][:, GLA_WIN_STEP:])
        W['w_in'] = jnp.concatenate(parts, axis=1)
    elif kind == 'mla':
        W['w_in'] = _rows_joined(g['mla_w_in'])
        uq = _cols_joined(g['mla_w_uq']).reshape(MLA_QR, MLA_HEADS, MLA_NOPE + MLA_ROPE)
        W['w_uq'] = jnp.pad(uq, ((0, 0), (0, 0), (0, MLA_QH - MLA_NOPE - MLA_ROPE))).reshape(MLA_QR, -1)
        ukv = _cols_joined(g['mla_w_ukv']).reshape(MLA_KVR, MLA_HEADS, 2, 128)
        W['w_ukv'] = ukv.transpose(0, 2, 1, 3).reshape(MLA_KVR, -1)
        W['w_out'] = _rows_joined(g['mla_w_out'])
    elif kind == 'conv':
        W['w_in'] = g['conv_w_in']
        W['w_out'] = _rows_joined(g['conv_w_out'])
    return W


def _pack_small_shards(w):
    flat = jnp.concatenate([w[n].reshape(-1) for n in SMALL_SHARDED])
    return jnp.pad(flat, (0, _size(SMALL_GATHER) - flat.shape[0])).reshape(SMALL_GATHER)


def _unpack_small_gathered(g):
    flat, out, off = g.reshape(N_CHIPS, -1), {}, 0
    for n in SMALL_SHARDED:
        shape, ax = WSPEC[n]
        seg = flat[:, off:off + _size(shape)].reshape((N_CHIPS,) + shape)
        out[n] = jnp.moveaxis(seg, 0, ax).reshape(_full_shape(n))
        off += _size(shape)
    return out


def _pack_small(vals, loss):
    flat = jnp.concatenate([vals[n].reshape(-1) for n in SMALL] + [loss.reshape(1)])
    return jnp.pad(flat, (0, SMALL_FULL_ROWS * 128 - flat.shape[0])).reshape(SMALL_FULL_ROWS, 128)


def _unpack_small(packed, q):
    flat = packed.reshape(-1)
    out, off = {}, 0
    for n in SMALL:
        shape, ax = WSPEC[n]
        full = flat[off:off + _size(_full_shape(n))].reshape(_full_shape(n))
        off += _size(_full_shape(n))
        out[n] = full if ax is None else lax.dynamic_slice_in_dim(full, q * shape[ax], shape[ax], axis=ax)
    return out


def _row_shards(dw):
    return dw.reshape(N_CHIPS, dw.shape[0] // N_CHIPS, dw.shape[1])


def _col_shards(dw):
    return jnp.moveaxis(dw.reshape(dw.shape[0], N_CHIPS, -1), 1, 0)


def _row(v):
    return v.reshape(1, -1)


def _layer_fwd(i, xin, xin_b, p_i, W, sm, cosp, sinp, rest=None, mid=None):
    kind, j = MIXER[i % 3], i // 3
    sv = {'xin': xin, 'xin_b': xin_b}
    if kind == 'gla':
        w_up = jnp.pad(sm['gla_w_gate_up'][j].astype(BF16), ((0, 128 - GLA_RANK), (0, 0)))
        pin = _mm(xin_b, W['w_in'], name="gla_in", tn=640, tm=FULL_ROWS)
        la = _mm(pin, w_up, name="gla_gate", K=128, tk=128, a_off=(0, (GLA_IN_PAD - 128) // 128), tn=512,
                 extras=[(_row(sm['gla_b_gate'][j]), 'n')],
                 epilogue=lambda acc, b: (_log_sigmoid(acc + b) * (1.0 / GLA_TAU),))
        o, states = _gla_fwd(pin, la)
        yb = _gla_post_fwd(o, pin, _row(sm['gla_norm_g'][j]))
        if rest is not None:
            W = {**W, **rest(yb)}
        mixed = yb
        sv.update(w_up=w_up, pin=pin, la=la, o=o, states=states, yb=yb)
    elif kind == 'mla':
        gq, gkv = sm['mla_q_norm'][j:j + 1], sm['mla_kv_norm'][j:j + 1]
        cin = _mm(xin_b, W['w_in'], name="mla_in", tn=640, tm=FULL_ROWS)
        qn, kvn, kr = _mla_norm_fwd(cin, gq, gkv, cosp, sinp)
        qr = _mla_qrope_fwd(_mm(qn, W['w_uq'], name="mla_uq"), cosp, sinp)
        knv = _mm(kvn, W['w_ukv'], name="mla_ukv", out_dtypes=(BF16,))
        o = _mla_attn_fwd(qr, knv, kr)
        ob = o.astype(BF16)
        mixed = ob
        sv.update(gq=gq, gkv=gkv, cin=cin, qn=qn, kvn=kvn, kr=kr, qr=qr, knv=knv, o=o, ob=ob)
    else:
        w8 = jnp.pad(sm['conv_w'][j], ((0, 5), (0, 0)))
        bcu = _mm(xin_b, W['w_in'], name="conv_in", tn=768, b_sh=True, tm=FULL_ROWS)
        yb = _conv_fwd(bcu, w8)
        mixed = yb
        sv.update(w8=w8, bcu=bcu, yb=yb)
    g0, b0 = _row(sm['ln_g'][i, 0]), _row(sm['ln_b'][i, 0])
    g1, b1 = _row(sm['ln_g'][i, 1]), _row(sm['ln_b'][i, 1])
    ln = dict(tm=512, tn=D_MODEL, out_dtypes=(F32, BF16, F32), epilogue=_ln_fwd_epilogue)
    x1, x1b, v0 = _mm(mixed, W['w_out'], name="mix_out_ln", extras=[(xin, 'mn'), (g0, 'n'), (b0, 'n')], **ln)
    ab, dadu = _mm(x1b, W['w1'], name="mlp_up", out_dtypes=(BF16, BF16), b_sh=True, tm=FULL_ROWS,
                   epilogue=lambda acc: (jnp.square(jnp.maximum(acc, 0.0)), 2.0 * jnp.maximum(acc, 0.0)))
    x2, x2b, v1 = _mm(ab, W['w2'], name="mlp_down_ln", tk=D_FF, extras=[(x1, 'mn'), (g1, 'n'), (b1, 'n')], **ln)
    order = [(mid(x2b), 'whole')] if mid else []
    pp = _mm(p_i, W['proj'], name="ple_proj", tn=256, b_sh=True, extras=order,
             epilogue=lambda acc, *unused: (acc,))
    z, x3, x3b = _mm(x2b, W['gate'], name="ple_gate", out_dtypes=(F32, F32, BF16),
                     extras=[(x2, 'mn'), (pp, 'mn')],
                     epilogue=lambda acc, xv, pv: (acc,) + (xv + jax.nn.sigmoid(acc) * pv,) * 2)
    sv.update(v0=v0, x1b=x1b, ab=ab, dadu=dadu, v1=v1, x2b=x2b, pp=pp, z=z, g0=g0, g1=g1)
    return x3, x3b, sv, W


def _layer_bwd(i, grads_in, p_i, W, sm, sv, cosp, sinp, token, early=None, below=None):
    kind, j = MIXER[i % 3], i // 3
    big, small = {}, {}
    dx, dpp_b, dz_b = grads_in
    big['ple_w_proj'] = _mm(p_i, dpp_b, ta=True, name="ple_proj_dw", tn=256, out_sh=True, out_dtypes=(BF16,))
    big['ple_w_gate'] = _row_shards(_mm(sv['x2b'], dz_b, ta=True, name="dw_dd", out_dtypes=(BF16,)))
    ln = dict(tb=True, tm=512, tn=D_MODEL, out_dtypes=(F32, BF16), n_sums=2)
    (dv1, dv1b), (dg1, db1) = _mm(dz_b, W['gate'], name="ple_gate_dx_ln", epilogue=_ln_bwd_epilogue(1.0),
                                  extras=[(dx, 'mn'), (sv['v1'], 'mn'), (sv['g1'], 'n'), (token, 'whole')], **ln)
    big['mlp_w2'] = _row_shards(_mm(sv['ab'], dv1b, ta=True, name="mlp_down_dw", out_dtypes=(BF16,)))
    dub = _mm(dv1b, W['w2'], tb=True, name="mlp_down_dx", out_dtypes=(BF16,), tm=FULL_ROWS,
              extras=[(sv['dadu'], 'mn')], epilogue=lambda acc, d: (acc * d.astype(F32),))
    big['mlp_w1'] = _mm(sv['x1b'], dub, ta=True, name="mlp_up_dw", out_sh=True, out_dtypes=(BF16,))
    order = []
    if early is not None:
        order, big = [(early(big), 'whole')], {}
    (dv0, dv0b), (dg0, db0) = _mm(dub, W['w1'], name="mlp_up_dx_ln", b_sh=True, tk=D_FF, epilogue=_ln_bwd_epilogue(ALPHA),
                                  extras=[(dv1, 'mn'), (sv['v0'], 'mn'), (sv['g0'], 'n')] + order, **ln)
    small['ln_g'] = jnp.stack([dg0[0], dg1[0]])
    small['ln_b'] = jnp.stack([db0[0], db1[0]])
    resid = dict(tb=True, tn=D_MODEL, tm=512 if below else 1024, epilogue=_input_grad_epilogue,
                 extras=[(dv0, 'mn')] + [(a, 'mn') for a in below or ()],
                 out_dtypes=(F32, BF16, BF16) if below else (F32,))
    if kind == 'gla':
        big['gla_w_out'] = _row_shards(_mm(sv['yb'], dv0b, ta=True, name="dw_dd", out_dtypes=(BF16,)))
        dy = _mm(dv0b, W['w_out'], tb=True, name="dx_dd", tn=1024)
        do, dr_b, dng = _gla_post_bwd(dy, sv['o'], sv['pin'], _row(sm['gla_norm_g'][j]))
        dq_b, dk_b, dvv_b, dla = _gla_bwd(sv['pin'], sv['la'], sv['states'], do)
        dzg_b, dbg = _gla_gate_bwd(dla, sv['la'])
        dw_up = _mm(sv['pin'], dzg_b, ta=True, name="gla_gate_dw", M=128, tm=128,
                    a_off=(0, (GLA_IN_PAD - 128) // 128))
        dglr_b = _mm(dzg_b, sv['w_up'], tb=True, name="gla_gate_dx", out_dtypes=(BF16,))
        dpin_b = jnp.concatenate([dq_b, dk_b, dvv_b, dr_b, dglr_b], axis=1)
        dw_in = _mm(sv['xin_b'], dpin_b, ta=True, name="gla_in_dw", tn=640, out_dtypes=(BF16,))
        dxin = _mm(dpin_b, W['w_in'], name="gla_in_dx", tk=GLA_IN_PAD, **resid)
        big['gla_w_in'] = jnp.stack([dw_in[:, GLA_WIN_STEP * qq:GLA_WIN_STEP * qq + GLA_WIN]
                                     for qq in range(N_CHIPS)])
        small.update(gla_w_gate_up=dw_up[:GLA_RANK], gla_b_gate=dbg[0], gla_norm_g=dng[0])
    elif kind == 'mla':
        big['mla_w_out'] = _row_shards(_mm(sv['ob'], dv0b, ta=True, name="dw_dd", out_dtypes=(BF16,)))
        do = _mm(dv0b, W['w_out'], tb=True, name="dx_dd", tn=1024)
        dqr, dkn_b, dvv_b, dkr = _mla_attn_bwd(sv['qr'], sv['knv'], sv['kr'], sv['o'], do)
        dq_b = _mla_qrope_bwd(dqr, cosp, sinp)
        dw_uq = _mm(sv['qn'], dq_b, ta=True, name="mla_up_dw", out_dtypes=(BF16,))
        dqn = _mm(dq_b, W['w_uq'], tb=True, name="mla_up_dx")
        dknv_b = jnp.concatenate([dkn_b, dvv_b], axis=1)
        dw_ukv = _mm(sv['kvn'], dknv_b, ta=True, name="mla_up_dw", out_dtypes=(BF16,))
        dkvn = _mm(dknv_b, W['w_ukv'], tb=True, name="mla_up_dx")
        dcin_b, dgq, dgkv = _mla_norm_bwd(sv['cin'], dqn, dkvn, dkr, sv['gq'], sv['gkv'], cosp, sinp)
        big['mla_w_in'] = _row_shards(_mm(sv['xin_b'], dcin_b, ta=True, name="mla_in_dw", tn=640,
                                          out_dtypes=(BF16,)))
        dxin = _mm(dcin_b, W['w_in'], name="mla_in_dx", tk=MLA_IN_PAD, **resid)
        big['mla_w_uq'] = _col_shards(
            dw_uq.reshape(MLA_QR, MLA_HEADS, MLA_QH)[:, :, :MLA_NOPE + MLA_ROPE].reshape(MLA_QR, -1))
        big['mla_w_ukv'] = _col_shards(
            dw_ukv.reshape(MLA_KVR, 2, MLA_HEADS, 128).transpose(0, 2, 1, 3).reshape(MLA_KVR, -1))
        small.update(mla_q_norm=dgq[0], mla_kv_norm=dgkv[0])
    else:
        big['conv_w_out'] = _row_shards(_mm(sv['yb'], dv0b, ta=True, name="dw_dd", out_dtypes=(BF16,)))
        dy = _mm(dv0b, W['w_out'], tb=True, name="dx_dd", tn=1024)
        db_b, dc_b, du_b, dw8 = _conv_bwd(sv['bcu'], sv['w8'], dy)
        dbcu_b = jnp.concatenate([db_b, dc_b, du_b], axis=1)
        big['conv_w_in'] = _mm(sv['xin_b'], dbcu_b, ta=True, name="conv_in_dw", tn=768, out_sh=True,
                               out_dtypes=(BF16,))
        dxin = _mm(dbcu_b, W['w_in'], name="conv_in_dx", tk=3 * D_MODEL, b_sh=True, **resid)
        small['conv_w'] = dw8[:3]
    return (dxin if below else (dxin,)), big, small


def _rope_tables(positions):
    inv_freq = ROPE_BASE ** (-jnp.arange(0, MLA_ROPE // 2, dtype=F32) * (2.0 / MLA_ROPE))
    ang = positions.astype(F32)[:, None] * inv_freq
    zeros = jnp.zeros((positions.shape[0], 64), F32)
    return (jnp.concatenate([jnp.cos(ang), jnp.cos(ang), zeros], axis=1),
            jnp.concatenate([jnp.sin(ang), jnp.sin(ang), zeros], axis=1))


FIRST_NEEDED = ['gla_w_in']


def _start_gathers(w, q):
    token, started = jnp.zeros(TOKEN, F32), []
    for i in range(DEPTH):
        sh = _layer_shards(w, i, q)
        for k, names in enumerate([list(sh)] if i > 0 else [FIRST_NEEDED, [n for n in sh if n not in FIRST_NEEDED]]):
            ops = [sh[n] for n in names]
            if i == 0 and k == 0:
                ops.append(_pack_small_shards(w))
            tag = "l%d%s" % (i, "ab"[k] if i == 0 else "")
            handle = _gather_start(_place_own(ops, token, "ag_own_" + tag), token, "ag_start_" + tag)
            token = handle[4]
            started.append((handle, names, tag))
    return started, token


def _pass_on(entry, after):
    handle, names, tag = entry
    _, lands = _gather_wait(handle, after, "ag_wait_" + tag)
    passing = _forward_start(lands, jnp.zeros(TOKEN, F32), "ag_pass_start_" + tag)
    return (passing, names, tag), passing[4]


def _gathered(passed, after):
    passing, names, tag = passed
    got = _forward_wait(passing, after, "ag_pass_wait_" + tag)
    return dict(zip(names, got)), got[-1]


def _local_shard_grad(name, g, q):
    if name == 'gla_w_in':
        return lax.dynamic_slice_in_dim(g, (GLA_SHARD - GLA_WIN_STEP) * q, GLA_SHARD, axis=1)
    if name == 'mla_w_in':
        return g[:, :MLA_IN]
    return g


def kernel(x, p, positions, gla_w_in, gla_w_gate_up, gla_b_gate, gla_norm_g, gla_w_out, mla_w_in, mla_q_norm, mla_kv_norm, mla_w_uq, mla_w_ukv, mla_w_out, conv_w_in, conv_w, conv_w_out, ln_g, ln_b, mlp_w1, mlp_w2, ple_w_gate, ple_w_proj, loss_target, m_gla_w_in, m_gla_w_gate_up, m_gla_b_gate, m_gla_norm_g, m_gla_w_out, m_mla_w_in, m_mla_q_norm, m_mla_kv_norm, m_mla_w_uq, m_mla_w_ukv, m_mla_w_out, m_conv_w_in, m_conv_w, m_conv_w_out, m_ln_g, m_ln_b, m_mlp_w1, m_mlp_w2, m_ple_w_gate, m_ple_w_proj, v_gla_w_in, v_gla_w_gate_up, v_gla_b_gate, v_gla_norm_g, v_gla_w_out, v_mla_w_in, v_mla_q_norm, v_mla_kv_norm, v_mla_w_uq, v_mla_w_ukv, v_mla_w_out, v_conv_w_in, v_conv_w, v_conv_w_out, v_ln_g, v_ln_b, v_mlp_w1, v_mlp_w2, v_ple_w_gate, v_ple_w_proj):
    args = locals()
    w = {n: args[n] for n in WNAMES}
    m = {n: args['m_' + n] for n in WNAMES}
    v = {n: args['v_' + n] for n in WNAMES}
    q = 2 * lax.axis_index("x") + lax.axis_index("y")
    cq = jnp.stack([lax.axis_index("c"), q]).astype(jnp.int32)

    cosp, sinp = _rope_tables(positions[0])
    started, after = _start_gathers(w, q)
    xin, saved, layers, sm = x[0], [], [], None
    xin_b = xin.astype(BF16)
    passed, after = _pass_on(started[0], after)
    for i in range(DEPTH):
        got, last = _gathered(passed, after)
        rest = mid = None
        if i == 0:
            sm = _unpack_small_gathered(last)
            sm['mla_q_norm'], sm['mla_kv_norm'] = w['mla_q_norm'], w['mla_kv_norm']
            rest = lambda after: _layer_weights(_gathered(*_pass_on(started[1], after))[0], 0)
        coming = {}
        if i + 1 < DEPTH:
            def mid(after, entry=started[i + 2], coming=coming):
                coming['passed'], token = _pass_on(entry, after)
                return token
        xin, xin_b, sv, W = _layer_fwd(i, xin, xin_b, p[i, 0], _layer_weights(got, i), sm, cosp, sinp, rest, mid)
        layers.append(W)
        saved.append(sv)
        passed, after = coming.get('passed'), xin
    *grads_in, loss_cols = _loss_head(xin, loss_target[0], saved[-1]['z'], saved[-1]['pp'])
    loss = jnp.sum(loss_cols[0])

    gbig = {n: [None] * WSPEC[n][0][0] for n in BIG}
    gsmall = {n: [None] * _full_shape(n)[0] for n in SMALL}
    pending = []

    def start(grads, i, tag):
        names = list(grads)
        gs = [grads[n] for n in names]
        handle = _reduce_direct_start(gs, tag) if i > 0 else _reduce_scatter_start(gs, cq, jnp.zeros(TOKEN, F32), tag)
        pending.append((handle, names, i, tag))
        return handle[4]

    joining = []

    def finish(above, after, token):
        for entry in [e for e in pending if e[2] > above]:
            pending.remove(entry)
            handle, names, i, tag = entry
            handle = (_reduce_direct_finish if i > 0 else _reduce_scatter_finish)(handle, cq, after, tag)
            joining.append((handle, names, i, tag))
            token = token + handle[4]
        return token

    token = jnp.zeros(TOKEN, F32)
    for i in reversed(range(DEPTH)):
        early = (lambda grads: start(grads, 0, "l0a")) if i == 0 else None
        below = (saved[i - 1]['z'], saved[i - 1]['pp']) if i > 0 else None
        grads_in, big, small = _layer_bwd(i, grads_in, p[i, 0], layers[i], sm, saved[i], cosp, sinp, token, early,
                                          below)
        dx = grads_in[0]
        token = finish(i + 1, dx, start(big, i, "l%d%s" % (i, "b" if i == 0 else "")))
        for n, g in small.items():
            gsmall[n][i if n in ('ln_g', 'ln_b') else i // 3] = g
    finish(-1, token, token)
    for handle, names, i, tag in joining:
        for n, g in zip(names, _join_wait(handle, joining[-1][0][4], "rs_join_wait_" + tag)):
            gbig[n][i if n in COMMON_BIG else i // 3] = _local_shard_grad(n, g, q)
    small_sum = _all_reduce_small(_pack_small({n: jnp.stack(g) for n, g in gsmall.items()}, loss))
    gsm, loss = _unpack_small(small_sum, q), small_sum.reshape(-1)[SMALL_FULL]

    grad, delta, new_m, new_v = {}, {}, {}, {}
    for n in BIG:
        update = _adamw_shard_major if n == 'gla_w_in' else _adamw
        grad[n], delta[n], new_m[n], new_v[n] = update(w[n], m[n], v[n], gbig[n], "adamw_" + n)
    flat2 = lambda a: a.reshape(-1, a.shape[-1])
    res = _adamw_small(*[[flat2(d[n]) for n in SMALL] for d in (w, gsm, m, v)])
    for k, out in enumerate((grad, delta, new_m, new_v)):
        for n, r in zip(SMALL, res[k::4]):
            out[n] = r.reshape(WSPEC[n][0])
    return (loss, dx[None], *[grad[n] for n in WNAMES], *[delta[n] for n in WNAMES],
            *[new_m[n] for n in WNAMES], *[new_v[n] for n in WNAMES])
```

```python
import functools

import numpy as np
import jax
import jax.numpy as jnp
from jax import lax
from jax.experimental import pallas as pl
from jax.experimental.pallas import tpu as pltpu

F32 = jnp.float32
BF16 = jnp.bfloat16
MESH = pl.DeviceIdType.MESH

D_MODEL = 1024
DEPTH = 4
CHUNK = 64
ALPHA = (2 * DEPTH) ** 0.25
LN_EPS = 1e-5
RMS_EPS = 1e-6
D_FF = 4 * D_MODEL
GLA_HEADS = 4
GLA_DK = 128
GLA_DV = 256
GLA_RANK = 16
GLA_TAU = 16.0
GLA_HK = GLA_HEADS * GLA_DK
GLA_HV = GLA_HEADS * GLA_DV
GLA_IN = 2 * GLA_HK + GLA_HV + D_MODEL + GLA_RANK
GLA_IN_PAD = 2 * GLA_HK + GLA_HV + D_MODEL + 128
GLA_SHARD = GLA_IN // 4
GLA_WIN = 896
GLA_WIN_STEP = 768
MLA_HEADS = 8
MLA_NOPE = 128
MLA_ROPE = 64
MLA_V = 128
MLA_QR = 256
MLA_KVR = 256
MLA_IN = MLA_QR + MLA_KVR + MLA_ROPE
MLA_IN_PAD = MLA_QR + MLA_KVR + 128
MLA_QH = 256
ROPE_BASE = 10000.0
ADAM_LR = 0.001
ADAM_B1 = 0.9
ADAM_B2 = 0.999
ADAM_EPS = 1e-08
ADAM_WD = 0.01
ADAM_STEP = 10

VMEM_LIMIT = 48 * 1024 * 1024
FULL_ROWS = 2048
N_CHIPS = 4

WSPEC = {
    'gla_w_in': ((2, 1024, 772), 2), 'gla_w_gate_up': ((2, 16, 128), 2), 'gla_b_gate': ((2, 128), 1),
    'gla_norm_g': ((2, 64), 1), 'gla_w_out': ((2, 256, 1024), 1), 'mla_w_in': ((1, 256, 576), 1),
    'mla_q_norm': ((1, 256), None), 'mla_kv_norm': ((1, 256), None), 'mla_w_uq': ((1, 256, 384), 2),
    'mla_w_ukv': ((1, 256, 512), 2), 'mla_w_out': ((1, 256, 1024), 1), 'conv_w_in': ((1, 1024, 768), 2),
    'conv_w': ((1, 3, 256), 2), 'conv_w_out': ((1, 256, 1024), 1), 'ln_g': ((4, 2, 256), 2),
    'ln_b': ((4, 2, 256), 2), 'mlp_w1': ((4, 1024, 1024), 2), 'mlp_w2': ((4, 1024, 1024), 1),
    'ple_w_gate': ((4, 256, 1024), 1), 'ple_w_proj': ((4, 256, 256), 2),
}
WNAMES = list(WSPEC)
BIG = ['gla_w_in', 'gla_w_out', 'mla_w_in', 'mla_w_uq', 'mla_w_ukv', 'mla_w_out', 'conv_w_in', 'conv_w_out',
       'mlp_w1', 'mlp_w2', 'ple_w_gate', 'ple_w_proj']
SMALL_SHARDED = ['gla_w_gate_up', 'gla_b_gate', 'gla_norm_g', 'conv_w', 'ln_g', 'ln_b']
SMALL = SMALL_SHARDED + ['mla_q_norm', 'mla_kv_norm']
MIXER = ['gla', 'mla', 'conv']
COMMON_BIG = ['mlp_w1', 'mlp_w2', 'ple_w_gate', 'ple_w_proj']


def _size(shape):
    return int(np.prod(shape))


def _full_shape(name):
    shape, ax = WSPEC[name]
    if ax is None:
        return shape
    return tuple(s * N_CHIPS if i == ax else s for i, s in enumerate(shape))


def _cparams(sem=None):
    return pltpu.CompilerParams(dimension_semantics=sem, vmem_limit_bytes=VMEM_LIMIT)


def _out(shape, dtype):
    return pltpu.HBM(shape, dtype)


def _hbm(v):
    return pltpu.with_memory_space_constraint(v, pltpu.HBM)


def _mm(a, b, *, name, ta=False, tb=False, M=None, N=None, K=None, out_dtypes=(F32,), epilogue=None, extras=(),
        tm=1024, tn=512, tk=None, a_off=(0, 0), b_sh=False, out_sh=False, n_sums=0):
    if M is None:
        M = a.shape[1] if ta else a.shape[0]
    if K is None:
        K = a.shape[0] if ta else a.shape[1]
    if b_sh:
        kw, nq = b.shape[1], b.shape[2]
        n_b, k_b = (kw, N_CHIPS * nq) if tb else (N_CHIPS * nq, kw)
        N = n_b if N is None else N
        assert K == k_b
    elif N is None:
        N = b.shape[0] if tb else b.shape[1]
    if tk is None:
        tk = FULL_ROWS if ta else 1024
    tm, tn, tk = min(tm, M), min(tn, N), min(tk, K)
    assert M % tm == 0 and N % tn == 0 and K % tk == 0, (name, M, N, K, tm, tn, tk)
    nk = K // tk
    n_ex, n_out = len(extras), len(out_dtypes)
    assert n_sums == 0 or tn == N

    n_b = N_CHIPS if (b_sh and tb and tk == K) else 1

    def body(a_ref, *rest):
        b_refs, rest = rest[:n_b], rest[n_b:]
        ex_refs, out_refs = rest[:n_ex], rest[n_ex:n_ex + n_out]
        sum_refs = rest[n_ex + n_out:n_ex + n_out + n_sums]
        first_rows = pl.program_id(0) == 0
        dims = ((((0,) if ta else (1,)), ((1,) if tb else (0,))), ((), ()))
        if n_b == 1:
            part = lax.dot_general(a_ref[...].astype(BF16), b_refs[0][...].astype(BF16), dims,
                                   preferred_element_type=F32)
        else:
            part = sum(lax.dot_general(a_ref[:, s * nq:(s + 1) * nq].astype(BF16), b_refs[s][...].astype(BF16), dims,
                                       preferred_element_type=F32) for s in range(n_b))

        def finish(acc):
            res = (acc,) if epilogue is None else epilogue(acc, *[r[...] for r in ex_refs])
            if n_sums:
                res, sums = res

                @pl.when(first_rows)
                def _():
                    for r in sum_refs:
                        r[...] = jnp.zeros(r.shape, F32)

                for r, v in zip(sum_refs, sums):
                    r[...] += jnp.broadcast_to(v, r.shape)
            for r, v in zip(out_refs, res):
                r[...] = v.astype(r.dtype)

        if nk == 1:
            finish(part)
        else:
            acc_ref = rest[-1]
            k = pl.program_id(2)

            @pl.when(k == 0)
            def _():
                acc_ref[...] = part

            @pl.when(k > 0)
            def _():
                acc_ref[...] += part

            @pl.when(k == nk - 1)
            def _():
                finish(acc_ref[...])

    if ta:
        a_spec = pl.BlockSpec((tk, tm), lambda i, j, k: (k + a_off[0], i + a_off[1]))
    else:
        a_spec = pl.BlockSpec((tm, tk), lambda i, j, k: (i + a_off[0], k + a_off[1]))
    once = dict(pipeline_mode=pl.Buffered(1)) if (tn == N and nk == 1) else {}
    if n_b > 1:
        b_specs = [pl.BlockSpec((None, tn, nq), functools.partial(lambda i, j, k, s: (s, j, 0), s=s), **once)
                   for s in range(n_b)]
    elif b_sh and tb:
        assert nq % tk == 0
        per = nq // tk
        b_spec = pl.BlockSpec((None, tn, tk), lambda i, j, k: (k // per, j, k % per), **once)
    elif b_sh:
        assert nq % tn == 0
        per = nq // tn
        b_spec = pl.BlockSpec((None, tk, tn), lambda i, j, k: (j // per, k, j % per), **once)
    elif tb:
        b_spec = pl.BlockSpec((tn, tk), lambda i, j, k: (j, k), **once)
    else:
        b_spec = pl.BlockSpec((tk, tn), lambda i, j, k: (k, j), **once)
    if n_b == 1:
        b_specs = [b_spec]
    ex_specs = []
    for arr, kind in extras:
        if kind == 'mn':
            ex_specs.append(pl.BlockSpec((tm, tn), lambda i, j, k: (i, j)))
        elif kind == 'n':
            ex_specs.append(pl.BlockSpec((1, tn), lambda i, j, k: (0, j)))
        else:
            ex_specs.append(pl.BlockSpec(arr.shape, lambda i, j, k: (0, 0)))
    if out_sh:
        assert (N // N_CHIPS) % tn == 0
        per_o = N // N_CHIPS // tn
        o_spec = pl.BlockSpec((None, tm, tn), lambda i, j, k: (j // per_o, i, j % per_o))
        o_shape = (N_CHIPS, M, N // N_CHIPS)
    else:
        o_spec = pl.BlockSpec((tm, tn), lambda i, j, k: (i, j))
        o_shape = (M, N)
    outs = pl.pallas_call(
        body, name=name, grid=(M // tm, N // tn, nk),
        in_specs=[a_spec] + b_specs + ex_specs,
        out_specs=[o_spec for _ in out_dtypes] + [pl.BlockSpec((8, N), lambda i, j, k: (0, 0))] * n_sums,
        out_shape=[_out(o_shape, d) for d in out_dtypes] + [_out((8, N), F32)] * n_sums,
        scratch_shapes=[pltpu.VMEM((tm, tn), F32)] if nk > 1 else [],
        compiler_params=_cparams(("arbitrary" if n_sums else "parallel", "parallel", "arbitrary")),
    )(a, *[b] * n_b, *[e[0] for e in extras])
    if n_sums:
        return tuple(outs[:n_out]), tuple(outs[n_out:])
    return outs[0] if n_out == 1 else tuple(outs)


def _rowwise(fn, *, name, rows, pars=(), outs=(), accs=(), tm=256):
    S = rows[0][0].shape[0]
    tm = min(tm, S)
    assert S % tm == 0
    n_r, n_p, n_o, n_a = len(rows), len(pars), len(outs), len(accs)

    def body(*refs):
        r_refs, p_refs = refs[:n_r], refs[n_r:n_r + n_p]
        o_refs, a_refs = refs[n_r + n_p:n_r + n_p + n_o], refs[n_r + n_p + n_o:]
        o_vals, a_vals = fn([r[...] for r in r_refs], [p[...] for p in p_refs])
        for r, v in zip(o_refs, o_vals):
            r[...] = v.astype(r.dtype)
        if n_a:
            i = pl.program_id(0)

            @pl.when(i == 0)
            def _():
                for r in a_refs:
                    r[...] = jnp.zeros(r.shape, r.dtype)

            for r, v in zip(a_refs, a_vals):
                r[...] += jnp.broadcast_to(v, r.shape)

    in_specs = [pl.BlockSpec((tm, w), functools.partial(lambda i, o: (i, o), o=off)) for _, w, off in rows]
    in_specs += [pl.BlockSpec(p.shape, functools.partial(lambda i, nd: (0,) * nd, nd=p.ndim)) for p in pars]
    out_specs = [pl.BlockSpec((tm, w), lambda i: (i, 0)) for w, _ in outs]
    out_specs += [pl.BlockSpec((8, w), lambda i: (0, 0)) for w in accs]
    out_shape = [_out((S, w), d) for w, d in outs]
    out_shape += [_out((8, w), F32) for w in accs]
    res = pl.pallas_call(
        body, name=name, grid=(S // tm,), in_specs=in_specs, out_specs=out_specs, out_shape=out_shape,
        compiler_params=_cparams(("arbitrary",)),
    )(*[r[0] for r in rows], *pars)
    return tuple(res)


def _colsum(v):
    return jnp.sum(v, axis=0, keepdims=True)


def _ln_stats(v):
    mu = jnp.mean(v, axis=-1, keepdims=True)
    d = v - mu
    var = jnp.mean(d * d, axis=-1, keepdims=True)
    rstd = lax.rsqrt(var + LN_EPS)
    return d * rstd, rstd


def _ln_fwd_epilogue(h, x, g, b, *unused):
    v = ALPHA * x + h
    xhat, _ = _ln_stats(v)
    y = xhat * g + b
    return y, y, v


def _ln_bwd_epilogue(scale):
    def epilogue(acc, resid, v, g, *unused):
        dy = acc + scale * resid
        xhat, rstd = _ln_stats(v)
        dxh = dy * g
        m1 = jnp.mean(dxh, axis=-1, keepdims=True)
        m2 = jnp.mean(dxh * xhat, axis=-1, keepdims=True)
        dv = rstd * (dxh - m1 - xhat * m2)
        return (dv, dv), (_colsum(dy * xhat), _colsum(dy))
    return epilogue


def _ple_gate_grads(dx3, z, pp):
    s = jax.nn.sigmoid(z)
    return dx3 * s, dx3 * pp * s * (1.0 - s)


def _loss_head(y, t, z, pp):
    def fn(r, p):
        d = r[0] - r[1]
        dy = d * (1.0 / D_MODEL)
        return [dy, *_ple_gate_grads(dy, r[2], r[3])], [_colsum(d * d) * (0.5 / D_MODEL)]
    return _rowwise(fn, name="loss_head", rows=[(a, D_MODEL, 0) for a in (y, t, z, pp)],
                    outs=[(D_MODEL, F32), (D_MODEL, BF16), (D_MODEL, BF16)], accs=[D_MODEL])


def _input_grad_epilogue(acc, dv, *below):
    dx = acc + ALPHA * dv
    return (dx, *_ple_gate_grads(dx, *below)) if below else (dx,)


N_LEVELS = 6
GLA_STEP = 4


def _gla_consts():
    C = CHUNK
    A = np.zeros((N_LEVELS + 3, C, C), np.float32)
    masks = np.zeros((N_LEVELS + 1, C, C), np.float32)
    r = np.arange(C)[:, None]
    u = np.arange(C)[None, :]
    for l in range(N_LEVELS):
        half = C >> (l + 1)
        mid = (r // (2 * half)) * (2 * half) + half - 1
        A[l] = np.where(r > mid, (u > mid) & (u <= r), (u > r) & (u <= mid))
        masks[l] = ((r // (2 * half)) == (u // (2 * half))) & (((r // half) % 2) != ((u // half) % 2))
    masks[N_LEVELS] = (r == u)
    A[N_LEVELS] = (u <= r)
    A[N_LEVELS + 1] = (u > r)
    A[N_LEVELS + 2] = 1.0
    A = A.reshape(-1, C)
    return A, np.ascontiguousarray(A.T), masks


def _split3(v):
    hi = v.astype(BF16)
    r1 = v - hi.astype(F32)
    mid = r1.astype(BF16)
    lo = (r1 - mid.astype(F32)).astype(BF16)
    return hi, mid, lo


def _dot_exact01(a01, v):
    hi, mid, lo = _split3(v)
    f = lambda p: jnp.dot(a01, p, preferred_element_type=F32)
    return f(hi) + f(mid) + f(lo)


def _nt(a, b):
    return lax.dot_general(a, b, (((1,), (1,)), ((), ())), preferred_element_type=F32)


def _tn(a, b):
    return lax.dot_general(a, b, (((0,), (0,)), ((), ())), preferred_element_type=F32)


def _nn(a, b):
    return jnp.dot(a, b, preferred_element_type=F32)


def _gla_chunk_terms(q, k, E, m_ref):
    C = CHUNK
    scores = m_ref[N_LEVELS] * _nt(q.astype(BF16), k.astype(BF16))
    qes, kes = [], []
    for l in range(N_LEVELS):
        El = E[l * C:(l + 1) * C]
        qe, ke = (q * El).astype(BF16), (k * El).astype(BF16)
        qes.append(qe)
        kes.append(ke)
        scores = scores + m_ref[l] * _nt(qe, ke)
    return qes, kes, scores


def _head(v, h, w):
    return v[:, h * w:(h + 1) * w]


def _gla_fwd(pin, la):
    S = pin.shape[0]
    NC = S // CHUNK
    C, R = CHUNK, CHUNK * GLA_STEP
    A, _, masks = _gla_consts()

    def body(q_ref, k_ref, v_ref, la_ref, a_ref, m_ref, o_ref, st_ref, state):
        @pl.when(pl.program_id(0) == 0)
        def _():
            state[...] = jnp.zeros(state.shape, F32)

        for ci in range(GLA_STEP):
            rows = pl.ds(ci * C, C)
            E_all = jnp.exp(_dot_exact01(a_ref[...], la_ref[rows, :]))
            q_all = q_ref[rows, :] * (GLA_DK ** -0.5)
            k_all, v_all = k_ref[rows, :], v_ref[rows, :]
            outs = []
            for h in range(GLA_HEADS):
                q, k, E = _head(q_all, h, GLA_DK), _head(k_all, h, GLA_DK), _head(E_all, h, GLA_DK)
                _, _, scores = _gla_chunk_terms(q, k, E, m_ref)
                Eq, Ek, Ee = E[6 * C:7 * C], E[7 * C:8 * C], E[8 * C:9 * C]
                st = state[h]
                st_ref[h, ci] = st
                vb = _head(v_all, h, GLA_DV).astype(BF16)
                outs.append(_nn(scores.astype(BF16), vb) + _nt((q * Eq).astype(BF16), st.astype(BF16)))
                state[h] = st * jnp.concatenate([Ee] * (GLA_DV // C), axis=0) + _tn(vb, (k * Ek).astype(BF16))
            o_ref[rows, :] = jnp.concatenate(outs, axis=1)

    return pl.pallas_call(
        body, name="gla_fwd", grid=(NC // GLA_STEP,),
        in_specs=[pl.BlockSpec((R, GLA_HK), lambda c: (c, 0)),
                  pl.BlockSpec((R, GLA_HK), lambda c: (c, 1)),
                  pl.BlockSpec((R, GLA_HV), lambda c: (c, 2 * GLA_HK // GLA_HV)),
                  pl.BlockSpec((R, GLA_HK), lambda c: (c, 0)),
                  pl.BlockSpec(A.shape, lambda c: (0, 0)),
                  pl.BlockSpec(masks.shape, lambda c: (0, 0, 0))],
        out_specs=[pl.BlockSpec((R, GLA_HV), lambda c: (c, 0)),
                   pl.BlockSpec((GLA_HEADS, GLA_STEP, GLA_DV, GLA_DK), lambda c: (0, c, 0, 0))],
        out_shape=[_out((S, GLA_HV), F32), _out((GLA_HEADS, NC, GLA_DV, GLA_DK), F32)],
        scratch_shapes=[pltpu.VMEM((GLA_HEADS, GLA_DV, GLA_DK), F32)],
        compiler_params=_cparams(("arbitrary",)),
    )(pin, pin, pin, la, jnp.asarray(A, BF16), jnp.asarray(masks))


def _gla_bwd(pin, la, states, do):
    S = pin.shape[0]
    NC = S // CHUNK
    C, R = CHUNK, CHUNK * GLA_STEP
    A, AT, masks = _gla_consts()
    scale = GLA_DK ** -0.5

    def body(q_ref, k_ref, v_ref, la_ref, st_ref, do_ref, a_ref, at_ref, m_ref,
             dq_ref, dk_ref, dv_ref, dla_ref, dstate):
        @pl.when(pl.program_id(0) == 0)
        def _():
            dstate[...] = jnp.zeros(dstate.shape, F32)

        for ci in reversed(range(GLA_STEP)):
            one_chunk(ci, pl.ds(ci * C, C), q_ref, k_ref, v_ref, la_ref, st_ref, do_ref, a_ref, at_ref, m_ref,
                      dq_ref, dk_ref, dv_ref, dla_ref, dstate)

    def one_chunk(ci, rows, q_ref, k_ref, v_ref, la_ref, st_ref, do_ref, a_ref, at_ref, m_ref,
                  dq_ref, dk_ref, dv_ref, dla_ref, dstate):
        E_all = jnp.exp(_dot_exact01(a_ref[...], la_ref[rows, :]))
        q_all = q_ref[rows, :] * scale
        k_all, v_all, do_all = k_ref[rows, :], v_ref[rows, :], do_ref[rows, :]
        dqs, dks, dvs, dXs = [], [], [], []
        for h in range(GLA_HEADS):
            q, k, E = _head(q_all, h, GLA_DK), _head(k_all, h, GLA_DK), _head(E_all, h, GLA_DK)
            qes, kes, scores = _gla_chunk_terms(q, k, E, m_ref)
            Eq, Ek, Ee = E[6 * C:7 * C], E[7 * C:8 * C], E[8 * C:9 * C]
            st, dst = st_ref[h, ci], dstate[h]
            dob, vb = _head(do_all, h, GLA_DV).astype(BF16), _head(v_all, h, GLA_DV).astype(BF16)
            dstb = dst.astype(BF16)
            qEq, kEk = (q * Eq).astype(BF16), (k * Ek).astype(BF16)
            dsc = _nt(dob, vb)
            dvs.append(_tn(scores.astype(BF16), dob) + _nt(kEk, dstb))
            dqEq = _nn(dob, st.astype(BF16))
            dkEk = _nn(vb, dstb)
            Gd = (m_ref[N_LEVELS] * dsc).astype(BF16)
            dq = _nn(Gd, k.astype(BF16)) + dqEq * Eq
            dk = _tn(Gd, q.astype(BF16)) + dkEk * Ek
            dX = []
            for l in range(N_LEVELS):
                El = E[l * C:(l + 1) * C]
                G = (m_ref[l] * dsc).astype(BF16)
                dqe, dke = _nn(G, kes[l]), _tn(G, qes[l])
                dq = dq + dqe * El
                dk = dk + dke * El
                dX.append((dqe * q + dke * k) * El)
            dX.append(dqEq * q * Eq)
            dX.append(dkEk * k * Ek)
            prod = dst * st
            dEe = prod[0:C]
            for i in range(1, GLA_DV // C):
                dEe = dEe + prod[i * C:(i + 1) * C]
            dX.append(dEe * Ee)
            dXs.append(jnp.concatenate(dX, axis=0))
            dqs.append(dq * scale)
            dks.append(dk)
            dstate[h] = dst * jnp.concatenate([Ee] * (GLA_DV // C), axis=0) + _tn(dob, qEq)
        dla_ref[rows, :] = _dot_exact01(at_ref[...], jnp.concatenate(dXs, axis=1))
        dq_ref[rows, :] = jnp.concatenate(dqs, axis=1).astype(dq_ref.dtype)
        dk_ref[rows, :] = jnp.concatenate(dks, axis=1).astype(dk_ref.dtype)
        dv_ref[rows, :] = jnp.concatenate(dvs, axis=1).astype(dv_ref.dtype)

    rc = lambda c: NC // GLA_STEP - 1 - c
    return pl.pallas_call(
        body, name="gla_bwd", grid=(NC // GLA_STEP,),
        in_specs=[pl.BlockSpec((R, GLA_HK), lambda c: (rc(c), 0)),
                  pl.BlockSpec((R, GLA_HK), lambda c: (rc(c), 1)),
                  pl.BlockSpec((R, GLA_HV), lambda c: (rc(c), 2 * GLA_HK // GLA_HV)),
                  pl.BlockSpec((R, GLA_HK), lambda c: (rc(c), 0)),
                  pl.BlockSpec((GLA_HEADS, GLA_STEP, GLA_DV, GLA_DK), lambda c: (0, rc(c), 0, 0)),
                  pl.BlockSpec((R, GLA_HV), lambda c: (rc(c), 0)),
                  pl.BlockSpec(A.shape, lambda c: (0, 0)),
                  pl.BlockSpec(AT.shape, lambda c: (0, 0)),
                  pl.BlockSpec(masks.shape, lambda c: (0, 0, 0))],
        out_specs=[pl.BlockSpec((R, GLA_HK), lambda c: (rc(c), 0)),
                   pl.BlockSpec((R, GLA_HK), lambda c: (rc(c), 0)),
                   pl.BlockSpec((R, GLA_HV), lambda c: (rc(c), 0)),
                   pl.BlockSpec((R, GLA_HK), lambda c: (rc(c), 0))],
        out_shape=[_out((S, GLA_HK), BF16), _out((S, GLA_HK), BF16), _out((S, GLA_HV), BF16),
                   _out((S, GLA_HK), F32)],
        scratch_shapes=[pltpu.VMEM((GLA_HEADS, GLA_DV, GLA_DK), F32)],
        compiler_params=_cparams(("arbitrary",)),
    )(pin, pin, pin, la, states, do, jnp.asarray(A, BF16), jnp.asarray(AT, BF16), jnp.asarray(masks))


def _gla_post_fwd(o, pin, g):
    def fn(r, p):
        ov, rv = r
        ys = []
        for h in range(GLA_HEADS):
            oh = ov[:, h * GLA_DV:(h + 1) * GLA_DV]
            rh = rv[:, h * GLA_DV:(h + 1) * GLA_DV]
            rs = lax.rsqrt(jnp.mean(oh * oh, axis=-1, keepdims=True) + RMS_EPS)
            ys.append(oh * rs * p[0] * (rh * jax.nn.sigmoid(rh)))
        return [jnp.concatenate(ys, axis=1)], []
    return _rowwise(fn, name="gla_post_fwd", rows=[(o, GLA_HV, 0), (pin, GLA_HV, (2 * GLA_HK + GLA_HV) // GLA_HV)],
                    pars=[g], outs=[(GLA_HV, BF16)])[0]


def _gla_post_bwd(dy, o, pin, g):
    def fn(r, p):
        dyv, ov, rv = r
        dos, drs, dg = [], [], 0.0
        for h in range(GLA_HEADS):
            sl = slice(h * GLA_DV, (h + 1) * GLA_DV)
            oh, rh, dyh = ov[:, sl], rv[:, sl], dyv[:, sl]
            rs = lax.rsqrt(jnp.mean(oh * oh, axis=-1, keepdims=True) + RMS_EPS)
            xh = oh * rs
            sg = jax.nn.sigmoid(rh)
            d_on = dyh * (rh * sg)
            drs.append(dyh * (xh * p[0]) * (sg * (1.0 + rh * (1.0 - sg))))
            dg = dg + _colsum(d_on * xh)
            dxh = d_on * p[0]
            dos.append(rs * (dxh - xh * jnp.mean(dxh * xh, axis=-1, keepdims=True)))
        return [jnp.concatenate(dos, axis=1), jnp.concatenate(drs, axis=1)], [dg]
    return _rowwise(fn, name="gla_post_bwd",
                    rows=[(dy, GLA_HV, 0), (o, GLA_HV, 0), (pin, GLA_HV, (2 * GLA_HK + GLA_HV) // GLA_HV)],
                    pars=[g], outs=[(GLA_HV, F32), (GLA_HV, BF16)], accs=[GLA_DV])


def _gla_gate_bwd(dla, la):
    def fn(r, p):
        dz = r[0] * (1.0 / GLA_TAU) * (1.0 - jnp.exp(GLA_TAU * r[1]))
        return [dz], [_colsum(dz)]
    return _rowwise(fn, name="gla_gate_bwd", rows=[(dla, GLA_HK, 0), (la, GLA_HK, 0)], outs=[(GLA_HK, BF16)],
                    accs=[GLA_HK])


def _log_sigmoid(z):
    return jnp.minimum(z, 0.0) - jnp.log(1.0 + jnp.exp(-jnp.abs(z)))


def _rot_half(v):
    lane = lax.broadcasted_iota(jnp.int32, v.shape, 1)
    return jnp.where(lane < 32, -pltpu.roll(v, 96, 1), jnp.where(lane < 64, pltpu.roll(v, 32, 1), 0.0))


def _rms(v):
    rs = lax.rsqrt(jnp.mean(v * v, axis=-1, keepdims=True) + RMS_EPS)
    return v * rs, rs


def _mla_norm_fwd(cin, gq, gkv, cosp, sinp):
    def fn(r, p):
        cv, cs, sn = r
        qn, _ = _rms(cv[:, :MLA_QR])
        kvn, _ = _rms(cv[:, MLA_QR:MLA_QR + MLA_KVR])
        kr = cv[:, MLA_QR + MLA_KVR:]
        return [qn * p[0], kvn * p[1], kr * cs + _rot_half(kr) * sn], []
    return _rowwise(fn, name="mla_norm_fwd", rows=[(cin, MLA_IN_PAD, 0), (cosp, 128, 0), (sinp, 128, 0)],
                    pars=[gq, gkv], outs=[(MLA_QR, BF16), (MLA_KVR, BF16), (128, BF16)])


def _mla_qrope_fwd(q, cosp, sinp):
    scale = (MLA_NOPE + MLA_ROPE) ** -0.5

    def fn(r, p):
        qv, cs, sn = r
        parts = []
        for h in range(MLA_HEADS):
            parts.append(qv[:, h * MLA_QH:h * MLA_QH + 128] * scale)
            rp = qv[:, h * MLA_QH + 128:(h + 1) * MLA_QH]
            parts.append((rp * cs + _rot_half(rp) * sn) * scale)
        return [jnp.concatenate(parts, axis=1)], []
    W = MLA_HEADS * MLA_QH
    return _rowwise(fn, name="mla_qrope_fwd", rows=[(q, W, 0), (cosp, 128, 0), (sinp, 128, 0)],
                    outs=[(W, BF16)])[0]


def _mla_qrope_bwd(dq, cosp, sinp):
    scale = (MLA_NOPE + MLA_ROPE) ** -0.5

    def fn(r, p):
        dv, cs, sn = r
        parts = []
        for h in range(MLA_HEADS):
            parts.append(dv[:, h * MLA_QH:h * MLA_QH + 128] * scale)
            rp = dv[:, h * MLA_QH + 128:(h + 1) * MLA_QH]
            parts.append((rp * cs - _rot_half(rp) * sn) * scale)
        return [jnp.concatenate(parts, axis=1)], []
    W = MLA_HEADS * MLA_QH
    return _rowwise(fn, name="mla_qrope_bwd", rows=[(dq, W, 0), (cosp, 128, 0), (sinp, 128, 0)],
                    outs=[(W, BF16)])[0]


def _mla_norm_bwd(cin, dqn, dkvn, dkr, gq, gkv, cosp, sinp):
    def fn(r, p):
        cv, dq_, dkv_, dkr_, cs, sn = r
        outs, accs = [], []
        for (lo, hi), dn, g in (((0, MLA_QR), dq_, p[0]), ((MLA_QR, MLA_QR + MLA_KVR), dkv_, p[1])):
            xh, rs = _rms(cv[:, lo:hi])
            dxh = dn * g
            outs.append(rs * (dxh - xh * jnp.mean(dxh * xh, axis=-1, keepdims=True)))
            accs.append(_colsum(dn * xh))
        dk = dkr_[:, 0:128]
        for h in range(1, MLA_HEADS):
            dk = dk + dkr_[:, h * 128:(h + 1) * 128]
        outs.append(dk * cs - _rot_half(dk) * sn)
        return [jnp.concatenate(outs, axis=1)], accs
    return _rowwise(fn, name="mla_norm_bwd",
                    rows=[(cin, MLA_IN_PAD, 0), (dqn, MLA_QR, 0), (dkvn, MLA_KVR, 0), (dkr, MLA_HEADS * 128, 0),
                          (cosp, 128, 0), (sinp, 128, 0)],
                    pars=[gq, gkv], outs=[(MLA_IN_PAD, BF16)], accs=[MLA_QR, MLA_KVR])


def _mla_probs(q, k, i, tq):
    s = _nt(q, k)
    row = (i * tq + lax.broadcasted_iota(jnp.int32, s.shape, 0)) // CHUNK
    col = lax.broadcasted_iota(jnp.int32, s.shape, 1) // CHUNK
    s = jnp.where(col <= row, s, -jnp.inf)
    e = jnp.exp(s - jnp.max(s, axis=-1, keepdims=True))
    return e / jnp.sum(e, axis=-1, keepdims=True)


def _mla_attn_fwd(qr, knv, kr, tq=256):
    S = qr.shape[0]
    tq = min(tq, S)

    def body(q_ref, kn_ref, v_ref, kr_ref, o_ref, k_cat):
        k_cat[:, :128] = kn_ref[...]
        k_cat[:, 128:] = kr_ref[...]
        for i in range(S // tq):
            rows, keys = pl.ds(i * tq, tq), pl.ds(0, (i + 1) * tq)
            pr = _mla_probs(q_ref[rows, :], k_cat[keys, :], i, tq)
            o_ref[rows, :] = _nn(pr.astype(BF16), v_ref[keys, :])

    return pl.pallas_call(
        body, name="mla_attn_fwd", grid=(MLA_HEADS,),
        in_specs=[pl.BlockSpec((S, MLA_QH), lambda h: (0, h)),
                  pl.BlockSpec((S, 128), lambda h: (0, h)),
                  pl.BlockSpec((S, 128), lambda h: (0, MLA_HEADS + h)),
                  pl.BlockSpec((S, 128), lambda h: (0, 0))],
        out_specs=pl.BlockSpec((S, 128), lambda h: (0, h)),
        out_shape=_out((S, MLA_HEADS * MLA_V), F32),
        scratch_shapes=[pltpu.VMEM((S, MLA_QH), BF16)],
        compiler_params=_cparams(("parallel",)),
    )(qr, knv, knv, kr)


def _mla_attn_bwd(qr, knv, kr, o, do, tq=256):
    S = qr.shape[0]
    tq = min(tq, S)
    W = MLA_HEADS * 128

    def body(q_ref, kn_ref, v_ref, kr_ref, o_ref, do_ref, dq_ref, dkn_ref, dv_ref, dkr_ref, k_cat, dk_acc, dv_acc):
        k_cat[:, :128] = kn_ref[...]
        k_cat[:, 128:] = kr_ref[...]
        dk_acc[...] = jnp.zeros(dk_acc.shape, F32)
        dv_acc[...] = jnp.zeros(dv_acc.shape, F32)
        for i in range(S // tq):
            rows, keys = pl.ds(i * tq, tq), pl.ds(0, (i + 1) * tq)
            q, k, v = q_ref[rows, :], k_cat[keys, :], v_ref[keys, :]
            pr = _mla_probs(q, k, i, tq)
            dov = do_ref[rows, :]
            delta = jnp.sum(dov * o_ref[rows, :], axis=-1, keepdims=True)
            dob = dov.astype(BF16)
            ds = (pr * (_nt(dob, v) - delta)).astype(BF16)
            dq_ref[rows, :] = _nn(ds, k)
            dk_acc[keys, :] += _tn(ds, q)
            dv_acc[keys, :] += _tn(pr.astype(BF16), dob)
        dkn_ref[...] = dk_acc[:, :128].astype(dkn_ref.dtype)
        dkr_ref[...] = dk_acc[:, 128:]
        dv_ref[...] = dv_acc[...].astype(dv_ref.dtype)

    head = lambda w: pl.BlockSpec((S, w), lambda h: (0, h))
    return pl.pallas_call(
        body, name="mla_attn_bwd", grid=(MLA_HEADS,),
        in_specs=[head(MLA_QH), head(128), pl.BlockSpec((S, 128), lambda h: (0, MLA_HEADS + h)),
                  pl.BlockSpec((S, 128), lambda h: (0, 0)), head(128), head(128)],
        out_specs=[head(MLA_QH), head(128), head(128), head(128)],
        out_shape=[_out((S, MLA_HEADS * MLA_QH), F32), _out((S, W), BF16), _out((S, W), BF16), _out((S, W), F32)],
        scratch_shapes=[pltpu.VMEM((S, MLA_QH), BF16), pltpu.VMEM((S, MLA_QH), F32), pltpu.VMEM((S, 128), F32)],
        compiler_params=_cparams(("parallel",)),
    )(qr, knv, knv, kr, o, do)


CONV_TILE = 256


def _shift_down(v, n):
    row = lax.broadcasted_iota(jnp.int32, v.shape, 0)
    return jnp.where(row >= n, pltpu.roll(v, n, 0), 0.0)


def _shift_up(v, n):
    S = v.shape[0]
    row = lax.broadcasted_iota(jnp.int32, v.shape, 0)
    return jnp.where(row < S - n, pltpu.roll(v, S - n, 0), 0.0)


def _conv_specs(S, n_extra_cols):
    nt = D_MODEL // CONV_TILE
    specs = [pl.BlockSpec((S, CONV_TILE), functools.partial(lambda j, o: (0, o + j), o=part * nt))
             for part in range(3)]
    specs.append(pl.BlockSpec((8, CONV_TILE), lambda j: (0, j)))
    specs += [pl.BlockSpec((S, CONV_TILE), lambda j: (0, j)) for _ in range(n_extra_cols)]
    return specs


def _conv_fwd(bcu, w8):
    S = bcu.shape[0]

    def body(b_ref, c_ref, u_ref, w_ref, y_ref):
        cu = c_ref[...] * u_ref[...]
        z = w_ref[2:3, :] * cu + w_ref[1:2, :] * _shift_down(cu, 1) + w_ref[0:1, :] * _shift_down(cu, 2)
        y_ref[...] = (b_ref[...] * z).astype(y_ref.dtype)

    return pl.pallas_call(
        body, name="conv_fwd", grid=(D_MODEL // CONV_TILE,), in_specs=_conv_specs(S, 0),
        out_specs=pl.BlockSpec((S, CONV_TILE), lambda j: (0, j)),
        out_shape=_out((S, D_MODEL), BF16),
        compiler_params=_cparams(("parallel",)),
    )(bcu, bcu, bcu, w8)


def _conv_bwd(bcu, w8, dy):
    S = bcu.shape[0]

    def body(b_ref, c_ref, u_ref, w_ref, dy_ref, db_ref, dc_ref, du_ref, dw_ref):
        b, c, u, dyv = b_ref[...], c_ref[...], u_ref[...], dy_ref[...]
        w0, w1, w2 = w_ref[0:1, :], w_ref[1:2, :], w_ref[2:3, :]
        cu = c * u
        cu1, cu2 = _shift_down(cu, 1), _shift_down(cu, 2)
        z = w2 * cu + w1 * cu1 + w0 * cu2
        dz = dyv * b
        db_ref[...] = (dyv * z).astype(db_ref.dtype)
        dcu = w2 * dz + w1 * _shift_up(dz, 1) + w0 * _shift_up(dz, 2)
        dc_ref[...] = (dcu * u).astype(dc_ref.dtype)
        du_ref[...] = (dcu * c).astype(du_ref.dtype)
        dw_ref[...] = jnp.zeros(dw_ref.shape, F32)
        dw_ref[0:1, :] = _colsum(dz * cu2)
        dw_ref[1:2, :] = _colsum(dz * cu1)
        dw_ref[2:3, :] = _colsum(dz * cu)

    col = pl.BlockSpec((S, CONV_TILE), lambda j: (0, j))
    return pl.pallas_call(
        body, name="conv_bwd", grid=(D_MODEL // CONV_TILE,), in_specs=_conv_specs(S, 1),
        out_specs=[col, col, col, pl.BlockSpec((8, CONV_TILE), lambda j: (0, j))],
        out_shape=[_out((S, D_MODEL), BF16)] * 3 + [_out((8, D_MODEL), F32)],
        compiler_params=_cparams(("parallel",)),
    )(bcu, bcu, bcu, w8, dy)


def _adamw_update(w, g, m, v):
    nm = ADAM_B1 * m + (1.0 - ADAM_B1) * g
    nv = ADAM_B2 * v + (1.0 - ADAM_B2) * jnp.square(g)
    m_hat = nm / (1.0 - ADAM_B1 ** ADAM_STEP)
    v_hat = nv / (1.0 - ADAM_B2 ** ADAM_STEP)
    return -ADAM_LR * (m_hat / (jnp.sqrt(v_hat) + ADAM_EPS) + ADAM_WD * w), nm, nv


def _adamw_shard_major(w, m, v, gs, name):
    view = lambda a: jnp.transpose(a, (2, 0, 1))
    g = jnp.stack([x.T for x in gs], axis=1)
    n, L, k = g.shape
    rows = n // 4
    assert n % 4 == 0

    def body(w_ref, m_ref, v_ref, g_ref, go_ref, d_ref, nm_ref, nv_ref):
        gv = g_ref[...]
        d_ref[...], nm_ref[...], nv_ref[...] = _adamw_update(w_ref[...], gv, m_ref[...], v_ref[...])
        go_ref[...] = gv

    spec = pl.BlockSpec((rows, L, k), lambda i: (i, 0, 0))
    outs = pl.pallas_call(
        body, name=name, grid=(4,), in_specs=[spec] * 4, out_specs=[spec] * 4,
        out_shape=[jax.ShapeDtypeStruct((n, L, k), F32)] * 4,
        compiler_params=_cparams(("parallel",)),
    )(view(w), view(m), view(v), g)
    return [jnp.transpose(o, (1, 2, 0)) for o in outs]


def _adamw_small(ws, gs, ms, vs):
    n = len(ws)

    def body(*refs):
        ins, outs = refs[:4 * n], refs[4 * n:]
        for t in range(n):
            w_ref, g_ref, m_ref, v_ref = (ins[k * n + t] for k in range(4))
            gv = g_ref[...]
            outs[4 * t][...] = gv
            outs[4 * t + 1][...], outs[4 * t + 2][...], outs[4 * t + 3][...] = _adamw_update(
                w_ref[...], gv, m_ref[...], v_ref[...])

    return pl.pallas_call(
        body, name="adamw_small",
        out_shape=[jax.ShapeDtypeStruct(a.shape, F32) for a in ws for _ in range(4)],
    )(*ws, *gs, *ms, *vs)


def _adamw(w, m, v, gs, name):
    L, R, Cn = w.shape
    assert len(gs) == L
    tr = R if R <= 256 else 256
    assert R % tr == 0

    def body(w_ref, m_ref, v_ref, *rest):
        g_refs, (go_ref, d_ref, nm_ref, nv_ref) = rest[:L], rest[L:]
        layer = pl.program_id(0)
        gv = g_refs[0][...]
        for k in range(1, L):
            gv = jnp.where(layer == k, g_refs[k][...], gv)
        d_ref[...], nm_ref[...], nv_ref[...] = _adamw_update(w_ref[...], gv, m_ref[...], v_ref[...])
        go_ref[...] = gv

    spec = pl.BlockSpec((None, tr, Cn), lambda l, i: (l, i, 0))
    g_specs = [pl.BlockSpec((tr, Cn), functools.partial(lambda l, i, k: (jnp.where(l == k, i, 0), 0), k=k))
               for k in range(L)]
    return pl.pallas_call(
        body, name=name, grid=(L, R // tr), in_specs=[spec] * 3 + g_specs, out_specs=[spec] * 4,
        out_shape=[jax.ShapeDtypeStruct((L, R, Cn), F32)] * 4,
        compiler_params=_cparams(("arbitrary", "arbitrary")),
    )(w, m, v, *gs)


HBM_SPEC = pl.BlockSpec(memory_space=pltpu.HBM)


def _place():
    return lax.axis_index("x"), lax.axis_index("y"), lax.axis_index("c")


def _other_chips(x, y):
    return [(1 - x, y), (x, 1 - y), (1 - x, 1 - y)]


SEM_SPEC = pl.BlockSpec(memory_space=pltpu.SEMAPHORE)
ANY_SPEC = pl.BlockSpec(memory_space=pl.ANY)
VMEM_SPEC = pl.BlockSpec(memory_space=pltpu.VMEM)
EFFECT = pltpu.SideEffectType.DATAFLOW_SIDE_EFFECTING
TOKEN = (8, 128)


def _ici_start(srcs, lands, after, copies, name, per_src=3):
    n, nl = len(srcs), len(lands)

    def body(*refs):
        src_refs, land_refs = refs[:n], refs[n:n + nl]
        send_sems, recv_sems, token = refs[n + nl + 1], refs[n + nl + 2], refs[-1]
        x, y, c = _place()
        for k, src, dst, to in copies(src_refs, land_refs, x, y, c):
            pltpu.make_async_remote_copy(src_ref=src, dst_ref=dst, send_sem=send_sems.at[k], recv_sem=recv_sems.at[k],
                                         device_id=to, device_id_type=MESH).start()
        token[...] = jnp.zeros(TOKEN, F32)

    n_copies = per_src * max(n, nl if n == 0 else 0)
    res = pl.pallas_call(
        body, name=name,
        out_shape=(pltpu.SemaphoreType.DMA((n_copies,)), pltpu.SemaphoreType.DMA((n_copies,)),
                   *[pltpu.HBM(s.shape, s.dtype) for s in srcs], *[pltpu.HBM(l.shape, l.dtype) for l in lands],
                   jax.ShapeDtypeStruct(TOKEN, F32)),
        in_specs=[HBM_SPEC] * (n + nl) + [ANY_SPEC],
        out_specs=(SEM_SPEC, SEM_SPEC, *[HBM_SPEC] * (n + nl), VMEM_SPEC),
        input_output_aliases={t: 2 + t for t in range(n + nl)},
        compiler_params=pltpu.CompilerParams(has_side_effects=EFFECT),
    )(*[_hbm(s) for s in srcs], *[_hbm(l) for l in lands], after)
    return res[0], res[1], list(res[2:2 + n]), list(res[2 + n:2 + n + nl]), res[-1]


def _ici_wait(handle, after, copies, name):
    send_sems, recv_sems, srcs, lands, _ = handle
    n, nl = len(srcs), len(lands)

    def body(*refs):
        src_refs, land_refs = refs[:n], refs[n:n + nl]
        send_s, recv_s = refs[n + nl], refs[n + nl + 1]
        x, y, c = _place()
        for k, src, dst, to in copies(src_refs, land_refs, x, y, c):
            cp = pltpu.make_async_remote_copy(src_ref=src, dst_ref=dst, send_sem=send_s.at[k], recv_sem=recv_s.at[k],
                                              device_id=to, device_id_type=MESH)
            cp.wait_send()
            cp.wait_recv()

    res = pl.pallas_call(
        body, name=name,
        out_shape=(*[pltpu.HBM(s.shape, s.dtype) for s in srcs], *[pltpu.HBM(l.shape, l.dtype) for l in lands]),
        in_specs=[HBM_SPEC] * (n + nl) + [SEM_SPEC, SEM_SPEC, ANY_SPEC],
        out_specs=tuple([HBM_SPEC] * (n + nl)),
        input_output_aliases={t: t for t in range(n + nl)},
        compiler_params=pltpu.CompilerParams(has_side_effects=EFFECT),
    )(*srcs, *lands, send_sems, recv_sems, after)
    return list(res[:n]), list(res[n:])


def _gather_copies(halves, arriving):
    def copies(src_refs, land_refs, x, y, c):
        q = 2 * x + y
        out = []
        for t, H in enumerate(halves):
            mine = land_refs[t].at[q, pl.ds(c * H, H), :]
            for j, (cx, cy) in enumerate(_other_chips(x, y)):
                theirs = land_refs[t].at[2 * cx + cy, pl.ds(c * H, H), :]
                out.append((3 * t + j, mine, theirs if arriving else mine, (cx, cy, c)))
        return out
    return copies


def _place_own(ops, after, name):
    n = len(ops)
    kinds = sorted({(o.shape, str(o.dtype)) for o in ops})
    kind_of = [kinds.index((o.shape, str(o.dtype))) for o in ops]

    def body(*refs):
        in_refs, out_refs = refs[:n], refs[n + 1:2 * n + 1]
        rd_sems, wr_sems, bufs = refs[2 * n + 1], refs[2 * n + 2], refs[2 * n + 3:]
        x, y, _ = _place()
        used = [0] * len(kinds)
        slot, busy = [], {}
        for t in range(n):
            slot.append((kind_of[t], used[kind_of[t]] % 2))
            used[kind_of[t]] += 1
        rd = lambda t: pltpu.make_async_copy(in_refs[t], bufs[slot[t][0]].at[slot[t][1]], rd_sems.at[t])
        wr = lambda t: pltpu.make_async_copy(bufs[slot[t][0]].at[slot[t][1]], out_refs[t].at[2 * x + y],
                                             wr_sems.at[t])
        rd(0).start()
        for t in range(n):
            rd(t).wait()
            wr(t).start()
            busy[slot[t]] = t
            if t + 1 < n:
                if slot[t + 1] in busy:
                    wr(busy.pop(slot[t + 1])).wait()
                rd(t + 1).start()
        for t in busy.values():
            wr(t).wait()

    return pl.pallas_call(
        body, name=name, in_specs=[HBM_SPEC] * n + [ANY_SPEC], out_specs=[HBM_SPEC] * n,
        out_shape=[jax.ShapeDtypeStruct((N_CHIPS,) + o.shape, o.dtype) for o in ops],
        scratch_shapes=[pltpu.SemaphoreType.DMA((n,)), pltpu.SemaphoreType.DMA((n,))]
        + [pltpu.VMEM((2,) + shape, jnp.dtype(dt)) for shape, dt in kinds],
        compiler_params=pltpu.CompilerParams(vmem_limit_bytes=VMEM_LIMIT),
    )(*ops, after)


def _gather_start(lands, after, name):
    return _ici_start([], lands, after, _gather_copies([l.shape[1] // 2 for l in lands], False), name)


def _gather_wait(handle, after, name):
    halves = [l.shape[1] // 2 for l in handle[3]]
    return _ici_wait(handle, after, _gather_copies(halves, True), name)


def _forward_copies(halves, arriving):
    def copies(src_refs, land_refs, x, y, c):
        out = []
        for t, H in enumerate(halves):
            for j, (cx, cy) in enumerate(_other_chips(x, y)):
                mine = land_refs[t].at[2 * cx + cy, pl.ds(c * H, H), :]
                theirs = land_refs[t].at[2 * cx + cy, pl.ds((1 - c) * H, H), :]
                out.append((3 * t + j, mine, theirs if arriving else mine, (x, y, 1 - c)))
        return out
    return copies


def _forward_start(lands, after, name):
    halves = [l.shape[1] // 2 for l in lands]
    return _ici_start([], lands, after, _forward_copies(halves, False), name)


def _forward_wait(handle, after, name):
    halves = [l.shape[1] // 2 for l in handle[3]]
    return _ici_wait(handle, after, _forward_copies(halves, True), name)[1]


def _swap_halves(ops, name):
    n = len(ops)

    def body(*refs):
        in_refs, out_refs, send_sems, recv_sems = refs[:n], refs[n:2 * n], refs[2 * n], refs[2 * n + 1]
        x, y, c = _place()
        cps = []
        for t in range(n):
            H = ops[t].shape[1] // 2
            cp = pltpu.make_async_remote_copy(src_ref=in_refs[t].at[:, pl.ds((1 - c) * H, H), :],
                                              dst_ref=out_refs[t], send_sem=send_sems.at[t],
                                              recv_sem=recv_sems.at[t], device_id=(x, y, 1 - c),
                                              device_id_type=MESH)
            cp.start()
            cps.append(cp)
        for cp in cps:
            cp.wait()

    return pl.pallas_call(
        body, name=name, in_specs=[HBM_SPEC] * n, out_specs=[HBM_SPEC] * n,
        out_shape=[jax.ShapeDtypeStruct((N_CHIPS, o.shape[1] // 2, o.shape[2]), o.dtype) for o in ops],
        scratch_shapes=[pltpu.SemaphoreType.DMA((n,)), pltpu.SemaphoreType.DMA((n,))],
    )(*ops)


def _sum_rows_tile(h):
    return h if h <= 512 else 512


def _pair_sum(g, t, cq, name):
    _, a, b = g.shape
    H = a // 2
    tr = _sum_rows_tile(H)

    def body(cq_ref, g_ref, t_ref, o_ref):
        o_ref[...] = (g_ref[...].astype(F32) + t_ref[...].astype(F32)).astype(o_ref.dtype)

    grid_spec = pltpu.PrefetchScalarGridSpec(
        num_scalar_prefetch=1, grid=(N_CHIPS, H // tr),
        in_specs=[pl.BlockSpec((None, None, tr, b), lambda j, i, cq_ref: (j, cq_ref[0], i, 0)),
                  pl.BlockSpec((None, tr, b), lambda j, i, cq_ref: (j, i, 0))],
        out_specs=pl.BlockSpec((None, tr, b), lambda j, i, cq_ref: (j, i, 0)))
    return pl.pallas_call(
        body, name=name, grid_spec=grid_spec, out_shape=_out(t.shape, BF16),
        compiler_params=_cparams(("parallel", "parallel")),
    )(cq, g.reshape(N_CHIPS, 2, H, b), t)


def _scatter_copies(src_refs, land_refs, x, y, c):
    out = []
    for j, (cx, cy) in enumerate(_other_chips(x, y)):
        for t in range(len(src_refs)):
            out.append((3 * t + j, src_refs[t].at[2 * cx + cy], land_refs[t].at[j], (cx, cy, c)))
    return out


def _scatter_start(ops, after, name):
    lands = [lax.empty((3,) + o.shape[1:], o.dtype) for o in ops]
    return _ici_start(ops, lands, after, _scatter_copies, name)


def _scatter_wait(handle, after, name):
    return _ici_wait(handle, after, _scatter_copies, name)


def _chip_sum(p, t, cq, name):
    _, H, b = p.shape
    tr = _sum_rows_tile(H)

    def body(cq_ref, p_ref, t_ref, o_ref):
        acc = p_ref[...].astype(F32)
        for j in range(3):
            acc = acc + t_ref[j].astype(F32)
        o_ref[...] = acc

    grid_spec = pltpu.PrefetchScalarGridSpec(
        num_scalar_prefetch=1, grid=(H // tr,),
        in_specs=[pl.BlockSpec((None, tr, b), lambda i, cq_ref: (cq_ref[1], i, 0)),
                  pl.BlockSpec((3, tr, b), lambda i, cq_ref: (0, i, 0))],
        out_specs=pl.BlockSpec((None, tr, b), lambda i, cq_ref: (cq_ref[0], i, 0)))
    out = pl.pallas_call(
        body, name=name, grid_spec=grid_spec, out_shape=_out((2, H, b), F32),
        compiler_params=_cparams(("parallel",)),
    )(cq, p, t)
    return out.reshape(2 * H, b)


def _join_copies(arriving):
    def copies(src_refs, land_refs, x, y, c):
        out = []
        for t, land in enumerate(land_refs):
            H = land.shape[0] // 2
            mine, theirs = land.at[pl.ds(c * H, H), :], land.at[pl.ds((1 - c) * H, H), :]
            out.append((t, mine, theirs if arriving else mine, (x, y, 1 - c)))
        return out
    return copies


def _join_start(fs, name):
    return _ici_start([], fs, jnp.zeros(TOKEN, F32), _join_copies(False), name, per_src=1)


def _join_wait(handle, after, name):
    return _ici_wait(handle, after, _join_copies(True), name)[1]


def _direct_copies(src_refs, land_refs, x, y, c):
    out = []
    for t in range(len(src_refs)):
        H = src_refs[t].shape[1] // 2
        for k in range(1, 8):
            px, py, pc = x ^ (k >> 2), y ^ ((k >> 1) & 1), c ^ (k & 1)
            out.append((7 * t + k - 1, src_refs[t].at[2 * px + py, pl.ds(pc * H, H), :], land_refs[t].at[k - 1],
                        (px, py, pc)))
    return out


def _direct_sum(g, t, cq, name):
    _, a, b = g.shape
    H = a // 2
    tr = _sum_rows_tile(H)

    def body(cq_ref, g_ref, t_ref, o_ref):
        acc = g_ref[...].astype(F32)
        for k in range(7):
            acc = acc + t_ref[k].astype(F32)
        o_ref[...] = acc

    grid_spec = pltpu.PrefetchScalarGridSpec(
        num_scalar_prefetch=1, grid=(H // tr,),
        in_specs=[pl.BlockSpec((None, None, tr, b), lambda i, cq_ref: (cq_ref[1], cq_ref[0], i, 0)),
                  pl.BlockSpec((7, tr, b), lambda i, cq_ref: (0, i, 0))],
        out_specs=pl.BlockSpec((None, tr, b), lambda i, cq_ref: (cq_ref[0], i, 0)))
    out = pl.pallas_call(
        body, name=name, grid_spec=grid_spec, out_shape=_out((2, H, b), F32),
        compiler_params=_cparams(("parallel",)),
    )(cq, g.reshape(N_CHIPS, 2, H, b), t)
    return out.reshape(a, b)


def _reduce_direct_start(gs, tag):
    lands = [lax.empty((7, g.shape[1] // 2, g.shape[2]), g.dtype) for g in gs]
    return _ici_start(gs, lands, jnp.zeros(TOKEN, F32), _direct_copies, "rs_direct_start_" + tag, per_src=7)


def _reduce_direct_finish(handle, cq, after, tag):
    gs, rs = _ici_wait(handle, after, _direct_copies, "rs_direct_wait_" + tag)
    fs = [_direct_sum(g, r, cq, "rs_direct_sum") for g, r in zip(gs, rs)]
    return _join_start(fs, "rs_join_start_" + tag)


def _reduce_scatter_start(gs, cq, after, tag):
    ts = _swap_halves(gs, "rs_swap_" + tag)
    ps = [_pair_sum(g, t, cq, "rs_pair_sum") for g, t in zip(gs, ts)]
    return _scatter_start(ps, after, "rs_scatter_start_" + tag)


def _reduce_scatter_finish(handle, cq, after, tag):
    ps, rs = _scatter_wait(handle, after, "rs_scatter_wait_" + tag)
    fs = [_chip_sum(p, r, cq, "rs_chip_sum") for p, r in zip(ps, rs)]
    return _join_start(fs, "rs_join_start_" + tag)


def _all_reduce_small(v):
    n = v.shape[0]

    def body(v_ref, out_ref, buf, send_sems, recv_sems):
        x, y, c = _place()
        me = 4 * x + 2 * y + c
        buf[me] = v_ref[...]
        cps = []
        for k in range(1, 8):
            peer = (x ^ (k >> 2), y ^ ((k >> 1) & 1), c ^ (k & 1))
            cp = pltpu.make_async_remote_copy(src_ref=v_ref, dst_ref=buf.at[me], send_sem=send_sems.at[k - 1],
                                              recv_sem=recv_sems.at[k - 1], device_id=peer, device_id_type=MESH)
            cp.start()
            cps.append(cp)
        for k in range(1, 8):
            px, py, pc = x ^ (k >> 2), y ^ ((k >> 1) & 1), c ^ (k & 1)
            land = buf.at[4 * px + 2 * py + pc]
            pltpu.make_async_remote_copy(src_ref=land, dst_ref=land, send_sem=send_sems.at[k - 1],
                                         recv_sem=recv_sems.at[k - 1], device_id=(px, py, pc),
                                         device_id_type=MESH).wait_recv()
        for cp in cps:
            cp.wait_send()
        acc = buf[0]
        for d in range(1, 8):
            acc = acc + buf[d]
        out_ref[...] = acc

    vm = pl.BlockSpec(memory_space=pltpu.VMEM)
    return pl.pallas_call(
        body, name="all_reduce_small", in_specs=[vm], out_specs=vm,
        out_shape=jax.ShapeDtypeStruct((n, 128), F32),
        scratch_shapes=[pltpu.VMEM((8, n, 128), F32), pltpu.SemaphoreType.DMA((7,)), pltpu.SemaphoreType.DMA((7,))],
    )(v)


SMALL_GATHER = (16, 1024)
SMALL_FULL = sum(_size(_full_shape(n)) for n in SMALL)
SMALL_FULL_ROWS = -(-(SMALL_FULL + 1) // 128 // 8) * 8


def _layer_shards(w, i, q):
    kind, j = MIXER[i % 3], i // 3
    out = {n: w[n][i].astype(BF16) for n in COMMON_BIG}
    if kind == 'gla':
        win = jnp.zeros((D_MODEL, GLA_WIN), F32)
        win = lax.dynamic_update_slice(win, w['gla_w_in'][j], (0, (GLA_SHARD - GLA_WIN_STEP) * q))
        out['gla_w_in'] = win.astype(BF16)
        out['gla_w_out'] = w['gla_w_out'][j].astype(BF16)
    elif kind == 'mla':
        out['mla_w_in'] = jnp.pad(w['mla_w_in'][j], ((0, 0), (0, MLA_IN_PAD - MLA_IN))).astype(BF16)
        for n in ('mla_w_uq', 'mla_w_ukv', 'mla_w_out'):
            out[n] = w[n][j].astype(BF16)
    else:
        out['conv_w_in'] = w['conv_w_in'][j].astype(BF16)
        out['conv_w_out'] = w['conv_w_out'][j].astype(BF16)
    return out


def _rows_joined(g):
    return g.reshape(g.shape[0] * g.shape[1], g.shape[2])


def _cols_joined(g):
    return jnp.moveaxis(g, 0, 1).reshape(g.shape[1], -1)


def _layer_weights(g, i):
    kind = MIXER[i % 3]
    W = {}
    if 'mlp_w1' in g:
        W = {'w1': g['mlp_w1'], 'w2': _rows_joined(g['mlp_w2']), 'gate': _rows_joined(g['ple_w_gate']),
             'proj': g['ple_w_proj']}
    if kind == 'gla' and 'gla_w_out' in g:
        W['w_out'] = _rows_joined(g['gla_w_out'])
    if kind == 'gla' and 'gla_w_in' in g:
        parts = []
        for qq in range(N_CHIPS):
            lo = g['gla_w_in'][qq][:, :128]
            if qq > 0:
                lo = lo + g['gla_w_in'][qq - 1][:, GLA_WIN_STEP:]
            parts += [lo, g['gla_w_in'][qq][:, 128:GLA_WIN_STEP]]
        parts.append(g['gla_w_in'][N_CHIPS - 1][:, GLA_WIN_STEP:])
        W['w_in'] = jnp.concatenate(parts, axis=1)
    elif kind == 'mla':
        W['w_in'] = _rows_joined(g['mla_w_in'])
        uq = _cols_joined(g['mla_w_uq']).reshape(MLA_QR, MLA_HEADS, MLA_NOPE + MLA_ROPE)
        W['w_uq'] = jnp.pad(uq, ((0, 0), (0, 0), (0, MLA_QH - MLA_NOPE - MLA_ROPE))).reshape(MLA_QR, -1)
        ukv = _cols_joined(g['mla_w_ukv']).reshape(MLA_KVR, MLA_HEADS, 2, 128)
        W['w_ukv'] = ukv.transpose(0, 2, 1, 3).reshape(MLA_KVR, -1)
        W['w_out'] = _rows_joined(g['mla_w_out'])
    elif kind == 'conv':
        W['w_in'] = g['conv_w_in']
        W['w_out'] = _rows_joined(g['conv_w_out'])
    return W


def _pack_small_shards(w):
    flat = jnp.concatenate([w[n].reshape(-1) for n in SMALL_SHARDED])
    return jnp.pad(flat, (0, _size(SMALL_GATHER) - flat.shape[0])).reshape(SMALL_GATHER)


def _unpack_small_gathered(g):
    flat, out, off = g.reshape(N_CHIPS, -1), {}, 0
    for n in SMALL_SHARDED:
        shape, ax = WSPEC[n]
        seg = flat[:, off:off + _size(shape)].reshape((N_CHIPS,) + shape)
        out[n] = jnp.moveaxis(seg, 0, ax).reshape(_full_shape(n))
        off += _size(shape)
    return out


def _pack_small(vals, loss):
    flat = jnp.concatenate([vals[n].reshape(-1) for n in SMALL] + [loss.reshape(1)])
    return jnp.pad(flat, (0, SMALL_FULL_ROWS * 128 - flat.shape[0])).reshape(SMALL_FULL_ROWS, 128)


def _unpack_small(packed, q):
    flat = packed.reshape(-1)
    out, off = {}, 0
    for n in SMALL:
        shape, ax = WSPEC[n]
        full = flat[off:off + _size(_full_shape(n))].reshape(_full_shape(n))
        off += _size(_full_shape(n))
        out[n] = full if ax is None else lax.dynamic_slice_in_dim(full, q * shape[ax], shape[ax], axis=ax)
    return out


def _row_shards(dw):
    return dw.reshape(N_CHIPS, dw.shape[0] // N_CHIPS, dw.shape[1])


def _col_shards(dw):
    return jnp.moveaxis(dw.reshape(dw.shape[0], N_CHIPS, -1), 1, 0)


def _row(v):
    return v.reshape(1, -1)


def _layer_fwd(i, xin, xin_b, p_i, W, sm, cosp, sinp, rest=None, mid=None):
    kind, j = MIXER[i % 3], i // 3
    sv = {'xin': xin, 'xin_b': xin_b}
    if kind == 'gla':
        w_up = jnp.pad(sm['gla_w_gate_up'][j].astype(BF16), ((0, 128 - GLA_RANK), (0, 0)))
        pin = _mm(xin_b, W['w_in'], name="gla_in", tn=640, tm=FULL_ROWS)
        la = _mm(pin, w_up, name="gla_gate", K=128, tk=128, a_off=(0, (GLA_IN_PAD - 128) // 128), tn=512,
                 extras=[(_row(sm['gla_b_gate'][j]), 'n')],
                 epilogue=lambda acc, b: (_log_sigmoid(acc + b) * (1.0 / GLA_TAU),))
        o, states = _gla_fwd(pin, la)
        yb = _gla_post_fwd(o, pin, _row(sm['gla_norm_g'][j]))
        if rest is not None:
            W = {**W, **rest(yb)}
        mixed = yb
        sv.update(w_up=w_up, pin=pin, la=la, o=o, states=states, yb=yb)
    elif kind == 'mla':
        gq, gkv = sm['mla_q_norm'][j:j + 1], sm['mla_kv_norm'][j:j + 1]
        cin = _mm(xin_b, W['w_in'], name="mla_in", tn=640, tm=FULL_ROWS)
        qn, kvn, kr = _mla_norm_fwd(cin, gq, gkv, cosp, sinp)
        qr = _mla_qrope_fwd(_mm(qn, W['w_uq'], name="mla_uq"), cosp, sinp)
        knv = _mm(kvn, W['w_ukv'], name="mla_ukv", out_dtypes=(BF16,))
        o = _mla_attn_fwd(qr, knv, kr)
        ob = o.astype(BF16)
        mixed = ob
        sv.update(gq=gq, gkv=gkv, cin=cin, qn=qn, kvn=kvn, kr=kr, qr=qr, knv=knv, o=o, ob=ob)
    else:
        w8 = jnp.pad(sm['conv_w'][j], ((0, 5), (0, 0)))
        bcu = _mm(xin_b, W['w_in'], name="conv_in", tn=768, b_sh=True, tm=FULL_ROWS)
        yb = _conv_fwd(bcu, w8)
        mixed = yb
        sv.update(w8=w8, bcu=bcu, yb=yb)
    g0, b0 = _row(sm['ln_g'][i, 0]), _row(sm['ln_b'][i, 0])
    g1, b1 = _row(sm['ln_g'][i, 1]), _row(sm['ln_b'][i, 1])
    ln = dict(tm=512, tn=D_MODEL, out_dtypes=(F32, BF16, F32), epilogue=_ln_fwd_epilogue)
    x1, x1b, v0 = _mm(mixed, W['w_out'], name="mix_out_ln", extras=[(xin, 'mn'), (g0, 'n'), (b0, 'n')], **ln)
    ab, dadu = _mm(x1b, W['w1'], name="mlp_up", out_dtypes=(BF16, BF16), b_sh=True, tm=FULL_ROWS,
                   epilogue=lambda acc: (jnp.square(jnp.maximum(acc, 0.0)), 2.0 * jnp.maximum(acc, 0.0)))
    x2, x2b, v1 = _mm(ab, W['w2'], name="mlp_down_ln", tk=D_FF, extras=[(x1, 'mn'), (g1, 'n'), (b1, 'n')], **ln)
    order = [(mid(x2b), 'whole')] if mid else []
    pp = _mm(p_i, W['proj'], name="ple_proj", tn=256, b_sh=True, extras=order,
             epilogue=lambda acc, *unused: (acc,))
    z, x3, x3b = _mm(x2b, W['gate'], name="ple_gate", out_dtypes=(F32, F32, BF16),
                     extras=[(x2, 'mn'), (pp, 'mn')],
                     epilogue=lambda acc, xv, pv: (acc,) + (xv + jax.nn.sigmoid(acc) * pv,) * 2)
    sv.update(v0=v0, x1b=x1b, ab=ab, dadu=dadu, v1=v1, x2b=x2b, pp=pp, z=z, g0=g0, g1=g1)
    return x3, x3b, sv, W


def _layer_bwd(i, grads_in, p_i, W, sm, sv, cosp, sinp, token, early=None, below=None):
    kind, j = MIXER[i % 3], i // 3
    big, small = {}, {}
    dx, dpp_b, dz_b = grads_in
    big['ple_w_proj'] = _mm(p_i, dpp_b, ta=True, name="ple_proj_dw", tn=256, out_sh=True, out_dtypes=(BF16,))
    big['ple_w_gate'] = _row_shards(_mm(sv['x2b'], dz_b, ta=True, name="dw_dd", out_dtypes=(BF16,)))
    ln = dict(tb=True, tm=512, tn=D_MODEL, out_dtypes=(F32, BF16), n_sums=2)
    (dv1, dv1b), (dg1, db1) = _mm(dz_b, W['gate'], name="ple_gate_dx_ln", epilogue=_ln_bwd_epilogue(1.0),
                                  extras=[(dx, 'mn'), (sv['v1'], 'mn'), (sv['g1'], 'n'), (token, 'whole')], **ln)
    big['mlp_w2'] = _row_shards(_mm(sv['ab'], dv1b, ta=True, name="mlp_down_dw", out_dtypes=(BF16,)))
    dub = _mm(dv1b, W['w2'], tb=True, name="mlp_down_dx", out_dtypes=(BF16,), tm=FULL_ROWS,
              extras=[(sv['dadu'], 'mn')], epilogue=lambda acc, d: (acc * d.astype(F32),))
    big['mlp_w1'] = _mm(sv['x1b'], dub, ta=True, name="mlp_up_dw", out_sh=True, out_dtypes=(BF16,))
    order = []
    if early is not None:
        order, big = [(early(big), 'whole')], {}
    (dv0, dv0b), (dg0, db0) = _mm(dub, W['w1'], name="mlp_up_dx_ln", b_sh=True, tk=D_FF, epilogue=_ln_bwd_epilogue(ALPHA),
                                  extras=[(dv1, 'mn'), (sv['v0'], 'mn'), (sv['g0'], 'n')] + order, **ln)
    small['ln_g'] = jnp.stack([dg0[0], dg1[0]])
    small['ln_b'] = jnp.stack([db0[0], db1[0]])
    resid = dict(tb=True, tn=D_MODEL, tm=512 if below else 1024, epilogue=_input_grad_epilogue,
                 extras=[(dv0, 'mn')] + [(a, 'mn') for a in below or ()],
                 out_dtypes=(F32, BF16, BF16) if below else (F32,))
    if kind == 'gla':
        big['gla_w_out'] = _row_shards(_mm(sv['yb'], dv0b, ta=True, name="dw_dd", out_dtypes=(BF16,)))
        dy = _mm(dv0b, W['w_out'], tb=True, name="dx_dd", tn=1024)
        do, dr_b, dng = _gla_post_bwd(dy, sv['o'], sv['pin'], _row(sm['gla_norm_g'][j]))
        dq_b, dk_b, dvv_b, dla = _gla_bwd(sv['pin'], sv['la'], sv['states'], do)
        dzg_b, dbg = _gla_gate_bwd(dla, sv['la'])
        dw_up = _mm(sv['pin'], dzg_b, ta=True, name="gla_gate_dw", M=128, tm=128,
                    a_off=(0, (GLA_IN_PAD - 128) // 128))
        dglr_b = _mm(dzg_b, sv['w_up'], tb=True, name="gla_gate_dx", out_dtypes=(BF16,))
        dpin_b = jnp.concatenate([dq_b, dk_b, dvv_b, dr_b, dglr_b], axis=1)
        dw_in = _mm(sv['xin_b'], dpin_b, ta=True, name="gla_in_dw", tn=640, out_dtypes=(BF16,))
        dxin = _mm(dpin_b, W['w_in'], name="gla_in_dx", tk=GLA_IN_PAD, **resid)
        big['gla_w_in'] = jnp.stack([dw_in[:, GLA_WIN_STEP * qq:GLA_WIN_STEP * qq + GLA_WIN]
                                     for qq in range(N_CHIPS)])
        small.update(gla_w_gate_up=dw_up[:GLA_RANK], gla_b_gate=dbg[0], gla_norm_g=dng[0])
    elif kind == 'mla':
        big['mla_w_out'] = _row_shards(_mm(sv['ob'], dv0b, ta=True, name="dw_dd", out_dtypes=(BF16,)))
        do = _mm(dv0b, W['w_out'], tb=True, name="dx_dd", tn=1024)
        dqr, dkn_b, dvv_b, dkr = _mla_attn_bwd(sv['qr'], sv['knv'], sv['kr'], sv['o'], do)
        dq_b = _mla_qrope_bwd(dqr, cosp, sinp)
        dw_uq = _mm(sv['qn'], dq_b, ta=True, name="mla_up_dw", out_dtypes=(BF16,))
        dqn = _mm(dq_b, W['w_uq'], tb=True, name="mla_up_dx")
        dknv_b = jnp.concatenate([dkn_b, dvv_b], axis=1)
        dw_ukv = _mm(sv['kvn'], dknv_b, ta=True, name="mla_up_dw", out_dtypes=(BF16,))
        dkvn = _mm(dknv_b, W['w_ukv'], tb=True, name="mla_up_dx")
        dcin_b, dgq, dgkv = _mla_norm_bwd(sv['cin'], dqn, dkvn, dkr, sv['gq'], sv['gkv'], cosp, sinp)
        big['mla_w_in'] = _row_shards(_mm(sv['xin_b'], dcin_b, ta=True, name="mla_in_dw", tn=640,
                                          out_dtypes=(BF16,)))
        dxin = _mm(dcin_b, W['w_in'], name="mla_in_dx", tk=MLA_IN_PAD, **resid)
        big['mla_w_uq'] = _col_shards(
            dw_uq.reshape(MLA_QR, MLA_HEADS, MLA_QH)[:, :, :MLA_NOPE + MLA_ROPE].reshape(MLA_QR, -1))
        big['mla_w_ukv'] = _col_shards(
            dw_ukv.reshape(MLA_KVR, 2, MLA_HEADS, 128).transpose(0, 2, 1, 3).reshape(MLA_KVR, -1))
        small.update(mla_q_norm=dgq[0], mla_kv_norm=dgkv[0])
    else:
        big['conv_w_out'] = _row_shards(_mm(sv['yb'], dv0b, ta=True, name="dw_dd", out_dtypes=(BF16,)))
        dy = _mm(dv0b, W['w_out'], tb=True, name="dx_dd", tn=1024)
        db_b, dc_b, du_b, dw8 = _conv_bwd(sv['bcu'], sv['w8'], dy)
        dbcu_b = jnp.concatenate([db_b, dc_b, du_b], axis=1)
        big['conv_w_in'] = _mm(sv['xin_b'], dbcu_b, ta=True, name="conv_in_dw", tn=768, out_sh=True,
                               out_dtypes=(BF16,))
        dxin = _mm(dbcu_b, W['w_in'], name="conv_in_dx", tk=3 * D_MODEL, b_sh=True, **resid)
        small['conv_w'] = dw8[:3]
    return (dxin if below else (dxin,)), big, small


def _rope_tables(positions):
    inv_freq = ROPE_BASE ** (-jnp.arange(0, MLA_ROPE // 2, dtype=F32) * (2.0 / MLA_ROPE))
    ang = positions.astype(F32)[:, None] * inv_freq
    zeros = jnp.zeros((positions.shape[0], 64), F32)
    return (jnp.concatenate([jnp.cos(ang), jnp.cos(ang), zeros], axis=1),
            jnp.concatenate([jnp.sin(ang), jnp.sin(ang), zeros], axis=1))


FIRST_NEEDED = ['gla_w_in']


def _start_gathers(w, q):
    token, started = jnp.zeros(TOKEN, F32), []
    for i in range(DEPTH):
        sh = _layer_shards(w, i, q)
        for k, names in enumerate([list(sh)] if i > 0 else [FIRST_NEEDED, [n for n in sh if n not in FIRST_NEEDED]]):
            ops = [sh[n] for n in names]
            if i == 0 and k == 0:
                ops.append(_pack_small_shards(w))
            tag = "l%d%s" % (i, "ab"[k] if i == 0 else "")
            handle = _gather_start(_place_own(ops, token, "ag_own_" + tag), token, "ag_start_" + tag)
            token = handle[4]
            started.append((handle, names, tag))
    return started, token


def _pass_on(entry, after):
    handle, names, tag = entry
    _, lands = _gather_wait(handle, after, "ag_wait_" + tag)
    passing = _forward_start(lands, jnp.zeros(TOKEN, F32), "ag_pass_start_" + tag)
    return (passing, names, tag), passing[4]


def _gathered(passed, after):
    passing, names, tag = passed
    got = _forward_wait(passing, after, "ag_pass_wait_" + tag)
    return dict(zip(names, got)), got[-1]


def _local_shard_grad(name, g, q):
    if name == 'gla_w_in':
        return lax.dynamic_slice_in_dim(g, (GLA_SHARD - GLA_WIN_STEP) * q, GLA_SHARD, axis=1)
    if name == 'mla_w_in':
        return g[:, :MLA_IN]
    return g


def kernel(x, p, positions, gla_w_in, gla_w_gate_up, gla_b_gate, gla_norm_g, gla_w_out, mla_w_in, mla_q_norm, mla_kv_norm, mla_w_uq, mla_w_ukv, mla_w_out, conv_w_in, conv_w, conv_w_out, ln_g, ln_b, mlp_w1, mlp_w2, ple_w_gate, ple_w_proj, loss_target, m_gla_w_in, m_gla_w_gate_up, m_gla_b_gate, m_gla_norm_g, m_gla_w_out, m_mla_w_in, m_mla_q_norm, m_mla_kv_norm, m_mla_w_uq, m_mla_w_ukv, m_mla_w_out, m_conv_w_in, m_conv_w, m_conv_w_out, m_ln_g, m_ln_b, m_mlp_w1, m_mlp_w2, m_ple_w_gate, m_ple_w_proj, v_gla_w_in, v_gla_w_gate_up, v_gla_b_gate, v_gla_norm_g, v_gla_w_out, v_mla_w_in, v_mla_q_norm, v_mla_kv_norm, v_mla_w_uq, v_mla_w_ukv, v_mla_w_out, v_conv_w_in, v_conv_w, v_conv_w_out, v_ln_g, v_ln_b, v_mlp_w1, v_mlp_w2, v_ple_w_gate, v_ple_w_proj):
    args = locals()
    w = {n: args[n] for n in WNAMES}
    m = {n: args['m_' + n] for n in WNAMES}
    v = {n: args['v_' + n] for n in WNAMES}
    q = 2 * lax.axis_index("x") + lax.axis_index("y")
    cq = jnp.stack([lax.axis_index("c"), q]).astype(jnp.int32)

    cosp, sinp = _rope_tables(positions[0])
    started, after = _start_gathers(w, q)
    xin, saved, layers, sm = x[0], [], [], None
    xin_b = xin.astype(BF16)
    passed, after = _pass_on(started[0], after)
    for i in range(DEPTH):
        got, last = _gathered(passed, after)
        rest = mid = None
        if i == 0:
            sm = _unpack_small_gathered(last)
            sm['mla_q_norm'], sm['mla_kv_norm'] = w['mla_q_norm'], w['mla_kv_norm']
            rest = lambda after: _layer_weights(_gathered(*_pass_on(started[1], after))[0], 0)
        coming = {}
        if i + 1 < DEPTH:
            def mid(after, entry=started[i + 2], coming=coming):
                coming['passed'], token = _pass_on(entry, after)
                return token
        xin, xin_b, sv, W = _layer_fwd(i, xin, xin_b, p[i, 0], _layer_weights(got, i), sm, cosp, sinp, rest, mid)
        layers.append(W)
        saved.append(sv)
        passed, after = coming.get('passed'), xin
    *grads_in, loss_cols = _loss_head(xin, loss_target[0], saved[-1]['z'], saved[-1]['pp'])
    loss = jnp.sum(loss_cols[0])

    gbig = {n: [None] * WSPEC[n][0][0] for n in BIG}
    gsmall = {n: [None] * _full_shape(n)[0] for n in SMALL}
    pending = []

    def start(grads, i, tag):
        names = list(grads)
        gs = [grads[n] for n in names]
        handle = _reduce_direct_start(gs, tag) if i > 0 else _reduce_scatter_start(gs, cq, jnp.zeros(TOKEN, F32), tag)
        pending.append((handle, names, i, tag))
        return handle[4]

    joining = []

    def finish(above, after, token):
        for entry in [e for e in pending if e[2] > above]:
            pending.remove(entry)
            handle, names, i, tag = entry
            handle = (_reduce_direct_finish if i > 0 else _reduce_scatter_finish)(handle, cq, after, tag)
            joining.append((handle, names, i, tag))
            token = token + handle[4]
        return token

    token = jnp.zeros(TOKEN, F32)
    for i in reversed(range(DEPTH)):
        early = (lambda grads: start(grads, 0, "l0a")) if i == 0 else None
        below = (saved[i - 1]['z'], saved[i - 1]['pp']) if i > 0 else None
        grads_in, big, small = _layer_bwd(i, grads_in, p[i, 0], layers[i], sm, saved[i], cosp, sinp, token, early,
                                          below)
        dx = grads_in[0]
        token = finish(i + 1, dx, start(big, i, "l%d%s" % (i, "b" if i == 0 else "")))
        for n, g in small.items():
            gsmall[n][i if n in ('ln_g', 'ln_b') else i // 3] = g
    grad, delta, new_m, new_v = {}, {}, {}, {}

    def take(entries):
        for handle, names, i, tag in entries:
            for n, g in zip(names, _join_wait(handle, entries[-1][0][4], "rs_join_wait_" + tag)):
                gbig[n][i if n in COMMON_BIG else i // 3] = _local_shard_grad(n, g, q)

    def update(n):
        adamw = _adamw_shard_major if n == 'gla_w_in' else _adamw
        grad[n], delta[n], new_m[n], new_v[n] = adamw(w[n], m[n], v[n], gbig[n], "adamw_" + n)

    token = finish(0, token, token)
    take(joining)
    del joining[:]
    ready = [n for n in BIG if n.startswith(('mla_', 'conv_'))]
    for n in ready:
        update(n)
    finish(-1, delta[ready[-1]], token)
    take(joining)
    for n in BIG:
        if n not in ready:
            update(n)
    small_sum = _all_reduce_small(_pack_small({n: jnp.stack(g) for n, g in gsmall.items()}, loss))
    gsm, loss = _unpack_small(small_sum, q), small_sum.reshape(-1)[SMALL_FULL]
    flat2 = lambda a: a.reshape(-1, a.shape[-1])
    res = _adamw_small(*[[flat2(d[n]) for n in SMALL] for d in (w, gsm, m, v)])
    for k, out in enumerate((grad, delta, new_m, new_v)):
        for n, r in zip(SMALL, res[k::4]):
            out[n] = r.reshape(WSPEC[n][0])
    return (loss, dx[None], *[grad[n] for n in WNAMES], *[delta[n] for n in WNAMES],
            *[new_m[n] for n in WNAMES], *[new_v[n] for n in WNAMES])
```

```python
import functools

import numpy as np
import jax
import jax.numpy as jnp
from jax import lax
from jax.experimental import pallas as pl
from jax.experimental.pallas import tpu as pltpu

F32 = jnp.float32
BF16 = jnp.bfloat16
MESH = pl.DeviceIdType.MESH

D_MODEL = 1024
DEPTH = 4
CHUNK = 64
ALPHA = (2 * DEPTH) ** 0.25
LN_EPS = 1e-5
RMS_EPS = 1e-6
D_FF = 4 * D_MODEL
GLA_HEADS = 4
GLA_DK = 128
GLA_DV = 256
GLA_RANK = 16
GLA_TAU = 16.0
GLA_HK = GLA_HEADS * GLA_DK
GLA_HV = GLA_HEADS * GLA_DV
GLA_IN = 2 * GLA_HK + GLA_HV + D_MODEL + GLA_RANK
GLA_IN_PAD = 2 * GLA_HK + GLA_HV + D_MODEL + 128
GLA_SHARD = GLA_IN // 4
GLA_WIN = 896
GLA_WIN_STEP = 768
MLA_HEADS = 8
MLA_NOPE = 128
MLA_ROPE = 64
MLA_V = 128
MLA_QR = 256
MLA_KVR = 256
MLA_IN = MLA_QR + MLA_KVR + MLA_ROPE
MLA_IN_PAD = MLA_QR + MLA_KVR + 128
MLA_QH = 256
ROPE_BASE = 10000.0
ADAM_LR = 0.001
ADAM_B1 = 0.9
ADAM_B2 = 0.999
ADAM_EPS = 1e-08
ADAM_WD = 0.01
ADAM_STEP = 10

VMEM_LIMIT = 48 * 1024 * 1024
FULL_ROWS = 2048
N_CHIPS = 4

WSPEC = {
    'gla_w_in': ((2, 1024, 772), 2), 'gla_w_gate_up': ((2, 16, 128), 2), 'gla_b_gate': ((2, 128), 1),
    'gla_norm_g': ((2, 64), 1), 'gla_w_out': ((2, 256, 1024), 1), 'mla_w_in': ((1, 256, 576), 1),
    'mla_q_norm': ((1, 256), None), 'mla_kv_norm': ((1, 256), None), 'mla_w_uq': ((1, 256, 384), 2),
    'mla_w_ukv': ((1, 256, 512), 2), 'mla_w_out': ((1, 256, 1024), 1), 'conv_w_in': ((1, 1024, 768), 2),
    'conv_w': ((1, 3, 256), 2), 'conv_w_out': ((1, 256, 1024), 1), 'ln_g': ((4, 2, 256), 2),
    'ln_b': ((4, 2, 256), 2), 'mlp_w1': ((4, 1024, 1024), 2), 'mlp_w2': ((4, 1024, 1024), 1),
    'ple_w_gate': ((4, 256, 1024), 1), 'ple_w_proj': ((4, 256, 256), 2),
}
WNAMES = list(WSPEC)
BIG = ['gla_w_in', 'gla_w_out', 'mla_w_in', 'mla_w_uq', 'mla_w_ukv', 'mla_w_out', 'conv_w_in', 'conv_w_out',
       'mlp_w1', 'mlp_w2', 'ple_w_gate', 'ple_w_proj']
SMALL_SHARDED = ['gla_w_gate_up', 'gla_b_gate', 'gla_norm_g', 'conv_w', 'ln_g', 'ln_b']
SMALL = SMALL_SHARDED + ['mla_q_norm', 'mla_kv_norm']
MIXER = ['gla', 'mla', 'conv']
COMMON_BIG = ['mlp_w1', 'mlp_w2', 'ple_w_gate', 'ple_w_proj']


def _size(shape):
    return int(np.prod(shape))


def _full_shape(name):
    shape, ax = WSPEC[name]
    if ax is None:
        return shape
    return tuple(s * N_CHIPS if i == ax else s for i, s in enumerate(shape))


def _cparams(sem=None):
    return pltpu.CompilerParams(dimension_semantics=sem, vmem_limit_bytes=VMEM_LIMIT)


def _out(shape, dtype):
    return pltpu.HBM(shape, dtype)


def _hbm(v):
    return pltpu.with_memory_space_constraint(v, pltpu.HBM)


def _mm(a, b, *, name, ta=False, tb=False, M=None, N=None, K=None, out_dtypes=(F32,), epilogue=None, extras=(),
        tm=1024, tn=512, tk=None, a_off=(0, 0), b_sh=False, out_sh=False, n_sums=0):
    if M is None:
        M = a.shape[1] if ta else a.shape[0]
    if K is None:
        K = a.shape[0] if ta else a.shape[1]
    if b_sh:
        kw, nq = b.shape[1], b.shape[2]
        n_b, k_b = (kw, N_CHIPS * nq) if tb else (N_CHIPS * nq, kw)
        N = n_b if N is None else N
        assert K == k_b
    elif N is None:
        N = b.shape[0] if tb else b.shape[1]
    if tk is None:
        tk = FULL_ROWS if ta else 1024
    tm, tn, tk = min(tm, M), min(tn, N), min(tk, K)
    assert M % tm == 0 and N % tn == 0 and K % tk == 0, (name, M, N, K, tm, tn, tk)
    nk = K // tk
    n_ex, n_out = len(extras), len(out_dtypes)
    assert n_sums == 0 or tn == N

    n_b = N_CHIPS if (b_sh and tb and tk == K) else 1

    def body(a_ref, *rest):
        b_refs, rest = rest[:n_b], rest[n_b:]
        ex_refs, out_refs = rest[:n_ex], rest[n_ex:n_ex + n_out]
        sum_refs = rest[n_ex + n_out:n_ex + n_out + n_sums]
        first_rows = pl.program_id(0) == 0
        dims = ((((0,) if ta else (1,)), ((1,) if tb else (0,))), ((), ()))
        if n_b == 1:
            part = lax.dot_general(a_ref[...].astype(BF16), b_refs[0][...].astype(BF16), dims,
                                   preferred_element_type=F32)
        else:
            part = sum(lax.dot_general(a_ref[:, s * nq:(s + 1) * nq].astype(BF16), b_refs[s][...].astype(BF16), dims,
                                       preferred_element_type=F32) for s in range(n_b))

        def finish(acc):
            res = (acc,) if epilogue is None else epilogue(acc, *[r[...] for r in ex_refs])
            if n_sums:
                res, sums = res

                @pl.when(first_rows)
                def _():
                    for r in sum_refs:
                        r[...] = jnp.zeros(r.shape, F32)

                for r, v in zip(sum_refs, sums):
                    r[...] += jnp.broadcast_to(v, r.shape)
            for r, v in zip(out_refs, res):
                r[...] = v.astype(r.dtype)

        if nk == 1:
            finish(part)
        else:
            acc_ref = rest[-1]
            k = pl.program_id(2)

            @pl.when(k == 0)
            def _():
                acc_ref[...] = part

            @pl.when(k > 0)
            def _():
                acc_ref[...] += part

            @pl.when(k == nk - 1)
            def _():
                finish(acc_ref[...])

    if ta:
        a_spec = pl.BlockSpec((tk, tm), lambda i, j, k: (k + a_off[0], i + a_off[1]))
    else:
        a_spec = pl.BlockSpec((tm, tk), lambda i, j, k: (i + a_off[0], k + a_off[1]))
    once = dict(pipeline_mode=pl.Buffered(1)) if (tn == N and nk == 1) else {}
    if n_b > 1:
        b_specs = [pl.BlockSpec((None, tn, nq), functools.partial(lambda i, j, k, s: (s, j, 0), s=s), **once)
                   for s in range(n_b)]
    elif b_sh and tb:
        assert nq % tk == 0
        per = nq // tk
        b_spec = pl.BlockSpec((None, tn, tk), lambda i, j, k: (k // per, j, k % per), **once)
    elif b_sh:
        assert nq % tn == 0
        per = nq // tn
        b_spec = pl.BlockSpec((None, tk, tn), lambda i, j, k: (j // per, k, j % per), **once)
    elif tb:
        b_spec = pl.BlockSpec((tn, tk), lambda i, j, k: (j, k), **once)
    else:
        b_spec = pl.BlockSpec((tk, tn), lambda i, j, k: (k, j), **once)
    if n_b == 1:
        b_specs = [b_spec]
    ex_specs = []
    for arr, kind in extras:
        if kind == 'mn':
            ex_specs.append(pl.BlockSpec((tm, tn), lambda i, j, k: (i, j)))
        elif kind == 'n':
            ex_specs.append(pl.BlockSpec((1, tn), lambda i, j, k: (0, j)))
        else:
            ex_specs.append(pl.BlockSpec(arr.shape, lambda i, j, k: (0, 0)))
    if out_sh:
        assert (N // N_CHIPS) % tn == 0
        per_o = N // N_CHIPS // tn
        o_spec = pl.BlockSpec((None, tm, tn), lambda i, j, k: (j // per_o, i, j % per_o))
        o_shape = (N_CHIPS, M, N // N_CHIPS)
    else:
        o_spec = pl.BlockSpec((tm, tn), lambda i, j, k: (i, j))
        o_shape = (M, N)
    outs = pl.pallas_call(
        body, name=name, grid=(M // tm, N // tn, nk),
        in_specs=[a_spec] + b_specs + ex_specs,
        out_specs=[o_spec for _ in out_dtypes] + [pl.BlockSpec((8, N), lambda i, j, k: (0, 0))] * n_sums,
        out_shape=[_out(o_shape, d) for d in out_dtypes] + [_out((8, N), F32)] * n_sums,
        scratch_shapes=[pltpu.VMEM((tm, tn), F32)] if nk > 1 else [],
        compiler_params=_cparams(("arbitrary" if n_sums else "parallel", "parallel", "arbitrary")),
    )(a, *[b] * n_b, *[e[0] for e in extras])
    if n_sums:
        return tuple(outs[:n_out]), tuple(outs[n_out:])
    return outs[0] if n_out == 1 else tuple(outs)


def _rowwise(fn, *, name, rows, pars=(), outs=(), accs=(), tm=256):
    S = rows[0][0].shape[0]
    tm = min(tm, S)
    assert S % tm == 0
    n_r, n_p, n_o, n_a = len(rows), len(pars), len(outs), len(accs)

    def body(*refs):
        r_refs, p_refs = refs[:n_r], refs[n_r:n_r + n_p]
        o_refs, a_refs = refs[n_r + n_p:n_r + n_p + n_o], refs[n_r + n_p + n_o:]
        o_vals, a_vals = fn([r[...] for r in r_refs], [p[...] for p in p_refs])
        for r, v in zip(o_refs, o_vals):
            r[...] = v.astype(r.dtype)
        if n_a:
            i = pl.program_id(0)

            @pl.when(i == 0)
            def _():
                for r in a_refs:
                    r[...] = jnp.zeros(r.shape, r.dtype)

            for r, v in zip(a_refs, a_vals):
                r[...] += jnp.broadcast_to(v, r.shape)

    in_specs = [pl.BlockSpec((tm, w), functools.partial(lambda i, o: (i, o), o=off)) for _, w, off in rows]
    in_specs += [pl.BlockSpec(p.shape, functools.partial(lambda i, nd: (0,) * nd, nd=p.ndim)) for p in pars]
    out_specs = [pl.BlockSpec((tm, w), lambda i: (i, 0)) for w, _ in outs]
    out_specs += [pl.BlockSpec((8, w), lambda i: (0, 0)) for w in accs]
    out_shape = [_out((S, w), d) for w, d in outs]
    out_shape += [_out((8, w), F32) for w in accs]
    res = pl.pallas_call(
        body, name=name, grid=(S // tm,), in_specs=in_specs, out_specs=out_specs, out_shape=out_shape,
        compiler_params=_cparams(("arbitrary",)),
    )(*[r[0] for r in rows], *pars)
    return tuple(res)


def _colsum(v):
    return jnp.sum(v, axis=0, keepdims=True)


def _ln_stats(v):
    mu = jnp.mean(v, axis=-1, keepdims=True)
    d = v - mu
    var = jnp.mean(d * d, axis=-1, keepdims=True)
    rstd = lax.rsqrt(var + LN_EPS)
    return d * rstd, rstd


def _ln_fwd_epilogue(h, x, g, b, *unused):
    v = ALPHA * x + h
    xhat, _ = _ln_stats(v)
    y = xhat * g + b
    return y, y, v


def _ln_bwd_epilogue(scale):
    def epilogue(acc, resid, v, g, *unused):
        dy = acc + scale * resid
        xhat, rstd = _ln_stats(v)
        dxh = dy * g
        m1 = jnp.mean(dxh, axis=-1, keepdims=True)
        m2 = jnp.mean(dxh * xhat, axis=-1, keepdims=True)
        dv = rstd * (dxh - m1 - xhat * m2)
        return (dv, dv), (_colsum(dy * xhat), _colsum(dy))
    return epilogue


def _ple_gate_grads(dx3, z, pp):
    s = jax.nn.sigmoid(z)
    return dx3 * s, dx3 * pp * s * (1.0 - s)


def _loss_head(y, t, z, pp):
    def fn(r, p):
        d = r[0] - r[1]
        dy = d * (1.0 / D_MODEL)
        return [dy, *_ple_gate_grads(dy, r[2], r[3])], [_colsum(d * d) * (0.5 / D_MODEL)]
    return _rowwise(fn, name="loss_head", rows=[(a, D_MODEL, 0) for a in (y, t, z, pp)],
                    outs=[(D_MODEL, F32), (D_MODEL, BF16), (D_MODEL, BF16)], accs=[D_MODEL])


def _input_grad_epilogue(acc, dv, *below):
    dx = acc + ALPHA * dv
    return (dx, *_ple_gate_grads(dx, *below)) if below else (dx,)


N_LEVELS = 6
GLA_STEP = 4


def _gla_consts():
    C = CHUNK
    A = np.zeros((N_LEVELS + 3, C, C), np.float32)
    masks = np.zeros((N_LEVELS + 1, C, C), np.float32)
    r = np.arange(C)[:, None]
    u = np.arange(C)[None, :]
    for l in range(N_LEVELS):
        half = C >> (l + 1)
        mid = (r // (2 * half)) * (2 * half) + half - 1
        A[l] = np.where(r > mid, (u > mid) & (u <= r), (u > r) & (u <= mid))
        masks[l] = ((r // (2 * half)) == (u // (2 * half))) & (((r // half) % 2) != ((u // half) % 2))
    masks[N_LEVELS] = (r == u)
    A[N_LEVELS] = (u <= r)
    A[N_LEVELS + 1] = (u > r)
    A[N_LEVELS + 2] = 1.0
    A = A.reshape(-1, C)
    return A, np.ascontiguousarray(A.T), masks


def _split3(v):
    hi = v.astype(BF16)
    r1 = v - hi.astype(F32)
    mid = r1.astype(BF16)
    lo = (r1 - mid.astype(F32)).astype(BF16)
    return hi, mid, lo


def _dot_exact01(a01, v):
    hi, mid, lo = _split3(v)
    f = lambda p: jnp.dot(a01, p, preferred_element_type=F32)
    return f(hi) + f(mid) + f(lo)


def _nt(a, b):
    return lax.dot_general(a, b, (((1,), (1,)), ((), ())), preferred_element_type=F32)


def _tn(a, b):
    return lax.dot_general(a, b, (((0,), (0,)), ((), ())), preferred_element_type=F32)


def _nn(a, b):
    return jnp.dot(a, b, preferred_element_type=F32)


def _gla_chunk_terms(q, k, E, m_ref):
    C = CHUNK
    scores = m_ref[N_LEVELS] * _nt(q.astype(BF16), k.astype(BF16))
    qes, kes = [], []
    for l in range(N_LEVELS):
        El = E[l * C:(l + 1) * C]
        qe, ke = (q * El).astype(BF16), (k * El).astype(BF16)
        qes.append(qe)
        kes.append(ke)
        scores = scores + m_ref[l] * _nt(qe, ke)
    return qes, kes, scores


def _head(v, h, w):
    return v[:, h * w:(h + 1) * w]


def _gla_fwd(pin, la):
    S = pin.shape[0]
    NC = S // CHUNK
    C, R = CHUNK, CHUNK * GLA_STEP
    A, _, masks = _gla_consts()

    def body(q_ref, k_ref, v_ref, la_ref, a_ref, m_ref, o_ref, st_ref, state):
        @pl.when(pl.program_id(0) == 0)
        def _():
            state[...] = jnp.zeros(state.shape, F32)

        for ci in range(GLA_STEP):
            rows = pl.ds(ci * C, C)
            E_all = jnp.exp(_dot_exact01(a_ref[...], la_ref[rows, :]))
            q_all = q_ref[rows, :] * (GLA_DK ** -0.5)
            k_all, v_all = k_ref[rows, :], v_ref[rows, :]
            outs = []
            for h in range(GLA_HEADS):
                q, k, E = _head(q_all, h, GLA_DK), _head(k_all, h, GLA_DK), _head(E_all, h, GLA_DK)
                _, _, scores = _gla_chunk_terms(q, k, E, m_ref)
                Eq, Ek, Ee = E[6 * C:7 * C], E[7 * C:8 * C], E[8 * C:9 * C]
                st = state[h]
                st_ref[h, ci] = st
                vb = _head(v_all, h, GLA_DV).astype(BF16)
                outs.append(_nn(scores.astype(BF16), vb) + _nt((q * Eq).astype(BF16), st.astype(BF16)))
                state[h] = st * jnp.concatenate([Ee] * (GLA_DV // C), axis=0) + _tn(vb, (k * Ek).astype(BF16))
            o_ref[rows, :] = jnp.concatenate(outs, axis=1)

    return pl.pallas_call(
        body, name="gla_fwd", grid=(NC // GLA_STEP,),
        in_specs=[pl.BlockSpec((R, GLA_HK), lambda c: (c, 0)),
                  pl.BlockSpec((R, GLA_HK), lambda c: (c, 1)),
                  pl.BlockSpec((R, GLA_HV), lambda c: (c, 2 * GLA_HK // GLA_HV)),
                  pl.BlockSpec((R, GLA_HK), lambda c: (c, 0)),
                  pl.BlockSpec(A.shape, lambda c: (0, 0)),
                  pl.BlockSpec(masks.shape, lambda c: (0, 0, 0))],
        out_specs=[pl.BlockSpec((R, GLA_HV), lambda c: (c, 0)),
                   pl.BlockSpec((GLA_HEADS, GLA_STEP, GLA_DV, GLA_DK), lambda c: (0, c, 0, 0))],
        out_shape=[_out((S, GLA_HV), F32), _out((GLA_HEADS, NC, GLA_DV, GLA_DK), F32)],
        scratch_shapes=[pltpu.VMEM((GLA_HEADS, GLA_DV, GLA_DK), F32)],
        compiler_params=_cparams(("arbitrary",)),
    )(pin, pin, pin, la, jnp.asarray(A, BF16), jnp.asarray(masks))


def _gla_bwd(pin, la, states, do):
    S = pin.shape[0]
    NC = S // CHUNK
    C, R = CHUNK, CHUNK * GLA_STEP
    A, AT, masks = _gla_consts()
    scale = GLA_DK ** -0.5

    def body(q_ref, k_ref, v_ref, la_ref, st_ref, do_ref, a_ref, at_ref, m_ref,
             dq_ref, dk_ref, dv_ref, dla_ref, dstate):
        @pl.when(pl.program_id(0) == 0)
        def _():
            dstate[...] = jnp.zeros(dstate.shape, F32)

        for ci in reversed(range(GLA_STEP)):
            one_chunk(ci, pl.ds(ci * C, C), q_ref, k_ref, v_ref, la_ref, st_ref, do_ref, a_ref, at_ref, m_ref,
                      dq_ref, dk_ref, dv_ref, dla_ref, dstate)

    def one_chunk(ci, rows, q_ref, k_ref, v_ref, la_ref, st_ref, do_ref, a_ref, at_ref, m_ref,
                  dq_ref, dk_ref, dv_ref, dla_ref, dstate):
        E_all = jnp.exp(_dot_exact01(a_ref[...], la_ref[rows, :]))
        q_all = q_ref[rows, :] * scale
        k_all, v_all, do_all = k_ref[rows, :], v_ref[rows, :], do_ref[rows, :]
        dqs, dks, dvs, dXs = [], [], [], []
        for h in range(GLA_HEADS):
            q, k, E = _head(q_all, h, GLA_DK), _head(k_all, h, GLA_DK), _head(E_all, h, GLA_DK)
            qes, kes, scores = _gla_chunk_terms(q, k, E, m_ref)
            Eq, Ek, Ee = E[6 * C:7 * C], E[7 * C:8 * C], E[8 * C:9 * C]
            st, dst = st_ref[h, ci], dstate[h]
            dob, vb = _head(do_all, h, GLA_DV).astype(BF16), _head(v_all, h, GLA_DV).astype(BF16)
            dstb = dst.astype(BF16)
            qEq, kEk = (q * Eq).astype(BF16), (k * Ek).astype(BF16)
            dsc = _nt(dob, vb)
            dvs.append(_tn(scores.astype(BF16), dob) + _nt(kEk, dstb))
            dqEq = _nn(dob, st.astype(BF16))
            dkEk = _nn(vb, dstb)
            Gd = (m_ref[N_LEVELS] * dsc).astype(BF16)
            dq = _nn(Gd, k.astype(BF16)) + dqEq * Eq
            dk = _tn(Gd, q.astype(BF16)) + dkEk * Ek
            dX = []
            for l in range(N_LEVELS):
                El = E[l * C:(l + 1) * C]
                G = (m_ref[l] * dsc).astype(BF16)
                dqe, dke = _nn(G, kes[l]), _tn(G, qes[l])
                dq = dq + dqe * El
                dk = dk + dke * El
                dX.append((dqe * q + dke * k) * El)
            dX.append(dqEq * q * Eq)
            dX.append(dkEk * k * Ek)
            prod = dst * st
            dEe = prod[0:C]
            for i in range(1, GLA_DV // C):
                dEe = dEe + prod[i * C:(i + 1) * C]
            dX.append(dEe * Ee)
            dXs.append(jnp.concatenate(dX, axis=0))
            dqs.append(dq * scale)
            dks.append(dk)
            dstate[h] = dst * jnp.concatenate([Ee] * (GLA_DV // C), axis=0) + _tn(dob, qEq)
        dla_ref[rows, :] = _dot_exact01(at_ref[...], jnp.concatenate(dXs, axis=1))
        dq_ref[rows, :] = jnp.concatenate(dqs, axis=1).astype(dq_ref.dtype)
        dk_ref[rows, :] = jnp.concatenate(dks, axis=1).astype(dk_ref.dtype)
        dv_ref[rows, :] = jnp.concatenate(dvs, axis=1).astype(dv_ref.dtype)

    rc = lambda c: NC // GLA_STEP - 1 - c
    return pl.pallas_call(
        body, name="gla_bwd", grid=(NC // GLA_STEP,),
        in_specs=[pl.BlockSpec((R, GLA_HK), lambda c: (rc(c), 0)),
                  pl.BlockSpec((R, GLA_HK), lambda c: (rc(c), 1)),
                  pl.BlockSpec((R, GLA_HV), lambda c: (rc(c), 2 * GLA_HK // GLA_HV)),
                  pl.BlockSpec((R, GLA_HK), lambda c: (rc(c), 0)),
                  pl.BlockSpec((GLA_HEADS, GLA_STEP, GLA_DV, GLA_DK), lambda c: (0, rc(c), 0, 0)),
                  pl.BlockSpec((R, GLA_HV), lambda c: (rc(c), 0)),
                  pl.BlockSpec(A.shape, lambda c: (0, 0)),
                  pl.BlockSpec(AT.shape, lambda c: (0, 0)),
                  pl.BlockSpec(masks.shape, lambda c: (0, 0, 0))],
        out_specs=[pl.BlockSpec((R, GLA_HK), lambda c: (rc(c), 0)),
                   pl.BlockSpec((R, GLA_HK), lambda c: (rc(c), 0)),
                   pl.BlockSpec((R, GLA_HV), lambda c: (rc(c), 0)),
                   pl.BlockSpec((R, GLA_HK), lambda c: (rc(c), 0))],
        out_shape=[_out((S, GLA_HK), BF16), _out((S, GLA_HK), BF16), _out((S, GLA_HV), BF16),
                   _out((S, GLA_HK), F32)],
        scratch_shapes=[pltpu.VMEM((GLA_HEADS, GLA_DV, GLA_DK), F32)],
        compiler_params=_cparams(("arbitrary",)),
    )(pin, pin, pin, la, states, do, jnp.asarray(A, BF16), jnp.asarray(AT, BF16), jnp.asarray(masks))


def _gla_post_fwd(o, pin, g):
    def fn(r, p):
        ov, rv = r
        ys = []
        for h in range(GLA_HEADS):
            oh = ov[:, h * GLA_DV:(h + 1) * GLA_DV]
            rh = rv[:, h * GLA_DV:(h + 1) * GLA_DV]
            rs = lax.rsqrt(jnp.mean(oh * oh, axis=-1, keepdims=True) + RMS_EPS)
            ys.append(oh * rs * p[0] * (rh * jax.nn.sigmoid(rh)))
        return [jnp.concatenate(ys, axis=1)], []
    return _rowwise(fn, name="gla_post_fwd", rows=[(o, GLA_HV, 0), (pin, GLA_HV, (2 * GLA_HK + GLA_HV) // GLA_HV)],
                    pars=[g], outs=[(GLA_HV, BF16)])[0]


def _gla_post_bwd(dy, o, pin, g):
    def fn(r, p):
        dyv, ov, rv = r
        dos, drs, dg = [], [], 0.0
        for h in range(GLA_HEADS):
            sl = slice(h * GLA_DV, (h + 1) * GLA_DV)
            oh, rh, dyh = ov[:, sl], rv[:, sl], dyv[:, sl]
            rs = lax.rsqrt(jnp.mean(oh * oh, axis=-1, keepdims=True) + RMS_EPS)
            xh = oh * rs
            sg = jax.nn.sigmoid(rh)
            d_on = dyh * (rh * sg)
            drs.append(dyh * (xh * p[0]) * (sg * (1.0 + rh * (1.0 - sg))))
            dg = dg + _colsum(d_on * xh)
            dxh = d_on * p[0]
            dos.append(rs * (dxh - xh * jnp.mean(dxh * xh, axis=-1, keepdims=True)))
        return [jnp.concatenate(dos, axis=1), jnp.concatenate(drs, axis=1)], [dg]
    return _rowwise(fn, name="gla_post_bwd",
                    rows=[(dy, GLA_HV, 0), (o, GLA_HV, 0), (pin, GLA_HV, (2 * GLA_HK + GLA_HV) // GLA_HV)],
                    pars=[g], outs=[(GLA_HV, F32), (GLA_HV, BF16)], accs=[GLA_DV])


def _gla_gate_bwd(dla, la):
    def fn(r, p):
        dz = r[0] * (1.0 / GLA_TAU) * (1.0 - jnp.exp(GLA_TAU * r[1]))
        return [dz], [_colsum(dz)]
    return _rowwise(fn, name="gla_gate_bwd", rows=[(dla, GLA_HK, 0), (la, GLA_HK, 0)], outs=[(GLA_HK, BF16)],
                    accs=[GLA_HK])


def _log_sigmoid(z):
    return jnp.minimum(z, 0.0) - jnp.log(1.0 + jnp.exp(-jnp.abs(z)))


def _rot_half(v):
    lane = lax.broadcasted_iota(jnp.int32, v.shape, 1)
    return jnp.where(lane < 32, -pltpu.roll(v, 96, 1), jnp.where(lane < 64, pltpu.roll(v, 32, 1), 0.0))


def _rms(v):
    rs = lax.rsqrt(jnp.mean(v * v, axis=-1, keepdims=True) + RMS_EPS)
    return v * rs, rs


def _mla_norm_fwd(cin, gq, gkv, cosp, sinp):
    def fn(r, p):
        cv, cs, sn = r
        qn, _ = _rms(cv[:, :MLA_QR])
        kvn, _ = _rms(cv[:, MLA_QR:MLA_QR + MLA_KVR])
        kr = cv[:, MLA_QR + MLA_KVR:]
        return [qn * p[0], kvn * p[1], kr * cs + _rot_half(kr) * sn], []
    return _rowwise(fn, name="mla_norm_fwd", rows=[(cin, MLA_IN_PAD, 0), (cosp, 128, 0), (sinp, 128, 0)],
                    pars=[gq, gkv], outs=[(MLA_QR, BF16), (MLA_KVR, BF16), (128, BF16)])


def _mla_qrope_fwd(q, cosp, sinp):
    scale = (MLA_NOPE + MLA_ROPE) ** -0.5

    def fn(r, p):
        qv, cs, sn = r
        parts = []
        for h in range(MLA_HEADS):
            parts.append(qv[:, h * MLA_QH:h * MLA_QH + 128] * scale)
            rp = qv[:, h * MLA_QH + 128:(h + 1) * MLA_QH]
            parts.append((rp * cs + _rot_half(rp) * sn) * scale)
        return [jnp.concatenate(parts, axis=1)], []
    W = MLA_HEADS * MLA_QH
    return _rowwise(fn, name="mla_qrope_fwd", rows=[(q, W, 0), (cosp, 128, 0), (sinp, 128, 0)],
                    outs=[(W, BF16)])[0]


def _mla_qrope_bwd(dq, cosp, sinp):
    scale = (MLA_NOPE + MLA_ROPE) ** -0.5

    def fn(r, p):
        dv, cs, sn = r
        parts = []
        for h in range(MLA_HEADS):
            parts.append(dv[:, h * MLA_QH:h * MLA_QH + 128] * scale)
            rp = dv[:, h * MLA_QH + 128:(h + 1) * MLA_QH]
            parts.append((rp * cs - _rot_half(rp) * sn) * scale)
        return [jnp.concatenate(parts, axis=1)], []
    W = MLA_HEADS * MLA_QH
    return _rowwise(fn, name="mla_qrope_bwd", rows=[(dq, W, 0), (cosp, 128, 0), (sinp, 128, 0)],
                    outs=[(W, BF16)])[0]


def _mla_norm_bwd(cin, dqn, dkvn, dkr, gq, gkv, cosp, sinp):
    def fn(r, p):
        cv, dq_, dkv_, dkr_, cs, sn = r
        outs, accs = [], []
        for (lo, hi), dn, g in (((0, MLA_QR), dq_, p[0]), ((MLA_QR, MLA_QR + MLA_KVR), dkv_, p[1])):
            xh, rs = _rms(cv[:, lo:hi])
            dxh = dn * g
            outs.append(rs * (dxh - xh * jnp.mean(dxh * xh, axis=-1, keepdims=True)))
            accs.append(_colsum(dn * xh))
        dk = dkr_[:, 0:128]
        for h in range(1, MLA_HEADS):
            dk = dk + dkr_[:, h * 128:(h + 1) * 128]
        outs.append(dk * cs - _rot_half(dk) * sn)
        return [jnp.concatenate(outs, axis=1)], accs
    return _rowwise(fn, name="mla_norm_bwd",
                    rows=[(cin, MLA_IN_PAD, 0), (dqn, MLA_QR, 0), (dkvn, MLA_KVR, 0), (dkr, MLA_HEADS * 128, 0),
                          (cosp, 128, 0), (sinp, 128, 0)],
                    pars=[gq, gkv], outs=[(MLA_IN_PAD, BF16)], accs=[MLA_QR, MLA_KVR])


def _mla_probs(q, k, i, tq):
    s = _nt(q, k)
    row = (i * tq + lax.broadcasted_iota(jnp.int32, s.shape, 0)) // CHUNK
    col = lax.broadcasted_iota(jnp.int32, s.shape, 1) // CHUNK
    s = jnp.where(col <= row, s, -jnp.inf)
    e = jnp.exp(s - jnp.max(s, axis=-1, keepdims=True))
    return e / jnp.sum(e, axis=-1, keepdims=True)


def _mla_attn_fwd(qr, knv, kr, tq=256):
    S = qr.shape[0]
    tq = min(tq, S)

    def body(q_ref, kn_ref, v_ref, kr_ref, o_ref, k_cat):
        k_cat[:, :128] = kn_ref[...]
        k_cat[:, 128:] = kr_ref[...]
        for i in range(S // tq):
            rows, keys = pl.ds(i * tq, tq), pl.ds(0, (i + 1) * tq)
            pr = _mla_probs(q_ref[rows, :], k_cat[keys, :], i, tq)
            o_ref[rows, :] = _nn(pr.astype(BF16), v_ref[keys, :])

    return pl.pallas_call(
        body, name="mla_attn_fwd", grid=(MLA_HEADS,),
        in_specs=[pl.BlockSpec((S, MLA_QH), lambda h: (0, h)),
                  pl.BlockSpec((S, 128), lambda h: (0, h)),
                  pl.BlockSpec((S, 128), lambda h: (0, MLA_HEADS + h)),
                  pl.BlockSpec((S, 128), lambda h: (0, 0))],
        out_specs=pl.BlockSpec((S, 128), lambda h: (0, h)),
        out_shape=_out((S, MLA_HEADS * MLA_V), F32),
        scratch_shapes=[pltpu.VMEM((S, MLA_QH), BF16)],
        compiler_params=_cparams(("parallel",)),
    )(qr, knv, knv, kr)


def _mla_attn_bwd(qr, knv, kr, o, do, tq=256):
    S = qr.shape[0]
    tq = min(tq, S)
    W = MLA_HEADS * 128

    def body(q_ref, kn_ref, v_ref, kr_ref, o_ref, do_ref, dq_ref, dkn_ref, dv_ref, dkr_ref, k_cat, dk_acc, dv_acc):
        k_cat[:, :128] = kn_ref[...]
        k_cat[:, 128:] = kr_ref[...]
        dk_acc[...] = jnp.zeros(dk_acc.shape, F32)
        dv_acc[...] = jnp.zeros(dv_acc.shape, F32)
        for i in range(S // tq):
            rows, keys = pl.ds(i * tq, tq), pl.ds(0, (i + 1) * tq)
            q, k, v = q_ref[rows, :], k_cat[keys, :], v_ref[keys, :]
            pr = _mla_probs(q, k, i, tq)
            dov = do_ref[rows, :]
            delta = jnp.sum(dov * o_ref[rows, :], axis=-1, keepdims=True)
            dob = dov.astype(BF16)
            ds = (pr * (_nt(dob, v) - delta)).astype(BF16)
            dq_ref[rows, :] = _nn(ds, k)
            dk_acc[keys, :] += _tn(ds, q)
            dv_acc[keys, :] += _tn(pr.astype(BF16), dob)
        dkn_ref[...] = dk_acc[:, :128].astype(dkn_ref.dtype)
        dkr_ref[...] = dk_acc[:, 128:]
        dv_ref[...] = dv_acc[...].astype(dv_ref.dtype)

    head = lambda w: pl.BlockSpec((S, w), lambda h: (0, h))
    return pl.pallas_call(
        body, name="mla_attn_bwd", grid=(MLA_HEADS,),
        in_specs=[head(MLA_QH), head(128), pl.BlockSpec((S, 128), lambda h: (0, MLA_HEADS + h)),
                  pl.BlockSpec((S, 128), lambda h: (0, 0)), head(128), head(128)],
        out_specs=[head(MLA_QH), head(128), head(128), head(128)],
        out_shape=[_out((S, MLA_HEADS * MLA_QH), F32), _out((S, W), BF16), _out((S, W), BF16), _out((S, W), F32)],
        scratch_shapes=[pltpu.VMEM((S, MLA_QH), BF16), pltpu.VMEM((S, MLA_QH), F32), pltpu.VMEM((S, 128), F32)],
        compiler_params=_cparams(("parallel",)),
    )(qr, knv, knv, kr, o, do)


CONV_TILE = 256


def _shift_down(v, n):
    row = lax.broadcasted_iota(jnp.int32, v.shape, 0)
    return jnp.where(row >= n, pltpu.roll(v, n, 0), 0.0)


def _shift_up(v, n):
    S = v.shape[0]
    row = lax.broadcasted_iota(jnp.int32, v.shape, 0)
    return jnp.where(row < S - n, pltpu.roll(v, S - n, 0), 0.0)


def _conv_specs(S, n_extra_cols):
    nt = D_MODEL // CONV_TILE
    specs = [pl.BlockSpec((S, CONV_TILE), functools.partial(lambda j, o: (0, o + j), o=part * nt))
             for part in range(3)]
    specs.append(pl.BlockSpec((8, CONV_TILE), lambda j: (0, j)))
    specs += [pl.BlockSpec((S, CONV_TILE), lambda j: (0, j)) for _ in range(n_extra_cols)]
    return specs


def _conv_fwd(bcu, w8):
    S = bcu.shape[0]

    def body(b_ref, c_ref, u_ref, w_ref, y_ref):
        cu = c_ref[...] * u_ref[...]
        z = w_ref[2:3, :] * cu + w_ref[1:2, :] * _shift_down(cu, 1) + w_ref[0:1, :] * _shift_down(cu, 2)
        y_ref[...] = (b_ref[...] * z).astype(y_ref.dtype)

    return pl.pallas_call(
        body, name="conv_fwd", grid=(D_MODEL // CONV_TILE,), in_specs=_conv_specs(S, 0),
        out_specs=pl.BlockSpec((S, CONV_TILE), lambda j: (0, j)),
        out_shape=_out((S, D_MODEL), BF16),
        compiler_params=_cparams(("parallel",)),
    )(bcu, bcu, bcu, w8)


def _conv_bwd(bcu, w8, dy):
    S = bcu.shape[0]

    def body(b_ref, c_ref, u_ref, w_ref, dy_ref, db_ref, dc_ref, du_ref, dw_ref):
        b, c, u, dyv = b_ref[...], c_ref[...], u_ref[...], dy_ref[...]
        w0, w1, w2 = w_ref[0:1, :], w_ref[1:2, :], w_ref[2:3, :]
        cu = c * u
        cu1, cu2 = _shift_down(cu, 1), _shift_down(cu, 2)
        z = w2 * cu + w1 * cu1 + w0 * cu2
        dz = dyv * b
        db_ref[...] = (dyv * z).astype(db_ref.dtype)
        dcu = w2 * dz + w1 * _shift_up(dz, 1) + w0 * _shift_up(dz, 2)
        dc_ref[...] = (dcu * u).astype(dc_ref.dtype)
        du_ref[...] = (dcu * c).astype(du_ref.dtype)
        dw_ref[...] = jnp.zeros(dw_ref.shape, F32)
        dw_ref[0:1, :] = _colsum(dz * cu2)
        dw_ref[1:2, :] = _colsum(dz * cu1)
        dw_ref[2:3, :] = _colsum(dz * cu)

    col = pl.BlockSpec((S, CONV_TILE), lambda j: (0, j))
    return pl.pallas_call(
        body, name="conv_bwd", grid=(D_MODEL // CONV_TILE,), in_specs=_conv_specs(S, 1),
        out_specs=[col, col, col, pl.BlockSpec((8, CONV_TILE), lambda j: (0, j))],
        out_shape=[_out((S, D_MODEL), BF16)] * 3 + [_out((8, D_MODEL), F32)],
        compiler_params=_cparams(("parallel",)),
    )(bcu, bcu, bcu, w8, dy)


def _adamw_update(w, g, m, v):
    nm = ADAM_B1 * m + (1.0 - ADAM_B1) * g
    nv = ADAM_B2 * v + (1.0 - ADAM_B2) * jnp.square(g)
    m_hat = nm / (1.0 - ADAM_B1 ** ADAM_STEP)
    v_hat = nv / (1.0 - ADAM_B2 ** ADAM_STEP)
    return -ADAM_LR * (m_hat / (jnp.sqrt(v_hat) + ADAM_EPS) + ADAM_WD * w), nm, nv


def _adamw_shard_major(w, m, v, gs, name):
    view = lambda a: jnp.transpose(a, (2, 0, 1))
    g = jnp.stack([x.T for x in gs], axis=1)
    n, L, k = g.shape
    rows = n // 4
    assert n % 4 == 0

    def body(w_ref, m_ref, v_ref, g_ref, go_ref, d_ref, nm_ref, nv_ref):
        gv = g_ref[...]
        d_ref[...], nm_ref[...], nv_ref[...] = _adamw_update(w_ref[...], gv, m_ref[...], v_ref[...])
        go_ref[...] = gv

    spec = pl.BlockSpec((rows, L, k), lambda i: (i, 0, 0))
    outs = pl.pallas_call(
        body, name=name, grid=(4,), in_specs=[spec] * 4, out_specs=[spec] * 4,
        out_shape=[jax.ShapeDtypeStruct((n, L, k), F32)] * 4,
        compiler_params=_cparams(("parallel",)),
    )(view(w), view(m), view(v), g)
    return [jnp.transpose(o, (1, 2, 0)) for o in outs]


def _adamw_small(ws, gs, ms, vs):
    n = len(ws)

    def body(*refs):
        ins, outs = refs[:4 * n], refs[4 * n:]
        for t in range(n):
            w_ref, g_ref, m_ref, v_ref = (ins[k * n + t] for k in range(4))
            gv = g_ref[...]
            outs[4 * t][...] = gv
            outs[4 * t + 1][...], outs[4 * t + 2][...], outs[4 * t + 3][...] = _adamw_update(
                w_ref[...], gv, m_ref[...], v_ref[...])

    return pl.pallas_call(
        body, name="adamw_small",
        out_shape=[jax.ShapeDtypeStruct(a.shape, F32) for a in ws for _ in range(4)],
    )(*ws, *gs, *ms, *vs)


def _adamw(w, m, v, gs, name):
    L, R, Cn = w.shape
    assert len(gs) == L
    tr = R if R <= 256 else 256
    assert R % tr == 0

    def body(w_ref, m_ref, v_ref, *rest):
        g_refs, (go_ref, d_ref, nm_ref, nv_ref) = rest[:L], rest[L:]
        layer = pl.program_id(0)
        gv = g_refs[0][...]
        for k in range(1, L):
            gv = jnp.where(layer == k, g_refs[k][...], gv)
        d_ref[...], nm_ref[...], nv_ref[...] = _adamw_update(w_ref[...], gv, m_ref[...], v_ref[...])
        go_ref[...] = gv

    spec = pl.BlockSpec((None, tr, Cn), lambda l, i: (l, i, 0))
    g_specs = [pl.BlockSpec((tr, Cn), functools.partial(lambda l, i, k: (jnp.where(l == k, i, 0), 0), k=k))
               for k in range(L)]
    return pl.pallas_call(
        body, name=name, grid=(L, R // tr), in_specs=[spec] * 3 + g_specs, out_specs=[spec] * 4,
        out_shape=[jax.ShapeDtypeStruct((L, R, Cn), F32)] * 4,
        compiler_params=_cparams(("arbitrary", "arbitrary")),
    )(w, m, v, *gs)


HBM_SPEC = pl.BlockSpec(memory_space=pltpu.HBM)


def _place():
    return lax.axis_index("x"), lax.axis_index("y"), lax.axis_index("c")


def _other_chips(x, y):
    return [(1 - x, y), (x, 1 - y), (1 - x, 1 - y)]


SEM_SPEC = pl.BlockSpec(memory_space=pltpu.SEMAPHORE)
ANY_SPEC = pl.BlockSpec(memory_space=pl.ANY)
VMEM_SPEC = pl.BlockSpec(memory_space=pltpu.VMEM)
EFFECT = pltpu.SideEffectType.DATAFLOW_SIDE_EFFECTING
TOKEN = (8, 128)


def _ici_start(srcs, lands, after, copies, name, per_src=3):
    n, nl = len(srcs), len(lands)

    def body(*refs):
        src_refs, land_refs = refs[:n], refs[n:n + nl]
        send_sems, recv_sems, token = refs[n + nl + 1], refs[n + nl + 2], refs[-1]
        x, y, c = _place()
        for k, src, dst, to in copies(src_refs, land_refs, x, y, c):
            pltpu.make_async_remote_copy(src_ref=src, dst_ref=dst, send_sem=send_sems.at[k], recv_sem=recv_sems.at[k],
                                         device_id=to, device_id_type=MESH).start()
        token[...] = jnp.zeros(TOKEN, F32)

    n_copies = per_src * max(n, nl if n == 0 else 0)
    res = pl.pallas_call(
        body, name=name,
        out_shape=(pltpu.SemaphoreType.DMA((n_copies,)), pltpu.SemaphoreType.DMA((n_copies,)),
                   *[pltpu.HBM(s.shape, s.dtype) for s in srcs], *[pltpu.HBM(l.shape, l.dtype) for l in lands],
                   jax.ShapeDtypeStruct(TOKEN, F32)),
        in_specs=[HBM_SPEC] * (n + nl) + [ANY_SPEC],
        out_specs=(SEM_SPEC, SEM_SPEC, *[HBM_SPEC] * (n + nl), VMEM_SPEC),
        input_output_aliases={t: 2 + t for t in range(n + nl)},
        compiler_params=pltpu.CompilerParams(has_side_effects=EFFECT),
    )(*[_hbm(s) for s in srcs], *[_hbm(l) for l in lands], after)
    return res[0], res[1], list(res[2:2 + n]), list(res[2 + n:2 + n + nl]), res[-1]


def _ici_wait(handle, after, copies, name):
    send_sems, recv_sems, srcs, lands, _ = handle
    n, nl = len(srcs), len(lands)

    def body(*refs):
        src_refs, land_refs = refs[:n], refs[n:n + nl]
        send_s, recv_s = refs[n + nl], refs[n + nl + 1]
        x, y, c = _place()
        for k, src, dst, to in copies(src_refs, land_refs, x, y, c):
            cp = pltpu.make_async_remote_copy(src_ref=src, dst_ref=dst, send_sem=send_s.at[k], recv_sem=recv_s.at[k],
                                              device_id=to, device_id_type=MESH)
            cp.wait_send()
            cp.wait_recv()

    res = pl.pallas_call(
        body, name=name,
        out_shape=(*[pltpu.HBM(s.shape, s.dtype) for s in srcs], *[pltpu.HBM(l.shape, l.dtype) for l in lands]),
        in_specs=[HBM_SPEC] * (n + nl) + [SEM_SPEC, SEM_SPEC, ANY_SPEC],
        out_specs=tuple([HBM_SPEC] * (n + nl)),
        input_output_aliases={t: t for t in range(n + nl)},
        compiler_params=pltpu.CompilerParams(has_side_effects=EFFECT),
    )(*srcs, *lands, send_sems, recv_sems, after)
    return list(res[:n]), list(res[n:])


def _gather_copies(halves, arriving):
    def copies(src_refs, land_refs, x, y, c):
        q = 2 * x + y
        out = []
        for t, H in enumerate(halves):
            mine = land_refs[t].at[q, pl.ds(c * H, H), :]
            for j, (cx, cy) in enumerate(_other_chips(x, y)):
                theirs = land_refs[t].at[2 * cx + cy, pl.ds(c * H, H), :]
                out.append((3 * t + j, mine, theirs if arriving else mine, (cx, cy, c)))
        return out
    return copies


def _place_own(ops, after, name):
    n = len(ops)
    kinds = sorted({(o.shape, str(o.dtype)) for o in ops})
    kind_of = [kinds.index((o.shape, str(o.dtype))) for o in ops]

    def body(*refs):
        in_refs, out_refs = refs[:n], refs[n + 1:2 * n + 1]
        rd_sems, wr_sems, bufs = refs[2 * n + 1], refs[2 * n + 2], refs[2 * n + 3:]
        x, y, _ = _place()
        used = [0] * len(kinds)
        slot, busy = [], {}
        for t in range(n):
            slot.append((kind_of[t], used[kind_of[t]] % 2))
            used[kind_of[t]] += 1
        rd = lambda t: pltpu.make_async_copy(in_refs[t], bufs[slot[t][0]].at[slot[t][1]], rd_sems.at[t])
        wr = lambda t: pltpu.make_async_copy(bufs[slot[t][0]].at[slot[t][1]], out_refs[t].at[2 * x + y],
                                             wr_sems.at[t])
        rd(0).start()
        for t in range(n):
            rd(t).wait()
            wr(t).start()
            busy[slot[t]] = t
            if t + 1 < n:
                if slot[t + 1] in busy:
                    wr(busy.pop(slot[t + 1])).wait()
                rd(t + 1).start()
        for t in busy.values():
            wr(t).wait()

    return pl.pallas_call(
        body, name=name, in_specs=[HBM_SPEC] * n + [ANY_SPEC], out_specs=[HBM_SPEC] * n,
        out_shape=[jax.ShapeDtypeStruct((N_CHIPS,) + o.shape, o.dtype) for o in ops],
        scratch_shapes=[pltpu.SemaphoreType.DMA((n,)), pltpu.SemaphoreType.DMA((n,))]
        + [pltpu.VMEM((2,) + shape, jnp.dtype(dt)) for shape, dt in kinds],
        compiler_params=pltpu.CompilerParams(vmem_limit_bytes=VMEM_LIMIT),
    )(*ops, after)


def _gather_start(lands, after, name):
    return _ici_start([], lands, after, _gather_copies([l.shape[1] // 2 for l in lands], False), name)


def _gather_wait(handle, after, name):
    halves = [l.shape[1] // 2 for l in handle[3]]
    return _ici_wait(handle, after, _gather_copies(halves, True), name)


def _forward_copies(halves, arriving):
    def copies(src_refs, land_refs, x, y, c):
        out = []
        for t, H in enumerate(halves):
            for j, (cx, cy) in enumerate(_other_chips(x, y)):
                mine = land_refs[t].at[2 * cx + cy, pl.ds(c * H, H), :]
                theirs = land_refs[t].at[2 * cx + cy, pl.ds((1 - c) * H, H), :]
                out.append((3 * t + j, mine, theirs if arriving else mine, (x, y, 1 - c)))
        return out
    return copies


def _forward_start(lands, after, name):
    halves = [l.shape[1] // 2 for l in lands]
    return _ici_start([], lands, after, _forward_copies(halves, False), name)


def _forward_wait(handle, after, name):
    halves = [l.shape[1] // 2 for l in handle[3]]
    return _ici_wait(handle, after, _forward_copies(halves, True), name)[1]


def _swap_halves(ops, name):
    n = len(ops)

    def body(*refs):
        in_refs, out_refs, send_sems, recv_sems = refs[:n], refs[n:2 * n], refs[2 * n], refs[2 * n + 1]
        x, y, c = _place()
        cps = []
        for t in range(n):
            H = ops[t].shape[1] // 2
            cp = pltpu.make_async_remote_copy(src_ref=in_refs[t].at[:, pl.ds((1 - c) * H, H), :],
                                              dst_ref=out_refs[t], send_sem=send_sems.at[t],
                                              recv_sem=recv_sems.at[t], device_id=(x, y, 1 - c),
                                              device_id_type=MESH)
            cp.start()
            cps.append(cp)
        for cp in cps:
            cp.wait()

    return pl.pallas_call(
        body, name=name, in_specs=[HBM_SPEC] * n, out_specs=[HBM_SPEC] * n,
        out_shape=[jax.ShapeDtypeStruct((N_CHIPS, o.shape[1] // 2, o.shape[2]), o.dtype) for o in ops],
        scratch_shapes=[pltpu.SemaphoreType.DMA((n,)), pltpu.SemaphoreType.DMA((n,))],
    )(*ops)


def _sum_rows_tile(h):
    return h if h <= 512 else 512


def _pair_sum(g, t, cq, name):
    _, a, b = g.shape
    H = a // 2
    tr = _sum_rows_tile(H)

    def body(cq_ref, g_ref, t_ref, o_ref):
        o_ref[...] = (g_ref[...].astype(F32) + t_ref[...].astype(F32)).astype(o_ref.dtype)

    grid_spec = pltpu.PrefetchScalarGridSpec(
        num_scalar_prefetch=1, grid=(N_CHIPS, H // tr),
        in_specs=[pl.BlockSpec((None, None, tr, b), lambda j, i, cq_ref: (j, cq_ref[0], i, 0)),
                  pl.BlockSpec((None, tr, b), lambda j, i, cq_ref: (j, i, 0))],
        out_specs=pl.BlockSpec((None, tr, b), lambda j, i, cq_ref: (j, i, 0)))
    return pl.pallas_call(
        body, name=name, grid_spec=grid_spec, out_shape=_out(t.shape, BF16),
        compiler_params=_cparams(("parallel", "parallel")),
    )(cq, g.reshape(N_CHIPS, 2, H, b), t)


def _scatter_copies(src_refs, land_refs, x, y, c):
    out = []
    for j, (cx, cy) in enumerate(_other_chips(x, y)):
        for t in range(len(src_refs)):
            out.append((3 * t + j, src_refs[t].at[2 * cx + cy], land_refs[t].at[j], (cx, cy, c)))
    return out


def _scatter_start(ops, after, name):
    lands = [lax.empty((3,) + o.shape[1:], o.dtype) for o in ops]
    return _ici_start(ops, lands, after, _scatter_copies, name)


def _scatter_wait(handle, after, name):
    return _ici_wait(handle, after, _scatter_copies, name)


def _chip_sum(p, t, cq, name):
    _, H, b = p.shape
    tr = _sum_rows_tile(H)

    def body(cq_ref, p_ref, t_ref, o_ref):
        acc = p_ref[...].astype(F32)
        for j in range(3):
            acc = acc + t_ref[j].astype(F32)
        o_ref[...] = acc

    grid_spec = pltpu.PrefetchScalarGridSpec(
        num_scalar_prefetch=1, grid=(H // tr,),
        in_specs=[pl.BlockSpec((None, tr, b), lambda i, cq_ref: (cq_ref[1], i, 0)),
                  pl.BlockSpec((3, tr, b), lambda i, cq_ref: (0, i, 0))],
        out_specs=pl.BlockSpec((None, tr, b), lambda i, cq_ref: (cq_ref[0], i, 0)))
    out = pl.pallas_call(
        body, name=name, grid_spec=grid_spec, out_shape=_out((2, H, b), F32),
        compiler_params=_cparams(("parallel",)),
    )(cq, p, t)
    return out.reshape(2 * H, b)


def _join_copies(arriving):
    def copies(src_refs, land_refs, x, y, c):
        out = []
        for t, land in enumerate(land_refs):
            H = land.shape[0] // 2
            mine, theirs = land.at[pl.ds(c * H, H), :], land.at[pl.ds((1 - c) * H, H), :]
            out.append((t, mine, theirs if arriving else mine, (x, y, 1 - c)))
        return out
    return copies


def _join_start(fs, name):
    return _ici_start([], fs, jnp.zeros(TOKEN, F32), _join_copies(False), name, per_src=1)


def _join_wait(handle, after, name):
    return _ici_wait(handle, after, _join_copies(True), name)[1]


def _direct_copies(src_refs, land_refs, x, y, c):
    out = []
    for t in range(len(src_refs)):
        H = src_refs[t].shape[1] // 2
        for k in range(1, 8):
            px, py, pc = x ^ (k >> 2), y ^ ((k >> 1) & 1), c ^ (k & 1)
            out.append((7 * t + k - 1, src_refs[t].at[2 * px + py, pl.ds(pc * H, H), :], land_refs[t].at[k - 1],
                        (px, py, pc)))
    return out


def _direct_sum(g, t, cq, name):
    _, a, b = g.shape
    H = a // 2
    tr = _sum_rows_tile(H)

    def body(cq_ref, g_ref, t_ref, o_ref):
        acc = g_ref[...].astype(F32)
        for k in range(7):
            acc = acc + t_ref[k].astype(F32)
        o_ref[...] = acc

    grid_spec = pltpu.PrefetchScalarGridSpec(
        num_scalar_prefetch=1, grid=(H // tr,),
        in_specs=[pl.BlockSpec((None, None, tr, b), lambda i, cq_ref: (cq_ref[1], cq_ref[0], i, 0)),
                  pl.BlockSpec((7, tr, b), lambda i, cq_ref: (0, i, 0))],
        out_specs=pl.BlockSpec((None, tr, b), lambda i, cq_ref: (cq_ref[0], i, 0)))
    out = pl.pallas_call(
        body, name=name, grid_spec=grid_spec, out_shape=_out((2, H, b), F32),
        compiler_params=_cparams(("parallel",)),
    )(cq, g.reshape(N_CHIPS, 2, H, b), t)
    return out.reshape(a, b)


def _reduce_direct_start(gs, tag):
    lands = [lax.empty((7, g.shape[1] // 2, g.shape[2]), g.dtype) for g in gs]
    return _ici_start(gs, lands, jnp.zeros(TOKEN, F32), _direct_copies, "rs_direct_start_" + tag, per_src=7)


def _reduce_direct_finish(handle, cq, after, tag):
    gs, rs = _ici_wait(handle, after, _direct_copies, "rs_direct_wait_" + tag)
    fs = [_direct_sum(g, r, cq, "rs_direct_sum") for g, r in zip(gs, rs)]
    return _join_start(fs, "rs_join_start_" + tag)


def _reduce_scatter_start(gs, cq, after, tag):
    ts = _swap_halves(gs, "rs_swap_" + tag)
    ps = [_pair_sum(g, t, cq, "rs_pair_sum") for g, t in zip(gs, ts)]
    return _scatter_start(ps, after, "rs_scatter_start_" + tag)


def _reduce_scatter_finish(handle, cq, after, tag):
    ps, rs = _scatter_wait(handle, after, "rs_scatter_wait_" + tag)
    fs = [_chip_sum(p, r, cq, "rs_chip_sum") for p, r in zip(ps, rs)]
    return _join_start(fs, "rs_join_start_" + tag)


def _all_reduce_small(v):
    n = v.shape[0]

    def body(v_ref, out_ref, buf, send_sems, recv_sems):
        x, y, c = _place()
        me = 4 * x + 2 * y + c
        buf[me] = v_ref[...]
        cps = []
        for k in range(1, 8):
            peer = (x ^ (k >> 2), y ^ ((k >> 1) & 1), c ^ (k & 1))
            cp = pltpu.make_async_remote_copy(src_ref=v_ref, dst_ref=buf.at[me], send_sem=send_sems.at[k - 1],
                                              recv_sem=recv_sems.at[k - 1], device_id=peer, device_id_type=MESH)
            cp.start()
            cps.append(cp)
        for k in range(1, 8):
            px, py, pc = x ^ (k >> 2), y ^ ((k >> 1) & 1), c ^ (k & 1)
            land = buf.at[4 * px + 2 * py + pc]
            pltpu.make_async_remote_copy(src_ref=land, dst_ref=land, send_sem=send_sems.at[k - 1],
                                         recv_sem=recv_sems.at[k - 1], device_id=(px, py, pc),
                                         device_id_type=MESH).wait_recv()
        for cp in cps:
            cp.wait_send()
        acc = buf[0]
        for d in range(1, 8):
            acc = acc + buf[d]
        out_ref[...] = acc

    vm = pl.BlockSpec(memory_space=pltpu.VMEM)
    return pl.pallas_call(
        body, name="all_reduce_small", in_specs=[vm], out_specs=vm,
        out_shape=jax.ShapeDtypeStruct((n, 128), F32),
        scratch_shapes=[pltpu.VMEM((8, n, 128), F32), pltpu.SemaphoreType.DMA((7,)), pltpu.SemaphoreType.DMA((7,))],
    )(v)


SMALL_GATHER = (16, 1024)
SMALL_FULL = sum(_size(_full_shape(n)) for n in SMALL)
SMALL_FULL_ROWS = -(-(SMALL_FULL + 1) // 128 // 8) * 8


def _layer_shards(w, i, q):
    kind, j = MIXER[i % 3], i // 3
    out = {n: w[n][i].astype(BF16) for n in COMMON_BIG}
    if kind == 'gla':
        win = jnp.zeros((GLA_WIN, D_MODEL), F32)
        win = lax.dynamic_update_slice(win, w['gla_w_in'][j].T, ((GLA_SHARD - GLA_WIN_STEP) * q, 0))
        out['gla_w_in'] = win.astype(BF16)
        out['gla_w_out'] = w['gla_w_out'][j].astype(BF16)
    elif kind == 'mla':
        out['mla_w_in'] = jnp.pad(w['mla_w_in'][j], ((0, 0), (0, MLA_IN_PAD - MLA_IN))).astype(BF16)
        for n in ('mla_w_uq', 'mla_w_ukv', 'mla_w_out'):
            out[n] = w[n][j].astype(BF16)
    else:
        out['conv_w_in'] = w['conv_w_in'][j].astype(BF16)
        out['conv_w_out'] = w['conv_w_out'][j].astype(BF16)
    return out


def _rows_joined(g):
    return g.reshape(g.shape[0] * g.shape[1], g.shape[2])


def _cols_joined(g):
    return jnp.moveaxis(g, 0, 1).reshape(g.shape[1], -1)


def _layer_weights(g, i):
    kind = MIXER[i % 3]
    W = {}
    if 'mlp_w1' in g:
        W = {'w1': g['mlp_w1'], 'w2': _rows_joined(g['mlp_w2']), 'gate': _rows_joined(g['ple_w_gate']),
             'proj': g['ple_w_proj']}
    if kind == 'gla' and 'gla_w_out' in g:
        W['w_out'] = _rows_joined(g['gla_w_out'])
    if kind == 'gla' and 'gla_w_in' in g:
        parts = []
        for qq in range(N_CHIPS):
            lo = g['gla_w_in'][qq][:128]
            if qq > 0:
                lo = lo + g['gla_w_in'][qq - 1][GLA_WIN_STEP:]
            parts += [lo, g['gla_w_in'][qq][128:GLA_WIN_STEP]]
        parts.append(g['gla_w_in'][N_CHIPS - 1][GLA_WIN_STEP:])
        W['w_in'] = jnp.concatenate(parts, axis=0)
    elif kind == 'mla':
        W['w_in'] = _rows_joined(g['mla_w_in'])
        uq = _cols_joined(g['mla_w_uq']).reshape(MLA_QR, MLA_HEADS, MLA_NOPE + MLA_ROPE)
        W['w_uq'] = jnp.pad(uq, ((0, 0), (0, 0), (0, MLA_QH - MLA_NOPE - MLA_ROPE))).reshape(MLA_QR, -1)
        ukv = _cols_joined(g['mla_w_ukv']).reshape(MLA_KVR, MLA_HEADS, 2, 128)
        W['w_ukv'] = ukv.transpose(0, 2, 1, 3).reshape(MLA_KVR, -1)
        W['w_out'] = _rows_joined(g['mla_w_out'])
    elif kind == 'conv':
        W['w_in'] = g['conv_w_in']
        W['w_out'] = _rows_joined(g['conv_w_out'])
    return W


def _pack_small_shards(w):
    flat = jnp.concatenate([w[n].reshape(-1) for n in SMALL_SHARDED])
    return jnp.pad(flat, (0, _size(SMALL_GATHER) - flat.shape[0])).reshape(SMALL_GATHER)


def _unpack_small_gathered(g):
    flat, out, off = g.reshape(N_CHIPS, -1), {}, 0
    for n in SMALL_SHARDED:
        shape, ax = WSPEC[n]
        seg = flat[:, off:off + _size(shape)].reshape((N_CHIPS,) + shape)
        out[n] = jnp.moveaxis(seg, 0, ax).reshape(_full_shape(n))
        off += _size(shape)
    return out


def _pack_small(vals, loss):
    flat = jnp.concatenate([vals[n].reshape(-1) for n in SMALL] + [loss.reshape(1)])
    return jnp.pad(flat, (0, SMALL_FULL_ROWS * 128 - flat.shape[0])).reshape(SMALL_FULL_ROWS, 128)


def _unpack_small(packed, q):
    flat = packed.reshape(-1)
    out, off = {}, 0
    for n in SMALL:
        shape, ax = WSPEC[n]
        full = flat[off:off + _size(_full_shape(n))].reshape(_full_shape(n))
        off += _size(_full_shape(n))
        out[n] = full if ax is None else lax.dynamic_slice_in_dim(full, q * shape[ax], shape[ax], axis=ax)
    return out


def _row_shards(dw):
    return dw.reshape(N_CHIPS, dw.shape[0] // N_CHIPS, dw.shape[1])


def _col_shards(dw):
    return jnp.moveaxis(dw.reshape(dw.shape[0], N_CHIPS, -1), 1, 0)


def _row(v):
    return v.reshape(1, -1)


def _layer_fwd(i, xin, xin_b, p_i, W, sm, cosp, sinp, rest=None, mid=None):
    kind, j = MIXER[i % 3], i // 3
    sv = {'xin': xin, 'xin_b': xin_b}
    if kind == 'gla':
        w_up = jnp.pad(sm['gla_w_gate_up'][j].astype(BF16), ((0, 128 - GLA_RANK), (0, 0)))
        pin = _mm(xin_b, W['w_in'], tb=True, name="gla_in", tn=640, tm=FULL_ROWS)
        la = _mm(pin, w_up, name="gla_gate", K=128, tk=128, a_off=(0, (GLA_IN_PAD - 128) // 128), tn=512,
                 extras=[(_row(sm['gla_b_gate'][j]), 'n')],
                 epilogue=lambda acc, b: (_log_sigmoid(acc + b) * (1.0 / GLA_TAU),))
        o, states = _gla_fwd(pin, la)
        yb = _gla_post_fwd(o, pin, _row(sm['gla_norm_g'][j]))
        if rest is not None:
            W = {**W, **rest(yb)}
        mixed = yb
        sv.update(w_up=w_up, pin=pin, la=la, o=o, states=states, yb=yb)
    elif kind == 'mla':
        gq, gkv = sm['mla_q_norm'][j:j + 1], sm['mla_kv_norm'][j:j + 1]
        cin = _mm(xin_b, W['w_in'], name="mla_in", tn=640, tm=FULL_ROWS)
        qn, kvn, kr = _mla_norm_fwd(cin, gq, gkv, cosp, sinp)
        qr = _mla_qrope_fwd(_mm(qn, W['w_uq'], name="mla_uq"), cosp, sinp)
        knv = _mm(kvn, W['w_ukv'], name="mla_ukv", out_dtypes=(BF16,))
        o = _mla_attn_fwd(qr, knv, kr)
        ob = o.astype(BF16)
        mixed = ob
        sv.update(gq=gq, gkv=gkv, cin=cin, qn=qn, kvn=kvn, kr=kr, qr=qr, knv=knv, o=o, ob=ob)
    else:
        w8 = jnp.pad(sm['conv_w'][j], ((0, 5), (0, 0)))
        bcu = _mm(xin_b, W['w_in'], name="conv_in", tn=768, b_sh=True, tm=FULL_ROWS)
        yb = _conv_fwd(bcu, w8)
        mixed = yb
        sv.update(w8=w8, bcu=bcu, yb=yb)
    g0, b0 = _row(sm['ln_g'][i, 0]), _row(sm['ln_b'][i, 0])
    g1, b1 = _row(sm['ln_g'][i, 1]), _row(sm['ln_b'][i, 1])
    ln = dict(tm=512, tn=D_MODEL, out_dtypes=(F32, BF16, F32), epilogue=_ln_fwd_epilogue)
    x1, x1b, v0 = _mm(mixed, W['w_out'], name="mix_out_ln", extras=[(xin, 'mn'), (g0, 'n'), (b0, 'n')], **ln)
    ab, dadu = _mm(x1b, W['w1'], name="mlp_up", out_dtypes=(BF16, BF16), b_sh=True, tm=FULL_ROWS,
                   epilogue=lambda acc: (jnp.square(jnp.maximum(acc, 0.0)), 2.0 * jnp.maximum(acc, 0.0)))
    x2, x2b, v1 = _mm(ab, W['w2'], name="mlp_down_ln", tk=D_FF, extras=[(x1, 'mn'), (g1, 'n'), (b1, 'n')], **ln)
    order = [(mid(x2b), 'whole')] if mid else []
    pp = _mm(p_i, W['proj'], name="ple_proj", tn=256, b_sh=True, extras=order,
             epilogue=lambda acc, *unused: (acc,))
    z, x3, x3b = _mm(x2b, W['gate'], name="ple_gate", out_dtypes=(F32, F32, BF16),
                     extras=[(x2, 'mn'), (pp, 'mn')],
                     epilogue=lambda acc, xv, pv: (acc,) + (xv + jax.nn.sigmoid(acc) * pv,) * 2)
    sv.update(v0=v0, x1b=x1b, ab=ab, dadu=dadu, v1=v1, x2b=x2b, pp=pp, z=z, g0=g0, g1=g1)
    return x3, x3b, sv, W


def _layer_bwd(i, grads_in, p_i, W, sm, sv, cosp, sinp, token, early=None, below=None):
    kind, j = MIXER[i % 3], i // 3
    big, small = {}, {}
    dx, dpp_b, dz_b = grads_in
    big['ple_w_proj'] = _mm(p_i, dpp_b, ta=True, name="ple_proj_dw", tn=256, out_sh=True, out_dtypes=(BF16,))
    big['ple_w_gate'] = _row_shards(_mm(sv['x2b'], dz_b, ta=True, name="dw_dd", out_dtypes=(BF16,)))
    ln = dict(tb=True, tm=512, tn=D_MODEL, out_dtypes=(F32, BF16), n_sums=2)
    (dv1, dv1b), (dg1, db1) = _mm(dz_b, W['gate'], name="ple_gate_dx_ln", epilogue=_ln_bwd_epilogue(1.0),
                                  extras=[(dx, 'mn'), (sv['v1'], 'mn'), (sv['g1'], 'n'), (token, 'whole')], **ln)
    big['mlp_w2'] = _row_shards(_mm(sv['ab'], dv1b, ta=True, name="mlp_down_dw", out_dtypes=(BF16,)))
    dub = _mm(dv1b, W['w2'], tb=True, name="mlp_down_dx", out_dtypes=(BF16,), tm=FULL_ROWS,
              extras=[(sv['dadu'], 'mn')], epilogue=lambda acc, d: (acc * d.astype(F32),))
    big['mlp_w1'] = _mm(sv['x1b'], dub, ta=True, name="mlp_up_dw", out_sh=True, out_dtypes=(BF16,))
    order = []
    if early is not None:
        order, big = [(early(big), 'whole')], {}
    (dv0, dv0b), (dg0, db0) = _mm(dub, W['w1'], name="mlp_up_dx_ln", b_sh=True, tk=D_FF, epilogue=_ln_bwd_epilogue(ALPHA),
                                  extras=[(dv1, 'mn'), (sv['v0'], 'mn'), (sv['g0'], 'n')] + order, **ln)
    small['ln_g'] = jnp.stack([dg0[0], dg1[0]])
    small['ln_b'] = jnp.stack([db0[0], db1[0]])
    resid = dict(tb=True, tn=D_MODEL, tm=512 if below else 1024, epilogue=_input_grad_epilogue,
                 extras=[(dv0, 'mn')] + [(a, 'mn') for a in below or ()],
                 out_dtypes=(F32, BF16, BF16) if below else (F32,))
    if kind == 'gla':
        big['gla_w_out'] = _row_shards(_mm(sv['yb'], dv0b, ta=True, name="dw_dd", out_dtypes=(BF16,)))
        dy = _mm(dv0b, W['w_out'], tb=True, name="dx_dd", tn=1024)
        do, dr_b, dng = _gla_post_bwd(dy, sv['o'], sv['pin'], _row(sm['gla_norm_g'][j]))
        dq_b, dk_b, dvv_b, dla = _gla_bwd(sv['pin'], sv['la'], sv['states'], do)
        dzg_b, dbg = _gla_gate_bwd(dla, sv['la'])
        dw_up = _mm(sv['pin'], dzg_b, ta=True, name="gla_gate_dw", M=128, tm=128,
                    a_off=(0, (GLA_IN_PAD - 128) // 128))
        dglr_b = _mm(dzg_b, sv['w_up'], tb=True, name="gla_gate_dx", out_dtypes=(BF16,))
        dpin_b = jnp.concatenate([dq_b, dk_b, dvv_b, dr_b, dglr_b], axis=1)
        dw_in = _mm(sv['xin_b'], dpin_b, ta=True, name="gla_in_dw", tn=640, out_dtypes=(BF16,))
        dxin = _mm(dpin_b, W['w_in'], name="gla_in_dx", tk=GLA_IN_PAD, **{**resid, 'tb': False})
        big['gla_w_in'] = jnp.stack([dw_in[:, GLA_WIN_STEP * qq:GLA_WIN_STEP * qq + GLA_WIN]
                                     for qq in range(N_CHIPS)])
        small.update(gla_w_gate_up=dw_up[:GLA_RANK], gla_b_gate=dbg[0], gla_norm_g=dng[0])
    elif kind == 'mla':
        big['mla_w_out'] = _row_shards(_mm(sv['ob'], dv0b, ta=True, name="dw_dd", out_dtypes=(BF16,)))
        do = _mm(dv0b, W['w_out'], tb=True, name="dx_dd", tn=1024)
        dqr, dkn_b, dvv_b, dkr = _mla_attn_bwd(sv['qr'], sv['knv'], sv['kr'], sv['o'], do)
        dq_b = _mla_qrope_bwd(dqr, cosp, sinp)
        dw_uq = _mm(sv['qn'], dq_b, ta=True, name="mla_up_dw", out_dtypes=(BF16,))
        dqn = _mm(dq_b, W['w_uq'], tb=True, name="mla_up_dx")
        dknv_b = jnp.concatenate([dkn_b, dvv_b], axis=1)
        dw_ukv = _mm(sv['kvn'], dknv_b, ta=True, name="mla_up_dw", out_dtypes=(BF16,))
        dkvn = _mm(dknv_b, W['w_ukv'], tb=True, name="mla_up_dx")
        dcin_b, dgq, dgkv = _mla_norm_bwd(sv['cin'], dqn, dkvn, dkr, sv['gq'], sv['gkv'], cosp, sinp)
        big['mla_w_in'] = _row_shards(_mm(sv['xin_b'], dcin_b, ta=True, name="mla_in_dw", tn=640,
                                          out_dtypes=(BF16,)))
        dxin = _mm(dcin_b, W['w_in'], name="mla_in_dx", tk=MLA_IN_PAD, **resid)
        big['mla_w_uq'] = _col_shards(
            dw_uq.reshape(MLA_QR, MLA_HEADS, MLA_QH)[:, :, :MLA_NOPE + MLA_ROPE].reshape(MLA_QR, -1))
        big['mla_w_ukv'] = _col_shards(
            dw_ukv.reshape(MLA_KVR, 2, MLA_HEADS, 128).transpose(0, 2, 1, 3).reshape(MLA_KVR, -1))
        small.update(mla_q_norm=dgq[0], mla_kv_norm=dgkv[0])
    else:
        big['conv_w_out'] = _row_shards(_mm(sv['yb'], dv0b, ta=True, name="dw_dd", out_dtypes=(BF16,)))
        dy = _mm(dv0b, W['w_out'], tb=True, name="dx_dd", tn=1024)
        db_b, dc_b, du_b, dw8 = _conv_bwd(sv['bcu'], sv['w8'], dy)
        dbcu_b = jnp.concatenate([db_b, dc_b, du_b], axis=1)
        big['conv_w_in'] = _mm(sv['xin_b'], dbcu_b, ta=True, name="conv_in_dw", tn=768, out_sh=True,
                               out_dtypes=(BF16,))
        dxin = _mm(dbcu_b, W['w_in'], name="conv_in_dx", tk=3 * D_MODEL, b_sh=True, **resid)
        small['conv_w'] = dw8[:3]
    return (dxin if below else (dxin,)), big, small


def _rope_tables(positions):
    inv_freq = ROPE_BASE ** (-jnp.arange(0, MLA_ROPE // 2, dtype=F32) * (2.0 / MLA_ROPE))
    ang = positions.astype(F32)[:, None] * inv_freq
    zeros = jnp.zeros((positions.shape[0], 64), F32)
    return (jnp.concatenate([jnp.cos(ang), jnp.cos(ang), zeros], axis=1),
            jnp.concatenate([jnp.sin(ang), jnp.sin(ang), zeros], axis=1))


FIRST_NEEDED = ['gla_w_in']


def _start_gathers(w, q):
    token, started = jnp.zeros(TOKEN, F32), []
    for i in range(DEPTH):
        sh = _layer_shards(w, i, q)
        for k, names in enumerate([list(sh)] if i > 0 else [FIRST_NEEDED, [n for n in sh if n not in FIRST_NEEDED]]):
            ops = [sh[n] for n in names]
            if i == 0 and k == 0:
                ops.append(_pack_small_shards(w))
            tag = "l%d%s" % (i, "ab"[k] if i == 0 else "")
            handle = _gather_start(_place_own(ops, token, "ag_own_" + tag), token, "ag_start_" + tag)
            token = handle[4]
            started.append((handle, names, tag))
    return started, token


def _pass_on(entry, after):
    handle, names, tag = entry
    _, lands = _gather_wait(handle, after, "ag_wait_" + tag)
    passing = _forward_start(lands, jnp.zeros(TOKEN, F32), "ag_pass_start_" + tag)
    return (passing, names, tag), passing[4]


def _gathered(passed, after):
    passing, names, tag = passed
    got = _forward_wait(passing, after, "ag_pass_wait_" + tag)
    return dict(zip(names, got)), got[-1]


def _local_shard_grad(name, g, q):
    if name == 'gla_w_in':
        return lax.dynamic_slice_in_dim(g, (GLA_SHARD - GLA_WIN_STEP) * q, GLA_SHARD, axis=1)
    if name == 'mla_w_in':
        return g[:, :MLA_IN]
    return g


def kernel(x, p, positions, gla_w_in, gla_w_gate_up, gla_b_gate, gla_norm_g, gla_w_out, mla_w_in, mla_q_norm, mla_kv_norm, mla_w_uq, mla_w_ukv, mla_w_out, conv_w_in, conv_w, conv_w_out, ln_g, ln_b, mlp_w1, mlp_w2, ple_w_gate, ple_w_proj, loss_target, m_gla_w_in, m_gla_w_gate_up, m_gla_b_gate, m_gla_norm_g, m_gla_w_out, m_mla_w_in, m_mla_q_norm, m_mla_kv_norm, m_mla_w_uq, m_mla_w_ukv, m_mla_w_out, m_conv_w_in, m_conv_w, m_conv_w_out, m_ln_g, m_ln_b, m_mlp_w1, m_mlp_w2, m_ple_w_gate, m_ple_w_proj, v_gla_w_in, v_gla_w_gate_up, v_gla_b_gate, v_gla_norm_g, v_gla_w_out, v_mla_w_in, v_mla_q_norm, v_mla_kv_norm, v_mla_w_uq, v_mla_w_ukv, v_mla_w_out, v_conv_w_in, v_conv_w, v_conv_w_out, v_ln_g, v_ln_b, v_mlp_w1, v_mlp_w2, v_ple_w_gate, v_ple_w_proj):
    args = locals()
    w = {n: args[n] for n in WNAMES}
    m = {n: args['m_' + n] for n in WNAMES}
    v = {n: args['v_' + n] for n in WNAMES}
    q = 2 * lax.axis_index("x") + lax.axis_index("y")
    cq = jnp.stack([lax.axis_index("c"), q]).astype(jnp.int32)

    cosp, sinp = _rope_tables(positions[0])
    started, after = _start_gathers(w, q)
    xin, saved, layers, sm = x[0], [], [], None
    xin_b = xin.astype(BF16)
    passed, after = _pass_on(started[0], after)
    for i in range(DEPTH):
        got, last = _gathered(passed, after)
        rest = mid = None
        if i == 0:
            sm = _unpack_small_gathered(last)
            sm['mla_q_norm'], sm['mla_kv_norm'] = w['mla_q_norm'], w['mla_kv_norm']
            rest = lambda after: _layer_weights(_gathered(*_pass_on(started[1], after))[0], 0)
        coming = {}
        if i + 1 < DEPTH:
            def mid(after, entry=started[i + 2], coming=coming):
                coming['passed'], token = _pass_on(entry, after)
                return token
        xin, xin_b, sv, W = _layer_fwd(i, xin, xin_b, p[i, 0], _layer_weights(got, i), sm, cosp, sinp, rest, mid)
        layers.append(W)
        saved.append(sv)
        passed, after = coming.get('passed'), xin
    *grads_in, loss_cols = _loss_head(xin, loss_target[0], saved[-1]['z'], saved[-1]['pp'])
    loss = jnp.sum(loss_cols[0])

    gbig = {n: [None] * WSPEC[n][0][0] for n in BIG}
    gsmall = {n: [None] * _full_shape(n)[0] for n in SMALL}
    pending = []

    def start(grads, i, tag):
        names = list(grads)
        gs = [grads[n] for n in names]
        handle = _reduce_direct_start(gs, tag) if i > 0 else _reduce_scatter_start(gs, cq, jnp.zeros(TOKEN, F32), tag)
        pending.append((handle, names, i, tag))
        return handle[4]

    joining = []

    def finish(above, after, token):
        for entry in [e for e in pending if e[2] > above]:
            pending.remove(entry)
            handle, names, i, tag = entry
            handle = (_reduce_direct_finish if i > 0 else _reduce_scatter_finish)(handle, cq, after, tag)
            joining.append((handle, names, i, tag))
            token = token + handle[4]
        return token

    token = jnp.zeros(TOKEN, F32)
    for i in reversed(range(DEPTH)):
        early = (lambda grads: start(grads, 0, "l0a")) if i == 0 else None
        below = (saved[i - 1]['z'], saved[i - 1]['pp']) if i > 0 else None
        grads_in, big, small = _layer_bwd(i, grads_in, p[i, 0], layers[i], sm, saved[i], cosp, sinp, token, early,
                                          below)
        dx = grads_in[0]
        token = finish(i + 1, dx, start(big, i, "l%d%s" % (i, "b" if i == 0 else "")))
        for n, g in small.items():
            gsmall[n][i if n in ('ln_g', 'ln_b') else i // 3] = g
    grad, delta, new_m, new_v = {}, {}, {}, {}

    def take(entries):
        for handle, names, i, tag in entries:
            for n, g in zip(names, _join_wait(handle, entries[-1][0][4], "rs_join_wait_" + tag)):
                gbig[n][i if n in COMMON_BIG else i // 3] = _local_shard_grad(n, g, q)

    def update(n):
        adamw = _adamw_shard_major if n == 'gla_w_in' else _adamw
        grad[n], delta[n], new_m[n], new_v[n] = adamw(w[n], m[n], v[n], gbig[n], "adamw_" + n)

    token = finish(0, token, token)
    take(joining)
    del joining[:]
    ready = [n for n in BIG if n.startswith(('mla_', 'conv_'))]
    for n in ready:
        update(n)
    finish(-1, delta[ready[-1]], token)
    take(joining)
    for n in BIG:
        if n not in ready:
            update(n)
    small_sum = _all_reduce_small(_pack_small({n: jnp.stack(g) for n, g in gsmall.items()}, loss))
    gsm, loss = _unpack_small(small_sum, q), small_sum.reshape(-1)[SMALL_FULL]
    flat2 = lambda a: a.reshape(-1, a.shape[-1])
    res = _adamw_small(*[[flat2(d[n]) for n in SMALL] for d in (w, gsm, m, v)])
    for k, out in enumerate((grad, delta, new_m, new_v)):
        for n, r in zip(SMALL, res[k::4]):
            out[n] = r.reshape(WSPEC[n][0])
    return (loss, dx[None], *[grad[n] for n in WNAMES], *[delta[n] for n in WNAMES],
            *[new_m[n] for n in WNAMES], *[new_v[n] for n in WNAMES])
```

```python
import functools

import numpy as np
import jax
import jax.numpy as jnp
from jax import lax
from jax.experimental import pallas as pl
from jax.experimental.pallas import tpu as pltpu

F32 = jnp.float32
BF16 = jnp.bfloat16
MESH = pl.DeviceIdType.MESH

D_MODEL = 1024
DEPTH = 4
CHUNK = 64
ALPHA = (2 * DEPTH) ** 0.25
LN_EPS = 1e-5
RMS_EPS = 1e-6
D_FF = 4 * D_MODEL
GLA_HEADS = 4
GLA_DK = 128
GLA_DV = 256
GLA_RANK = 16
GLA_TAU = 16.0
GLA_HK = GLA_HEADS * GLA_DK
GLA_HV = GLA_HEADS * GLA_DV
GLA_IN = 2 * GLA_HK + GLA_HV + D_MODEL + GLA_RANK
GLA_IN_PAD = 2 * GLA_HK + GLA_HV + D_MODEL + 128
GLA_SHARD = GLA_IN // 4
GLA_WIN = 896
GLA_WIN_STEP = 768
MLA_HEADS = 8
MLA_NOPE = 128
MLA_ROPE = 64
MLA_V = 128
MLA_QR = 256
MLA_KVR = 256
MLA_IN = MLA_QR + MLA_KVR + MLA_ROPE
MLA_IN_PAD = MLA_QR + MLA_KVR + 128
MLA_QH = 256
ROPE_BASE = 10000.0
ADAM_LR = 0.001
ADAM_B1 = 0.9
ADAM_B2 = 0.999
ADAM_EPS = 1e-08
ADAM_WD = 0.01
ADAM_STEP = 10

VMEM_LIMIT = 48 * 1024 * 1024
FULL_ROWS = 2048
N_CHIPS = 4

WSPEC = {
    'gla_w_in': ((2, 1024, 772), 2), 'gla_w_gate_up': ((2, 16, 128), 2), 'gla_b_gate': ((2, 128), 1),
    'gla_norm_g': ((2, 64), 1), 'gla_w_out': ((2, 256, 1024), 1), 'mla_w_in': ((1, 256, 576), 1),
    'mla_q_norm': ((1, 256), None), 'mla_kv_norm': ((1, 256), None), 'mla_w_uq': ((1, 256, 384), 2),
    'mla_w_ukv': ((1, 256, 512), 2), 'mla_w_out': ((1, 256, 1024), 1), 'conv_w_in': ((1, 1024, 768), 2),
    'conv_w': ((1, 3, 256), 2), 'conv_w_out': ((1, 256, 1024), 1), 'ln_g': ((4, 2, 256), 2),
    'ln_b': ((4, 2, 256), 2), 'mlp_w1': ((4, 1024, 1024), 2), 'mlp_w2': ((4, 1024, 1024), 1),
    'ple_w_gate': ((4, 256, 1024), 1), 'ple_w_proj': ((4, 256, 256), 2),
}
WNAMES = list(WSPEC)
BIG = ['gla_w_in', 'gla_w_out', 'mla_w_in', 'mla_w_uq', 'mla_w_ukv', 'mla_w_out', 'conv_w_in', 'conv_w_out',
       'mlp_w1', 'mlp_w2', 'ple_w_gate', 'ple_w_proj']
SMALL_SHARDED = ['gla_w_gate_up', 'gla_b_gate', 'gla_norm_g', 'conv_w', 'ln_g', 'ln_b']
SMALL = SMALL_SHARDED + ['mla_q_norm', 'mla_kv_norm']
MIXER = ['gla', 'mla', 'conv']
COMMON_BIG = ['mlp_w1', 'mlp_w2', 'ple_w_gate', 'ple_w_proj']


def _size(shape):
    return int(np.prod(shape))


def _full_shape(name):
    shape, ax = WSPEC[name]
    if ax is None:
        return shape
    return tuple(s * N_CHIPS if i == ax else s for i, s in enumerate(shape))


def _cparams(sem=None):
    return pltpu.CompilerParams(dimension_semantics=sem, vmem_limit_bytes=VMEM_LIMIT)


def _out(shape, dtype):
    return pltpu.HBM(shape, dtype)


def _hbm(v):
    return pltpu.with_memory_space_constraint(v, pltpu.HBM)


def _mm(a, b, *, name, ta=False, tb=False, M=None, N=None, K=None, out_dtypes=(F32,), epilogue=None, extras=(),
        tm=1024, tn=512, tk=None, a_off=(0, 0), b_sh=False, out_sh=False, n_sums=0):
    if M is None:
        M = a.shape[1] if ta else a.shape[0]
    if K is None:
        K = a.shape[0] if ta else a.shape[1]
    if b_sh:
        kw, nq = b.shape[1], b.shape[2]
        n_b, k_b = (kw, N_CHIPS * nq) if tb else (N_CHIPS * nq, kw)
        N = n_b if N is None else N
        assert K == k_b
    elif N is None:
        N = b.shape[0] if tb else b.shape[1]
    if tk is None:
        tk = FULL_ROWS if ta else 1024
    tm, tn, tk = min(tm, M), min(tn, N), min(tk, K)
    assert M % tm == 0 and N % tn == 0 and K % tk == 0, (name, M, N, K, tm, tn, tk)
    nk = K // tk
    n_ex, n_out = len(extras), len(out_dtypes)
    assert n_sums == 0 or tn == N

    n_b = N_CHIPS if (b_sh and tb and tk == K) else 1

    def body(a_ref, *rest):
        b_refs, rest = rest[:n_b], rest[n_b:]
        ex_refs, out_refs = rest[:n_ex], rest[n_ex:n_ex + n_out]
        sum_refs = rest[n_ex + n_out:n_ex + n_out + n_sums]
        first_rows = pl.program_id(0) == 0
        dims = ((((0,) if ta else (1,)), ((1,) if tb else (0,))), ((), ()))
        if n_b == 1:
            part = lax.dot_general(a_ref[...].astype(BF16), b_refs[0][...].astype(BF16), dims,
                                   preferred_element_type=F32)
        else:
            part = sum(lax.dot_general(a_ref[:, s * nq:(s + 1) * nq].astype(BF16), b_refs[s][...].astype(BF16), dims,
                                       preferred_element_type=F32) for s in range(n_b))

        def finish(acc):
            res = (acc,) if epilogue is None else epilogue(acc, *[r[...] for r in ex_refs])
            if n_sums:
                res, sums = res

                @pl.when(first_rows)
                def _():
                    for r in sum_refs:
                        r[...] = jnp.zeros(r.shape, F32)

                for r, v in zip(sum_refs, sums):
                    r[...] += jnp.broadcast_to(v, r.shape)
            for r, v in zip(out_refs, res):
                r[...] = v.astype(r.dtype)

        if nk == 1:
            finish(part)
        else:
            acc_ref = rest[-1]
            k = pl.program_id(2)

            @pl.when(k == 0)
            def _():
                acc_ref[...] = part

            @pl.when(k > 0)
            def _():
                acc_ref[...] += part

            @pl.when(k == nk - 1)
            def _():
                finish(acc_ref[...])

    if ta:
        a_spec = pl.BlockSpec((tk, tm), lambda i, j, k: (k + a_off[0], i + a_off[1]))
    else:
        a_spec = pl.BlockSpec((tm, tk), lambda i, j, k: (i + a_off[0], k + a_off[1]))
    once = dict(pipeline_mode=pl.Buffered(1)) if (tn == N and nk == 1) else {}
    if n_b > 1:
        b_specs = [pl.BlockSpec((None, tn, nq), functools.partial(lambda i, j, k, s: (s, j, 0), s=s), **once)
                   for s in range(n_b)]
    elif b_sh and tb:
        assert nq % tk == 0
        per = nq // tk
        b_spec = pl.BlockSpec((None, tn, tk), lambda i, j, k: (k // per, j, k % per), **once)
    elif b_sh:
        assert nq % tn == 0
        per = nq // tn
        b_spec = pl.BlockSpec((None, tk, tn), lambda i, j, k: (j // per, k, j % per), **once)
    elif tb:
        b_spec = pl.BlockSpec((tn, tk), lambda i, j, k: (j, k), **once)
    else:
        b_spec = pl.BlockSpec((tk, tn), lambda i, j, k: (k, j), **once)
    if n_b == 1:
        b_specs = [b_spec]
    ex_specs = []
    for arr, kind in extras:
        if kind == 'mn':
            ex_specs.append(pl.BlockSpec((tm, tn), lambda i, j, k: (i, j)))
        elif kind == 'n':
            ex_specs.append(pl.BlockSpec((1, tn), lambda i, j, k: (0, j)))
        else:
            ex_specs.append(pl.BlockSpec(arr.shape, lambda i, j, k: (0, 0)))
    if out_sh:
        assert (N // N_CHIPS) % tn == 0
        per_o = N // N_CHIPS // tn
        o_spec = pl.BlockSpec((None, tm, tn), lambda i, j, k: (j // per_o, i, j % per_o))
        o_shape = (N_CHIPS, M, N // N_CHIPS)
    else:
        o_spec = pl.BlockSpec((tm, tn), lambda i, j, k: (i, j))
        o_shape = (M, N)
    outs = pl.pallas_call(
        body, name=name, grid=(M // tm, N // tn, nk),
        in_specs=[a_spec] + b_specs + ex_specs,
        out_specs=[o_spec for _ in out_dtypes] + [pl.BlockSpec((8, N), lambda i, j, k: (0, 0))] * n_sums,
        out_shape=[_out(o_shape, d) for d in out_dtypes] + [_out((8, N), F32)] * n_sums,
        scratch_shapes=[pltpu.VMEM((tm, tn), F32)] if nk > 1 else [],
        compiler_params=_cparams(("arbitrary" if n_sums else "parallel", "parallel", "arbitrary")),
    )(a, *[b] * n_b, *[e[0] for e in extras])
    if n_sums:
        return tuple(outs[:n_out]), tuple(outs[n_out:])
    return outs[0] if n_out == 1 else tuple(outs)


def _rowwise(fn, *, name, rows, pars=(), outs=(), accs=(), tm=256):
    S = rows[0][0].shape[0]
    tm = min(tm, S)
    assert S % tm == 0
    n_r, n_p, n_o, n_a = len(rows), len(pars), len(outs), len(accs)

    def body(*refs):
        r_refs, p_refs = refs[:n_r], refs[n_r:n_r + n_p]
        o_refs, a_refs = refs[n_r + n_p:n_r + n_p + n_o], refs[n_r + n_p + n_o:]
        o_vals, a_vals = fn([r[...] for r in r_refs], [p[...] for p in p_refs])
        for r, v in zip(o_refs, o_vals):
            r[...] = v.astype(r.dtype)
        if n_a:
            i = pl.program_id(0)

            @pl.when(i == 0)
            def _():
                for r in a_refs:
                    r[...] = jnp.zeros(r.shape, r.dtype)

            for r, v in zip(a_refs, a_vals):
                r[...] += jnp.broadcast_to(v, r.shape)

    in_specs = [pl.BlockSpec((tm, w), functools.partial(lambda i, o: (i, o), o=off)) for _, w, off in rows]
    in_specs += [pl.BlockSpec(p.shape, functools.partial(lambda i, nd: (0,) * nd, nd=p.ndim)) for p in pars]
    out_specs = [pl.BlockSpec((tm, w), lambda i: (i, 0)) for w, _ in outs]
    out_specs += [pl.BlockSpec((8, w), lambda i: (0, 0)) for w in accs]
    out_shape = [_out((S, w), d) for w, d in outs]
    out_shape += [_out((8, w), F32) for w in accs]
    res = pl.pallas_call(
        body, name=name, grid=(S // tm,), in_specs=in_specs, out_specs=out_specs, out_shape=out_shape,
        compiler_params=_cparams(("arbitrary",)),
    )(*[r[0] for r in rows], *pars)
    return tuple(res)


def _colsum(v):
    return jnp.sum(v, axis=0, keepdims=True)


def _ln_stats(v):
    mu = jnp.mean(v, axis=-1, keepdims=True)
    d = v - mu
    var = jnp.mean(d * d, axis=-1, keepdims=True)
    rstd = lax.rsqrt(var + LN_EPS)
    return d * rstd, rstd


def _ln_fwd_epilogue(h, x, g, b, *unused):
    v = ALPHA * x + h
    xhat, _ = _ln_stats(v)
    y = xhat * g + b
    return y, y, v


def _ln_bwd_epilogue(scale):
    def epilogue(acc, resid, v, g, *unused):
        dy = acc + scale * resid
        xhat, rstd = _ln_stats(v)
        dxh = dy * g
        m1 = jnp.mean(dxh, axis=-1, keepdims=True)
        m2 = jnp.mean(dxh * xhat, axis=-1, keepdims=True)
        dv = rstd * (dxh - m1 - xhat * m2)
        return (dv, dv), (_colsum(dy * xhat), _colsum(dy))
    return epilogue


def _ple_gate_grads(dx3, z, pp):
    s = jax.nn.sigmoid(z)
    return dx3 * s, dx3 * pp * s * (1.0 - s)


def _loss_head(y, t, z, pp):
    def fn(r, p):
        d = r[0] - r[1]
        dy = d * (1.0 / D_MODEL)
        return [dy, *_ple_gate_grads(dy, r[2], r[3])], [_colsum(d * d) * (0.5 / D_MODEL)]
    return _rowwise(fn, name="loss_head", rows=[(a, D_MODEL, 0) for a in (y, t, z, pp)],
                    outs=[(D_MODEL, F32), (D_MODEL, BF16), (D_MODEL, BF16)], accs=[D_MODEL])


def _input_grad_epilogue(acc, dv, *below):
    dx = acc + ALPHA * dv
    return (dx, *_ple_gate_grads(dx, *below)) if below else (dx,)


N_LEVELS = 6
GLA_STEP = 4


def _gla_consts():
    C = CHUNK
    A = np.zeros((N_LEVELS + 3, C, C), np.float32)
    masks = np.zeros((N_LEVELS + 1, C, C), np.float32)
    r = np.arange(C)[:, None]
    u = np.arange(C)[None, :]
    for l in range(N_LEVELS):
        half = C >> (l + 1)
        mid = (r // (2 * half)) * (2 * half) + half - 1
        A[l] = np.where(r > mid, (u > mid) & (u <= r), (u > r) & (u <= mid))
        masks[l] = ((r // (2 * half)) == (u // (2 * half))) & (((r // half) % 2) != ((u // half) % 2))
    masks[N_LEVELS] = (r == u)
    A[N_LEVELS] = (u <= r)
    A[N_LEVELS + 1] = (u > r)
    A[N_LEVELS + 2] = 1.0
    A = A.reshape(-1, C)
    return A, np.ascontiguousarray(A.T), masks


def _split3(v):
    hi = v.astype(BF16)
    r1 = v - hi.astype(F32)
    mid = r1.astype(BF16)
    lo = (r1 - mid.astype(F32)).astype(BF16)
    return hi, mid, lo


def _dot_exact01(a01, v):
    hi, mid, lo = _split3(v)
    f = lambda p: jnp.dot(a01, p, preferred_element_type=F32)
    return f(hi) + f(mid) + f(lo)


def _nt(a, b):
    return lax.dot_general(a, b, (((1,), (1,)), ((), ())), preferred_element_type=F32)


def _tn(a, b):
    return lax.dot_general(a, b, (((0,), (0,)), ((), ())), preferred_element_type=F32)


def _nn(a, b):
    return jnp.dot(a, b, preferred_element_type=F32)


def _gla_chunk_terms(q, k, E, m_ref):
    C = CHUNK
    scores = m_ref[N_LEVELS] * _nt(q.astype(BF16), k.astype(BF16))
    qes, kes = [], []
    for l in range(N_LEVELS):
        El = E[l * C:(l + 1) * C]
        qe, ke = (q * El).astype(BF16), (k * El).astype(BF16)
        qes.append(qe)
        kes.append(ke)
        scores = scores + m_ref[l] * _nt(qe, ke)
    return qes, kes, scores


def _head(v, h, w):
    return v[:, h * w:(h + 1) * w]


def _gla_fwd(pin, la):
    S = pin.shape[0]
    NC = S // CHUNK
    C, R = CHUNK, CHUNK * GLA_STEP
    A, _, masks = _gla_consts()

    def body(q_ref, k_ref, v_ref, la_ref, a_ref, m_ref, o_ref, st_ref, state):
        @pl.when(pl.program_id(0) == 0)
        def _():
            state[...] = jnp.zeros(state.shape, F32)

        for ci in range(GLA_STEP):
            rows = pl.ds(ci * C, C)
            E_all = jnp.exp(_dot_exact01(a_ref[...], la_ref[rows, :]))
            q_all = q_ref[rows, :] * (GLA_DK ** -0.5)
            k_all, v_all = k_ref[rows, :], v_ref[rows, :]
            outs = []
            for h in range(GLA_HEADS):
                q, k, E = _head(q_all, h, GLA_DK), _head(k_all, h, GLA_DK), _head(E_all, h, GLA_DK)
                _, _, scores = _gla_chunk_terms(q, k, E, m_ref)
                Eq, Ek, Ee = E[6 * C:7 * C], E[7 * C:8 * C], E[8 * C:9 * C]
                st = state[h]
                st_ref[h, ci] = st
                vb = _head(v_all, h, GLA_DV).astype(BF16)
                outs.append(_nn(scores.astype(BF16), vb) + _nt((q * Eq).astype(BF16), st.astype(BF16)))
                state[h] = st * jnp.concatenate([Ee] * (GLA_DV // C), axis=0) + _tn(vb, (k * Ek).astype(BF16))
            o_ref[rows, :] = jnp.concatenate(outs, axis=1)

    return pl.pallas_call(
        body, name="gla_fwd", grid=(NC // GLA_STEP,),
        in_specs=[pl.BlockSpec((R, GLA_HK), lambda c: (c, 0)),
                  pl.BlockSpec((R, GLA_HK), lambda c: (c, 1)),
                  pl.BlockSpec((R, GLA_HV), lambda c: (c, 2 * GLA_HK // GLA_HV)),
                  pl.BlockSpec((R, GLA_HK), lambda c: (c, 0)),
                  pl.BlockSpec(A.shape, lambda c: (0, 0)),
                  pl.BlockSpec(masks.shape, lambda c: (0, 0, 0))],
        out_specs=[pl.BlockSpec((R, GLA_HV), lambda c: (c, 0)),
                   pl.BlockSpec((GLA_HEADS, GLA_STEP, GLA_DV, GLA_DK), lambda c: (0, c, 0, 0))],
        out_shape=[_out((S, GLA_HV), F32), _out((GLA_HEADS, NC, GLA_DV, GLA_DK), F32)],
        scratch_shapes=[pltpu.VMEM((GLA_HEADS, GLA_DV, GLA_DK), F32)],
        compiler_params=_cparams(("arbitrary",)),
    )(pin, pin, pin, la, jnp.asarray(A, BF16), jnp.asarray(masks))


def _gla_bwd(pin, la, states, do):
    S = pin.shape[0]
    NC = S // CHUNK
    C, R = CHUNK, CHUNK * GLA_STEP
    A, AT, masks = _gla_consts()
    scale = GLA_DK ** -0.5

    def body(q_ref, k_ref, v_ref, la_ref, st_ref, do_ref, a_ref, at_ref, m_ref,
             dq_ref, dk_ref, dv_ref, dla_ref, dstate):
        @pl.when(pl.program_id(0) == 0)
        def _():
            dstate[...] = jnp.zeros(dstate.shape, F32)

        for ci in reversed(range(GLA_STEP)):
            one_chunk(ci, pl.ds(ci * C, C), q_ref, k_ref, v_ref, la_ref, st_ref, do_ref, a_ref, at_ref, m_ref,
                      dq_ref, dk_ref, dv_ref, dla_ref, dstate)

    def one_chunk(ci, rows, q_ref, k_ref, v_ref, la_ref, st_ref, do_ref, a_ref, at_ref, m_ref,
                  dq_ref, dk_ref, dv_ref, dla_ref, dstate):
        E_all = jnp.exp(_dot_exact01(a_ref[...], la_ref[rows, :]))
        q_all = q_ref[rows, :] * scale
        k_all, v_all, do_all = k_ref[rows, :], v_ref[rows, :], do_ref[rows, :]
        dqs, dks, dvs, dXs = [], [], [], []
        for h in range(GLA_HEADS):
            q, k, E = _head(q_all, h, GLA_DK), _head(k_all, h, GLA_DK), _head(E_all, h, GLA_DK)
            qes, kes, scores = _gla_chunk_terms(q, k, E, m_ref)
            Eq, Ek, Ee = E[6 * C:7 * C], E[7 * C:8 * C], E[8 * C:9 * C]
            st, dst = st_ref[h, ci], dstate[h]
            dob, vb = _head(do_all, h, GLA_DV).astype(BF16), _head(v_all, h, GLA_DV).astype(BF16)
            dstb = dst.astype(BF16)
            qEq, kEk = (q * Eq).astype(BF16), (k * Ek).astype(BF16)
            dsc = _nt(dob, vb)
            dvs.append(_tn(scores.astype(BF16), dob) + _nt(kEk, dstb))
            dqEq = _nn(dob, st.astype(BF16))
            dkEk = _nn(vb, dstb)
            Gd = (m_ref[N_LEVELS] * dsc).astype(BF16)
            dq = _nn(Gd, k.astype(BF16)) + dqEq * Eq
            dk = _tn(Gd, q.astype(BF16)) + dkEk * Ek
            dX = []
            for l in range(N_LEVELS):
                El = E[l * C:(l + 1) * C]
                G = (m_ref[l] * dsc).astype(BF16)
                dqe, dke = _nn(G, kes[l]), _tn(G, qes[l])
                dq = dq + dqe * El
                dk = dk + dke * El
                dX.append((dqe * q + dke * k) * El)
            dX.append(dqEq * q * Eq)
            dX.append(dkEk * k * Ek)
            prod = dst * st
            dEe = prod[0:C]
            for i in range(1, GLA_DV // C):
                dEe = dEe + prod[i * C:(i + 1) * C]
            dX.append(dEe * Ee)
            dXs.append(jnp.concatenate(dX, axis=0))
            dqs.append(dq * scale)
            dks.append(dk)
            dstate[h] = dst * jnp.concatenate([Ee] * (GLA_DV // C), axis=0) + _tn(dob, qEq)
        dla_ref[rows, :] = _dot_exact01(at_ref[...], jnp.concatenate(dXs, axis=1))
        dq_ref[rows, :] = jnp.concatenate(dqs, axis=1).astype(dq_ref.dtype)
        dk_ref[rows, :] = jnp.concatenate(dks, axis=1).astype(dk_ref.dtype)
        dv_ref[rows, :] = jnp.concatenate(dvs, axis=1).astype(dv_ref.dtype)

    rc = lambda c: NC // GLA_STEP - 1 - c
    return pl.pallas_call(
        body, name="gla_bwd", grid=(NC // GLA_STEP,),
        in_specs=[pl.BlockSpec((R, GLA_HK), lambda c: (rc(c), 0)),
                  pl.BlockSpec((R, GLA_HK), lambda c: (rc(c), 1)),
                  pl.BlockSpec((R, GLA_HV), lambda c: (rc(c), 2 * GLA_HK // GLA_HV)),
                  pl.BlockSpec((R, GLA_HK), lambda c: (rc(c), 0)),
                  pl.BlockSpec((GLA_HEADS, GLA_STEP, GLA_DV, GLA_DK), lambda c: (0, rc(c), 0, 0)),
                  pl.BlockSpec((R, GLA_HV), lambda c: (rc(c), 0)),
                  pl.BlockSpec(A.shape, lambda c: (0, 0)),
                  pl.BlockSpec(AT.shape, lambda c: (0, 0)),
                  pl.BlockSpec(masks.shape, lambda c: (0, 0, 0))],
        out_specs=[pl.BlockSpec((R, GLA_HK), lambda c: (rc(c), 0)),
                   pl.BlockSpec((R, GLA_HK), lambda c: (rc(c), 0)),
                   pl.BlockSpec((R, GLA_HV), lambda c: (rc(c), 0)),
                   pl.BlockSpec((R, GLA_HK), lambda c: (rc(c), 0))],
        out_shape=[_out((S, GLA_HK), BF16), _out((S, GLA_HK), BF16), _out((S, GLA_HV), BF16),
                   _out((S, GLA_HK), F32)],
        scratch_shapes=[pltpu.VMEM((GLA_HEADS, GLA_DV, GLA_DK), F32)],
        compiler_params=_cparams(("arbitrary",)),
    )(pin, pin, pin, la, states, do, jnp.asarray(A, BF16), jnp.asarray(AT, BF16), jnp.asarray(masks))


def _gla_post_fwd(o, pin, g):
    def fn(r, p):
        ov, rv = r
        ys = []
        for h in range(GLA_HEADS):
            oh = ov[:, h * GLA_DV:(h + 1) * GLA_DV]
            rh = rv[:, h * GLA_DV:(h + 1) * GLA_DV]
            rs = lax.rsqrt(jnp.mean(oh * oh, axis=-1, keepdims=True) + RMS_EPS)
            ys.append(oh * rs * p[0] * (rh * jax.nn.sigmoid(rh)))
        return [jnp.concatenate(ys, axis=1)], []
    return _rowwise(fn, name="gla_post_fwd", rows=[(o, GLA_HV, 0), (pin, GLA_HV, (2 * GLA_HK + GLA_HV) // GLA_HV)],
                    pars=[g], outs=[(GLA_HV, BF16)])[0]


def _gla_post_bwd(dy, o, pin, g):
    def fn(r, p):
        dyv, ov, rv = r
        dos, drs, dg = [], [], 0.0
        for h in range(GLA_HEADS):
            sl = slice(h * GLA_DV, (h + 1) * GLA_DV)
            oh, rh, dyh = ov[:, sl], rv[:, sl], dyv[:, sl]
            rs = lax.rsqrt(jnp.mean(oh * oh, axis=-1, keepdims=True) + RMS_EPS)
            xh = oh * rs
            sg = jax.nn.sigmoid(rh)
            d_on = dyh * (rh * sg)
            drs.append(dyh * (xh * p[0]) * (sg * (1.0 + rh * (1.0 - sg))))
            dg = dg + _colsum(d_on * xh)
            dxh = d_on * p[0]
            dos.append(rs * (dxh - xh * jnp.mean(dxh * xh, axis=-1, keepdims=True)))
        return [jnp.concatenate(dos, axis=1), jnp.concatenate(drs, axis=1)], [dg]
    return _rowwise(fn, name="gla_post_bwd",
                    rows=[(dy, GLA_HV, 0), (o, GLA_HV, 0), (pin, GLA_HV, (2 * GLA_HK + GLA_HV) // GLA_HV)],
                    pars=[g], outs=[(GLA_HV, F32), (GLA_HV, BF16)], accs=[GLA_DV])


def _gla_gate_bwd(dla, la):
    def fn(r, p):
        dz = r[0] * (1.0 / GLA_TAU) * (1.0 - jnp.exp(GLA_TAU * r[1]))
        return [dz], [_colsum(dz)]
    return _rowwise(fn, name="gla_gate_bwd", rows=[(dla, GLA_HK, 0), (la, GLA_HK, 0)], outs=[(GLA_HK, BF16)],
                    accs=[GLA_HK])


def _log_sigmoid(z):
    return jnp.minimum(z, 0.0) - jnp.log(1.0 + jnp.exp(-jnp.abs(z)))


def _rot_half(v):
    lane = lax.broadcasted_iota(jnp.int32, v.shape, 1)
    return jnp.where(lane < 32, -pltpu.roll(v, 96, 1), jnp.where(lane < 64, pltpu.roll(v, 32, 1), 0.0))


def _rms(v):
    rs = lax.rsqrt(jnp.mean(v * v, axis=-1, keepdims=True) + RMS_EPS)
    return v * rs, rs


def _mla_norm_fwd(cin, gq, gkv, cosp, sinp):
    def fn(r, p):
        cv, cs, sn = r
        qn, _ = _rms(cv[:, :MLA_QR])
        kvn, _ = _rms(cv[:, MLA_QR:MLA_QR + MLA_KVR])
        kr = cv[:, MLA_QR + MLA_KVR:]
        return [qn * p[0], kvn * p[1], kr * cs + _rot_half(kr) * sn], []
    return _rowwise(fn, name="mla_norm_fwd", rows=[(cin, MLA_IN_PAD, 0), (cosp, 128, 0), (sinp, 128, 0)],
                    pars=[gq, gkv], outs=[(MLA_QR, BF16), (MLA_KVR, BF16), (128, BF16)])


def _mla_qrope_fwd(q, cosp, sinp):
    scale = (MLA_NOPE + MLA_ROPE) ** -0.5

    def fn(r, p):
        qv, cs, sn = r
        parts = []
        for h in range(MLA_HEADS):
            parts.append(qv[:, h * MLA_QH:h * MLA_QH + 128] * scale)
            rp = qv[:, h * MLA_QH + 128:(h + 1) * MLA_QH]
            parts.append((rp * cs + _rot_half(rp) * sn) * scale)
        return [jnp.concatenate(parts, axis=1)], []
    W = MLA_HEADS * MLA_QH
    return _rowwise(fn, name="mla_qrope_fwd", rows=[(q, W, 0), (cosp, 128, 0), (sinp, 128, 0)],
                    outs=[(W, BF16)])[0]


def _mla_qrope_bwd(dq, cosp, sinp):
    scale = (MLA_NOPE + MLA_ROPE) ** -0.5

    def fn(r, p):
        dv, cs, sn = r
        parts = []
        for h in range(MLA_HEADS):
            parts.append(dv[:, h * MLA_QH:h * MLA_QH + 128] * scale)
            rp = dv[:, h * MLA_QH + 128:(h + 1) * MLA_QH]
            parts.append((rp * cs - _rot_half(rp) * sn) * scale)
        return [jnp.concatenate(parts, axis=1)], []
    W = MLA_HEADS * MLA_QH
    return _rowwise(fn, name="mla_qrope_bwd", rows=[(dq, W, 0), (cosp, 128, 0), (sinp, 128, 0)],
                    outs=[(W, BF16)])[0]


def _mla_norm_bwd(cin, dqn, dkvn, dkr, gq, gkv, cosp, sinp):
    def fn(r, p):
        cv, dq_, dkv_, dkr_, cs, sn = r
        outs, accs = [], []
        for (lo, hi), dn, g in (((0, MLA_QR), dq_, p[0]), ((MLA_QR, MLA_QR + MLA_KVR), dkv_, p[1])):
            xh, rs = _rms(cv[:, lo:hi])
            dxh = dn * g
            outs.append(rs * (dxh - xh * jnp.mean(dxh * xh, axis=-1, keepdims=True)))
            accs.append(_colsum(dn * xh))
        dk = dkr_[:, 0:128]
        for h in range(1, MLA_HEADS):
            dk = dk + dkr_[:, h * 128:(h + 1) * 128]
        outs.append(dk * cs - _rot_half(dk) * sn)
        return [jnp.concatenate(outs, axis=1)], accs
    return _rowwise(fn, name="mla_norm_bwd",
                    rows=[(cin, MLA_IN_PAD, 0), (dqn, MLA_QR, 0), (dkvn, MLA_KVR, 0), (dkr, MLA_HEADS * 128, 0),
                          (cosp, 128, 0), (sinp, 128, 0)],
                    pars=[gq, gkv], outs=[(MLA_IN_PAD, BF16)], accs=[MLA_QR, MLA_KVR])


def _mla_probs(q, k, i, tq):
    s = _nt(q, k)
    row = (i * tq + lax.broadcasted_iota(jnp.int32, s.shape, 0)) // CHUNK
    col = lax.broadcasted_iota(jnp.int32, s.shape, 1) // CHUNK
    s = jnp.where(col <= row, s, -jnp.inf)
    e = jnp.exp(s - jnp.max(s, axis=-1, keepdims=True))
    return e / jnp.sum(e, axis=-1, keepdims=True)


def _mla_attn_fwd(qr, knv, kr, tq=256):
    S = qr.shape[0]
    tq = min(tq, S)

    def body(q_ref, kn_ref, v_ref, kr_ref, o_ref, k_cat):
        k_cat[:, :128] = kn_ref[...]
        k_cat[:, 128:] = kr_ref[...]
        for i in range(S // tq):
            rows, keys = pl.ds(i * tq, tq), pl.ds(0, (i + 1) * tq)
            pr = _mla_probs(q_ref[rows, :], k_cat[keys, :], i, tq)
            o_ref[rows, :] = _nn(pr.astype(BF16), v_ref[keys, :])

    return pl.pallas_call(
        body, name="mla_attn_fwd", grid=(MLA_HEADS,),
        in_specs=[pl.BlockSpec((S, MLA_QH), lambda h: (0, h)),
                  pl.BlockSpec((S, 128), lambda h: (0, h)),
                  pl.BlockSpec((S, 128), lambda h: (0, MLA_HEADS + h)),
                  pl.BlockSpec((S, 128), lambda h: (0, 0))],
        out_specs=pl.BlockSpec((S, 128), lambda h: (0, h)),
        out_shape=_out((S, MLA_HEADS * MLA_V), F32),
        scratch_shapes=[pltpu.VMEM((S, MLA_QH), BF16)],
        compiler_params=_cparams(("parallel",)),
    )(qr, knv, knv, kr)


def _mla_attn_bwd(qr, knv, kr, o, do, tq=256):
    S = qr.shape[0]
    tq = min(tq, S)
    W = MLA_HEADS * 128

    def body(q_ref, kn_ref, v_ref, kr_ref, o_ref, do_ref, dq_ref, dkn_ref, dv_ref, dkr_ref, k_cat, dk_acc, dv_acc):
        k_cat[:, :128] = kn_ref[...]
        k_cat[:, 128:] = kr_ref[...]
        dk_acc[...] = jnp.zeros(dk_acc.shape, F32)
        dv_acc[...] = jnp.zeros(dv_acc.shape, F32)
        for i in range(S // tq):
            rows, keys = pl.ds(i * tq, tq), pl.ds(0, (i + 1) * tq)
            q, k, v = q_ref[rows, :], k_cat[keys, :], v_ref[keys, :]
            pr = _mla_probs(q, k, i, tq)
            dov = do_ref[rows, :]
            delta = jnp.sum(dov * o_ref[rows, :], axis=-1, keepdims=True)
            dob = dov.astype(BF16)
            ds = (pr * (_nt(dob, v) - delta)).astype(BF16)
            dq_ref[rows, :] = _nn(ds, k)
            dk_acc[keys, :] += _tn(ds, q)
            dv_acc[keys, :] += _tn(pr.astype(BF16), dob)
        dkn_ref[...] = dk_acc[:, :128].astype(dkn_ref.dtype)
        dkr_ref[...] = dk_acc[:, 128:]
        dv_ref[...] = dv_acc[...].astype(dv_ref.dtype)

    head = lambda w: pl.BlockSpec((S, w), lambda h: (0, h))
    return pl.pallas_call(
        body, name="mla_attn_bwd", grid=(MLA_HEADS,),
        in_specs=[head(MLA_QH), head(128), pl.BlockSpec((S, 128), lambda h: (0, MLA_HEADS + h)),
                  pl.BlockSpec((S, 128), lambda h: (0, 0)), head(128), head(128)],
        out_specs=[head(MLA_QH), head(128), head(128), head(128)],
        out_shape=[_out((S, MLA_HEADS * MLA_QH), F32), _out((S, W), BF16), _out((S, W), BF16), _out((S, W), F32)],
        scratch_shapes=[pltpu.VMEM((S, MLA_QH), BF16), pltpu.VMEM((S, MLA_QH), F32), pltpu.VMEM((S, 128), F32)],
        compiler_params=_cparams(("parallel",)),
    )(qr, knv, knv, kr, o, do)


CONV_TILE = 256


def _shift_down(v, n):
    row = lax.broadcasted_iota(jnp.int32, v.shape, 0)
    return jnp.where(row >= n, pltpu.roll(v, n, 0), 0.0)


def _shift_up(v, n):
    S = v.shape[0]
    row = lax.broadcasted_iota(jnp.int32, v.shape, 0)
    return jnp.where(row < S - n, pltpu.roll(v, S - n, 0), 0.0)


def _conv_specs(S, n_extra_cols):
    nt = D_MODEL // CONV_TILE
    specs = [pl.BlockSpec((S, CONV_TILE), functools.partial(lambda j, o: (0, o + j), o=part * nt))
             for part in range(3)]
    specs.append(pl.BlockSpec((8, CONV_TILE), lambda j: (0, j)))
    specs += [pl.BlockSpec((S, CONV_TILE), lambda j: (0, j)) for _ in range(n_extra_cols)]
    return specs


def _conv_fwd(bcu, w8):
    S = bcu.shape[0]

    def body(b_ref, c_ref, u_ref, w_ref, y_ref):
        cu = c_ref[...] * u_ref[...]
        z = w_ref[2:3, :] * cu + w_ref[1:2, :] * _shift_down(cu, 1) + w_ref[0:1, :] * _shift_down(cu, 2)
        y_ref[...] = (b_ref[...] * z).astype(y_ref.dtype)

    return pl.pallas_call(
        body, name="conv_fwd", grid=(D_MODEL // CONV_TILE,), in_specs=_conv_specs(S, 0),
        out_specs=pl.BlockSpec((S, CONV_TILE), lambda j: (0, j)),
        out_shape=_out((S, D_MODEL), BF16),
        compiler_params=_cparams(("parallel",)),
    )(bcu, bcu, bcu, w8)


def _conv_bwd(bcu, w8, dy):
    S = bcu.shape[0]

    def body(b_ref, c_ref, u_ref, w_ref, dy_ref, db_ref, dc_ref, du_ref, dw_ref):
        b, c, u, dyv = b_ref[...], c_ref[...], u_ref[...], dy_ref[...]
        w0, w1, w2 = w_ref[0:1, :], w_ref[1:2, :], w_ref[2:3, :]
        cu = c * u
        cu1, cu2 = _shift_down(cu, 1), _shift_down(cu, 2)
        z = w2 * cu + w1 * cu1 + w0 * cu2
        dz = dyv * b
        db_ref[...] = (dyv * z).astype(db_ref.dtype)
        dcu = w2 * dz + w1 * _shift_up(dz, 1) + w0 * _shift_up(dz, 2)
        dc_ref[...] = (dcu * u).astype(dc_ref.dtype)
        du_ref[...] = (dcu * c).astype(du_ref.dtype)
        dw_ref[...] = jnp.zeros(dw_ref.shape, F32)
        dw_ref[0:1, :] = _colsum(dz * cu2)
        dw_ref[1:2, :] = _colsum(dz * cu1)
        dw_ref[2:3, :] = _colsum(dz * cu)

    col = pl.BlockSpec((S, CONV_TILE), lambda j: (0, j))
    return pl.pallas_call(
        body, name="conv_bwd", grid=(D_MODEL // CONV_TILE,), in_specs=_conv_specs(S, 1),
        out_specs=[col, col, col, pl.BlockSpec((8, CONV_TILE), lambda j: (0, j))],
        out_shape=[_out((S, D_MODEL), BF16)] * 3 + [_out((8, D_MODEL), F32)],
        compiler_params=_cparams(("parallel",)),
    )(bcu, bcu, bcu, w8, dy)


def _adamw_update(w, g, m, v):
    nm = ADAM_B1 * m + (1.0 - ADAM_B1) * g
    nv = ADAM_B2 * v + (1.0 - ADAM_B2) * jnp.square(g)
    m_hat = nm / (1.0 - ADAM_B1 ** ADAM_STEP)
    v_hat = nv / (1.0 - ADAM_B2 ** ADAM_STEP)
    return -ADAM_LR * (m_hat / (jnp.sqrt(v_hat) + ADAM_EPS) + ADAM_WD * w), nm, nv


def _adamw_shard_major(w, m, v, gs, name):
    view = lambda a: jnp.transpose(a, (2, 0, 1))
    g = jnp.stack([x.T for x in gs], axis=1)
    n, L, k = g.shape
    rows = n // 4
    assert n % 4 == 0

    def body(w_ref, m_ref, v_ref, g_ref, go_ref, d_ref, nm_ref, nv_ref):
        gv = g_ref[...]
        d_ref[...], nm_ref[...], nv_ref[...] = _adamw_update(w_ref[...], gv, m_ref[...], v_ref[...])
        go_ref[...] = gv

    spec = pl.BlockSpec((rows, L, k), lambda i: (i, 0, 0))
    outs = pl.pallas_call(
        body, name=name, grid=(4,), in_specs=[spec] * 4, out_specs=[spec] * 4,
        out_shape=[jax.ShapeDtypeStruct((n, L, k), F32)] * 4,
        compiler_params=_cparams(("parallel",)),
    )(view(w), view(m), view(v), g)
    return [jnp.transpose(o, (1, 2, 0)) for o in outs]


def _adamw_small(ws, gs, ms, vs):
    n = len(ws)

    def body(*refs):
        ins, outs = refs[:4 * n], refs[4 * n:]
        for t in range(n):
            w_ref, g_ref, m_ref, v_ref = (ins[k * n + t] for k in range(4))
            gv = g_ref[...]
            outs[4 * t][...] = gv
            outs[4 * t + 1][...], outs[4 * t + 2][...], outs[4 * t + 3][...] = _adamw_update(
                w_ref[...], gv, m_ref[...], v_ref[...])

    return pl.pallas_call(
        body, name="adamw_small",
        out_shape=[jax.ShapeDtypeStruct(a.shape, F32) for a in ws for _ in range(4)],
    )(*ws, *gs, *ms, *vs)


def _adamw(w, m, v, gs, name):
    L, R, Cn = w.shape
    assert len(gs) == L
    tr = R if R <= 256 else 256
    assert R % tr == 0

    def body(w_ref, m_ref, v_ref, *rest):
        g_refs, (go_ref, d_ref, nm_ref, nv_ref) = rest[:L], rest[L:]
        layer = pl.program_id(0)
        gv = g_refs[0][...]
        for k in range(1, L):
            gv = jnp.where(layer == k, g_refs[k][...], gv)
        d_ref[...], nm_ref[...], nv_ref[...] = _adamw_update(w_ref[...], gv, m_ref[...], v_ref[...])
        go_ref[...] = gv

    spec = pl.BlockSpec((None, tr, Cn), lambda l, i: (l, i, 0))
    g_specs = [pl.BlockSpec((tr, Cn), functools.partial(lambda l, i, k: (jnp.where(l == k, i, 0), 0), k=k))
               for k in range(L)]
    return pl.pallas_call(
        body, name=name, grid=(L, R // tr), in_specs=[spec] * 3 + g_specs, out_specs=[spec] * 4,
        out_shape=[jax.ShapeDtypeStruct((L, R, Cn), F32)] * 4,
        compiler_params=_cparams(("arbitrary", "arbitrary")),
    )(w, m, v, *gs)


HBM_SPEC = pl.BlockSpec(memory_space=pltpu.HBM)


def _place():
    return lax.axis_index("x"), lax.axis_index("y"), lax.axis_index("c")


def _other_chips(x, y):
    return [(1 - x, y), (x, 1 - y), (1 - x, 1 - y)]


SEM_SPEC = pl.BlockSpec(memory_space=pltpu.SEMAPHORE)
ANY_SPEC = pl.BlockSpec(memory_space=pl.ANY)
VMEM_SPEC = pl.BlockSpec(memory_space=pltpu.VMEM)
EFFECT = pltpu.SideEffectType.DATAFLOW_SIDE_EFFECTING
TOKEN = (8, 128)


def _ici_start(srcs, lands, after, copies, name, per_src=3):
    n, nl = len(srcs), len(lands)

    def body(*refs):
        src_refs, land_refs = refs[:n], refs[n:n + nl]
        send_sems, recv_sems, token = refs[n + nl + 1], refs[n + nl + 2], refs[-1]
        x, y, c = _place()
        for k, src, dst, to in copies(src_refs, land_refs, x, y, c):
            pltpu.make_async_remote_copy(src_ref=src, dst_ref=dst, send_sem=send_sems.at[k], recv_sem=recv_sems.at[k],
                                         device_id=to, device_id_type=MESH).start()
        token[...] = jnp.zeros(TOKEN, F32)

    n_copies = per_src * max(n, nl if n == 0 else 0)
    res = pl.pallas_call(
        body, name=name,
        out_shape=(pltpu.SemaphoreType.DMA((n_copies,)), pltpu.SemaphoreType.DMA((n_copies,)),
                   *[pltpu.HBM(s.shape, s.dtype) for s in srcs], *[pltpu.HBM(l.shape, l.dtype) for l in lands],
                   jax.ShapeDtypeStruct(TOKEN, F32)),
        in_specs=[HBM_SPEC] * (n + nl) + [ANY_SPEC],
        out_specs=(SEM_SPEC, SEM_SPEC, *[HBM_SPEC] * (n + nl), VMEM_SPEC),
        input_output_aliases={t: 2 + t for t in range(n + nl)},
        compiler_params=pltpu.CompilerParams(has_side_effects=EFFECT),
    )(*[_hbm(s) for s in srcs], *[_hbm(l) for l in lands], after)
    return res[0], res[1], list(res[2:2 + n]), list(res[2 + n:2 + n + nl]), res[-1]


def _ici_wait(handle, after, copies, name):
    send_sems, recv_sems, srcs, lands, _ = handle
    n, nl = len(srcs), len(lands)

    def body(*refs):
        src_refs, land_refs = refs[:n], refs[n:n + nl]
        send_s, recv_s = refs[n + nl], refs[n + nl + 1]
        x, y, c = _place()
        for k, src, dst, to in copies(src_refs, land_refs, x, y, c):
            cp = pltpu.make_async_remote_copy(src_ref=src, dst_ref=dst, send_sem=send_s.at[k], recv_sem=recv_s.at[k],
                                              device_id=to, device_id_type=MESH)
            cp.wait_send()
            cp.wait_recv()

    res = pl.pallas_call(
        body, name=name,
        out_shape=(*[pltpu.HBM(s.shape, s.dtype) for s in srcs], *[pltpu.HBM(l.shape, l.dtype) for l in lands]),
        in_specs=[HBM_SPEC] * (n + nl) + [SEM_SPEC, SEM_SPEC, ANY_SPEC],
        out_specs=tuple([HBM_SPEC] * (n + nl)),
        input_output_aliases={t: t for t in range(n + nl)},
        compiler_params=pltpu.CompilerParams(has_side_effects=EFFECT),
    )(*srcs, *lands, send_sems, recv_sems, after)
    return list(res[:n]), list(res[n:])


def _gather_copies(halves, arriving):
    def copies(src_refs, land_refs, x, y, c):
        q = 2 * x + y
        out = []
        for t, H in enumerate(halves):
            mine = land_refs[t].at[q, pl.ds(c * H, H), :]
            for j, (cx, cy) in enumerate(_other_chips(x, y)):
                theirs = land_refs[t].at[2 * cx + cy, pl.ds(c * H, H), :]
                out.append((3 * t + j, mine, theirs if arriving else mine, (cx, cy, c)))
        return out
    return copies


def _place_own(ops, after, name):
    n = len(ops)
    kinds = sorted({(o.shape, str(o.dtype)) for o in ops})
    kind_of = [kinds.index((o.shape, str(o.dtype))) for o in ops]

    def body(*refs):
        in_refs, out_refs = refs[:n], refs[n + 1:2 * n + 1]
        rd_sems, wr_sems, bufs = refs[2 * n + 1], refs[2 * n + 2], refs[2 * n + 3:]
        x, y, _ = _place()
        used = [0] * len(kinds)
        slot, busy = [], {}
        for t in range(n):
            slot.append((kind_of[t], used[kind_of[t]] % 2))
            used[kind_of[t]] += 1
        rd = lambda t: pltpu.make_async_copy(in_refs[t], bufs[slot[t][0]].at[slot[t][1]], rd_sems.at[t])
        wr = lambda t: pltpu.make_async_copy(bufs[slot[t][0]].at[slot[t][1]], out_refs[t].at[2 * x + y],
                                             wr_sems.at[t])
        rd(0).start()
        for t in range(n):
            rd(t).wait()
            wr(t).start()
            busy[slot[t]] = t
            if t + 1 < n:
                if slot[t + 1] in busy:
                    wr(busy.pop(slot[t + 1])).wait()
                rd(t + 1).start()
        for t in busy.values():
            wr(t).wait()

    return pl.pallas_call(
        body, name=name, in_specs=[HBM_SPEC] * n + [ANY_SPEC], out_specs=[HBM_SPEC] * n,
        out_shape=[jax.ShapeDtypeStruct((N_CHIPS,) + o.shape, o.dtype) for o in ops],
        scratch_shapes=[pltpu.SemaphoreType.DMA((n,)), pltpu.SemaphoreType.DMA((n,))]
        + [pltpu.VMEM((2,) + shape, jnp.dtype(dt)) for shape, dt in kinds],
        compiler_params=pltpu.CompilerParams(vmem_limit_bytes=VMEM_LIMIT),
    )(*ops, after)


def _gather_start(lands, after, name):
    return _ici_start([], lands, after, _gather_copies([l.shape[1] // 2 for l in lands], False), name)


def _gather_wait(handle, after, name):
    halves = [l.shape[1] // 2 for l in handle[3]]
    return _ici_wait(handle, after, _gather_copies(halves, True), name)


def _forward_copies(halves, arriving):
    def copies(src_refs, land_refs, x, y, c):
        out = []
        for t, H in enumerate(halves):
            for j, (cx, cy) in enumerate(_other_chips(x, y)):
                mine = land_refs[t].at[2 * cx + cy, pl.ds(c * H, H), :]
                theirs = land_refs[t].at[2 * cx + cy, pl.ds((1 - c) * H, H), :]
                out.append((3 * t + j, mine, theirs if arriving else mine, (x, y, 1 - c)))
        return out
    return copies


def _forward_start(lands, after, name):
    halves = [l.shape[1] // 2 for l in lands]
    return _ici_start([], lands, after, _forward_copies(halves, False), name)


def _forward_wait(handle, after, name):
    halves = [l.shape[1] // 2 for l in handle[3]]
    return _ici_wait(handle, after, _forward_copies(halves, True), name)[1]


def _swap_halves(ops, name):
    n = len(ops)

    def body(*refs):
        in_refs, out_refs, send_sems, recv_sems = refs[:n], refs[n:2 * n], refs[2 * n], refs[2 * n + 1]
        x, y, c = _place()
        cps = []
        for t in range(n):
            H = ops[t].shape[1] // 2
            cp = pltpu.make_async_remote_copy(src_ref=in_refs[t].at[:, pl.ds((1 - c) * H, H), :],
                                              dst_ref=out_refs[t], send_sem=send_sems.at[t],
                                              recv_sem=recv_sems.at[t], device_id=(x, y, 1 - c),
                                              device_id_type=MESH)
            cp.start()
            cps.append(cp)
        for cp in cps:
            cp.wait()

    return pl.pallas_call(
        body, name=name, in_specs=[HBM_SPEC] * n, out_specs=[HBM_SPEC] * n,
        out_shape=[jax.ShapeDtypeStruct((N_CHIPS, o.shape[1] // 2, o.shape[2]), o.dtype) for o in ops],
        scratch_shapes=[pltpu.SemaphoreType.DMA((n,)), pltpu.SemaphoreType.DMA((n,))],
    )(*ops)


def _sum_rows_tile(h):
    return h if h <= 512 else 512


def _pair_sum(g, t, cq, name):
    _, a, b = g.shape
    H = a // 2
    tr = _sum_rows_tile(H)

    def body(cq_ref, g_ref, t_ref, o_ref):
        o_ref[...] = (g_ref[...].astype(F32) + t_ref[...].astype(F32)).astype(o_ref.dtype)

    grid_spec = pltpu.PrefetchScalarGridSpec(
        num_scalar_prefetch=1, grid=(N_CHIPS, H // tr),
        in_specs=[pl.BlockSpec((None, None, tr, b), lambda j, i, cq_ref: (j, cq_ref[0], i, 0)),
                  pl.BlockSpec((None, tr, b), lambda j, i, cq_ref: (j, i, 0))],
        out_specs=pl.BlockSpec((None, tr, b), lambda j, i, cq_ref: (j, i, 0)))
    return pl.pallas_call(
        body, name=name, grid_spec=grid_spec, out_shape=_out(t.shape, BF16),
        compiler_params=_cparams(("parallel", "parallel")),
    )(cq, g.reshape(N_CHIPS, 2, H, b), t)


def _scatter_copies(src_refs, land_refs, x, y, c):
    out = []
    for j, (cx, cy) in enumerate(_other_chips(x, y)):
        for t in range(len(src_refs)):
            out.append((3 * t + j, src_refs[t].at[2 * cx + cy], land_refs[t].at[j], (cx, cy, c)))
    return out


def _scatter_start(ops, after, name):
    lands = [lax.empty((3,) + o.shape[1:], o.dtype) for o in ops]
    return _ici_start(ops, lands, after, _scatter_copies, name)


def _scatter_wait(handle, after, name):
    return _ici_wait(handle, after, _scatter_copies, name)


def _chip_sum(p, t, cq, name):
    _, H, b = p.shape
    tr = _sum_rows_tile(H)

    def body(cq_ref, p_ref, t_ref, o_ref):
        acc = p_ref[...].astype(F32)
        for j in range(3):
            acc = acc + t_ref[j].astype(F32)
        o_ref[...] = acc

    grid_spec = pltpu.PrefetchScalarGridSpec(
        num_scalar_prefetch=1, grid=(H // tr,),
        in_specs=[pl.BlockSpec((None, tr, b), lambda i, cq_ref: (cq_ref[1], i, 0)),
                  pl.BlockSpec((3, tr, b), lambda i, cq_ref: (0, i, 0))],
        out_specs=pl.BlockSpec((None, tr, b), lambda i, cq_ref: (cq_ref[0], i, 0)))
    out = pl.pallas_call(
        body, name=name, grid_spec=grid_spec, out_shape=_out((2, H, b), F32),
        compiler_params=_cparams(("parallel",)),
    )(cq, p, t)
    return out.reshape(2 * H, b)


def _join_copies(arriving):
    def copies(src_refs, land_refs, x, y, c):
        out = []
        for t, land in enumerate(land_refs):
            H = land.shape[0] // 2
            mine, theirs = land.at[pl.ds(c * H, H), :], land.at[pl.ds((1 - c) * H, H), :]
            out.append((t, mine, theirs if arriving else mine, (x, y, 1 - c)))
        return out
    return copies


def _join_start(fs, name):
    return _ici_start([], fs, jnp.zeros(TOKEN, F32), _join_copies(False), name, per_src=1)


def _join_wait(handle, after, name):
    return _ici_wait(handle, after, _join_copies(True), name)[1]


def _direct_copies(src_refs, land_refs, x, y, c):
    out = []
    for t in range(len(src_refs)):
        H = src_refs[t].shape[1] // 2
        for k in range(1, 8):
            px, py, pc = x ^ (k >> 2), y ^ ((k >> 1) & 1), c ^ (k & 1)
            out.append((7 * t + k - 1, src_refs[t].at[2 * px + py, pl.ds(pc * H, H), :], land_refs[t].at[k - 1],
                        (px, py, pc)))
    return out


def _direct_sum(g, t, cq, name):
    _, a, b = g.shape
    H = a // 2
    tr = _sum_rows_tile(H)

    def body(cq_ref, g_ref, t_ref, o_ref):
        acc = g_ref[...].astype(F32)
        for k in range(7):
            acc = acc + t_ref[k].astype(F32)
        o_ref[...] = acc

    grid_spec = pltpu.PrefetchScalarGridSpec(
        num_scalar_prefetch=1, grid=(H // tr,),
        in_specs=[pl.BlockSpec((None, None, tr, b), lambda i, cq_ref: (cq_ref[1], cq_ref[0], i, 0)),
                  pl.BlockSpec((7, tr, b), lambda i, cq_ref: (0, i, 0))],
        out_specs=pl.BlockSpec((None, tr, b), lambda i, cq_ref: (cq_ref[0], i, 0)))
    out = pl.pallas_call(
        body, name=name, grid_spec=grid_spec, out_shape=_out((2, H, b), F32),
        compiler_params=_cparams(("parallel",)),
    )(cq, g.reshape(N_CHIPS, 2, H, b), t)
    return out.reshape(a, b)


def _reduce_direct_start(gs, tag):
    lands = [lax.empty((7, g.shape[1] // 2, g.shape[2]), g.dtype) for g in gs]
    return _ici_start(gs, lands, jnp.zeros(TOKEN, F32), _direct_copies, "rs_direct_start_" + tag, per_src=7)


def _reduce_direct_finish(handle, cq, after, tag):
    gs, rs = _ici_wait(handle, after, _direct_copies, "rs_direct_wait_" + tag)
    fs = [_direct_sum(g, r, cq, "rs_direct_sum") for g, r in zip(gs, rs)]
    return _join_start(fs, "rs_join_start_" + tag)


def _reduce_scatter_start(gs, cq, after, tag):
    ts = _swap_halves(gs, "rs_swap_" + tag)
    ps = [_pair_sum(g, t, cq, "rs_pair_sum") for g, t in zip(gs, ts)]
    return _scatter_start(ps, after, "rs_scatter_start_" + tag)


def _reduce_scatter_finish(handle, cq, after, tag):
    ps, rs = _scatter_wait(handle, after, "rs_scatter_wait_" + tag)
    fs = [_chip_sum(p, r, cq, "rs_chip_sum") for p, r in zip(ps, rs)]
    return _join_start(fs, "rs_join_start_" + tag)


def _small_copies(src_refs, land_refs, x, y, c):
    return [(k - 1, src_refs[0], land_refs[0].at[k - 1], (x ^ (k >> 2), y ^ ((k >> 1) & 1), c ^ (k & 1)))
            for k in range(1, 8)]


def _all_reduce_small_start(v, after):
    return _ici_start([v], [lax.empty((7,) + v.shape, F32)], after, _small_copies, "small_start", per_src=7)


def _all_reduce_small_finish(handle, after):
    (v,), (t,) = _ici_wait(handle, after, _small_copies, "small_wait")
    n = v.shape[0]

    def body(v_ref, t_ref, out_ref, buf):
        x, y, c = _place()
        buf[4 * x + 2 * y + c] = v_ref[...]
        for k in range(1, 8):
            buf[4 * (x ^ (k >> 2)) + 2 * (y ^ ((k >> 1) & 1)) + (c ^ (k & 1))] = t_ref[k - 1]
        acc = buf[0]
        for d in range(1, 8):
            acc = acc + buf[d]
        out_ref[...] = acc

    return pl.pallas_call(
        body, name="small_sum", in_specs=[VMEM_SPEC, VMEM_SPEC], out_specs=VMEM_SPEC,
        out_shape=jax.ShapeDtypeStruct((n, 128), F32), scratch_shapes=[pltpu.VMEM((8, n, 128), F32)],
    )(v, t)


SMALL_GATHER = (16, 1024)
SMALL_FULL = sum(_size(_full_shape(n)) for n in SMALL)
SMALL_FULL_ROWS = -(-(SMALL_FULL + 1) // 128 // 8) * 8


def _layer_shards(w, i, q):
    kind, j = MIXER[i % 3], i // 3
    out = {n: w[n][i].astype(BF16) for n in COMMON_BIG}
    if kind == 'gla':
        win = jnp.zeros((GLA_WIN, D_MODEL), F32)
        win = lax.dynamic_update_slice(win, w['gla_w_in'][j].T, ((GLA_SHARD - GLA_WIN_STEP) * q, 0))
        out['gla_w_in'] = win.astype(BF16)
        out['gla_w_out'] = w['gla_w_out'][j].astype(BF16)
    elif kind == 'mla':
        out['mla_w_in'] = jnp.pad(w['mla_w_in'][j], ((0, 0), (0, MLA_IN_PAD - MLA_IN))).astype(BF16)
        for n in ('mla_w_uq', 'mla_w_ukv', 'mla_w_out'):
            out[n] = w[n][j].astype(BF16)
    else:
        out['conv_w_in'] = w['conv_w_in'][j].astype(BF16)
        out['conv_w_out'] = w['conv_w_out'][j].astype(BF16)
    return out


def _rows_joined(g):
    return g.reshape(g.shape[0] * g.shape[1], g.shape[2])


def _cols_joined(g):
    return jnp.moveaxis(g, 0, 1).reshape(g.shape[1], -1)


def _layer_weights(g, i):
    kind = MIXER[i % 3]
    W = {}
    if 'mlp_w1' in g:
        W = {'w1': g['mlp_w1'], 'w2': _rows_joined(g['mlp_w2']), 'gate': _rows_joined(g['ple_w_gate']),
             'proj': g['ple_w_proj']}
    if kind == 'gla' and 'gla_w_out' in g:
        W['w_out'] = _rows_joined(g['gla_w_out'])
    if kind == 'gla' and 'gla_w_in' in g:
        parts = []
        for qq in range(N_CHIPS):
            lo = g['gla_w_in'][qq][:128]
            if qq > 0:
                lo = lo + g['gla_w_in'][qq - 1][GLA_WIN_STEP:]
            parts += [lo, g['gla_w_in'][qq][128:GLA_WIN_STEP]]
        parts.append(g['gla_w_in'][N_CHIPS - 1][GLA_WIN_STEP:])
        W['w_in'] = jnp.concatenate(parts, axis=0)
    elif kind == 'mla':
        W['w_in'] = _rows_joined(g['mla_w_in'])
        uq = _cols_joined(g['mla_w_uq']).reshape(MLA_QR, MLA_HEADS, MLA_NOPE + MLA_ROPE)
        W['w_uq'] = jnp.pad(uq, ((0, 0), (0, 0), (0, MLA_QH - MLA_NOPE - MLA_ROPE))).reshape(MLA_QR, -1)
        ukv = _cols_joined(g['mla_w_ukv']).reshape(MLA_KVR, MLA_HEADS, 2, 128)
        W['w_ukv'] = ukv.transpose(0, 2, 1, 3).reshape(MLA_KVR, -1)
        W['w_out'] = _rows_joined(g['mla_w_out'])
    elif kind == 'conv':
        W['w_in'] = g['conv_w_in']
        W['w_out'] = _rows_joined(g['conv_w_out'])
    return W


def _pack_small_shards(w):
    flat = jnp.concatenate([w[n].reshape(-1) for n in SMALL_SHARDED])
    return jnp.pad(flat, (0, _size(SMALL_GATHER) - flat.shape[0])).reshape(SMALL_GATHER)


def _unpack_small_gathered(g):
    flat, out, off = g.reshape(N_CHIPS, -1), {}, 0
    for n in SMALL_SHARDED:
        shape, ax = WSPEC[n]
        seg = flat[:, off:off + _size(shape)].reshape((N_CHIPS,) + shape)
        out[n] = jnp.moveaxis(seg, 0, ax).reshape(_full_shape(n))
        off += _size(shape)
    return out


def _pack_small(vals, loss):
    flat = jnp.concatenate([vals[n].reshape(-1) for n in SMALL] + [loss.reshape(1)])
    return jnp.pad(flat, (0, SMALL_FULL_ROWS * 128 - flat.shape[0])).reshape(SMALL_FULL_ROWS, 128)


def _unpack_small(packed, q):
    flat = packed.reshape(-1)
    out, off = {}, 0
    for n in SMALL:
        shape, ax = WSPEC[n]
        full = flat[off:off + _size(_full_shape(n))].reshape(_full_shape(n))
        off += _size(_full_shape(n))
        out[n] = full if ax is None else lax.dynamic_slice_in_dim(full, q * shape[ax], shape[ax], axis=ax)
    return out


def _row_shards(dw):
    return dw.reshape(N_CHIPS, dw.shape[0] // N_CHIPS, dw.shape[1])


def _col_shards(dw):
    return jnp.moveaxis(dw.reshape(dw.shape[0], N_CHIPS, -1), 1, 0)


def _row(v):
    return v.reshape(1, -1)


def _layer_fwd(i, xin, xin_b, p_i, W, sm, cosp, sinp, rest=None, mid=None):
    kind, j = MIXER[i % 3], i // 3
    sv = {'xin': xin, 'xin_b': xin_b}
    if kind == 'gla':
        w_up = jnp.pad(sm['gla_w_gate_up'][j].astype(BF16), ((0, 128 - GLA_RANK), (0, 0)))
        pin = _mm(xin_b, W['w_in'], tb=True, name="gla_in", tn=640, tm=FULL_ROWS)
        la = _mm(pin, w_up, name="gla_gate", K=128, tk=128, a_off=(0, (GLA_IN_PAD - 128) // 128), tn=512,
                 extras=[(_row(sm['gla_b_gate'][j]), 'n')],
                 epilogue=lambda acc, b: (_log_sigmoid(acc + b) * (1.0 / GLA_TAU),))
        o, states = _gla_fwd(pin, la)
        yb = _gla_post_fwd(o, pin, _row(sm['gla_norm_g'][j]))
        if rest is not None:
            W = {**W, **rest(yb)}
        mixed = yb
        sv.update(w_up=w_up, pin=pin, la=la, o=o, states=states, yb=yb)
    elif kind == 'mla':
        gq, gkv = sm['mla_q_norm'][j:j + 1], sm['mla_kv_norm'][j:j + 1]
        cin = _mm(xin_b, W['w_in'], name="mla_in", tn=640, tm=FULL_ROWS)
        qn, kvn, kr = _mla_norm_fwd(cin, gq, gkv, cosp, sinp)
        qr = _mla_qrope_fwd(_mm(qn, W['w_uq'], name="mla_uq"), cosp, sinp)
        knv = _mm(kvn, W['w_ukv'], name="mla_ukv", out_dtypes=(BF16,))
        o = _mla_attn_fwd(qr, knv, kr)
        ob = o.astype(BF16)
        mixed = ob
        sv.update(gq=gq, gkv=gkv, cin=cin, qn=qn, kvn=kvn, kr=kr, qr=qr, knv=knv, o=o, ob=ob)
    else:
        w8 = jnp.pad(sm['conv_w'][j], ((0, 5), (0, 0)))
        bcu = _mm(xin_b, W['w_in'], name="conv_in", tn=768, b_sh=True, tm=FULL_ROWS)
        yb = _conv_fwd(bcu, w8)
        mixed = yb
        sv.update(w8=w8, bcu=bcu, yb=yb)
    g0, b0 = _row(sm['ln_g'][i, 0]), _row(sm['ln_b'][i, 0])
    g1, b1 = _row(sm['ln_g'][i, 1]), _row(sm['ln_b'][i, 1])
    ln = dict(tm=512, tn=D_MODEL, out_dtypes=(F32, BF16, F32), epilogue=_ln_fwd_epilogue)
    x1, x1b, v0 = _mm(mixed, W['w_out'], name="mix_out_ln", extras=[(xin, 'mn'), (g0, 'n'), (b0, 'n')], **ln)
    ab, dadu = _mm(x1b, W['w1'], name="mlp_up", out_dtypes=(BF16, BF16), b_sh=True, tm=FULL_ROWS,
                   epilogue=lambda acc: (jnp.square(jnp.maximum(acc, 0.0)), 2.0 * jnp.maximum(acc, 0.0)))
    x2, x2b, v1 = _mm(ab, W['w2'], name="mlp_down_ln", tk=D_FF, extras=[(x1, 'mn'), (g1, 'n'), (b1, 'n')], **ln)
    order = [(mid(x2b), 'whole')] if mid else []
    pp = _mm(p_i, W['proj'], name="ple_proj", tn=256, b_sh=True, extras=order,
             epilogue=lambda acc, *unused: (acc,))
    z, x3, x3b = _mm(x2b, W['gate'], name="ple_gate", out_dtypes=(F32, F32, BF16),
                     extras=[(x2, 'mn'), (pp, 'mn')],
                     epilogue=lambda acc, xv, pv: (acc,) + (xv + jax.nn.sigmoid(acc) * pv,) * 2)
    sv.update(v0=v0, x1b=x1b, ab=ab, dadu=dadu, v1=v1, x2b=x2b, pp=pp, z=z, g0=g0, g1=g1)
    return x3, x3b, sv, W


def _layer_bwd(i, grads_in, p_i, W, sm, sv, cosp, sinp, token, early=None, below=None):
    kind, j = MIXER[i % 3], i // 3
    big, small = {}, {}
    dx, dpp_b, dz_b = grads_in
    big['ple_w_proj'] = _mm(p_i, dpp_b, ta=True, name="ple_proj_dw", tn=256, out_sh=True, out_dtypes=(BF16,))
    big['ple_w_gate'] = _row_shards(_mm(sv['x2b'], dz_b, ta=True, name="dw_dd", out_dtypes=(BF16,)))
    ln = dict(tb=True, tm=512, tn=D_MODEL, out_dtypes=(F32, BF16), n_sums=2)
    (dv1, dv1b), (dg1, db1) = _mm(dz_b, W['gate'], name="ple_gate_dx_ln", epilogue=_ln_bwd_epilogue(1.0),
                                  extras=[(dx, 'mn'), (sv['v1'], 'mn'), (sv['g1'], 'n'), (token, 'whole')], **ln)
    big['mlp_w2'] = _row_shards(_mm(sv['ab'], dv1b, ta=True, name="mlp_down_dw", out_dtypes=(BF16,)))
    dub = _mm(dv1b, W['w2'], tb=True, name="mlp_down_dx", out_dtypes=(BF16,), tm=FULL_ROWS,
              extras=[(sv['dadu'], 'mn')], epilogue=lambda acc, d: (acc * d.astype(F32),))
    big['mlp_w1'] = _mm(sv['x1b'], dub, ta=True, name="mlp_up_dw", out_sh=True, out_dtypes=(BF16,))
    order = []
    if early is not None:
        order, big = [(early(big), 'whole')], {}
    (dv0, dv0b), (dg0, db0) = _mm(dub, W['w1'], name="mlp_up_dx_ln", b_sh=True, tk=D_FF, epilogue=_ln_bwd_epilogue(ALPHA),
                                  extras=[(dv1, 'mn'), (sv['v0'], 'mn'), (sv['g0'], 'n')] + order, **ln)
    small['ln_g'] = jnp.stack([dg0[0], dg1[0]])
    small['ln_b'] = jnp.stack([db0[0], db1[0]])
    resid = dict(tb=True, tn=D_MODEL, tm=512 if below else 1024, epilogue=_input_grad_epilogue,
                 extras=[(dv0, 'mn')] + [(a, 'mn') for a in below or ()],
                 out_dtypes=(F32, BF16, BF16) if below else (F32,))
    if kind == 'gla':
        big['gla_w_out'] = _row_shards(_mm(sv['yb'], dv0b, ta=True, name="dw_dd", out_dtypes=(BF16,)))
        dy = _mm(dv0b, W['w_out'], tb=True, name="dx_dd", tn=1024)
        do, dr_b, dng = _gla_post_bwd(dy, sv['o'], sv['pin'], _row(sm['gla_norm_g'][j]))
        dq_b, dk_b, dvv_b, dla = _gla_bwd(sv['pin'], sv['la'], sv['states'], do)
        dzg_b, dbg = _gla_gate_bwd(dla, sv['la'])
        dw_up = _mm(sv['pin'], dzg_b, ta=True, name="gla_gate_dw", M=128, tm=128,
                    a_off=(0, (GLA_IN_PAD - 128) // 128))
        dglr_b = _mm(dzg_b, sv['w_up'], tb=True, name="gla_gate_dx", out_dtypes=(BF16,))
        dpin_b = jnp.concatenate([dq_b, dk_b, dvv_b, dr_b, dglr_b], axis=1)
        dw_in = _mm(sv['xin_b'], dpin_b, ta=True, name="gla_in_dw", tn=640, out_dtypes=(BF16,))
        dxin = _mm(dpin_b, W['w_in'], name="gla_in_dx", tk=GLA_IN_PAD, **{**resid, 'tb': False})
        big['gla_w_in'] = jnp.stack([dw_in[:, GLA_WIN_STEP * qq:GLA_WIN_STEP * qq + GLA_WIN]
                                     for qq in range(N_CHIPS)])
        small.update(gla_w_gate_up=dw_up[:GLA_RANK], gla_b_gate=dbg[0], gla_norm_g=dng[0])
    elif kind == 'mla':
        big['mla_w_out'] = _row_shards(_mm(sv['ob'], dv0b, ta=True, name="dw_dd", out_dtypes=(BF16,)))
        do = _mm(dv0b, W['w_out'], tb=True, name="dx_dd", tn=1024)
        dqr, dkn_b, dvv_b, dkr = _mla_attn_bwd(sv['qr'], sv['knv'], sv['kr'], sv['o'], do)
        dq_b = _mla_qrope_bwd(dqr, cosp, sinp)
        dw_uq = _mm(sv['qn'], dq_b, ta=True, name="mla_up_dw", out_dtypes=(BF16,))
        dqn = _mm(dq_b, W['w_uq'], tb=True, name="mla_up_dx")
        dknv_b = jnp.concatenate([dkn_b, dvv_b], axis=1)
        dw_ukv = _mm(sv['kvn'], dknv_b, ta=True, name="mla_up_dw", out_dtypes=(BF16,))
        dkvn = _mm(dknv_b, W['w_ukv'], tb=True, name="mla_up_dx")
        dcin_b, dgq, dgkv = _mla_norm_bwd(sv['cin'], dqn, dkvn, dkr, sv['gq'], sv['gkv'], cosp, sinp)
        big['mla_w_in'] = _row_shards(_mm(sv['xin_b'], dcin_b, ta=True, name="mla_in_dw", tn=640,
                                          out_dtypes=(BF16,)))
        dxin = _mm(dcin_b, W['w_in'], name="mla_in_dx", tk=MLA_IN_PAD, **resid)
        big['mla_w_uq'] = _col_shards(
            dw_uq.reshape(MLA_QR, MLA_HEADS, MLA_QH)[:, :, :MLA_NOPE + MLA_ROPE].reshape(MLA_QR, -1))
        big['mla_w_ukv'] = _col_shards(
            dw_ukv.reshape(MLA_KVR, 2, MLA_HEADS, 128).transpose(0, 2, 1, 3).reshape(MLA_KVR, -1))
        small.update(mla_q_norm=dgq[0], mla_kv_norm=dgkv[0])
    else:
        big['conv_w_out'] = _row_shards(_mm(sv['yb'], dv0b, ta=True, name="dw_dd", out_dtypes=(BF16,)))
        dy = _mm(dv0b, W['w_out'], tb=True, name="dx_dd", tn=1024)
        db_b, dc_b, du_b, dw8 = _conv_bwd(sv['bcu'], sv['w8'], dy)
        dbcu_b = jnp.concatenate([db_b, dc_b, du_b], axis=1)
        big['conv_w_in'] = _mm(sv['xin_b'], dbcu_b, ta=True, name="conv_in_dw", tn=768, out_sh=True,
                               out_dtypes=(BF16,))
        dxin = _mm(dbcu_b, W['w_in'], name="conv_in_dx", tk=3 * D_MODEL, b_sh=True, **resid)
        small['conv_w'] = dw8[:3]
    return (dxin if below else (dxin,)), big, small


def _rope_tables(positions):
    inv_freq = ROPE_BASE ** (-jnp.arange(0, MLA_ROPE // 2, dtype=F32) * (2.0 / MLA_ROPE))
    ang = positions.astype(F32)[:, None] * inv_freq
    zeros = jnp.zeros((positions.shape[0], 64), F32)
    return (jnp.concatenate([jnp.cos(ang), jnp.cos(ang), zeros], axis=1),
            jnp.concatenate([jnp.sin(ang), jnp.sin(ang), zeros], axis=1))


FIRST_NEEDED = ['gla_w_in']


def _start_gathers(w, q):
    token, started = jnp.zeros(TOKEN, F32), []
    for i in range(DEPTH):
        sh = _layer_shards(w, i, q)
        for k, names in enumerate([list(sh)] if i > 0 else [FIRST_NEEDED, [n for n in sh if n not in FIRST_NEEDED]]):
            ops = [sh[n] for n in names]
            if i == 0 and k == 0:
                ops.append(_pack_small_shards(w))
            tag = "l%d%s" % (i, "ab"[k] if i == 0 else "")
            handle = _gather_start(_place_own(ops, token, "ag_own_" + tag), token, "ag_start_" + tag)
            token = handle[4]
            started.append((handle, names, tag))
    return started, token


def _pass_on(entry, after):
    handle, names, tag = entry
    _, lands = _gather_wait(handle, after, "ag_wait_" + tag)
    passing = _forward_start(lands, jnp.zeros(TOKEN, F32), "ag_pass_start_" + tag)
    return (passing, names, tag), passing[4]


def _gathered(passed, after):
    passing, names, tag = passed
    got = _forward_wait(passing, after, "ag_pass_wait_" + tag)
    return dict(zip(names, got)), got[-1]


def _local_shard_grad(name, g, q):
    if name == 'gla_w_in':
        return lax.dynamic_slice_in_dim(g, (GLA_SHARD - GLA_WIN_STEP) * q, GLA_SHARD, axis=1)
    if name == 'mla_w_in':
        return g[:, :MLA_IN]
    return g


def kernel(x, p, positions, gla_w_in, gla_w_gate_up, gla_b_gate, gla_norm_g, gla_w_out, mla_w_in, mla_q_norm, mla_kv_norm, mla_w_uq, mla_w_ukv, mla_w_out, conv_w_in, conv_w, conv_w_out, ln_g, ln_b, mlp_w1, mlp_w2, ple_w_gate, ple_w_proj, loss_target, m_gla_w_in, m_gla_w_gate_up, m_gla_b_gate, m_gla_norm_g, m_gla_w_out, m_mla_w_in, m_mla_q_norm, m_mla_kv_norm, m_mla_w_uq, m_mla_w_ukv, m_mla_w_out, m_conv_w_in, m_conv_w, m_conv_w_out, m_ln_g, m_ln_b, m_mlp_w1, m_mlp_w2, m_ple_w_gate, m_ple_w_proj, v_gla_w_in, v_gla_w_gate_up, v_gla_b_gate, v_gla_norm_g, v_gla_w_out, v_mla_w_in, v_mla_q_norm, v_mla_kv_norm, v_mla_w_uq, v_mla_w_ukv, v_mla_w_out, v_conv_w_in, v_conv_w, v_conv_w_out, v_ln_g, v_ln_b, v_mlp_w1, v_mlp_w2, v_ple_w_gate, v_ple_w_proj):
    args = locals()
    w = {n: args[n] for n in WNAMES}
    m = {n: args['m_' + n] for n in WNAMES}
    v = {n: args['v_' + n] for n in WNAMES}
    q = 2 * lax.axis_index("x") + lax.axis_index("y")
    cq = jnp.stack([lax.axis_index("c"), q]).astype(jnp.int32)

    cosp, sinp = _rope_tables(positions[0])
    started, after = _start_gathers(w, q)
    xin, saved, layers, sm = x[0], [], [], None
    xin_b = xin.astype(BF16)
    passed, after = _pass_on(started[0], after)
    for i in range(DEPTH):
        got, last = _gathered(passed, after)
        rest = mid = None
        if i == 0:
            sm = _unpack_small_gathered(last)
            sm['mla_q_norm'], sm['mla_kv_norm'] = w['mla_q_norm'], w['mla_kv_norm']
            rest = lambda after: _layer_weights(_gathered(*_pass_on(started[1], after))[0], 0)
        coming = {}
        if i + 1 < DEPTH:
            def mid(after, entry=started[i + 2], coming=coming):
                coming['passed'], token = _pass_on(entry, after)
                return token
        xin, xin_b, sv, W = _layer_fwd(i, xin, xin_b, p[i, 0], _layer_weights(got, i), sm, cosp, sinp, rest, mid)
        layers.append(W)
        saved.append(sv)
        passed, after = coming.get('passed'), xin
    *grads_in, loss_cols = _loss_head(xin, loss_target[0], saved[-1]['z'], saved[-1]['pp'])
    loss = jnp.sum(loss_cols[0])

    gbig = {n: [None] * WSPEC[n][0][0] for n in BIG}
    gsmall = {n: [None] * _full_shape(n)[0] for n in SMALL}
    pending = []

    def start(grads, i, tag):
        names = list(grads)
        gs = [grads[n] for n in names]
        handle = _reduce_direct_start(gs, tag) if i > 0 else _reduce_scatter_start(gs, cq, jnp.zeros(TOKEN, F32), tag)
        pending.append((handle, names, i, tag))
        return handle[4]

    joining = []

    def finish(above, after, token):
        for entry in [e for e in pending if e[2] > above]:
            pending.remove(entry)
            handle, names, i, tag = entry
            handle = (_reduce_direct_finish if i > 0 else _reduce_scatter_finish)(handle, cq, after, tag)
            joining.append((handle, names, i, tag))
            token = token + handle[4]
        return token

    token = jnp.zeros(TOKEN, F32)
    for i in reversed(range(DEPTH)):
        early = (lambda grads: start(grads, 0, "l0a")) if i == 0 else None
        below = (saved[i - 1]['z'], saved[i - 1]['pp']) if i > 0 else None
        grads_in, big, small = _layer_bwd(i, grads_in, p[i, 0], layers[i], sm, saved[i], cosp, sinp, token, early,
                                          below)
        dx = grads_in[0]
        token = finish(i + 1, dx, start(big, i, "l%d%s" % (i, "b" if i == 0 else "")))
        for n, g in small.items():
            gsmall[n][i if n in ('ln_g', 'ln_b') else i // 3] = g
    grad, delta, new_m, new_v = {}, {}, {}, {}

    def take(entries, after=None):
        last = entries[-1][0][4] if after is None else entries[-1][0][4] + after
        for handle, names, i, tag in entries:
            for n, g in zip(names, _join_wait(handle, last, "rs_join_wait_" + tag)):
                gbig[n][i if n in COMMON_BIG else i // 3] = _local_shard_grad(n, g, q)

    def update(n):
        adamw = _adamw_shard_major if n == 'gla_w_in' else _adamw
        grad[n], delta[n], new_m[n], new_v[n] = adamw(w[n], m[n], v[n], gbig[n], "adamw_" + n)

    token = finish(0, token, token)
    take(joining)
    del joining[:]
    ready = [n for n in BIG if n.startswith(('mla_', 'conv_'))]
    for n in ready:
        update(n)
    token = finish(-1, delta[ready[-1]], token)
    summing = _all_reduce_small_start(_pack_small({n: jnp.stack(g) for n, g in gsmall.items()}, loss), token)
    take(joining, summing[4])
    others = [n for n in BIG if n not in ready]
    for n in others:
        update(n)
    small_sum = _all_reduce_small_finish(summing, delta[others[-1]])
    gsm, loss = _unpack_small(small_sum, q), small_sum.reshape(-1)[SMALL_FULL]
    flat2 = lambda a: a.reshape(-1, a.shape[-1])
    res = _adamw_small(*[[flat2(d[n]) for n in SMALL] for d in (w, gsm, m, v)])
    for k, out in enumerate((grad, delta, new_m, new_v)):
        for n, r in zip(SMALL, res[k::4]):
            out[n] = r.reshape(WSPEC[n][0])
    return (loss, dx[None], *[grad[n] for n in WNAMES], *[delta[n] for n in WNAMES],
            *[new_m[n] for n in WNAMES], *[new_v[n] for n in WNAMES])
```

```python
import functools

import numpy as np
import jax
import jax.numpy as jnp
from jax import lax
from jax.experimental import pallas as pl
from jax.experimental.pallas import tpu as pltpu

F32 = jnp.float32
BF16 = jnp.bfloat16
MESH = pl.DeviceIdType.MESH

D_MODEL = 1024
DEPTH = 4
CHUNK = 64
ALPHA = (2 * DEPTH) ** 0.25
LN_EPS = 1e-5
RMS_EPS = 1e-6
D_FF = 4 * D_MODEL
GLA_HEADS = 4
GLA_DK = 128
GLA_DV = 256
GLA_RANK = 16
GLA_TAU = 16.0
GLA_HK = GLA_HEADS * GLA_DK
GLA_HV = GLA_HEADS * GLA_DV
GLA_IN = 2 * GLA_HK + GLA_HV + D_MODEL + GLA_RANK
GLA_IN_PAD = 2 * GLA_HK + GLA_HV + D_MODEL + 128
GLA_SHARD = GLA_IN // 4
GLA_WIN = 896
GLA_WIN_STEP = 768
MLA_HEADS = 8
MLA_NOPE = 128
MLA_ROPE = 64
MLA_V = 128
MLA_QR = 256
MLA_KVR = 256
MLA_IN = MLA_QR + MLA_KVR + MLA_ROPE
MLA_IN_PAD = MLA_QR + MLA_KVR + 128
MLA_QH = 256
ROPE_BASE = 10000.0
ADAM_LR = 0.001
ADAM_B1 = 0.9
ADAM_B2 = 0.999
ADAM_EPS = 1e-08
ADAM_WD = 0.01
ADAM_STEP = 10

VMEM_LIMIT = 48 * 1024 * 1024
FULL_ROWS = 2048
N_CHIPS = 4

WSPEC = {
    'gla_w_in': ((2, 1024, 772), 2), 'gla_w_gate_up': ((2, 16, 128), 2), 'gla_b_gate': ((2, 128), 1),
    'gla_norm_g': ((2, 64), 1), 'gla_w_out': ((2, 256, 1024), 1), 'mla_w_in': ((1, 256, 576), 1),
    'mla_q_norm': ((1, 256), None), 'mla_kv_norm': ((1, 256), None), 'mla_w_uq': ((1, 256, 384), 2),
    'mla_w_ukv': ((1, 256, 512), 2), 'mla_w_out': ((1, 256, 1024), 1), 'conv_w_in': ((1, 1024, 768), 2),
    'conv_w': ((1, 3, 256), 2), 'conv_w_out': ((1, 256, 1024), 1), 'ln_g': ((4, 2, 256), 2),
    'ln_b': ((4, 2, 256), 2), 'mlp_w1': ((4, 1024, 1024), 2), 'mlp_w2': ((4, 1024, 1024), 1),
    'ple_w_gate': ((4, 256, 1024), 1), 'ple_w_proj': ((4, 256, 256), 2),
}
WNAMES = list(WSPEC)
BIG = ['gla_w_in', 'gla_w_out', 'mla_w_in', 'mla_w_uq', 'mla_w_ukv', 'mla_w_out', 'conv_w_in', 'conv_w_out',
       'mlp_w1', 'mlp_w2', 'ple_w_gate', 'ple_w_proj']
SMALL_SHARDED = ['gla_w_gate_up', 'gla_b_gate', 'gla_norm_g', 'conv_w', 'ln_g', 'ln_b']
SMALL = SMALL_SHARDED + ['mla_q_norm', 'mla_kv_norm']
MIXER = ['gla', 'mla', 'conv']
COMMON_BIG = ['mlp_w1', 'mlp_w2', 'ple_w_gate', 'ple_w_proj']


def _size(shape):
    return int(np.prod(shape))


def _full_shape(name):
    shape, ax = WSPEC[name]
    if ax is None:
        return shape
    return tuple(s * N_CHIPS if i == ax else s for i, s in enumerate(shape))


def _cparams(sem=None):
    return pltpu.CompilerParams(dimension_semantics=sem, vmem_limit_bytes=VMEM_LIMIT)


def _out(shape, dtype):
    return pltpu.HBM(shape, dtype)


def _hbm(v):
    return pltpu.with_memory_space_constraint(v, pltpu.HBM)


def _mm(a, b, *, name, ta=False, tb=False, M=None, N=None, K=None, out_dtypes=(F32,), epilogue=None, extras=(),
        tm=1024, tn=512, tk=None, a_off=(0, 0), b_sh=False, out_sh=False, n_sums=0):
    if M is None:
        M = a.shape[1] if ta else a.shape[0]
    if K is None:
        K = a.shape[0] if ta else a.shape[1]
    if b_sh:
        kw, nq = b.shape[1], b.shape[2]
        n_b, k_b = (kw, N_CHIPS * nq) if tb else (N_CHIPS * nq, kw)
        N = n_b if N is None else N
        assert K == k_b
    elif N is None:
        N = b.shape[0] if tb else b.shape[1]
    if tk is None:
        tk = FULL_ROWS if ta else 1024
    tm, tn, tk = min(tm, M), min(tn, N), min(tk, K)
    assert M % tm == 0 and N % tn == 0 and K % tk == 0, (name, M, N, K, tm, tn, tk)
    nk = K // tk
    n_ex, n_out = len(extras), len(out_dtypes)
    assert n_sums == 0 or tn == N

    n_b = N_CHIPS if (b_sh and tb and tk == K) else 1

    def body(a_ref, *rest):
        b_refs, rest = rest[:n_b], rest[n_b:]
        ex_refs, out_refs = rest[:n_ex], rest[n_ex:n_ex + n_out]
        sum_refs = rest[n_ex + n_out:n_ex + n_out + n_sums]
        first_rows = pl.program_id(0) == 0
        dims = ((((0,) if ta else (1,)), ((1,) if tb else (0,))), ((), ()))
        if n_b == 1:
            part = lax.dot_general(a_ref[...].astype(BF16), b_refs[0][...].astype(BF16), dims,
                                   preferred_element_type=F32)
        else:
            part = sum(lax.dot_general(a_ref[:, s * nq:(s + 1) * nq].astype(BF16), b_refs[s][...].astype(BF16), dims,
                                       preferred_element_type=F32) for s in range(n_b))

        def finish(acc):
            res = (acc,) if epilogue is None else epilogue(acc, *[r[...] for r in ex_refs])
            if n_sums:
                res, sums = res

                @pl.when(first_rows)
                def _():
                    for r in sum_refs:
                        r[...] = jnp.zeros(r.shape, F32)

                for r, v in zip(sum_refs, sums):
                    r[...] += jnp.broadcast_to(v, r.shape)
            for r, v in zip(out_refs, res):
                r[...] = v.astype(r.dtype)

        if nk == 1:
            finish(part)
        else:
            acc_ref = rest[-1]
            k = pl.program_id(2)

            @pl.when(k == 0)
            def _():
                acc_ref[...] = part

            @pl.when(k > 0)
            def _():
                acc_ref[...] += part

            @pl.when(k == nk - 1)
            def _():
                finish(acc_ref[...])

    if ta:
        a_spec = pl.BlockSpec((tk, tm), lambda i, j, k: (k + a_off[0], i + a_off[1]))
    else:
        a_spec = pl.BlockSpec((tm, tk), lambda i, j, k: (i + a_off[0], k + a_off[1]))
    once = dict(pipeline_mode=pl.Buffered(1)) if (tn == N and nk == 1) else {}
    if n_b > 1:
        b_specs = [pl.BlockSpec((None, tn, nq), functools.partial(lambda i, j, k, s: (s, j, 0), s=s), **once)
                   for s in range(n_b)]
    elif b_sh and tb:
        assert nq % tk == 0
        per = nq // tk
        b_spec = pl.BlockSpec((None, tn, tk), lambda i, j, k: (k // per, j, k % per), **once)
    elif b_sh:
        assert nq % tn == 0
        per = nq // tn
        b_spec = pl.BlockSpec((None, tk, tn), lambda i, j, k: (j // per, k, j % per), **once)
    elif tb:
        b_spec = pl.BlockSpec((tn, tk), lambda i, j, k: (j, k), **once)
    else:
        b_spec = pl.BlockSpec((tk, tn), lambda i, j, k: (k, j), **once)
    if n_b == 1:
        b_specs = [b_spec]
    ex_specs = []
    for arr, kind in extras:
        if kind == 'mn':
            ex_specs.append(pl.BlockSpec((tm, tn), lambda i, j, k: (i, j)))
        elif kind == 'n':
            ex_specs.append(pl.BlockSpec((1, tn), lambda i, j, k: (0, j)))
        else:
            ex_specs.append(pl.BlockSpec(arr.shape, lambda i, j, k: (0, 0)))
    if out_sh:
        assert (N // N_CHIPS) % tn == 0
        per_o = N // N_CHIPS // tn
        o_spec = pl.BlockSpec((None, tm, tn), lambda i, j, k: (j // per_o, i, j % per_o))
        o_shape = (N_CHIPS, M, N // N_CHIPS)
    else:
        o_spec = pl.BlockSpec((tm, tn), lambda i, j, k: (i, j))
        o_shape = (M, N)
    outs = pl.pallas_call(
        body, name=name, grid=(M // tm, N // tn, nk),
        in_specs=[a_spec] + b_specs + ex_specs,
        out_specs=[o_spec for _ in out_dtypes] + [pl.BlockSpec((8, N), lambda i, j, k: (0, 0))] * n_sums,
        out_shape=[_out(o_shape, d) for d in out_dtypes] + [_out((8, N), F32)] * n_sums,
        scratch_shapes=[pltpu.VMEM((tm, tn), F32)] if nk > 1 else [],
        compiler_params=_cparams(("arbitrary" if n_sums else "parallel", "parallel", "arbitrary")),
    )(a, *[b] * n_b, *[e[0] for e in extras])
    if n_sums:
        return tuple(outs[:n_out]), tuple(outs[n_out:])
    return outs[0] if n_out == 1 else tuple(outs)


def _rowwise(fn, *, name, rows, pars=(), outs=(), accs=(), tm=256):
    S = rows[0][0].shape[0]
    tm = min(tm, S)
    assert S % tm == 0
    n_r, n_p, n_o, n_a = len(rows), len(pars), len(outs), len(accs)

    def body(*refs):
        r_refs, p_refs = refs[:n_r], refs[n_r:n_r + n_p]
        o_refs, a_refs = refs[n_r + n_p:n_r + n_p + n_o], refs[n_r + n_p + n_o:]
        o_vals, a_vals = fn([r[...] for r in r_refs], [p[...] for p in p_refs])
        for r, v in zip(o_refs, o_vals):
            r[...] = v.astype(r.dtype)
        if n_a:
            i = pl.program_id(0)

            @pl.when(i == 0)
            def _():
                for r in a_refs:
                    r[...] = jnp.zeros(r.shape, r.dtype)

            for r, v in zip(a_refs, a_vals):
                r[...] += jnp.broadcast_to(v, r.shape)

    in_specs = [pl.BlockSpec((tm, w), functools.partial(lambda i, o: (i, o), o=off)) for _, w, off in rows]
    in_specs += [pl.BlockSpec(p.shape, functools.partial(lambda i, nd: (0,) * nd, nd=p.ndim)) for p in pars]
    out_specs = [pl.BlockSpec((tm, w), lambda i: (i, 0)) for w, _ in outs]
    out_specs += [pl.BlockSpec((8, w), lambda i: (0, 0)) for w in accs]
    out_shape = [_out((S, w), d) for w, d in outs]
    out_shape += [_out((8, w), F32) for w in accs]
    res = pl.pallas_call(
        body, name=name, grid=(S // tm,), in_specs=in_specs, out_specs=out_specs, out_shape=out_shape,
        compiler_params=_cparams(("arbitrary",)),
    )(*[r[0] for r in rows], *pars)
    return tuple(res)


def _colsum(v):
    return jnp.sum(v, axis=0, keepdims=True)


def _ln_stats(v):
    mu = jnp.mean(v, axis=-1, keepdims=True)
    d = v - mu
    var = jnp.mean(d * d, axis=-1, keepdims=True)
    rstd = lax.rsqrt(var + LN_EPS)
    return d * rstd, rstd


def _ln_fwd_epilogue(h, x, g, b, *unused):
    v = ALPHA * x + h
    xhat, _ = _ln_stats(v)
    y = xhat * g + b
    return y, y, v


def _ln_bwd_epilogue(scale):
    def epilogue(acc, resid, v, g, *unused):
        dy = acc + scale * resid
        xhat, rstd = _ln_stats(v)
        dxh = dy * g
        m1 = jnp.mean(dxh, axis=-1, keepdims=True)
        m2 = jnp.mean(dxh * xhat, axis=-1, keepdims=True)
        dv = rstd * (dxh - m1 - xhat * m2)
        return (dv, dv), (_colsum(dy * xhat), _colsum(dy))
    return epilogue


def _ple_gate_grads(dx3, z, pp):
    s = jax.nn.sigmoid(z)
    return dx3 * s, dx3 * pp * s * (1.0 - s)


def _loss_head(y, t, z, pp):
    def fn(r, p):
        d = r[0] - r[1]
        dy = d * (1.0 / D_MODEL)
        return [dy, *_ple_gate_grads(dy, r[2], r[3])], [_colsum(d * d) * (0.5 / D_MODEL)]
    return _rowwise(fn, name="loss_head", rows=[(a, D_MODEL, 0) for a in (y, t, z, pp)],
                    outs=[(D_MODEL, F32), (D_MODEL, BF16), (D_MODEL, BF16)], accs=[D_MODEL])


def _input_grad_epilogue(acc, dv, *below):
    dx = acc + ALPHA * dv
    return (dx, *_ple_gate_grads(dx, *below)) if below else (dx,)


N_LEVELS = 6
GLA_STEP = 4


def _gla_consts():
    C = CHUNK
    A = np.zeros((N_LEVELS + 3, C, C), np.float32)
    masks = np.zeros((N_LEVELS + 1, C, C), np.float32)
    r = np.arange(C)[:, None]
    u = np.arange(C)[None, :]
    for l in range(N_LEVELS):
        half = C >> (l + 1)
        mid = (r // (2 * half)) * (2 * half) + half - 1
        A[l] = np.where(r > mid, (u > mid) & (u <= r), (u > r) & (u <= mid))
        masks[l] = ((r // (2 * half)) == (u // (2 * half))) & (((r // half) % 2) != ((u // half) % 2))
    masks[N_LEVELS] = (r == u)
    A[N_LEVELS] = (u <= r)
    A[N_LEVELS + 1] = (u > r)
    A[N_LEVELS + 2] = 1.0
    A = A.reshape(-1, C)
    return A, np.ascontiguousarray(A.T), masks


def _split3(v):
    hi = v.astype(BF16)
    r1 = v - hi.astype(F32)
    mid = r1.astype(BF16)
    lo = (r1 - mid.astype(F32)).astype(BF16)
    return hi, mid, lo


def _dot_exact01(a01, v):
    hi, mid, lo = _split3(v)
    f = lambda p: jnp.dot(a01, p, preferred_element_type=F32)
    return f(hi) + f(mid) + f(lo)


def _nt(a, b):
    return lax.dot_general(a, b, (((1,), (1,)), ((), ())), preferred_element_type=F32)


def _tn(a, b):
    return lax.dot_general(a, b, (((0,), (0,)), ((), ())), preferred_element_type=F32)


def _nn(a, b):
    return jnp.dot(a, b, preferred_element_type=F32)


def _gla_chunk_terms(q, k, E, m_ref):
    C = CHUNK
    scores = m_ref[N_LEVELS] * _nt(q.astype(BF16), k.astype(BF16))
    qes, kes = [], []
    for l in range(N_LEVELS):
        El = E[l * C:(l + 1) * C]
        qe, ke = (q * El).astype(BF16), (k * El).astype(BF16)
        qes.append(qe)
        kes.append(ke)
        scores = scores + m_ref[l] * _nt(qe, ke)
    return qes, kes, scores


def _head(v, h, w):
    return v[:, h * w:(h + 1) * w]


def _gla_fwd(pin, la):
    S = pin.shape[0]
    NC = S // CHUNK
    C, R = CHUNK, CHUNK * GLA_STEP
    A, _, masks = _gla_consts()

    def body(q_ref, k_ref, v_ref, la_ref, a_ref, m_ref, o_ref, st_ref, state):
        @pl.when(pl.program_id(0) == 0)
        def _():
            state[...] = jnp.zeros(state.shape, F32)

        for ci in range(GLA_STEP):
            rows = pl.ds(ci * C, C)
            E_all = jnp.exp(_dot_exact01(a_ref[...], la_ref[rows, :]))
            q_all = q_ref[rows, :] * (GLA_DK ** -0.5)
            k_all, v_all = k_ref[rows, :], v_ref[rows, :]
            outs = []
            for h in range(GLA_HEADS):
                q, k, E = _head(q_all, h, GLA_DK), _head(k_all, h, GLA_DK), _head(E_all, h, GLA_DK)
                _, _, scores = _gla_chunk_terms(q, k, E, m_ref)
                Eq, Ek, Ee = E[6 * C:7 * C], E[7 * C:8 * C], E[8 * C:9 * C]
                st = state[h]
                st_ref[h, ci] = st
                vb = _head(v_all, h, GLA_DV).astype(BF16)
                outs.append(_nn(scores.astype(BF16), vb) + _nt((q * Eq).astype(BF16), st.astype(BF16)))
                state[h] = st * jnp.concatenate([Ee] * (GLA_DV // C), axis=0) + _tn(vb, (k * Ek).astype(BF16))
            o_ref[rows, :] = jnp.concatenate(outs, axis=1)

    return pl.pallas_call(
        body, name="gla_fwd", grid=(NC // GLA_STEP,),
        in_specs=[pl.BlockSpec((R, GLA_HK), lambda c: (c, 0)),
                  pl.BlockSpec((R, GLA_HK), lambda c: (c, 1)),
                  pl.BlockSpec((R, GLA_HV), lambda c: (c, 2 * GLA_HK // GLA_HV)),
                  pl.BlockSpec((R, GLA_HK), lambda c: (c, 0)),
                  pl.BlockSpec(A.shape, lambda c: (0, 0)),
                  pl.BlockSpec(masks.shape, lambda c: (0, 0, 0))],
        out_specs=[pl.BlockSpec((R, GLA_HV), lambda c: (c, 0)),
                   pl.BlockSpec((GLA_HEADS, GLA_STEP, GLA_DV, GLA_DK), lambda c: (0, c, 0, 0))],
        out_shape=[_out((S, GLA_HV), F32), _out((GLA_HEADS, NC, GLA_DV, GLA_DK), F32)],
        scratch_shapes=[pltpu.VMEM((GLA_HEADS, GLA_DV, GLA_DK), F32)],
        compiler_params=_cparams(("arbitrary",)),
    )(pin, pin, pin, la, jnp.asarray(A, BF16), jnp.asarray(masks))


def _gla_bwd(pin, la, states, do):
    S = pin.shape[0]
    NC = S // CHUNK
    C, R = CHUNK, CHUNK * GLA_STEP
    A, AT, masks = _gla_consts()
    scale = GLA_DK ** -0.5

    def body(q_ref, k_ref, v_ref, la_ref, st_ref, do_ref, a_ref, at_ref, m_ref,
             dq_ref, dk_ref, dv_ref, dla_ref, dstate):
        @pl.when(pl.program_id(0) == 0)
        def _():
            dstate[...] = jnp.zeros(dstate.shape, F32)

        for ci in reversed(range(GLA_STEP)):
            one_chunk(ci, pl.ds(ci * C, C), q_ref, k_ref, v_ref, la_ref, st_ref, do_ref, a_ref, at_ref, m_ref,
                      dq_ref, dk_ref, dv_ref, dla_ref, dstate)

    def one_chunk(ci, rows, q_ref, k_ref, v_ref, la_ref, st_ref, do_ref, a_ref, at_ref, m_ref,
                  dq_ref, dk_ref, dv_ref, dla_ref, dstate):
        E_all = jnp.exp(_dot_exact01(a_ref[...], la_ref[rows, :]))
        q_all = q_ref[rows, :] * scale
        k_all, v_all, do_all = k_ref[rows, :], v_ref[rows, :], do_ref[rows, :]
        dqs, dks, dvs, dXs = [], [], [], []
        for h in range(GLA_HEADS):
            q, k, E = _head(q_all, h, GLA_DK), _head(k_all, h, GLA_DK), _head(E_all, h, GLA_DK)
            qes, kes, scores = _gla_chunk_terms(q, k, E, m_ref)
            Eq, Ek, Ee = E[6 * C:7 * C], E[7 * C:8 * C], E[8 * C:9 * C]
            st, dst = st_ref[h, ci], dstate[h]
            dob, vb = _head(do_all, h, GLA_DV).astype(BF16), _head(v_all, h, GLA_DV).astype(BF16)
            dstb = dst.astype(BF16)
            qEq, kEk = (q * Eq).astype(BF16), (k * Ek).astype(BF16)
            dsc = _nt(dob, vb)
            dvs.append(_tn(scores.astype(BF16), dob) + _nt(kEk, dstb))
            dqEq = _nn(dob, st.astype(BF16))
            dkEk = _nn(vb, dstb)
            Gd = (m_ref[N_LEVELS] * dsc).astype(BF16)
            dq = _nn(Gd, k.astype(BF16)) + dqEq * Eq
            dk = _tn(Gd, q.astype(BF16)) + dkEk * Ek
            dX = []
            for l in range(N_LEVELS):
                El = E[l * C:(l + 1) * C]
                G = (m_ref[l] * dsc).astype(BF16)
                dqe, dke = _nn(G, kes[l]), _tn(G, qes[l])
                dq = dq + dqe * El
                dk = dk + dke * El
                dX.append((dqe * q + dke * k) * El)
            dX.append(dqEq * q * Eq)
            dX.append(dkEk * k * Ek)
            prod = dst * st
            dEe = prod[0:C]
            for i in range(1, GLA_DV // C):
                dEe = dEe + prod[i * C:(i + 1) * C]
            dX.append(dEe * Ee)
            dXs.append(jnp.concatenate(dX, axis=0))
            dqs.append(dq * scale)
            dks.append(dk)
            dstate[h] = dst * jnp.concatenate([Ee] * (GLA_DV // C), axis=0) + _tn(dob, qEq)
        dla_ref[rows, :] = _dot_exact01(at_ref[...], jnp.concatenate(dXs, axis=1))
        dq_ref[rows, :] = jnp.concatenate(dqs, axis=1).astype(dq_ref.dtype)
        dk_ref[rows, :] = jnp.concatenate(dks, axis=1).astype(dk_ref.dtype)
        dv_ref[rows, :] = jnp.concatenate(dvs, axis=1).astype(dv_ref.dtype)

    rc = lambda c: NC // GLA_STEP - 1 - c
    return pl.pallas_call(
        body, name="gla_bwd", grid=(NC // GLA_STEP,),
        in_specs=[pl.BlockSpec((R, GLA_HK), lambda c: (rc(c), 0)),
                  pl.BlockSpec((R, GLA_HK), lambda c: (rc(c), 1)),
                  pl.BlockSpec((R, GLA_HV), lambda c: (rc(c), 2 * GLA_HK // GLA_HV)),
                  pl.BlockSpec((R, GLA_HK), lambda c: (rc(c), 0)),
                  pl.BlockSpec((GLA_HEADS, GLA_STEP, GLA_DV, GLA_DK), lambda c: (0, rc(c), 0, 0)),
                  pl.BlockSpec((R, GLA_HV), lambda c: (rc(c), 0)),
                  pl.BlockSpec(A.shape, lambda c: (0, 0)),
                  pl.BlockSpec(AT.shape, lambda c: (0, 0)),
                  pl.BlockSpec(masks.shape, lambda c: (0, 0, 0))],
        out_specs=[pl.BlockSpec((R, GLA_HK), lambda c: (rc(c), 0)),
                   pl.BlockSpec((R, GLA_HK), lambda c: (rc(c), 0)),
                   pl.BlockSpec((R, GLA_HV), lambda c: (rc(c), 0)),
                   pl.BlockSpec((R, GLA_HK), lambda c: (rc(c), 0))],
        out_shape=[_out((S, GLA_HK), BF16), _out((S, GLA_HK), BF16), _out((S, GLA_HV), BF16),
                   _out((S, GLA_HK), F32)],
        scratch_shapes=[pltpu.VMEM((GLA_HEADS, GLA_DV, GLA_DK), F32)],
        compiler_params=_cparams(("arbitrary",)),
    )(pin, pin, pin, la, states, do, jnp.asarray(A, BF16), jnp.asarray(AT, BF16), jnp.asarray(masks))


def _gla_post_fwd(o, pin, g):
    def fn(r, p):
        ov, rv = r
        ys = []
        for h in range(GLA_HEADS):
            oh = ov[:, h * GLA_DV:(h + 1) * GLA_DV]
            rh = rv[:, h * GLA_DV:(h + 1) * GLA_DV]
            rs = lax.rsqrt(jnp.mean(oh * oh, axis=-1, keepdims=True) + RMS_EPS)
            ys.append(oh * rs * p[0] * (rh * jax.nn.sigmoid(rh)))
        return [jnp.concatenate(ys, axis=1)], []
    return _rowwise(fn, name="gla_post_fwd", rows=[(o, GLA_HV, 0), (pin, GLA_HV, (2 * GLA_HK + GLA_HV) // GLA_HV)],
                    pars=[g], outs=[(GLA_HV, BF16)])[0]


def _gla_post_bwd(dy, o, pin, g):
    def fn(r, p):
        dyv, ov, rv = r
        dos, drs, dg = [], [], 0.0
        for h in range(GLA_HEADS):
            sl = slice(h * GLA_DV, (h + 1) * GLA_DV)
            oh, rh, dyh = ov[:, sl], rv[:, sl], dyv[:, sl]
            rs = lax.rsqrt(jnp.mean(oh * oh, axis=-1, keepdims=True) + RMS_EPS)
            xh = oh * rs
            sg = jax.nn.sigmoid(rh)
            d_on = dyh * (rh * sg)
            drs.append(dyh * (xh * p[0]) * (sg * (1.0 + rh * (1.0 - sg))))
            dg = dg + _colsum(d_on * xh)
            dxh = d_on * p[0]
            dos.append(rs * (dxh - xh * jnp.mean(dxh * xh, axis=-1, keepdims=True)))
        return [jnp.concatenate(dos, axis=1), jnp.concatenate(drs, axis=1)], [dg]
    return _rowwise(fn, name="gla_post_bwd",
                    rows=[(dy, GLA_HV, 0), (o, GLA_HV, 0), (pin, GLA_HV, (2 * GLA_HK + GLA_HV) // GLA_HV)],
                    pars=[g], outs=[(GLA_HV, F32), (GLA_HV, BF16)], accs=[GLA_DV])


def _gla_gate_bwd(dla, la):
    def fn(r, p):
        dz = r[0] * (1.0 / GLA_TAU) * (1.0 - jnp.exp(GLA_TAU * r[1]))
        return [dz], [_colsum(dz)]
    return _rowwise(fn, name="gla_gate_bwd", rows=[(dla, GLA_HK, 0), (la, GLA_HK, 0)], outs=[(GLA_HK, BF16)],
                    accs=[GLA_HK])


def _log_sigmoid(z):
    return jnp.minimum(z, 0.0) - jnp.log(1.0 + jnp.exp(-jnp.abs(z)))


def _rot_half(v):
    lane = lax.broadcasted_iota(jnp.int32, v.shape, 1)
    return jnp.where(lane < 32, -pltpu.roll(v, 96, 1), jnp.where(lane < 64, pltpu.roll(v, 32, 1), 0.0))


def _rms(v):
    rs = lax.rsqrt(jnp.mean(v * v, axis=-1, keepdims=True) + RMS_EPS)
    return v * rs, rs


def _mla_norm_fwd(cin, gq, gkv, cosp, sinp):
    def fn(r, p):
        cv, cs, sn = r
        qn, _ = _rms(cv[:, :MLA_QR])
        kvn, _ = _rms(cv[:, MLA_QR:MLA_QR + MLA_KVR])
        kr = cv[:, MLA_QR + MLA_KVR:]
        return [qn * p[0], kvn * p[1], kr * cs + _rot_half(kr) * sn], []
    return _rowwise(fn, name="mla_norm_fwd", rows=[(cin, MLA_IN_PAD, 0), (cosp, 128, 0), (sinp, 128, 0)],
                    pars=[gq, gkv], outs=[(MLA_QR, BF16), (MLA_KVR, BF16), (128, BF16)])


def _mla_qrope_fwd(q, cosp, sinp):
    scale = (MLA_NOPE + MLA_ROPE) ** -0.5

    def fn(r, p):
        qv, cs, sn = r
        parts = []
        for h in range(MLA_HEADS):
            parts.append(qv[:, h * MLA_QH:h * MLA_QH + 128] * scale)
            rp = qv[:, h * MLA_QH + 128:(h + 1) * MLA_QH]
            parts.append((rp * cs + _rot_half(rp) * sn) * scale)
        return [jnp.concatenate(parts, axis=1)], []
    W = MLA_HEADS * MLA_QH
    return _rowwise(fn, name="mla_qrope_fwd", rows=[(q, W, 0), (cosp, 128, 0), (sinp, 128, 0)],
                    outs=[(W, BF16)])[0]


def _mla_qrope_bwd(dq, cosp, sinp):
    scale = (MLA_NOPE + MLA_ROPE) ** -0.5

    def fn(r, p):
        dv, cs, sn = r
        parts = []
        for h in range(MLA_HEADS):
            parts.append(dv[:, h * MLA_QH:h * MLA_QH + 128] * scale)
            rp = dv[:, h * MLA_QH + 128:(h + 1) * MLA_QH]
            parts.append((rp * cs - _rot_half(rp) * sn) * scale)
        return [jnp.concatenate(parts, axis=1)], []
    W = MLA_HEADS * MLA_QH
    return _rowwise(fn, name="mla_qrope_bwd", rows=[(dq, W, 0), (cosp, 128, 0), (sinp, 128, 0)],
                    outs=[(W, BF16)])[0]


def _mla_norm_bwd(cin, dqn, dkvn, dkr, gq, gkv, cosp, sinp):
    def fn(r, p):
        cv, dq_, dkv_, dkr_, cs, sn = r
        outs, accs = [], []
        for (lo, hi), dn, g in (((0, MLA_QR), dq_, p[0]), ((MLA_QR, MLA_QR + MLA_KVR), dkv_, p[1])):
            xh, rs = _rms(cv[:, lo:hi])
            dxh = dn * g
            outs.append(rs * (dxh - xh * jnp.mean(dxh * xh, axis=-1, keepdims=True)))
            accs.append(_colsum(dn * xh))
        dk = dkr_[:, 0:128]
        for h in range(1, MLA_HEADS):
            dk = dk + dkr_[:, h * 128:(h + 1) * 128]
        outs.append(dk * cs - _rot_half(dk) * sn)
        return [jnp.concatenate(outs, axis=1)], accs
    return _rowwise(fn, name="mla_norm_bwd",
                    rows=[(cin, MLA_IN_PAD, 0), (dqn, MLA_QR, 0), (dkvn, MLA_KVR, 0), (dkr, MLA_HEADS * 128, 0),
                          (cosp, 128, 0), (sinp, 128, 0)],
                    pars=[gq, gkv], outs=[(MLA_IN_PAD, BF16)], accs=[MLA_QR, MLA_KVR])


def _mla_probs(q, k, i, tq):
    s = _nt(q, k)
    row = (i * tq + lax.broadcasted_iota(jnp.int32, s.shape, 0)) // CHUNK
    col = lax.broadcasted_iota(jnp.int32, s.shape, 1) // CHUNK
    s = jnp.where(col <= row, s, -jnp.inf)
    e = jnp.exp(s - jnp.max(s, axis=-1, keepdims=True))
    return e / jnp.sum(e, axis=-1, keepdims=True)


def _mla_attn_fwd(qr, knv, kr, tq=256):
    S = qr.shape[0]
    tq = min(tq, S)

    def body(q_ref, kn_ref, v_ref, kr_ref, o_ref, k_cat):
        k_cat[:, :128] = kn_ref[...]
        k_cat[:, 128:] = kr_ref[...]
        for i in range(S // tq):
            rows, keys = pl.ds(i * tq, tq), pl.ds(0, (i + 1) * tq)
            pr = _mla_probs(q_ref[rows, :], k_cat[keys, :], i, tq)
            o_ref[rows, :] = _nn(pr.astype(BF16), v_ref[keys, :])

    return pl.pallas_call(
        body, name="mla_attn_fwd", grid=(MLA_HEADS,),
        in_specs=[pl.BlockSpec((S, MLA_QH), lambda h: (0, h)),
                  pl.BlockSpec((S, 128), lambda h: (0, h)),
                  pl.BlockSpec((S, 128), lambda h: (0, MLA_HEADS + h)),
                  pl.BlockSpec((S, 128), lambda h: (0, 0))],
        out_specs=pl.BlockSpec((S, 128), lambda h: (0, h)),
        out_shape=_out((S, MLA_HEADS * MLA_V), F32),
        scratch_shapes=[pltpu.VMEM((S, MLA_QH), BF16)],
        compiler_params=_cparams(("parallel",)),
    )(qr, knv, knv, kr)


def _mla_attn_bwd(qr, knv, kr, o, do, tq=256):
    S = qr.shape[0]
    tq = min(tq, S)
    W = MLA_HEADS * 128

    def body(q_ref, kn_ref, v_ref, kr_ref, o_ref, do_ref, dq_ref, dkn_ref, dv_ref, dkr_ref, k_cat, dk_acc, dv_acc):
        k_cat[:, :128] = kn_ref[...]
        k_cat[:, 128:] = kr_ref[...]
        dk_acc[...] = jnp.zeros(dk_acc.shape, F32)
        dv_acc[...] = jnp.zeros(dv_acc.shape, F32)
        for i in range(S // tq):
            rows, keys = pl.ds(i * tq, tq), pl.ds(0, (i + 1) * tq)
            q, k, v = q_ref[rows, :], k_cat[keys, :], v_ref[keys, :]
            pr = _mla_probs(q, k, i, tq)
            dov = do_ref[rows, :]
            delta = jnp.sum(dov * o_ref[rows, :], axis=-1, keepdims=True)
            dob = dov.astype(BF16)
            ds = (pr * (_nt(dob, v) - delta)).astype(BF16)
            dq_ref[rows, :] = _nn(ds, k)
            dk_acc[keys, :] += _tn(ds, q)
            dv_acc[keys, :] += _tn(pr.astype(BF16), dob)
        dkn_ref[...] = dk_acc[:, :128].astype(dkn_ref.dtype)
        dkr_ref[...] = dk_acc[:, 128:]
        dv_ref[...] = dv_acc[...].astype(dv_ref.dtype)

    head = lambda w: pl.BlockSpec((S, w), lambda h: (0, h))
    return pl.pallas_call(
        body, name="mla_attn_bwd", grid=(MLA_HEADS,),
        in_specs=[head(MLA_QH), head(128), pl.BlockSpec((S, 128), lambda h: (0, MLA_HEADS + h)),
                  pl.BlockSpec((S, 128), lambda h: (0, 0)), head(128), head(128)],
        out_specs=[head(MLA_QH), head(128), head(128), head(128)],
        out_shape=[_out((S, MLA_HEADS * MLA_QH), F32), _out((S, W), BF16), _out((S, W), BF16), _out((S, W), F32)],
        scratch_shapes=[pltpu.VMEM((S, MLA_QH), BF16), pltpu.VMEM((S, MLA_QH), F32), pltpu.VMEM((S, 128), F32)],
        compiler_params=_cparams(("parallel",)),
    )(qr, knv, knv, kr, o, do)


CONV_TILE = 256


def _shift_down(v, n):
    row = lax.broadcasted_iota(jnp.int32, v.shape, 0)
    return jnp.where(row >= n, pltpu.roll(v, n, 0), 0.0)


def _shift_up(v, n):
    S = v.shape[0]
    row = lax.broadcasted_iota(jnp.int32, v.shape, 0)
    return jnp.where(row < S - n, pltpu.roll(v, S - n, 0), 0.0)


def _conv_specs(S, n_extra_cols):
    nt = D_MODEL // CONV_TILE
    specs = [pl.BlockSpec((S, CONV_TILE), functools.partial(lambda j, o: (0, o + j), o=part * nt))
             for part in range(3)]
    specs.append(pl.BlockSpec((8, CONV_TILE), lambda j: (0, j)))
    specs += [pl.BlockSpec((S, CONV_TILE), lambda j: (0, j)) for _ in range(n_extra_cols)]
    return specs


def _conv_fwd(bcu, w8):
    S = bcu.shape[0]

    def body(b_ref, c_ref, u_ref, w_ref, y_ref):
        cu = c_ref[...] * u_ref[...]
        z = w_ref[2:3, :] * cu + w_ref[1:2, :] * _shift_down(cu, 1) + w_ref[0:1, :] * _shift_down(cu, 2)
        y_ref[...] = (b_ref[...] * z).astype(y_ref.dtype)

    return pl.pallas_call(
        body, name="conv_fwd", grid=(D_MODEL // CONV_TILE,), in_specs=_conv_specs(S, 0),
        out_specs=pl.BlockSpec((S, CONV_TILE), lambda j: (0, j)),
        out_shape=_out((S, D_MODEL), BF16),
        compiler_params=_cparams(("parallel",)),
    )(bcu, bcu, bcu, w8)


def _conv_bwd(bcu, w8, dy):
    S = bcu.shape[0]

    def body(b_ref, c_ref, u_ref, w_ref, dy_ref, db_ref, dc_ref, du_ref, dw_ref):
        b, c, u, dyv = b_ref[...], c_ref[...], u_ref[...], dy_ref[...]
        w0, w1, w2 = w_ref[0:1, :], w_ref[1:2, :], w_ref[2:3, :]
        cu = c * u
        cu1, cu2 = _shift_down(cu, 1), _shift_down(cu, 2)
        z = w2 * cu + w1 * cu1 + w0 * cu2
        dz = dyv * b
        db_ref[...] = (dyv * z).astype(db_ref.dtype)
        dcu = w2 * dz + w1 * _shift_up(dz, 1) + w0 * _shift_up(dz, 2)
        dc_ref[...] = (dcu * u).astype(dc_ref.dtype)
        du_ref[...] = (dcu * c).astype(du_ref.dtype)
        dw_ref[...] = jnp.zeros(dw_ref.shape, F32)
        dw_ref[0:1, :] = _colsum(dz * cu2)
        dw_ref[1:2, :] = _colsum(dz * cu1)
        dw_ref[2:3, :] = _colsum(dz * cu)

    col = pl.BlockSpec((S, CONV_TILE), lambda j: (0, j))
    return pl.pallas_call(
        body, name="conv_bwd", grid=(D_MODEL // CONV_TILE,), in_specs=_conv_specs(S, 1),
        out_specs=[col, col, col, pl.BlockSpec((8, CONV_TILE), lambda j: (0, j))],
        out_shape=[_out((S, D_MODEL), BF16)] * 3 + [_out((8, D_MODEL), F32)],
        compiler_params=_cparams(("parallel",)),
    )(bcu, bcu, bcu, w8, dy)


def _adamw_update(w, g, m, v):
    nm = ADAM_B1 * m + (1.0 - ADAM_B1) * g
    nv = ADAM_B2 * v + (1.0 - ADAM_B2) * jnp.square(g)
    m_hat = nm / (1.0 - ADAM_B1 ** ADAM_STEP)
    v_hat = nv / (1.0 - ADAM_B2 ** ADAM_STEP)
    return -ADAM_LR * (m_hat / (jnp.sqrt(v_hat) + ADAM_EPS) + ADAM_WD * w), nm, nv


def _adamw_shard_major(w, m, v, gs, name):
    view = lambda a: jnp.transpose(a, (2, 0, 1))
    g = jnp.stack([x.T for x in gs], axis=1)
    n, L, k = g.shape
    rows = n // 4
    assert n % 4 == 0

    def body(w_ref, m_ref, v_ref, g_ref, go_ref, d_ref, nm_ref, nv_ref):
        gv = g_ref[...]
        d_ref[...], nm_ref[...], nv_ref[...] = _adamw_update(w_ref[...], gv, m_ref[...], v_ref[...])
        go_ref[...] = gv

    spec = pl.BlockSpec((rows, L, k), lambda i: (i, 0, 0))
    outs = pl.pallas_call(
        body, name=name, grid=(4,), in_specs=[spec] * 4, out_specs=[spec] * 4,
        out_shape=[jax.ShapeDtypeStruct((n, L, k), F32)] * 4,
        compiler_params=_cparams(("parallel",)),
    )(view(w), view(m), view(v), g)
    return [jnp.transpose(o, (1, 2, 0)) for o in outs]


def _adamw_small(ws, gs, ms, vs):
    n = len(ws)

    def body(*refs):
        ins, outs = refs[:4 * n], refs[4 * n:]
        for t in range(n):
            w_ref, g_ref, m_ref, v_ref = (ins[k * n + t] for k in range(4))
            gv = g_ref[...]
            outs[4 * t][...] = gv
            outs[4 * t + 1][...], outs[4 * t + 2][...], outs[4 * t + 3][...] = _adamw_update(
                w_ref[...], gv, m_ref[...], v_ref[...])

    return pl.pallas_call(
        body, name="adamw_small",
        out_shape=[jax.ShapeDtypeStruct(a.shape, F32) for a in ws for _ in range(4)],
    )(*ws, *gs, *ms, *vs)


def _adamw(w, m, v, gs, name):
    L, R, Cn = w.shape
    assert len(gs) == L
    tr = R if R <= 256 else 256
    assert R % tr == 0

    def body(w_ref, m_ref, v_ref, *rest):
        g_refs, (go_ref, d_ref, nm_ref, nv_ref) = rest[:L], rest[L:]
        layer = pl.program_id(0)
        gv = g_refs[0][...]
        for k in range(1, L):
            gv = jnp.where(layer == k, g_refs[k][...], gv)
        d_ref[...], nm_ref[...], nv_ref[...] = _adamw_update(w_ref[...], gv, m_ref[...], v_ref[...])
        go_ref[...] = gv

    spec = pl.BlockSpec((None, tr, Cn), lambda l, i: (l, i, 0))
    g_specs = [pl.BlockSpec((tr, Cn), functools.partial(lambda l, i, k: (jnp.where(l == k, i, 0), 0), k=k))
               for k in range(L)]
    return pl.pallas_call(
        body, name=name, grid=(L, R // tr), in_specs=[spec] * 3 + g_specs, out_specs=[spec] * 4,
        out_shape=[jax.ShapeDtypeStruct((L, R, Cn), F32)] * 4,
        compiler_params=_cparams(("arbitrary", "arbitrary")),
    )(w, m, v, *gs)


HBM_SPEC = pl.BlockSpec(memory_space=pltpu.HBM)


def _place():
    return lax.axis_index("x"), lax.axis_index("y"), lax.axis_index("c")


def _other_chips(x, y):
    return [(1 - x, y), (x, 1 - y), (1 - x, 1 - y)]


SEM_SPEC = pl.BlockSpec(memory_space=pltpu.SEMAPHORE)
ANY_SPEC = pl.BlockSpec(memory_space=pl.ANY)
VMEM_SPEC = pl.BlockSpec(memory_space=pltpu.VMEM)
EFFECT = pltpu.SideEffectType.DATAFLOW_SIDE_EFFECTING
TOKEN = (8, 128)


def _ici_start(srcs, lands, after, copies, name, per_src=3):
    n, nl = len(srcs), len(lands)

    def body(*refs):
        src_refs, land_refs = refs[:n], refs[n:n + nl]
        send_sems, recv_sems, token = refs[n + nl + 1], refs[n + nl + 2], refs[-1]
        x, y, c = _place()
        for k, src, dst, to in copies(src_refs, land_refs, x, y, c):
            pltpu.make_async_remote_copy(src_ref=src, dst_ref=dst, send_sem=send_sems.at[k], recv_sem=recv_sems.at[k],
                                         device_id=to, device_id_type=MESH).start()
        token[...] = jnp.zeros(TOKEN, F32)

    n_copies = per_src * max(n, nl if n == 0 else 0)
    res = pl.pallas_call(
        body, name=name,
        out_shape=(pltpu.SemaphoreType.DMA((n_copies,)), pltpu.SemaphoreType.DMA((n_copies,)),
                   *[pltpu.HBM(s.shape, s.dtype) for s in srcs], *[pltpu.HBM(l.shape, l.dtype) for l in lands],
                   jax.ShapeDtypeStruct(TOKEN, F32)),
        in_specs=[HBM_SPEC] * (n + nl) + [ANY_SPEC],
        out_specs=(SEM_SPEC, SEM_SPEC, *[HBM_SPEC] * (n + nl), VMEM_SPEC),
        input_output_aliases={t: 2 + t for t in range(n + nl)},
        compiler_params=pltpu.CompilerParams(has_side_effects=EFFECT),
    )(*[_hbm(s) for s in srcs], *[_hbm(l) for l in lands], after)
    return res[0], res[1], list(res[2:2 + n]), list(res[2 + n:2 + n + nl]), res[-1]


def _ici_wait(handle, after, copies, name):
    send_sems, recv_sems, srcs, lands, _ = handle
    n, nl = len(srcs), len(lands)

    def body(*refs):
        src_refs, land_refs = refs[:n], refs[n:n + nl]
        send_s, recv_s = refs[n + nl], refs[n + nl + 1]
        x, y, c = _place()
        for k, src, dst, to in copies(src_refs, land_refs, x, y, c):
            cp = pltpu.make_async_remote_copy(src_ref=src, dst_ref=dst, send_sem=send_s.at[k], recv_sem=recv_s.at[k],
                                              device_id=to, device_id_type=MESH)
            cp.wait_send()
            cp.wait_recv()

    res = pl.pallas_call(
        body, name=name,
        out_shape=(*[pltpu.HBM(s.shape, s.dtype) for s in srcs], *[pltpu.HBM(l.shape, l.dtype) for l in lands]),
        in_specs=[HBM_SPEC] * (n + nl) + [SEM_SPEC, SEM_SPEC, ANY_SPEC],
        out_specs=tuple([HBM_SPEC] * (n + nl)),
        input_output_aliases={t: t for t in range(n + nl)},
        compiler_params=pltpu.CompilerParams(has_side_effects=EFFECT),
    )(*srcs, *lands, send_sems, recv_sems, after)
    return list(res[:n]), list(res[n:])


def _gather_copies(halves, arriving):
    def copies(src_refs, land_refs, x, y, c):
        q = 2 * x + y
        out = []
        for t, H in enumerate(halves):
            mine = land_refs[t].at[q, pl.ds(c * H, H), :]
            for j, (cx, cy) in enumerate(_other_chips(x, y)):
                theirs = land_refs[t].at[2 * cx + cy, pl.ds(c * H, H), :]
                out.append((3 * t + j, mine, theirs if arriving else mine, (cx, cy, c)))
        return out
    return copies


OWN_TILES = 4


def _place_own(ops, after, cq, name):
    n = len(ops)
    in_specs, out_specs, out_shape = [], [], []
    for op in ops:
        arr, layer = op if isinstance(op, tuple) else (op, None)
        a, b = arr.shape[-2:]
        tiled = a % (16 * OWN_TILES) == 0
        ta = a // OWN_TILES if tiled else a
        if layer is None:
            in_specs.append(pl.BlockSpec((ta, b), lambda i, cq_ref, tiled=tiled: (i if tiled else 0, 0)))
        else:
            in_specs.append(pl.BlockSpec((None, ta, b),
                                         lambda i, cq_ref, tiled=tiled, layer=layer: (layer, i if tiled else 0, 0)))
        out_specs.append(pl.BlockSpec((None, ta, b), lambda i, cq_ref, tiled=tiled: (cq_ref[1], i if tiled else 0, 0)))
        out_shape.append(jax.ShapeDtypeStruct((N_CHIPS, a, b), arr.dtype if layer is None else BF16))

    def body(cq_ref, *refs):
        for in_ref, out_ref in zip(refs[:n], refs[n + 1:]):
            out_ref[...] = in_ref[...].astype(out_ref.dtype)

    grid_spec = pltpu.PrefetchScalarGridSpec(num_scalar_prefetch=1, grid=(OWN_TILES,),
                                             in_specs=in_specs + [ANY_SPEC], out_specs=out_specs)
    return pl.pallas_call(
        body, name=name, grid_spec=grid_spec, out_shape=out_shape, compiler_params=_cparams(("arbitrary",)),
    )(cq, *[op[0] if isinstance(op, tuple) else op for op in ops], after)


def _gather_start(lands, after, name):
    return _ici_start([], lands, after, _gather_copies([l.shape[1] // 2 for l in lands], False), name)


def _gather_wait(handle, after, name):
    halves = [l.shape[1] // 2 for l in handle[3]]
    return _ici_wait(handle, after, _gather_copies(halves, True), name)


def _forward_copies(halves, arriving):
    def copies(src_refs, land_refs, x, y, c):
        out = []
        for t, H in enumerate(halves):
            for j, (cx, cy) in enumerate(_other_chips(x, y)):
                mine = land_refs[t].at[2 * cx + cy, pl.ds(c * H, H), :]
                theirs = land_refs[t].at[2 * cx + cy, pl.ds((1 - c) * H, H), :]
                out.append((3 * t + j, mine, theirs if arriving else mine, (x, y, 1 - c)))
        return out
    return copies


def _forward_start(lands, after, name):
    halves = [l.shape[1] // 2 for l in lands]
    return _ici_start([], lands, after, _forward_copies(halves, False), name)


def _forward_wait(handle, after, name):
    halves = [l.shape[1] // 2 for l in handle[3]]
    return _ici_wait(handle, after, _forward_copies(halves, True), name)[1]


def _swap_halves(ops, name):
    n = len(ops)

    def body(*refs):
        in_refs, out_refs, send_sems, recv_sems = refs[:n], refs[n:2 * n], refs[2 * n], refs[2 * n + 1]
        x, y, c = _place()
        cps = []
        for t in range(n):
            H = ops[t].shape[1] // 2
            cp = pltpu.make_async_remote_copy(src_ref=in_refs[t].at[:, pl.ds((1 - c) * H, H), :],
                                              dst_ref=out_refs[t], send_sem=send_sems.at[t],
                                              recv_sem=recv_sems.at[t], device_id=(x, y, 1 - c),
                                              device_id_type=MESH)
            cp.start()
            cps.append(cp)
        for cp in cps:
            cp.wait()

    return pl.pallas_call(
        body, name=name, in_specs=[HBM_SPEC] * n, out_specs=[HBM_SPEC] * n,
        out_shape=[jax.ShapeDtypeStruct((N_CHIPS, o.shape[1] // 2, o.shape[2]), o.dtype) for o in ops],
        scratch_shapes=[pltpu.SemaphoreType.DMA((n,)), pltpu.SemaphoreType.DMA((n,))],
    )(*ops)


def _sum_rows_tile(h):
    return h if h <= 512 else 512


def _pair_sum(g, t, cq, name):
    _, a, b = g.shape
    H = a // 2
    tr = _sum_rows_tile(H)

    def body(cq_ref, g_ref, t_ref, o_ref):
        o_ref[...] = (g_ref[...].astype(F32) + t_ref[...].astype(F32)).astype(o_ref.dtype)

    grid_spec = pltpu.PrefetchScalarGridSpec(
        num_scalar_prefetch=1, grid=(N_CHIPS, H // tr),
        in_specs=[pl.BlockSpec((None, None, tr, b), lambda j, i, cq_ref: (j, cq_ref[0], i, 0)),
                  pl.BlockSpec((None, tr, b), lambda j, i, cq_ref: (j, i, 0))],
        out_specs=pl.BlockSpec((None, tr, b), lambda j, i, cq_ref: (j, i, 0)))
    return pl.pallas_call(
        body, name=name, grid_spec=grid_spec, out_shape=_out(t.shape, BF16),
        compiler_params=_cparams(("parallel", "parallel")),
    )(cq, g.reshape(N_CHIPS, 2, H, b), t)


def _scatter_copies(src_refs, land_refs, x, y, c):
    out = []
    for j, (cx, cy) in enumerate(_other_chips(x, y)):
        for t in range(len(src_refs)):
            out.append((3 * t + j, src_refs[t].at[2 * cx + cy], land_refs[t].at[j], (cx, cy, c)))
    return out


def _scatter_start(ops, after, name):
    lands = [lax.empty((3,) + o.shape[1:], o.dtype) for o in ops]
    return _ici_start(ops, lands, after, _scatter_copies, name)


def _scatter_wait(handle, after, name):
    return _ici_wait(handle, after, _scatter_copies, name)


def _chip_sum(p, t, cq, name):
    _, H, b = p.shape
    tr = _sum_rows_tile(H)

    def body(cq_ref, p_ref, t_ref, o_ref):
        acc = p_ref[...].astype(F32)
        for j in range(3):
            acc = acc + t_ref[j].astype(F32)
        o_ref[...] = acc

    grid_spec = pltpu.PrefetchScalarGridSpec(
        num_scalar_prefetch=1, grid=(H // tr,),
        in_specs=[pl.BlockSpec((None, tr, b), lambda i, cq_ref: (cq_ref[1], i, 0)),
                  pl.BlockSpec((3, tr, b), lambda i, cq_ref: (0, i, 0))],
        out_specs=pl.BlockSpec((None, tr, b), lambda i, cq_ref: (cq_ref[0], i, 0)))
    out = pl.pallas_call(
        body, name=name, grid_spec=grid_spec, out_shape=_out((2, H, b), F32),
        compiler_params=_cparams(("parallel",)),
    )(cq, p, t)
    return out.reshape(2 * H, b)


def _join_copies(arriving):
    def copies(src_refs, land_refs, x, y, c):
        out = []
        for t, land in enumerate(land_refs):
            H = land.shape[0] // 2
            mine, theirs = land.at[pl.ds(c * H, H), :], land.at[pl.ds((1 - c) * H, H), :]
            out.append((t, mine, theirs if arriving else mine, (x, y, 1 - c)))
        return out
    return copies


def _join_start(fs, name):
    return _ici_start([], fs, jnp.zeros(TOKEN, F32), _join_copies(False), name, per_src=1)


def _join_wait(handle, after, name):
    return _ici_wait(handle, after, _join_copies(True), name)[1]


def _direct_copies(src_refs, land_refs, x, y, c):
    out = []
    for t in range(len(src_refs)):
        H = src_refs[t].shape[1] // 2
        for k in range(1, 8):
            px, py, pc = x ^ (k >> 2), y ^ ((k >> 1) & 1), c ^ (k & 1)
            out.append((7 * t + k - 1, src_refs[t].at[2 * px + py, pl.ds(pc * H, H), :], land_refs[t].at[k - 1],
                        (px, py, pc)))
    return out


def _direct_sum(g, t, cq, name):
    _, a, b = g.shape
    H = a // 2
    tr = _sum_rows_tile(H)

    def body(cq_ref, g_ref, t_ref, o_ref):
        acc = g_ref[...].astype(F32)
        for k in range(7):
            acc = acc + t_ref[k].astype(F32)
        o_ref[...] = acc

    grid_spec = pltpu.PrefetchScalarGridSpec(
        num_scalar_prefetch=1, grid=(H // tr,),
        in_specs=[pl.BlockSpec((None, None, tr, b), lambda i, cq_ref: (cq_ref[1], cq_ref[0], i, 0)),
                  pl.BlockSpec((7, tr, b), lambda i, cq_ref: (0, i, 0))],
        out_specs=pl.BlockSpec((None, tr, b), lambda i, cq_ref: (cq_ref[0], i, 0)))
    out = pl.pallas_call(
        body, name=name, grid_spec=grid_spec, out_shape=_out((2, H, b), F32),
        compiler_params=_cparams(("parallel",)),
    )(cq, g.reshape(N_CHIPS, 2, H, b), t)
    return out.reshape(a, b)


def _reduce_direct_start(gs, tag):
    lands = [lax.empty((7, g.shape[1] // 2, g.shape[2]), g.dtype) for g in gs]
    return _ici_start(gs, lands, jnp.zeros(TOKEN, F32), _direct_copies, "rs_direct_start_" + tag, per_src=7)


def _reduce_direct_finish(handle, cq, after, tag):
    gs, rs = _ici_wait(handle, after, _direct_copies, "rs_direct_wait_" + tag)
    fs = [_direct_sum(g, r, cq, "rs_direct_sum") for g, r in zip(gs, rs)]
    return _join_start(fs, "rs_join_start_" + tag)


def _reduce_scatter_start(gs, cq, after, tag):
    ts = _swap_halves(gs, "rs_swap_" + tag)
    ps = [_pair_sum(g, t, cq, "rs_pair_sum") for g, t in zip(gs, ts)]
    return _scatter_start(ps, after, "rs_scatter_start_" + tag)


def _reduce_scatter_finish(handle, cq, after, tag):
    ps, rs = _scatter_wait(handle, after, "rs_scatter_wait_" + tag)
    fs = [_chip_sum(p, r, cq, "rs_chip_sum") for p, r in zip(ps, rs)]
    return _join_start(fs, "rs_join_start_" + tag)


def _small_copies(src_refs, land_refs, x, y, c):
    return [(k - 1, src_refs[0], land_refs[0].at[k - 1], (x ^ (k >> 2), y ^ ((k >> 1) & 1), c ^ (k & 1)))
            for k in range(1, 8)]


def _all_reduce_small_start(v, after):
    return _ici_start([v], [lax.empty((7,) + v.shape, F32)], after, _small_copies, "small_start", per_src=7)


def _all_reduce_small_finish(handle, after):
    (v,), (t,) = _ici_wait(handle, after, _small_copies, "small_wait")
    n = v.shape[0]

    def body(v_ref, t_ref, out_ref, buf):
        x, y, c = _place()
        buf[4 * x + 2 * y + c] = v_ref[...]
        for k in range(1, 8):
            buf[4 * (x ^ (k >> 2)) + 2 * (y ^ ((k >> 1) & 1)) + (c ^ (k & 1))] = t_ref[k - 1]
        acc = buf[0]
        for d in range(1, 8):
            acc = acc + buf[d]
        out_ref[...] = acc

    return pl.pallas_call(
        body, name="small_sum", in_specs=[VMEM_SPEC, VMEM_SPEC], out_specs=VMEM_SPEC,
        out_shape=jax.ShapeDtypeStruct((n, 128), F32), scratch_shapes=[pltpu.VMEM((8, n, 128), F32)],
    )(v, t)


SMALL_GATHER = (16, 1024)
SMALL_FULL = sum(_size(_full_shape(n)) for n in SMALL)
SMALL_FULL_ROWS = -(-(SMALL_FULL + 1) // 128 // 8) * 8


def _layer_shards(w, i, q):
    kind, j = MIXER[i % 3], i // 3
    out = {n: w[n][i].astype(BF16) for n in COMMON_BIG}
    if kind == 'gla':
        win = jnp.zeros((GLA_WIN, D_MODEL), F32)
        win = lax.dynamic_update_slice(win, w['gla_w_in'][j].T, ((GLA_SHARD - GLA_WIN_STEP) * q, 0))
        out['gla_w_in'] = win.astype(BF16)
        out['gla_w_out'] = w['gla_w_out'][j].astype(BF16)
    elif kind == 'mla':
        out['mla_w_in'] = jnp.pad(w['mla_w_in'][j], ((0, 0), (0, MLA_IN_PAD - MLA_IN))).astype(BF16)
        for n in ('mla_w_uq', 'mla_w_ukv', 'mla_w_out'):
            out[n] = w[n][j].astype(BF16)
    else:
        out['conv_w_in'] = w['conv_w_in'][j].astype(BF16)
        out['conv_w_out'] = w['conv_w_out'][j].astype(BF16)
    return out


def _rows_joined(g):
    return g.reshape(g.shape[0] * g.shape[1], g.shape[2])


def _cols_joined(g):
    return jnp.moveaxis(g, 0, 1).reshape(g.shape[1], -1)


def _layer_weights(g, i):
    kind = MIXER[i % 3]
    W = {}
    if 'mlp_w1' in g:
        W = {'w1': g['mlp_w1'], 'w2': _rows_joined(g['mlp_w2']), 'gate': _rows_joined(g['ple_w_gate']),
             'proj': g['ple_w_proj']}
    if kind == 'gla' and 'gla_w_out' in g:
        W['w_out'] = _rows_joined(g['gla_w_out'])
    if kind == 'gla' and 'gla_w_in' in g:
        parts = []
        for qq in range(N_CHIPS):
            lo = g['gla_w_in'][qq][:128]
            if qq > 0:
                lo = lo + g['gla_w_in'][qq - 1][GLA_WIN_STEP:]
            parts += [lo, g['gla_w_in'][qq][128:GLA_WIN_STEP]]
        parts.append(g['gla_w_in'][N_CHIPS - 1][GLA_WIN_STEP:])
        W['w_in'] = jnp.concatenate(parts, axis=0)
    elif kind == 'mla':
        W['w_in'] = _rows_joined(g['mla_w_in'])
        uq = _cols_joined(g['mla_w_uq']).reshape(MLA_QR, MLA_HEADS, MLA_NOPE + MLA_ROPE)
        W['w_uq'] = jnp.pad(uq, ((0, 0), (0, 0), (0, MLA_QH - MLA_NOPE - MLA_ROPE))).reshape(MLA_QR, -1)
        ukv = _cols_joined(g['mla_w_ukv']).reshape(MLA_KVR, MLA_HEADS, 2, 128)
        W['w_ukv'] = ukv.transpose(0, 2, 1, 3).reshape(MLA_KVR, -1)
        W['w_out'] = _rows_joined(g['mla_w_out'])
    elif kind == 'conv':
        W['w_in'] = g['conv_w_in']
        W['w_out'] = _rows_joined(g['conv_w_out'])
    return W


def _pack_small_shards(w):
    flat = jnp.concatenate([w[n].reshape(-1) for n in SMALL_SHARDED])
    return jnp.pad(flat, (0, _size(SMALL_GATHER) - flat.shape[0])).reshape(SMALL_GATHER)


def _unpack_small_gathered(g):
    flat, out, off = g.reshape(N_CHIPS, -1), {}, 0
    for n in SMALL_SHARDED:
        shape, ax = WSPEC[n]
        seg = flat[:, off:off + _size(shape)].reshape((N_CHIPS,) + shape)
        out[n] = jnp.moveaxis(seg, 0, ax).reshape(_full_shape(n))
        off += _size(shape)
    return out


def _pack_small(vals, loss):
    flat = jnp.concatenate([vals[n].reshape(-1) for n in SMALL] + [loss.reshape(1)])
    return jnp.pad(flat, (0, SMALL_FULL_ROWS * 128 - flat.shape[0])).reshape(SMALL_FULL_ROWS, 128)


def _unpack_small(packed, q):
    flat = packed.reshape(-1)
    out, off = {}, 0
    for n in SMALL:
        shape, ax = WSPEC[n]
        full = flat[off:off + _size(_full_shape(n))].reshape(_full_shape(n))
        off += _size(_full_shape(n))
        out[n] = full if ax is None else lax.dynamic_slice_in_dim(full, q * shape[ax], shape[ax], axis=ax)
    return out


def _row_shards(dw):
    return dw.reshape(N_CHIPS, dw.shape[0] // N_CHIPS, dw.shape[1])


def _col_shards(dw):
    return jnp.moveaxis(dw.reshape(dw.shape[0], N_CHIPS, -1), 1, 0)


def _row(v):
    return v.reshape(1, -1)


def _layer_fwd(i, xin, xin_b, p_i, W, sm, cosp, sinp, rest=None, mid=None):
    kind, j = MIXER[i % 3], i // 3
    sv = {'xin': xin, 'xin_b': xin_b}
    if kind == 'gla':
        w_up = jnp.pad(sm['gla_w_gate_up'][j].astype(BF16), ((0, 128 - GLA_RANK), (0, 0)))
        pin = _mm(xin_b, W['w_in'], tb=True, name="gla_in", tn=640, tm=FULL_ROWS)
        la = _mm(pin, w_up, name="gla_gate", K=128, tk=128, a_off=(0, (GLA_IN_PAD - 128) // 128), tn=512,
                 extras=[(_row(sm['gla_b_gate'][j]), 'n')],
                 epilogue=lambda acc, b: (_log_sigmoid(acc + b) * (1.0 / GLA_TAU),))
        o, states = _gla_fwd(pin, la)
        yb = _gla_post_fwd(o, pin, _row(sm['gla_norm_g'][j]))
        if rest is not None:
            W = {**W, **rest(yb)}
        mixed = yb
        sv.update(w_up=w_up, pin=pin, la=la, o=o, states=states, yb=yb)
    elif kind == 'mla':
        gq, gkv = sm['mla_q_norm'][j:j + 1], sm['mla_kv_norm'][j:j + 1]
        cin = _mm(xin_b, W['w_in'], name="mla_in", tn=640, tm=FULL_ROWS)
        qn, kvn, kr = _mla_norm_fwd(cin, gq, gkv, cosp, sinp)
        qr = _mla_qrope_fwd(_mm(qn, W['w_uq'], name="mla_uq"), cosp, sinp)
        knv = _mm(kvn, W['w_ukv'], name="mla_ukv", out_dtypes=(BF16,))
        o = _mla_attn_fwd(qr, knv, kr)
        ob = o.astype(BF16)
        mixed = ob
        sv.update(gq=gq, gkv=gkv, cin=cin, qn=qn, kvn=kvn, kr=kr, qr=qr, knv=knv, o=o, ob=ob)
    else:
        w8 = jnp.pad(sm['conv_w'][j], ((0, 5), (0, 0)))
        bcu = _mm(xin_b, W['w_in'], name="conv_in", tn=768, b_sh=True, tm=FULL_ROWS)
        yb = _conv_fwd(bcu, w8)
        mixed = yb
        sv.update(w8=w8, bcu=bcu, yb=yb)
    g0, b0 = _row(sm['ln_g'][i, 0]), _row(sm['ln_b'][i, 0])
    g1, b1 = _row(sm['ln_g'][i, 1]), _row(sm['ln_b'][i, 1])
    ln = dict(tm=512, tn=D_MODEL, out_dtypes=(F32, BF16, F32), epilogue=_ln_fwd_epilogue)
    x1, x1b, v0 = _mm(mixed, W['w_out'], name="mix_out_ln", extras=[(xin, 'mn'), (g0, 'n'), (b0, 'n')], **ln)
    ab, dadu = _mm(x1b, W['w1'], name="mlp_up", out_dtypes=(BF16, BF16), b_sh=True, tm=FULL_ROWS,
                   epilogue=lambda acc: (jnp.square(jnp.maximum(acc, 0.0)), 2.0 * jnp.maximum(acc, 0.0)))
    x2, x2b, v1 = _mm(ab, W['w2'], name="mlp_down_ln", tk=D_FF, extras=[(x1, 'mn'), (g1, 'n'), (b1, 'n')], **ln)
    order = [(mid(x2b), 'whole')] if mid else []
    pp = _mm(p_i, W['proj'], name="ple_proj", tn=256, b_sh=True, extras=order,
             epilogue=lambda acc, *unused: (acc,))
    z, x3, x3b = _mm(x2b, W['gate'], name="ple_gate", out_dtypes=(F32, F32, BF16),
                     extras=[(x2, 'mn'), (pp, 'mn')],
                     epilogue=lambda acc, xv, pv: (acc,) + (xv + jax.nn.sigmoid(acc) * pv,) * 2)
    sv.update(v0=v0, x1b=x1b, ab=ab, dadu=dadu, v1=v1, x2b=x2b, pp=pp, z=z, g0=g0, g1=g1)
    return x3, x3b, sv, W


def _layer_bwd(i, grads_in, p_i, W, sm, sv, cosp, sinp, token, early=None, below=None):
    kind, j = MIXER[i % 3], i // 3
    big, small = {}, {}
    dx, dpp_b, dz_b = grads_in
    big['ple_w_proj'] = _mm(p_i, dpp_b, ta=True, name="ple_proj_dw", tn=256, out_sh=True, out_dtypes=(BF16,))
    big['ple_w_gate'] = _row_shards(_mm(sv['x2b'], dz_b, ta=True, name="dw_dd", out_dtypes=(BF16,)))
    ln = dict(tb=True, tm=512, tn=D_MODEL, out_dtypes=(F32, BF16), n_sums=2)
    (dv1, dv1b), (dg1, db1) = _mm(dz_b, W['gate'], name="ple_gate_dx_ln", epilogue=_ln_bwd_epilogue(1.0),
                                  extras=[(dx, 'mn'), (sv['v1'], 'mn'), (sv['g1'], 'n'), (token, 'whole')], **ln)
    big['mlp_w2'] = _row_shards(_mm(sv['ab'], dv1b, ta=True, name="mlp_down_dw", out_dtypes=(BF16,)))
    dub = _mm(dv1b, W['w2'], tb=True, name="mlp_down_dx", out_dtypes=(BF16,), tm=FULL_ROWS,
              extras=[(sv['dadu'], 'mn')], epilogue=lambda acc, d: (acc * d.astype(F32),))
    big['mlp_w1'] = _mm(sv['x1b'], dub, ta=True, name="mlp_up_dw", out_sh=True, out_dtypes=(BF16,))
    order = []
    if early is not None:
        order, big = [(early(big), 'whole')], {}
    (dv0, dv0b), (dg0, db0) = _mm(dub, W['w1'], name="mlp_up_dx_ln", b_sh=True, tk=D_FF, epilogue=_ln_bwd_epilogue(ALPHA),
                                  extras=[(dv1, 'mn'), (sv['v0'], 'mn'), (sv['g0'], 'n')] + order, **ln)
    small['ln_g'] = jnp.stack([dg0[0], dg1[0]])
    small['ln_b'] = jnp.stack([db0[0], db1[0]])
    resid = dict(tb=True, tn=D_MODEL, tm=512 if below else 1024, epilogue=_input_grad_epilogue,
                 extras=[(dv0, 'mn')] + [(a, 'mn') for a in below or ()],
                 out_dtypes=(F32, BF16, BF16) if below else (F32,))
    if kind == 'gla':
        big['gla_w_out'] = _row_shards(_mm(sv['yb'], dv0b, ta=True, name="dw_dd", out_dtypes=(BF16,)))
        dy = _mm(dv0b, W['w_out'], tb=True, name="dx_dd", tn=1024)
        do, dr_b, dng = _gla_post_bwd(dy, sv['o'], sv['pin'], _row(sm['gla_norm_g'][j]))
        dq_b, dk_b, dvv_b, dla = _gla_bwd(sv['pin'], sv['la'], sv['states'], do)
        dzg_b, dbg = _gla_gate_bwd(dla, sv['la'])
        dw_up = _mm(sv['pin'], dzg_b, ta=True, name="gla_gate_dw", M=128, tm=128,
                    a_off=(0, (GLA_IN_PAD - 128) // 128))
        dglr_b = _mm(dzg_b, sv['w_up'], tb=True, name="gla_gate_dx", out_dtypes=(BF16,))
        dpin_b = jnp.concatenate([dq_b, dk_b, dvv_b, dr_b, dglr_b], axis=1)
        dw_in = _mm(sv['xin_b'], dpin_b, ta=True, name="gla_in_dw", tn=640, out_dtypes=(BF16,))
        dxin = _mm(dpin_b, W['w_in'], name="gla_in_dx", tk=GLA_IN_PAD, **{**resid, 'tb': False})
        big['gla_w_in'] = jnp.stack([dw_in[:, GLA_WIN_STEP * qq:GLA_WIN_STEP * qq + GLA_WIN]
                                     for qq in range(N_CHIPS)])
        small.update(gla_w_gate_up=dw_up[:GLA_RANK], gla_b_gate=dbg[0], gla_norm_g=dng[0])
    elif kind == 'mla':
        big['mla_w_out'] = _row_shards(_mm(sv['ob'], dv0b, ta=True, name="dw_dd", out_dtypes=(BF16,)))
        do = _mm(dv0b, W['w_out'], tb=True, name="dx_dd", tn=1024)
        dqr, dkn_b, dvv_b, dkr = _mla_attn_bwd(sv['qr'], sv['knv'], sv['kr'], sv['o'], do)
        dq_b = _mla_qrope_bwd(dqr, cosp, sinp)
        dw_uq = _mm(sv['qn'], dq_b, ta=True, name="mla_up_dw", out_dtypes=(BF16,))
        dqn = _mm(dq_b, W['w_uq'], tb=True, name="mla_up_dx")
        dknv_b = jnp.concatenate([dkn_b, dvv_b], axis=1)
        dw_ukv = _mm(sv['kvn'], dknv_b, ta=True, name="mla_up_dw", out_dtypes=(BF16,))
        dkvn = _mm(dknv_b, W['w_ukv'], tb=True, name="mla_up_dx")
        dcin_b, dgq, dgkv = _mla_norm_bwd(sv['cin'], dqn, dkvn, dkr, sv['gq'], sv['gkv'], cosp, sinp)
        big['mla_w_in'] = _row_shards(_mm(sv['xin_b'], dcin_b, ta=True, name="mla_in_dw", tn=640,
                                          out_dtypes=(BF16,)))
        dxin = _mm(dcin_b, W['w_in'], name="mla_in_dx", tk=MLA_IN_PAD, **resid)
        big['mla_w_uq'] = _col_shards(
            dw_uq.reshape(MLA_QR, MLA_HEADS, MLA_QH)[:, :, :MLA_NOPE + MLA_ROPE].reshape(MLA_QR, -1))
        big['mla_w_ukv'] = _col_shards(
            dw_ukv.reshape(MLA_KVR, 2, MLA_HEADS, 128).transpose(0, 2, 1, 3).reshape(MLA_KVR, -1))
        small.update(mla_q_norm=dgq[0], mla_kv_norm=dgkv[0])
    else:
        big['conv_w_out'] = _row_shards(_mm(sv['yb'], dv0b, ta=True, name="dw_dd", out_dtypes=(BF16,)))
        dy = _mm(dv0b, W['w_out'], tb=True, name="dx_dd", tn=1024)
        db_b, dc_b, du_b, dw8 = _conv_bwd(sv['bcu'], sv['w8'], dy)
        dbcu_b = jnp.concatenate([db_b, dc_b, du_b], axis=1)
        big['conv_w_in'] = _mm(sv['xin_b'], dbcu_b, ta=True, name="conv_in_dw", tn=768, out_sh=True,
                               out_dtypes=(BF16,))
        dxin = _mm(dbcu_b, W['w_in'], name="conv_in_dx", tk=3 * D_MODEL, b_sh=True, **resid)
        small['conv_w'] = dw8[:3]
    return (dxin if below else (dxin,)), big, small


def _rope_tables(positions):
    inv_freq = ROPE_BASE ** (-jnp.arange(0, MLA_ROPE // 2, dtype=F32) * (2.0 / MLA_ROPE))
    ang = positions.astype(F32)[:, None] * inv_freq
    zeros = jnp.zeros((positions.shape[0], 64), F32)
    return (jnp.concatenate([jnp.cos(ang), jnp.cos(ang), zeros], axis=1),
            jnp.concatenate([jnp.sin(ang), jnp.sin(ang), zeros], axis=1))


FIRST_NEEDED = ['gla_w_in']


def _plain_shards(w, i):
    kind, j = MIXER[i % 3], i // 3
    mixer = {'gla': ['gla_w_out'], 'mla': ['mla_w_uq', 'mla_w_ukv', 'mla_w_out'], 'conv': ['conv_w_in', 'conv_w_out']}
    return {**{n: (w[n], i) for n in COMMON_BIG}, **{n: (w[n], j) for n in mixer[kind]}}


def _start_gathers(w, q, cq):
    token, started = jnp.zeros(TOKEN, F32), []
    for i in range(DEPTH):
        sh, plain = _layer_shards(w, i, q), _plain_shards(w, i)
        for k, names in enumerate([list(sh)] if i > 0 else [FIRST_NEEDED, [n for n in sh if n not in FIRST_NEEDED]]):
            ops = [plain.get(n, sh[n]) for n in names]
            if i == 0 and k == 0:
                ops.append(_pack_small_shards(w))
            tag = "l%d%s" % (i, "ab"[k] if i == 0 else "")
            handle = _gather_start(_place_own(ops, token, cq, "ag_own_" + tag), token, "ag_start_" + tag)
            token = handle[4]
            started.append((handle, names, tag))
    return started, token


def _pass_on(entry, after):
    handle, names, tag = entry
    _, lands = _gather_wait(handle, after, "ag_wait_" + tag)
    passing = _forward_start(lands, jnp.zeros(TOKEN, F32), "ag_pass_start_" + tag)
    return (passing, names, tag), passing[4]


def _gathered(passed, after):
    passing, names, tag = passed
    got = _forward_wait(passing, after, "ag_pass_wait_" + tag)
    return dict(zip(names, got)), got[-1]


def _local_shard_grad(name, g, q):
    if name == 'gla_w_in':
        return lax.dynamic_slice_in_dim(g, (GLA_SHARD - GLA_WIN_STEP) * q, GLA_SHARD, axis=1)
    if name == 'mla_w_in':
        return g[:, :MLA_IN]
    return g


def kernel(x, p, positions, gla_w_in, gla_w_gate_up, gla_b_gate, gla_norm_g, gla_w_out, mla_w_in, mla_q_norm, mla_kv_norm, mla_w_uq, mla_w_ukv, mla_w_out, conv_w_in, conv_w, conv_w_out, ln_g, ln_b, mlp_w1, mlp_w2, ple_w_gate, ple_w_proj, loss_target, m_gla_w_in, m_gla_w_gate_up, m_gla_b_gate, m_gla_norm_g, m_gla_w_out, m_mla_w_in, m_mla_q_norm, m_mla_kv_norm, m_mla_w_uq, m_mla_w_ukv, m_mla_w_out, m_conv_w_in, m_conv_w, m_conv_w_out, m_ln_g, m_ln_b, m_mlp_w1, m_mlp_w2, m_ple_w_gate, m_ple_w_proj, v_gla_w_in, v_gla_w_gate_up, v_gla_b_gate, v_gla_norm_g, v_gla_w_out, v_mla_w_in, v_mla_q_norm, v_mla_kv_norm, v_mla_w_uq, v_mla_w_ukv, v_mla_w_out, v_conv_w_in, v_conv_w, v_conv_w_out, v_ln_g, v_ln_b, v_mlp_w1, v_mlp_w2, v_ple_w_gate, v_ple_w_proj):
    args = locals()
    w = {n: args[n] for n in WNAMES}
    m = {n: args['m_' + n] for n in WNAMES}
    v = {n: args['v_' + n] for n in WNAMES}
    q = 2 * lax.axis_index("x") + lax.axis_index("y")
    cq = jnp.stack([lax.axis_index("c"), q]).astype(jnp.int32)

    cosp, sinp = _rope_tables(positions[0])
    started, after = _start_gathers(w, q, cq)
    xin, saved, layers, sm = x[0], [], [], None
    xin_b = xin.astype(BF16)
    passed, after = _pass_on(started[0], after)
    for i in range(DEPTH):
        got, last = _gathered(passed, after)
        rest = mid = None
        if i == 0:
            sm = _unpack_small_gathered(last)
            sm['mla_q_norm'], sm['mla_kv_norm'] = w['mla_q_norm'], w['mla_kv_norm']
            rest = lambda after: _layer_weights(_gathered(*_pass_on(started[1], after))[0], 0)
        coming = {}
        if i + 1 < DEPTH:
            def mid(after, entry=started[i + 2], coming=coming):
                coming['passed'], token = _pass_on(entry, after)
                return token
        xin, xin_b, sv, W = _layer_fwd(i, xin, xin_b, p[i, 0], _layer_weights(got, i), sm, cosp, sinp, rest, mid)
        layers.append(W)
        saved.append(sv)
        passed, after = coming.get('passed'), xin
    *grads_in, loss_cols = _loss_head(xin, loss_target[0], saved[-1]['z'], saved[-1]['pp'])
    loss = jnp.sum(loss_cols[0])

    gbig = {n: [None] * WSPEC[n][0][0] for n in BIG}
    gsmall = {n: [None] * _full_shape(n)[0] for n in SMALL}
    pending = []

    def start(grads, i, tag):
        names = list(grads)
        gs = [grads[n] for n in names]
        handle = _reduce_direct_start(gs, tag) if i > 0 else _reduce_scatter_start(gs, cq, jnp.zeros(TOKEN, F32), tag)
        pending.append((handle, names, i, tag))
        return handle[4]

    joining = []

    def finish(above, after, token):
        for entry in [e for e in pending if e[2] > above]:
            pending.remove(entry)
            handle, names, i, tag = entry
            handle = (_reduce_direct_finish if i > 0 else _reduce_scatter_finish)(handle, cq, after, tag)
            joining.append((handle, names, i, tag))
            token = token + handle[4]
        return token

    token = jnp.zeros(TOKEN, F32)
    for i in reversed(range(DEPTH)):
        early = (lambda grads: start(grads, 0, "l0a")) if i == 0 else None
        below = (saved[i - 1]['z'], saved[i - 1]['pp']) if i > 0 else None
        grads_in, big, small = _layer_bwd(i, grads_in, p[i, 0], layers[i], sm, saved[i], cosp, sinp, token, early,
                                          below)
        dx = grads_in[0]
        token = finish(i + 1, dx, start(big, i, "l%d%s" % (i, "b" if i == 0 else "")))
        for n, g in small.items():
            gsmall[n][i if n in ('ln_g', 'ln_b') else i // 3] = g
    grad, delta, new_m, new_v = {}, {}, {}, {}

    def take(entries, after=None):
        last = entries[-1][0][4] if after is None else entries[-1][0][4] + after
        for handle, names, i, tag in entries:
            for n, g in zip(names, _join_wait(handle, last, "rs_join_wait_" + tag)):
                gbig[n][i if n in COMMON_BIG else i // 3] = _local_shard_grad(n, g, q)

    def update(n):
        adamw = _adamw_shard_major if n == 'gla_w_in' else _adamw
        grad[n], delta[n], new_m[n], new_v[n] = adamw(w[n], m[n], v[n], gbig[n], "adamw_" + n)

    token = finish(0, token, token)
    take(joining)
    del joining[:]
    ready = [n for n in BIG if n.startswith(('mla_', 'conv_'))]
    for n in ready:
        update(n)
    token = finish(-1, delta[ready[-1]], token)
    summing = _all_reduce_small_start(_pack_small({n: jnp.stack(g) for n, g in gsmall.items()}, loss), token)
    take(joining, summing[4])
    others = [n for n in BIG if n not in ready]
    for n in others:
        update(n)
    small_sum = _all_reduce_small_finish(summing, delta[others[-1]])
    gsm, loss = _unpack_small(small_sum, q), small_sum.reshape(-1)[SMALL_FULL]
    flat2 = lambda a: a.reshape(-1, a.shape[-1])
    res = _adamw_small(*[[flat2(d[n]) for n in SMALL] for d in (w, gsm, m, v)])
    for k, out in enumerate((grad, delta, new_m, new_v)):
        for n, r in zip(SMALL, res[k::4]):
            out[n] = r.reshape(WSPEC[n][0])
    return (loss, dx[None], *[grad[n] for n in WNAMES], *[delta[n] for n in WNAMES],
            *[new_m[n] for n in WNAMES], *[new_v[n] for n in WNAMES])
```
